```python
import jax, jax.numpy as jnp
from jax import lax
import numpy as np

D_MODEL = 1024
BATCH = 4
SEQ = 8192
DEPTH = 2

N_MEM = 256
N_MIXERS = 2
EPS = 1e-6
CONV_KERNEL = 31
GLA_HEADS = 4
GLA_DK = D_MODEL // 2
GLA_DV = D_MODEL
GLA_DKH = GLA_DK // GLA_HEADS
GLA_DVH = GLA_DV // GLA_HEADS
GLA_RANK = 16
GLA_TAU = 16.0
GLA_CHUNK = 64
XATTN_HEADS = 4
XATTN_HD = D_MODEL // XATTN_HEADS
N_GROUPS = 4
EXPERTS_PER_GROUP = 8
N_EXPERTS = N_GROUPS * EXPERTS_PER_GROUP
TOP_K = 2
D_EXPERT = D_MODEL // 2
MOE_BLOCK = 128

kernel_name = 'hybrid_conv_gla_memxattn_hmoe'


def rms_norm(x, g):
    xf = x.astype(jnp.float32)
    y = xf * lax.rsqrt(jnp.mean(xf * xf, axis=-1, keepdims=True) + EPS)
    return (y * g.astype(jnp.float32)).astype(x.dtype)


def conformer_conv(h, w_in, b_in, w_dw, b_dw, ln_g, ln_b, w_out, b_out):
    u = h @ w_in + b_in
    a, gt = jnp.split(u, 2, axis=-1)
    u = a * jax.nn.sigmoid(gt)
    u = lax.conv_general_dilated(
        u, w_dw[:, None, :], window_strides=(1,), padding=[(CONV_KERNEL - 1, 0)],
        dimension_numbers=('NWC', 'WIO', 'NWC'), feature_group_count=D_MODEL) + b_dw
    uf = u.astype(jnp.float32)
    mu = jnp.mean(uf, axis=-1, keepdims=True)
    var = jnp.mean(jnp.square(uf - mu), axis=-1, keepdims=True)
    un = ((uf - mu) * lax.rsqrt(var + EPS) * ln_g.astype(jnp.float32) + ln_b.astype(jnp.float32)).astype(h.dtype)
    return jax.nn.silu(un) @ w_out + b_out


def gla_mixer(h, w_in, w_a2, b_a, norm_g, w_o):
    B, S, _ = h.shape
    n_chunks = S // GLA_CHUNK
    proj = h @ w_in
    q, k, v, a_lr, r = jnp.split(
        proj, [GLA_DK, 2 * GLA_DK, 2 * GLA_DK + GLA_DV, 2 * GLA_DK + GLA_DV + GLA_RANK], axis=-1)
    log_a = jax.nn.log_sigmoid((a_lr @ w_a2 + b_a).astype(jnp.float32)) / GLA_TAU

    def chunks(t, d):
        return t.astype(jnp.float32).reshape(B, n_chunks, GLA_CHUNK, GLA_HEADS, d).transpose(1, 0, 3, 2, 4)

    qc = chunks(q, GLA_DKH) * (GLA_DKH ** -0.5)
    kc = chunks(k, GLA_DKH)
    vc = chunks(v, GLA_DVH)
    bc = jnp.cumsum(chunks(log_a, GLA_DKH), axis=3)
    causal = jnp.tril(jnp.ones((GLA_CHUNK, GLA_CHUNK), dtype=bool))[:, :, None]

    def step(state, inp):
        qi, ki, vi, bi = inp
        diff = bi[:, :, :, None, :] - bi[:, :, None, :, :]
        decay = jnp.exp(jnp.where(causal, diff, -jnp.inf))
        att = jnp.sum(qi[:, :, :, None, :] * ki[:, :, None, :, :] * decay, axis=-1)
        o = jnp.einsum('bhij,bhjv->bhiv', att, vi) + jnp.einsum('bhik,bhkv->bhiv', qi * jnp.exp(bi), state)
        b_last = bi[:, :, -1:, :]
        k_dec = ki * jnp.exp(b_last - bi)
        new_state = jnp.exp(b_last)[:, :, 0, :, None] * state + jnp.einsum('bhjk,bhjv->bhkv', k_dec, vi)
        return new_state, o

    state0 = jnp.zeros((B, GLA_HEADS, GLA_DKH, GLA_DVH), jnp.float32)
    _, o = lax.scan(step, state0, (qc, kc, vc, bc))
    o = o.transpose(1, 0, 3, 2, 4).reshape(B, S, GLA_HEADS, GLA_DVH)
    o = o * lax.rsqrt(jnp.mean(o * o, axis=-1, keepdims=True) + EPS) * norm_g.astype(jnp.float32)
    o = o.reshape(B, S, GLA_DV).astype(h.dtype) * jax.nn.silu(r)
    return o @ w_o


def mem_cross_attn(h, m, w_q, w_kv, w_o):
    B, S, _ = h.shape
    q = (h @ w_q).reshape(B, S, XATTN_HEADS, XATTN_HD)
    k, v = jnp.split(m @ w_kv, 2, axis=-1)
    k = k.reshape(B, N_MEM, XATTN_HEADS, XATTN_HD)
    v = v.reshape(B, N_MEM, XATTN_HEADS, XATTN_HD)
    s = jnp.einsum('bshd,bmhd->bhsm', q, k).astype(jnp.float32) * (XATTN_HD ** -0.5)
    p = jax.nn.softmax(s, axis=-1).astype(v.dtype)
    o = jnp.einsum('bhsm,bmhd->bshd', p, v).reshape(B, S, D_MODEL)
    return o @ w_o


def hier_moe(h, w_grp, b_grp, w_exp, b_exp, w_gate, w_up, w_down):
    B, S, D = h.shape
    T = B * S
    hf = h.reshape(T, D)
    gl = (hf @ w_grp + b_grp).astype(jnp.float32)
    pg = jax.nn.softmax(gl, axis=-1)
    g_sel = jnp.argmax(gl, axis=-1)
    pg_sel = jnp.take_along_axis(pg, g_sel[:, None], axis=1)
    el = (hf @ w_exp + b_exp).astype(jnp.float32).reshape(T, N_GROUPS, EXPERTS_PER_GROUP)
    el_sel = jnp.take_along_axis(el, g_sel[:, None, None], axis=1)[:, 0]
    pe = jax.nn.softmax(el_sel, axis=-1)
    top_p, top_i = lax.top_k(pe, TOP_K)
    gate = pg_sel * top_p / jnp.sum(top_p, axis=-1, keepdims=True)
    expert_id = g_sel[:, None].astype(jnp.int32) * EXPERTS_PER_GROUP + top_i.astype(jnp.int32)

    A = T * TOP_K
    e_flat = expert_id.reshape(A)
    g_flat = gate.reshape(A)
    order = jnp.argsort(e_flat)
    sorted_e = e_flat[order]
    token_idx = order // TOP_K
    counts = jnp.bincount(e_flat, length=N_EXPERTS)
    offsets = jnp.cumsum(counts) - counts
    padded = (counts + MOE_BLOCK - 1) // MOE_BLOCK * MOE_BLOCK
    pad_ends = jnp.cumsum(padded)
    pad_offsets = pad_ends - padded
    dest = pad_offsets[sorted_e] + jnp.arange(A) - offsets[sorted_e]
    n_blocks = -(-A // MOE_BLOCK) + N_EXPERTS
    x_buf = jnp.zeros((n_blocks * MOE_BLOCK, D), h.dtype).at[dest].set(hf[token_idx])
    block_expert = jnp.minimum(
        jnp.searchsorted(pad_ends, jnp.arange(n_blocks) * MOE_BLOCK, side='right'), N_EXPERTS - 1)

    def expert_block(args):
        xb, e = args
        return (jax.nn.silu(xb @ w_gate[e]) * (xb @ w_up[e])) @ w_down[e]

    y_buf = lax.map(expert_block, (x_buf.reshape(n_blocks, MOE_BLOCK, D), block_expert)).reshape(-1, D)
    y = y_buf[dest] * g_flat[order][:, None].astype(h.dtype)
    out = jnp.zeros((T, D), h.dtype).at[token_idx].add(y)
    return out.reshape(B, S, D)


def setup_inputs(seed: int = 0) -> dict:
    key = jax.random.key(seed)
    ks = iter(jax.random.split(key, 40))
    n_conv = (DEPTH + 1) // 2
    n_gla = DEPTH // 2
    D = D_MODEL

    def nrm(shape, scale):
        return jax.random.normal(next(ks), shape, jnp.float32) * scale

    def gain(shape):
        return 1.0 + 0.02 * jax.random.normal(next(ks), shape, jnp.float32)

    gla_in_cols = 2 * GLA_DK + GLA_DV + GLA_RANK + GLA_DV
    return {
        'x': nrm((BATCH, SEQ, D), 1.0),
        'mem': nrm((BATCH, N_MEM, D), 1.0),
        'norm_mix': gain((DEPTH, D)),
        'norm_xattn': gain((DEPTH, D)),
        'norm_ffn': gain((DEPTH, D)),
        'norm_mem': gain((D,)),
        'norm_final': gain((D,)),
        'conv_w_in': nrm((n_conv, D, 2 * D), D ** -0.5),
        'conv_b_in': nrm((n_conv, 2 * D), 0.02),
        'conv_w_dw': nrm((n_conv, CONV_KERNEL, D), CONV_KERNEL ** -0.5),
        'conv_b_dw': nrm((n_conv, D), 0.02),
        'conv_ln_g': gain((n_conv, D)),
        'conv_ln_b': nrm((n_conv, D), 0.02),
        'conv_w_out': nrm((n_conv, D, D), D ** -0.5),
        'conv_b_out': nrm((n_conv, D), 0.02),
        'gla_w_in': nrm((n_gla, D, gla_in_cols), D ** -0.5),
        'gla_w_a2': nrm((n_gla, GLA_RANK, GLA_DK), GLA_RANK ** -0.5),
        'gla_b_a': nrm((n_gla, GLA_DK), 0.1),
        'gla_norm_g': gain((n_gla, GLA_DVH)),
        'gla_w_o': nrm((n_gla, GLA_DV, D), GLA_DV ** -0.5),
        'xa_w_q': nrm((DEPTH, D, D), D ** -0.5),
        'xa_w_kv': nrm((DEPTH, D, 2 * D), D ** -0.5),
        'xa_w_o': nrm((DEPTH, D, D), D ** -0.5),
        'moe_w_grp': nrm((DEPTH, D, N_GROUPS), D ** -0.5),
        'moe_b_grp': nrm((DEPTH, N_GROUPS), 0.01),
        'moe_w_exp': nrm((DEPTH, D, N_EXPERTS), D ** -0.5),
        'moe_b_exp': nrm((DEPTH, N_EXPERTS), 0.01),
        'moe_w_gate': nrm((DEPTH, N_EXPERTS, D, D_EXPERT), D ** -0.5),
        'moe_w_up': nrm((DEPTH, N_EXPERTS, D, D_EXPERT), D ** -0.5),
        'moe_w_down': nrm((DEPTH, N_EXPERTS, D_EXPERT, D), D_EXPERT ** -0.5),
    }


def reference(x, mem, norm_mix, norm_xattn, norm_ffn, norm_mem, norm_final,
              conv_w_in, conv_b_in, conv_w_dw, conv_b_dw, conv_ln_g, conv_ln_b, conv_w_out, conv_b_out,
              gla_w_in, gla_w_a2, gla_b_a, gla_norm_g, gla_w_o,
              xa_w_q, xa_w_kv, xa_w_o,
              moe_w_grp, moe_b_grp, moe_w_exp, moe_b_exp, moe_w_gate, moe_w_up, moe_w_down):
    mem_n = rms_norm(mem, norm_mem)
    for i in range(DEPTH):
        h = rms_norm(x, norm_mix[i])
        j = i // N_MIXERS
        if i % N_MIXERS == 0:
            x = x + conformer_conv(h, conv_w_in[j], conv_b_in[j], conv_w_dw[j], conv_b_dw[j],
                                   conv_ln_g[j], conv_ln_b[j], conv_w_out[j], conv_b_out[j])
        else:
            x = x + gla_mixer(h, gla_w_in[j], gla_w_a2[j], gla_b_a[j], gla_norm_g[j], gla_w_o[j])
        x = x + mem_cross_attn(rms_norm(x, norm_xattn[i]), mem_n, xa_w_q[i], xa_w_kv[i], xa_w_o[i])
        x = x + hier_moe(rms_norm(x, norm_ffn[i]), moe_w_grp[i], moe_b_grp[i], moe_w_exp[i], moe_b_exp[i],
                         moe_w_gate[i], moe_w_up[i], moe_w_down[i])
    return rms_norm(x, norm_final)
```

```python
import functools

import jax
import jax.numpy as jnp
from jax import lax
from jax.experimental import pallas as pl
from jax.experimental.pallas import tpu as pltpu

F32 = jnp.float32
BF16 = jnp.bfloat16
I32 = jnp.int32
U32 = jnp.uint32

EPS = 1e-6
CONV_KERNEL = 31
CONV_CARRY = 32
CONV_ROWS = 16
SUBLANES = 8
GLA_HEADS = 4
GLA_RANK = 16
GLA_RANK_PAD = 128
GLA_TAU = 16.0
GLA_LEAF = 32
XATTN_HEADS = 4
N_GROUPS = 4
EXPERTS_PER_GROUP = 8
N_EXPERTS = N_GROUPS * EXPERTS_PER_GROUP
ROUTER_ROWS = 40
TOP_K = 2

TILE_CONV = 256
TILE_GLA = 256
TILE_XATTN = 512
TILE_DISPATCH = 512
TILE_COMBINE = 256
MOE_BLOCK_ROWS = 256
VMEM_LIMIT = 56 * 1024 * 1024


def _cparams(sem):
    return pltpu.CompilerParams(dimension_semantics=sem, vmem_limit_bytes=VMEM_LIMIT)


def _rms(x, g):
    return x * lax.rsqrt(jnp.mean(x * x, axis=-1, keepdims=True) + EPS) * g


def _sigmoid(x):
    return 1.0 / (1.0 + jnp.exp(-x))


def _split_bf16(x):
    hi = x.astype(BF16)
    lo = (x - hi.astype(F32)).astype(BF16)
    return hi, lo


def _dot(a, b):
    return jnp.dot(a, b, preferred_element_type=F32)


def _dot_nt(a, b):
    return lax.dot_general(a, b, (((1,), (1,)), ((), ())), preferred_element_type=F32)


def _dot_tn(a, b):
    return lax.dot_general(a, b, (((0,), (0,)), ((), ())), preferred_element_type=F32)


def _pack_bf16_pairs(x):
    w = x.shape[1] // 2
    hi = lax.bitcast_convert_type(x[:, :w].astype(BF16).astype(F32), U32)
    lo = lax.bitcast_convert_type(x[:, w:].astype(BF16).astype(F32), U32)
    return hi | (lo >> 16)


def _unpack_bf16_pairs(p):
    hi = lax.bitcast_convert_type(p & jnp.uint32(0xFFFF0000), F32)
    lo = lax.bitcast_convert_type(p << 16, F32)
    return jnp.concatenate([hi, lo], axis=1)


def _memkv_kernel(mem_ref, g_ref, w_ref, k_ref, v_ref):
    d = mem_ref.shape[-1]
    mn = _rms(mem_ref[0], g_ref[...]).astype(BF16)
    kv = _dot(mn, w_ref[0])
    k_ref[0, 0] = kv[:, :d].astype(BF16)
    v_ref[0, 0] = kv[:, d:].astype(BF16)


def _mem_kv(mem, norm_mem, w_kv):
    b, nm, d = mem.shape
    depth = w_kv.shape[0]
    out = jax.ShapeDtypeStruct((depth, b, nm, d), BF16)
    return pl.pallas_call(
        _memkv_kernel,
        out_shape=(out, out),
        grid=(depth, b),
        in_specs=[
            pl.BlockSpec((1, nm, d), lambda l, i: (i, 0, 0)),
            pl.BlockSpec((1, d), lambda l, i: (0, 0)),
            pl.BlockSpec((1, d, 2 * d), lambda l, i: (l, 0, 0)),
        ],
        out_specs=(
            pl.BlockSpec((1, 1, nm, d), lambda l, i: (l, i, 0, 0)),
            pl.BlockSpec((1, 1, nm, d), lambda l, i: (l, i, 0, 0)),
        ),
        compiler_params=_cparams(("arbitrary", "arbitrary")),
        name="mem_kv",
    )(mem, norm_mem.reshape(1, d), w_kv.astype(BF16))


def _conv_kernel(x_ref, g_ref, win_ref, bin_ref, wdw_ref, bdw_ref, lng_ref, lnb_ref, wout_ref, bout_ref,
                 o_ref, ext_ref, conv_ref):
    ts, d = x_ref.shape[1], x_ref.shape[2]

    @pl.when(pl.program_id(1) == 0)
    def _():
        ext_ref[...] = jnp.zeros(ext_ref.shape, F32)

    x = x_ref[0]
    h = _rms(x, g_ref[...]).astype(BF16)
    u = _dot(h, win_ref[...]) + bin_ref[...]
    glu = u[:, :d] * _sigmoid(u[:, d:])
    for b in range(SUBLANES):
        ext_ref[b, CONV_CARRY - b:CONV_CARRY - b + ts, :] = glu

    first = CONV_CARRY - (CONV_KERNEL - 1)

    def chunk(i, carry):
        r0 = pl.multiple_of(i * CONV_ROWS, CONV_ROWS)
        acc = jnp.zeros((CONV_ROWS, d), F32)
        for k in range(CONV_KERNEL):
            b = (first + k) % SUBLANES
            acc = acc + ext_ref[b, pl.ds(r0 + (first + k - b), CONV_ROWS), :] * wdw_ref[k:k + 1, :]
        conv_ref[pl.ds(r0, CONV_ROWS), :] = acc
        return carry

    lax.fori_loop(0, ts // CONV_ROWS, chunk, 0)
    for b in range(SUBLANES):
        ext_ref[b, 0:CONV_CARRY, :] = ext_ref[b, ts:ts + CONV_CARRY, :]

    c = conv_ref[...] + bdw_ref[...]
    mu = jnp.mean(c, axis=-1, keepdims=True)
    cc = c - mu
    var = jnp.mean(cc * cc, axis=-1, keepdims=True)
    un = cc * lax.rsqrt(var + EPS) * lng_ref[...] + lnb_ref[...]
    act = (un * _sigmoid(un)).astype(BF16)
    o_ref[0] = x + _dot(act, wout_ref[...]) + bout_ref[...]


def _conv_mixer(x, g, w_in, b_in, w_dw, b_dw, ln_g, ln_b, w_out, b_out):
    b, s, d = x.shape
    ts = min(TILE_CONV, s)
    row = lambda v: v.reshape(1, -1)
    const = lambda shape: pl.BlockSpec(shape, lambda i, j: (0,) * len(shape))
    return pl.pallas_call(
        _conv_kernel,
        out_shape=jax.ShapeDtypeStruct(x.shape, F32),
        grid=(b, s // ts),
        in_specs=[
            pl.BlockSpec((1, ts, d), lambda i, j: (i, j, 0)),
            const((1, d)), const((d, 2 * d)), const((1, 2 * d)), const((CONV_KERNEL, d)), const((1, d)),
            const((1, d)), const((1, d)), const((d, d)), const((1, d)),
        ],
        out_specs=pl.BlockSpec((1, ts, d), lambda i, j: (i, j, 0)),
        scratch_shapes=[pltpu.VMEM((SUBLANES, CONV_CARRY + ts, d), F32), pltpu.VMEM((ts, d), F32)],
        compiler_params=_cparams(("arbitrary", "arbitrary")),
        name="conv_mixer",
    )(x, row(g), w_in.astype(BF16), row(b_in), w_dw, row(b_dw), row(ln_g), row(ln_b), w_out.astype(BF16),
      row(b_out))


def _gla_levels(ts):
    sizes = [GLA_LEAF]
    while sizes[-1] < ts:
        sizes.append(sizes[-1] * 2)
    return sizes


def _gla_kernel(x_ref, g_ref, wq_ref, wk_ref, wv_ref, wa_ref, wr_ref, wa2_ref, ba_ref, ng_ref, wo_ref,
                o_ref, state_ref):
    ts, d = x_ref.shape[1], x_ref.shape[2]
    dk = wq_ref.shape[1]
    dkh = dk // GLA_HEADS
    dvh = d // GLA_HEADS

    @pl.when(pl.program_id(1) == 0)
    def _():
        state_ref[...] = jnp.zeros(state_ref.shape, F32)

    x = x_ref[0]
    h = _rms(x, g_ref[...]).astype(BF16)
    q = _dot(h, wq_ref[...]) * (dkh ** -0.5)
    k = _dot(h, wk_ref[...])
    v = _dot(h, wv_ref[...]).astype(BF16)
    r = _dot(h, wr_ref[...])
    a_hi, a_lo = _split_bf16(_dot(h, wa_ref[...]))
    w2_hi, w2_lo = wa2_ref[0], wa2_ref[1]
    z = _dot(a_hi, w2_hi) + _dot(a_lo, w2_hi) + _dot(a_hi, w2_lo) + ba_ref[...]
    log_a = -(jnp.maximum(-z, 0.0) + jnp.log(1.0 + jnp.exp(-jnp.abs(z)))) * (1.0 / GLA_TAU)

    row = lax.broadcasted_iota(I32, (ts, ts), 0)
    col = lax.broadcasted_iota(I32, (ts, ts), 1)
    tri = jnp.where(col <= row, 1.0, 0.0).astype(BF16)
    la_hi, la_lo = _split_bf16(log_a)
    bcum = _dot(tri, la_hi) + _dot(tri, la_lo)
    b_hi = bcum.astype(BF16)
    b_last = bcum[ts - 1:ts, :]

    q_in = (q * jnp.exp(bcum)).astype(BF16)
    k_out = (k * jnp.exp(b_last - bcum)).astype(BF16)

    sizes = _gla_levels(ts)
    scores = [None] * GLA_HEADS
    for lvl, size in enumerate(sizes):
        half = size // 2
        same_block = (row & -size) == (col & -size)
        sel_col = (row & -size) + half
        if lvl == 0:
            pair = same_block & (col <= row)
            q_ok = k_ok = None
        else:
            pair = same_block & ((row & (size - 1)) >= half) & ((col & (size - 1)) < half)
            pos = lax.broadcasted_iota(I32, (ts, dk), 0) & (size - 1)
            q_ok = pos >= half
            k_ok = pos < half
        sel = jnp.where(col == sel_col, 1.0, 0.0).astype(BF16)
        ref = _dot(sel, b_hi)
        ql = q * jnp.exp(bcum - ref)
        kl = k * jnp.exp(ref - bcum)
        if q_ok is not None:
            ql = jnp.where(q_ok, ql, 0.0)
            kl = jnp.where(k_ok, kl, 0.0)
        ql = ql.astype(BF16)
        kl = kl.astype(BF16)
        for hd in range(GLA_HEADS):
            c0 = hd * dkh
            a = _dot_nt(ql[:, c0:c0 + dkh], kl[:, c0:c0 + dkh])
            a = jnp.where(pair, a, 0.0)
            scores[hd] = a if scores[hd] is None else scores[hd] + a

    outs = []
    for hd in range(GLA_HEADS):
        c0 = hd * dkh
        v_h = v[:, hd * dvh:(hd + 1) * dvh]
        st = state_ref[hd]
        o_h = _dot(scores[hd].astype(BF16), v_h) + _dot_nt(q_in[:, c0:c0 + dkh], st.astype(BF16))
        decay = jnp.exp(b_last[:, c0:c0 + dkh])
        state_ref[hd] = st * decay + _dot_tn(v_h, k_out[:, c0:c0 + dkh])
        o_h = o_h * lax.rsqrt(jnp.mean(o_h * o_h, axis=-1, keepdims=True) + EPS) * ng_ref[...]
        outs.append(o_h)
    o = jnp.concatenate(outs, axis=1) * (r * _sigmoid(r))
    o_ref[0] = x + _dot(o.astype(BF16), wo_ref[...])


def _gla_mixer(x, g, w_in, w_a2, b_a, norm_g, w_o):
    b, s, d = x.shape
    dk = w_a2.shape[1]
    ts = min(TILE_GLA, s)
    row = lambda v: v.reshape(1, -1)
    const = lambda shape: pl.BlockSpec(shape, lambda i, j: (0,) * len(shape))
    wq = w_in[:, :dk].astype(BF16)
    wk = w_in[:, dk:2 * dk].astype(BF16)
    wv = w_in[:, 2 * dk:2 * dk + d].astype(BF16)
    wa = jnp.pad(w_in[:, 2 * dk + d:2 * dk + d + GLA_RANK], ((0, 0), (0, GLA_RANK_PAD - GLA_RANK))).astype(BF16)
    wr = w_in[:, 2 * dk + d + GLA_RANK:].astype(BF16)
    wa2 = jnp.pad(w_a2, ((0, GLA_RANK_PAD - GLA_RANK), (0, 0)))
    wa2_hi = wa2.astype(BF16)
    wa2_lo = (wa2 - wa2_hi.astype(F32)).astype(BF16)
    wa2_split = jnp.stack([wa2_hi, wa2_lo])
    dvh = d // GLA_HEADS
    return pl.pallas_call(
        _gla_kernel,
        out_shape=jax.ShapeDtypeStruct(x.shape, F32),
        grid=(b, s // ts),
        in_specs=[
            pl.BlockSpec((1, ts, d), lambda i, j: (i, j, 0)),
            const((1, d)), const((d, dk)), const((d, dk)), const((d, d)), const((d, GLA_RANK_PAD)),
            const((d, d)), const((2, GLA_RANK_PAD, dk)), const((1, dk)), const((1, dvh)), const((d, d)),
        ],
        out_specs=pl.BlockSpec((1, ts, d), lambda i, j: (i, j, 0)),
        scratch_shapes=[pltpu.VMEM((GLA_HEADS, dvh, dk // GLA_HEADS), F32)],
        compiler_params=_cparams(("arbitrary", "arbitrary")),
        name="gla_mixer",
    )(x, row(g), wq, wk, wv, wa, wr, wa2_split, row(b_a), row(norm_g), w_o.astype(BF16))


def _xattn_router_kernel(x_ref, gx_ref, wq_ref, k_ref, v_ref, wo_ref, gf_ref, wr_ref, br_ref, upper_ref,
                         x_out_ref, h_out_ref, route_ref, cnt_ref, carry_ref):
    ts, d = x_ref.shape[1], x_ref.shape[2]
    hd = d // XATTN_HEADS
    first = (pl.program_id(0) == 0) & (pl.program_id(1) == 0)

    @pl.when(first)
    def _():
        carry_ref[...] = jnp.zeros(carry_ref.shape, F32)

    x = x_ref[0]
    h = _rms(x, gx_ref[...]).astype(BF16)
    q = (_dot(h, wq_ref[...]) * (hd ** -0.5)).astype(BF16)
    k = k_ref[0, 0]
    v = v_ref[0, 0]
    outs = []
    for a in range(XATTN_HEADS):
        sl = slice(a * hd, (a + 1) * hd)
        s = _dot_nt(q[:, sl], k[:, sl])
        p = jnp.exp(s - jnp.max(s, axis=-1, keepdims=True))
        o = _dot(p.astype(BF16), v[:, sl]) / jnp.sum(p, axis=-1, keepdims=True)
        outs.append(o)
    att = jnp.concatenate(outs, axis=1).astype(BF16)
    x2 = x + _dot(att, wo_ref[...])
    x_out_ref[0] = x2

    hf = _rms(x2, gf_ref[...])
    h_out_ref[...] = _pack_bf16_pairs(hf)

    h_hi, h_lo = _split_bf16(hf)
    w_hi, w_lo = wr_ref[0], wr_ref[1]
    logits = _dot_nt(w_hi, h_hi) + _dot_nt(w_hi, h_lo) + _dot_nt(w_lo, h_hi) + br_ref[...]
    gl = logits[N_EXPERTS:N_EXPERTS + N_GROUPS, :]
    gi = lax.broadcasted_iota(I32, gl.shape, 0).astype(F32)
    gmax = jnp.max(gl, axis=0, keepdims=True)
    g_sel = jnp.min(jnp.where(gl == gmax, gi, float(N_GROUPS)), axis=0, keepdims=True)
    pg_sel = 1.0 / jnp.sum(jnp.exp(gl - gmax), axis=0, keepdims=True)

    el = jnp.zeros((EXPERTS_PER_GROUP, ts), F32)
    for gidx in range(N_GROUPS):
        lo = gidx * EXPERTS_PER_GROUP
        el = jnp.where(g_sel == float(gidx), logits[lo:lo + EXPERTS_PER_GROUP, :], el)
    ei = lax.broadcasted_iota(I32, el.shape, 0).astype(F32)
    m1 = jnp.max(el, axis=0, keepdims=True)
    i1 = jnp.min(jnp.where(el == m1, ei, float(EXPERTS_PER_GROUP)), axis=0, keepdims=True)
    rest = jnp.where(ei == i1, -jnp.inf, el)
    m2 = jnp.max(rest, axis=0, keepdims=True)
    i2 = jnp.min(jnp.where(rest == m2, ei, float(EXPERTS_PER_GROUP)), axis=0, keepdims=True)
    ratio = jnp.exp(m2 - m1)
    gate1 = pg_sel / (1.0 + ratio)
    gate2 = pg_sel * ratio / (1.0 + ratio)
    e1 = g_sel * float(EXPERTS_PER_GROUP) + i1
    e2 = g_sel * float(EXPERTS_PER_GROUP) + i2

    xi = lax.broadcasted_iota(I32, (N_EXPERTS, ts), 0).astype(F32)
    oh1 = jnp.where(xi == e1, 1.0, 0.0)
    oh2 = jnp.where(xi == e2, 1.0, 0.0)
    oh = oh1 + oh2
    before = _dot(oh.astype(BF16), upper_ref[...]) + carry_ref[...]
    rank1 = jnp.sum(oh1 * before, axis=0, keepdims=True)
    rank2 = jnp.sum(oh2 * before, axis=0, keepdims=True)
    carry_ref[...] = carry_ref[...] + jnp.sum(oh, axis=1, keepdims=True)
    cnt_ref[...] = jnp.broadcast_to(carry_ref[...], cnt_ref.shape)

    route_ref[0:1, :] = e1.astype(I32)
    route_ref[1:2, :] = e2.astype(I32)
    route_ref[2:3, :] = rank1.astype(I32)
    route_ref[3:4, :] = rank2.astype(I32)
    route_ref[4:5, :] = lax.bitcast_convert_type(gate1, I32)
    route_ref[5:6, :] = lax.bitcast_convert_type(gate2, I32)
    route_ref[6:8, :] = jnp.zeros((2, ts), I32)


def _xattn_router(x, g_x, w_q, k_mem, v_mem, layer, w_o, g_f, w_grp, b_grp, w_exp, b_exp):
    b, s, d = x.shape
    nm = k_mem.shape[2]
    t = b * s
    ts = min(TILE_XATTN, s)
    nj = s // ts
    row = lambda v: v.reshape(1, -1)
    const = lambda shape: pl.BlockSpec(shape, lambda i, j: (0,) * len(shape))
    pad = ROUTER_ROWS - N_GROUPS - N_EXPERTS
    w_r = jnp.pad(jnp.concatenate([w_exp, w_grp], axis=1).T, ((0, pad), (0, 0)))
    w_r_hi = w_r.astype(BF16)
    w_r_split = jnp.stack([w_r_hi, (w_r - w_r_hi.astype(F32)).astype(BF16)])
    b_r = jnp.pad(jnp.concatenate([b_exp, b_grp]), (0, pad)).reshape(ROUTER_ROWS, 1)
    ti = jnp.arange(ts)
    upper = (ti[:, None] < ti[None, :]).astype(BF16)
    return pl.pallas_call(
        _xattn_router_kernel,
        out_shape=(
            jax.ShapeDtypeStruct(x.shape, F32),
            jax.ShapeDtypeStruct((t, d // 2), U32),
            jax.ShapeDtypeStruct((8, t), I32),
            jax.ShapeDtypeStruct((N_EXPERTS, 128), F32),
        ),
        grid=(b, nj),
        in_specs=[
            pl.BlockSpec((1, ts, d), lambda i, j: (i, j, 0)),
            const((1, d)), const((d, d)),
            pl.BlockSpec((1, 1, nm, d), lambda i, j: (layer, i, 0, 0)),
            pl.BlockSpec((1, 1, nm, d), lambda i, j: (layer, i, 0, 0)),
            const((d, d)), const((1, d)), const((2, ROUTER_ROWS, d)), const((ROUTER_ROWS, 1)), const((ts, ts)),
        ],
        out_specs=(
            pl.BlockSpec((1, ts, d), lambda i, j: (i, j, 0)),
            pl.BlockSpec((ts, d // 2), lambda i, j: (i * nj + j, 0)),
            pl.BlockSpec((8, ts), lambda i, j: (0, i * nj + j)),
            pl.BlockSpec((N_EXPERTS, 128), lambda i, j: (0, 0)),
        ),
        scratch_shapes=[pltpu.VMEM((N_EXPERTS, 1), F32)],
        compiler_params=_cparams(("arbitrary", "arbitrary")),
        name="xattn_router",
    )(x, row(g_x), w_q.astype(BF16), k_mem, v_mem, w_o.astype(BF16), row(g_f), w_r_split, b_r, upper)


def _dispatch_kernel(dest_ref, h_ref, zero_ref, xbuf_ref, sem):
    del zero_ref
    ts = h_ref.shape[0]

    def row_copy(t, slot):
        return pltpu.make_async_copy(h_ref.at[pl.ds(t, 1), :], xbuf_ref.at[pl.ds(dest_ref[slot, t], 1), :], sem)

    def issue(t, carry):
        row_copy(t, 0).start()
        row_copy(t, 1).start()
        return carry

    lax.fori_loop(0, ts, issue, 0)
    for _ in range(TOP_K):
        pltpu.make_async_copy(h_ref, xbuf_ref.at[pl.ds(0, ts), :], sem).wait()


def _dispatch(h_packed, dest, n_rows):
    t, w = h_packed.shape
    ts = min(TILE_DISPATCH, t)
    zeros = jnp.zeros((n_rows, w), U32)
    return pl.pallas_call(
        _dispatch_kernel,
        out_shape=jax.ShapeDtypeStruct((n_rows, w), U32),
        grid=(t // ts,),
        in_specs=[
            pl.BlockSpec((TOP_K, ts), lambda i: (0, i), memory_space=pltpu.SMEM),
            pl.BlockSpec((ts, w), lambda i: (i, 0)),
            pl.BlockSpec(memory_space=pl.ANY),
        ],
        out_specs=pl.BlockSpec(memory_space=pl.ANY),
        scratch_shapes=[pltpu.SemaphoreType.DMA],
        input_output_aliases={2: 0},
        compiler_params=_cparams(("arbitrary",)),
        name="moe_dispatch",
    )(dest, h_packed, zeros)


def _expert_kernel(be_ref, bf_ref, nu_ref, x_ref, wg_ref, wu_ref, wd_ref, y_ref, wg_bf, wu_bf, wd_bf):
    del be_ref
    blk = pl.program_id(0)

    @pl.when(bf_ref[blk] == 1)
    def _():
        wg_bf[...] = wg_ref[0, 0].astype(BF16)
        wu_bf[...] = wu_ref[0, 0].astype(BF16)
        wd_bf[...] = wd_ref[0, 0].astype(BF16)

    @pl.when(blk < nu_ref[0])
    def _():
        xb = _unpack_bf16_pairs(x_ref[...]).astype(BF16)
        gt = _dot(xb, wg_bf[...])
        up = _dot(xb, wu_bf[...])
        act = (gt * _sigmoid(gt) * up).astype(BF16)
        y_ref[...] = _pack_bf16_pairs(_dot(act, wd_bf[...]))

    @pl.when(blk >= nu_ref[0])
    def _():
        y_ref[...] = jnp.zeros(y_ref.shape, U32)


def _experts(x_buf, block_expert, block_first, n_used, w_gate, w_up, w_down, layer):
    n_rows, w = x_buf.shape
    d, de = w_gate.shape[2], w_gate.shape[3]
    bm = MOE_BLOCK_ROWS
    grid_spec = pltpu.PrefetchScalarGridSpec(
        num_scalar_prefetch=3,
        grid=(n_rows // bm,),
        in_specs=[
            pl.BlockSpec((bm, w), lambda i, be, bf, nu: (i, 0)),
            pl.BlockSpec((1, 1, d, de), lambda i, be, bf, nu: (layer, be[i], 0, 0)),
            pl.BlockSpec((1, 1, d, de), lambda i, be, bf, nu: (layer, be[i], 0, 0)),
            pl.BlockSpec((1, 1, de, d), lambda i, be, bf, nu: (layer, be[i], 0, 0)),
        ],
        out_specs=pl.BlockSpec((bm, w), lambda i, be, bf, nu: (i, 0)),
        scratch_shapes=[pltpu.VMEM((d, de), BF16), pltpu.VMEM((d, de), BF16), pltpu.VMEM((de, d), BF16)],
    )
    return pl.pallas_call(
        _expert_kernel,
        out_shape=jax.ShapeDtypeStruct((n_rows, w), U32),
        grid_spec=grid_spec,
        compiler_params=_cparams(("arbitrary",)),
        name="moe_experts",
    )(block_expert, block_first, n_used, x_buf, w_gate, w_up, w_down)


def _combine_kernel(dest_ref, x_ref, gate_ref, ybuf_ref, gfin_ref, o_ref, rows_ref, sem, *, final_norm):
    ts = x_ref.shape[0]

    def row_copy(t, slot):
        return pltpu.make_async_copy(ybuf_ref.at[pl.ds(dest_ref[slot, t], 1), :],
                                     rows_ref.at[slot, pl.ds(t, 1), :], sem)

    def issue(t, carry):
        row_copy(t, 0).start()
        row_copy(t, 1).start()
        return carry

    lax.fori_loop(0, ts, issue, 0)
    for slot in range(TOP_K):
        pltpu.make_async_copy(ybuf_ref.at[pl.ds(0, ts), :], rows_ref.at[slot], sem).wait()

    gates = gate_ref[...]
    y = (_unpack_bf16_pairs(rows_ref[0]) * gates[:, 0:1] + _unpack_bf16_pairs(rows_ref[1]) * gates[:, 1:2])
    out = x_ref[...] + y
    if final_norm:
        out = _rms(out, gfin_ref[...])
    o_ref[...] = out


def _combine(x2, y_buf, dest, gates, g_final, final_norm):
    t, d = x2.shape
    w = y_buf.shape[1]
    ts = min(TILE_COMBINE, t)
    return pl.pallas_call(
        functools.partial(_combine_kernel, final_norm=final_norm),
        out_shape=jax.ShapeDtypeStruct((t, d), F32),
        grid=(t // ts,),
        in_specs=[
            pl.BlockSpec((TOP_K, ts), lambda i: (0, i), memory_space=pltpu.SMEM),
            pl.BlockSpec((ts, d), lambda i: (i, 0)),
            pl.BlockSpec((ts, TOP_K), lambda i: (i, 0)),
            pl.BlockSpec(memory_space=pl.ANY),
            pl.BlockSpec((1, d), lambda i: (0, 0)),
        ],
        out_specs=pl.BlockSpec((ts, d), lambda i: (i, 0)),
        scratch_shapes=[pltpu.VMEM((TOP_K, ts, w), U32), pltpu.SemaphoreType.DMA],
        compiler_params=_cparams(("arbitrary",)),
        name="moe_combine",
    )(dest, x2, gates, y_buf, g_final.reshape(1, d))


def _moe_layout(route, counts):
    bm = MOE_BLOCK_ROWS
    t = route.shape[1]
    n_blocks = (t * TOP_K) // bm + N_EXPERTS
    cnt = counts[:, 0].astype(I32)
    padded = (cnt + bm - 1) // bm * bm
    pad_ends = jnp.cumsum(padded)
    pad_off = pad_ends - padded
    dest = pad_off[route[0:TOP_K]] + route[TOP_K:2 * TOP_K]
    gates = lax.bitcast_convert_type(route[2 * TOP_K:3 * TOP_K], F32).T
    starts = jnp.arange(n_blocks, dtype=I32) * bm
    block_expert = jnp.minimum(jnp.searchsorted(pad_ends, starts, side='right'), N_EXPERTS - 1).astype(I32)
    block_first = jnp.concatenate([jnp.ones((1,), I32), (block_expert[1:] != block_expert[:-1]).astype(I32)])
    n_used = (pad_ends[-1:] // bm).astype(I32)
    return dest, gates, block_expert, block_first, n_used, n_blocks * bm


def kernel(x, mem, norm_mix, norm_xattn, norm_ffn, norm_mem, norm_final, conv_w_in, conv_b_in, conv_w_dw,
           conv_b_dw, conv_ln_g, conv_ln_b, conv_w_out, conv_b_out, gla_w_in, gla_w_a2, gla_b_a, gla_norm_g,
           gla_w_o, xa_w_q, xa_w_kv, xa_w_o, moe_w_grp, moe_b_grp, moe_w_exp, moe_b_exp, moe_w_gate, moe_w_up,
           moe_w_down):
    b, s, d = x.shape
    depth = norm_mix.shape[0]
    k_mem, v_mem = _mem_kv(mem, norm_mem, xa_w_kv)
    for i in range(depth):
        j = i // 2
        if i % 2 == 0:
            x = _conv_mixer(x, norm_mix[i], conv_w_in[j], conv_b_in[j], conv_w_dw[j], conv_b_dw[j],
                            conv_ln_g[j], conv_ln_b[j], conv_w_out[j], conv_b_out[j])
        else:
            x = _gla_mixer(x, norm_mix[i], gla_w_in[j], gla_w_a2[j], gla_b_a[j], gla_norm_g[j], gla_w_o[j])
        x2, h_packed, route, counts = _xattn_router(
            x, norm_xattn[i], xa_w_q[i], k_mem, v_mem, i, xa_w_o[i], norm_ffn[i],
            moe_w_grp[i], moe_b_grp[i], moe_w_exp[i], moe_b_exp[i])
        dest, gates, block_expert, block_first, n_used, n_rows = _moe_layout(route, counts)
        x_buf = _dispatch(h_packed, dest, n_rows)
        y_buf = _experts(x_buf, block_expert, block_first, n_used, moe_w_gate, moe_w_up, moe_w_down, i)
        x = _combine(x2.reshape(b * s, d), y_buf, dest, gates, norm_final, i == depth - 1).reshape(b, s, d)
    return x
```

```python
import functools

import jax
import jax.numpy as jnp
from jax import lax
from jax.experimental import pallas as pl
from jax.experimental.pallas import tpu as pltpu

F32 = jnp.float32
BF16 = jnp.bfloat16
I32 = jnp.int32
U32 = jnp.uint32

EPS = 1e-6
CONV_KERNEL = 31
CONV_CARRY = 32
CONV_ROWS = 64
CONV_COLS = 256
SUBLANES = 8
GLA_HEADS = 4
GLA_RANK = 16
GLA_RANK_PAD = 128
GLA_TAU = 16.0
GLA_LEAF = 32
XATTN_HEADS = 4
N_GROUPS = 4
EXPERTS_PER_GROUP = 8
N_EXPERTS = N_GROUPS * EXPERTS_PER_GROUP
ROUTER_ROWS = 40
TOP_K = 2

TILE_CONV = 256
TILE_GLA = 256
TILE_XATTN = 512
TILE_DISPATCH = 512
TILE_COMBINE = 256
MOE_BLOCK_ROWS = 512
VMEM_LIMIT = 56 * 1024 * 1024


def _cparams(sem):
    return pltpu.CompilerParams(dimension_semantics=sem, vmem_limit_bytes=VMEM_LIMIT)


def _rms(x, g):
    return x * lax.rsqrt(jnp.mean(x * x, axis=-1, keepdims=True) + EPS) * g


def _sigmoid(x):
    return 0.5 * jnp.tanh(0.5 * x) + 0.5


def _split_bf16(x):
    hi = x.astype(BF16)
    lo = (x - hi.astype(F32)).astype(BF16)
    return hi, lo


def _dot(a, b):
    return jnp.dot(a, b, preferred_element_type=F32)


def _dot_nt(a, b):
    return lax.dot_general(a, b, (((1,), (1,)), ((), ())), preferred_element_type=F32)


def _dot_tn(a, b):
    return lax.dot_general(a, b, (((0,), (0,)), ((), ())), preferred_element_type=F32)


def _pack_bf16_pairs(x):
    w = x.shape[1] // 2
    hi = lax.bitcast_convert_type(x[:, :w].astype(BF16).astype(F32), U32)
    lo = lax.bitcast_convert_type(x[:, w:].astype(BF16).astype(F32), U32)
    return hi | (lo >> 16)


def _unpack_bf16_pairs(p):
    hi = lax.bitcast_convert_type(p & jnp.uint32(0xFFFF0000), F32)
    lo = lax.bitcast_convert_type(p << 16, F32)
    return jnp.concatenate([hi, lo], axis=1)


def _memkv_kernel(mem_ref, g_ref, w_ref, k_ref, v_ref):
    d = mem_ref.shape[-1]
    mn = _rms(mem_ref[0], g_ref[...]).astype(BF16)
    kv = _dot(mn, w_ref[0])
    k_ref[0, 0] = kv[:, :d].astype(BF16)
    v_ref[0, 0] = kv[:, d:].astype(BF16)


def _mem_kv(mem, norm_mem, w_kv):
    b, nm, d = mem.shape
    depth = w_kv.shape[0]
    out = jax.ShapeDtypeStruct((depth, b, nm, d), BF16)
    return pl.pallas_call(
        _memkv_kernel,
        out_shape=(out, out),
        grid=(depth, b),
        in_specs=[
            pl.BlockSpec((1, nm, d), lambda l, i: (i, 0, 0)),
            pl.BlockSpec((1, d), lambda l, i: (0, 0)),
            pl.BlockSpec((1, d, 2 * d), lambda l, i: (l, 0, 0)),
        ],
        out_specs=(
            pl.BlockSpec((1, 1, nm, d), lambda l, i: (l, i, 0, 0)),
            pl.BlockSpec((1, 1, nm, d), lambda l, i: (l, i, 0, 0)),
        ),
        compiler_params=_cparams(("arbitrary", "arbitrary")),
        name="mem_kv",
    )(mem, norm_mem.reshape(1, d), w_kv.astype(BF16))


def _conv_kernel(x_ref, g_ref, win_ref, bin_ref, wdw_ref, bdw_ref, lng_ref, lnb_ref, wout_ref, bout_ref,
                 o_ref, ext_ref, conv_ref):
    ts, d = x_ref.shape[1], x_ref.shape[2]

    @pl.when(pl.program_id(1) == 0)
    def _():
        ext_ref[...] = jnp.zeros(ext_ref.shape, F32)

    x = x_ref[0]
    h = _rms(x, g_ref[...]).astype(BF16)
    u = _dot(h, win_ref[...]) + bin_ref[...]
    glu = u[:, :d] * _sigmoid(u[:, d:])
    for b in range(SUBLANES):
        ext_ref[b, CONV_CARRY - b:CONV_CARRY - b + ts, :] = glu

    first = CONV_CARRY - (CONV_KERNEL - 1)

    def chunk(i, carry):
        r0 = pl.multiple_of(i * CONV_ROWS, CONV_ROWS)
        for c0 in range(0, d, CONV_COLS):
            cols = slice(c0, c0 + CONV_COLS)
            acc = [jnp.zeros((SUBLANES, CONV_COLS), F32) for _ in range(CONV_ROWS // SUBLANES)]
            for k in range(CONV_KERNEL):
                b = (first + k) % SUBLANES
                wk = wdw_ref[k * SUBLANES:(k + 1) * SUBLANES, cols]
                for j in range(CONV_ROWS // SUBLANES):
                    rows = pl.ds(r0 + (first + k - b) + j * SUBLANES, SUBLANES)
                    acc[j] = acc[j] + ext_ref[b, rows, cols] * wk
            for j in range(CONV_ROWS // SUBLANES):
                conv_ref[pl.ds(r0 + j * SUBLANES, SUBLANES), cols] = acc[j]
        return carry

    lax.fori_loop(0, ts // CONV_ROWS, chunk, 0)
    for b in range(SUBLANES):
        ext_ref[b, 0:CONV_CARRY, :] = ext_ref[b, ts:ts + CONV_CARRY, :]

    c = conv_ref[...] + bdw_ref[...]
    mu = jnp.mean(c, axis=-1, keepdims=True)
    cc = c - mu
    var = jnp.mean(cc * cc, axis=-1, keepdims=True)
    un = cc * lax.rsqrt(var + EPS) * lng_ref[...] + lnb_ref[...]
    act = (un * _sigmoid(un)).astype(BF16)
    o_ref[0] = x + _dot(act, wout_ref[...]) + bout_ref[...]


def _conv_mixer(x, g, w_in, b_in, w_dw, b_dw, ln_g, ln_b, w_out, b_out):
    b, s, d = x.shape
    ts = min(TILE_CONV, s)
    row = lambda v: v.reshape(1, -1)
    const = lambda shape: pl.BlockSpec(shape, lambda i, j: (0,) * len(shape))
    return pl.pallas_call(
        _conv_kernel,
        out_shape=jax.ShapeDtypeStruct(x.shape, F32),
        grid=(b, s // ts),
        in_specs=[
            pl.BlockSpec((1, ts, d), lambda i, j: (i, j, 0)),
            const((1, d)), const((d, 2 * d)), const((1, 2 * d)), const((CONV_KERNEL * SUBLANES, d)), const((1, d)),
            const((1, d)), const((1, d)), const((d, d)), const((1, d)),
        ],
        out_specs=pl.BlockSpec((1, ts, d), lambda i, j: (i, j, 0)),
        scratch_shapes=[pltpu.VMEM((SUBLANES, CONV_CARRY + ts, d), F32), pltpu.VMEM((ts, d), F32)],
        compiler_params=_cparams(("arbitrary", "arbitrary")),
        name="conv_mixer",
    )(x, row(g), w_in.astype(BF16), row(b_in), jnp.repeat(w_dw, SUBLANES, axis=0), row(b_dw), row(ln_g), row(ln_b), w_out.astype(BF16),
      row(b_out))


def _gla_levels(ts):
    sizes = [GLA_LEAF]
    while sizes[-1] < ts:
        sizes.append(sizes[-1] * 2)
    return sizes


def _gla_kernel(x_ref, g_ref, wq_ref, wk_ref, wv_ref, wa_ref, wr_ref, wa2_ref, ba_ref, ng_ref, wo_ref,
                o_ref, state_ref):
    ts, d = x_ref.shape[1], x_ref.shape[2]
    dk = wq_ref.shape[1]
    dkh = dk // GLA_HEADS
    dvh = d // GLA_HEADS

    @pl.when(pl.program_id(1) == 0)
    def _():
        state_ref[...] = jnp.zeros(state_ref.shape, F32)

    x = x_ref[0]
    h = _rms(x, g_ref[...]).astype(BF16)
    q = _dot(h, wq_ref[...]) * (dkh ** -0.5)
    k = _dot(h, wk_ref[...])
    v = _dot(h, wv_ref[...]).astype(BF16)
    r = _dot(h, wr_ref[...])
    a_hi, a_lo = _split_bf16(_dot(h, wa_ref[...]))
    w2_hi, w2_lo = wa2_ref[0], wa2_ref[1]
    z = _dot(a_hi, w2_hi) + _dot(a_lo, w2_hi) + _dot(a_hi, w2_lo) + ba_ref[...]
    log_a = -(jnp.maximum(-z, 0.0) + jnp.log(1.0 + jnp.exp(-jnp.abs(z)))) * (1.0 / GLA_TAU)

    row = lax.broadcasted_iota(I32, (ts, ts), 0)
    col = lax.broadcasted_iota(I32, (ts, ts), 1)
    tri = jnp.where(col <= row, 1.0, 0.0).astype(BF16)
    la_hi, la_lo = _split_bf16(log_a)
    bcum = _dot(tri, la_hi) + _dot(tri, la_lo)
    b_last = bcum[ts - 1:ts, :]

    q_in = (q * jnp.exp(bcum)).astype(BF16)
    k_out = (k * jnp.exp(b_last - bcum)).astype(BF16)

    sizes = _gla_levels(ts)
    scores = [None] * GLA_HEADS
    for lvl, size in enumerate(sizes):
        half = size // 2
        same_block = (row & -size) == (col & -size)
        if lvl == 0:
            pair = same_block & (col <= row)
            q_ok = k_ok = None
        else:
            pair = same_block & ((row & (size - 1)) >= half) & ((col & (size - 1)) < half)
            pos = lax.broadcasted_iota(I32, (ts, dk), 0) & (size - 1)
            q_ok = pos >= half
            k_ok = pos < half
        ref = jnp.concatenate(
            [jnp.broadcast_to(bcum[r0 + half:r0 + half + 1, :], (size, dk)) for r0 in range(0, ts, size)], axis=0)
        ql = q * jnp.exp(bcum - ref)
        kl = k * jnp.exp(ref - bcum)
        if q_ok is not None:
            ql = jnp.where(q_ok, ql, 0.0)
            kl = jnp.where(k_ok, kl, 0.0)
        ql = ql.astype(BF16)
        kl = kl.astype(BF16)
        for hd in range(GLA_HEADS):
            c0 = hd * dkh
            a = _dot_nt(ql[:, c0:c0 + dkh], kl[:, c0:c0 + dkh])
            a = jnp.where(pair, a, 0.0)
            scores[hd] = a if scores[hd] is None else scores[hd] + a

    outs = []
    for hd in range(GLA_HEADS):
        c0 = hd * dkh
        v_h = v[:, hd * dvh:(hd + 1) * dvh]
        st = state_ref[hd]
        o_h = _dot(scores[hd].astype(BF16), v_h) + _dot_nt(q_in[:, c0:c0 + dkh], st.astype(BF16))
        decay = jnp.exp(b_last[:, c0:c0 + dkh])
        state_ref[hd] = st * decay + _dot_tn(v_h, k_out[:, c0:c0 + dkh])
        o_h = o_h * lax.rsqrt(jnp.mean(o_h * o_h, axis=-1, keepdims=True) + EPS) * ng_ref[...]
        outs.append(o_h)
    o = jnp.concatenate(outs, axis=1) * (r * _sigmoid(r))
    o_ref[0] = x + _dot(o.astype(BF16), wo_ref[...])


def _gla_mixer(x, g, w_in, w_a2, b_a, norm_g, w_o):
    b, s, d = x.shape
    dk = w_a2.shape[1]
    ts = min(TILE_GLA, s)
    row = lambda v: v.reshape(1, -1)
    const = lambda shape: pl.BlockSpec(shape, lambda i, j: (0,) * len(shape))
    wq = w_in[:, :dk].astype(BF16)
    wk = w_in[:, dk:2 * dk].astype(BF16)
    wv = w_in[:, 2 * dk:2 * dk + d].astype(BF16)
    wa = jnp.pad(w_in[:, 2 * dk + d:2 * dk + d + GLA_RANK], ((0, 0), (0, GLA_RANK_PAD - GLA_RANK))).astype(BF16)
    wr = w_in[:, 2 * dk + d + GLA_RANK:].astype(BF16)
    wa2 = jnp.pad(w_a2, ((0, GLA_RANK_PAD - GLA_RANK), (0, 0)))
    wa2_hi = wa2.astype(BF16)
    wa2_lo = (wa2 - wa2_hi.astype(F32)).astype(BF16)
    wa2_split = jnp.stack([wa2_hi, wa2_lo])
    dvh = d // GLA_HEADS
    return pl.pallas_call(
        _gla_kernel,
        out_shape=jax.ShapeDtypeStruct(x.shape, F32),
        grid=(b, s // ts),
        in_specs=[
            pl.BlockSpec((1, ts, d), lambda i, j: (i, j, 0)),
            const((1, d)), const((d, dk)), const((d, dk)), const((d, d)), const((d, GLA_RANK_PAD)),
            const((d, d)), const((2, GLA_RANK_PAD, dk)), const((1, dk)), const((1, dvh)), const((d, d)),
        ],
        out_specs=pl.BlockSpec((1, ts, d), lambda i, j: (i, j, 0)),
        scratch_shapes=[pltpu.VMEM((GLA_HEADS, dvh, dk // GLA_HEADS), F32)],
        compiler_params=_cparams(("arbitrary", "arbitrary")),
        name="gla_mixer",
    )(x, row(g), wq, wk, wv, wa, wr, wa2_split, row(b_a), row(norm_g), w_o.astype(BF16))


def _xattn_router_kernel(x_ref, gx_ref, wq_ref, k_ref, v_ref, wo_ref, gf_ref, wr_ref, br_ref, upper_ref,
                         x_out_ref, h_out_ref, route_ref, cnt_ref, carry_ref):
    ts, d = x_ref.shape[1], x_ref.shape[2]
    hd = d // XATTN_HEADS
    first = (pl.program_id(0) == 0) & (pl.program_id(1) == 0)

    @pl.when(first)
    def _():
        carry_ref[...] = jnp.zeros(carry_ref.shape, F32)

    x = x_ref[0]
    h = _rms(x, gx_ref[...]).astype(BF16)
    q = (_dot(h, wq_ref[...]) * (hd ** -0.5)).astype(BF16)
    k = k_ref[0, 0]
    v = v_ref[0, 0]
    outs = []
    for a in range(XATTN_HEADS):
        sl = slice(a * hd, (a + 1) * hd)
        s = _dot_nt(q[:, sl], k[:, sl])
        p = jnp.exp(s - jnp.max(s, axis=-1, keepdims=True))
        o = _dot(p.astype(BF16), v[:, sl]) / jnp.sum(p, axis=-1, keepdims=True)
        outs.append(o)
    att = jnp.concatenate(outs, axis=1).astype(BF16)
    x2 = x + _dot(att, wo_ref[...])
    x_out_ref[0] = x2

    hf = _rms(x2, gf_ref[...])
    h_out_ref[...] = _pack_bf16_pairs(hf)

    h_hi, h_lo = _split_bf16(hf)
    w_hi, w_lo = wr_ref[0], wr_ref[1]
    logits = _dot_nt(w_hi, h_hi) + _dot_nt(w_hi, h_lo) + _dot_nt(w_lo, h_hi) + br_ref[...]
    gl = logits[N_EXPERTS:N_EXPERTS + N_GROUPS, :]
    gi = lax.broadcasted_iota(I32, gl.shape, 0).astype(F32)
    gmax = jnp.max(gl, axis=0, keepdims=True)
    g_sel = jnp.min(jnp.where(gl == gmax, gi, float(N_GROUPS)), axis=0, keepdims=True)
    pg_sel = 1.0 / jnp.sum(jnp.exp(gl - gmax), axis=0, keepdims=True)

    el = jnp.zeros((EXPERTS_PER_GROUP, ts), F32)
    for gidx in range(N_GROUPS):
        lo = gidx * EXPERTS_PER_GROUP
        el = jnp.where(g_sel == float(gidx), logits[lo:lo + EXPERTS_PER_GROUP, :], el)
    ei = lax.broadcasted_iota(I32, el.shape, 0).astype(F32)
    m1 = jnp.max(el, axis=0, keepdims=True)
    i1 = jnp.min(jnp.where(el == m1, ei, float(EXPERTS_PER_GROUP)), axis=0, keepdims=True)
    rest = jnp.where(ei == i1, -jnp.inf, el)
    m2 = jnp.max(rest, axis=0, keepdims=True)
    i2 = jnp.min(jnp.where(rest == m2, ei, float(EXPERTS_PER_GROUP)), axis=0, keepdims=True)
    ratio = jnp.exp(m2 - m1)
    gate1 = pg_sel / (1.0 + ratio)
    gate2 = pg_sel * ratio / (1.0 + ratio)
    e1 = g_sel * float(EXPERTS_PER_GROUP) + i1
    e2 = g_sel * float(EXPERTS_PER_GROUP) + i2

    xi = lax.broadcasted_iota(I32, (N_EXPERTS, ts), 0).astype(F32)
    oh1 = jnp.where(xi == e1, 1.0, 0.0)
    oh2 = jnp.where(xi == e2, 1.0, 0.0)
    oh = oh1 + oh2
    before = _dot(oh.astype(BF16), upper_ref[...]) + carry_ref[...]
    rank1 = jnp.sum(oh1 * before, axis=0, keepdims=True)
    rank2 = jnp.sum(oh2 * before, axis=0, keepdims=True)
    carry_ref[...] = carry_ref[...] + jnp.sum(oh, axis=1, keepdims=True)
    cnt_ref[...] = jnp.broadcast_to(carry_ref[...], cnt_ref.shape)

    route_ref[0:1, :] = e1.astype(I32)
    route_ref[1:2, :] = e2.astype(I32)
    route_ref[2:3, :] = rank1.astype(I32)
    route_ref[3:4, :] = rank2.astype(I32)
    route_ref[4:5, :] = lax.bitcast_convert_type(gate1, I32)
    route_ref[5:6, :] = lax.bitcast_convert_type(gate2, I32)
    route_ref[6:8, :] = jnp.zeros((2, ts), I32)


def _xattn_router(x, g_x, w_q, k_mem, v_mem, layer, w_o, g_f, w_grp, b_grp, w_exp, b_exp):
    b, s, d = x.shape
    nm = k_mem.shape[2]
    t = b * s
    ts = min(TILE_XATTN, s)
    nj = s // ts
    row = lambda v: v.reshape(1, -1)
    const = lambda shape: pl.BlockSpec(shape, lambda i, j: (0,) * len(shape))
    pad = ROUTER_ROWS - N_GROUPS - N_EXPERTS
    w_r = jnp.pad(jnp.concatenate([w_exp, w_grp], axis=1).T, ((0, pad), (0, 0)))
    w_r_hi = w_r.astype(BF16)
    w_r_split = jnp.stack([w_r_hi, (w_r - w_r_hi.astype(F32)).astype(BF16)])
    b_r = jnp.pad(jnp.concatenate([b_exp, b_grp]), (0, pad)).reshape(ROUTER_ROWS, 1)
    ti = jnp.arange(ts)
    upper = (ti[:, None] < ti[None, :]).astype(BF16)
    return pl.pallas_call(
        _xattn_router_kernel,
        out_shape=(
            jax.ShapeDtypeStruct(x.shape, F32),
            jax.ShapeDtypeStruct((t, d // 2), U32),
            jax.ShapeDtypeStruct((8, t), I32),
            jax.ShapeDtypeStruct((N_EXPERTS, 128), F32),
        ),
        grid=(b, nj),
        in_specs=[
            pl.BlockSpec((1, ts, d), lambda i, j: (i, j, 0)),
            const((1, d)), const((d, d)),
            pl.BlockSpec((1, 1, nm, d), lambda i, j: (layer, i, 0, 0)),
            pl.BlockSpec((1, 1, nm, d), lambda i, j: (layer, i, 0, 0)),
            const((d, d)), const((1, d)), const((2, ROUTER_ROWS, d)), const((ROUTER_ROWS, 1)), const((ts, ts)),
        ],
        out_specs=(
            pl.BlockSpec((1, ts, d), lambda i, j: (i, j, 0)),
            pl.BlockSpec((ts, d // 2), lambda i, j: (i * nj + j, 0)),
            pl.BlockSpec((8, ts), lambda i, j: (0, i * nj + j)),
            pl.BlockSpec((N_EXPERTS, 128), lambda i, j: (0, 0)),
        ),
        scratch_shapes=[pltpu.VMEM((N_EXPERTS, 1), F32)],
        compiler_params=_cparams(("arbitrary", "arbitrary")),
        name="xattn_router",
    )(x, row(g_x), w_q.astype(BF16), k_mem, v_mem, w_o.astype(BF16), row(g_f), w_r_split, b_r, upper)


def _dispatch_kernel(dest_ref, h_ref, zero_ref, xbuf_ref, sem):
    del zero_ref
    ts = h_ref.shape[0]

    def row_copy(t, slot):
        return pltpu.make_async_copy(h_ref.at[pl.ds(t, 1), :], xbuf_ref.at[pl.ds(dest_ref[slot, t], 1), :], sem)

    def issue(t, carry):
        row_copy(t, 0).start()
        row_copy(t, 1).start()
        return carry

    lax.fori_loop(0, ts, issue, 0)
    for _ in range(TOP_K):
        pltpu.make_async_copy(h_ref, xbuf_ref.at[pl.ds(0, ts), :], sem).wait()


def _dispatch(h_packed, dest, n_rows):
    t, w = h_packed.shape
    ts = min(TILE_DISPATCH, t)
    zeros = jnp.zeros((n_rows, w), U32)
    return pl.pallas_call(
        _dispatch_kernel,
        out_shape=jax.ShapeDtypeStruct((n_rows, w), U32),
        grid=(t // ts,),
        in_specs=[
            pl.BlockSpec((TOP_K, ts), lambda i: (0, i), memory_space=pltpu.SMEM),
            pl.BlockSpec((ts, w), lambda i: (i, 0)),
            pl.BlockSpec(memory_space=pl.ANY),
        ],
        out_specs=pl.BlockSpec(memory_space=pl.ANY),
        scratch_shapes=[pltpu.SemaphoreType.DMA],
        input_output_aliases={2: 0},
        compiler_params=_cparams(("arbitrary",)),
        name="moe_dispatch",
    )(dest, h_packed, zeros)


def _expert_kernel(be_ref, bf_ref, nu_ref, x_ref, wg_ref, wu_ref, wd_ref, y_ref, wg_bf, wu_bf, wd_bf):
    del be_ref
    blk = pl.program_id(0)

    @pl.when(bf_ref[blk] == 1)
    def _():
        wg_bf[...] = wg_ref[0, 0].astype(BF16)
        wu_bf[...] = wu_ref[0, 0].astype(BF16)
        wd_bf[...] = wd_ref[0, 0].astype(BF16)

    @pl.when(blk < nu_ref[0])
    def _():
        xb = _unpack_bf16_pairs(x_ref[...]).astype(BF16)
        gt = _dot(xb, wg_bf[...])
        up = _dot(xb, wu_bf[...])
        act = (gt * _sigmoid(gt) * up).astype(BF16)
        y_ref[...] = _pack_bf16_pairs(_dot(act, wd_bf[...]))

    @pl.when(blk >= nu_ref[0])
    def _():
        y_ref[...] = jnp.zeros(y_ref.shape, U32)


def _experts(x_buf, block_expert, block_first, n_used, w_gate, w_up, w_down, layer):
    n_rows, w = x_buf.shape
    d, de = w_gate.shape[2], w_gate.shape[3]
    bm = MOE_BLOCK_ROWS
    grid_spec = pltpu.PrefetchScalarGridSpec(
        num_scalar_prefetch=3,
        grid=(n_rows // bm,),
        in_specs=[
            pl.BlockSpec((bm, w), lambda i, be, bf, nu: (i, 0)),
            pl.BlockSpec((1, 1, d, de), lambda i, be, bf, nu: (layer, be[i], 0, 0)),
            pl.BlockSpec((1, 1, d, de), lambda i, be, bf, nu: (layer, be[i], 0, 0)),
            pl.BlockSpec((1, 1, de, d), lambda i, be, bf, nu: (layer, be[i], 0, 0)),
        ],
        out_specs=pl.BlockSpec((bm, w), lambda i, be, bf, nu: (i, 0)),
        scratch_shapes=[pltpu.VMEM((d, de), BF16), pltpu.VMEM((d, de), BF16), pltpu.VMEM((de, d), BF16)],
    )
    return pl.pallas_call(
        _expert_kernel,
        out_shape=jax.ShapeDtypeStruct((n_rows, w), U32),
        grid_spec=grid_spec,
        compiler_params=_cparams(("arbitrary",)),
        name="moe_experts",
    )(block_expert, block_first, n_used, x_buf, w_gate, w_up, w_down)


def _combine_kernel(dest_ref, x_ref, gate_ref, ybuf_ref, gfin_ref, o_ref, rows_ref, sem, *, final_norm):
    ts = x_ref.shape[0]

    def row_copy(t, slot):
        return pltpu.make_async_copy(ybuf_ref.at[pl.ds(dest_ref[slot, t], 1), :],
                                     rows_ref.at[slot, pl.ds(t, 1), :], sem)

    def issue(t, carry):
        row_copy(t, 0).start()
        row_copy(t, 1).start()
        return carry

    lax.fori_loop(0, ts, issue, 0)
    for slot in range(TOP_K):
        pltpu.make_async_copy(ybuf_ref.at[pl.ds(0, ts), :], rows_ref.at[slot], sem).wait()

    gates = gate_ref[...]
    y = (_unpack_bf16_pairs(rows_ref[0]) * gates[:, 0:1] + _unpack_bf16_pairs(rows_ref[1]) * gates[:, 1:2])
    out = x_ref[...] + y
    if final_norm:
        out = _rms(out, gfin_ref[...])
    o_ref[...] = out


def _combine(x2, y_buf, dest, gates, g_final, final_norm):
    t, d = x2.shape
    w = y_buf.shape[1]
    ts = min(TILE_COMBINE, t)
    return pl.pallas_call(
        functools.partial(_combine_kernel, final_norm=final_norm),
        out_shape=jax.ShapeDtypeStruct((t, d), F32),
        grid=(t // ts,),
        in_specs=[
            pl.BlockSpec((TOP_K, ts), lambda i: (0, i), memory_space=pltpu.SMEM),
            pl.BlockSpec((ts, d), lambda i: (i, 0)),
            pl.BlockSpec((ts, TOP_K), lambda i: (i, 0)),
            pl.BlockSpec(memory_space=pl.ANY),
            pl.BlockSpec((1, d), lambda i: (0, 0)),
        ],
        out_specs=pl.BlockSpec((ts, d), lambda i: (i, 0)),
        scratch_shapes=[pltpu.VMEM((TOP_K, ts, w), U32), pltpu.SemaphoreType.DMA],
        compiler_params=_cparams(("arbitrary",)),
        name="moe_combine",
    )(dest, x2, gates, y_buf, g_final.reshape(1, d))


def _moe_layout(route, counts):
    bm = MOE_BLOCK_ROWS
    t = route.shape[1]
    n_blocks = (t * TOP_K) // bm + N_EXPERTS
    cnt = counts[:, 0].astype(I32)
    padded = (cnt + bm - 1) // bm * bm
    pad_ends = jnp.cumsum(padded)
    pad_off = pad_ends - padded
    experts = jnp.arange(N_EXPERTS, dtype=I32)
    hit = route[0:TOP_K, :, None] == experts
    dest = jnp.sum(jnp.where(hit, pad_off, 0), axis=-1) + route[TOP_K:2 * TOP_K]
    gates = lax.bitcast_convert_type(route[2 * TOP_K:3 * TOP_K], F32).T
    starts = jnp.arange(n_blocks, dtype=I32) * bm
    block_expert = jnp.minimum(jnp.sum((pad_ends[None, :] <= starts[:, None]).astype(I32), axis=1),
                               N_EXPERTS - 1)
    block_first = jnp.concatenate([jnp.ones((1,), I32), (block_expert[1:] != block_expert[:-1]).astype(I32)])
    n_used = (pad_ends[-1:] // bm).astype(I32)
    return dest, gates, block_expert, block_first, n_used, n_blocks * bm


def kernel(x, mem, norm_mix, norm_xattn, norm_ffn, norm_mem, norm_final, conv_w_in, conv_b_in, conv_w_dw,
           conv_b_dw, conv_ln_g, conv_ln_b, conv_w_out, conv_b_out, gla_w_in, gla_w_a2, gla_b_a, gla_norm_g,
           gla_w_o, xa_w_q, xa_w_kv, xa_w_o, moe_w_grp, moe_b_grp, moe_w_exp, moe_b_exp, moe_w_gate, moe_w_up,
           moe_w_down):
    b, s, d = x.shape
    depth = norm_mix.shape[0]
    k_mem, v_mem = _mem_kv(mem, norm_mem, xa_w_kv)
    for i in range(depth):
        j = i // 2
        if i % 2 == 0:
            x = _conv_mixer(x, norm_mix[i], conv_w_in[j], conv_b_in[j], conv_w_dw[j], conv_b_dw[j],
                            conv_ln_g[j], conv_ln_b[j], conv_w_out[j], conv_b_out[j])
        else:
            x = _gla_mixer(x, norm_mix[i], gla_w_in[j], gla_w_a2[j], gla_b_a[j], gla_norm_g[j], gla_w_o[j])
        x2, h_packed, route, counts = _xattn_router(
            x, norm_xattn[i], xa_w_q[i], k_mem, v_mem, i, xa_w_o[i], norm_ffn[i],
            moe_w_grp[i], moe_b_grp[i], moe_w_exp[i], moe_b_exp[i])
        dest, gates, block_expert, block_first, n_used, n_rows = _moe_layout(route, counts)
        x_buf = _dispatch(h_packed, dest, n_rows)
        y_buf = _experts(x_buf, block_expert, block_first, n_used, moe_w_gate, moe_w_up, moe_w_down, i)
        x = _combine(x2.reshape(b * s, d), y_buf, dest, gates, norm_final, i == depth - 1).reshape(b, s, d)
    return x
```

```python
import functools

import jax
import jax.numpy as jnp
from jax import lax
from jax.experimental import pallas as pl
from jax.experimental.pallas import tpu as pltpu
from jax.experimental.pallas import tpu_sc as plsc

F32 = jnp.float32
BF16 = jnp.bfloat16
I32 = jnp.int32
U32 = jnp.uint32

EPS = 1e-6
CONV_KERNEL = 31
CONV_CARRY = 32
CONV_ROWS = 64
CONV_COLS = 256
SUBLANES = 8
GLA_HEADS = 4
GLA_RANK = 16
GLA_RANK_PAD = 128
GLA_TAU = 16.0
GLA_LEAF = 32
XATTN_HEADS = 4
N_GROUPS = 4
EXPERTS_PER_GROUP = 8
N_EXPERTS = N_GROUPS * EXPERTS_PER_GROUP
ROUTER_ROWS = 40
TOP_K = 2

TILE_CONV = 256
TILE_GLA = 256
TILE_XATTN = 512
TILE_COMBINE = 512
SC_CORES = 2
SC_SUBCORES = 16
SC_WORKERS = SC_CORES * SC_SUBCORES
SC_CHUNK = 64
MOE_BLOCK_ROWS = 512
VMEM_LIMIT = 56 * 1024 * 1024


def _cparams(sem):
    return pltpu.CompilerParams(dimension_semantics=sem, vmem_limit_bytes=VMEM_LIMIT)


def _rms(x, g):
    return x * lax.rsqrt(jnp.mean(x * x, axis=-1, keepdims=True) + EPS) * g


def _sigmoid(x):
    return 0.5 * jnp.tanh(0.5 * x) + 0.5


def _split_bf16(x):
    hi = x.astype(BF16)
    lo = (x - hi.astype(F32)).astype(BF16)
    return hi, lo


def _dot(a, b):
    return jnp.dot(a, b, preferred_element_type=F32)


def _dot_nt(a, b):
    return lax.dot_general(a, b, (((1,), (1,)), ((), ())), preferred_element_type=F32)


def _dot_tn(a, b):
    return lax.dot_general(a, b, (((0,), (0,)), ((), ())), preferred_element_type=F32)


def _pack_bf16_pairs(x):
    w = x.shape[1] // 2
    hi = lax.bitcast_convert_type(x[:, :w].astype(BF16).astype(F32), U32)
    lo = lax.bitcast_convert_type(x[:, w:].astype(BF16).astype(F32), U32)
    return hi | (lo >> 16)


def _unpack_bf16_pairs(p):
    hi = lax.bitcast_convert_type(p & jnp.uint32(0xFFFF0000), F32)
    lo = lax.bitcast_convert_type(p << 16, F32)
    return jnp.concatenate([hi, lo], axis=1)


def _memkv_kernel(mem_ref, g_ref, w_ref, k_ref, v_ref):
    d = mem_ref.shape[-1]
    mn = _rms(mem_ref[0], g_ref[...]).astype(BF16)
    kv = _dot(mn, w_ref[0])
    k_ref[0, 0] = kv[:, :d].astype(BF16)
    v_ref[0, 0] = kv[:, d:].astype(BF16)


def _mem_kv(mem, norm_mem, w_kv):
    b, nm, d = mem.shape
    depth = w_kv.shape[0]
    out = jax.ShapeDtypeStruct((depth, b, nm, d), BF16)
    return pl.pallas_call(
        _memkv_kernel,
        out_shape=(out, out),
        grid=(depth, b),
        in_specs=[
            pl.BlockSpec((1, nm, d), lambda l, i: (i, 0, 0)),
            pl.BlockSpec((1, d), lambda l, i: (0, 0)),
            pl.BlockSpec((1, d, 2 * d), lambda l, i: (l, 0, 0)),
        ],
        out_specs=(
            pl.BlockSpec((1, 1, nm, d), lambda l, i: (l, i, 0, 0)),
            pl.BlockSpec((1, 1, nm, d), lambda l, i: (l, i, 0, 0)),
        ),
        compiler_params=_cparams(("arbitrary", "arbitrary")),
        name="mem_kv",
    )(mem, norm_mem.reshape(1, d), w_kv.astype(BF16))


def _conv_kernel(x_ref, g_ref, win_ref, bin_ref, wdw_ref, bdw_ref, lng_ref, lnb_ref, wout_ref, bout_ref,
                 o_ref, ext_ref, conv_ref):
    ts, d = x_ref.shape[1], x_ref.shape[2]

    @pl.when(pl.program_id(1) == 0)
    def _():
        ext_ref[...] = jnp.zeros(ext_ref.shape, F32)

    x = x_ref[0]
    h = _rms(x, g_ref[...]).astype(BF16)
    u = _dot(h, win_ref[...]) + bin_ref[...]
    glu = u[:, :d] * _sigmoid(u[:, d:])
    for b in range(SUBLANES):
        ext_ref[b, CONV_CARRY - b:CONV_CARRY - b + ts, :] = glu

    first = CONV_CARRY - (CONV_KERNEL - 1)

    def chunk(i, carry):
        r0 = pl.multiple_of(i * CONV_ROWS, CONV_ROWS)
        for c0 in range(0, d, CONV_COLS):
            cols = slice(c0, c0 + CONV_COLS)
            acc = [jnp.zeros((SUBLANES, CONV_COLS), F32) for _ in range(CONV_ROWS // SUBLANES)]
            for k in range(CONV_KERNEL):
                b = (first + k) % SUBLANES
                wk = wdw_ref[k * SUBLANES:(k + 1) * SUBLANES, cols]
                for j in range(CONV_ROWS // SUBLANES):
                    rows = pl.ds(r0 + (first + k - b) + j * SUBLANES, SUBLANES)
                    acc[j] = acc[j] + ext_ref[b, rows, cols] * wk
            for j in range(CONV_ROWS // SUBLANES):
                conv_ref[pl.ds(r0 + j * SUBLANES, SUBLANES), cols] = acc[j]
        return carry

    lax.fori_loop(0, ts // CONV_ROWS, chunk, 0)
    for b in range(SUBLANES):
        ext_ref[b, 0:CONV_CARRY, :] = ext_ref[b, ts:ts + CONV_CARRY, :]

    c = conv_ref[...] + bdw_ref[...]
    mu = jnp.mean(c, axis=-1, keepdims=True)
    cc = c - mu
    var = jnp.mean(cc * cc, axis=-1, keepdims=True)
    un = cc * lax.rsqrt(var + EPS) * lng_ref[...] + lnb_ref[...]
    act = (un * _sigmoid(un)).astype(BF16)
    o_ref[0] = x + _dot(act, wout_ref[...]) + bout_ref[...]


def _conv_mixer(x, g, w_in, b_in, w_dw, b_dw, ln_g, ln_b, w_out, b_out):
    b, s, d = x.shape
    ts = min(TILE_CONV, s)
    row = lambda v: v.reshape(1, -1)
    const = lambda shape: pl.BlockSpec(shape, lambda i, j: (0,) * len(shape))
    return pl.pallas_call(
        _conv_kernel,
        out_shape=jax.ShapeDtypeStruct(x.shape, F32),
        grid=(b, s // ts),
        in_specs=[
            pl.BlockSpec((1, ts, d), lambda i, j: (i, j, 0)),
            const((1, d)), const((d, 2 * d)), const((1, 2 * d)), const((CONV_KERNEL * SUBLANES, d)), const((1, d)),
            const((1, d)), const((1, d)), const((d, d)), const((1, d)),
        ],
        out_specs=pl.BlockSpec((1, ts, d), lambda i, j: (i, j, 0)),
        scratch_shapes=[pltpu.VMEM((SUBLANES, CONV_CARRY + ts, d), F32), pltpu.VMEM((ts, d), F32)],
        compiler_params=_cparams(("arbitrary", "arbitrary")),
        name="conv_mixer",
    )(x, row(g), w_in.astype(BF16), row(b_in), jnp.repeat(w_dw, SUBLANES, axis=0), row(b_dw), row(ln_g), row(ln_b), w_out.astype(BF16),
      row(b_out))


def _gla_levels(ts):
    sizes = [GLA_LEAF]
    while sizes[-1] < ts:
        sizes.append(sizes[-1] * 2)
    return sizes


def _gla_kernel(x_ref, g_ref, wq_ref, wk_ref, wv_ref, wa_ref, wr_ref, wa2_ref, ba_ref, ng_ref, wo_ref,
                o_ref, state_ref):
    ts, d = x_ref.shape[1], x_ref.shape[2]
    dk = wq_ref.shape[1]
    dkh = dk // GLA_HEADS
    dvh = d // GLA_HEADS

    @pl.when(pl.program_id(1) == 0)
    def _():
        state_ref[...] = jnp.zeros(state_ref.shape, F32)

    x = x_ref[0]
    h = _rms(x, g_ref[...]).astype(BF16)
    q = _dot(h, wq_ref[...]) * (dkh ** -0.5)
    k = _dot(h, wk_ref[...])
    v = _dot(h, wv_ref[...]).astype(BF16)
    r = _dot(h, wr_ref[...])
    a_hi, a_lo = _split_bf16(_dot(h, wa_ref[...]))
    w2_hi, w2_lo = wa2_ref[0], wa2_ref[1]
    z = _dot(a_hi, w2_hi) + _dot(a_lo, w2_hi) + _dot(a_hi, w2_lo) + ba_ref[...]
    log_a = -(jnp.maximum(-z, 0.0) + jnp.log(1.0 + jnp.exp(-jnp.abs(z)))) * (1.0 / GLA_TAU)

    row = lax.broadcasted_iota(I32, (ts, ts), 0)
    col = lax.broadcasted_iota(I32, (ts, ts), 1)
    tri = jnp.where(col <= row, 1.0, 0.0).astype(BF16)
    la_hi, la_lo = _split_bf16(log_a)
    bcum = _dot(tri, la_hi) + _dot(tri, la_lo)
    b_last = bcum[ts - 1:ts, :]

    q_in = (q * jnp.exp(bcum)).astype(BF16)
    k_out = (k * jnp.exp(b_last - bcum)).astype(BF16)

    sizes = _gla_levels(ts)
    scores = [None] * GLA_HEADS
    for lvl, size in enumerate(sizes):
        half = size // 2
        same_block = (row & -size) == (col & -size)
        if lvl == 0:
            pair = same_block & (col <= row)
            q_ok = k_ok = None
        else:
            pair = same_block & ((row & (size - 1)) >= half) & ((col & (size - 1)) < half)
            pos = lax.broadcasted_iota(I32, (ts, dk), 0) & (size - 1)
            q_ok = pos >= half
            k_ok = pos < half
        ref = jnp.concatenate(
            [jnp.broadcast_to(bcum[r0 + half:r0 + half + 1, :], (size, dk)) for r0 in range(0, ts, size)], axis=0)
        ql = q * jnp.exp(bcum - ref)
        kl = k * jnp.exp(ref - bcum)
        if q_ok is not None:
            ql = jnp.where(q_ok, ql, 0.0)
            kl = jnp.where(k_ok, kl, 0.0)
        ql = ql.astype(BF16)
        kl = kl.astype(BF16)
        for hd in range(GLA_HEADS):
            c0 = hd * dkh
            a = _dot_nt(ql[:, c0:c0 + dkh], kl[:, c0:c0 + dkh])
            a = jnp.where(pair, a, 0.0)
            scores[hd] = a if scores[hd] is None else scores[hd] + a

    outs = []
    for hd in range(GLA_HEADS):
        c0 = hd * dkh
        v_h = v[:, hd * dvh:(hd + 1) * dvh]
        st = state_ref[hd]
        o_h = _dot(scores[hd].astype(BF16), v_h) + _dot_nt(q_in[:, c0:c0 + dkh], st.astype(BF16))
        decay = jnp.exp(b_last[:, c0:c0 + dkh])
        state_ref[hd] = st * decay + _dot_tn(v_h, k_out[:, c0:c0 + dkh])
        o_h = o_h * lax.rsqrt(jnp.mean(o_h * o_h, axis=-1, keepdims=True) + EPS) * ng_ref[...]
        outs.append(o_h)
    o = jnp.concatenate(outs, axis=1) * (r * _sigmoid(r))
    o_ref[0] = x + _dot(o.astype(BF16), wo_ref[...])


def _gla_mixer(x, g, w_in, w_a2, b_a, norm_g, w_o):
    b, s, d = x.shape
    dk = w_a2.shape[1]
    ts = min(TILE_GLA, s)
    row = lambda v: v.reshape(1, -1)
    const = lambda shape: pl.BlockSpec(shape, lambda i, j: (0,) * len(shape))
    wq = w_in[:, :dk].astype(BF16)
    wk = w_in[:, dk:2 * dk].astype(BF16)
    wv = w_in[:, 2 * dk:2 * dk + d].astype(BF16)
    wa = jnp.pad(w_in[:, 2 * dk + d:2 * dk + d + GLA_RANK], ((0, 0), (0, GLA_RANK_PAD - GLA_RANK))).astype(BF16)
    wr = w_in[:, 2 * dk + d + GLA_RANK:].astype(BF16)
    wa2 = jnp.pad(w_a2, ((0, GLA_RANK_PAD - GLA_RANK), (0, 0)))
    wa2_hi = wa2.astype(BF16)
    wa2_lo = (wa2 - wa2_hi.astype(F32)).astype(BF16)
    wa2_split = jnp.stack([wa2_hi, wa2_lo])
    dvh = d // GLA_HEADS
    return pl.pallas_call(
        _gla_kernel,
        out_shape=jax.ShapeDtypeStruct(x.shape, F32),
        grid=(b, s // ts),
        in_specs=[
            pl.BlockSpec((1, ts, d), lambda i, j: (i, j, 0)),
            const((1, d)), const((d, dk)), const((d, dk)), const((d, d)), const((d, GLA_RANK_PAD)),
            const((d, d)), const((2, GLA_RANK_PAD, dk)), const((1, dk)), const((1, dvh)), const((d, d)),
        ],
        out_specs=pl.BlockSpec((1, ts, d), lambda i, j: (i, j, 0)),
        scratch_shapes=[pltpu.VMEM((GLA_HEADS, dvh, dk // GLA_HEADS), F32)],
        compiler_params=_cparams(("arbitrary", "arbitrary")),
        name="gla_mixer",
    )(x, row(g), wq, wk, wv, wa, wr, wa2_split, row(b_a), row(norm_g), w_o.astype(BF16))


def _xattn_router_kernel(x_ref, gx_ref, wq_ref, k_ref, v_ref, wo_ref, gf_ref, wr_ref, br_ref, upper_ref,
                         x_out_ref, h_out_ref, route_ref, cnt_ref, carry_ref):
    ts, d = x_ref.shape[1], x_ref.shape[2]
    hd = d // XATTN_HEADS
    first = (pl.program_id(0) == 0) & (pl.program_id(1) == 0)

    @pl.when(first)
    def _():
        carry_ref[...] = jnp.zeros(carry_ref.shape, F32)

    x = x_ref[0]
    h = _rms(x, gx_ref[...]).astype(BF16)
    q = (_dot(h, wq_ref[...]) * (hd ** -0.5)).astype(BF16)
    k = k_ref[0, 0]
    v = v_ref[0, 0]
    outs = []
    for a in range(XATTN_HEADS):
        sl = slice(a * hd, (a + 1) * hd)
        s = _dot_nt(q[:, sl], k[:, sl])
        p = jnp.exp(s - jnp.max(s, axis=-1, keepdims=True))
        o = _dot(p.astype(BF16), v[:, sl]) / jnp.sum(p, axis=-1, keepdims=True)
        outs.append(o)
    att = jnp.concatenate(outs, axis=1).astype(BF16)
    x2 = x + _dot(att, wo_ref[...])
    x_out_ref[0] = x2

    hf = _rms(x2, gf_ref[...])
    h_out_ref[...] = _pack_bf16_pairs(hf)

    h_hi, h_lo = _split_bf16(hf)
    w_hi, w_lo = wr_ref[0], wr_ref[1]
    logits = _dot_nt(w_hi, h_hi) + _dot_nt(w_hi, h_lo) + _dot_nt(w_lo, h_hi) + br_ref[...]
    gl = logits[N_EXPERTS:N_EXPERTS + N_GROUPS, :]
    gi = lax.broadcasted_iota(I32, gl.shape, 0).astype(F32)
    gmax = jnp.max(gl, axis=0, keepdims=True)
    g_sel = jnp.min(jnp.where(gl == gmax, gi, float(N_GROUPS)), axis=0, keepdims=True)
    pg_sel = 1.0 / jnp.sum(jnp.exp(gl - gmax), axis=0, keepdims=True)

    el = jnp.zeros((EXPERTS_PER_GROUP, ts), F32)
    for gidx in range(N_GROUPS):
        lo = gidx * EXPERTS_PER_GROUP
        el = jnp.where(g_sel == float(gidx), logits[lo:lo + EXPERTS_PER_GROUP, :], el)
    ei = lax.broadcasted_iota(I32, el.shape, 0).astype(F32)
    m1 = jnp.max(el, axis=0, keepdims=True)
    i1 = jnp.min(jnp.where(el == m1, ei, float(EXPERTS_PER_GROUP)), axis=0, keepdims=True)
    rest = jnp.where(ei == i1, -jnp.inf, el)
    m2 = jnp.max(rest, axis=0, keepdims=True)
    i2 = jnp.min(jnp.where(rest == m2, ei, float(EXPERTS_PER_GROUP)), axis=0, keepdims=True)
    ratio = jnp.exp(m2 - m1)
    gate1 = pg_sel / (1.0 + ratio)
    gate2 = pg_sel * ratio / (1.0 + ratio)
    e1 = g_sel * float(EXPERTS_PER_GROUP) + i1
    e2 = g_sel * float(EXPERTS_PER_GROUP) + i2

    xi = lax.broadcasted_iota(I32, (N_EXPERTS, ts), 0).astype(F32)
    oh1 = jnp.where(xi == e1, 1.0, 0.0)
    oh2 = jnp.where(xi == e2, 1.0, 0.0)
    oh = oh1 + oh2
    before = _dot(oh.astype(BF16), upper_ref[...]) + carry_ref[...]
    rank1 = jnp.sum(oh1 * before, axis=0, keepdims=True)
    rank2 = jnp.sum(oh2 * before, axis=0, keepdims=True)
    carry_ref[...] = carry_ref[...] + jnp.sum(oh, axis=1, keepdims=True)
    cnt_ref[...] = jnp.broadcast_to(carry_ref[...], cnt_ref.shape)

    route_ref[0:1, :] = e1.astype(I32)
    route_ref[1:2, :] = e2.astype(I32)
    route_ref[2:3, :] = rank1.astype(I32)
    route_ref[3:4, :] = rank2.astype(I32)
    route_ref[4:5, :] = lax.bitcast_convert_type(gate1, I32)
    route_ref[5:6, :] = lax.bitcast_convert_type(gate2, I32)
    route_ref[6:8, :] = jnp.zeros((2, ts), I32)


def _xattn_router(x, g_x, w_q, k_mem, v_mem, layer, w_o, g_f, w_grp, b_grp, w_exp, b_exp):
    b, s, d = x.shape
    nm = k_mem.shape[2]
    t = b * s
    ts = min(TILE_XATTN, s)
    nj = s // ts
    row = lambda v: v.reshape(1, -1)
    const = lambda shape: pl.BlockSpec(shape, lambda i, j: (0,) * len(shape))
    pad = ROUTER_ROWS - N_GROUPS - N_EXPERTS
    w_r = jnp.pad(jnp.concatenate([w_exp, w_grp], axis=1).T, ((0, pad), (0, 0)))
    w_r_hi = w_r.astype(BF16)
    w_r_split = jnp.stack([w_r_hi, (w_r - w_r_hi.astype(F32)).astype(BF16)])
    b_r = jnp.pad(jnp.concatenate([b_exp, b_grp]), (0, pad)).reshape(ROUTER_ROWS, 1)
    ti = jnp.arange(ts)
    upper = (ti[:, None] < ti[None, :]).astype(BF16)
    return pl.pallas_call(
        _xattn_router_kernel,
        out_shape=(
            jax.ShapeDtypeStruct(x.shape, F32),
            jax.ShapeDtypeStruct((t, d // 2), U32),
            jax.ShapeDtypeStruct((8, t), I32),
            jax.ShapeDtypeStruct((N_EXPERTS, 128), F32),
        ),
        grid=(b, nj),
        in_specs=[
            pl.BlockSpec((1, ts, d), lambda i, j: (i, j, 0)),
            const((1, d)), const((d, d)),
            pl.BlockSpec((1, 1, nm, d), lambda i, j: (layer, i, 0, 0)),
            pl.BlockSpec((1, 1, nm, d), lambda i, j: (layer, i, 0, 0)),
            const((d, d)), const((1, d)), const((2, ROUTER_ROWS, d)), const((ROUTER_ROWS, 1)), const((ts, ts)),
        ],
        out_specs=(
            pl.BlockSpec((1, ts, d), lambda i, j: (i, j, 0)),
            pl.BlockSpec((ts, d // 2), lambda i, j: (i * nj + j, 0)),
            pl.BlockSpec((8, ts), lambda i, j: (0, i * nj + j)),
            pl.BlockSpec((N_EXPERTS, 128), lambda i, j: (0, 0)),
        ),
        scratch_shapes=[pltpu.VMEM((N_EXPERTS, 1), F32)],
        compiler_params=_cparams(("arbitrary", "arbitrary")),
        name="xattn_router",
    )(x, row(g_x), w_q.astype(BF16), k_mem, v_mem, w_o.astype(BF16), row(g_f), w_r_split, b_r, upper)


def _sc_mesh():
    return plsc.VectorSubcoreMesh(core_axis_name="c", subcore_axis_name="s",
                                  num_cores=SC_CORES, num_subcores=SC_SUBCORES)


def _sc_worker():
    return lax.axis_index("s") * SC_CORES + lax.axis_index("c")


def _dispatch(h_packed, dest, n_rows):
    t, w = h_packed.shape
    per_worker = t // SC_WORKERS
    n_chunks = per_worker // SC_CHUNK
    dest = dest.reshape(TOP_K, SC_WORKERS * n_chunks, SC_CHUNK)

    @functools.partial(
        pl.kernel, mesh=_sc_mesh(),
        out_type=jax.ShapeDtypeStruct((n_rows, w), U32),
        scratch_types=[pltpu.VMEM((n_chunks, SC_CHUNK), I32), pltpu.VMEM((n_chunks, SC_CHUNK), I32),
                       pltpu.VMEM((SC_CHUNK, w), U32)],
        name="moe_dispatch_sc",
    )
    def run(h_hbm, d0_hbm, d1_hbm, xbuf_hbm, idx0_v, idx1_v, rows_v):
        wid = _sc_worker()
        pltpu.sync_copy(d0_hbm.at[pl.ds(wid * n_chunks, n_chunks)], idx0_v)
        pltpu.sync_copy(d1_hbm.at[pl.ds(wid * n_chunks, n_chunks)], idx1_v)

        @pl.loop(0, n_chunks)
        def _(i):
            pltpu.sync_copy(h_hbm.at[pl.ds(wid * per_worker + i * SC_CHUNK, SC_CHUNK)], rows_v)
            pltpu.sync_copy(rows_v, xbuf_hbm.at[idx0_v.at[i]])
            pltpu.sync_copy(rows_v, xbuf_hbm.at[idx1_v.at[i]])

    return run(h_packed, dest[0], dest[1])


def _gather_pairs(y_buf, dest):
    t = dest.shape[1]
    w = y_buf.shape[1]
    per_worker = t // SC_WORKERS
    n_chunks = per_worker // SC_CHUNK
    dest = dest.reshape(TOP_K, SC_WORKERS * n_chunks, SC_CHUNK)
    out = jax.ShapeDtypeStruct((t, w), U32)

    @functools.partial(
        pl.kernel, mesh=_sc_mesh(),
        out_type=(out, out),
        scratch_types=[pltpu.VMEM((n_chunks, SC_CHUNK), I32), pltpu.VMEM((n_chunks, SC_CHUNK), I32),
                       pltpu.VMEM((SC_CHUNK, w), U32)],
        name="moe_gather_sc",
    )
    def run(y_hbm, d0_hbm, d1_hbm, y0_hbm, y1_hbm, idx0_v, idx1_v, rows_v):
        wid = _sc_worker()
        pltpu.sync_copy(d0_hbm.at[pl.ds(wid * n_chunks, n_chunks)], idx0_v)
        pltpu.sync_copy(d1_hbm.at[pl.ds(wid * n_chunks, n_chunks)], idx1_v)

        @pl.loop(0, n_chunks)
        def _(i):
            rows = pl.ds(wid * per_worker + i * SC_CHUNK, SC_CHUNK)
            pltpu.sync_copy(y_hbm.at[idx0_v.at[i]], rows_v)
            pltpu.sync_copy(rows_v, y0_hbm.at[rows])
            pltpu.sync_copy(y_hbm.at[idx1_v.at[i]], rows_v)
            pltpu.sync_copy(rows_v, y1_hbm.at[rows])

    return run(y_buf, dest[0], dest[1])


def _expert_kernel(be_ref, bf_ref, bv_ref, x_ref, wg_ref, wu_ref, wd_ref, y_ref, wg_bf, wu_bf, wd_bf):
    del be_ref
    blk = pl.program_id(0)
    valid = bv_ref[blk]

    @pl.when(bf_ref[blk] == 1)
    def _():
        wg_bf[...] = wg_ref[0, 0].astype(BF16)
        wu_bf[...] = wu_ref[0, 0].astype(BF16)
        wd_bf[...] = wd_ref[0, 0].astype(BF16)

    @pl.when(valid > 0)
    def _():
        live = lax.broadcasted_iota(I32, x_ref.shape, 0) < valid
        xb = _unpack_bf16_pairs(jnp.where(live, x_ref[...], jnp.uint32(0))).astype(BF16)
        gt = _dot(xb, wg_bf[...])
        up = _dot(xb, wu_bf[...])
        act = (gt * _sigmoid(gt) * up).astype(BF16)
        y_ref[...] = _pack_bf16_pairs(_dot(act, wd_bf[...]))

    @pl.when(valid <= 0)
    def _():
        y_ref[...] = jnp.zeros(y_ref.shape, U32)


def _experts(x_buf, block_expert, block_first, block_valid, w_gate, w_up, w_down, layer):
    n_rows, w = x_buf.shape
    d, de = w_gate.shape[2], w_gate.shape[3]
    bm = MOE_BLOCK_ROWS
    grid_spec = pltpu.PrefetchScalarGridSpec(
        num_scalar_prefetch=3,
        grid=(n_rows // bm,),
        in_specs=[
            pl.BlockSpec((bm, w), lambda i, be, bf, bv: (i, 0)),
            pl.BlockSpec((1, 1, d, de), lambda i, be, bf, bv: (layer, be[i], 0, 0)),
            pl.BlockSpec((1, 1, d, de), lambda i, be, bf, bv: (layer, be[i], 0, 0)),
            pl.BlockSpec((1, 1, de, d), lambda i, be, bf, bv: (layer, be[i], 0, 0)),
        ],
        out_specs=pl.BlockSpec((bm, w), lambda i, be, bf, bv: (i, 0)),
        scratch_shapes=[pltpu.VMEM((d, de), BF16), pltpu.VMEM((d, de), BF16), pltpu.VMEM((de, d), BF16)],
    )
    return pl.pallas_call(
        _expert_kernel,
        out_shape=jax.ShapeDtypeStruct((n_rows, w), U32),
        grid_spec=grid_spec,
        compiler_params=_cparams(("arbitrary",)),
        name="moe_experts",
    )(block_expert, block_first, block_valid, x_buf, w_gate, w_up, w_down)


def _combine_kernel(x_ref, gate_ref, y0_ref, y1_ref, gfin_ref, o_ref, *, final_norm):
    gates = gate_ref[...]
    y = _unpack_bf16_pairs(y0_ref[...]) * gates[:, 0:1] + _unpack_bf16_pairs(y1_ref[...]) * gates[:, 1:2]
    out = x_ref[...] + y
    if final_norm:
        out = _rms(out, gfin_ref[...])
    o_ref[...] = out


def _combine(x2, y0, y1, gates, g_final, final_norm):
    t, d = x2.shape
    w = y0.shape[1]
    ts = min(TILE_COMBINE, t)
    return pl.pallas_call(
        functools.partial(_combine_kernel, final_norm=final_norm),
        out_shape=jax.ShapeDtypeStruct((t, d), F32),
        grid=(t // ts,),
        in_specs=[
            pl.BlockSpec((ts, d), lambda i: (i, 0)),
            pl.BlockSpec((ts, TOP_K), lambda i: (i, 0)),
            pl.BlockSpec((ts, w), lambda i: (i, 0)),
            pl.BlockSpec((ts, w), lambda i: (i, 0)),
            pl.BlockSpec((1, d), lambda i: (0, 0)),
        ],
        out_specs=pl.BlockSpec((ts, d), lambda i: (i, 0)),
        compiler_params=_cparams(("arbitrary",)),
        name="moe_combine",
    )(x2, gates, y0, y1, g_final.reshape(1, d))


def _moe_layout(route, counts):
    bm = MOE_BLOCK_ROWS
    t = route.shape[1]
    n_blocks = (t * TOP_K) // bm + N_EXPERTS
    cnt = counts[:, 0].astype(I32)
    padded = (cnt + bm - 1) // bm * bm
    pad_ends = jnp.cumsum(padded)
    pad_off = pad_ends - padded
    experts = jnp.arange(N_EXPERTS, dtype=I32)
    hit = route[0:TOP_K, :, None] == experts
    dest = jnp.sum(jnp.where(hit, pad_off, 0), axis=-1) + route[TOP_K:2 * TOP_K]
    gates = lax.bitcast_convert_type(route[2 * TOP_K:3 * TOP_K], F32).T
    starts = jnp.arange(n_blocks, dtype=I32) * bm
    block_expert = jnp.minimum(jnp.sum((pad_ends[None, :] <= starts[:, None]).astype(I32), axis=1),
                               N_EXPERTS - 1)
    block_first = jnp.concatenate([jnp.ones((1,), I32), (block_expert[1:] != block_expert[:-1]).astype(I32)])
    own = block_expert[:, None] == experts
    block_valid = jnp.clip(jnp.sum(jnp.where(own, cnt + pad_off, 0), axis=1) - starts, 0, bm)
    block_valid = jnp.where(starts < pad_ends[-1], block_valid, 0).astype(I32)
    return dest, gates, block_expert, block_first, block_valid, n_blocks * bm


def kernel(x, mem, norm_mix, norm_xattn, norm_ffn, norm_mem, norm_final, conv_w_in, conv_b_in, conv_w_dw,
           conv_b_dw, conv_ln_g, conv_ln_b, conv_w_out, conv_b_out, gla_w_in, gla_w_a2, gla_b_a, gla_norm_g,
           gla_w_o, xa_w_q, xa_w_kv, xa_w_o, moe_w_grp, moe_b_grp, moe_w_exp, moe_b_exp, moe_w_gate, moe_w_up,
           moe_w_down):
    b, s, d = x.shape
    depth = norm_mix.shape[0]
    k_mem, v_mem = _mem_kv(mem, norm_mem, xa_w_kv)
    for i in range(depth):
        j = i // 2
        if i % 2 == 0:
            x = _conv_mixer(x, norm_mix[i], conv_w_in[j], conv_b_in[j], conv_w_dw[j], conv_b_dw[j],
                            conv_ln_g[j], conv_ln_b[j], conv_w_out[j], conv_b_out[j])
        else:
            x = _gla_mixer(x, norm_mix[i], gla_w_in[j], gla_w_a2[j], gla_b_a[j], gla_norm_g[j], gla_w_o[j])
        x2, h_packed, route, counts = _xattn_router(
            x, norm_xattn[i], xa_w_q[i], k_mem, v_mem, i, xa_w_o[i], norm_ffn[i],
            moe_w_grp[i], moe_b_grp[i], moe_w_exp[i], moe_b_exp[i])
        dest, gates, block_expert, block_first, block_valid, n_rows = _moe_layout(route, counts)
        x_buf = _dispatch(h_packed, dest, n_rows)
        y_buf = _experts(x_buf, block_expert, block_first, block_valid, moe_w_gate, moe_w_up, moe_w_down, i)
        y0, y1 = _gather_pairs(y_buf, dest)
        x = _combine(x2.reshape(b * s, d), y0, y1, gates, norm_final, i == depth - 1).reshape(b, s, d)
    return x
```

```python
import functools

import jax
import jax.numpy as jnp
from jax import lax
from jax.experimental import pallas as pl
from jax.experimental.pallas import tpu as pltpu
from jax.experimental.pallas import tpu_sc as plsc

F32 = jnp.float32
BF16 = jnp.bfloat16
I32 = jnp.int32
U32 = jnp.uint32

EPS = 1e-6
CONV_KERNEL = 31
CONV_CARRY = 32
CONV_ROWS = 64
CONV_COLS = 256
SUBLANES = 8
LANES = 128
GLA_HEADS = 4
GLA_RANK = 16
GLA_RANK_PAD = 128
GLA_TAU = 16.0
GLA_LEAF = 32
XATTN_HEADS = 4
N_GROUPS = 4
EXPERTS_PER_GROUP = 8
N_EXPERTS = N_GROUPS * EXPERTS_PER_GROUP
ROUTER_ROWS = 40
TOP_K = 2

TILE_CONV = 256
TILE_GLA = 256
TILE_XATTN = 512
TILE_COMBINE = 512
SC_CORES = 2
SC_SUBCORES = 16
SC_WORKERS = SC_CORES * SC_SUBCORES
SC_CHUNK = 64
MOE_BLOCK_ROWS = 512
MOE_SUB_ROWS = 512
VMEM_LIMIT = 56 * 1024 * 1024


def _cparams(sem):
    return pltpu.CompilerParams(dimension_semantics=sem, vmem_limit_bytes=VMEM_LIMIT)


def _rms(x, g):
    return x * lax.rsqrt(jnp.mean(x * x, axis=-1, keepdims=True) + EPS) * g


def _sigmoid(x):
    return 0.5 * jnp.tanh(0.5 * x) + 0.5


def _split_bf16(x):
    hi = x.astype(BF16)
    lo = (x - hi.astype(F32)).astype(BF16)
    return hi, lo


def _dot(a, b):
    return jnp.dot(a, b, preferred_element_type=F32)


def _dot_nt(a, b):
    return lax.dot_general(a, b, (((1,), (1,)), ((), ())), preferred_element_type=F32)


def _dot_tn(a, b):
    return lax.dot_general(a, b, (((0,), (0,)), ((), ())), preferred_element_type=F32)


def _pack_bf16_pairs(x):
    w = x.shape[1] // 2
    hi = lax.bitcast_convert_type(x[:, :w].astype(BF16).astype(F32), U32)
    lo = lax.bitcast_convert_type(x[:, w:].astype(BF16).astype(F32), U32)
    return hi | (lo >> 16)


def _unpack_bf16_pairs(p):
    hi = lax.bitcast_convert_type(p & jnp.uint32(0xFFFF0000), F32)
    lo = lax.bitcast_convert_type(p << 16, F32)
    return jnp.concatenate([hi, lo], axis=1)


def _memkv_kernel(mem_ref, g_ref, w_ref, k_ref, v_ref):
    d = mem_ref.shape[-1]
    mn = _rms(mem_ref[0], g_ref[...]).astype(BF16)
    kv = _dot(mn, w_ref[0])
    k_ref[0, 0] = kv[:, :d].astype(BF16)
    v_ref[0, 0] = kv[:, d:].astype(BF16)


def _mem_kv(mem, norm_mem, w_kv):
    b, nm, d = mem.shape
    depth = w_kv.shape[0]
    out = jax.ShapeDtypeStruct((depth, b, nm, d), BF16)
    return pl.pallas_call(
        _memkv_kernel,
        out_shape=(out, out),
        grid=(depth, b),
        in_specs=[
            pl.BlockSpec((1, nm, d), lambda l, i: (i, 0, 0)),
            pl.BlockSpec((1, d), lambda l, i: (0, 0)),
            pl.BlockSpec((1, d, 2 * d), lambda l, i: (l, 0, 0)),
        ],
        out_specs=(
            pl.BlockSpec((1, 1, nm, d), lambda l, i: (l, i, 0, 0)),
            pl.BlockSpec((1, 1, nm, d), lambda l, i: (l, i, 0, 0)),
        ),
        compiler_params=_cparams(("arbitrary", "arbitrary")),
        name="mem_kv",
    )(mem, norm_mem.reshape(1, d), w_kv.astype(BF16))


def _conv_kernel(x_ref, g_ref, win_ref, bin_ref, wdw_ref, bdw_ref, lng_ref, lnb_ref, wout_ref, bout_ref,
                 o_ref, ext_ref, conv_ref):
    ts, d = x_ref.shape[1], x_ref.shape[2]

    @pl.when(pl.program_id(1) == 0)
    def _():
        ext_ref[...] = jnp.zeros(ext_ref.shape, F32)

    x = x_ref[0]
    h = _rms(x, g_ref[...]).astype(BF16)
    u = _dot(h, win_ref[...]) + bin_ref[...]
    glu = u[:, :d] * _sigmoid(u[:, d:])
    for b in range(SUBLANES):
        ext_ref[b, CONV_CARRY - b:CONV_CARRY - b + ts, :] = glu

    first = CONV_CARRY - (CONV_KERNEL - 1)

    def chunk(i, carry):
        r0 = pl.multiple_of(i * CONV_ROWS, CONV_ROWS)
        for c0 in range(0, d, CONV_COLS):
            cols = slice(c0, c0 + CONV_COLS)
            acc = [jnp.zeros((SUBLANES, CONV_COLS), F32) for _ in range(CONV_ROWS // SUBLANES)]
            for k in range(CONV_KERNEL):
                b = (first + k) % SUBLANES
                wk = wdw_ref[k * SUBLANES:(k + 1) * SUBLANES, cols]
                for j in range(CONV_ROWS // SUBLANES):
                    rows = pl.ds(r0 + (first + k - b) + j * SUBLANES, SUBLANES)
                    acc[j] = acc[j] + ext_ref[b, rows, cols] * wk
            for j in range(CONV_ROWS // SUBLANES):
                conv_ref[pl.ds(r0 + j * SUBLANES, SUBLANES), cols] = acc[j]
        return carry

    lax.fori_loop(0, ts // CONV_ROWS, chunk, 0)
    for b in range(SUBLANES):
        ext_ref[b, 0:CONV_CARRY, :] = ext_ref[b, ts:ts + CONV_CARRY, :]

    c = conv_ref[...] + bdw_ref[...]
    mu = jnp.mean(c, axis=-1, keepdims=True)
    cc = c - mu
    var = jnp.mean(cc * cc, axis=-1, keepdims=True)
    un = cc * lax.rsqrt(var + EPS) * lng_ref[...] + lnb_ref[...]
    act = (un * _sigmoid(un)).astype(BF16)
    o_ref[0] = x + _dot(act, wout_ref[...]) + bout_ref[...]


def _conv_mixer(x, g, w_in, b_in, w_dw, b_dw, ln_g, ln_b, w_out, b_out):
    b, s, d = x.shape
    ts = min(TILE_CONV, s)
    row = lambda v: v.reshape(1, -1)
    const = lambda shape: pl.BlockSpec(shape, lambda i, j: (0,) * len(shape))
    return pl.pallas_call(
        _conv_kernel,
        out_shape=jax.ShapeDtypeStruct(x.shape, F32),
        grid=(b, s // ts),
        in_specs=[
            pl.BlockSpec((1, ts, d), lambda i, j: (i, j, 0)),
            const((1, d)), const((d, 2 * d)), const((1, 2 * d)), const((CONV_KERNEL * SUBLANES, d)), const((1, d)),
            const((1, d)), const((1, d)), const((d, d)), const((1, d)),
        ],
        out_specs=pl.BlockSpec((1, ts, d), lambda i, j: (i, j, 0)),
        scratch_shapes=[pltpu.VMEM((SUBLANES, CONV_CARRY + ts, d), F32), pltpu.VMEM((ts, d), F32)],
        compiler_params=_cparams(("arbitrary", "arbitrary")),
        name="conv_mixer",
    )(x, row(g), w_in.astype(BF16), row(b_in), jnp.repeat(w_dw, SUBLANES, axis=0), row(b_dw), row(ln_g), row(ln_b), w_out.astype(BF16),
      row(b_out))


def _gla_levels(ts):
    sizes = [GLA_LEAF]
    while sizes[-1] < ts:
        sizes.append(sizes[-1] * 2)
    return sizes


def _moe_residual(x, gates, y0, y1):
    return x + _unpack_bf16_pairs(y0) * gates[:, 0:1] + _unpack_bf16_pairs(y1) * gates[:, 1:2]


def _gla_kernel(x_ref, gate_ref, y0_ref, y1_ref, g_ref, wq_ref, wk_ref, wv_ref, wa_ref, wr_ref, wa2_ref, ba_ref,
                ng_ref, wo_ref, o_ref, state_ref):
    ts, d = x_ref.shape[1], x_ref.shape[2]
    dk = wq_ref.shape[1]
    dkh = dk // GLA_HEADS
    dvh = d // GLA_HEADS

    @pl.when(pl.program_id(1) == 0)
    def _():
        state_ref[...] = jnp.zeros(state_ref.shape, F32)

    x = _moe_residual(x_ref[0], gate_ref[...], y0_ref[...], y1_ref[...])
    h = _rms(x, g_ref[...]).astype(BF16)
    q = _dot(h, wq_ref[...]) * (dkh ** -0.5)
    k = _dot(h, wk_ref[...])
    v = _dot(h, wv_ref[...]).astype(BF16)
    r = _dot(h, wr_ref[...])
    a_hi, a_lo = _split_bf16(_dot(h, wa_ref[...]))
    w2_hi, w2_lo = wa2_ref[0], wa2_ref[1]
    z = _dot(a_hi, w2_hi) + _dot(a_lo, w2_hi) + _dot(a_hi, w2_lo) + ba_ref[...]
    log_a = -(jnp.maximum(-z, 0.0) + jnp.log(1.0 + jnp.exp(-jnp.abs(z)))) * (1.0 / GLA_TAU)

    row = lax.broadcasted_iota(I32, (ts, ts), 0)
    col = lax.broadcasted_iota(I32, (ts, ts), 1)
    tri = jnp.where(col <= row, 1.0, 0.0).astype(BF16)
    la_hi, la_lo = _split_bf16(log_a)
    bcum = _dot(tri, la_hi) + _dot(tri, la_lo)
    b_last = bcum[ts - 1:ts, :]

    q_in = (q * jnp.exp(bcum)).astype(BF16)
    k_out = (k * jnp.exp(b_last - bcum)).astype(BF16)

    sizes = _gla_levels(ts)
    scores = [None] * GLA_HEADS
    for lvl, size in enumerate(sizes):
        half = size // 2
        same_block = (row & -size) == (col & -size)
        if lvl == 0:
            pair = same_block & (col <= row)
            q_ok = k_ok = None
        else:
            pair = same_block & ((row & (size - 1)) >= half) & ((col & (size - 1)) < half)
            pos = lax.broadcasted_iota(I32, (ts, dk), 0) & (size - 1)
            q_ok = pos >= half
            k_ok = pos < half
        ref = jnp.concatenate(
            [jnp.broadcast_to(bcum[r0 + half:r0 + half + 1, :], (size, dk)) for r0 in range(0, ts, size)], axis=0)
        ql = q * jnp.exp(bcum - ref)
        kl = k * jnp.exp(ref - bcum)
        if q_ok is not None:
            ql = jnp.where(q_ok, ql, 0.0)
            kl = jnp.where(k_ok, kl, 0.0)
        ql = ql.astype(BF16)
        kl = kl.astype(BF16)
        for hd in range(GLA_HEADS):
            c0 = hd * dkh
            a = _dot_nt(ql[:, c0:c0 + dkh], kl[:, c0:c0 + dkh])
            a = jnp.where(pair, a, 0.0)
            scores[hd] = a if scores[hd] is None else scores[hd] + a

    outs = []
    for hd in range(GLA_HEADS):
        c0 = hd * dkh
        v_h = v[:, hd * dvh:(hd + 1) * dvh]
        st = state_ref[hd]
        o_h = _dot(scores[hd].astype(BF16), v_h) + _dot_nt(q_in[:, c0:c0 + dkh], st.astype(BF16))
        decay = jnp.exp(b_last[:, c0:c0 + dkh])
        state_ref[hd] = st * decay + _dot_tn(v_h, k_out[:, c0:c0 + dkh])
        o_h = o_h * lax.rsqrt(jnp.mean(o_h * o_h, axis=-1, keepdims=True) + EPS) * ng_ref[...]
        outs.append(o_h)
    o = jnp.concatenate(outs, axis=1) * (r * _sigmoid(r))
    o_ref[0] = x + _dot(o.astype(BF16), wo_ref[...])


def _gla_mixer(x, gates, y0, y1, g, w_in, w_a2, b_a, norm_g, w_o):
    b, s, d = x.shape
    dk = w_a2.shape[1]
    ts = min(TILE_GLA, s)
    nj = s // ts
    row = lambda v: v.reshape(1, -1)
    const = lambda shape: pl.BlockSpec(shape, lambda i, j: (0,) * len(shape))
    wq = w_in[:, :dk].astype(BF16)
    wk = w_in[:, dk:2 * dk].astype(BF16)
    wv = w_in[:, 2 * dk:2 * dk + d].astype(BF16)
    wa = jnp.pad(w_in[:, 2 * dk + d:2 * dk + d + GLA_RANK], ((0, 0), (0, GLA_RANK_PAD - GLA_RANK))).astype(BF16)
    wr = w_in[:, 2 * dk + d + GLA_RANK:].astype(BF16)
    wa2 = jnp.pad(w_a2, ((0, GLA_RANK_PAD - GLA_RANK), (0, 0)))
    wa2_hi = wa2.astype(BF16)
    wa2_lo = (wa2 - wa2_hi.astype(F32)).astype(BF16)
    wa2_split = jnp.stack([wa2_hi, wa2_lo])
    dvh = d // GLA_HEADS
    return pl.pallas_call(
        _gla_kernel,
        out_shape=jax.ShapeDtypeStruct(x.shape, F32),
        grid=(b, s // ts),
        in_specs=[
            pl.BlockSpec((1, ts, d), lambda i, j: (i, j, 0)),
            pl.BlockSpec((ts, TOP_K), lambda i, j: (i * nj + j, 0)),
            pl.BlockSpec((ts, d // 2), lambda i, j: (i * nj + j, 0)),
            pl.BlockSpec((ts, d // 2), lambda i, j: (i * nj + j, 0)),
            const((1, d)), const((d, dk)), const((d, dk)), const((d, d)), const((d, GLA_RANK_PAD)),
            const((d, d)), const((2, GLA_RANK_PAD, dk)), const((1, dk)), const((1, dvh)), const((d, d)),
        ],
        out_specs=pl.BlockSpec((1, ts, d), lambda i, j: (i, j, 0)),
        scratch_shapes=[pltpu.VMEM((GLA_HEADS, dvh, dk // GLA_HEADS), F32)],
        compiler_params=_cparams(("arbitrary", "arbitrary")),
        name="gla_mixer",
    )(x, gates, y0, y1, row(g), wq, wk, wv, wa, wr, wa2_split, row(b_a), row(norm_g), w_o.astype(BF16))


def _xattn_router_kernel(x_ref, gx_ref, wq_ref, k_ref, v_ref, wo_ref, gf_ref, wr_ref, br_ref, upper_ref,
                         x_out_ref, h_out_ref, route_ref, cnt_ref, carry_ref):
    ts, d = x_ref.shape[1], x_ref.shape[2]
    hd = d // XATTN_HEADS
    first = (pl.program_id(0) == 0) & (pl.program_id(1) == 0)

    @pl.when(first)
    def _():
        carry_ref[...] = jnp.zeros(carry_ref.shape, F32)

    x = x_ref[0]
    h = _rms(x, gx_ref[...]).astype(BF16)
    q = (_dot(h, wq_ref[...]) * (hd ** -0.5)).astype(BF16)
    k = k_ref[0, 0]
    v = v_ref[0, 0]
    outs = []
    for a in range(XATTN_HEADS):
        sl = slice(a * hd, (a + 1) * hd)
        s = _dot_nt(q[:, sl], k[:, sl])
        p = jnp.exp(s - jnp.max(s, axis=-1, keepdims=True))
        o = _dot(p.astype(BF16), v[:, sl]) / jnp.sum(p, axis=-1, keepdims=True)
        outs.append(o)
    att = jnp.concatenate(outs, axis=1).astype(BF16)
    x2 = x + _dot(att, wo_ref[...])
    x_out_ref[0] = x2

    hf = _rms(x2, gf_ref[...])
    h_out_ref[...] = _pack_bf16_pairs(hf)

    h_hi, h_lo = _split_bf16(hf)
    both = _dot_nt(wr_ref[...], h_hi)
    logits = (both[0:ROUTER_ROWS, :] + both[ROUTER_ROWS:2 * ROUTER_ROWS, :]
              + _dot_nt(wr_ref[0:ROUTER_ROWS, :], h_lo) + br_ref[...])
    gl = logits[N_EXPERTS:N_EXPERTS + N_GROUPS, :]
    gi = lax.broadcasted_iota(I32, gl.shape, 0).astype(F32)
    gmax = jnp.max(gl, axis=0, keepdims=True)
    g_sel = jnp.min(jnp.where(gl == gmax, gi, float(N_GROUPS)), axis=0, keepdims=True)
    pg_sel = 1.0 / jnp.sum(jnp.exp(gl - gmax), axis=0, keepdims=True)

    el = jnp.zeros((EXPERTS_PER_GROUP, ts), F32)
    for gidx in range(N_GROUPS):
        lo = gidx * EXPERTS_PER_GROUP
        el = jnp.where(g_sel == float(gidx), logits[lo:lo + EXPERTS_PER_GROUP, :], el)
    ei = lax.broadcasted_iota(I32, el.shape, 0).astype(F32)
    m1 = jnp.max(el, axis=0, keepdims=True)
    i1 = jnp.min(jnp.where(el == m1, ei, float(EXPERTS_PER_GROUP)), axis=0, keepdims=True)
    rest = jnp.where(ei == i1, -jnp.inf, el)
    m2 = jnp.max(rest, axis=0, keepdims=True)
    i2 = jnp.min(jnp.where(rest == m2, ei, float(EXPERTS_PER_GROUP)), axis=0, keepdims=True)
    ratio = jnp.exp(m2 - m1)
    gate1 = pg_sel / (1.0 + ratio)
    gate2 = pg_sel * ratio / (1.0 + ratio)
    e1 = g_sel * float(EXPERTS_PER_GROUP) + i1
    e2 = g_sel * float(EXPERTS_PER_GROUP) + i2

    xi = lax.broadcasted_iota(I32, (N_EXPERTS, ts), 0).astype(F32)
    oh1 = jnp.where(xi == e1, 1.0, 0.0)
    oh2 = jnp.where(xi == e2, 1.0, 0.0)
    oh = oh1 + oh2
    n_blk = ts // LANES
    stacked = jnp.concatenate([oh[:, c * LANES:(c + 1) * LANES] for c in range(n_blk)], axis=0)
    within = _dot(stacked.astype(BF16), upper_ref[...])
    totals = jnp.sum(stacked, axis=1, keepdims=True)
    run = carry_ref[...]
    before = []
    for c in range(n_blk):
        before.append(within[c * N_EXPERTS:(c + 1) * N_EXPERTS, :] + run)
        run = run + totals[c * N_EXPERTS:(c + 1) * N_EXPERTS, :]
    before = jnp.concatenate(before, axis=1)
    rank1 = jnp.sum(oh1 * before, axis=0, keepdims=True)
    rank2 = jnp.sum(oh2 * before, axis=0, keepdims=True)
    carry_ref[...] = run
    cnt_ref[...] = jnp.broadcast_to(run, cnt_ref.shape)

    route_ref[0:1, :] = e1.astype(I32)
    route_ref[1:2, :] = e2.astype(I32)
    route_ref[2:3, :] = rank1.astype(I32)
    route_ref[3:4, :] = rank2.astype(I32)
    route_ref[4:5, :] = lax.bitcast_convert_type(gate1, I32)
    route_ref[5:6, :] = lax.bitcast_convert_type(gate2, I32)
    route_ref[6:8, :] = jnp.zeros((2, ts), I32)


def _xattn_router(x, g_x, w_q, k_mem, v_mem, layer, w_o, g_f, w_grp, b_grp, w_exp, b_exp):
    b, s, d = x.shape
    nm = k_mem.shape[2]
    t = b * s
    ts = min(TILE_XATTN, s)
    nj = s // ts
    row = lambda v: v.reshape(1, -1)
    const = lambda shape: pl.BlockSpec(shape, lambda i, j: (0,) * len(shape))
    pad = ROUTER_ROWS - N_GROUPS - N_EXPERTS
    w_r = jnp.pad(jnp.concatenate([w_exp, w_grp], axis=1).T, ((0, pad), (0, 0)))
    w_r_hi = w_r.astype(BF16)
    w_r_split = jnp.concatenate([w_r_hi, (w_r - w_r_hi.astype(F32)).astype(BF16)], axis=0)
    b_r = jnp.pad(jnp.concatenate([b_exp, b_grp]), (0, pad)).reshape(ROUTER_ROWS, 1)
    ti = jnp.arange(LANES)
    upper = (ti[:, None] < ti[None, :]).astype(BF16)
    return pl.pallas_call(
        _xattn_router_kernel,
        out_shape=(
            jax.ShapeDtypeStruct(x.shape, F32),
            jax.ShapeDtypeStruct((t, d // 2), U32),
            jax.ShapeDtypeStruct((8, t), I32),
            jax.ShapeDtypeStruct((N_EXPERTS, 128), F32),
        ),
        grid=(b, nj),
        in_specs=[
            pl.BlockSpec((1, ts, d), lambda i, j: (i, j, 0)),
            const((1, d)), const((d, d)),
            pl.BlockSpec((1, 1, nm, d), lambda i, j: (layer, i, 0, 0)),
            pl.BlockSpec((1, 1, nm, d), lambda i, j: (layer, i, 0, 0)),
            const((d, d)), const((1, d)), const((2 * ROUTER_ROWS, d)), const((ROUTER_ROWS, 1)), const((LANES, LANES)),
        ],
        out_specs=(
            pl.BlockSpec((1, ts, d), lambda i, j: (i, j, 0)),
            pl.BlockSpec((ts, d // 2), lambda i, j: (i * nj + j, 0)),
            pl.BlockSpec((8, ts), lambda i, j: (0, i * nj + j)),
            pl.BlockSpec((N_EXPERTS, 128), lambda i, j: (0, 0)),
        ),
        scratch_shapes=[pltpu.VMEM((N_EXPERTS, 1), F32)],
        compiler_params=_cparams(("arbitrary", "arbitrary")),
        name="xattn_router",
    )(x, row(g_x), w_q.astype(BF16), k_mem, v_mem, w_o.astype(BF16), row(g_f), w_r_split, b_r, upper)


def _sc_mesh():
    return plsc.VectorSubcoreMesh(core_axis_name="c", subcore_axis_name="s",
                                  num_cores=SC_CORES, num_subcores=SC_SUBCORES)


def _sc_worker():
    return lax.axis_index("s") * SC_CORES + lax.axis_index("c")


def _dispatch(h_packed, dest, n_rows):
    t, w = h_packed.shape
    per_worker = t // SC_WORKERS
    n_chunks = per_worker // SC_CHUNK
    dest = dest.reshape(TOP_K, SC_WORKERS * n_chunks, SC_CHUNK)

    @functools.partial(
        pl.kernel, mesh=_sc_mesh(),
        out_type=jax.ShapeDtypeStruct((n_rows, w), U32),
        scratch_types=[pltpu.VMEM((n_chunks, SC_CHUNK), I32), pltpu.VMEM((n_chunks, SC_CHUNK), I32),
                       pltpu.VMEM((SC_CHUNK, w), U32)],
        name="moe_dispatch_sc",
    )
    def run(h_hbm, d0_hbm, d1_hbm, xbuf_hbm, idx0_v, idx1_v, rows_v):
        wid = _sc_worker()
        pltpu.sync_copy(d0_hbm.at[pl.ds(wid * n_chunks, n_chunks)], idx0_v)
        pltpu.sync_copy(d1_hbm.at[pl.ds(wid * n_chunks, n_chunks)], idx1_v)

        @pl.loop(0, n_chunks)
        def _(i):
            pltpu.sync_copy(h_hbm.at[pl.ds(wid * per_worker + i * SC_CHUNK, SC_CHUNK)], rows_v)
            pltpu.sync_copy(rows_v, xbuf_hbm.at[idx0_v.at[i]])
            pltpu.sync_copy(rows_v, xbuf_hbm.at[idx1_v.at[i]])

    return run(h_packed, dest[0], dest[1])


def _gather_pairs(y_buf, dest):
    t = dest.shape[1]
    w = y_buf.shape[1]
    per_worker = t // SC_WORKERS
    n_chunks = per_worker // SC_CHUNK
    dest = dest.reshape(TOP_K, SC_WORKERS * n_chunks, SC_CHUNK)
    out = jax.ShapeDtypeStruct((t, w), U32)

    @functools.partial(
        pl.kernel, mesh=_sc_mesh(),
        out_type=(out, out),
        scratch_types=[pltpu.VMEM((n_chunks, SC_CHUNK), I32), pltpu.VMEM((n_chunks, SC_CHUNK), I32),
                       pltpu.VMEM((SC_CHUNK, w), U32)],
        name="moe_gather_sc",
    )
    def run(y_hbm, d0_hbm, d1_hbm, y0_hbm, y1_hbm, idx0_v, idx1_v, rows_v):
        wid = _sc_worker()
        pltpu.sync_copy(d0_hbm.at[pl.ds(wid * n_chunks, n_chunks)], idx0_v)
        pltpu.sync_copy(d1_hbm.at[pl.ds(wid * n_chunks, n_chunks)], idx1_v)

        @pl.loop(0, n_chunks)
        def _(i):
            rows = pl.ds(wid * per_worker + i * SC_CHUNK, SC_CHUNK)
            pltpu.sync_copy(y_hbm.at[idx0_v.at[i]], rows_v)
            pltpu.sync_copy(rows_v, y0_hbm.at[rows])
            pltpu.sync_copy(y_hbm.at[idx1_v.at[i]], rows_v)
            pltpu.sync_copy(rows_v, y1_hbm.at[rows])

    return run(y_buf, dest[0], dest[1])


def _expert_kernel(be_ref, bf_ref, bv_ref, x_ref, wg_ref, wu_ref, wd_ref, y_ref, wg_bf, wu_bf, wd_bf):
    del be_ref
    blk = pl.program_id(0)
    valid = bv_ref[blk]

    @pl.when(bf_ref[blk] == 1)
    def _():
        wg_bf[...] = wg_ref[0, 0].astype(BF16)
        wu_bf[...] = wu_ref[0, 0].astype(BF16)
        wd_bf[...] = wd_ref[0, 0].astype(BF16)

    @pl.when(valid > 0)
    def _():
        for r0 in range(0, x_ref.shape[0], MOE_SUB_ROWS):
            rows = slice(r0, r0 + MOE_SUB_ROWS)
            live = lax.broadcasted_iota(I32, (MOE_SUB_ROWS, x_ref.shape[1]), 0) < valid - r0
            xb = _unpack_bf16_pairs(jnp.where(live, x_ref[rows, :], jnp.uint32(0))).astype(BF16)
            gt = _dot(xb, wg_bf[...])
            up = _dot(xb, wu_bf[...])
            act = (gt * _sigmoid(gt) * up).astype(BF16)
            y_ref[rows, :] = _pack_bf16_pairs(_dot(act, wd_bf[...]))

    @pl.when(valid <= 0)
    def _():
        y_ref[...] = jnp.zeros(y_ref.shape, U32)


def _experts(x_buf, block_expert, block_first, block_valid, w_gate, w_up, w_down, layer):
    n_rows, w = x_buf.shape
    d, de = w_gate.shape[2], w_gate.shape[3]
    bm = MOE_BLOCK_ROWS
    grid_spec = pltpu.PrefetchScalarGridSpec(
        num_scalar_prefetch=3,
        grid=(n_rows // bm,),
        in_specs=[
            pl.BlockSpec((bm, w), lambda i, be, bf, bv: (i, 0)),
            pl.BlockSpec((1, 1, d, de), lambda i, be, bf, bv: (layer, be[i], 0, 0)),
            pl.BlockSpec((1, 1, d, de), lambda i, be, bf, bv: (layer, be[i], 0, 0)),
            pl.BlockSpec((1, 1, de, d), lambda i, be, bf, bv: (layer, be[i], 0, 0)),
        ],
        out_specs=pl.BlockSpec((bm, w), lambda i, be, bf, bv: (i, 0)),
        scratch_shapes=[pltpu.VMEM((d, de), BF16), pltpu.VMEM((d, de), BF16), pltpu.VMEM((de, d), BF16)],
    )
    return pl.pallas_call(
        _expert_kernel,
        out_shape=jax.ShapeDtypeStruct((n_rows, w), U32),
        grid_spec=grid_spec,
        compiler_params=_cparams(("arbitrary",)),
        name="moe_experts",
    )(block_expert, block_first, block_valid, x_buf, w_gate, w_up, w_down)


def _combine_kernel(x_ref, gate_ref, y0_ref, y1_ref, gfin_ref, o_ref, *, final_norm):
    out = _moe_residual(x_ref[...], gate_ref[...], y0_ref[...], y1_ref[...])
    if final_norm:
        out = _rms(out, gfin_ref[...])
    o_ref[...] = out


def _combine(x2, y0, y1, gates, g_final, final_norm):
    t, d = x2.shape
    w = y0.shape[1]
    ts = min(TILE_COMBINE, t)
    return pl.pallas_call(
        functools.partial(_combine_kernel, final_norm=final_norm),
        out_shape=jax.ShapeDtypeStruct((t, d), F32),
        grid=(t // ts,),
        in_specs=[
            pl.BlockSpec((ts, d), lambda i: (i, 0)),
            pl.BlockSpec((ts, TOP_K), lambda i: (i, 0)),
            pl.BlockSpec((ts, w), lambda i: (i, 0)),
            pl.BlockSpec((ts, w), lambda i: (i, 0)),
            pl.BlockSpec((1, d), lambda i: (0, 0)),
        ],
        out_specs=pl.BlockSpec((ts, d), lambda i: (i, 0)),
        compiler_params=_cparams(("arbitrary",)),
        name="moe_combine",
    )(x2, gates, y0, y1, g_final.reshape(1, d))


def _moe_layout(route, counts):
    bm = MOE_BLOCK_ROWS
    t = route.shape[1]
    n_blocks = (t * TOP_K) // bm + N_EXPERTS
    cnt = counts[:, 0].astype(I32)
    padded = (cnt + bm - 1) // bm * bm
    pad_ends = jnp.cumsum(padded)
    pad_off = pad_ends - padded
    experts = jnp.arange(N_EXPERTS, dtype=I32)
    hit = route[0:TOP_K, :, None] == experts
    dest = jnp.sum(jnp.where(hit, pad_off, 0), axis=-1) + route[TOP_K:2 * TOP_K]
    gates = lax.bitcast_convert_type(route[2 * TOP_K:3 * TOP_K], F32).T
    starts = jnp.arange(n_blocks, dtype=I32) * bm
    block_expert = jnp.minimum(jnp.sum((pad_ends[None, :] <= starts[:, None]).astype(I32), axis=1),
                               N_EXPERTS - 1)
    block_first = jnp.concatenate([jnp.ones((1,), I32), (block_expert[1:] != block_expert[:-1]).astype(I32)])
    own = block_expert[:, None] == experts
    block_valid = jnp.clip(jnp.sum(jnp.where(own, cnt + pad_off, 0), axis=1) - starts, 0, bm)
    block_valid = jnp.where(starts < pad_ends[-1], block_valid, 0).astype(I32)
    return dest, gates, block_expert, block_first, block_valid, n_blocks * bm


def kernel(x, mem, norm_mix, norm_xattn, norm_ffn, norm_mem, norm_final, conv_w_in, conv_b_in, conv_w_dw,
           conv_b_dw, conv_ln_g, conv_ln_b, conv_w_out, conv_b_out, gla_w_in, gla_w_a2, gla_b_a, gla_norm_g,
           gla_w_o, xa_w_q, xa_w_kv, xa_w_o, moe_w_grp, moe_b_grp, moe_w_exp, moe_b_exp, moe_w_gate, moe_w_up,
           moe_w_down):
    b, s, d = x.shape
    depth = norm_mix.shape[0]
    k_mem, v_mem = _mem_kv(mem, norm_mem, xa_w_kv)
    moe = None
    for i in range(depth):
        j = i // 2
        if i % 2 == 0:
            if moe is not None:
                x = _combine(x.reshape(b * s, d), moe[1], moe[2], moe[0], norm_final, False).reshape(b, s, d)
            x = _conv_mixer(x, norm_mix[i], conv_w_in[j], conv_b_in[j], conv_w_dw[j], conv_b_dw[j],
                            conv_ln_g[j], conv_ln_b[j], conv_w_out[j], conv_b_out[j])
        else:
            x = _gla_mixer(x, *moe, norm_mix[i], gla_w_in[j], gla_w_a2[j], gla_b_a[j], gla_norm_g[j], gla_w_o[j])
        x2, h_packed, route, counts = _xattn_router(
            x, norm_xattn[i], xa_w_q[i], k_mem, v_mem, i, xa_w_o[i], norm_ffn[i],
            moe_w_grp[i], moe_b_grp[i], moe_w_exp[i], moe_b_exp[i])
        dest, gates, block_expert, block_first, block_valid, n_rows = _moe_layout(route, counts)
        x_buf = _dispatch(h_packed, dest, n_rows)
        y_buf = _experts(x_buf, block_expert, block_first, block_valid, moe_w_gate, moe_w_up, moe_w_down, i)
        y0, y1 = _gather_pairs(y_buf, dest)
        x, moe = x2, (gates, y0, y1)
    return _combine(x.reshape(b * s, d), moe[1], moe[2], moe[0], norm_final, True).reshape(b, s, d)
```

```python
import functools

import jax
import jax.numpy as jnp
from jax import lax
from jax.experimental import pallas as pl
from jax.experimental.pallas import tpu as pltpu
from jax.experimental.pallas import tpu_sc as plsc

F32 = jnp.float32
BF16 = jnp.bfloat16
I32 = jnp.int32
U32 = jnp.uint32

EPS = 1e-6
CONV_KERNEL = 31
CONV_CARRY = 32
CONV_ROWS = 64
CONV_COLS = 256
SUBLANES = 8
LANES = 128
GLA_HEADS = 4
GLA_RANK = 16
GLA_RANK_PAD = 128
GLA_TAU = 16.0
GLA_LEAF = 32
XATTN_HEADS = 4
N_GROUPS = 4
EXPERTS_PER_GROUP = 8
N_EXPERTS = N_GROUPS * EXPERTS_PER_GROUP
ROUTER_ROWS = 40
TOP_K = 2

TILE_CONV = 256
TILE_GLA = 256
TILE_XATTN = 512
TILE_COMBINE = 512
SC_CORES = 2
SC_SUBCORES = 16
SC_WORKERS = SC_CORES * SC_SUBCORES
SC_CHUNK = 64
MOE_BLOCK_ROWS = 512
VMEM_LIMIT = 56 * 1024 * 1024


def _cparams(sem):
    return pltpu.CompilerParams(dimension_semantics=sem, vmem_limit_bytes=VMEM_LIMIT)


def _rms(x, g):
    return x * lax.rsqrt(jnp.mean(x * x, axis=-1, keepdims=True) + EPS) * g


def _sigmoid(x):
    return 0.5 * jnp.tanh(0.5 * x) + 0.5


def _split_bf16(x):
    hi = x.astype(BF16)
    lo = (x - hi.astype(F32)).astype(BF16)
    return hi, lo


def _dot(a, b):
    return jnp.dot(a, b, preferred_element_type=F32)


def _dot_nt(a, b):
    return lax.dot_general(a, b, (((1,), (1,)), ((), ())), preferred_element_type=F32)


def _dot_tn(a, b):
    return lax.dot_general(a, b, (((0,), (0,)), ((), ())), preferred_element_type=F32)


def _pack_bf16_pairs(x):
    w = x.shape[1] // 2
    hi = lax.bitcast_convert_type(x[:, :w].astype(BF16).astype(F32), U32)
    lo = lax.bitcast_convert_type(x[:, w:].astype(BF16).astype(F32), U32)
    return hi | (lo >> 16)


def _unpack_bf16_pairs(p):
    hi = lax.bitcast_convert_type(p & jnp.uint32(0xFFFF0000), F32)
    lo = lax.bitcast_convert_type(p << 16, F32)
    return jnp.concatenate([hi, lo], axis=1)


def _memkv_kernel(mem_ref, g_ref, w_ref, k_ref, v_ref):
    d = mem_ref.shape[-1]
    mn = _rms(mem_ref[0], g_ref[...]).astype(BF16)
    kv = _dot(mn, w_ref[0])
    k_ref[0, 0] = kv[:, :d].astype(BF16)
    v_ref[0, 0] = kv[:, d:].astype(BF16)


def _mem_kv(mem, norm_mem, w_kv):
    b, nm, d = mem.shape
    depth = w_kv.shape[0]
    out = jax.ShapeDtypeStruct((depth, b, nm, d), BF16)
    return pl.pallas_call(
        _memkv_kernel,
        out_shape=(out, out),
        grid=(depth, b),
        in_specs=[
            pl.BlockSpec((1, nm, d), lambda l, i: (i, 0, 0)),
            pl.BlockSpec((1, d), lambda l, i: (0, 0)),
            pl.BlockSpec((1, d, 2 * d), lambda l, i: (l, 0, 0)),
        ],
        out_specs=(
            pl.BlockSpec((1, 1, nm, d), lambda l, i: (l, i, 0, 0)),
            pl.BlockSpec((1, 1, nm, d), lambda l, i: (l, i, 0, 0)),
        ),
        compiler_params=_cparams(("arbitrary", "arbitrary")),
        name="mem_kv",
    )(mem, norm_mem.reshape(1, d), w_kv.astype(BF16))


def _conv_kernel(xa_ref, xc_ref, g_ref, win_ref, bin_ref, wdw_ref, bdw_ref, lng_ref, lnb_ref, wout_ref, bout_ref,
                 o_ref, ext0_ref, ext1_ref, act_ref, conv_ref, *, tiles_per_seq):
    j = pl.program_id(0)
    ts, d = xa_ref.shape[1], xa_ref.shape[2]
    first = CONV_CARRY - (CONV_KERNEL - 1)

    @pl.when(j == 0)
    def _():
        ext0_ref[...] = jnp.zeros(ext0_ref.shape, F32)
        ext1_ref[...] = jnp.zeros(ext1_ref.shape, F32)
        act_ref[...] = jnp.zeros(act_ref.shape, BF16)

    def step(ext_new, ext_old, parity):
        o_ref[0] = xc_ref[0] + _dot(act_ref[parity], wout_ref[...]) + bout_ref[...]

        for r0 in range(0, ts, CONV_ROWS):
            for c0 in range(0, d, CONV_COLS):
                cols = slice(c0, c0 + CONV_COLS)
                acc = [jnp.zeros((SUBLANES, CONV_COLS), F32) for _ in range(CONV_ROWS // SUBLANES)]
                for k in range(CONV_KERNEL):
                    b = (first + k) % SUBLANES
                    wk = wdw_ref[k * SUBLANES:(k + 1) * SUBLANES, cols]
                    for i in range(CONV_ROWS // SUBLANES):
                        lo = r0 + (first + k - b) + i * SUBLANES
                        acc[i] = acc[i] + ext_old[b, lo:lo + SUBLANES, cols] * wk
                for i in range(CONV_ROWS // SUBLANES):
                    conv_ref[r0 + i * SUBLANES:r0 + (i + 1) * SUBLANES, cols] = acc[i]
        c = conv_ref[...] + bdw_ref[...]
        mu = jnp.mean(c, axis=-1, keepdims=True)
        cc = c - mu
        var = jnp.mean(cc * cc, axis=-1, keepdims=True)
        un = cc * lax.rsqrt(var + EPS) * lng_ref[...] + lnb_ref[...]
        act_ref[1 - parity] = (un * _sigmoid(un)).astype(BF16)

        h = _rms(xa_ref[0], g_ref[...]).astype(BF16)
        u = _dot(h, win_ref[...]) + bin_ref[...]
        glu = u[:, :d] * _sigmoid(u[:, d:])
        seq_start = (j % tiles_per_seq) == 0
        for b in range(SUBLANES):
            ext_new[b, 0:CONV_CARRY, :] = jnp.where(seq_start, 0.0, ext_old[b, ts:ts + CONV_CARRY, :])
            ext_new[b, CONV_CARRY - b:CONV_CARRY - b + ts, :] = glu

    @pl.when(j % 2 == 0)
    def _():
        step(ext0_ref, ext1_ref, 0)

    @pl.when(j % 2 == 1)
    def _():
        step(ext1_ref, ext0_ref, 1)


def _conv_mixer(x, g, w_in, b_in, w_dw, b_dw, ln_g, ln_b, w_out, b_out):
    b, s, d = x.shape
    ts = min(TILE_CONV, s)
    ns = s // ts
    n = b * ns
    row = lambda v: v.reshape(1, -1)
    const = lambda shape: pl.BlockSpec(shape, lambda j: (0,) * len(shape))

    def tile(j, lag):
        t = jnp.clip(j - lag, 0, n - 1)
        return (t // ns, t % ns, 0)

    ext = pltpu.VMEM((SUBLANES, CONV_CARRY + ts, d), F32)
    return pl.pallas_call(
        functools.partial(_conv_kernel, tiles_per_seq=ns),
        out_shape=jax.ShapeDtypeStruct(x.shape, F32),
        grid=(n + 2,),
        in_specs=[
            pl.BlockSpec((1, ts, d), lambda j: tile(j, 0)),
            pl.BlockSpec((1, ts, d), lambda j: tile(j, 2)),
            const((1, d)), const((d, 2 * d)), const((1, 2 * d)), const((CONV_KERNEL * SUBLANES, d)), const((1, d)),
            const((1, d)), const((1, d)), const((d, d)), const((1, d)),
        ],
        out_specs=pl.BlockSpec((1, ts, d), lambda j: tile(j, 2)),
        scratch_shapes=[ext, ext, pltpu.VMEM((2, ts, d), BF16), pltpu.VMEM((ts, d), F32)],
        compiler_params=_cparams(("arbitrary",)),
        name="conv_mixer",
    )(x, x, row(g), w_in.astype(BF16), row(b_in), jnp.repeat(w_dw, SUBLANES, axis=0), row(b_dw), row(ln_g),
      row(ln_b), w_out.astype(BF16), row(b_out))


def _gla_levels(ts):
    sizes = [GLA_LEAF]
    while sizes[-1] < ts:
        sizes.append(sizes[-1] * 2)
    return sizes


def _moe_residual(x, gates, y0, y1):
    return x + _unpack_bf16_pairs(y0) * gates[:, 0:1] + _unpack_bf16_pairs(y1) * gates[:, 1:2]


def _gla_kernel(x_ref, gate_ref, y0_ref, y1_ref, g_ref, wq_ref, wk_ref, wv_ref, wa_ref, wr_ref, wa2_ref, ba_ref,
                ng_ref, wo_ref, o_ref, state_ref):
    ts, d = x_ref.shape[1], x_ref.shape[2]
    dk = wq_ref.shape[1]
    dkh = dk // GLA_HEADS
    dvh = d // GLA_HEADS

    @pl.when(pl.program_id(1) == 0)
    def _():
        state_ref[...] = jnp.zeros(state_ref.shape, F32)

    x = _moe_residual(x_ref[0], gate_ref[...], y0_ref[...], y1_ref[...])
    h = _rms(x, g_ref[...]).astype(BF16)
    q = _dot(h, wq_ref[...]) * (dkh ** -0.5)
    k = _dot(h, wk_ref[...])
    v = _dot(h, wv_ref[...]).astype(BF16)
    r = _dot(h, wr_ref[...])
    a_hi, a_lo = _split_bf16(_dot(h, wa_ref[...]))
    w2_hi, w2_lo = wa2_ref[0], wa2_ref[1]
    z = _dot(a_hi, w2_hi) + _dot(a_lo, w2_hi) + _dot(a_hi, w2_lo) + ba_ref[...]
    log_a = -(jnp.maximum(-z, 0.0) + jnp.log(1.0 + jnp.exp(-jnp.abs(z)))) * (1.0 / GLA_TAU)

    row = lax.broadcasted_iota(I32, (ts, ts), 0)
    col = lax.broadcasted_iota(I32, (ts, ts), 1)
    tri = jnp.where(col <= row, 1.0, 0.0).astype(BF16)
    la_hi, la_lo = _split_bf16(log_a)
    bcum = _dot(tri, la_hi) + _dot(tri, la_lo)
    b_last = bcum[ts - 1:ts, :]

    q_in = (q * jnp.exp(bcum)).astype(BF16)
    k_out = (k * jnp.exp(b_last - bcum)).astype(BF16)

    sizes = _gla_levels(ts)
    scores = [None] * GLA_HEADS
    for lvl, size in enumerate(sizes):
        half = size // 2
        same_block = (row & -size) == (col & -size)
        if lvl == 0:
            pair = same_block & (col <= row)
            q_ok = k_ok = None
        else:
            pair = same_block & ((row & (size - 1)) >= half) & ((col & (size - 1)) < half)
            pos = lax.broadcasted_iota(I32, (ts, dk), 0) & (size - 1)
            q_ok = pos >= half
            k_ok = pos < half
        ref = jnp.concatenate(
            [jnp.broadcast_to(bcum[r0 + half:r0 + half + 1, :], (size, dk)) for r0 in range(0, ts, size)], axis=0)
        ql = q * jnp.exp(bcum - ref)
        kl = k * jnp.exp(ref - bcum)
        if q_ok is not None:
            ql = jnp.where(q_ok, ql, 0.0)
            kl = jnp.where(k_ok, kl, 0.0)
        ql = ql.astype(BF16)
        kl = kl.astype(BF16)
        for hd in range(GLA_HEADS):
            c0 = hd * dkh
            a = _dot_nt(ql[:, c0:c0 + dkh], kl[:, c0:c0 + dkh])
            a = jnp.where(pair, a, 0.0)
            scores[hd] = a if scores[hd] is None else scores[hd] + a

    outs = []
    for hd in range(GLA_HEADS):
        c0 = hd * dkh
        v_h = v[:, hd * dvh:(hd + 1) * dvh]
        st = state_ref[hd]
        o_h = _dot(scores[hd].astype(BF16), v_h) + _dot_nt(q_in[:, c0:c0 + dkh], st.astype(BF16))
        decay = jnp.exp(b_last[:, c0:c0 + dkh])
        state_ref[hd] = st * decay + _dot_tn(v_h, k_out[:, c0:c0 + dkh])
        o_h = o_h * lax.rsqrt(jnp.mean(o_h * o_h, axis=-1, keepdims=True) + EPS) * ng_ref[...]
        outs.append(o_h)
    o = jnp.concatenate(outs, axis=1) * (r * _sigmoid(r))
    o_ref[0] = x + _dot(o.astype(BF16), wo_ref[...])


def _gla_mixer(x, gates, y0, y1, g, w_in, w_a2, b_a, norm_g, w_o):
    b, s, d = x.shape
    dk = w_a2.shape[1]
    ts = min(TILE_GLA, s)
    nj = s // ts
    row = lambda v: v.reshape(1, -1)
    const = lambda shape: pl.BlockSpec(shape, lambda i, j: (0,) * len(shape))
    wq = w_in[:, :dk].astype(BF16)
    wk = w_in[:, dk:2 * dk].astype(BF16)
    wv = w_in[:, 2 * dk:2 * dk + d].astype(BF16)
    wa = jnp.pad(w_in[:, 2 * dk + d:2 * dk + d + GLA_RANK], ((0, 0), (0, GLA_RANK_PAD - GLA_RANK))).astype(BF16)
    wr = w_in[:, 2 * dk + d + GLA_RANK:].astype(BF16)
    wa2 = jnp.pad(w_a2, ((0, GLA_RANK_PAD - GLA_RANK), (0, 0)))
    wa2_hi = wa2.astype(BF16)
    wa2_lo = (wa2 - wa2_hi.astype(F32)).astype(BF16)
    wa2_split = jnp.stack([wa2_hi, wa2_lo])
    dvh = d // GLA_HEADS
    return pl.pallas_call(
        _gla_kernel,
        out_shape=jax.ShapeDtypeStruct(x.shape, F32),
        grid=(b, s // ts),
        in_specs=[
            pl.BlockSpec((1, ts, d), lambda i, j: (i, j, 0)),
            pl.BlockSpec((ts, TOP_K), lambda i, j: (i * nj + j, 0)),
            pl.BlockSpec((ts, d // 2), lambda i, j: (i * nj + j, 0)),
            pl.BlockSpec((ts, d // 2), lambda i, j: (i * nj + j, 0)),
            const((1, d)), const((d, dk)), const((d, dk)), const((d, d)), const((d, GLA_RANK_PAD)),
            const((d, d)), const((2, GLA_RANK_PAD, dk)), const((1, dk)), const((1, dvh)), const((d, d)),
        ],
        out_specs=pl.BlockSpec((1, ts, d), lambda i, j: (i, j, 0)),
        scratch_shapes=[pltpu.VMEM((GLA_HEADS, dvh, dk // GLA_HEADS), F32)],
        compiler_params=_cparams(("arbitrary", "arbitrary")),
        name="gla_mixer",
    )(x, gates, y0, y1, row(g), wq, wk, wv, wa, wr, wa2_split, row(b_a), row(norm_g), w_o.astype(BF16))


def _xattn_router_kernel(x_ref, gx_ref, wq_ref, k_ref, v_ref, wo_ref, gf_ref, wr_ref, br_ref, upper_ref,
                         x_out_ref, h_out_ref, route_ref, cnt_ref, carry_ref):
    ts, d = x_ref.shape[1], x_ref.shape[2]
    hd = d // XATTN_HEADS
    first = (pl.program_id(0) == 0) & (pl.program_id(1) == 0)

    @pl.when(first)
    def _():
        carry_ref[...] = jnp.zeros(carry_ref.shape, F32)

    x = x_ref[0]
    h = _rms(x, gx_ref[...]).astype(BF16)
    q = (_dot(h, wq_ref[...]) * (hd ** -0.5)).astype(BF16)
    k = k_ref[0, 0]
    v = v_ref[0, 0]
    outs = []
    for a in range(XATTN_HEADS):
        sl = slice(a * hd, (a + 1) * hd)
        s = _dot_nt(q[:, sl], k[:, sl])
        p = jnp.exp(s - jnp.max(s, axis=-1, keepdims=True))
        o = _dot(p.astype(BF16), v[:, sl]) / jnp.sum(p, axis=-1, keepdims=True)
        outs.append(o)
    att = jnp.concatenate(outs, axis=1).astype(BF16)
    x2 = x + _dot(att, wo_ref[...])
    x_out_ref[0] = x2

    hf = _rms(x2, gf_ref[...])
    h_out_ref[...] = _pack_bf16_pairs(hf)

    h_hi, h_lo = _split_bf16(hf)
    both = _dot_nt(wr_ref[...], h_hi)
    logits = (both[0:ROUTER_ROWS, :] + both[ROUTER_ROWS:2 * ROUTER_ROWS, :]
              + _dot_nt(wr_ref[0:ROUTER_ROWS, :], h_lo) + br_ref[...])
    gl = logits[N_EXPERTS:N_EXPERTS + N_GROUPS, :]
    gi = lax.broadcasted_iota(I32, gl.shape, 0).astype(F32)
    gmax = jnp.max(gl, axis=0, keepdims=True)
    g_sel = jnp.min(jnp.where(gl == gmax, gi, float(N_GROUPS)), axis=0, keepdims=True)
    pg_sel = 1.0 / jnp.sum(jnp.exp(gl - gmax), axis=0, keepdims=True)

    el = jnp.zeros((EXPERTS_PER_GROUP, ts), F32)
    for gidx in range(N_GROUPS):
        lo = gidx * EXPERTS_PER_GROUP
        el = jnp.where(g_sel == float(gidx), logits[lo:lo + EXPERTS_PER_GROUP, :], el)
    ei = lax.broadcasted_iota(I32, el.shape, 0).astype(F32)
    m1 = jnp.max(el, axis=0, keepdims=True)
    i1 = jnp.min(jnp.where(el == m1, ei, float(EXPERTS_PER_GROUP)), axis=0, keepdims=True)
    rest = jnp.where(ei == i1, -jnp.inf, el)
    m2 = jnp.max(rest, axis=0, keepdims=True)
    i2 = jnp.min(jnp.where(rest == m2, ei, float(EXPERTS_PER_GROUP)), axis=0, keepdims=True)
    ratio = jnp.exp(m2 - m1)
    gate1 = pg_sel / (1.0 + ratio)
    gate2 = pg_sel * ratio / (1.0 + ratio)
    e1 = g_sel * float(EXPERTS_PER_GROUP) + i1
    e2 = g_sel * float(EXPERTS_PER_GROUP) + i2

    xi = lax.broadcasted_iota(I32, (N_EXPERTS, ts), 0).astype(F32)
    oh1 = jnp.where(xi == e1, 1.0, 0.0)
    oh2 = jnp.where(xi == e2, 1.0, 0.0)
    oh = oh1 + oh2
    n_blk = ts // LANES
    stacked = jnp.concatenate([oh[:, c * LANES:(c + 1) * LANES] for c in range(n_blk)], axis=0)
    within = _dot(stacked.astype(BF16), upper_ref[...])
    totals = jnp.sum(stacked, axis=1, keepdims=True)
    run = carry_ref[...]
    before = []
    for c in range(n_blk):
        before.append(within[c * N_EXPERTS:(c + 1) * N_EXPERTS, :] + run)
        run = run + totals[c * N_EXPERTS:(c + 1) * N_EXPERTS, :]
    before = jnp.concatenate(before, axis=1)
    rank1 = jnp.sum(oh1 * before, axis=0, keepdims=True)
    rank2 = jnp.sum(oh2 * before, axis=0, keepdims=True)
    carry_ref[...] = run
    cnt_ref[...] = jnp.broadcast_to(run, cnt_ref.shape)

    route_ref[0:1, :] = e1.astype(I32)
    route_ref[1:2, :] = e2.astype(I32)
    route_ref[2:3, :] = rank1.astype(I32)
    route_ref[3:4, :] = rank2.astype(I32)
    route_ref[4:5, :] = lax.bitcast_convert_type(gate1, I32)
    route_ref[5:6, :] = lax.bitcast_convert_type(gate2, I32)
    route_ref[6:8, :] = jnp.zeros((2, ts), I32)


def _xattn_router(x, g_x, w_q, k_mem, v_mem, layer, w_o, g_f, w_grp, b_grp, w_exp, b_exp):
    b, s, d = x.shape
    nm = k_mem.shape[2]
    t = b * s
    ts = min(TILE_XATTN, s)
    nj = s // ts
    row = lambda v: v.reshape(1, -1)
    const = lambda shape: pl.BlockSpec(shape, lambda i, j: (0,) * len(shape))
    pad = ROUTER_ROWS - N_GROUPS - N_EXPERTS
    w_r = jnp.pad(jnp.concatenate([w_exp, w_grp], axis=1).T, ((0, pad), (0, 0)))
    w_r_hi = w_r.astype(BF16)
    w_r_split = jnp.concatenate([w_r_hi, (w_r - w_r_hi.astype(F32)).astype(BF16)], axis=0)
    b_r = jnp.pad(jnp.concatenate([b_exp, b_grp]), (0, pad)).reshape(ROUTER_ROWS, 1)
    ti = jnp.arange(LANES)
    upper = (ti[:, None] < ti[None, :]).astype(BF16)
    return pl.pallas_call(
        _xattn_router_kernel,
        out_shape=(
            jax.ShapeDtypeStruct(x.shape, F32),
            jax.ShapeDtypeStruct((t, d // 2), U32),
            jax.ShapeDtypeStruct((8, t), I32),
            jax.ShapeDtypeStruct((N_EXPERTS, 128), F32),
        ),
        grid=(b, nj),
        in_specs=[
            pl.BlockSpec((1, ts, d), lambda i, j: (i, j, 0)),
            const((1, d)), const((d, d)),
            pl.BlockSpec((1, 1, nm, d), lambda i, j: (layer, i, 0, 0)),
            pl.BlockSpec((1, 1, nm, d), lambda i, j: (layer, i, 0, 0)),
            const((d, d)), const((1, d)), const((2 * ROUTER_ROWS, d)), const((ROUTER_ROWS, 1)), const((LANES, LANES)),
        ],
        out_specs=(
            pl.BlockSpec((1, ts, d), lambda i, j: (i, j, 0)),
            pl.BlockSpec((ts, d // 2), lambda i, j: (i * nj + j, 0)),
            pl.BlockSpec((8, ts), lambda i, j: (0, i * nj + j)),
            pl.BlockSpec((N_EXPERTS, 128), lambda i, j: (0, 0)),
        ),
        scratch_shapes=[pltpu.VMEM((N_EXPERTS, 1), F32)],
        compiler_params=_cparams(("arbitrary", "arbitrary")),
        name="xattn_router",
    )(x, row(g_x), w_q.astype(BF16), k_mem, v_mem, w_o.astype(BF16), row(g_f), w_r_split, b_r, upper)


def _sc_mesh():
    return plsc.VectorSubcoreMesh(core_axis_name="c", subcore_axis_name="s",
                                  num_cores=SC_CORES, num_subcores=SC_SUBCORES)


def _sc_worker():
    return lax.axis_index("s") * SC_CORES + lax.axis_index("c")


def _dispatch(h_packed, dest, n_rows):
    t, w = h_packed.shape
    per_worker = t // SC_WORKERS
    n_chunks = per_worker // SC_CHUNK
    dest = dest.reshape(TOP_K, SC_WORKERS * n_chunks, SC_CHUNK)

    @functools.partial(
        pl.kernel, mesh=_sc_mesh(),
        out_type=jax.ShapeDtypeStruct((n_rows, w), U32),
        scratch_types=[pltpu.VMEM((n_chunks, SC_CHUNK), I32), pltpu.VMEM((n_chunks, SC_CHUNK), I32),
                       pltpu.VMEM((SC_CHUNK, w), U32)],
        name="moe_dispatch_sc",
    )
    def run(h_hbm, d0_hbm, d1_hbm, xbuf_hbm, idx0_v, idx1_v, rows_v):
        wid = _sc_worker()
        pltpu.sync_copy(d0_hbm.at[pl.ds(wid * n_chunks, n_chunks)], idx0_v)
        pltpu.sync_copy(d1_hbm.at[pl.ds(wid * n_chunks, n_chunks)], idx1_v)

        @pl.loop(0, n_chunks)
        def _(i):
            pltpu.sync_copy(h_hbm.at[pl.ds(wid * per_worker + i * SC_CHUNK, SC_CHUNK)], rows_v)
            pltpu.sync_copy(rows_v, xbuf_hbm.at[idx0_v.at[i]])
            pltpu.sync_copy(rows_v, xbuf_hbm.at[idx1_v.at[i]])

    return run(h_packed, dest[0], dest[1])


def _gather_pairs(y_buf, dest):
    t = dest.shape[1]
    w = y_buf.shape[1]
    per_worker = t // SC_WORKERS
    n_chunks = per_worker // SC_CHUNK
    dest = dest.reshape(TOP_K, SC_WORKERS * n_chunks, SC_CHUNK)
    out = jax.ShapeDtypeStruct((t, w), U32)

    @functools.partial(
        pl.kernel, mesh=_sc_mesh(),
        out_type=(out, out),
        scratch_types=[pltpu.VMEM((n_chunks, SC_CHUNK), I32), pltpu.VMEM((n_chunks, SC_CHUNK), I32),
                       pltpu.VMEM((SC_CHUNK, w), U32)],
        name="moe_gather_sc",
    )
    def run(y_hbm, d0_hbm, d1_hbm, y0_hbm, y1_hbm, idx0_v, idx1_v, rows_v):
        wid = _sc_worker()
        pltpu.sync_copy(d0_hbm.at[pl.ds(wid * n_chunks, n_chunks)], idx0_v)
        pltpu.sync_copy(d1_hbm.at[pl.ds(wid * n_chunks, n_chunks)], idx1_v)

        @pl.loop(0, n_chunks)
        def _(i):
            rows = pl.ds(wid * per_worker + i * SC_CHUNK, SC_CHUNK)
            pltpu.sync_copy(y_hbm.at[idx0_v.at[i]], rows_v)
            pltpu.sync_copy(rows_v, y0_hbm.at[rows])
            pltpu.sync_copy(y_hbm.at[idx1_v.at[i]], rows_v)
            pltpu.sync_copy(rows_v, y1_hbm.at[rows])

    return run(y_buf, dest[0], dest[1])


def _expert_kernel(be_ref, bf_ref, bv_ref, x_ref, wg_ref, wu_ref, wd_ref, y_ref, wg_bf, wu_bf, wd_bf):
    del be_ref
    blk = pl.program_id(0)
    valid = bv_ref[blk]

    @pl.when(bf_ref[blk] == 1)
    def _():
        wg_bf[...] = wg_ref[0, 0].astype(BF16)
        wu_bf[...] = wu_ref[0, 0].astype(BF16)
        wd_bf[...] = wd_ref[0, 0].astype(BF16)

    @pl.when(valid > 0)
    def _():
        live = lax.broadcasted_iota(I32, x_ref.shape, 0) < valid
        xb = _unpack_bf16_pairs(jnp.where(live, x_ref[...], jnp.uint32(0))).astype(BF16)
        gt = _dot(xb, wg_bf[...])
        up = _dot(xb, wu_bf[...])
        act = (gt * _sigmoid(gt) * up).astype(BF16)
        y_ref[...] = _pack_bf16_pairs(_dot(act, wd_bf[...]))

    @pl.when(valid <= 0)
    def _():
        y_ref[...] = jnp.zeros(y_ref.shape, U32)


def _experts(x_buf, block_expert, block_first, block_valid, w_gate, w_up, w_down, layer):
    n_rows, w = x_buf.shape
    d, de = w_gate.shape[2], w_gate.shape[3]
    bm = MOE_BLOCK_ROWS
    grid_spec = pltpu.PrefetchScalarGridSpec(
        num_scalar_prefetch=3,
        grid=(n_rows // bm,),
        in_specs=[
            pl.BlockSpec((bm, w), lambda i, be, bf, bv: (i, 0)),
            pl.BlockSpec((1, 1, d, de), lambda i, be, bf, bv: (layer, be[i], 0, 0)),
            pl.BlockSpec((1, 1, d, de), lambda i, be, bf, bv: (layer, be[i], 0, 0)),
            pl.BlockSpec((1, 1, de, d), lambda i, be, bf, bv: (layer, be[i], 0, 0)),
        ],
        out_specs=pl.BlockSpec((bm, w), lambda i, be, bf, bv: (i, 0)),
        scratch_shapes=[pltpu.VMEM((d, de), BF16), pltpu.VMEM((d, de), BF16), pltpu.VMEM((de, d), BF16)],
    )
    return pl.pallas_call(
        _expert_kernel,
        out_shape=jax.ShapeDtypeStruct((n_rows, w), U32),
        grid_spec=grid_spec,
        compiler_params=_cparams(("arbitrary",)),
        name="moe_experts",
    )(block_expert, block_first, block_valid, x_buf, w_gate, w_up, w_down)


def _combine_kernel(x_ref, gate_ref, y0_ref, y1_ref, gfin_ref, o_ref, *, final_norm):
    out = _moe_residual(x_ref[...], gate_ref[...], y0_ref[...], y1_ref[...])
    if final_norm:
        out = _rms(out, gfin_ref[...])
    o_ref[...] = out


def _combine(x2, y0, y1, gates, g_final, final_norm):
    t, d = x2.shape
    w = y0.shape[1]
    ts = min(TILE_COMBINE, t)
    return pl.pallas_call(
        functools.partial(_combine_kernel, final_norm=final_norm),
        out_shape=jax.ShapeDtypeStruct((t, d), F32),
        grid=(t // ts,),
        in_specs=[
            pl.BlockSpec((ts, d), lambda i: (i, 0)),
            pl.BlockSpec((ts, TOP_K), lambda i: (i, 0)),
            pl.BlockSpec((ts, w), lambda i: (i, 0)),
            pl.BlockSpec((ts, w), lambda i: (i, 0)),
            pl.BlockSpec((1, d), lambda i: (0, 0)),
        ],
        out_specs=pl.BlockSpec((ts, d), lambda i: (i, 0)),
        compiler_params=_cparams(("arbitrary",)),
        name="moe_combine",
    )(x2, gates, y0, y1, g_final.reshape(1, d))


def _moe_layout(route, counts):
    bm = MOE_BLOCK_ROWS
    t = route.shape[1]
    n_blocks = (t * TOP_K) // bm + N_EXPERTS
    cnt = counts[:, 0].astype(I32)
    padded = (cnt + bm - 1) // bm * bm
    pad_ends = jnp.cumsum(padded)
    pad_off = pad_ends - padded
    experts = jnp.arange(N_EXPERTS, dtype=I32)
    hit = route[0:TOP_K, :, None] == experts
    dest = jnp.sum(jnp.where(hit, pad_off, 0), axis=-1) + route[TOP_K:2 * TOP_K]
    gates = lax.bitcast_convert_type(route[2 * TOP_K:3 * TOP_K], F32).T
    starts = jnp.arange(n_blocks, dtype=I32) * bm
    block_expert = jnp.minimum(jnp.sum((pad_ends[None, :] <= starts[:, None]).astype(I32), axis=1),
                               N_EXPERTS - 1)
    block_first = jnp.concatenate([jnp.ones((1,), I32), (block_expert[1:] != block_expert[:-1]).astype(I32)])
    own = block_expert[:, None] == experts
    block_valid = jnp.clip(jnp.sum(jnp.where(own, cnt + pad_off, 0), axis=1) - starts, 0, bm)
    block_valid = jnp.where(starts < pad_ends[-1], block_valid, 0).astype(I32)
    return dest, gates, block_expert, block_first, block_valid, n_blocks * bm


def kernel(x, mem, norm_mix, norm_xattn, norm_ffn, norm_mem, norm_final, conv_w_in, conv_b_in, conv_w_dw,
           conv_b_dw, conv_ln_g, conv_ln_b, conv_w_out, conv_b_out, gla_w_in, gla_w_a2, gla_b_a, gla_norm_g,
           gla_w_o, xa_w_q, xa_w_kv, xa_w_o, moe_w_grp, moe_b_grp, moe_w_exp, moe_b_exp, moe_w_gate, moe_w_up,
           moe_w_down):
    b, s, d = x.shape
    depth = norm_mix.shape[0]
    k_mem, v_mem = _mem_kv(mem, norm_mem, xa_w_kv)
    moe = None
    for i in range(depth):
        j = i // 2
        if i % 2 == 0:
            if moe is not None:
                x = _combine(x.reshape(b * s, d), moe[1], moe[2], moe[0], norm_final, False).reshape(b, s, d)
            x = _conv_mixer(x, norm_mix[i], conv_w_in[j], conv_b_in[j], conv_w_dw[j], conv_b_dw[j],
                            conv_ln_g[j], conv_ln_b[j], conv_w_out[j], conv_b_out[j])
        else:
            x = _gla_mixer(x, *moe, norm_mix[i], gla_w_in[j], gla_w_a2[j], gla_b_a[j], gla_norm_g[j], gla_w_o[j])
        x2, h_packed, route, counts = _xattn_router(
            x, norm_xattn[i], xa_w_q[i], k_mem, v_mem, i, xa_w_o[i], norm_ffn[i],
            moe_w_grp[i], moe_b_grp[i], moe_w_exp[i], moe_b_exp[i])
        dest, gates, block_expert, block_first, block_valid, n_rows = _moe_layout(route, counts)
        x_buf = _dispatch(h_packed, dest, n_rows)
        y_buf = _experts(x_buf, block_expert, block_first, block_valid, moe_w_gate, moe_w_up, moe_w_down, i)
        y0, y1 = _gather_pairs(y_buf, dest)
        x, moe = x2, (gates, y0, y1)
    return _combine(x.reshape(b * s, d), moe[1], moe[2], moe[0], norm_final, True).reshape(b, s, d)
```

```python
import functools

import jax
import jax.numpy as jnp
from jax import lax
from jax.experimental import pallas as pl
from jax.experimental.pallas import tpu as pltpu
from jax.experimental.pallas import tpu_sc as plsc

F32 = jnp.float32
BF16 = jnp.bfloat16
I32 = jnp.int32
U32 = jnp.uint32

EPS = 1e-6
CONV_KERNEL = 31
CONV_CARRY = 32
CONV_ROWS = 64
CONV_COLS = 256
SUBLANES = 8
LANES = 128
GLA_HEADS = 4
GLA_RANK = 16
GLA_RANK_PAD = 128
GLA_TAU = 16.0
GLA_LEAF = 32
XATTN_HEADS = 4
N_GROUPS = 4
EXPERTS_PER_GROUP = 8
N_EXPERTS = N_GROUPS * EXPERTS_PER_GROUP
ROUTER_ROWS = 40
TOP_K = 2

TILE_CONV = 256
TILE_GLA = 256
TILE_XATTN = 512
TILE_COMBINE = 512
SC_CORES = 2
SC_SUBCORES = 16
SC_WORKERS = SC_CORES * SC_SUBCORES
SC_CHUNK = 64
MOE_BLOCK_ROWS = 512
VMEM_LIMIT = 56 * 1024 * 1024


def _cparams(sem):
    return pltpu.CompilerParams(dimension_semantics=sem, vmem_limit_bytes=VMEM_LIMIT)


def _rms(x, g):
    return x * lax.rsqrt(jnp.mean(x * x, axis=-1, keepdims=True) + EPS) * g


def _sigmoid(x):
    return 0.5 * jnp.tanh(0.5 * x) + 0.5


def _split_bf16(x):
    hi = x.astype(BF16)
    lo = (x - hi.astype(F32)).astype(BF16)
    return hi, lo


def _dot(a, b):
    return jnp.dot(a, b, preferred_element_type=F32)


def _dot_nt(a, b):
    return lax.dot_general(a, b, (((1,), (1,)), ((), ())), preferred_element_type=F32)


def _dot_tn(a, b):
    return lax.dot_general(a, b, (((0,), (0,)), ((), ())), preferred_element_type=F32)


def _pack_bf16_pairs(x):
    w = x.shape[1] // 2
    hi = lax.bitcast_convert_type(x[:, :w].astype(BF16).astype(F32), U32)
    lo = lax.bitcast_convert_type(x[:, w:].astype(BF16).astype(F32), U32)
    return hi | (lo >> 16)


def _unpack_bf16_pairs(p):
    hi = lax.bitcast_convert_type(p & jnp.uint32(0xFFFF0000), F32)
    lo = lax.bitcast_convert_type(p << 16, F32)
    return jnp.concatenate([hi, lo], axis=1)


def _memkv_kernel(mem_ref, g_ref, w_ref, k_ref, v_ref):
    d = mem_ref.shape[-1]
    mn = _rms(mem_ref[0], g_ref[...]).astype(BF16)
    kv = _dot(mn, w_ref[0])
    k_ref[0, 0] = kv[:, :d].astype(BF16)
    v_ref[0, 0] = kv[:, d:].astype(BF16)


def _mem_kv(mem, norm_mem, w_kv):
    b, nm, d = mem.shape
    depth = w_kv.shape[0]
    out = jax.ShapeDtypeStruct((depth, b, nm, d), BF16)
    return pl.pallas_call(
        _memkv_kernel,
        out_shape=(out, out),
        grid=(depth, b),
        in_specs=[
            pl.BlockSpec((1, nm, d), lambda l, i: (i, 0, 0)),
            pl.BlockSpec((1, d), lambda l, i: (0, 0)),
            pl.BlockSpec((1, d, 2 * d), lambda l, i: (l, 0, 0)),
        ],
        out_specs=(
            pl.BlockSpec((1, 1, nm, d), lambda l, i: (l, i, 0, 0)),
            pl.BlockSpec((1, 1, nm, d), lambda l, i: (l, i, 0, 0)),
        ),
        compiler_params=_cparams(("arbitrary", "arbitrary")),
        name="mem_kv",
    )(mem, norm_mem.reshape(1, d), w_kv.astype(BF16))


def _conv_kernel(x_ref, g_ref, win_ref, bin_ref, wdw_ref, bdw_ref, lng_ref, lnb_ref, wout_ref, bout_ref,
                 o_ref, ext_ref, conv_ref):
    ts, d = x_ref.shape[1], x_ref.shape[2]

    @pl.when(pl.program_id(1) == 0)
    def _():
        ext_ref[...] = jnp.zeros(ext_ref.shape, F32)

    x = x_ref[0]
    h = _rms(x, g_ref[...]).astype(BF16)
    u = _dot(h, win_ref[...]) + bin_ref[...]
    glu = u[:, :d] * _sigmoid(u[:, d:])
    for b in range(SUBLANES):
        ext_ref[b, CONV_CARRY - b:CONV_CARRY - b + ts, :] = glu

    first = CONV_CARRY - (CONV_KERNEL - 1)

    def chunk(i, carry):
        r0 = pl.multiple_of(i * CONV_ROWS, CONV_ROWS)
        for c0 in range(0, d, CONV_COLS):
            cols = slice(c0, c0 + CONV_COLS)
            acc = [jnp.zeros((SUBLANES, CONV_COLS), F32) for _ in range(CONV_ROWS // SUBLANES)]
            for k in range(CONV_KERNEL):
                b = (first + k) % SUBLANES
                wk = wdw_ref[k * SUBLANES:(k + 1) * SUBLANES, cols]
                for j in range(CONV_ROWS // SUBLANES):
                    rows = pl.ds(r0 + (first + k - b) + j * SUBLANES, SUBLANES)
                    acc[j] = acc[j] + ext_ref[b, rows, cols] * wk
            for j in range(CONV_ROWS // SUBLANES):
                conv_ref[pl.ds(r0 + j * SUBLANES, SUBLANES), cols] = acc[j]
        return carry

    lax.fori_loop(0, ts // CONV_ROWS, chunk, 0)
    for b in range(SUBLANES):
        ext_ref[b, 0:CONV_CARRY, :] = ext_ref[b, ts:ts + CONV_CARRY, :]

    c = conv_ref[...] + bdw_ref[...]
    mu = jnp.mean(c, axis=-1, keepdims=True)
    cc = c - mu
    var = jnp.mean(cc * cc, axis=-1, keepdims=True)
    un = cc * lax.rsqrt(var + EPS) * lng_ref[...] + lnb_ref[...]
    act = (un * _sigmoid(un)).astype(BF16)
    o_ref[0] = x + _dot(act, wout_ref[...]) + bout_ref[...]


def _conv_mixer(x, g, w_in, b_in, w_dw, b_dw, ln_g, ln_b, w_out, b_out):
    b, s, d = x.shape
    ts = min(TILE_CONV, s)
    row = lambda v: v.reshape(1, -1)
    const = lambda shape: pl.BlockSpec(shape, lambda i, j: (0,) * len(shape))
    return pl.pallas_call(
        _conv_kernel,
        out_shape=jax.ShapeDtypeStruct(x.shape, F32),
        grid=(b, s // ts),
        in_specs=[
            pl.BlockSpec((1, ts, d), lambda i, j: (i, j, 0)),
            const((1, d)), const((d, 2 * d)), const((1, 2 * d)), const((CONV_KERNEL * SUBLANES, d)), const((1, d)),
            const((1, d)), const((1, d)), const((d, d)), const((1, d)),
        ],
        out_specs=pl.BlockSpec((1, ts, d), lambda i, j: (i, j, 0)),
        scratch_shapes=[pltpu.VMEM((SUBLANES, CONV_CARRY + ts, d), F32), pltpu.VMEM((ts, d), F32)],
        compiler_params=_cparams(("arbitrary", "arbitrary")),
        name="conv_mixer",
    )(x, row(g), w_in.astype(BF16), row(b_in), jnp.repeat(w_dw, SUBLANES, axis=0), row(b_dw), row(ln_g), row(ln_b),
      w_out.astype(BF16), row(b_out))


def _gla_levels(ts):
    sizes = [GLA_LEAF]
    while sizes[-1] < ts:
        sizes.append(sizes[-1] * 2)
    return sizes


def _moe_residual(x, gates, y0, y1):
    return x + _unpack_bf16_pairs(y0) * gates[:, 0:1] + _unpack_bf16_pairs(y1) * gates[:, 1:2]


def _gla_kernel(x_ref, gate_ref, y0_ref, y1_ref, g_ref, wq_ref, wk_ref, wv_ref, wa_ref, wr_ref, wa2_ref, ba_ref,
                ng_ref, wo_ref, o_ref, state_ref):
    ts, d = x_ref.shape[1], x_ref.shape[2]
    dk = wq_ref.shape[1]
    dkh = dk // GLA_HEADS
    dvh = d // GLA_HEADS

    @pl.when(pl.program_id(1) == 0)
    def _():
        state_ref[...] = jnp.zeros(state_ref.shape, F32)

    x = _moe_residual(x_ref[0], gate_ref[...], y0_ref[...], y1_ref[...])
    h = _rms(x, g_ref[...]).astype(BF16)
    q = _dot(h, wq_ref[...]) * (dkh ** -0.5)
    k = _dot(h, wk_ref[...])
    v = _dot(h, wv_ref[...]).astype(BF16)
    r = _dot(h, wr_ref[...])
    a_hi, a_lo = _split_bf16(_dot(h, wa_ref[...]))
    w2_hi, w2_lo = wa2_ref[0], wa2_ref[1]
    z = _dot(a_hi, w2_hi) + _dot(a_lo, w2_hi) + _dot(a_hi, w2_lo) + ba_ref[...]
    log_a = -(jnp.maximum(-z, 0.0) + jnp.log(1.0 + jnp.exp(-jnp.abs(z)))) * (1.0 / GLA_TAU)

    row = lax.broadcasted_iota(I32, (ts, ts), 0)
    col = lax.broadcasted_iota(I32, (ts, ts), 1)
    tri = jnp.where(col <= row, 1.0, 0.0).astype(BF16)
    la_hi, la_lo = _split_bf16(log_a)
    bcum = _dot(tri, la_hi) + _dot(tri, la_lo)
    b_last = bcum[ts - 1:ts, :]

    q_in = (q * jnp.exp(bcum)).astype(BF16)
    k_out = (k * jnp.exp(b_last - bcum)).astype(BF16)

    sizes = _gla_levels(ts)
    scores = [None] * GLA_HEADS
    for lvl, size in enumerate(sizes):
        half = size // 2
        same_block = (row & -size) == (col & -size)
        if lvl == 0:
            pair = same_block & (col <= row)
            q_ok = k_ok = None
        else:
            pair = same_block & ((row & (size - 1)) >= half) & ((col & (size - 1)) < half)
            pos = lax.broadcasted_iota(I32, (ts, dk), 0) & (size - 1)
            q_ok = pos >= half
            k_ok = pos < half
        ref = jnp.concatenate(
            [jnp.broadcast_to(bcum[r0 + half:r0 + half + 1, :], (size, dk)) for r0 in range(0, ts, size)], axis=0)
        ql = q * jnp.exp(bcum - ref)
        kl = k * jnp.exp(ref - bcum)
        if q_ok is not None:
            ql = jnp.where(q_ok, ql, 0.0)
            kl = jnp.where(k_ok, kl, 0.0)
        ql = ql.astype(BF16)
        kl = kl.astype(BF16)
        for hd in range(GLA_HEADS):
            c0 = hd * dkh
            a = _dot_nt(ql[:, c0:c0 + dkh], kl[:, c0:c0 + dkh])
            a = jnp.where(pair, a, 0.0)
            scores[hd] = a if scores[hd] is None else scores[hd] + a

    outs = []
    for hd in range(GLA_HEADS):
        c0 = hd * dkh
        v_h = v[:, hd * dvh:(hd + 1) * dvh]
        st = state_ref[hd]
        o_h = _dot(scores[hd].astype(BF16), v_h) + _dot_nt(q_in[:, c0:c0 + dkh], st.astype(BF16))
        decay = jnp.exp(b_last[:, c0:c0 + dkh])
        state_ref[hd] = st * decay + _dot_tn(v_h, k_out[:, c0:c0 + dkh])
        o_h = o_h * lax.rsqrt(jnp.mean(o_h * o_h, axis=-1, keepdims=True) + EPS) * ng_ref[...]
        outs.append(o_h)
    o = jnp.concatenate(outs, axis=1) * (r * _sigmoid(r))
    o_ref[0] = x + _dot(o.astype(BF16), wo_ref[...])


def _gla_mixer(x, gates, y0, y1, g, w_in, w_a2, b_a, norm_g, w_o):
    b, s, d = x.shape
    dk = w_a2.shape[1]
    ts = min(TILE_GLA, s)
    nj = s // ts
    row = lambda v: v.reshape(1, -1)
    const = lambda shape: pl.BlockSpec(shape, lambda i, j: (0,) * len(shape))
    wq = w_in[:, :dk].astype(BF16)
    wk = w_in[:, dk:2 * dk].astype(BF16)
    wv = w_in[:, 2 * dk:2 * dk + d].astype(BF16)
    wa = jnp.pad(w_in[:, 2 * dk + d:2 * dk + d + GLA_RANK], ((0, 0), (0, GLA_RANK_PAD - GLA_RANK))).astype(BF16)
    wr = w_in[:, 2 * dk + d + GLA_RANK:].astype(BF16)
    wa2 = jnp.pad(w_a2, ((0, GLA_RANK_PAD - GLA_RANK), (0, 0)))
    wa2_hi = wa2.astype(BF16)
    wa2_lo = (wa2 - wa2_hi.astype(F32)).astype(BF16)
    wa2_split = jnp.stack([wa2_hi, wa2_lo])
    dvh = d // GLA_HEADS
    return pl.pallas_call(
        _gla_kernel,
        out_shape=jax.ShapeDtypeStruct(x.shape, F32),
        grid=(b, s // ts),
        in_specs=[
            pl.BlockSpec((1, ts, d), lambda i, j: (i, j, 0)),
            pl.BlockSpec((ts, TOP_K), lambda i, j: (i * nj + j, 0)),
            pl.BlockSpec((ts, d // 2), lambda i, j: (i * nj + j, 0)),
            pl.BlockSpec((ts, d // 2), lambda i, j: (i * nj + j, 0)),
            const((1, d)), const((d, dk)), const((d, dk)), const((d, d)), const((d, GLA_RANK_PAD)),
            const((d, d)), const((2, GLA_RANK_PAD, dk)), const((1, dk)), const((1, dvh)), const((d, d)),
        ],
        out_specs=pl.BlockSpec((1, ts, d), lambda i, j: (i, j, 0)),
        scratch_shapes=[pltpu.VMEM((GLA_HEADS, dvh, dk // GLA_HEADS), F32)],
        compiler_params=_cparams(("arbitrary", "arbitrary")),
        name="gla_mixer",
    )(x, gates, y0, y1, row(g), wq, wk, wv, wa, wr, wa2_split, row(b_a), row(norm_g), w_o.astype(BF16))


def _xattn_router_kernel(x_ref, gx_ref, wq_ref, k_ref, v_ref, wo_ref, gf_ref, wr_ref, br_ref, upper_ref,
                         x_out_ref, h_out_ref, route_ref, cnt_ref, carry_ref):
    ts, d = x_ref.shape[1], x_ref.shape[2]
    hd = d // XATTN_HEADS
    first = (pl.program_id(0) == 0) & (pl.program_id(1) == 0)

    @pl.when(first)
    def _():
        carry_ref[...] = jnp.zeros(carry_ref.shape, F32)

    x = x_ref[0]
    h = _rms(x, gx_ref[...]).astype(BF16)
    q = (_dot(h, wq_ref[...]) * (hd ** -0.5)).astype(BF16)
    k = k_ref[0, 0]
    v = v_ref[0, 0]
    outs = []
    for a in range(XATTN_HEADS):
        sl = slice(a * hd, (a + 1) * hd)
        s = _dot_nt(q[:, sl], k[:, sl])
        p = jnp.exp(s - jnp.max(s, axis=-1, keepdims=True))
        o = _dot(p.astype(BF16), v[:, sl]) / jnp.sum(p, axis=-1, keepdims=True)
        outs.append(o)
    att = jnp.concatenate(outs, axis=1).astype(BF16)
    x2 = x + _dot(att, wo_ref[...])
    x_out_ref[0] = x2

    hf = _rms(x2, gf_ref[...])
    h_out_ref[...] = _pack_bf16_pairs(hf)

    h_hi, h_lo = _split_bf16(hf)
    both = _dot_nt(wr_ref[...], h_hi)
    logits = (both[0:ROUTER_ROWS, :] + both[ROUTER_ROWS:2 * ROUTER_ROWS, :]
              + _dot_nt(wr_ref[0:ROUTER_ROWS, :], h_lo) + br_ref[...])
    gl = logits[N_EXPERTS:N_EXPERTS + N_GROUPS, :]
    gi = lax.broadcasted_iota(I32, gl.shape, 0).astype(F32)
    gmax = jnp.max(gl, axis=0, keepdims=True)
    g_sel = jnp.min(jnp.where(gl == gmax, gi, float(N_GROUPS)), axis=0, keepdims=True)
    pg_sel = 1.0 / jnp.sum(jnp.exp(gl - gmax), axis=0, keepdims=True)

    el = jnp.zeros((EXPERTS_PER_GROUP, ts), F32)
    for gidx in range(N_GROUPS):
        lo = gidx * EXPERTS_PER_GROUP
        el = jnp.where(g_sel == float(gidx), logits[lo:lo + EXPERTS_PER_GROUP, :], el)
    ei = lax.broadcasted_iota(I32, el.shape, 0).astype(F32)
    m1 = jnp.max(el, axis=0, keepdims=True)
    i1 = jnp.min(jnp.where(el == m1, ei, float(EXPERTS_PER_GROUP)), axis=0, keepdims=True)
    rest = jnp.where(ei == i1, -jnp.inf, el)
    m2 = jnp.max(rest, axis=0, keepdims=True)
    i2 = jnp.min(jnp.where(rest == m2, ei, float(EXPERTS_PER_GROUP)), axis=0, keepdims=True)
    ratio = jnp.exp(m2 - m1)
    gate1 = pg_sel / (1.0 + ratio)
    gate2 = pg_sel * ratio / (1.0 + ratio)
    e1 = g_sel * float(EXPERTS_PER_GROUP) + i1
    e2 = g_sel * float(EXPERTS_PER_GROUP) + i2

    xi = lax.broadcasted_iota(I32, (N_EXPERTS, ts), 0).astype(F32)
    oh1 = jnp.where(xi == e1, 1.0, 0.0)
    oh2 = jnp.where(xi == e2, 1.0, 0.0)
    oh = oh1 + oh2
    n_blk = ts // LANES
    stacked = jnp.concatenate([oh[:, c * LANES:(c + 1) * LANES] for c in range(n_blk)], axis=0)
    within = _dot(stacked.astype(BF16), upper_ref[...])
    totals = jnp.sum(stacked, axis=1, keepdims=True)
    run = carry_ref[...]
    before = []
    for c in range(n_blk):
        before.append(within[c * N_EXPERTS:(c + 1) * N_EXPERTS, :] + run)
        run = run + totals[c * N_EXPERTS:(c + 1) * N_EXPERTS, :]
    before = jnp.concatenate(before, axis=1)
    rank1 = jnp.sum(oh1 * before, axis=0, keepdims=True)
    rank2 = jnp.sum(oh2 * before, axis=0, keepdims=True)
    carry_ref[...] = run
    cnt_ref[...] = jnp.broadcast_to(run, cnt_ref.shape)

    route_ref[0:1, :] = e1.astype(I32)
    route_ref[1:2, :] = e2.astype(I32)
    route_ref[2:3, :] = rank1.astype(I32)
    route_ref[3:4, :] = rank2.astype(I32)
    route_ref[4:5, :] = lax.bitcast_convert_type(gate1, I32)
    route_ref[5:6, :] = lax.bitcast_convert_type(gate2, I32)
    route_ref[6:8, :] = jnp.zeros((2, ts), I32)


def _xattn_router(x, g_x, w_q, k_mem, v_mem, layer, w_o, g_f, w_grp, b_grp, w_exp, b_exp):
    b, s, d = x.shape
    nm = k_mem.shape[2]
    t = b * s
    ts = min(TILE_XATTN, s)
    nj = s // ts
    row = lambda v: v.reshape(1, -1)
    const = lambda shape: pl.BlockSpec(shape, lambda i, j: (0,) * len(shape))
    pad = ROUTER_ROWS - N_GROUPS - N_EXPERTS
    w_r = jnp.pad(jnp.concatenate([w_exp, w_grp], axis=1).T, ((0, pad), (0, 0)))
    w_r_hi = w_r.astype(BF16)
    w_r_split = jnp.concatenate([w_r_hi, (w_r - w_r_hi.astype(F32)).astype(BF16)], axis=0)
    b_r = jnp.pad(jnp.concatenate([b_exp, b_grp]), (0, pad)).reshape(ROUTER_ROWS, 1)
    ti = jnp.arange(LANES)
    upper = (ti[:, None] < ti[None, :]).astype(BF16)
    return pl.pallas_call(
        _xattn_router_kernel,
        out_shape=(
            jax.ShapeDtypeStruct(x.shape, F32),
            jax.ShapeDtypeStruct((t, d // 2), U32),
            jax.ShapeDtypeStruct((8, t), I32),
            jax.ShapeDtypeStruct((N_EXPERTS, 128), F32),
        ),
        grid=(b, nj),
        in_specs=[
            pl.BlockSpec((1, ts, d), lambda i, j: (i, j, 0)),
            const((1, d)), const((d, d)),
            pl.BlockSpec((1, 1, nm, d), lambda i, j: (layer, i, 0, 0)),
            pl.BlockSpec((1, 1, nm, d), lambda i, j: (layer, i, 0, 0)),
            const((d, d)), const((1, d)), const((2 * ROUTER_ROWS, d)), const((ROUTER_ROWS, 1)), const((LANES, LANES)),
        ],
        out_specs=(
            pl.BlockSpec((1, ts, d), lambda i, j: (i, j, 0)),
            pl.BlockSpec((ts, d // 2), lambda i, j: (i * nj + j, 0)),
            pl.BlockSpec((8, ts), lambda i, j: (0, i * nj + j)),
            pl.BlockSpec((N_EXPERTS, 128), lambda i, j: (0, 0)),
        ),
        scratch_shapes=[pltpu.VMEM((N_EXPERTS, 1), F32)],
        compiler_params=_cparams(("arbitrary", "arbitrary")),
        name="xattn_router",
    )(x, row(g_x), w_q.astype(BF16), k_mem, v_mem, w_o.astype(BF16), row(g_f), w_r_split, b_r, upper)


def _sc_mesh():
    return plsc.VectorSubcoreMesh(core_axis_name="c", subcore_axis_name="s",
                                  num_cores=SC_CORES, num_subcores=SC_SUBCORES)


def _sc_worker():
    return lax.axis_index("s") * SC_CORES + lax.axis_index("c")


def _dispatch(h_packed, dest, n_rows):
    t, w = h_packed.shape
    per_worker = t // SC_WORKERS
    n_chunks = per_worker // SC_CHUNK
    dest = dest.reshape(TOP_K, SC_WORKERS * n_chunks, SC_CHUNK)

    @functools.partial(
        pl.kernel, mesh=_sc_mesh(),
        out_type=jax.ShapeDtypeStruct((n_rows, w), U32),
        scratch_types=[pltpu.VMEM((n_chunks, SC_CHUNK), I32), pltpu.VMEM((n_chunks, SC_CHUNK), I32),
                       pltpu.VMEM((SC_CHUNK, w), U32)],
        name="moe_dispatch_sc",
    )
    def run(h_hbm, d0_hbm, d1_hbm, xbuf_hbm, idx0_v, idx1_v, rows_v):
        wid = _sc_worker()
        pltpu.sync_copy(d0_hbm.at[pl.ds(wid * n_chunks, n_chunks)], idx0_v)
        pltpu.sync_copy(d1_hbm.at[pl.ds(wid * n_chunks, n_chunks)], idx1_v)

        @pl.loop(0, n_chunks)
        def _(i):
            pltpu.sync_copy(h_hbm.at[pl.ds(wid * per_worker + i * SC_CHUNK, SC_CHUNK)], rows_v)
            pltpu.sync_copy(rows_v, xbuf_hbm.at[idx0_v.at[i]])
            pltpu.sync_copy(rows_v, xbuf_hbm.at[idx1_v.at[i]])

    return run(h_packed, dest[0], dest[1])


def _gather_pairs(y_buf, dest):
    t = dest.shape[1]
    w = y_buf.shape[1]
    per_worker = t // SC_WORKERS
    n_chunks = per_worker // SC_CHUNK
    dest = dest.reshape(TOP_K, SC_WORKERS * n_chunks, SC_CHUNK)
    out = jax.ShapeDtypeStruct((t, w), U32)

    @functools.partial(
        pl.kernel, mesh=_sc_mesh(),
        out_type=(out, out),
        scratch_types=[pltpu.VMEM((n_chunks, SC_CHUNK), I32), pltpu.VMEM((n_chunks, SC_CHUNK), I32),
                       pltpu.VMEM((SC_CHUNK, w), U32)],
        name="moe_gather_sc",
    )
    def run(y_hbm, d0_hbm, d1_hbm, y0_hbm, y1_hbm, idx0_v, idx1_v, rows_v):
        wid = _sc_worker()
        pltpu.sync_copy(d0_hbm.at[pl.ds(wid * n_chunks, n_chunks)], idx0_v)
        pltpu.sync_copy(d1_hbm.at[pl.ds(wid * n_chunks, n_chunks)], idx1_v)

        @pl.loop(0, n_chunks)
        def _(i):
            rows = pl.ds(wid * per_worker + i * SC_CHUNK, SC_CHUNK)
            pltpu.sync_copy(y_hbm.at[idx0_v.at[i]], rows_v)
            pltpu.sync_copy(rows_v, y0_hbm.at[rows])
            pltpu.sync_copy(y_hbm.at[idx1_v.at[i]], rows_v)
            pltpu.sync_copy(rows_v, y1_hbm.at[rows])

    return run(y_buf, dest[0], dest[1])


def _expert_kernel(be_ref, bf_ref, bv_ref, slot_ref, next_ref, x_ref, wg_hbm, wu_hbm, wd_hbm, y_ref,
                   wg_f32, wu_f32, wd_f32, wg_bf, wu_bf, wd_bf, sems, *, layer):
    blk = pl.program_id(0)
    valid = bv_ref[blk]

    def weight_copies(expert, slot):
        return [pltpu.make_async_copy(src.at[layer, expert], dst.at[slot], sems.at[slot, i])
                for i, (src, dst) in enumerate(((wg_hbm, wg_f32), (wu_hbm, wu_f32), (wd_hbm, wd_f32)))]

    @pl.when(blk == 0)
    def _():
        for copy in weight_copies(be_ref[0], 0):
            copy.start()

    @pl.when(bf_ref[blk] == 1)
    def _():
        slot = slot_ref[blk]
        for copy in weight_copies(be_ref[blk], slot):
            copy.wait()

        @pl.when(next_ref[blk] >= 0)
        def _():
            for copy in weight_copies(next_ref[blk], 1 - slot):
                copy.start()

        wg_bf[...] = wg_f32[slot].astype(BF16)
        wu_bf[...] = wu_f32[slot].astype(BF16)
        wd_bf[...] = wd_f32[slot].astype(BF16)

    @pl.when(valid > 0)
    def _():
        live = lax.broadcasted_iota(I32, x_ref.shape, 0) < valid
        xb = _unpack_bf16_pairs(jnp.where(live, x_ref[...], jnp.uint32(0))).astype(BF16)
        gt = _dot(xb, wg_bf[...])
        up = _dot(xb, wu_bf[...])
        act = (gt * _sigmoid(gt) * up).astype(BF16)
        y_ref[...] = _pack_bf16_pairs(_dot(act, wd_bf[...]))

    @pl.when(valid <= 0)
    def _():
        y_ref[...] = jnp.zeros(y_ref.shape, U32)


def _experts(x_buf, block_expert, block_first, block_valid, w_gate, w_up, w_down, layer):
    n_rows, w = x_buf.shape
    d, de = w_gate.shape[2], w_gate.shape[3]
    bm = MOE_BLOCK_ROWS
    n = n_rows // bm
    block_slot = (jnp.cumsum(block_first) - 1) % 2
    idx = jnp.arange(n, dtype=I32)
    later_first = jnp.concatenate([jnp.where(block_first[1:] == 1, idx[1:], n), jnp.full((1,), n, I32)])
    next_first = lax.cummin(later_first, reverse=True)
    block_next = jnp.where(next_first < n, block_expert[jnp.minimum(next_first, n - 1)], -1).astype(I32)
    any_space = pl.BlockSpec(memory_space=pl.ANY)
    grid_spec = pltpu.PrefetchScalarGridSpec(
        num_scalar_prefetch=5,
        grid=(n,),
        in_specs=[pl.BlockSpec((bm, w), lambda i, *_: (i, 0)), any_space, any_space, any_space],
        out_specs=pl.BlockSpec((bm, w), lambda i, *_: (i, 0)),
        scratch_shapes=[pltpu.VMEM((2, d, de), F32), pltpu.VMEM((2, d, de), F32), pltpu.VMEM((2, de, d), F32),
                        pltpu.VMEM((d, de), BF16), pltpu.VMEM((d, de), BF16), pltpu.VMEM((de, d), BF16),
                        pltpu.SemaphoreType.DMA((2, 3))],
    )
    return pl.pallas_call(
        functools.partial(_expert_kernel, layer=layer),
        out_shape=jax.ShapeDtypeStruct((n_rows, w), U32),
        grid_spec=grid_spec,
        compiler_params=_cparams(("arbitrary",)),
        name="moe_experts",
    )(block_expert, block_first, block_valid, block_slot.astype(I32), block_next, x_buf, w_gate, w_up, w_down)


def _combine_kernel(x_ref, gate_ref, y0_ref, y1_ref, gfin_ref, o_ref, *, final_norm):
    out = _moe_residual(x_ref[...], gate_ref[...], y0_ref[...], y1_ref[...])
    if final_norm:
        out = _rms(out, gfin_ref[...])
    o_ref[...] = out


def _combine(x2, y0, y1, gates, g_final, final_norm):
    t, d = x2.shape
    w = y0.shape[1]
    ts = min(TILE_COMBINE, t)
    return pl.pallas_call(
        functools.partial(_combine_kernel, final_norm=final_norm),
        out_shape=jax.ShapeDtypeStruct((t, d), F32),
        grid=(t // ts,),
        in_specs=[
            pl.BlockSpec((ts, d), lambda i: (i, 0)),
            pl.BlockSpec((ts, TOP_K), lambda i: (i, 0)),
            pl.BlockSpec((ts, w), lambda i: (i, 0)),
            pl.BlockSpec((ts, w), lambda i: (i, 0)),
            pl.BlockSpec((1, d), lambda i: (0, 0)),
        ],
        out_specs=pl.BlockSpec((ts, d), lambda i: (i, 0)),
        compiler_params=_cparams(("arbitrary",)),
        name="moe_combine",
    )(x2, gates, y0, y1, g_final.reshape(1, d))


def _moe_layout(route, counts):
    bm = MOE_BLOCK_ROWS
    t = route.shape[1]
    n_blocks = (t * TOP_K) // bm + N_EXPERTS
    cnt = counts[:, 0].astype(I32)
    padded = (cnt + bm - 1) // bm * bm
    pad_ends = jnp.cumsum(padded)
    pad_off = pad_ends - padded
    experts = jnp.arange(N_EXPERTS, dtype=I32)
    hit = route[0:TOP_K, :, None] == experts
    dest = jnp.sum(jnp.where(hit, pad_off, 0), axis=-1) + route[TOP_K:2 * TOP_K]
    gates = lax.bitcast_convert_type(route[2 * TOP_K:3 * TOP_K], F32).T
    starts = jnp.arange(n_blocks, dtype=I32) * bm
    block_expert = jnp.minimum(jnp.sum((pad_ends[None, :] <= starts[:, None]).astype(I32), axis=1),
                               N_EXPERTS - 1)
    block_first = jnp.concatenate([jnp.ones((1,), I32), (block_expert[1:] != block_expert[:-1]).astype(I32)])
    own = block_expert[:, None] == experts
    block_valid = jnp.clip(jnp.sum(jnp.where(own, cnt + pad_off, 0), axis=1) - starts, 0, bm)
    block_valid = jnp.where(starts < pad_ends[-1], block_valid, 0).astype(I32)
    return dest, gates, block_expert, block_first, block_valid, n_blocks * bm


def kernel(x, mem, norm_mix, norm_xattn, norm_ffn, norm_mem, norm_final, conv_w_in, conv_b_in, conv_w_dw,
           conv_b_dw, conv_ln_g, conv_ln_b, conv_w_out, conv_b_out, gla_w_in, gla_w_a2, gla_b_a, gla_norm_g,
           gla_w_o, xa_w_q, xa_w_kv, xa_w_o, moe_w_grp, moe_b_grp, moe_w_exp, moe_b_exp, moe_w_gate, moe_w_up,
           moe_w_down):
    b, s, d = x.shape
    depth = norm_mix.shape[0]
    k_mem, v_mem = _mem_kv(mem, norm_mem, xa_w_kv)
    moe = None
    for i in range(depth):
        j = i // 2
        if i % 2 == 0:
            if moe is not None:
                x = _combine(x.reshape(b * s, d), moe[1], moe[2], moe[0], norm_final, False).reshape(b, s, d)
            x = _conv_mixer(x, norm_mix[i], conv_w_in[j], conv_b_in[j], conv_w_dw[j], conv_b_dw[j],
                            conv_ln_g[j], conv_ln_b[j], conv_w_out[j], conv_b_out[j])
        else:
            x = _gla_mixer(x, *moe, norm_mix[i], gla_w_in[j], gla_w_a2[j], gla_b_a[j], gla_norm_g[j], gla_w_o[j])
        x2, h_packed, route, counts = _xattn_router(
            x, norm_xattn[i], xa_w_q[i], k_mem, v_mem, i, xa_w_o[i], norm_ffn[i],
            moe_w_grp[i], moe_b_grp[i], moe_w_exp[i], moe_b_exp[i])
        dest, gates, block_expert, block_first, block_valid, n_rows = _moe_layout(route, counts)
        x_buf = _dispatch(h_packed, dest, n_rows)
        y_buf = _experts(x_buf, block_expert, block_first, block_valid, moe_w_gate, moe_w_up, moe_w_down, i)
        y0, y1 = _gather_pairs(y_buf, dest)
        x, moe = x2, (gates, y0, y1)
    return _combine(x.reshape(b * s, d), moe[1], moe[2], moe[0], norm_final, True).reshape(b, s, d)
```

```python
import functools

import jax
import jax.numpy as jnp
from jax import lax
from jax.experimental import pallas as pl
from jax.experimental.pallas import tpu as pltpu
from jax.experimental.pallas import tpu_sc as plsc

F32 = jnp.float32
BF16 = jnp.bfloat16
I32 = jnp.int32
U32 = jnp.uint32

EPS = 1e-6
CONV_KERNEL = 31
CONV_CARRY = 32
CONV_ROWS = 64
CONV_COLS = 256
SUBLANES = 8
LANES = 128
GLA_HEADS = 4
GLA_RANK = 16
GLA_RANK_PAD = 128
GLA_TAU = 16.0
GLA_LEAF = 32
XATTN_HEADS = 4
N_GROUPS = 4
EXPERTS_PER_GROUP = 8
N_EXPERTS = N_GROUPS * EXPERTS_PER_GROUP
ROUTER_ROWS = 40
TOP_K = 2

TILE_CONV = 256
TILE_GLA = 256
TILE_XATTN = 512
TILE_COMBINE = 512
SC_CORES = 2
SC_SUBCORES = 16
SC_WORKERS = SC_CORES * SC_SUBCORES
SC_DISPATCH_CHUNK = 64
SC_GATHER_CHUNK = 32
MOE_BLOCK_ROWS = 512
VMEM_LIMIT = 56 * 1024 * 1024


def _cparams(sem):
    return pltpu.CompilerParams(dimension_semantics=sem, vmem_limit_bytes=VMEM_LIMIT)


def _rms(x, g):
    return x * lax.rsqrt(jnp.mean(x * x, axis=-1, keepdims=True) + EPS) * g


def _sigmoid(x):
    return 0.5 * jnp.tanh(0.5 * x) + 0.5


def _split_bf16(x):
    hi = x.astype(BF16)
    lo = (x - hi.astype(F32)).astype(BF16)
    return hi, lo


def _dot(a, b):
    return jnp.dot(a, b, preferred_element_type=F32)


def _dot_nt(a, b):
    return lax.dot_general(a, b, (((1,), (1,)), ((), ())), preferred_element_type=F32)


def _dot_tn(a, b):
    return lax.dot_general(a, b, (((0,), (0,)), ((), ())), preferred_element_type=F32)


def _pack_bf16_pairs(x):
    w = x.shape[1] // 2
    hi = lax.bitcast_convert_type(x[:, :w].astype(BF16).astype(F32), U32)
    lo = lax.bitcast_convert_type(x[:, w:].astype(BF16).astype(F32), U32)
    return hi | (lo >> 16)


def _unpack_bf16_pairs(p):
    hi = lax.bitcast_convert_type(p & jnp.uint32(0xFFFF0000), F32)
    lo = lax.bitcast_convert_type(p << 16, F32)
    return jnp.concatenate([hi, lo], axis=1)


def _memkv_kernel(mem_ref, g_ref, w_ref, k_ref, v_ref):
    d = mem_ref.shape[-1]
    mn = _rms(mem_ref[0], g_ref[...]).astype(BF16)
    kv = _dot(mn, w_ref[0])
    k_ref[0, 0] = kv[:, :d].astype(BF16)
    v_ref[0, 0] = kv[:, d:].astype(BF16)


def _mem_kv(mem, norm_mem, w_kv):
    b, nm, d = mem.shape
    depth = w_kv.shape[0]
    out = jax.ShapeDtypeStruct((depth, b, nm, d), BF16)
    return pl.pallas_call(
        _memkv_kernel,
        out_shape=(out, out),
        grid=(depth, b),
        in_specs=[
            pl.BlockSpec((1, nm, d), lambda l, i: (i, 0, 0)),
            pl.BlockSpec((1, d), lambda l, i: (0, 0)),
            pl.BlockSpec((1, d, 2 * d), lambda l, i: (l, 0, 0)),
        ],
        out_specs=(
            pl.BlockSpec((1, 1, nm, d), lambda l, i: (l, i, 0, 0)),
            pl.BlockSpec((1, 1, nm, d), lambda l, i: (l, i, 0, 0)),
        ),
        compiler_params=_cparams(("arbitrary", "arbitrary")),
        name="mem_kv",
    )(mem, norm_mem.reshape(1, d), w_kv.astype(BF16))


def _conv_kernel(x_ref, g_ref, win_ref, bin_ref, wdw_ref, bdw_ref, lng_ref, lnb_ref, wout_ref, bout_ref,
                 o_ref, ext_ref, conv_ref):
    ts, d = x_ref.shape[1], x_ref.shape[2]

    @pl.when(pl.program_id(1) == 0)
    def _():
        ext_ref[...] = jnp.zeros(ext_ref.shape, F32)

    x = x_ref[0]
    h = _rms(x, g_ref[...]).astype(BF16)
    u = _dot(h, win_ref[...]) + bin_ref[...]
    glu = u[:, :d] * _sigmoid(u[:, d:])
    for b in range(SUBLANES):
        ext_ref[b, CONV_CARRY - b:CONV_CARRY - b + ts, :] = glu

    first = CONV_CARRY - (CONV_KERNEL - 1)

    def chunk(i, carry):
        r0 = pl.multiple_of(i * CONV_ROWS, CONV_ROWS)
        for c0 in range(0, d, CONV_COLS):
            cols = slice(c0, c0 + CONV_COLS)
            acc = [jnp.zeros((SUBLANES, CONV_COLS), F32) for _ in range(CONV_ROWS // SUBLANES)]
            for k in range(CONV_KERNEL):
                b = (first + k) % SUBLANES
                wk = wdw_ref[k * SUBLANES:(k + 1) * SUBLANES, cols]
                for j in range(CONV_ROWS // SUBLANES):
                    rows = pl.ds(r0 + (first + k - b) + j * SUBLANES, SUBLANES)
                    acc[j] = acc[j] + ext_ref[b, rows, cols] * wk
            for j in range(CONV_ROWS // SUBLANES):
                conv_ref[pl.ds(r0 + j * SUBLANES, SUBLANES), cols] = acc[j]
        return carry

    lax.fori_loop(0, ts // CONV_ROWS, chunk, 0)
    for b in range(SUBLANES):
        ext_ref[b, 0:CONV_CARRY, :] = ext_ref[b, ts:ts + CONV_CARRY, :]

    c = conv_ref[...] + bdw_ref[...]
    mu = jnp.mean(c, axis=-1, keepdims=True)
    cc = c - mu
    var = jnp.mean(cc * cc, axis=-1, keepdims=True)
    un = cc * lax.rsqrt(var + EPS) * lng_ref[...] + lnb_ref[...]
    act = (un * _sigmoid(un)).astype(BF16)
    o_ref[0] = x + _dot(act, wout_ref[...]) + bout_ref[...]


def _conv_mixer(x, g, w_in, b_in, w_dw, b_dw, ln_g, ln_b, w_out, b_out):
    b, s, d = x.shape
    ts = min(TILE_CONV, s)
    row = lambda v: v.reshape(1, -1)
    const = lambda shape: pl.BlockSpec(shape, lambda i, j: (0,) * len(shape))
    return pl.pallas_call(
        _conv_kernel,
        out_shape=jax.ShapeDtypeStruct(x.shape, F32),
        grid=(b, s // ts),
        in_specs=[
            pl.BlockSpec((1, ts, d), lambda i, j: (i, j, 0)),
            const((1, d)), const((d, 2 * d)), const((1, 2 * d)), const((CONV_KERNEL * SUBLANES, d)), const((1, d)),
            const((1, d)), const((1, d)), const((d, d)), const((1, d)),
        ],
        out_specs=pl.BlockSpec((1, ts, d), lambda i, j: (i, j, 0)),
        scratch_shapes=[pltpu.VMEM((SUBLANES, CONV_CARRY + ts, d), F32), pltpu.VMEM((ts, d), F32)],
        compiler_params=_cparams(("arbitrary", "arbitrary")),
        name="conv_mixer",
    )(x, row(g), w_in.astype(BF16), row(b_in), jnp.repeat(w_dw, SUBLANES, axis=0), row(b_dw), row(ln_g), row(ln_b),
      w_out.astype(BF16), row(b_out))


def _gla_levels(ts):
    sizes = [GLA_LEAF]
    while sizes[-1] < ts:
        sizes.append(sizes[-1] * 2)
    return sizes


def _moe_residual(x, gates, y0, y1):
    return x + _unpack_bf16_pairs(y0) * gates[:, 0:1] + _unpack_bf16_pairs(y1) * gates[:, 1:2]


def _gla_kernel(x_ref, gate_ref, y0_ref, y1_ref, g_ref, wq_ref, wk_ref, wv_ref, wa_ref, wr_ref, wa2_ref, ba_ref,
                ng_ref, wo_ref, o_ref, state_ref):
    ts, d = x_ref.shape[1], x_ref.shape[2]
    dk = wq_ref.shape[1]
    dkh = dk // GLA_HEADS
    dvh = d // GLA_HEADS

    @pl.when(pl.program_id(1) == 0)
    def _():
        state_ref[...] = jnp.zeros(state_ref.shape, F32)

    x = _moe_residual(x_ref[0], gate_ref[...], y0_ref[...], y1_ref[...])
    h = _rms(x, g_ref[...]).astype(BF16)
    q = _dot(h, wq_ref[...]) * (dkh ** -0.5)
    k = _dot(h, wk_ref[...])
    v = _dot(h, wv_ref[...]).astype(BF16)
    r = _dot(h, wr_ref[...])
    a_hi, a_lo = _split_bf16(_dot(h, wa_ref[...]))
    w2_hi, w2_lo = wa2_ref[0], wa2_ref[1]
    z = _dot(a_hi, w2_hi) + _dot(a_lo, w2_hi) + _dot(a_hi, w2_lo) + ba_ref[...]
    log_a = -(jnp.maximum(-z, 0.0) + jnp.log(1.0 + jnp.exp(-jnp.abs(z)))) * (1.0 / GLA_TAU)

    row = lax.broadcasted_iota(I32, (ts, ts), 0)
    col = lax.broadcasted_iota(I32, (ts, ts), 1)
    tri = jnp.where(col <= row, 1.0, 0.0).astype(BF16)
    la_hi, la_lo = _split_bf16(log_a)
    bcum = _dot(tri, la_hi) + _dot(tri, la_lo)
    b_last = bcum[ts - 1:ts, :]

    q_in = (q * jnp.exp(bcum)).astype(BF16)
    k_out = (k * jnp.exp(b_last - bcum)).astype(BF16)

    sizes = _gla_levels(ts)
    scores = [None] * GLA_HEADS
    for lvl, size in enumerate(sizes):
        half = size // 2
        same_block = (row & -size) == (col & -size)
        if lvl == 0:
            pair = same_block & (col <= row)
            q_ok = k_ok = None
        else:
            pair = same_block & ((row & (size - 1)) >= half) & ((col & (size - 1)) < half)
            pos = lax.broadcasted_iota(I32, (ts, dk), 0) & (size - 1)
            q_ok = pos >= half
            k_ok = pos < half
        ref = jnp.concatenate(
            [jnp.broadcast_to(bcum[r0 + half:r0 + half + 1, :], (size, dk)) for r0 in range(0, ts, size)], axis=0)
        ql = q * jnp.exp(bcum - ref)
        kl = k * jnp.exp(ref - bcum)
        if q_ok is not None:
            ql = jnp.where(q_ok, ql, 0.0)
            kl = jnp.where(k_ok, kl, 0.0)
        ql = ql.astype(BF16)
        kl = kl.astype(BF16)
        for hd in range(GLA_HEADS):
            c0 = hd * dkh
            a = _dot_nt(ql[:, c0:c0 + dkh], kl[:, c0:c0 + dkh])
            a = jnp.where(pair, a, 0.0)
            scores[hd] = a if scores[hd] is None else scores[hd] + a

    outs = []
    for hd in range(GLA_HEADS):
        c0 = hd * dkh
        v_h = v[:, hd * dvh:(hd + 1) * dvh]
        st = state_ref[hd]
        o_h = _dot(scores[hd].astype(BF16), v_h) + _dot_nt(q_in[:, c0:c0 + dkh], st.astype(BF16))
        decay = jnp.exp(b_last[:, c0:c0 + dkh])
        state_ref[hd] = st * decay + _dot_tn(v_h, k_out[:, c0:c0 + dkh])
        o_h = o_h * lax.rsqrt(jnp.mean(o_h * o_h, axis=-1, keepdims=True) + EPS) * ng_ref[...]
        outs.append(o_h)
    o = jnp.concatenate(outs, axis=1) * (r * _sigmoid(r))
    o_ref[0] = x + _dot(o.astype(BF16), wo_ref[...])


def _gla_mixer(x, gates, y0, y1, g, w_in, w_a2, b_a, norm_g, w_o):
    b, s, d = x.shape
    dk = w_a2.shape[1]
    ts = min(TILE_GLA, s)
    nj = s // ts
    row = lambda v: v.reshape(1, -1)
    const = lambda shape: pl.BlockSpec(shape, lambda i, j: (0,) * len(shape))
    wq = w_in[:, :dk].astype(BF16)
    wk = w_in[:, dk:2 * dk].astype(BF16)
    wv = w_in[:, 2 * dk:2 * dk + d].astype(BF16)
    wa = jnp.pad(w_in[:, 2 * dk + d:2 * dk + d + GLA_RANK], ((0, 0), (0, GLA_RANK_PAD - GLA_RANK))).astype(BF16)
    wr = w_in[:, 2 * dk + d + GLA_RANK:].astype(BF16)
    wa2 = jnp.pad(w_a2, ((0, GLA_RANK_PAD - GLA_RANK), (0, 0)))
    wa2_hi = wa2.astype(BF16)
    wa2_lo = (wa2 - wa2_hi.astype(F32)).astype(BF16)
    wa2_split = jnp.stack([wa2_hi, wa2_lo])
    dvh = d // GLA_HEADS
    return pl.pallas_call(
        _gla_kernel,
        out_shape=jax.ShapeDtypeStruct(x.shape, F32),
        grid=(b, s // ts),
        in_specs=[
            pl.BlockSpec((1, ts, d), lambda i, j: (i, j, 0)),
            pl.BlockSpec((ts, TOP_K), lambda i, j: (i * nj + j, 0)),
            pl.BlockSpec((ts, d // 2), lambda i, j: (i * nj + j, 0)),
            pl.BlockSpec((ts, d // 2), lambda i, j: (i * nj + j, 0)),
            const((1, d)), const((d, dk)), const((d, dk)), const((d, d)), const((d, GLA_RANK_PAD)),
            const((d, d)), const((2, GLA_RANK_PAD, dk)), const((1, dk)), const((1, dvh)), const((d, d)),
        ],
        out_specs=pl.BlockSpec((1, ts, d), lambda i, j: (i, j, 0)),
        scratch_shapes=[pltpu.VMEM((GLA_HEADS, dvh, dk // GLA_HEADS), F32)],
        compiler_params=_cparams(("arbitrary", "arbitrary")),
        name="gla_mixer",
    )(x, gates, y0, y1, row(g), wq, wk, wv, wa, wr, wa2_split, row(b_a), row(norm_g), w_o.astype(BF16))


def _xattn_router_kernel(x_ref, gx_ref, wq_ref, k_ref, v_ref, wo_ref, gf_ref, wr_ref, br_ref, upper_ref,
                         x_out_ref, h_out_ref, route_ref, cnt_ref, carry_ref):
    ts, d = x_ref.shape[1], x_ref.shape[2]
    hd = d // XATTN_HEADS
    first = (pl.program_id(0) == 0) & (pl.program_id(1) == 0)

    @pl.when(first)
    def _():
        carry_ref[...] = jnp.zeros(carry_ref.shape, F32)

    x = x_ref[0]
    h = _rms(x, gx_ref[...]).astype(BF16)
    q = (_dot(h, wq_ref[...]) * (hd ** -0.5)).astype(BF16)
    k = k_ref[0, 0]
    v = v_ref[0, 0]
    outs = []
    for a in range(XATTN_HEADS):
        sl = slice(a * hd, (a + 1) * hd)
        s = _dot_nt(q[:, sl], k[:, sl])
        p = jnp.exp(s - jnp.max(s, axis=-1, keepdims=True))
        o = _dot(p.astype(BF16), v[:, sl]) / jnp.sum(p, axis=-1, keepdims=True)
        outs.append(o)
    att = jnp.concatenate(outs, axis=1).astype(BF16)
    x2 = x + _dot(att, wo_ref[...])
    x_out_ref[0] = x2

    hf = _rms(x2, gf_ref[...])
    h_out_ref[...] = _pack_bf16_pairs(hf)

    h_hi, h_lo = _split_bf16(hf)
    both = _dot_nt(wr_ref[...], h_hi)
    logits = (both[0:ROUTER_ROWS, :] + both[ROUTER_ROWS:2 * ROUTER_ROWS, :]
              + _dot_nt(wr_ref[0:ROUTER_ROWS, :], h_lo) + br_ref[...])
    gl = logits[N_EXPERTS:N_EXPERTS + N_GROUPS, :]
    gi = lax.broadcasted_iota(I32, gl.shape, 0).astype(F32)
    gmax = jnp.max(gl, axis=0, keepdims=True)
    g_sel = jnp.min(jnp.where(gl == gmax, gi, float(N_GROUPS)), axis=0, keepdims=True)
    pg_sel = 1.0 / jnp.sum(jnp.exp(gl - gmax), axis=0, keepdims=True)

    el = jnp.zeros((EXPERTS_PER_GROUP, ts), F32)
    for gidx in range(N_GROUPS):
        lo = gidx * EXPERTS_PER_GROUP
        el = jnp.where(g_sel == float(gidx), logits[lo:lo + EXPERTS_PER_GROUP, :], el)
    ei = lax.broadcasted_iota(I32, el.shape, 0).astype(F32)
    m1 = jnp.max(el, axis=0, keepdims=True)
    i1 = jnp.min(jnp.where(el == m1, ei, float(EXPERTS_PER_GROUP)), axis=0, keepdims=True)
    rest = jnp.where(ei == i1, -jnp.inf, el)
    m2 = jnp.max(rest, axis=0, keepdims=True)
    i2 = jnp.min(jnp.where(rest == m2, ei, float(EXPERTS_PER_GROUP)), axis=0, keepdims=True)
    ratio = jnp.exp(m2 - m1)
    gate1 = pg_sel / (1.0 + ratio)
    gate2 = pg_sel * ratio / (1.0 + ratio)
    e1 = g_sel * float(EXPERTS_PER_GROUP) + i1
    e2 = g_sel * float(EXPERTS_PER_GROUP) + i2

    xi = lax.broadcasted_iota(I32, (N_EXPERTS, ts), 0).astype(F32)
    oh1 = jnp.where(xi == e1, 1.0, 0.0)
    oh2 = jnp.where(xi == e2, 1.0, 0.0)
    oh = oh1 + oh2
    n_blk = ts // LANES
    stacked = jnp.concatenate([oh[:, c * LANES:(c + 1) * LANES] for c in range(n_blk)], axis=0)
    within = _dot(stacked.astype(BF16), upper_ref[...])
    totals = jnp.sum(stacked, axis=1, keepdims=True)
    run = carry_ref[...]
    before = []
    for c in range(n_blk):
        before.append(within[c * N_EXPERTS:(c + 1) * N_EXPERTS, :] + run)
        run = run + totals[c * N_EXPERTS:(c + 1) * N_EXPERTS, :]
    before = jnp.concatenate(before, axis=1)
    rank1 = jnp.sum(oh1 * before, axis=0, keepdims=True)
    rank2 = jnp.sum(oh2 * before, axis=0, keepdims=True)
    carry_ref[...] = run
    cnt_ref[...] = jnp.broadcast_to(run, cnt_ref.shape)

    route_ref[0:1, :] = e1.astype(I32)
    route_ref[1:2, :] = e2.astype(I32)
    route_ref[2:3, :] = rank1.astype(I32)
    route_ref[3:4, :] = rank2.astype(I32)
    route_ref[4:5, :] = lax.bitcast_convert_type(gate1, I32)
    route_ref[5:6, :] = lax.bitcast_convert_type(gate2, I32)
    route_ref[6:8, :] = jnp.zeros((2, ts), I32)


def _xattn_router(x, g_x, w_q, k_mem, v_mem, layer, w_o, g_f, w_grp, b_grp, w_exp, b_exp):
    b, s, d = x.shape
    nm = k_mem.shape[2]
    t = b * s
    ts = min(TILE_XATTN, s)
    nj = s // ts
    row = lambda v: v.reshape(1, -1)
    const = lambda shape: pl.BlockSpec(shape, lambda i, j: (0,) * len(shape))
    pad = ROUTER_ROWS - N_GROUPS - N_EXPERTS
    w_r = jnp.pad(jnp.concatenate([w_exp, w_grp], axis=1).T, ((0, pad), (0, 0)))
    w_r_hi = w_r.astype(BF16)
    w_r_split = jnp.concatenate([w_r_hi, (w_r - w_r_hi.astype(F32)).astype(BF16)], axis=0)
    b_r = jnp.pad(jnp.concatenate([b_exp, b_grp]), (0, pad)).reshape(ROUTER_ROWS, 1)
    ti = jnp.arange(LANES)
    upper = (ti[:, None] < ti[None, :]).astype(BF16)
    return pl.pallas_call(
        _xattn_router_kernel,
        out_shape=(
            jax.ShapeDtypeStruct(x.shape, F32),
            jax.ShapeDtypeStruct((t, d // 2), U32),
            jax.ShapeDtypeStruct((8, t), I32),
            jax.ShapeDtypeStruct((N_EXPERTS, 128), F32),
        ),
        grid=(b, nj),
        in_specs=[
            pl.BlockSpec((1, ts, d), lambda i, j: (i, j, 0)),
            const((1, d)), const((d, d)),
            pl.BlockSpec((1, 1, nm, d), lambda i, j: (layer, i, 0, 0)),
            pl.BlockSpec((1, 1, nm, d), lambda i, j: (layer, i, 0, 0)),
            const((d, d)), const((1, d)), const((2 * ROUTER_ROWS, d)), const((ROUTER_ROWS, 1)), const((LANES, LANES)),
        ],
        out_specs=(
            pl.BlockSpec((1, ts, d), lambda i, j: (i, j, 0)),
            pl.BlockSpec((ts, d // 2), lambda i, j: (i * nj + j, 0)),
            pl.BlockSpec((8, ts), lambda i, j: (0, i * nj + j)),
            pl.BlockSpec((N_EXPERTS, 128), lambda i, j: (0, 0)),
        ),
        scratch_shapes=[pltpu.VMEM((N_EXPERTS, 1), F32)],
        compiler_params=_cparams(("arbitrary", "arbitrary")),
        name="xattn_router",
    )(x, row(g_x), w_q.astype(BF16), k_mem, v_mem, w_o.astype(BF16), row(g_f), w_r_split, b_r, upper)


def _sc_mesh():
    return plsc.VectorSubcoreMesh(core_axis_name="c", subcore_axis_name="s",
                                  num_cores=SC_CORES, num_subcores=SC_SUBCORES)


def _sc_worker():
    return lax.axis_index("s") * SC_CORES + lax.axis_index("c")


def _dispatch(h_packed, dest, n_rows):
    t, w = h_packed.shape
    chunk = SC_DISPATCH_CHUNK
    per_worker = t // SC_WORKERS
    n_chunks = per_worker // chunk
    assert n_chunks % 2 == 0 and n_chunks * chunk * SC_WORKERS == t
    dest = dest.reshape(TOP_K, SC_WORKERS * n_chunks, chunk)
    rows_buf = pltpu.VMEM((chunk, w), U32)

    @functools.partial(
        pl.kernel, mesh=_sc_mesh(),
        out_type=jax.ShapeDtypeStruct((n_rows, w), U32),
        scratch_types=[pltpu.VMEM((n_chunks, chunk), I32), pltpu.VMEM((n_chunks, chunk), I32), rows_buf, rows_buf,
                       pltpu.SemaphoreType.DMA((2,)), pltpu.SemaphoreType.DMA((2, TOP_K))],
        name="moe_dispatch_sc",
    )
    def run(h_hbm, d0_hbm, d1_hbm, xbuf_hbm, idx0_v, idx1_v, buf_a, buf_b, read_sem, write_sem):
        wid = _sc_worker()
        pltpu.sync_copy(d0_hbm.at[pl.ds(wid * n_chunks, n_chunks)], idx0_v)
        pltpu.sync_copy(d1_hbm.at[pl.ds(wid * n_chunks, n_chunks)], idx1_v)

        @pl.loop(0, n_chunks, step=2)
        def _(i):
            reads = [pltpu.async_copy(h_hbm.at[pl.ds(wid * per_worker + (i + j) * chunk, chunk)], buf, read_sem.at[j])
                     for j, buf in enumerate((buf_a, buf_b))]
            writes = []
            for j, buf in enumerate((buf_a, buf_b)):
                reads[j].wait()
                writes.append(pltpu.async_copy(buf, xbuf_hbm.at[idx0_v.at[i + j]], write_sem.at[j, 0]))
                writes.append(pltpu.async_copy(buf, xbuf_hbm.at[idx1_v.at[i + j]], write_sem.at[j, 1]))
            for copy in writes:
                copy.wait()

    return run(h_packed, dest[0], dest[1])


def _gather_pairs(y_buf, dest):
    t = dest.shape[1]
    w = y_buf.shape[1]
    chunk = SC_GATHER_CHUNK
    per_worker = t // SC_WORKERS
    n_chunks = per_worker // chunk
    assert n_chunks % 2 == 0 and n_chunks * chunk * SC_WORKERS == t
    dest = dest.reshape(TOP_K, SC_WORKERS * n_chunks, chunk)
    out = jax.ShapeDtypeStruct((t, w), U32)
    rows_buf = pltpu.VMEM((chunk, w), U32)

    @functools.partial(
        pl.kernel, mesh=_sc_mesh(),
        out_type=(out, out),
        scratch_types=[pltpu.VMEM((n_chunks, chunk), I32), pltpu.VMEM((n_chunks, chunk), I32),
                       rows_buf, rows_buf, rows_buf, rows_buf, pltpu.SemaphoreType.DMA((2, TOP_K))],
        name="moe_gather_sc",
    )
    def run(y_hbm, d0_hbm, d1_hbm, y0_hbm, y1_hbm, idx0_v, idx1_v, buf_a0, buf_a1, buf_b0, buf_b1, sem):
        wid = _sc_worker()
        pltpu.sync_copy(d0_hbm.at[pl.ds(wid * n_chunks, n_chunks)], idx0_v)
        pltpu.sync_copy(d1_hbm.at[pl.ds(wid * n_chunks, n_chunks)], idx1_v)

        @pl.loop(0, n_chunks, step=2)
        def _(i):
            bufs = ((buf_a0, buf_a1), (buf_b0, buf_b1))
            gathers = [[pltpu.async_copy(y_hbm.at[idx_v.at[i + j]], bufs[j][k], sem.at[j, k])
                        for k, idx_v in enumerate((idx0_v, idx1_v))] for j in range(2)]
            stores = []
            for j in range(2):
                rows = pl.ds(wid * per_worker + (i + j) * chunk, chunk)
                for k, out_hbm in enumerate((y0_hbm, y1_hbm)):
                    gathers[j][k].wait()
                    stores.append(pltpu.async_copy(bufs[j][k], out_hbm.at[rows], sem.at[j, k]))
            for copy in stores:
                copy.wait()

    return run(y_buf, dest[0], dest[1])


def _expert_kernel(be_ref, bf_ref, bv_ref, slot_ref, next_ref, x_ref, wg_hbm, wu_hbm, wd_hbm, y_ref,
                   wg_f32, wu_f32, wd_f32, wg_bf, wu_bf, wd_bf, sems, *, layer):
    blk = pl.program_id(0)
    valid = bv_ref[blk]

    def weight_copies(expert, slot):
        return [pltpu.make_async_copy(src.at[layer, expert], dst.at[slot], sems.at[slot, i])
                for i, (src, dst) in enumerate(((wg_hbm, wg_f32), (wu_hbm, wu_f32), (wd_hbm, wd_f32)))]

    @pl.when(blk == 0)
    def _():
        for copy in weight_copies(be_ref[0], 0):
            copy.start()

    @pl.when(bf_ref[blk] == 1)
    def _():
        slot = slot_ref[blk]
        for copy in weight_copies(be_ref[blk], slot):
            copy.wait()

        @pl.when(next_ref[blk] >= 0)
        def _():
            for copy in weight_copies(next_ref[blk], 1 - slot):
                copy.start()

        wg_bf[...] = wg_f32[slot].astype(BF16)
        wu_bf[...] = wu_f32[slot].astype(BF16)
        wd_bf[...] = wd_f32[slot].astype(BF16)

    @pl.when(valid > 0)
    def _():
        live = lax.broadcasted_iota(I32, x_ref.shape, 0) < valid
        xb = _unpack_bf16_pairs(jnp.where(live, x_ref[...], jnp.uint32(0))).astype(BF16)
        gt = _dot(xb, wg_bf[...])
        up = _dot(xb, wu_bf[...])
        act = (gt * _sigmoid(gt) * up).astype(BF16)
        y_ref[...] = _pack_bf16_pairs(_dot(act, wd_bf[...]))

    @pl.when(valid <= 0)
    def _():
        y_ref[...] = jnp.zeros(y_ref.shape, U32)


def _experts(x_buf, block_expert, block_first, block_valid, w_gate, w_up, w_down, layer):
    n_rows, w = x_buf.shape
    d, de = w_gate.shape[2], w_gate.shape[3]
    bm = MOE_BLOCK_ROWS
    n = n_rows // bm
    block_slot = (jnp.cumsum(block_first) - 1) % 2
    idx = jnp.arange(n, dtype=I32)
    later_first = jnp.concatenate([jnp.where(block_first[1:] == 1, idx[1:], n), jnp.full((1,), n, I32)])
    next_first = lax.cummin(later_first, reverse=True)
    block_next = jnp.where(next_first < n, block_expert[jnp.minimum(next_first, n - 1)], -1).astype(I32)
    any_space = pl.BlockSpec(memory_space=pl.ANY)
    grid_spec = pltpu.PrefetchScalarGridSpec(
        num_scalar_prefetch=5,
        grid=(n,),
        in_specs=[pl.BlockSpec((bm, w), lambda i, *_: (i, 0)), any_space, any_space, any_space],
        out_specs=pl.BlockSpec((bm, w), lambda i, *_: (i, 0)),
        scratch_shapes=[pltpu.VMEM((2, d, de), F32), pltpu.VMEM((2, d, de), F32), pltpu.VMEM((2, de, d), F32),
                        pltpu.VMEM((d, de), BF16), pltpu.VMEM((d, de), BF16), pltpu.VMEM((de, d), BF16),
                        pltpu.SemaphoreType.DMA((2, 3))],
    )
    return pl.pallas_call(
        functools.partial(_expert_kernel, layer=layer),
        out_shape=jax.ShapeDtypeStruct((n_rows, w), U32),
        grid_spec=grid_spec,
        compiler_params=_cparams(("arbitrary",)),
        name="moe_experts",
    )(block_expert, block_first, block_valid, block_slot.astype(I32), block_next, x_buf, w_gate, w_up, w_down)


def _combine_kernel(x_ref, gate_ref, y0_ref, y1_ref, gfin_ref, o_ref, *, final_norm):
    out = _moe_residual(x_ref[...], gate_ref[...], y0_ref[...], y1_ref[...])
    if final_norm:
        out = _rms(out, gfin_ref[...])
    o_ref[...] = out


def _combine(x2, y0, y1, gates, g_final, final_norm):
    t, d = x2.shape
    w = y0.shape[1]
    ts = min(TILE_COMBINE, t)
    return pl.pallas_call(
        functools.partial(_combine_kernel, final_norm=final_norm),
        out_shape=jax.ShapeDtypeStruct((t, d), F32),
        grid=(t // ts,),
        in_specs=[
            pl.BlockSpec((ts, d), lambda i: (i, 0)),
            pl.BlockSpec((ts, TOP_K), lambda i: (i, 0)),
            pl.BlockSpec((ts, w), lambda i: (i, 0)),
            pl.BlockSpec((ts, w), lambda i: (i, 0)),
            pl.BlockSpec((1, d), lambda i: (0, 0)),
        ],
        out_specs=pl.BlockSpec((ts, d), lambda i: (i, 0)),
        compiler_params=_cparams(("arbitrary",)),
        name="moe_combine",
    )(x2, gates, y0, y1, g_final.reshape(1, d))


def _moe_layout(route, counts):
    bm = MOE_BLOCK_ROWS
    t = route.shape[1]
    n_blocks = (t * TOP_K) // bm + N_EXPERTS
    cnt = counts[:, 0].astype(I32)
    padded = (cnt + bm - 1) // bm * bm
    pad_ends = jnp.cumsum(padded)
    pad_off = pad_ends - padded
    experts = jnp.arange(N_EXPERTS, dtype=I32)
    hit = route[0:TOP_K, :, None] == experts
    dest = jnp.sum(jnp.where(hit, pad_off, 0), axis=-1) + route[TOP_K:2 * TOP_K]
    gates = lax.bitcast_convert_type(route[2 * TOP_K:3 * TOP_K], F32).T
    starts = jnp.arange(n_blocks, dtype=I32) * bm
    block_expert = jnp.minimum(jnp.sum((pad_ends[None, :] <= starts[:, None]).astype(I32), axis=1),
                               N_EXPERTS - 1)
    block_first = jnp.concatenate([jnp.ones((1,), I32), (block_expert[1:] != block_expert[:-1]).astype(I32)])
    own = block_expert[:, None] == experts
    block_valid = jnp.clip(jnp.sum(jnp.where(own, cnt + pad_off, 0), axis=1) - starts, 0, bm)
    block_valid = jnp.where(starts < pad_ends[-1], block_valid, 0).astype(I32)
    return dest, gates, block_expert, block_first, block_valid, n_blocks * bm


def kernel(x, mem, norm_mix, norm_xattn, norm_ffn, norm_mem, norm_final, conv_w_in, conv_b_in, conv_w_dw,
           conv_b_dw, conv_ln_g, conv_ln_b, conv_w_out, conv_b_out, gla_w_in, gla_w_a2, gla_b_a, gla_norm_g,
           gla_w_o, xa_w_q, xa_w_kv, xa_w_o, moe_w_grp, moe_b_grp, moe_w_exp, moe_b_exp, moe_w_gate, moe_w_up,
           moe_w_down):
    b, s, d = x.shape
    depth = norm_mix.shape[0]
    k_mem, v_mem = _mem_kv(mem, norm_mem, xa_w_kv)
    moe = None
    for i in range(depth):
        j = i // 2
        if i % 2 == 0:
            if moe is not None:
                x = _combine(x.reshape(b * s, d), moe[1], moe[2], moe[0], norm_final, False).reshape(b, s, d)
            x = _conv_mixer(x, norm_mix[i], conv_w_in[j], conv_b_in[j], conv_w_dw[j], conv_b_dw[j],
                            conv_ln_g[j], conv_ln_b[j], conv_w_out[j], conv_b_out[j])
        else:
            x = _gla_mixer(x, *moe, norm_mix[i], gla_w_in[j], gla_w_a2[j], gla_b_a[j], gla_norm_g[j], gla_w_o[j])
        x2, h_packed, route, counts = _xattn_router(
            x, norm_xattn[i], xa_w_q[i], k_mem, v_mem, i, xa_w_o[i], norm_ffn[i],
            moe_w_grp[i], moe_b_grp[i], moe_w_exp[i], moe_b_exp[i])
        dest, gates, block_expert, block_first, block_valid, n_rows = _moe_layout(route, counts)
        x_buf = _dispatch(h_packed, dest, n_rows)
        y_buf = _experts(x_buf, block_expert, block_first, block_valid, moe_w_gate, moe_w_up, moe_w_down, i)
        y0, y1 = _gather_pairs(y_buf, dest)
        x, moe = x2, (gates, y0, y1)
    return _combine(x.reshape(b * s, d), moe[1], moe[2], moe[0], norm_final, True).reshape(b, s, d)
```

```python
import functools

import jax
import jax.numpy as jnp
from jax import lax
from jax.experimental import pallas as pl
from jax.experimental.pallas import tpu as pltpu
from jax.experimental.pallas import tpu_sc as plsc

F32 = jnp.float32
BF16 = jnp.bfloat16
I32 = jnp.int32
U32 = jnp.uint32

EPS = 1e-6
CONV_KERNEL = 31
CONV_CARRY = 32
CONV_ROWS = 64
CONV_COLS = 256
SUBLANES = 8
LANES = 128
GLA_HEADS = 4
GLA_RANK = 16
GLA_RANK_PAD = 128
GLA_TAU = 16.0
GLA_LEAF = 32
XATTN_HEADS = 4
N_GROUPS = 4
EXPERTS_PER_GROUP = 8
N_EXPERTS = N_GROUPS * EXPERTS_PER_GROUP
ROUTER_ROWS = 40
TOP_K = 2

TILE_CONV = 256
TILE_GLA = 256
TILE_XATTN = 512
TILE_COMBINE = 512
FINAL_PARTS = 2
SC_CORES = 2
SC_SUBCORES = 16
SC_WORKERS = SC_CORES * SC_SUBCORES
SC_DISPATCH_CHUNK = 64
SC_GATHER_CHUNK = 32
MOE_BLOCK_ROWS = 512
VMEM_LIMIT = 56 * 1024 * 1024


def _cparams(sem):
    return pltpu.CompilerParams(dimension_semantics=sem, vmem_limit_bytes=VMEM_LIMIT)


def _rms(x, g):
    return x * lax.rsqrt(jnp.mean(x * x, axis=-1, keepdims=True) + EPS) * g


def _sigmoid(x):
    return 0.5 * jnp.tanh(0.5 * x) + 0.5


def _split_bf16(x):
    hi = x.astype(BF16)
    lo = (x - hi.astype(F32)).astype(BF16)
    return hi, lo


def _dot(a, b):
    return jnp.dot(a, b, preferred_element_type=F32)


def _dot_nt(a, b):
    return lax.dot_general(a, b, (((1,), (1,)), ((), ())), preferred_element_type=F32)


def _dot_tn(a, b):
    return lax.dot_general(a, b, (((0,), (0,)), ((), ())), preferred_element_type=F32)


def _pack_bf16_pairs(x):
    w = x.shape[1] // 2
    hi = lax.bitcast_convert_type(x[:, :w].astype(BF16).astype(F32), U32)
    lo = lax.bitcast_convert_type(x[:, w:].astype(BF16).astype(F32), U32)
    return hi | (lo >> 16)


def _unpack_bf16_pairs(p):
    hi = lax.bitcast_convert_type(p & jnp.uint32(0xFFFF0000), F32)
    lo = lax.bitcast_convert_type(p << 16, F32)
    return jnp.concatenate([hi, lo], axis=1)


def _memkv_kernel(mem_ref, g_ref, w_ref, k_ref, v_ref):
    d = mem_ref.shape[-1]
    mn = _rms(mem_ref[0], g_ref[...]).astype(BF16)
    kv = _dot(mn, w_ref[0])
    k_ref[0, 0] = kv[:, :d].astype(BF16)
    v_ref[0, 0] = kv[:, d:].astype(BF16)


def _mem_kv(mem, norm_mem, w_kv):
    b, nm, d = mem.shape
    depth = w_kv.shape[0]
    out = jax.ShapeDtypeStruct((depth, b, nm, d), BF16)
    return pl.pallas_call(
        _memkv_kernel,
        out_shape=(out, out),
        grid=(depth, b),
        in_specs=[
            pl.BlockSpec((1, nm, d), lambda l, i: (i, 0, 0)),
            pl.BlockSpec((1, d), lambda l, i: (0, 0)),
            pl.BlockSpec((1, d, 2 * d), lambda l, i: (l, 0, 0)),
        ],
        out_specs=(
            pl.BlockSpec((1, 1, nm, d), lambda l, i: (l, i, 0, 0)),
            pl.BlockSpec((1, 1, nm, d), lambda l, i: (l, i, 0, 0)),
        ),
        compiler_params=_cparams(("arbitrary", "arbitrary")),
        name="mem_kv",
    )(mem, norm_mem.reshape(1, d), w_kv.astype(BF16))


def _conv_kernel(x_ref, g_ref, win_ref, bin_ref, wdw_ref, bdw_ref, lng_ref, lnb_ref, wout_ref, bout_ref,
                 o_ref, ext_ref, conv_ref):
    ts, d = x_ref.shape[1], x_ref.shape[2]

    @pl.when(pl.program_id(1) == 0)
    def _():
        ext_ref[...] = jnp.zeros(ext_ref.shape, F32)

    x = x_ref[0]
    h = _rms(x, g_ref[...]).astype(BF16)
    u = _dot(h, win_ref[...]) + bin_ref[...]
    glu = u[:, :d] * _sigmoid(u[:, d:])
    for b in range(SUBLANES):
        ext_ref[b, CONV_CARRY - b:CONV_CARRY - b + ts, :] = glu

    first = CONV_CARRY - (CONV_KERNEL - 1)

    def chunk(i, carry):
        r0 = pl.multiple_of(i * CONV_ROWS, CONV_ROWS)
        for c0 in range(0, d, CONV_COLS):
            cols = slice(c0, c0 + CONV_COLS)
            acc = [jnp.zeros((SUBLANES, CONV_COLS), F32) for _ in range(CONV_ROWS // SUBLANES)]
            for k in range(CONV_KERNEL):
                b = (first + k) % SUBLANES
                wk = wdw_ref[k * SUBLANES:(k + 1) * SUBLANES, cols]
                for j in range(CONV_ROWS // SUBLANES):
                    rows = pl.ds(r0 + (first + k - b) + j * SUBLANES, SUBLANES)
                    acc[j] = acc[j] + ext_ref[b, rows, cols] * wk
            for j in range(CONV_ROWS // SUBLANES):
                conv_ref[pl.ds(r0 + j * SUBLANES, SUBLANES), cols] = acc[j]
        return carry

    lax.fori_loop(0, ts // CONV_ROWS, chunk, 0)
    for b in range(SUBLANES):
        ext_ref[b, 0:CONV_CARRY, :] = ext_ref[b, ts:ts + CONV_CARRY, :]

    c = conv_ref[...] + bdw_ref[...]
    mu = jnp.mean(c, axis=-1, keepdims=True)
    cc = c - mu
    var = jnp.mean(cc * cc, axis=-1, keepdims=True)
    un = cc * lax.rsqrt(var + EPS) * lng_ref[...] + lnb_ref[...]
    act = (un * _sigmoid(un)).astype(BF16)
    o_ref[0] = x + _dot(act, wout_ref[...]) + bout_ref[...]


def _conv_mixer(x, g, w_in, b_in, w_dw, b_dw, ln_g, ln_b, w_out, b_out):
    b, s, d = x.shape
    ts = min(TILE_CONV, s)
    row = lambda v: v.reshape(1, -1)
    const = lambda shape: pl.BlockSpec(shape, lambda i, j: (0,) * len(shape))
    return pl.pallas_call(
        _conv_kernel,
        out_shape=jax.ShapeDtypeStruct(x.shape, F32),
        grid=(b, s // ts),
        in_specs=[
            pl.BlockSpec((1, ts, d), lambda i, j: (i, j, 0)),
            const((1, d)), const((d, 2 * d)), const((1, 2 * d)), const((CONV_KERNEL * SUBLANES, d)), const((1, d)),
            const((1, d)), const((1, d)), const((d, d)), const((1, d)),
        ],
        out_specs=pl.BlockSpec((1, ts, d), lambda i, j: (i, j, 0)),
        scratch_shapes=[pltpu.VMEM((SUBLANES, CONV_CARRY + ts, d), F32), pltpu.VMEM((ts, d), F32)],
        compiler_params=_cparams(("arbitrary", "arbitrary")),
        name="conv_mixer",
    )(x, row(g), w_in.astype(BF16), row(b_in), jnp.repeat(w_dw, SUBLANES, axis=0), row(b_dw), row(ln_g), row(ln_b),
      w_out.astype(BF16), row(b_out))


def _gla_levels(ts):
    sizes = [GLA_LEAF]
    while sizes[-1] < ts:
        sizes.append(sizes[-1] * 2)
    return sizes


def _moe_residual(x, gates, y0, y1):
    return x + _unpack_bf16_pairs(y0) * gates[:, 0:1] + _unpack_bf16_pairs(y1) * gates[:, 1:2]


def _gla_kernel(x_ref, gate_ref, y0_ref, y1_ref, g_ref, wq_ref, wk_ref, wv_ref, wa_ref, wr_ref, wa2_ref, ba_ref,
                ng_ref, wo_ref, o_ref, state_ref):
    ts, d = x_ref.shape[1], x_ref.shape[2]
    dk = wq_ref.shape[1]
    dkh = dk // GLA_HEADS
    dvh = d // GLA_HEADS

    @pl.when(pl.program_id(1) == 0)
    def _():
        state_ref[...] = jnp.zeros(state_ref.shape, F32)

    x = _moe_residual(x_ref[0], gate_ref[...], y0_ref[...], y1_ref[...])
    h = _rms(x, g_ref[...]).astype(BF16)
    q = _dot(h, wq_ref[...]) * (dkh ** -0.5)
    k = _dot(h, wk_ref[...])
    v = _dot(h, wv_ref[...]).astype(BF16)
    r = _dot(h, wr_ref[...])
    a_hi, a_lo = _split_bf16(_dot(h, wa_ref[...]))
    w2_hi, w2_lo = wa2_ref[0], wa2_ref[1]
    z = _dot(a_hi, w2_hi) + _dot(a_lo, w2_hi) + _dot(a_hi, w2_lo) + ba_ref[...]
    log_a = -(jnp.maximum(-z, 0.0) + jnp.log(1.0 + jnp.exp(-jnp.abs(z)))) * (1.0 / GLA_TAU)

    row = lax.broadcasted_iota(I32, (ts, ts), 0)
    col = lax.broadcasted_iota(I32, (ts, ts), 1)
    tri = jnp.where(col <= row, 1.0, 0.0).astype(BF16)
    la_hi, la_lo = _split_bf16(log_a)
    bcum = _dot(tri, la_hi) + _dot(tri, la_lo)
    b_last = bcum[ts - 1:ts, :]

    q_in = (q * jnp.exp(bcum)).astype(BF16)
    k_out = (k * jnp.exp(b_last - bcum)).astype(BF16)

    sizes = _gla_levels(ts)
    scores = [None] * GLA_HEADS
    for lvl, size in enumerate(sizes):
        half = size // 2
        same_block = (row & -size) == (col & -size)
        if lvl == 0:
            pair = same_block & (col <= row)
            q_ok = k_ok = None
        else:
            pair = same_block & ((row & (size - 1)) >= half) & ((col & (size - 1)) < half)
            pos = lax.broadcasted_iota(I32, (ts, dk), 0) & (size - 1)
            q_ok = pos >= half
            k_ok = pos < half
        ref = jnp.concatenate(
            [jnp.broadcast_to(bcum[r0 + half:r0 + half + 1, :], (size, dk)) for r0 in range(0, ts, size)], axis=0)
        ql = q * jnp.exp(bcum - ref)
        kl = k * jnp.exp(ref - bcum)
        if q_ok is not None:
            ql = jnp.where(q_ok, ql, 0.0)
            kl = jnp.where(k_ok, kl, 0.0)
        ql = ql.astype(BF16)
        kl = kl.astype(BF16)
        for hd in range(GLA_HEADS):
            c0 = hd * dkh
            a = _dot_nt(ql[:, c0:c0 + dkh], kl[:, c0:c0 + dkh])
            a = jnp.where(pair, a, 0.0)
            scores[hd] = a if scores[hd] is None else scores[hd] + a

    outs = []
    for hd in range(GLA_HEADS):
        c0 = hd * dkh
        v_h = v[:, hd * dvh:(hd + 1) * dvh]
        st = state_ref[hd]
        o_h = _dot(scores[hd].astype(BF16), v_h) + _dot_nt(q_in[:, c0:c0 + dkh], st.astype(BF16))
        decay = jnp.exp(b_last[:, c0:c0 + dkh])
        state_ref[hd] = st * decay + _dot_tn(v_h, k_out[:, c0:c0 + dkh])
        o_h = o_h * lax.rsqrt(jnp.mean(o_h * o_h, axis=-1, keepdims=True) + EPS) * ng_ref[...]
        outs.append(o_h)
    o = jnp.concatenate(outs, axis=1) * (r * _sigmoid(r))
    o_ref[0] = x + _dot(o.astype(BF16), wo_ref[...])


def _gla_mixer(x, gates, y0, y1, g, w_in, w_a2, b_a, norm_g, w_o):
    b, s, d = x.shape
    dk = w_a2.shape[1]
    ts = min(TILE_GLA, s)
    nj = s // ts
    row = lambda v: v.reshape(1, -1)
    const = lambda shape: pl.BlockSpec(shape, lambda i, j: (0,) * len(shape))
    wq = w_in[:, :dk].astype(BF16)
    wk = w_in[:, dk:2 * dk].astype(BF16)
    wv = w_in[:, 2 * dk:2 * dk + d].astype(BF16)
    wa = jnp.pad(w_in[:, 2 * dk + d:2 * dk + d + GLA_RANK], ((0, 0), (0, GLA_RANK_PAD - GLA_RANK))).astype(BF16)
    wr = w_in[:, 2 * dk + d + GLA_RANK:].astype(BF16)
    wa2 = jnp.pad(w_a2, ((0, GLA_RANK_PAD - GLA_RANK), (0, 0)))
    wa2_hi = wa2.astype(BF16)
    wa2_lo = (wa2 - wa2_hi.astype(F32)).astype(BF16)
    wa2_split = jnp.stack([wa2_hi, wa2_lo])
    dvh = d // GLA_HEADS
    return pl.pallas_call(
        _gla_kernel,
        out_shape=jax.ShapeDtypeStruct(x.shape, F32),
        grid=(b, s // ts),
        in_specs=[
            pl.BlockSpec((1, ts, d), lambda i, j: (i, j, 0)),
            pl.BlockSpec((ts, TOP_K), lambda i, j: (i * nj + j, 0)),
            pl.BlockSpec((ts, d // 2), lambda i, j: (i * nj + j, 0)),
            pl.BlockSpec((ts, d // 2), lambda i, j: (i * nj + j, 0)),
            const((1, d)), const((d, dk)), const((d, dk)), const((d, d)), const((d, GLA_RANK_PAD)),
            const((d, d)), const((2, GLA_RANK_PAD, dk)), const((1, dk)), const((1, dvh)), const((d, d)),
        ],
        out_specs=pl.BlockSpec((1, ts, d), lambda i, j: (i, j, 0)),
        scratch_shapes=[pltpu.VMEM((GLA_HEADS, dvh, dk // GLA_HEADS), F32)],
        compiler_params=_cparams(("arbitrary", "arbitrary")),
        name="gla_mixer",
    )(x, gates, y0, y1, row(g), wq, wk, wv, wa, wr, wa2_split, row(b_a), row(norm_g), w_o.astype(BF16))


def _xattn_router_kernel(x_ref, gx_ref, wq_ref, k_ref, v_ref, wo_ref, gf_ref, wr_ref, br_ref, upper_ref,
                         x_out_ref, h_out_ref, route_ref, cnt_ref, carry_ref):
    ts, d = x_ref.shape[1], x_ref.shape[2]
    hd = d // XATTN_HEADS
    first = (pl.program_id(0) == 0) & (pl.program_id(1) == 0)

    @pl.when(first)
    def _():
        carry_ref[...] = jnp.zeros(carry_ref.shape, F32)

    x = x_ref[0]
    h = _rms(x, gx_ref[...]).astype(BF16)
    q = (_dot(h, wq_ref[...]) * (hd ** -0.5)).astype(BF16)
    k = k_ref[0, 0]
    v = v_ref[0, 0]
    outs = []
    for a in range(XATTN_HEADS):
        sl = slice(a * hd, (a + 1) * hd)
        s = _dot_nt(q[:, sl], k[:, sl])
        p = jnp.exp(s - jnp.max(s, axis=-1, keepdims=True))
        o = _dot(p.astype(BF16), v[:, sl]) / jnp.sum(p, axis=-1, keepdims=True)
        outs.append(o)
    att = jnp.concatenate(outs, axis=1).astype(BF16)
    x2 = x + _dot(att, wo_ref[...])
    x_out_ref[0] = x2

    hf = _rms(x2, gf_ref[...])
    h_out_ref[...] = _pack_bf16_pairs(hf)

    h_hi, h_lo = _split_bf16(hf)
    both = _dot_nt(wr_ref[...], h_hi)
    logits = (both[0:ROUTER_ROWS, :] + both[ROUTER_ROWS:2 * ROUTER_ROWS, :]
              + _dot_nt(wr_ref[0:ROUTER_ROWS, :], h_lo) + br_ref[...])
    gl = logits[N_EXPERTS:N_EXPERTS + N_GROUPS, :]
    gi = lax.broadcasted_iota(I32, gl.shape, 0).astype(F32)
    gmax = jnp.max(gl, axis=0, keepdims=True)
    g_sel = jnp.min(jnp.where(gl == gmax, gi, float(N_GROUPS)), axis=0, keepdims=True)
    pg_sel = 1.0 / jnp.sum(jnp.exp(gl - gmax), axis=0, keepdims=True)

    el = jnp.zeros((EXPERTS_PER_GROUP, ts), F32)
    for gidx in range(N_GROUPS):
        lo = gidx * EXPERTS_PER_GROUP
        el = jnp.where(g_sel == float(gidx), logits[lo:lo + EXPERTS_PER_GROUP, :], el)
    ei = lax.broadcasted_iota(I32, el.shape, 0).astype(F32)
    m1 = jnp.max(el, axis=0, keepdims=True)
    i1 = jnp.min(jnp.where(el == m1, ei, float(EXPERTS_PER_GROUP)), axis=0, keepdims=True)
    rest = jnp.where(ei == i1, -jnp.inf, el)
    m2 = jnp.max(rest, axis=0, keepdims=True)
    i2 = jnp.min(jnp.where(rest == m2, ei, float(EXPERTS_PER_GROUP)), axis=0, keepdims=True)
    ratio = jnp.exp(m2 - m1)
    gate1 = pg_sel / (1.0 + ratio)
    gate2 = pg_sel * ratio / (1.0 + ratio)
    e1 = g_sel * float(EXPERTS_PER_GROUP) + i1
    e2 = g_sel * float(EXPERTS_PER_GROUP) + i2

    xi = lax.broadcasted_iota(I32, (N_EXPERTS, ts), 0).astype(F32)
    oh1 = jnp.where(xi == e1, 1.0, 0.0)
    oh2 = jnp.where(xi == e2, 1.0, 0.0)
    oh = oh1 + oh2
    n_blk = ts // LANES
    stacked = jnp.concatenate([oh[:, c * LANES:(c + 1) * LANES] for c in range(n_blk)], axis=0)
    within = _dot(stacked.astype(BF16), upper_ref[...])
    totals = jnp.sum(stacked, axis=1, keepdims=True)
    run = carry_ref[...]
    before = []
    for c in range(n_blk):
        before.append(within[c * N_EXPERTS:(c + 1) * N_EXPERTS, :] + run)
        run = run + totals[c * N_EXPERTS:(c + 1) * N_EXPERTS, :]
    before = jnp.concatenate(before, axis=1)
    rank1 = jnp.sum(oh1 * before, axis=0, keepdims=True)
    rank2 = jnp.sum(oh2 * before, axis=0, keepdims=True)
    carry_ref[...] = run
    cnt_ref[...] = jnp.broadcast_to(run, cnt_ref.shape)

    route_ref[0:1, :] = e1.astype(I32)
    route_ref[1:2, :] = e2.astype(I32)
    route_ref[2:3, :] = rank1.astype(I32)
    route_ref[3:4, :] = rank2.astype(I32)
    route_ref[4:5, :] = lax.bitcast_convert_type(gate1, I32)
    route_ref[5:6, :] = lax.bitcast_convert_type(gate2, I32)
    route_ref[6:8, :] = jnp.zeros((2, ts), I32)


def _xattn_router(x, g_x, w_q, k_mem, v_mem, layer, w_o, g_f, w_grp, b_grp, w_exp, b_exp):
    b, s, d = x.shape
    nm = k_mem.shape[2]
    t = b * s
    ts = min(TILE_XATTN, s)
    nj = s // ts
    row = lambda v: v.reshape(1, -1)
    const = lambda shape: pl.BlockSpec(shape, lambda i, j: (0,) * len(shape))
    pad = ROUTER_ROWS - N_GROUPS - N_EXPERTS
    w_r = jnp.pad(jnp.concatenate([w_exp, w_grp], axis=1).T, ((0, pad), (0, 0)))
    w_r_hi = w_r.astype(BF16)
    w_r_split = jnp.concatenate([w_r_hi, (w_r - w_r_hi.astype(F32)).astype(BF16)], axis=0)
    b_r = jnp.pad(jnp.concatenate([b_exp, b_grp]), (0, pad)).reshape(ROUTER_ROWS, 1)
    ti = jnp.arange(LANES)
    upper = (ti[:, None] < ti[None, :]).astype(BF16)
    return pl.pallas_call(
        _xattn_router_kernel,
        out_shape=(
            jax.ShapeDtypeStruct(x.shape, F32),
            jax.ShapeDtypeStruct((t, d // 2), U32),
            jax.ShapeDtypeStruct((8, t), I32),
            jax.ShapeDtypeStruct((N_EXPERTS, 128), F32),
        ),
        grid=(b, nj),
        in_specs=[
            pl.BlockSpec((1, ts, d), lambda i, j: (i, j, 0)),
            const((1, d)), const((d, d)),
            pl.BlockSpec((1, 1, nm, d), lambda i, j: (layer, i, 0, 0)),
            pl.BlockSpec((1, 1, nm, d), lambda i, j: (layer, i, 0, 0)),
            const((d, d)), const((1, d)), const((2 * ROUTER_ROWS, d)), const((ROUTER_ROWS, 1)), const((LANES, LANES)),
        ],
        out_specs=(
            pl.BlockSpec((1, ts, d), lambda i, j: (i, j, 0)),
            pl.BlockSpec((ts, d // 2), lambda i, j: (i * nj + j, 0)),
            pl.BlockSpec((8, ts), lambda i, j: (0, i * nj + j)),
            pl.BlockSpec((N_EXPERTS, 128), lambda i, j: (0, 0)),
        ),
        scratch_shapes=[pltpu.VMEM((N_EXPERTS, 1), F32)],
        compiler_params=_cparams(("arbitrary", "arbitrary")),
        name="xattn_router",
    )(x, row(g_x), w_q.astype(BF16), k_mem, v_mem, w_o.astype(BF16), row(g_f), w_r_split, b_r, upper)


def _sc_mesh():
    return plsc.VectorSubcoreMesh(core_axis_name="c", subcore_axis_name="s",
                                  num_cores=SC_CORES, num_subcores=SC_SUBCORES)


def _sc_worker():
    return lax.axis_index("s") * SC_CORES + lax.axis_index("c")


def _dispatch(h_packed, dest, n_rows):
    t, w = h_packed.shape
    chunk = SC_DISPATCH_CHUNK
    per_worker = t // SC_WORKERS
    n_chunks = per_worker // chunk
    assert n_chunks % 2 == 0 and n_chunks * chunk * SC_WORKERS == t
    dest = dest.reshape(TOP_K, SC_WORKERS * n_chunks, chunk)
    rows_buf = pltpu.VMEM((chunk, w), U32)

    @functools.partial(
        pl.kernel, mesh=_sc_mesh(),
        out_type=jax.ShapeDtypeStruct((n_rows, w), U32),
        scratch_types=[pltpu.VMEM((n_chunks, chunk), I32), pltpu.VMEM((n_chunks, chunk), I32), rows_buf, rows_buf,
                       pltpu.SemaphoreType.DMA((2,)), pltpu.SemaphoreType.DMA((2, TOP_K))],
        name="moe_dispatch_sc",
    )
    def run(h_hbm, d0_hbm, d1_hbm, xbuf_hbm, idx0_v, idx1_v, buf_a, buf_b, read_sem, write_sem):
        wid = _sc_worker()
        pltpu.sync_copy(d0_hbm.at[pl.ds(wid * n_chunks, n_chunks)], idx0_v)
        pltpu.sync_copy(d1_hbm.at[pl.ds(wid * n_chunks, n_chunks)], idx1_v)

        @pl.loop(0, n_chunks, step=2)
        def _(i):
            reads = [pltpu.async_copy(h_hbm.at[pl.ds(wid * per_worker + (i + j) * chunk, chunk)], buf, read_sem.at[j])
                     for j, buf in enumerate((buf_a, buf_b))]
            writes = []
            for j, buf in enumerate((buf_a, buf_b)):
                reads[j].wait()
                writes.append(pltpu.async_copy(buf, xbuf_hbm.at[idx0_v.at[i + j]], write_sem.at[j, 0]))
                writes.append(pltpu.async_copy(buf, xbuf_hbm.at[idx1_v.at[i + j]], write_sem.at[j, 1]))
            for copy in writes:
                copy.wait()

    return run(h_packed, dest[0], dest[1])


def _gather_pairs(y_buf, dest):
    t = dest.shape[1]
    w = y_buf.shape[1]
    chunk = SC_GATHER_CHUNK
    per_worker = t // SC_WORKERS
    n_chunks = per_worker // chunk
    assert n_chunks % 2 == 0 and n_chunks * chunk * SC_WORKERS == t
    dest = dest.reshape(TOP_K, SC_WORKERS * n_chunks, chunk)
    out = jax.ShapeDtypeStruct((t, w), U32)
    rows_buf = pltpu.VMEM((chunk, w), U32)

    @functools.partial(
        pl.kernel, mesh=_sc_mesh(),
        out_type=(out, out),
        scratch_types=[pltpu.VMEM((n_chunks, chunk), I32), pltpu.VMEM((n_chunks, chunk), I32),
                       rows_buf, rows_buf, rows_buf, rows_buf, pltpu.SemaphoreType.DMA((2, TOP_K))],
        name="moe_gather_sc",
    )
    def run(y_hbm, d0_hbm, d1_hbm, y0_hbm, y1_hbm, idx0_v, idx1_v, buf_a0, buf_a1, buf_b0, buf_b1, sem):
        wid = _sc_worker()
        pltpu.sync_copy(d0_hbm.at[pl.ds(wid * n_chunks, n_chunks)], idx0_v)
        pltpu.sync_copy(d1_hbm.at[pl.ds(wid * n_chunks, n_chunks)], idx1_v)

        @pl.loop(0, n_chunks, step=2)
        def _(i):
            bufs = ((buf_a0, buf_a1), (buf_b0, buf_b1))
            gathers = [[pltpu.async_copy(y_hbm.at[idx_v.at[i + j]], bufs[j][k], sem.at[j, k])
                        for k, idx_v in enumerate((idx0_v, idx1_v))] for j in range(2)]
            stores = []
            for j in range(2):
                rows = pl.ds(wid * per_worker + (i + j) * chunk, chunk)
                for k, out_hbm in enumerate((y0_hbm, y1_hbm)):
                    gathers[j][k].wait()
                    stores.append(pltpu.async_copy(bufs[j][k], out_hbm.at[rows], sem.at[j, k]))
            for copy in stores:
                copy.wait()

    return run(y_buf, dest[0], dest[1])


def _expert_kernel(be_ref, bf_ref, bv_ref, slot_ref, next_ref, x_ref, wg_hbm, wu_hbm, wd_hbm, y_ref,
                   wg_f32, wu_f32, wd_f32, wg_bf, wu_bf, wd_bf, sems, *, layer):
    blk = pl.program_id(0)
    valid = bv_ref[blk]

    def weight_copies(expert, slot):
        return [pltpu.make_async_copy(src.at[layer, expert], dst.at[slot], sems.at[slot, i])
                for i, (src, dst) in enumerate(((wg_hbm, wg_f32), (wu_hbm, wu_f32), (wd_hbm, wd_f32)))]

    @pl.when(blk == 0)
    def _():
        for copy in weight_copies(be_ref[0], 0):
            copy.start()

    @pl.when(bf_ref[blk] == 1)
    def _():
        slot = slot_ref[blk]
        for copy in weight_copies(be_ref[blk], slot):
            copy.wait()

        @pl.when(next_ref[blk] >= 0)
        def _():
            for copy in weight_copies(next_ref[blk], 1 - slot):
                copy.start()

        wg_bf[...] = wg_f32[slot].astype(BF16)
        wu_bf[...] = wu_f32[slot].astype(BF16)
        wd_bf[...] = wd_f32[slot].astype(BF16)

    @pl.when(valid > 0)
    def _():
        live = lax.broadcasted_iota(I32, x_ref.shape, 0) < valid
        xb = _unpack_bf16_pairs(jnp.where(live, x_ref[...], jnp.uint32(0))).astype(BF16)
        gt = _dot(xb, wg_bf[...])
        up = _dot(xb, wu_bf[...])
        act = (gt * _sigmoid(gt) * up).astype(BF16)
        y_ref[...] = _pack_bf16_pairs(_dot(act, wd_bf[...]))

    @pl.when(valid <= 0)
    def _():
        y_ref[...] = jnp.zeros(y_ref.shape, U32)


def _experts(x_buf, block_expert, block_first, block_valid, w_gate, w_up, w_down, layer):
    n_rows, w = x_buf.shape
    d, de = w_gate.shape[2], w_gate.shape[3]
    bm = MOE_BLOCK_ROWS
    n = n_rows // bm
    block_slot = (jnp.cumsum(block_first) - 1) % 2
    idx = jnp.arange(n, dtype=I32)
    later_first = jnp.concatenate([jnp.where(block_first[1:] == 1, idx[1:], n), jnp.full((1,), n, I32)])
    next_first = lax.cummin(later_first, reverse=True)
    block_next = jnp.where(next_first < n, block_expert[jnp.minimum(next_first, n - 1)], -1).astype(I32)
    any_space = pl.BlockSpec(memory_space=pl.ANY)
    grid_spec = pltpu.PrefetchScalarGridSpec(
        num_scalar_prefetch=5,
        grid=(n,),
        in_specs=[pl.BlockSpec((bm, w), lambda i, *_: (i, 0)), any_space, any_space, any_space],
        out_specs=pl.BlockSpec((bm, w), lambda i, *_: (i, 0)),
        scratch_shapes=[pltpu.VMEM((2, d, de), F32), pltpu.VMEM((2, d, de), F32), pltpu.VMEM((2, de, d), F32),
                        pltpu.VMEM((d, de), BF16), pltpu.VMEM((d, de), BF16), pltpu.VMEM((de, d), BF16),
                        pltpu.SemaphoreType.DMA((2, 3))],
    )
    return pl.pallas_call(
        functools.partial(_expert_kernel, layer=layer),
        out_shape=jax.ShapeDtypeStruct((n_rows, w), U32),
        grid_spec=grid_spec,
        compiler_params=_cparams(("arbitrary",)),
        name="moe_experts",
    )(block_expert, block_first, block_valid, block_slot.astype(I32), block_next, x_buf, w_gate, w_up, w_down)


def _combine_kernel(x_ref, gate_ref, y0_ref, y1_ref, gfin_ref, *rest, final_norm):
    o_ref = rest[-1]
    out = _moe_residual(x_ref[...], gate_ref[...], y0_ref[...], y1_ref[...])
    if final_norm:
        out = _rms(out, gfin_ref[...])
    o_ref[...] = out


def _combine(x2, y0, y1, gates, g_final, final_norm, first_row=0, out_so_far=None):
    t, d = x2.shape
    rows, w = y0.shape
    ts = min(TILE_COMBINE, rows)
    off = first_row // ts
    in_specs = [
        pl.BlockSpec((ts, d), lambda i: (i + off, 0)),
        pl.BlockSpec((ts, TOP_K), lambda i: (i + off, 0)),
        pl.BlockSpec((ts, w), lambda i: (i, 0)),
        pl.BlockSpec((ts, w), lambda i: (i, 0)),
        pl.BlockSpec((1, d), lambda i: (0, 0)),
    ]
    args = [x2, gates, y0, y1, g_final.reshape(1, d)]
    aliases = {}
    if out_so_far is not None:
        in_specs.append(pl.BlockSpec(memory_space=pl.ANY))
        args.append(out_so_far)
        aliases = {len(args) - 1: 0}
    return pl.pallas_call(
        functools.partial(_combine_kernel, final_norm=final_norm),
        out_shape=jax.ShapeDtypeStruct((t, d), F32),
        grid=(rows // ts,),
        in_specs=in_specs,
        out_specs=pl.BlockSpec((ts, d), lambda i: (i + off, 0)),
        input_output_aliases=aliases,
        compiler_params=_cparams(("arbitrary",)),
        name="moe_combine",
    )(*args)


def _moe_layout(route, counts):
    bm = MOE_BLOCK_ROWS
    t = route.shape[1]
    n_blocks = (t * TOP_K) // bm + N_EXPERTS
    cnt = counts[:, 0].astype(I32)
    padded = (cnt + bm - 1) // bm * bm
    pad_ends = jnp.cumsum(padded)
    pad_off = pad_ends - padded
    experts = jnp.arange(N_EXPERTS, dtype=I32)
    hit = route[0:TOP_K, :, None] == experts
    dest = jnp.sum(jnp.where(hit, pad_off, 0), axis=-1) + route[TOP_K:2 * TOP_K]
    gates = lax.bitcast_convert_type(route[2 * TOP_K:3 * TOP_K], F32).T
    starts = jnp.arange(n_blocks, dtype=I32) * bm
    block_expert = jnp.minimum(jnp.sum((pad_ends[None, :] <= starts[:, None]).astype(I32), axis=1),
                               N_EXPERTS - 1)
    block_first = jnp.concatenate([jnp.ones((1,), I32), (block_expert[1:] != block_expert[:-1]).astype(I32)])
    own = block_expert[:, None] == experts
    block_valid = jnp.clip(jnp.sum(jnp.where(own, cnt + pad_off, 0), axis=1) - starts, 0, bm)
    block_valid = jnp.where(starts < pad_ends[-1], block_valid, 0).astype(I32)
    return dest, gates, block_expert, block_first, block_valid, n_blocks * bm


def kernel(x, mem, norm_mix, norm_xattn, norm_ffn, norm_mem, norm_final, conv_w_in, conv_b_in, conv_w_dw,
           conv_b_dw, conv_ln_g, conv_ln_b, conv_w_out, conv_b_out, gla_w_in, gla_w_a2, gla_b_a, gla_norm_g,
           gla_w_o, xa_w_q, xa_w_kv, xa_w_o, moe_w_grp, moe_b_grp, moe_w_exp, moe_b_exp, moe_w_gate, moe_w_up,
           moe_w_down):
    b, s, d = x.shape
    depth = norm_mix.shape[0]
    k_mem, v_mem = _mem_kv(mem, norm_mem, xa_w_kv)
    moe = None
    for i in range(depth):
        j = i // 2
        if i % 2 == 0:
            if moe is not None:
                x = _combine(x.reshape(b * s, d), moe[1], moe[2], moe[0], norm_final, False).reshape(b, s, d)
            x = _conv_mixer(x, norm_mix[i], conv_w_in[j], conv_b_in[j], conv_w_dw[j], conv_b_dw[j],
                            conv_ln_g[j], conv_ln_b[j], conv_w_out[j], conv_b_out[j])
        else:
            x = _gla_mixer(x, *moe, norm_mix[i], gla_w_in[j], gla_w_a2[j], gla_b_a[j], gla_norm_g[j], gla_w_o[j])
        x2, h_packed, route, counts = _xattn_router(
            x, norm_xattn[i], xa_w_q[i], k_mem, v_mem, i, xa_w_o[i], norm_ffn[i],
            moe_w_grp[i], moe_b_grp[i], moe_w_exp[i], moe_b_exp[i])
        dest, gates, block_expert, block_first, block_valid, n_rows = _moe_layout(route, counts)
        x_buf = _dispatch(h_packed, dest, n_rows)
        y_buf = _experts(x_buf, block_expert, block_first, block_valid, moe_w_gate, moe_w_up, moe_w_down, i)
        if i < depth - 1:
            y0, y1 = _gather_pairs(y_buf, dest)
            x, moe = x2, (gates, y0, y1)
    part = (b * s) // FINAL_PARTS
    out = None
    for p in range(FINAL_PARTS):
        y0, y1 = _gather_pairs(y_buf, dest[:, p * part:(p + 1) * part])
        out = _combine(x2.reshape(b * s, d), y0, y1, gates, norm_final, True, p * part, out)
    return out.reshape(b, s, d)
```

```python
import functools

import jax
import jax.numpy as jnp
from jax import lax
from jax.experimental import pallas as pl
from jax.experimental.pallas import tpu as pltpu
from jax.experimental.pallas import tpu_sc as plsc

F32 = jnp.float32
BF16 = jnp.bfloat16
I32 = jnp.int32
U32 = jnp.uint32

EPS = 1e-6
CONV_KERNEL = 31
CONV_CARRY = 32
CONV_ROWS = 64
CONV_COLS = 256
SUBLANES = 8
LANES = 128
GLA_HEADS = 4
GLA_RANK = 16
GLA_RANK_PAD = 128
GLA_TAU = 16.0
GLA_LEAF = 32
XATTN_HEADS = 4
N_GROUPS = 4
EXPERTS_PER_GROUP = 8
N_EXPERTS = N_GROUPS * EXPERTS_PER_GROUP
ROUTER_ROWS = 40
TOP_K = 2

TILE_CONV = 256
TILE_GLA = 256
TILE_XATTN = 512
TILE_COMBINE = 512
BATCH_PARTS = 2
SC_CORES = 2
SC_SUBCORES = 16
SC_WORKERS = SC_CORES * SC_SUBCORES
SC_DISPATCH_CHUNK = 64
SC_GATHER_CHUNK = 32
MOE_BLOCK_ROWS = 512
VMEM_LIMIT = 56 * 1024 * 1024


def _cparams(sem):
    return pltpu.CompilerParams(dimension_semantics=sem, vmem_limit_bytes=VMEM_LIMIT)


def _rms(x, g):
    return x * lax.rsqrt(jnp.mean(x * x, axis=-1, keepdims=True) + EPS) * g


def _sigmoid(x):
    return 0.5 * jnp.tanh(0.5 * x) + 0.5


def _split_bf16(x):
    hi = x.astype(BF16)
    lo = (x - hi.astype(F32)).astype(BF16)
    return hi, lo


def _dot(a, b):
    return jnp.dot(a, b, preferred_element_type=F32)


def _dot_nt(a, b):
    return lax.dot_general(a, b, (((1,), (1,)), ((), ())), preferred_element_type=F32)


def _dot_tn(a, b):
    return lax.dot_general(a, b, (((0,), (0,)), ((), ())), preferred_element_type=F32)


def _pack_bf16_pairs(x):
    w = x.shape[1] // 2
    hi = lax.bitcast_convert_type(x[:, :w].astype(BF16).astype(F32), U32)
    lo = lax.bitcast_convert_type(x[:, w:].astype(BF16).astype(F32), U32)
    return hi | (lo >> 16)


def _unpack_bf16_pairs(p):
    hi = lax.bitcast_convert_type(p & jnp.uint32(0xFFFF0000), F32)
    lo = lax.bitcast_convert_type(p << 16, F32)
    return jnp.concatenate([hi, lo], axis=1)


def _memkv_kernel(mem_ref, g_ref, w_ref, k_ref, v_ref):
    d = mem_ref.shape[-1]
    mn = _rms(mem_ref[0], g_ref[...]).astype(BF16)
    kv = _dot(mn, w_ref[0])
    k_ref[0, 0] = kv[:, :d].astype(BF16)
    v_ref[0, 0] = kv[:, d:].astype(BF16)


def _mem_kv(mem, norm_mem, w_kv):
    b, nm, d = mem.shape
    depth = w_kv.shape[0]
    out = jax.ShapeDtypeStruct((depth, b, nm, d), BF16)
    return pl.pallas_call(
        _memkv_kernel,
        out_shape=(out, out),
        grid=(depth, b),
        in_specs=[
            pl.BlockSpec((1, nm, d), lambda l, i: (i, 0, 0)),
            pl.BlockSpec((1, d), lambda l, i: (0, 0)),
            pl.BlockSpec((1, d, 2 * d), lambda l, i: (l, 0, 0)),
        ],
        out_specs=(
            pl.BlockSpec((1, 1, nm, d), lambda l, i: (l, i, 0, 0)),
            pl.BlockSpec((1, 1, nm, d), lambda l, i: (l, i, 0, 0)),
        ),
        compiler_params=_cparams(("arbitrary", "arbitrary")),
        name="mem_kv",
    )(mem, norm_mem.reshape(1, d), w_kv.astype(BF16))


def _conv_kernel(x_ref, g_ref, win_ref, bin_ref, wdw_ref, bdw_ref, lng_ref, lnb_ref, wout_ref, bout_ref,
                 o_ref, ext_ref, conv_ref):
    ts, d = x_ref.shape[1], x_ref.shape[2]

    @pl.when(pl.program_id(1) == 0)
    def _():
        ext_ref[...] = jnp.zeros(ext_ref.shape, F32)

    x = x_ref[0]
    h = _rms(x, g_ref[...]).astype(BF16)
    u = _dot(h, win_ref[...]) + bin_ref[...]
    glu = u[:, :d] * _sigmoid(u[:, d:])
    for b in range(SUBLANES):
        ext_ref[b, CONV_CARRY - b:CONV_CARRY - b + ts, :] = glu

    first = CONV_CARRY - (CONV_KERNEL - 1)

    def chunk(i, carry):
        r0 = pl.multiple_of(i * CONV_ROWS, CONV_ROWS)
        for c0 in range(0, d, CONV_COLS):
            cols = slice(c0, c0 + CONV_COLS)
            acc = [jnp.zeros((SUBLANES, CONV_COLS), F32) for _ in range(CONV_ROWS // SUBLANES)]
            for k in range(CONV_KERNEL):
                b = (first + k) % SUBLANES
                wk = wdw_ref[k * SUBLANES:(k + 1) * SUBLANES, cols]
                for j in range(CONV_ROWS // SUBLANES):
                    rows = pl.ds(r0 + (first + k - b) + j * SUBLANES, SUBLANES)
                    acc[j] = acc[j] + ext_ref[b, rows, cols] * wk
            for j in range(CONV_ROWS // SUBLANES):
                conv_ref[pl.ds(r0 + j * SUBLANES, SUBLANES), cols] = acc[j]
        return carry

    lax.fori_loop(0, ts // CONV_ROWS, chunk, 0)
    for b in range(SUBLANES):
        ext_ref[b, 0:CONV_CARRY, :] = ext_ref[b, ts:ts + CONV_CARRY, :]

    c = conv_ref[...] + bdw_ref[...]
    mu = jnp.mean(c, axis=-1, keepdims=True)
    cc = c - mu
    var = jnp.mean(cc * cc, axis=-1, keepdims=True)
    un = cc * lax.rsqrt(var + EPS) * lng_ref[...] + lnb_ref[...]
    act = (un * _sigmoid(un)).astype(BF16)
    o_ref[0] = x + _dot(act, wout_ref[...]) + bout_ref[...]


def _conv_mixer(x, b0, nb, g, w_in, b_in, w_dw, b_dw, ln_g, ln_b, w_out, b_out):
    _, s, d = x.shape
    ts = min(TILE_CONV, s)
    row = lambda v: v.reshape(1, -1)
    const = lambda shape: pl.BlockSpec(shape, lambda i, j: (0,) * len(shape))
    return pl.pallas_call(
        _conv_kernel,
        out_shape=jax.ShapeDtypeStruct((nb, s, d), F32),
        grid=(nb, s // ts),
        in_specs=[
            pl.BlockSpec((1, ts, d), lambda i, j: (i + b0, j, 0)),
            const((1, d)), const((d, 2 * d)), const((1, 2 * d)), const((CONV_KERNEL * SUBLANES, d)), const((1, d)),
            const((1, d)), const((1, d)), const((d, d)), const((1, d)),
        ],
        out_specs=pl.BlockSpec((1, ts, d), lambda i, j: (i, j, 0)),
        scratch_shapes=[pltpu.VMEM((SUBLANES, CONV_CARRY + ts, d), F32), pltpu.VMEM((ts, d), F32)],
        compiler_params=_cparams(("arbitrary", "arbitrary")),
        name="conv_mixer",
    )(x, row(g), w_in.astype(BF16), row(b_in), jnp.repeat(w_dw, SUBLANES, axis=0), row(b_dw), row(ln_g), row(ln_b),
      w_out.astype(BF16), row(b_out))


def _gla_levels(ts):
    sizes = [GLA_LEAF]
    while sizes[-1] < ts:
        sizes.append(sizes[-1] * 2)
    return sizes


def _moe_residual(x, gates, y0, y1):
    return x + _unpack_bf16_pairs(y0) * gates[:, 0:1] + _unpack_bf16_pairs(y1) * gates[:, 1:2]


def _gla_kernel(x_ref, gate_ref, y0_ref, y1_ref, g_ref, wq_ref, wk_ref, wv_ref, wa_ref, wr_ref, wa2_ref, ba_ref,
                ng_ref, wo_ref, o_ref, state_ref):
    ts, d = x_ref.shape[1], x_ref.shape[2]
    dk = wq_ref.shape[1]
    dkh = dk // GLA_HEADS
    dvh = d // GLA_HEADS

    @pl.when(pl.program_id(1) == 0)
    def _():
        state_ref[...] = jnp.zeros(state_ref.shape, F32)

    x = _moe_residual(x_ref[0], gate_ref[...], y0_ref[...], y1_ref[...])
    h = _rms(x, g_ref[...]).astype(BF16)
    q = _dot(h, wq_ref[...]) * (dkh ** -0.5)
    k = _dot(h, wk_ref[...])
    v = _dot(h, wv_ref[...]).astype(BF16)
    r = _dot(h, wr_ref[...])
    a_hi, a_lo = _split_bf16(_dot(h, wa_ref[...]))
    w2_hi, w2_lo = wa2_ref[0], wa2_ref[1]
    z = _dot(a_hi, w2_hi) + _dot(a_lo, w2_hi) + _dot(a_hi, w2_lo) + ba_ref[...]
    log_a = -(jnp.maximum(-z, 0.0) + jnp.log(1.0 + jnp.exp(-jnp.abs(z)))) * (1.0 / GLA_TAU)

    row = lax.broadcasted_iota(I32, (ts, ts), 0)
    col = lax.broadcasted_iota(I32, (ts, ts), 1)
    tri = jnp.where(col <= row, 1.0, 0.0).astype(BF16)
    la_hi, la_lo = _split_bf16(log_a)
    bcum = _dot(tri, la_hi) + _dot(tri, la_lo)
    b_last = bcum[ts - 1:ts, :]

    q_in = (q * jnp.exp(bcum)).astype(BF16)
    k_out = (k * jnp.exp(b_last - bcum)).astype(BF16)

    sizes = _gla_levels(ts)
    scores = [None] * GLA_HEADS
    for lvl, size in enumerate(sizes):
        half = size // 2
        same_block = (row & -size) == (col & -size)
        if lvl == 0:
            pair = same_block & (col <= row)
            q_ok = k_ok = None
        else:
            pair = same_block & ((row & (size - 1)) >= half) & ((col & (size - 1)) < half)
            pos = lax.broadcasted_iota(I32, (ts, dk), 0) & (size - 1)
            q_ok = pos >= half
            k_ok = pos < half
        ref = jnp.concatenate(
            [jnp.broadcast_to(bcum[r0 + half:r0 + half + 1, :], (size, dk)) for r0 in range(0, ts, size)], axis=0)
        ql = q * jnp.exp(bcum - ref)
        kl = k * jnp.exp(ref - bcum)
        if q_ok is not None:
            ql = jnp.where(q_ok, ql, 0.0)
            kl = jnp.where(k_ok, kl, 0.0)
        ql = ql.astype(BF16)
        kl = kl.astype(BF16)
        for hd in range(GLA_HEADS):
            c0 = hd * dkh
            a = _dot_nt(ql[:, c0:c0 + dkh], kl[:, c0:c0 + dkh])
            a = jnp.where(pair, a, 0.0)
            scores[hd] = a if scores[hd] is None else scores[hd] + a

    outs = []
    for hd in range(GLA_HEADS):
        c0 = hd * dkh
        v_h = v[:, hd * dvh:(hd + 1) * dvh]
        st = state_ref[hd]
        o_h = _dot(scores[hd].astype(BF16), v_h) + _dot_nt(q_in[:, c0:c0 + dkh], st.astype(BF16))
        decay = jnp.exp(b_last[:, c0:c0 + dkh])
        state_ref[hd] = st * decay + _dot_tn(v_h, k_out[:, c0:c0 + dkh])
        o_h = o_h * lax.rsqrt(jnp.mean(o_h * o_h, axis=-1, keepdims=True) + EPS) * ng_ref[...]
        outs.append(o_h)
    o = jnp.concatenate(outs, axis=1) * (r * _sigmoid(r))
    o_ref[0] = x + _dot(o.astype(BF16), wo_ref[...])


def _gla_mixer(x, gates, y0, y1, g, w_in, w_a2, b_a, norm_g, w_o):
    b, s, d = x.shape
    dk = w_a2.shape[1]
    ts = min(TILE_GLA, s)
    nj = s // ts
    row = lambda v: v.reshape(1, -1)
    const = lambda shape: pl.BlockSpec(shape, lambda i, j: (0,) * len(shape))
    wq = w_in[:, :dk].astype(BF16)
    wk = w_in[:, dk:2 * dk].astype(BF16)
    wv = w_in[:, 2 * dk:2 * dk + d].astype(BF16)
    wa = jnp.pad(w_in[:, 2 * dk + d:2 * dk + d + GLA_RANK], ((0, 0), (0, GLA_RANK_PAD - GLA_RANK))).astype(BF16)
    wr = w_in[:, 2 * dk + d + GLA_RANK:].astype(BF16)
    wa2 = jnp.pad(w_a2, ((0, GLA_RANK_PAD - GLA_RANK), (0, 0)))
    wa2_hi = wa2.astype(BF16)
    wa2_lo = (wa2 - wa2_hi.astype(F32)).astype(BF16)
    wa2_split = jnp.stack([wa2_hi, wa2_lo])
    dvh = d // GLA_HEADS
    return pl.pallas_call(
        _gla_kernel,
        out_shape=jax.ShapeDtypeStruct(x.shape, F32),
        grid=(b, s // ts),
        in_specs=[
            pl.BlockSpec((1, ts, d), lambda i, j: (i, j, 0)),
            pl.BlockSpec((ts, TOP_K), lambda i, j: (i * nj + j, 0)),
            pl.BlockSpec((ts, d // 2), lambda i, j: (i * nj + j, 0)),
            pl.BlockSpec((ts, d // 2), lambda i, j: (i * nj + j, 0)),
            const((1, d)), const((d, dk)), const((d, dk)), const((d, d)), const((d, GLA_RANK_PAD)),
            const((d, d)), const((2, GLA_RANK_PAD, dk)), const((1, dk)), const((1, dvh)), const((d, d)),
        ],
        out_specs=pl.BlockSpec((1, ts, d), lambda i, j: (i, j, 0)),
        scratch_shapes=[pltpu.VMEM((GLA_HEADS, dvh, dk // GLA_HEADS), F32)],
        compiler_params=_cparams(("arbitrary", "arbitrary")),
        name="gla_mixer",
    )(x, gates, y0, y1, row(g), wq, wk, wv, wa, wr, wa2_split, row(b_a), row(norm_g), w_o.astype(BF16))


def _xattn_router_kernel(x_ref, gx_ref, wq_ref, k_ref, v_ref, wo_ref, gf_ref, wr_ref, br_ref, upper_ref,
                         x_out_ref, h_out_ref, route_ref, cnt_ref, carry_ref):
    ts, d = x_ref.shape[1], x_ref.shape[2]
    hd = d // XATTN_HEADS
    first = (pl.program_id(0) == 0) & (pl.program_id(1) == 0)

    @pl.when(first)
    def _():
        carry_ref[...] = jnp.zeros(carry_ref.shape, F32)

    x = x_ref[0]
    h = _rms(x, gx_ref[...]).astype(BF16)
    q = (_dot(h, wq_ref[...]) * (hd ** -0.5)).astype(BF16)
    k = k_ref[0, 0]
    v = v_ref[0, 0]
    outs = []
    for a in range(XATTN_HEADS):
        sl = slice(a * hd, (a + 1) * hd)
        s = _dot_nt(q[:, sl], k[:, sl])
        p = jnp.exp(s - jnp.max(s, axis=-1, keepdims=True))
        o = _dot(p.astype(BF16), v[:, sl]) / jnp.sum(p, axis=-1, keepdims=True)
        outs.append(o)
    att = jnp.concatenate(outs, axis=1).astype(BF16)
    x2 = x + _dot(att, wo_ref[...])
    x_out_ref[0] = x2

    hf = _rms(x2, gf_ref[...])
    h_out_ref[...] = _pack_bf16_pairs(hf)

    h_hi, h_lo = _split_bf16(hf)
    both = _dot_nt(wr_ref[...], h_hi)
    logits = (both[0:ROUTER_ROWS, :] + both[ROUTER_ROWS:2 * ROUTER_ROWS, :]
              + _dot_nt(wr_ref[0:ROUTER_ROWS, :], h_lo) + br_ref[...])
    gl = logits[N_EXPERTS:N_EXPERTS + N_GROUPS, :]
    gi = lax.broadcasted_iota(I32, gl.shape, 0).astype(F32)
    gmax = jnp.max(gl, axis=0, keepdims=True)
    g_sel = jnp.min(jnp.where(gl == gmax, gi, float(N_GROUPS)), axis=0, keepdims=True)
    pg_sel = 1.0 / jnp.sum(jnp.exp(gl - gmax), axis=0, keepdims=True)

    el = jnp.zeros((EXPERTS_PER_GROUP, ts), F32)
    for gidx in range(N_GROUPS):
        lo = gidx * EXPERTS_PER_GROUP
        el = jnp.where(g_sel == float(gidx), logits[lo:lo + EXPERTS_PER_GROUP, :], el)
    ei = lax.broadcasted_iota(I32, el.shape, 0).astype(F32)
    m1 = jnp.max(el, axis=0, keepdims=True)
    i1 = jnp.min(jnp.where(el == m1, ei, float(EXPERTS_PER_GROUP)), axis=0, keepdims=True)
    rest = jnp.where(ei == i1, -jnp.inf, el)
    m2 = jnp.max(rest, axis=0, keepdims=True)
    i2 = jnp.min(jnp.where(rest == m2, ei, float(EXPERTS_PER_GROUP)), axis=0, keepdims=True)
    ratio = jnp.exp(m2 - m1)
    gate1 = pg_sel / (1.0 + ratio)
    gate2 = pg_sel * ratio / (1.0 + ratio)
    e1 = g_sel * float(EXPERTS_PER_GROUP) + i1
    e2 = g_sel * float(EXPERTS_PER_GROUP) + i2

    xi = lax.broadcasted_iota(I32, (N_EXPERTS, ts), 0).astype(F32)
    oh1 = jnp.where(xi == e1, 1.0, 0.0)
    oh2 = jnp.where(xi == e2, 1.0, 0.0)
    oh = oh1 + oh2
    n_blk = ts // LANES
    stacked = jnp.concatenate([oh[:, c * LANES:(c + 1) * LANES] for c in range(n_blk)], axis=0)
    within = _dot(stacked.astype(BF16), upper_ref[...])
    totals = jnp.sum(stacked, axis=1, keepdims=True)
    run = carry_ref[...]
    before = []
    for c in range(n_blk):
        before.append(within[c * N_EXPERTS:(c + 1) * N_EXPERTS, :] + run)
        run = run + totals[c * N_EXPERTS:(c + 1) * N_EXPERTS, :]
    before = jnp.concatenate(before, axis=1)
    rank1 = jnp.sum(oh1 * before, axis=0, keepdims=True)
    rank2 = jnp.sum(oh2 * before, axis=0, keepdims=True)
    carry_ref[...] = run
    cnt_ref[...] = jnp.broadcast_to(run, cnt_ref.shape)

    route_ref[0:1, :] = e1.astype(I32)
    route_ref[1:2, :] = e2.astype(I32)
    route_ref[2:3, :] = rank1.astype(I32)
    route_ref[3:4, :] = rank2.astype(I32)
    route_ref[4:5, :] = lax.bitcast_convert_type(gate1, I32)
    route_ref[5:6, :] = lax.bitcast_convert_type(gate2, I32)
    route_ref[6:8, :] = jnp.zeros((2, ts), I32)


def _xattn_router(x, b0, g_x, w_q, k_mem, v_mem, layer, w_o, g_f, w_grp, b_grp, w_exp, b_exp):
    b, s, d = x.shape
    nm = k_mem.shape[2]
    t = b * s
    ts = min(TILE_XATTN, s)
    nj = s // ts
    row = lambda v: v.reshape(1, -1)
    const = lambda shape: pl.BlockSpec(shape, lambda i, j: (0,) * len(shape))
    pad = ROUTER_ROWS - N_GROUPS - N_EXPERTS
    w_r = jnp.pad(jnp.concatenate([w_exp, w_grp], axis=1).T, ((0, pad), (0, 0)))
    w_r_hi = w_r.astype(BF16)
    w_r_split = jnp.concatenate([w_r_hi, (w_r - w_r_hi.astype(F32)).astype(BF16)], axis=0)
    b_r = jnp.pad(jnp.concatenate([b_exp, b_grp]), (0, pad)).reshape(ROUTER_ROWS, 1)
    ti = jnp.arange(LANES)
    upper = (ti[:, None] < ti[None, :]).astype(BF16)
    return pl.pallas_call(
        _xattn_router_kernel,
        out_shape=(
            jax.ShapeDtypeStruct(x.shape, F32),
            jax.ShapeDtypeStruct((t, d // 2), U32),
            jax.ShapeDtypeStruct((8, t), I32),
            jax.ShapeDtypeStruct((N_EXPERTS, 128), F32),
        ),
        grid=(b, nj),
        in_specs=[
            pl.BlockSpec((1, ts, d), lambda i, j: (i, j, 0)),
            const((1, d)), const((d, d)),
            pl.BlockSpec((1, 1, nm, d), lambda i, j: (layer, i + b0, 0, 0)),
            pl.BlockSpec((1, 1, nm, d), lambda i, j: (layer, i + b0, 0, 0)),
            const((d, d)), const((1, d)), const((2 * ROUTER_ROWS, d)), const((ROUTER_ROWS, 1)), const((LANES, LANES)),
        ],
        out_specs=(
            pl.BlockSpec((1, ts, d), lambda i, j: (i, j, 0)),
            pl.BlockSpec((ts, d // 2), lambda i, j: (i * nj + j, 0)),
            pl.BlockSpec((8, ts), lambda i, j: (0, i * nj + j)),
            pl.BlockSpec((N_EXPERTS, 128), lambda i, j: (0, 0)),
        ),
        scratch_shapes=[pltpu.VMEM((N_EXPERTS, 1), F32)],
        compiler_params=_cparams(("arbitrary", "arbitrary")),
        name="xattn_router",
    )(x, row(g_x), w_q.astype(BF16), k_mem, v_mem, w_o.astype(BF16), row(g_f), w_r_split, b_r, upper)


def _sc_mesh():
    return plsc.VectorSubcoreMesh(core_axis_name="c", subcore_axis_name="s",
                                  num_cores=SC_CORES, num_subcores=SC_SUBCORES)


def _sc_worker():
    return lax.axis_index("s") * SC_CORES + lax.axis_index("c")


def _dispatch(h_packed, dest, n_rows):
    t, w = h_packed.shape
    chunk = SC_DISPATCH_CHUNK
    per_worker = t // SC_WORKERS
    n_chunks = per_worker // chunk
    assert n_chunks % 2 == 0 and n_chunks * chunk * SC_WORKERS == t
    dest = dest.reshape(TOP_K, SC_WORKERS * n_chunks, chunk)
    rows_buf = pltpu.VMEM((chunk, w), U32)

    @functools.partial(
        pl.kernel, mesh=_sc_mesh(),
        out_type=jax.ShapeDtypeStruct((n_rows, w), U32),
        scratch_types=[pltpu.VMEM((n_chunks, chunk), I32), pltpu.VMEM((n_chunks, chunk), I32), rows_buf, rows_buf,
                       pltpu.SemaphoreType.DMA((2,)), pltpu.SemaphoreType.DMA((2, TOP_K))],
        name="moe_dispatch_sc",
    )
    def run(h_hbm, d0_hbm, d1_hbm, xbuf_hbm, idx0_v, idx1_v, buf_a, buf_b, read_sem, write_sem):
        wid = _sc_worker()
        pltpu.sync_copy(d0_hbm.at[pl.ds(wid * n_chunks, n_chunks)], idx0_v)
        pltpu.sync_copy(d1_hbm.at[pl.ds(wid * n_chunks, n_chunks)], idx1_v)

        @pl.loop(0, n_chunks, step=2)
        def _(i):
            reads = [pltpu.async_copy(h_hbm.at[pl.ds(wid * per_worker + (i + j) * chunk, chunk)], buf, read_sem.at[j])
                     for j, buf in enumerate((buf_a, buf_b))]
            writes = []
            for j, buf in enumerate((buf_a, buf_b)):
                reads[j].wait()
                writes.append(pltpu.async_copy(buf, xbuf_hbm.at[idx0_v.at[i + j]], write_sem.at[j, 0]))
                writes.append(pltpu.async_copy(buf, xbuf_hbm.at[idx1_v.at[i + j]], write_sem.at[j, 1]))
            for copy in writes:
                copy.wait()

    return run(h_packed, dest[0], dest[1])


def _gather_pairs(y_buf, dest):
    t = dest.shape[1]
    w = y_buf.shape[1]
    chunk = SC_GATHER_CHUNK
    per_worker = t // SC_WORKERS
    n_chunks = per_worker // chunk
    assert n_chunks % 2 == 0 and n_chunks * chunk * SC_WORKERS == t
    dest = dest.reshape(TOP_K, SC_WORKERS * n_chunks, chunk)
    out = jax.ShapeDtypeStruct((t, w), U32)
    rows_buf = pltpu.VMEM((chunk, w), U32)

    @functools.partial(
        pl.kernel, mesh=_sc_mesh(),
        out_type=(out, out),
        scratch_types=[pltpu.VMEM((n_chunks, chunk), I32), pltpu.VMEM((n_chunks, chunk), I32),
                       rows_buf, rows_buf, rows_buf, rows_buf, pltpu.SemaphoreType.DMA((2, TOP_K))],
        name="moe_gather_sc",
    )
    def run(y_hbm, d0_hbm, d1_hbm, y0_hbm, y1_hbm, idx0_v, idx1_v, buf_a0, buf_a1, buf_b0, buf_b1, sem):
        wid = _sc_worker()
        pltpu.sync_copy(d0_hbm.at[pl.ds(wid * n_chunks, n_chunks)], idx0_v)
        pltpu.sync_copy(d1_hbm.at[pl.ds(wid * n_chunks, n_chunks)], idx1_v)

        @pl.loop(0, n_chunks, step=2)
        def _(i):
            bufs = ((buf_a0, buf_a1), (buf_b0, buf_b1))
            gathers = [[pltpu.async_copy(y_hbm.at[idx_v.at[i + j]], bufs[j][k], sem.at[j, k])
                        for k, idx_v in enumerate((idx0_v, idx1_v))] for j in range(2)]
            stores = []
            for j in range(2):
                rows = pl.ds(wid * per_worker + (i + j) * chunk, chunk)
                for k, out_hbm in enumerate((y0_hbm, y1_hbm)):
                    gathers[j][k].wait()
                    stores.append(pltpu.async_copy(bufs[j][k], out_hbm.at[rows], sem.at[j, k]))
            for copy in stores:
                copy.wait()

    return run(y_buf, dest[0], dest[1])


def _expert_kernel(be_ref, bf_ref, bv_ref, slot_ref, next_ref, x_ref, wg_hbm, wu_hbm, wd_hbm, y_ref,
                   wg_f32, wu_f32, wd_f32, wg_bf, wu_bf, wd_bf, sems, *, layer):
    blk = pl.program_id(0)
    valid = bv_ref[blk]

    def weight_copies(expert, slot):
        return [pltpu.make_async_copy(src.at[layer, expert], dst.at[slot], sems.at[slot, i])
                for i, (src, dst) in enumerate(((wg_hbm, wg_f32), (wu_hbm, wu_f32), (wd_hbm, wd_f32)))]

    @pl.when(blk == 0)
    def _():
        for copy in weight_copies(be_ref[0], 0):
            copy.start()

    @pl.when(bf_ref[blk] == 1)
    def _():
        slot = slot_ref[blk]
        for copy in weight_copies(be_ref[blk], slot):
            copy.wait()

        @pl.when(next_ref[blk] >= 0)
        def _():
            for copy in weight_copies(next_ref[blk], 1 - slot):
                copy.start()

        wg_bf[...] = wg_f32[slot].astype(BF16)
        wu_bf[...] = wu_f32[slot].astype(BF16)
        wd_bf[...] = wd_f32[slot].astype(BF16)

    @pl.when(valid > 0)
    def _():
        live = lax.broadcasted_iota(I32, x_ref.shape, 0) < valid
        xb = _unpack_bf16_pairs(jnp.where(live, x_ref[...], jnp.uint32(0))).astype(BF16)
        gt = _dot(xb, wg_bf[...])
        up = _dot(xb, wu_bf[...])
        act = (gt * _sigmoid(gt) * up).astype(BF16)
        y_ref[...] = _pack_bf16_pairs(_dot(act, wd_bf[...]))

    @pl.when(valid <= 0)
    def _():
        y_ref[...] = jnp.zeros(y_ref.shape, U32)


def _experts(x_buf, block_expert, block_first, block_valid, w_gate, w_up, w_down, layer):
    n_rows, w = x_buf.shape
    d, de = w_gate.shape[2], w_gate.shape[3]
    bm = MOE_BLOCK_ROWS
    n = n_rows // bm
    block_slot = (jnp.cumsum(block_first) - 1) % 2
    idx = jnp.arange(n, dtype=I32)
    later_first = jnp.concatenate([jnp.where(block_first[1:] == 1, idx[1:], n), jnp.full((1,), n, I32)])
    next_first = lax.cummin(later_first, reverse=True)
    block_next = jnp.where(next_first < n, block_expert[jnp.minimum(next_first, n - 1)], -1).astype(I32)
    any_space = pl.BlockSpec(memory_space=pl.ANY)
    grid_spec = pltpu.PrefetchScalarGridSpec(
        num_scalar_prefetch=5,
        grid=(n,),
        in_specs=[pl.BlockSpec((bm, w), lambda i, *_: (i, 0)), any_space, any_space, any_space],
        out_specs=pl.BlockSpec((bm, w), lambda i, *_: (i, 0)),
        scratch_shapes=[pltpu.VMEM((2, d, de), F32), pltpu.VMEM((2, d, de), F32), pltpu.VMEM((2, de, d), F32),
                        pltpu.VMEM((d, de), BF16), pltpu.VMEM((d, de), BF16), pltpu.VMEM((de, d), BF16),
                        pltpu.SemaphoreType.DMA((2, 3))],
    )
    return pl.pallas_call(
        functools.partial(_expert_kernel, layer=layer),
        out_shape=jax.ShapeDtypeStruct((n_rows, w), U32),
        grid_spec=grid_spec,
        compiler_params=_cparams(("arbitrary",)),
        name="moe_experts",
    )(block_expert, block_first, block_valid, block_slot.astype(I32), block_next, x_buf, w_gate, w_up, w_down)


def _combine_kernel(x_ref, gate_ref, y0_ref, y1_ref, gfin_ref, *rest, final_norm):
    o_ref = rest[-1]
    out = _moe_residual(x_ref[...], gate_ref[...], y0_ref[...], y1_ref[...])
    if final_norm:
        out = _rms(out, gfin_ref[...])
    o_ref[...] = out


def _combine(x2, y0, y1, gates, g_final, final_norm, total_rows=None, first_row=0, out_so_far=None):
    rows, d = x2.shape
    t = rows if total_rows is None else total_rows
    w = y0.shape[1]
    ts = min(TILE_COMBINE, rows)
    off = first_row // ts
    in_specs = [
        pl.BlockSpec((ts, d), lambda i: (i, 0)),
        pl.BlockSpec((ts, TOP_K), lambda i: (i, 0)),
        pl.BlockSpec((ts, w), lambda i: (i, 0)),
        pl.BlockSpec((ts, w), lambda i: (i, 0)),
        pl.BlockSpec((1, d), lambda i: (0, 0)),
    ]
    args = [x2, gates, y0, y1, g_final.reshape(1, d)]
    aliases = {}
    if out_so_far is not None:
        in_specs.append(pl.BlockSpec(memory_space=pl.ANY))
        args.append(out_so_far)
        aliases = {len(args) - 1: 0}
    return pl.pallas_call(
        functools.partial(_combine_kernel, final_norm=final_norm),
        out_shape=jax.ShapeDtypeStruct((t, d), F32),
        grid=(rows // ts,),
        in_specs=in_specs,
        out_specs=pl.BlockSpec((ts, d), lambda i: (i + off, 0)),
        input_output_aliases=aliases,
        compiler_params=_cparams(("arbitrary",)),
        name="moe_combine",
    )(*args)


def _moe_layout(route, counts):
    bm = MOE_BLOCK_ROWS
    t = route.shape[1]
    n_blocks = (t * TOP_K) // bm + N_EXPERTS
    cnt = counts[:, 0].astype(I32)
    padded = (cnt + bm - 1) // bm * bm
    pad_ends = jnp.cumsum(padded)
    pad_off = pad_ends - padded
    experts = jnp.arange(N_EXPERTS, dtype=I32)
    hit = route[0:TOP_K, :, None] == experts
    dest = jnp.sum(jnp.where(hit, pad_off, 0), axis=-1) + route[TOP_K:2 * TOP_K]
    gates = lax.bitcast_convert_type(route[2 * TOP_K:3 * TOP_K], F32).T
    starts = jnp.arange(n_blocks, dtype=I32) * bm
    block_expert = jnp.minimum(jnp.sum((pad_ends[None, :] <= starts[:, None]).astype(I32), axis=1),
                               N_EXPERTS - 1)
    block_first = jnp.concatenate([jnp.ones((1,), I32), (block_expert[1:] != block_expert[:-1]).astype(I32)])
    own = block_expert[:, None] == experts
    block_valid = jnp.clip(jnp.sum(jnp.where(own, cnt + pad_off, 0), axis=1) - starts, 0, bm)
    block_valid = jnp.where(starts < pad_ends[-1], block_valid, 0).astype(I32)
    return dest, gates, block_expert, block_first, block_valid, n_blocks * bm


def kernel(x, mem, norm_mix, norm_xattn, norm_ffn, norm_mem, norm_final, conv_w_in, conv_b_in, conv_w_dw,
           conv_b_dw, conv_ln_g, conv_ln_b, conv_w_out, conv_b_out, gla_w_in, gla_w_a2, gla_b_a, gla_norm_g,
           gla_w_o, xa_w_q, xa_w_kv, xa_w_o, moe_w_grp, moe_b_grp, moe_w_exp, moe_b_exp, moe_w_gate, moe_w_up,
           moe_w_down):
    b, s, d = x.shape
    depth = norm_mix.shape[0]
    k_mem, v_mem = _mem_kv(mem, norm_mem, xa_w_kv)
    nb = b // BATCH_PARTS
    xs = [x] * BATCH_PARTS
    offs = [p * nb for p in range(BATCH_PARTS)]
    moes = [None] * BATCH_PARTS
    for i in range(depth):
        j = i // 2
        for p in range(BATCH_PARTS):
            if i % 2 == 0:
                if moes[p] is not None:
                    gates, y0, y1 = moes[p]
                    xs[p] = _combine(xs[p].reshape(nb * s, d), y0, y1, gates, norm_final, False).reshape(nb, s, d)
                xs[p] = _conv_mixer(xs[p], offs[p], nb, norm_mix[i], conv_w_in[j], conv_b_in[j], conv_w_dw[j],
                                    conv_b_dw[j], conv_ln_g[j], conv_ln_b[j], conv_w_out[j], conv_b_out[j])
            else:
                xs[p] = _gla_mixer(xs[p], *moes[p], norm_mix[i], gla_w_in[j], gla_w_a2[j], gla_b_a[j],
                                   gla_norm_g[j], gla_w_o[j])
            offs[p] = 0
        for p in range(BATCH_PARTS):
            x2, h_packed, route, counts = _xattn_router(
                xs[p], p * nb, norm_xattn[i], xa_w_q[i], k_mem, v_mem, i, xa_w_o[i], norm_ffn[i],
                moe_w_grp[i], moe_b_grp[i], moe_w_exp[i], moe_b_exp[i])
            dest, gates, block_expert, block_first, block_valid, n_rows = _moe_layout(route, counts)
            x_buf = _dispatch(h_packed, dest, n_rows)
            y_buf = _experts(x_buf, block_expert, block_first, block_valid, moe_w_gate, moe_w_up, moe_w_down, i)
            y0, y1 = _gather_pairs(y_buf, dest)
            xs[p], moes[p] = x2, (gates, y0, y1)
    out = None
    for p in range(BATCH_PARTS):
        gates, y0, y1 = moes[p]
        out = _combine(xs[p].reshape(nb * s, d), y0, y1, gates, norm_final, True, b * s, p * nb * s, out)
    return out.reshape(b, s, d)
```

```python
import functools

import jax
import jax.numpy as jnp
from jax import lax
from jax.experimental import pallas as pl
from jax.experimental.pallas import tpu as pltpu
from jax.experimental.pallas import tpu_sc as plsc

F32 = jnp.float32
BF16 = jnp.bfloat16
I32 = jnp.int32
U32 = jnp.uint32

EPS = 1e-6
CONV_KERNEL = 31
CONV_CARRY = 32
CONV_ROWS = 64
CONV_COLS = 256
SUBLANES = 8
LANES = 128
GLA_HEADS = 4
GLA_RANK = 16
GLA_RANK_PAD = 128
GLA_TAU = 16.0
GLA_LEAF = 32
XATTN_HEADS = 4
N_GROUPS = 4
EXPERTS_PER_GROUP = 8
N_EXPERTS = N_GROUPS * EXPERTS_PER_GROUP
ROUTER_ROWS = 40
TOP_K = 2

TILE_CONV = 512
TILE_GLA = 256
TILE_XATTN = 1024
TILE_COMBINE = 512
FINAL_PARTS = 2
SC_CORES = 2
SC_SUBCORES = 16
SC_WORKERS = SC_CORES * SC_SUBCORES
SC_DISPATCH_CHUNK = 64
SC_GATHER_CHUNK = 32
MOE_BLOCK_ROWS = 512
VMEM_LIMIT = 56 * 1024 * 1024


def _cparams(sem):
    return pltpu.CompilerParams(dimension_semantics=sem, vmem_limit_bytes=VMEM_LIMIT)


def _rms(x, g):
    return x * lax.rsqrt(jnp.mean(x * x, axis=-1, keepdims=True) + EPS) * g


def _sigmoid(x):
    return 0.5 * jnp.tanh(0.5 * x) + 0.5


def _split_bf16(x):
    hi = x.astype(BF16)
    lo = (x - hi.astype(F32)).astype(BF16)
    return hi, lo


def _dot(a, b):
    return jnp.dot(a, b, preferred_element_type=F32)


def _dot_nt(a, b):
    return lax.dot_general(a, b, (((1,), (1,)), ((), ())), preferred_element_type=F32)


def _dot_tn(a, b):
    return lax.dot_general(a, b, (((0,), (0,)), ((), ())), preferred_element_type=F32)


def _pack_bf16_pairs(x):
    w = x.shape[1] // 2
    hi = lax.bitcast_convert_type(x[:, :w].astype(BF16).astype(F32), U32)
    lo = lax.bitcast_convert_type(x[:, w:].astype(BF16).astype(F32), U32)
    return hi | (lo >> 16)


def _unpack_bf16_pairs(p):
    hi = lax.bitcast_convert_type(p & jnp.uint32(0xFFFF0000), F32)
    lo = lax.bitcast_convert_type(p << 16, F32)
    return jnp.concatenate([hi, lo], axis=1)


def _memkv_kernel(mem_ref, g_ref, w_ref, k_ref, v_ref):
    d = mem_ref.shape[-1]
    mn = _rms(mem_ref[0], g_ref[...]).astype(BF16)
    kv = _dot(mn, w_ref[0])
    k_ref[0, 0] = kv[:, :d].astype(BF16)
    v_ref[0, 0] = kv[:, d:].astype(BF16)


def _mem_kv(mem, norm_mem, w_kv):
    b, nm, d = mem.shape
    depth = w_kv.shape[0]
    out = jax.ShapeDtypeStruct((depth, b, nm, d), BF16)
    return pl.pallas_call(
        _memkv_kernel,
        out_shape=(out, out),
        grid=(depth, b),
        in_specs=[
            pl.BlockSpec((1, nm, d), lambda l, i: (i, 0, 0)),
            pl.BlockSpec((1, d), lambda l, i: (0, 0)),
            pl.BlockSpec((1, d, 2 * d), lambda l, i: (l, 0, 0)),
        ],
        out_specs=(
            pl.BlockSpec((1, 1, nm, d), lambda l, i: (l, i, 0, 0)),
            pl.BlockSpec((1, 1, nm, d), lambda l, i: (l, i, 0, 0)),
        ),
        compiler_params=_cparams(("arbitrary", "arbitrary")),
        name="mem_kv",
    )(mem, norm_mem.reshape(1, d), w_kv.astype(BF16))


def _conv_kernel(x_ref, g_ref, win_ref, bin_ref, wdw_ref, bdw_ref, lng_ref, lnb_ref, wout_ref, bout_ref,
                 o_ref, ext_ref, conv_ref):
    ts, d = x_ref.shape[1], x_ref.shape[2]

    @pl.when(pl.program_id(1) == 0)
    def _():
        ext_ref[...] = jnp.zeros(ext_ref.shape, F32)

    x = x_ref[0]
    h = _rms(x, g_ref[...]).astype(BF16)
    u = _dot(h, win_ref[...]) + bin_ref[...]
    glu = u[:, :d] * _sigmoid(u[:, d:])
    for b in range(SUBLANES):
        ext_ref[b, CONV_CARRY - b:CONV_CARRY - b + ts, :] = glu

    first = CONV_CARRY - (CONV_KERNEL - 1)

    def chunk(i, carry):
        r0 = pl.multiple_of(i * CONV_ROWS, CONV_ROWS)
        for c0 in range(0, d, CONV_COLS):
            cols = slice(c0, c0 + CONV_COLS)
            acc = [jnp.zeros((SUBLANES, CONV_COLS), F32) for _ in range(CONV_ROWS // SUBLANES)]
            for k in range(CONV_KERNEL):
                b = (first + k) % SUBLANES
                wk = wdw_ref[k * SUBLANES:(k + 1) * SUBLANES, cols]
                for j in range(CONV_ROWS // SUBLANES):
                    rows = pl.ds(r0 + (first + k - b) + j * SUBLANES, SUBLANES)
                    acc[j] = acc[j] + ext_ref[b, rows, cols] * wk
            for j in range(CONV_ROWS // SUBLANES):
                conv_ref[pl.ds(r0 + j * SUBLANES, SUBLANES), cols] = acc[j]
        return carry

    lax.fori_loop(0, ts // CONV_ROWS, chunk, 0)
    for b in range(SUBLANES):
        ext_ref[b, 0:CONV_CARRY, :] = ext_ref[b, ts:ts + CONV_CARRY, :]

    c = conv_ref[...] + bdw_ref[...]
    mu = jnp.mean(c, axis=-1, keepdims=True)
    cc = c - mu
    var = jnp.mean(cc * cc, axis=-1, keepdims=True)
    un = cc * lax.rsqrt(var + EPS) * lng_ref[...] + lnb_ref[...]
    act = (un * _sigmoid(un)).astype(BF16)
    o_ref[0] = x + _dot(act, wout_ref[...]) + bout_ref[...]


def _conv_mixer(x, g, w_in, b_in, w_dw, b_dw, ln_g, ln_b, w_out, b_out):
    b, s, d = x.shape
    ts = min(TILE_CONV, s)
    row = lambda v: v.reshape(1, -1)
    const = lambda shape: pl.BlockSpec(shape, lambda i, j: (0,) * len(shape))
    return pl.pallas_call(
        _conv_kernel,
        out_shape=jax.ShapeDtypeStruct(x.shape, F32),
        grid=(b, s // ts),
        in_specs=[
            pl.BlockSpec((1, ts, d), lambda i, j: (i, j, 0)),
            const((1, d)), const((d, 2 * d)), const((1, 2 * d)), const((CONV_KERNEL * SUBLANES, d)), const((1, d)),
            const((1, d)), const((1, d)), const((d, d)), const((1, d)),
        ],
        out_specs=pl.BlockSpec((1, ts, d), lambda i, j: (i, j, 0)),
        scratch_shapes=[pltpu.VMEM((SUBLANES, CONV_CARRY + ts, d), F32), pltpu.VMEM((ts, d), F32)],
        compiler_params=_cparams(("arbitrary", "arbitrary")),
        name="conv_mixer",
    )(x, row(g), w_in.astype(BF16), row(b_in), jnp.repeat(w_dw, SUBLANES, axis=0), row(b_dw), row(ln_g), row(ln_b),
      w_out.astype(BF16), row(b_out))


def _gla_levels(ts):
    sizes = [GLA_LEAF]
    while sizes[-1] < ts:
        sizes.append(sizes[-1] * 2)
    return sizes


def _moe_residual(x, route, y0, y1):
    g = lax.bitcast_convert_type(route, F32)
    g = jnp.concatenate([g, jnp.zeros((LANES - g.shape[0], g.shape[1]), F32)], axis=0).T
    gate0 = g[:, 2 * TOP_K:2 * TOP_K + 1]
    gate1 = g[:, 2 * TOP_K + 1:2 * TOP_K + 2]
    return x + _unpack_bf16_pairs(y0) * gate0 + _unpack_bf16_pairs(y1) * gate1


def _gla_kernel(x_ref, gate_ref, y0_ref, y1_ref, g_ref, wq_ref, wk_ref, wv_ref, wa_ref, wr_ref, wa2_ref, ba_ref,
                ng_ref, wo_ref, o_ref, state_ref, x_s, q_s, k_s, v_s, og_s, la_s, *, tiles_per_seq):
    j = pl.program_id(0)
    dkh = wq_ref.shape[1] // GLA_HEADS

    @pl.when(j == 0)
    def _():
        for ref in (x_s, q_s, k_s, og_s, la_s):
            ref[...] = jnp.zeros(ref.shape, F32)
        v_s[...] = jnp.zeros(v_s.shape, BF16)

    @pl.when((j == 0) | ((j - 1) % tiles_per_seq == 0))
    def _():
        state_ref[...] = jnp.zeros(state_ref.shape, F32)

    def stage_a(slot):
        x = _moe_residual(x_ref[0], gate_ref[...], y0_ref[...], y1_ref[...])
        h = _rms(x, g_ref[...]).astype(BF16)
        x_s[slot] = x
        q_s[slot] = _dot(h, wq_ref[...]) * (dkh ** -0.5)
        k_s[slot] = _dot(h, wk_ref[...])
        v_s[slot] = _dot(h, wv_ref[...]).astype(BF16)
        r = _dot(h, wr_ref[...])
        og_s[slot] = r * _sigmoid(r)
        a_hi, a_lo = _split_bf16(_dot(h, wa_ref[...]))
        w2_hi, w2_lo = wa2_ref[0], wa2_ref[1]
        z = _dot(a_hi, w2_hi) + _dot(a_lo, w2_hi) + _dot(a_hi, w2_lo) + ba_ref[...]
        la_s[slot] = -(jnp.maximum(-z, 0.0) + jnp.log(1.0 + jnp.exp(-jnp.abs(z)))) * (1.0 / GLA_TAU)

    def stage_b(slot):
        _gla_recurrence(x_s[slot], q_s[slot], k_s[slot], v_s[slot], og_s[slot], la_s[slot],
                        ng_ref, wo_ref, o_ref, state_ref)

    @pl.when(j % 2 == 0)
    def _():
        stage_b(1)
        stage_a(0)

    @pl.when(j % 2 == 1)
    def _():
        stage_b(0)
        stage_a(1)


def _gla_recurrence(x, q, k, v, out_gate, log_a, ng_ref, wo_ref, o_ref, state_ref):
    ts, d = x.shape
    dk = q.shape[1]
    dkh = dk // GLA_HEADS
    dvh = d // GLA_HEADS

    row = lax.broadcasted_iota(I32, (ts, ts), 0)
    col = lax.broadcasted_iota(I32, (ts, ts), 1)
    tri = jnp.where(col <= row, 1.0, 0.0).astype(BF16)
    la_hi, la_lo = _split_bf16(log_a)
    bcum = _dot(tri, la_hi) + _dot(tri, la_lo)
    b_last = bcum[ts - 1:ts, :]

    q_in = (q * jnp.exp(bcum)).astype(BF16)
    k_out = (k * jnp.exp(b_last - bcum)).astype(BF16)

    sizes = _gla_levels(ts)
    scores = [None] * GLA_HEADS
    for lvl, size in enumerate(sizes):
        half = size // 2
        same_block = (row & -size) == (col & -size)
        if lvl == 0:
            pair = same_block & (col <= row)
            q_ok = k_ok = None
        else:
            pair = same_block & ((row & (size - 1)) >= half) & ((col & (size - 1)) < half)
            pos = lax.broadcasted_iota(I32, (ts, dk), 0) & (size - 1)
            q_ok = pos >= half
            k_ok = pos < half
        ref = jnp.concatenate(
            [jnp.broadcast_to(bcum[r0 + half:r0 + half + 1, :], (size, dk)) for r0 in range(0, ts, size)], axis=0)
        ql = q * jnp.exp(bcum - ref)
        kl = k * jnp.exp(ref - bcum)
        if q_ok is not None:
            ql = jnp.where(q_ok, ql, 0.0)
            kl = jnp.where(k_ok, kl, 0.0)
        ql = ql.astype(BF16)
        kl = kl.astype(BF16)
        for hd in range(GLA_HEADS):
            c0 = hd * dkh
            a = _dot_nt(ql[:, c0:c0 + dkh], kl[:, c0:c0 + dkh])
            a = jnp.where(pair, a, 0.0)
            scores[hd] = a if scores[hd] is None else scores[hd] + a

    outs = []
    for hd in range(GLA_HEADS):
        c0 = hd * dkh
        v_h = v[:, hd * dvh:(hd + 1) * dvh]
        st = state_ref[hd]
        o_h = _dot(scores[hd].astype(BF16), v_h) + _dot_nt(q_in[:, c0:c0 + dkh], st.astype(BF16))
        decay = jnp.exp(b_last[:, c0:c0 + dkh])
        state_ref[hd] = st * decay + _dot_tn(v_h, k_out[:, c0:c0 + dkh])
        o_h = o_h * lax.rsqrt(jnp.mean(o_h * o_h, axis=-1, keepdims=True) + EPS) * ng_ref[...]
        outs.append(o_h)
    o = jnp.concatenate(outs, axis=1) * out_gate
    o_ref[0] = x + _dot(o.astype(BF16), wo_ref[...])


def _gla_mixer(x, gates, y0, y1, g, w_in, w_a2, b_a, norm_g, w_o):
    b, s, d = x.shape
    dk = w_a2.shape[1]
    ts = min(TILE_GLA, s)
    nj = s // ts
    row = lambda v: v.reshape(1, -1)
    const = lambda shape: pl.BlockSpec(shape, lambda j: (0,) * len(shape))
    wq = w_in[:, :dk].astype(BF16)
    wk = w_in[:, dk:2 * dk].astype(BF16)
    wv = w_in[:, 2 * dk:2 * dk + d].astype(BF16)
    wa = jnp.pad(w_in[:, 2 * dk + d:2 * dk + d + GLA_RANK], ((0, 0), (0, GLA_RANK_PAD - GLA_RANK))).astype(BF16)
    wr = w_in[:, 2 * dk + d + GLA_RANK:].astype(BF16)
    wa2 = jnp.pad(w_a2, ((0, GLA_RANK_PAD - GLA_RANK), (0, 0)))
    wa2_hi = wa2.astype(BF16)
    wa2_lo = (wa2 - wa2_hi.astype(F32)).astype(BF16)
    wa2_split = jnp.stack([wa2_hi, wa2_lo])
    dvh = d // GLA_HEADS
    n = b * nj

    def tile(j, lag):
        return jnp.clip(j - lag, 0, n - 1)

    tokens = lambda lag: (lambda j: (tile(j, lag), 0))
    return pl.pallas_call(
        functools.partial(_gla_kernel, tiles_per_seq=nj),
        out_shape=jax.ShapeDtypeStruct(x.shape, F32),
        grid=(n + 1,),
        in_specs=[
            pl.BlockSpec((1, ts, d), lambda j: (tile(j, 0) // nj, tile(j, 0) % nj, 0)),
            pl.BlockSpec((8, ts), lambda j: (0, tile(j, 0))),
            pl.BlockSpec((ts, d // 2), tokens(0)),
            pl.BlockSpec((ts, d // 2), tokens(0)),
            const((1, d)), const((d, dk)), const((d, dk)), const((d, d)), const((d, GLA_RANK_PAD)),
            const((d, d)), const((2, GLA_RANK_PAD, dk)), const((1, dk)), const((1, dvh)), const((d, d)),
        ],
        out_specs=pl.BlockSpec((1, ts, d), lambda j: (tile(j, 1) // nj, tile(j, 1) % nj, 0)),
        scratch_shapes=[pltpu.VMEM((GLA_HEADS, dvh, dk // GLA_HEADS), F32),
                        pltpu.VMEM((2, ts, d), F32), pltpu.VMEM((2, ts, dk), F32), pltpu.VMEM((2, ts, dk), F32),
                        pltpu.VMEM((2, ts, d), BF16), pltpu.VMEM((2, ts, d), F32), pltpu.VMEM((2, ts, dk), F32)],
        compiler_params=_cparams(("arbitrary",)),
        name="gla_mixer",
    )(x, gates, y0, y1, row(g), wq, wk, wv, wa, wr, wa2_split, row(b_a), row(norm_g), w_o.astype(BF16))


def _xattn_router_kernel(x_ref, gx_ref, wq_ref, k_ref, v_ref, wo_ref, gf_ref, wr_ref, br_ref, upper_ref,
                         x_out_ref, h_out_ref, route_ref, cnt_ref, carry_ref):
    ts, d = x_ref.shape[1], x_ref.shape[2]
    hd = d // XATTN_HEADS
    first = (pl.program_id(0) == 0) & (pl.program_id(1) == 0)

    @pl.when(first)
    def _():
        carry_ref[...] = jnp.zeros(carry_ref.shape, F32)

    x = x_ref[0]
    h = _rms(x, gx_ref[...]).astype(BF16)
    q = (_dot(h, wq_ref[...]) * (hd ** -0.5)).astype(BF16)
    k = k_ref[0, 0]
    v = v_ref[0, 0]
    outs = []
    for a in range(XATTN_HEADS):
        sl = slice(a * hd, (a + 1) * hd)
        s = _dot_nt(q[:, sl], k[:, sl])
        p = jnp.exp(s - jnp.max(s, axis=-1, keepdims=True))
        o = _dot(p.astype(BF16), v[:, sl]) / jnp.sum(p, axis=-1, keepdims=True)
        outs.append(o)
    att = jnp.concatenate(outs, axis=1).astype(BF16)
    x2 = x + _dot(att, wo_ref[...])
    x_out_ref[0] = x2

    hf = _rms(x2, gf_ref[...])
    h_out_ref[...] = _pack_bf16_pairs(hf)

    h_hi, h_lo = _split_bf16(hf)
    both = _dot_nt(wr_ref[...], h_hi)
    logits = (both[0:ROUTER_ROWS, :] + both[ROUTER_ROWS:2 * ROUTER_ROWS, :]
              + _dot_nt(wr_ref[0:ROUTER_ROWS, :], h_lo) + br_ref[...])
    gl = logits[N_EXPERTS:N_EXPERTS + N_GROUPS, :]
    gi = lax.broadcasted_iota(I32, gl.shape, 0).astype(F32)
    gmax = jnp.max(gl, axis=0, keepdims=True)
    g_sel = jnp.min(jnp.where(gl == gmax, gi, float(N_GROUPS)), axis=0, keepdims=True)
    pg_sel = 1.0 / jnp.sum(jnp.exp(gl - gmax), axis=0, keepdims=True)

    el = jnp.zeros((EXPERTS_PER_GROUP, ts), F32)
    for gidx in range(N_GROUPS):
        lo = gidx * EXPERTS_PER_GROUP
        el = jnp.where(g_sel == float(gidx), logits[lo:lo + EXPERTS_PER_GROUP, :], el)
    ei = lax.broadcasted_iota(I32, el.shape, 0).astype(F32)
    m1 = jnp.max(el, axis=0, keepdims=True)
    i1 = jnp.min(jnp.where(el == m1, ei, float(EXPERTS_PER_GROUP)), axis=0, keepdims=True)
    rest = jnp.where(ei == i1, -jnp.inf, el)
    m2 = jnp.max(rest, axis=0, keepdims=True)
    i2 = jnp.min(jnp.where(rest == m2, ei, float(EXPERTS_PER_GROUP)), axis=0, keepdims=True)
    ratio = jnp.exp(m2 - m1)
    gate1 = pg_sel / (1.0 + ratio)
    gate2 = pg_sel * ratio / (1.0 + ratio)
    e1 = g_sel * float(EXPERTS_PER_GROUP) + i1
    e2 = g_sel * float(EXPERTS_PER_GROUP) + i2

    xi = lax.broadcasted_iota(I32, (N_EXPERTS, ts), 0).astype(F32)
    oh1 = jnp.where(xi == e1, 1.0, 0.0)
    oh2 = jnp.where(xi == e2, 1.0, 0.0)
    oh = oh1 + oh2
    n_blk = ts // LANES
    stacked = jnp.concatenate([oh[:, c * LANES:(c + 1) * LANES] for c in range(n_blk)], axis=0)
    within = _dot(stacked.astype(BF16), upper_ref[...])
    totals = jnp.sum(stacked, axis=1, keepdims=True)
    run = carry_ref[...]
    before = []
    for c in range(n_blk):
        before.append(within[c * N_EXPERTS:(c + 1) * N_EXPERTS, :] + run)
        run = run + totals[c * N_EXPERTS:(c + 1) * N_EXPERTS, :]
    before = jnp.concatenate(before, axis=1)
    rank1 = jnp.sum(oh1 * before, axis=0, keepdims=True)
    rank2 = jnp.sum(oh2 * before, axis=0, keepdims=True)
    carry_ref[...] = run
    cnt_ref[...] = jnp.broadcast_to(run, cnt_ref.shape)

    route_ref[0:1, :] = e1.astype(I32)
    route_ref[1:2, :] = e2.astype(I32)
    route_ref[2:3, :] = rank1.astype(I32)
    route_ref[3:4, :] = rank2.astype(I32)
    route_ref[4:5, :] = lax.bitcast_convert_type(gate1, I32)
    route_ref[5:6, :] = lax.bitcast_convert_type(gate2, I32)
    route_ref[6:8, :] = jnp.zeros((2, ts), I32)


def _xattn_router(x, g_x, w_q, k_mem, v_mem, layer, w_o, g_f, w_grp, b_grp, w_exp, b_exp):
    b, s, d = x.shape
    nm = k_mem.shape[2]
    t = b * s
    ts = min(TILE_XATTN, s)
    nj = s // ts
    row = lambda v: v.reshape(1, -1)
    const = lambda shape: pl.BlockSpec(shape, lambda i, j: (0,) * len(shape))
    pad = ROUTER_ROWS - N_GROUPS - N_EXPERTS
    w_r = jnp.pad(jnp.concatenate([w_exp, w_grp], axis=1).T, ((0, pad), (0, 0)))
    w_r_hi = w_r.astype(BF16)
    w_r_split = jnp.concatenate([w_r_hi, (w_r - w_r_hi.astype(F32)).astype(BF16)], axis=0)
    b_r = jnp.pad(jnp.concatenate([b_exp, b_grp]), (0, pad)).reshape(ROUTER_ROWS, 1)
    ti = jnp.arange(LANES)
    upper = (ti[:, None] < ti[None, :]).astype(BF16)
    return pl.pallas_call(
        _xattn_router_kernel,
        out_shape=(
            jax.ShapeDtypeStruct(x.shape, F32),
            jax.ShapeDtypeStruct((t, d // 2), U32),
            jax.ShapeDtypeStruct((8, t), I32),
            jax.ShapeDtypeStruct((N_EXPERTS, 128), F32),
        ),
        grid=(b, nj),
        in_specs=[
            pl.BlockSpec((1, ts, d), lambda i, j: (i, j, 0)),
            const((1, d)), const((d, d)),
            pl.BlockSpec((1, 1, nm, d), lambda i, j: (layer, i, 0, 0)),
            pl.BlockSpec((1, 1, nm, d), lambda i, j: (layer, i, 0, 0)),
            const((d, d)), const((1, d)), const((2 * ROUTER_ROWS, d)), const((ROUTER_ROWS, 1)), const((LANES, LANES)),
        ],
        out_specs=(
            pl.BlockSpec((1, ts, d), lambda i, j: (i, j, 0)),
            pl.BlockSpec((ts, d // 2), lambda i, j: (i * nj + j, 0)),
            pl.BlockSpec((8, ts), lambda i, j: (0, i * nj + j)),
            pl.BlockSpec((N_EXPERTS, 128), lambda i, j: (0, 0)),
        ),
        scratch_shapes=[pltpu.VMEM((N_EXPERTS, 1), F32)],
        compiler_params=_cparams(("arbitrary", "arbitrary")),
        name="xattn_router",
    )(x, row(g_x), w_q.astype(BF16), k_mem, v_mem, w_o.astype(BF16), row(g_f), w_r_split, b_r, upper)


def _sc_mesh():
    return plsc.VectorSubcoreMesh(core_axis_name="c", subcore_axis_name="s",
                                  num_cores=SC_CORES, num_subcores=SC_SUBCORES)


def _sc_worker():
    return lax.axis_index("s") * SC_CORES + lax.axis_index("c")


def _dispatch(h_packed, dest, n_rows):
    t, w = h_packed.shape
    chunk = SC_DISPATCH_CHUNK
    per_worker = t // SC_WORKERS
    n_chunks = per_worker // chunk
    assert n_chunks % 2 == 0 and n_chunks * chunk * SC_WORKERS == t
    dest = dest.reshape(TOP_K, SC_WORKERS * n_chunks, chunk)
    rows_buf = pltpu.VMEM((chunk, w), U32)

    @functools.partial(
        pl.kernel, mesh=_sc_mesh(),
        out_type=jax.ShapeDtypeStruct((n_rows, w), U32),
        scratch_types=[pltpu.VMEM((n_chunks, chunk), I32), pltpu.VMEM((n_chunks, chunk), I32), rows_buf, rows_buf,
                       pltpu.SemaphoreType.DMA((2,)), pltpu.SemaphoreType.DMA((2, TOP_K))],
        name="moe_dispatch_sc",
    )
    def run(h_hbm, d0_hbm, d1_hbm, xbuf_hbm, idx0_v, idx1_v, buf_a, buf_b, read_sem, write_sem):
        wid = _sc_worker()
        pltpu.sync_copy(d0_hbm.at[pl.ds(wid * n_chunks, n_chunks)], idx0_v)
        pltpu.sync_copy(d1_hbm.at[pl.ds(wid * n_chunks, n_chunks)], idx1_v)

        @pl.loop(0, n_chunks, step=2)
        def _(i):
            reads = [pltpu.async_copy(h_hbm.at[pl.ds(wid * per_worker + (i + j) * chunk, chunk)], buf, read_sem.at[j])
                     for j, buf in enumerate((buf_a, buf_b))]
            writes = []
            for j, buf in enumerate((buf_a, buf_b)):
                reads[j].wait()
                writes.append(pltpu.async_copy(buf, xbuf_hbm.at[idx0_v.at[i + j]], write_sem.at[j, 0]))
                writes.append(pltpu.async_copy(buf, xbuf_hbm.at[idx1_v.at[i + j]], write_sem.at[j, 1]))
            for copy in writes:
                copy.wait()

    return run(h_packed, dest[0], dest[1])


def _gather_pairs(y_buf, dest):
    t = dest.shape[1]
    w = y_buf.shape[1]
    chunk = SC_GATHER_CHUNK
    per_worker = t // SC_WORKERS
    n_chunks = per_worker // chunk
    assert n_chunks % 2 == 0 and n_chunks * chunk * SC_WORKERS == t
    dest = dest.reshape(TOP_K, SC_WORKERS * n_chunks, chunk)
    out = jax.ShapeDtypeStruct((t, w), U32)
    rows_buf = pltpu.VMEM((chunk, w), U32)

    @functools.partial(
        pl.kernel, mesh=_sc_mesh(),
        out_type=(out, out),
        scratch_types=[pltpu.VMEM((n_chunks, chunk), I32), pltpu.VMEM((n_chunks, chunk), I32),
                       rows_buf, rows_buf, rows_buf, rows_buf, pltpu.SemaphoreType.DMA((2, TOP_K))],
        name="moe_gather_sc",
    )
    def run(y_hbm, d0_hbm, d1_hbm, y0_hbm, y1_hbm, idx0_v, idx1_v, buf_a0, buf_a1, buf_b0, buf_b1, sem):
        wid = _sc_worker()
        pltpu.sync_copy(d0_hbm.at[pl.ds(wid * n_chunks, n_chunks)], idx0_v)
        pltpu.sync_copy(d1_hbm.at[pl.ds(wid * n_chunks, n_chunks)], idx1_v)

        @pl.loop(0, n_chunks, step=2)
        def _(i):
            bufs = ((buf_a0, buf_a1), (buf_b0, buf_b1))
            gathers = [[pltpu.async_copy(y_hbm.at[idx_v.at[i + j]], bufs[j][k], sem.at[j, k])
                        for k, idx_v in enumerate((idx0_v, idx1_v))] for j in range(2)]
            stores = []
            for j in range(2):
                rows = pl.ds(wid * per_worker + (i + j) * chunk, chunk)
                for k, out_hbm in enumerate((y0_hbm, y1_hbm)):
                    gathers[j][k].wait()
                    stores.append(pltpu.async_copy(bufs[j][k], out_hbm.at[rows], sem.at[j, k]))
            for copy in stores:
                copy.wait()

    return run(y_buf, dest[0], dest[1])


def _expert_kernel(be_ref, bf_ref, bv_ref, slot_ref, next_ref, x_ref, wg_hbm, wu_hbm, wd_hbm, y_ref,
                   wg_f32, wu_f32, wd_f32, wg_bf, wu_bf, wd_bf, sems, *, layer):
    blk = pl.program_id(0)
    valid = bv_ref[blk]

    def weight_copies(expert, slot):
        return [pltpu.make_async_copy(src.at[layer, expert], dst.at[slot], sems.at[slot, i])
                for i, (src, dst) in enumerate(((wg_hbm, wg_f32), (wu_hbm, wu_f32), (wd_hbm, wd_f32)))]

    @pl.when(blk == 0)
    def _():
        for copy in weight_copies(be_ref[0], 0):
            copy.start()

    @pl.when(bf_ref[blk] == 1)
    def _():
        slot = slot_ref[blk]
        for copy in weight_copies(be_ref[blk], slot):
            copy.wait()

        @pl.when(next_ref[blk] >= 0)
        def _():
            for copy in weight_copies(next_ref[blk], 1 - slot):
                copy.start()

        wg_bf[...] = wg_f32[slot].astype(BF16)
        wu_bf[...] = wu_f32[slot].astype(BF16)
        wd_bf[...] = wd_f32[slot].astype(BF16)

    @pl.when(valid > 0)
    def _():
        live = lax.broadcasted_iota(I32, x_ref.shape, 0) < valid
        xb = _unpack_bf16_pairs(jnp.where(live, x_ref[...], jnp.uint32(0))).astype(BF16)
        gt = _dot(xb, wg_bf[...])
        up = _dot(xb, wu_bf[...])
        act = (gt * _sigmoid(gt) * up).astype(BF16)
        y_ref[...] = _pack_bf16_pairs(_dot(act, wd_bf[...]))

    @pl.when(valid <= 0)
    def _():
        y_ref[...] = jnp.zeros(y_ref.shape, U32)


def _experts(x_buf, block_expert, block_first, block_valid, w_gate, w_up, w_down, layer):
    n_rows, w = x_buf.shape
    d, de = w_gate.shape[2], w_gate.shape[3]
    bm = MOE_BLOCK_ROWS
    n = n_rows // bm
    block_slot = (jnp.cumsum(block_first) - 1) % 2
    idx = jnp.arange(n, dtype=I32)
    later_first = jnp.concatenate([jnp.where(block_first[1:] == 1, idx[1:], n), jnp.full((1,), n, I32)])
    next_first = lax.cummin(later_first, reverse=True)
    block_next = jnp.where(next_first < n, block_expert[jnp.minimum(next_first, n - 1)], -1).astype(I32)
    any_space = pl.BlockSpec(memory_space=pl.ANY)
    grid_spec = pltpu.PrefetchScalarGridSpec(
        num_scalar_prefetch=5,
        grid=(n,),
        in_specs=[pl.BlockSpec((bm, w), lambda i, *_: (i, 0)), any_space, any_space, any_space],
        out_specs=pl.BlockSpec((bm, w), lambda i, *_: (i, 0)),
        scratch_shapes=[pltpu.VMEM((2, d, de), F32), pltpu.VMEM((2, d, de), F32), pltpu.VMEM((2, de, d), F32),
                        pltpu.VMEM((d, de), BF16), pltpu.VMEM((d, de), BF16), pltpu.VMEM((de, d), BF16),
                        pltpu.SemaphoreType.DMA((2, 3))],
    )
    return pl.pallas_call(
        functools.partial(_expert_kernel, layer=layer),
        out_shape=jax.ShapeDtypeStruct((n_rows, w), U32),
        grid_spec=grid_spec,
        compiler_params=_cparams(("arbitrary",)),
        name="moe_experts",
    )(block_expert, block_first, block_valid, block_slot.astype(I32), block_next, x_buf, w_gate, w_up, w_down)


def _combine_kernel(x_ref, gate_ref, y0_ref, y1_ref, gfin_ref, *rest, final_norm):
    o_ref = rest[-1]
    out = _moe_residual(x_ref[...], gate_ref[...], y0_ref[...], y1_ref[...])
    if final_norm:
        out = _rms(out, gfin_ref[...])
    o_ref[...] = out


def _combine(x2, y0, y1, gates, g_final, final_norm, first_row=0, out_so_far=None):
    t, d = x2.shape
    rows, w = y0.shape
    ts = min(TILE_COMBINE, rows)
    off = first_row // ts
    in_specs = [
        pl.BlockSpec((ts, d), lambda i: (i + off, 0)),
        pl.BlockSpec((8, ts), lambda i: (0, i + off)),
        pl.BlockSpec((ts, w), lambda i: (i, 0)),
        pl.BlockSpec((ts, w), lambda i: (i, 0)),
        pl.BlockSpec((1, d), lambda i: (0, 0)),
    ]
    args = [x2, gates, y0, y1, g_final.reshape(1, d)]
    aliases = {}
    if out_so_far is not None:
        in_specs.append(pl.BlockSpec(memory_space=pl.ANY))
        args.append(out_so_far)
        aliases = {len(args) - 1: 0}
    return pl.pallas_call(
        functools.partial(_combine_kernel, final_norm=final_norm),
        out_shape=jax.ShapeDtypeStruct((t, d), F32),
        grid=(rows // ts,),
        in_specs=in_specs,
        out_specs=pl.BlockSpec((ts, d), lambda i: (i + off, 0)),
        input_output_aliases=aliases,
        compiler_params=_cparams(("arbitrary",)),
        name="moe_combine",
    )(*args)


def _moe_layout(route, counts):
    bm = MOE_BLOCK_ROWS
    t = route.shape[1]
    n_blocks = (t * TOP_K) // bm + N_EXPERTS
    cnt = counts[:, 0].astype(I32)
    padded = (cnt + bm - 1) // bm * bm
    pad_ends = jnp.cumsum(padded)
    pad_off = pad_ends - padded
    experts = jnp.arange(N_EXPERTS, dtype=I32)
    hit = route[0:TOP_K, :, None] == experts
    dest = jnp.sum(jnp.where(hit, pad_off, 0), axis=-1) + route[TOP_K:2 * TOP_K]
    gates = route
    starts = jnp.arange(n_blocks, dtype=I32) * bm
    block_expert = jnp.minimum(jnp.sum((pad_ends[None, :] <= starts[:, None]).astype(I32), axis=1),
                               N_EXPERTS - 1)
    block_first = jnp.concatenate([jnp.ones((1,), I32), (block_expert[1:] != block_expert[:-1]).astype(I32)])
    own = block_expert[:, None] == experts
    block_valid = jnp.clip(jnp.sum(jnp.where(own, cnt + pad_off, 0), axis=1) - starts, 0, bm)
    block_valid = jnp.where(starts < pad_ends[-1], block_valid, 0).astype(I32)
    return dest, gates, block_expert, block_first, block_valid, n_blocks * bm


def kernel(x, mem, norm_mix, norm_xattn, norm_ffn, norm_mem, norm_final, conv_w_in, conv_b_in, conv_w_dw,
           conv_b_dw, conv_ln_g, conv_ln_b, conv_w_out, conv_b_out, gla_w_in, gla_w_a2, gla_b_a, gla_norm_g,
           gla_w_o, xa_w_q, xa_w_kv, xa_w_o, moe_w_grp, moe_b_grp, moe_w_exp, moe_b_exp, moe_w_gate, moe_w_up,
           moe_w_down):
    b, s, d = x.shape
    depth = norm_mix.shape[0]
    k_mem, v_mem = _mem_kv(mem, norm_mem, xa_w_kv)
    moe = None
    for i in range(depth):
        j = i // 2
        if i % 2 == 0:
            if moe is not None:
                x = _combine(x.reshape(b * s, d), moe[1], moe[2], moe[0], norm_final, False).reshape(b, s, d)
            x = _conv_mixer(x, norm_mix[i], conv_w_in[j], conv_b_in[j], conv_w_dw[j], conv_b_dw[j],
                            conv_ln_g[j], conv_ln_b[j], conv_w_out[j], conv_b_out[j])
        else:
            x = _gla_mixer(x, *moe, norm_mix[i], gla_w_in[j], gla_w_a2[j], gla_b_a[j], gla_norm_g[j], gla_w_o[j])
        x2, h_packed, route, counts = _xattn_router(
            x, norm_xattn[i], xa_w_q[i], k_mem, v_mem, i, xa_w_o[i], norm_ffn[i],
            moe_w_grp[i], moe_b_grp[i], moe_w_exp[i], moe_b_exp[i])
        dest, gates, block_expert, block_first, block_valid, n_rows = _moe_layout(route, counts)
        x_buf = _dispatch(h_packed, dest, n_rows)
        y_buf = _experts(x_buf, block_expert, block_first, block_valid, moe_w_gate, moe_w_up, moe_w_down, i)
        if i < depth - 1:
            y0, y1 = _gather_pairs(y_buf, dest)
            x, moe = x2, (gates, y0, y1)
    part = (b * s) // FINAL_PARTS
    out = None
    for p in range(FINAL_PARTS):
        y0, y1 = _gather_pairs(y_buf, dest[:, p * part:(p + 1) * part])
        out = _combine(x2.reshape(b * s, d), y0, y1, gates, norm_final, True, p * part, out)
    return out.reshape(b, s, d)
```

```python
import functools

import jax
import jax.numpy as jnp
from jax import lax
from jax.experimental import pallas as pl
from jax.experimental.pallas import tpu as pltpu
from jax.experimental.pallas import tpu_sc as plsc

F32 = jnp.float32
BF16 = jnp.bfloat16
I32 = jnp.int32
U32 = jnp.uint32

EPS = 1e-6
CONV_KERNEL = 31
CONV_CARRY = 32
CONV_ROWS = 64
CONV_COLS = 256
SUBLANES = 8
LANES = 128
GLA_HEADS = 4
GLA_RANK = 16
GLA_RANK_PAD = 128
GLA_TAU = 16.0
GLA_LEAF = 32
XATTN_HEADS = 4
N_GROUPS = 4
EXPERTS_PER_GROUP = 8
N_EXPERTS = N_GROUPS * EXPERTS_PER_GROUP
ROUTER_ROWS = 40
TOP_K = 2

TILE_CONV = 512
TILE_GLA = 256
TILE_XATTN = 1024
TILE_COMBINE = 512
FINAL_PARTS = 2
SC_CORES = 2
SC_SUBCORES = 16
SC_WORKERS = SC_CORES * SC_SUBCORES
SC_DISPATCH_CHUNK = 64
SC_GATHER_CHUNK = 32
MOE_BLOCK_ROWS = 512
VMEM_LIMIT = 56 * 1024 * 1024


def _cparams(sem):
    return pltpu.CompilerParams(dimension_semantics=sem, vmem_limit_bytes=VMEM_LIMIT)


def _rms(x, g):
    return x * lax.rsqrt(jnp.mean(x * x, axis=-1, keepdims=True) + EPS) * g


def _sigmoid(x):
    return 0.5 * jnp.tanh(0.5 * x) + 0.5


def _split_bf16(x):
    hi = x.astype(BF16)
    lo = (x - hi.astype(F32)).astype(BF16)
    return hi, lo


def _dot(a, b):
    return jnp.dot(a, b, preferred_element_type=F32)


def _dot_nt(a, b):
    return lax.dot_general(a, b, (((1,), (1,)), ((), ())), preferred_element_type=F32)


def _dot_tn(a, b):
    return lax.dot_general(a, b, (((0,), (0,)), ((), ())), preferred_element_type=F32)


def _pack_bf16_pairs(x):
    w = x.shape[1] // 2
    hi = lax.bitcast_convert_type(x[:, :w].astype(BF16).astype(F32), U32)
    lo = lax.bitcast_convert_type(x[:, w:].astype(BF16).astype(F32), U32)
    return hi | (lo >> 16)


def _unpack_bf16_pairs(p):
    hi = lax.bitcast_convert_type(p & jnp.uint32(0xFFFF0000), F32)
    lo = lax.bitcast_convert_type(p << 16, F32)
    return jnp.concatenate([hi, lo], axis=1)


def _memkv_kernel(mem_ref, g_ref, w_ref, k_ref, v_ref):
    d = mem_ref.shape[-1]
    mn = _rms(mem_ref[0], g_ref[...]).astype(BF16)
    kv = _dot(mn, w_ref[0])
    k_ref[0, 0] = kv[:, :d].astype(BF16)
    v_ref[0, 0] = kv[:, d:].astype(BF16)


def _mem_kv(mem, norm_mem, w_kv):
    b, nm, d = mem.shape
    depth = w_kv.shape[0]
    out = jax.ShapeDtypeStruct((depth, b, nm, d), BF16)
    return pl.pallas_call(
        _memkv_kernel,
        out_shape=(out, out),
        grid=(depth, b),
        in_specs=[
            pl.BlockSpec((1, nm, d), lambda l, i: (i, 0, 0)),
            pl.BlockSpec((1, d), lambda l, i: (0, 0)),
            pl.BlockSpec((1, d, 2 * d), lambda l, i: (l, 0, 0)),
        ],
        out_specs=(
            pl.BlockSpec((1, 1, nm, d), lambda l, i: (l, i, 0, 0)),
            pl.BlockSpec((1, 1, nm, d), lambda l, i: (l, i, 0, 0)),
        ),
        compiler_params=_cparams(("arbitrary", "arbitrary")),
        name="mem_kv",
    )(mem, norm_mem.reshape(1, d), w_kv.astype(BF16))


def _conv_kernel(x_ref, g_ref, win_ref, bin_ref, wdw_ref, bdw_ref, lng_ref, lnb_ref, wout_ref, bout_ref,
                 o_ref, ext_ref, conv_ref):
    ts, d = x_ref.shape[1], x_ref.shape[2]

    @pl.when(pl.program_id(1) == 0)
    def _():
        ext_ref[...] = jnp.zeros(ext_ref.shape, F32)

    x = x_ref[0]
    h = _rms(x, g_ref[...]).astype(BF16)
    u = _dot(h, win_ref[...]) + bin_ref[...]
    glu = u[:, :d] * _sigmoid(u[:, d:])
    for b in range(SUBLANES):
        ext_ref[b, CONV_CARRY - b:CONV_CARRY - b + ts, :] = glu

    first = CONV_CARRY - (CONV_KERNEL - 1)

    def chunk(i, carry):
        r0 = pl.multiple_of(i * CONV_ROWS, CONV_ROWS)
        for c0 in range(0, d, CONV_COLS):
            cols = slice(c0, c0 + CONV_COLS)
            acc = [jnp.zeros((SUBLANES, CONV_COLS), F32) for _ in range(CONV_ROWS // SUBLANES)]
            for k in range(CONV_KERNEL):
                b = (first + k) % SUBLANES
                wk = wdw_ref[k * SUBLANES:(k + 1) * SUBLANES, cols]
                for j in range(CONV_ROWS // SUBLANES):
                    rows = pl.ds(r0 + (first + k - b) + j * SUBLANES, SUBLANES)
                    acc[j] = acc[j] + ext_ref[b, rows, cols] * wk
            for j in range(CONV_ROWS // SUBLANES):
                conv_ref[pl.ds(r0 + j * SUBLANES, SUBLANES), cols] = acc[j]
        return carry

    lax.fori_loop(0, ts // CONV_ROWS, chunk, 0)
    for b in range(SUBLANES):
        ext_ref[b, 0:CONV_CARRY, :] = ext_ref[b, ts:ts + CONV_CARRY, :]

    c = conv_ref[...] + bdw_ref[...]
    mu = jnp.mean(c, axis=-1, keepdims=True)
    cc = c - mu
    var = jnp.mean(cc * cc, axis=-1, keepdims=True)
    un = cc * lax.rsqrt(var + EPS) * lng_ref[...] + lnb_ref[...]
    act = (un * _sigmoid(un)).astype(BF16)
    o_ref[0] = x + _dot(act, wout_ref[...]) + bout_ref[...]


def _conv_mixer(x, g, w_in, b_in, w_dw, b_dw, ln_g, ln_b, w_out, b_out):
    b, s, d = x.shape
    ts = min(TILE_CONV, s)
    row = lambda v: v.reshape(1, -1)
    const = lambda shape: pl.BlockSpec(shape, lambda i, j: (0,) * len(shape))
    return pl.pallas_call(
        _conv_kernel,
        out_shape=jax.ShapeDtypeStruct(x.shape, F32),
        grid=(b, s // ts),
        in_specs=[
            pl.BlockSpec((1, ts, d), lambda i, j: (i, j, 0)),
            const((1, d)), const((d, 2 * d)), const((1, 2 * d)), const((CONV_KERNEL * SUBLANES, d)), const((1, d)),
            const((1, d)), const((1, d)), const((d, d)), const((1, d)),
        ],
        out_specs=pl.BlockSpec((1, ts, d), lambda i, j: (i, j, 0)),
        scratch_shapes=[pltpu.VMEM((SUBLANES, CONV_CARRY + ts, d), F32), pltpu.VMEM((ts, d), F32)],
        compiler_params=_cparams(("arbitrary", "arbitrary")),
        name="conv_mixer",
    )(x, row(g), w_in.astype(BF16), row(b_in), jnp.repeat(w_dw, SUBLANES, axis=0), row(b_dw), row(ln_g), row(ln_b),
      w_out.astype(BF16), row(b_out))


def _gla_levels(ts):
    sizes = [GLA_LEAF]
    while sizes[-1] < ts:
        sizes.append(sizes[-1] * 2)
    return sizes


def _moe_residual(x, route, y0, y1):
    g = lax.bitcast_convert_type(route, F32)
    g = jnp.concatenate([g, jnp.zeros((LANES - g.shape[0], g.shape[1]), F32)], axis=0).T
    gate0 = g[:, 2 * TOP_K:2 * TOP_K + 1]
    gate1 = g[:, 2 * TOP_K + 1:2 * TOP_K + 2]
    return x + _unpack_bf16_pairs(y0) * gate0 + _unpack_bf16_pairs(y1) * gate1


def _gla_kernel(x_ref, gate_ref, y0_ref, y1_ref, g_ref, wq_ref, wk_ref, wv_ref, wa_ref, wr_ref, wa2_ref, ba_ref,
                ng_ref, wo_ref, o_ref, state_ref, x_s, q_s, k_s, v_s, og_s, la_s, *, tiles_per_seq):
    j = pl.program_id(0)
    dkh = wq_ref.shape[1] // GLA_HEADS

    @pl.when(j == 0)
    def _():
        for ref in (x_s, q_s, k_s, og_s, la_s):
            ref[...] = jnp.zeros(ref.shape, F32)
        v_s[...] = jnp.zeros(v_s.shape, BF16)

    @pl.when((j == 0) | ((j - 1) % tiles_per_seq == 0))
    def _():
        state_ref[...] = jnp.zeros(state_ref.shape, F32)

    def stage_a(slot):
        x = _moe_residual(x_ref[0], gate_ref[...], y0_ref[...], y1_ref[...])
        h = _rms(x, g_ref[...]).astype(BF16)
        x_s[slot] = x
        q_s[slot] = _dot(h, wq_ref[...]) * (dkh ** -0.5)
        k_s[slot] = _dot(h, wk_ref[...])
        v_s[slot] = _dot(h, wv_ref[...]).astype(BF16)
        r = _dot(h, wr_ref[...])
        og_s[slot] = r * _sigmoid(r)
        a_hi, a_lo = _split_bf16(_dot(h, wa_ref[...]))
        w2_hi, w2_lo = wa2_ref[0], wa2_ref[1]
        z = _dot(a_hi, w2_hi) + _dot(a_lo, w2_hi) + _dot(a_hi, w2_lo) + ba_ref[...]
        la_s[slot] = -(jnp.maximum(-z, 0.0) + jnp.log(1.0 + jnp.exp(-jnp.abs(z)))) * (1.0 / GLA_TAU)

    def stage_b(slot):
        _gla_recurrence(x_s[slot], q_s[slot], k_s[slot], v_s[slot], og_s[slot], la_s[slot],
                        ng_ref, wo_ref, o_ref, state_ref)

    @pl.when(j % 2 == 0)
    def _():
        stage_b(1)
        stage_a(0)

    @pl.when(j % 2 == 1)
    def _():
        stage_b(0)
        stage_a(1)


def _gla_recurrence(x, q, k, v, out_gate, log_a, ng_ref, wo_ref, o_ref, state_ref):
    ts, d = x.shape
    dk = q.shape[1]
    dkh = dk // GLA_HEADS
    dvh = d // GLA_HEADS

    row = lax.broadcasted_iota(I32, (ts, ts), 0)
    col = lax.broadcasted_iota(I32, (ts, ts), 1)
    tri = jnp.where(col <= row, 1.0, 0.0).astype(BF16)
    la_hi, la_lo = _split_bf16(log_a)
    bcum = _dot(tri, la_hi) + _dot(tri, la_lo)
    b_last = bcum[ts - 1:ts, :]

    q_in = (q * jnp.exp(bcum)).astype(BF16)
    k_out = (k * jnp.exp(b_last - bcum)).astype(BF16)

    sizes = _gla_levels(ts)
    scores = [None] * GLA_HEADS
    for lvl, size in enumerate(sizes):
        half = size // 2
        same_block = (row & -size) == (col & -size)
        if lvl == 0:
            pair = same_block & (col <= row)
            q_ok = k_ok = None
        else:
            pair = same_block & ((row & (size - 1)) >= half) & ((col & (size - 1)) < half)
            pos = lax.broadcasted_iota(I32, (ts, dk), 0) & (size - 1)
            q_ok = pos >= half
            k_ok = pos < half
        ref = jnp.concatenate(
            [jnp.broadcast_to(bcum[r0 + half:r0 + half + 1, :], (size, dk)) for r0 in range(0, ts, size)], axis=0)
        ql = q * jnp.exp(bcum - ref)
        kl = k * jnp.exp(ref - bcum)
        if q_ok is not None:
            ql = jnp.where(q_ok, ql, 0.0)
            kl = jnp.where(k_ok, kl, 0.0)
        ql = ql.astype(BF16)
        kl = kl.astype(BF16)
        for hd in range(GLA_HEADS):
            c0 = hd * dkh
            a = _dot_nt(ql[:, c0:c0 + dkh], kl[:, c0:c0 + dkh])
            a = jnp.where(pair, a, 0.0)
            scores[hd] = a if scores[hd] is None else scores[hd] + a

    outs = []
    for hd in range(GLA_HEADS):
        c0 = hd * dkh
        v_h = v[:, hd * dvh:(hd + 1) * dvh]
        st = state_ref[hd]
        o_h = _dot(scores[hd].astype(BF16), v_h) + _dot_nt(q_in[:, c0:c0 + dkh], st.astype(BF16))
        decay = jnp.exp(b_last[:, c0:c0 + dkh])
        state_ref[hd] = st * decay + _dot_tn(v_h, k_out[:, c0:c0 + dkh])
        o_h = o_h * lax.rsqrt(jnp.mean(o_h * o_h, axis=-1, keepdims=True) + EPS) * ng_ref[...]
        outs.append(o_h)
    o = jnp.concatenate(outs, axis=1) * out_gate
    o_ref[0] = x + _dot(o.astype(BF16), wo_ref[...])


def _gla_mixer(x, gates, y0, y1, g, w_in, w_a2, b_a, norm_g, w_o):
    b, s, d = x.shape
    dk = w_a2.shape[1]
    ts = min(TILE_GLA, s)
    nj = s // ts
    row = lambda v: v.reshape(1, -1)
    const = lambda shape: pl.BlockSpec(shape, lambda j: (0,) * len(shape))
    wq = w_in[:, :dk].astype(BF16)
    wk = w_in[:, dk:2 * dk].astype(BF16)
    wv = w_in[:, 2 * dk:2 * dk + d].astype(BF16)
    wa = jnp.pad(w_in[:, 2 * dk + d:2 * dk + d + GLA_RANK], ((0, 0), (0, GLA_RANK_PAD - GLA_RANK))).astype(BF16)
    wr = w_in[:, 2 * dk + d + GLA_RANK:].astype(BF16)
    wa2 = jnp.pad(w_a2, ((0, GLA_RANK_PAD - GLA_RANK), (0, 0)))
    wa2_hi = wa2.astype(BF16)
    wa2_lo = (wa2 - wa2_hi.astype(F32)).astype(BF16)
    wa2_split = jnp.stack([wa2_hi, wa2_lo])
    dvh = d // GLA_HEADS
    n = b * nj

    def tile(j, lag):
        return jnp.clip(j - lag, 0, n - 1)

    tokens = lambda lag: (lambda j: (tile(j, lag), 0))
    return pl.pallas_call(
        functools.partial(_gla_kernel, tiles_per_seq=nj),
        out_shape=jax.ShapeDtypeStruct(x.shape, F32),
        grid=(n + 1,),
        in_specs=[
            pl.BlockSpec((1, ts, d), lambda j: (tile(j, 0) // nj, tile(j, 0) % nj, 0)),
            pl.BlockSpec((8, ts), lambda j: (0, tile(j, 0))),
            pl.BlockSpec((ts, d // 2), tokens(0)),
            pl.BlockSpec((ts, d // 2), tokens(0)),
            const((1, d)), const((d, dk)), const((d, dk)), const((d, d)), const((d, GLA_RANK_PAD)),
            const((d, d)), const((2, GLA_RANK_PAD, dk)), const((1, dk)), const((1, dvh)), const((d, d)),
        ],
        out_specs=pl.BlockSpec((1, ts, d), lambda j: (tile(j, 1) // nj, tile(j, 1) % nj, 0)),
        scratch_shapes=[pltpu.VMEM((GLA_HEADS, dvh, dk // GLA_HEADS), F32),
                        pltpu.VMEM((2, ts, d), F32), pltpu.VMEM((2, ts, dk), F32), pltpu.VMEM((2, ts, dk), F32),
                        pltpu.VMEM((2, ts, d), BF16), pltpu.VMEM((2, ts, d), F32), pltpu.VMEM((2, ts, dk), F32)],
        compiler_params=_cparams(("arbitrary",)),
        name="gla_mixer",
    )(x, gates, y0, y1, row(g), wq, wk, wv, wa, wr, wa2_split, row(b_a), row(norm_g), w_o.astype(BF16))


def _xattn_router_kernel(x_ref, gx_ref, wq_ref, k_ref, v_ref, wo_ref, gf_ref, wr_ref, br_ref, upper_ref,
                         x_out_ref, h_out_ref, route_ref, cnt_ref, carry_ref):
    ts, d = x_ref.shape[1], x_ref.shape[2]
    hd = d // XATTN_HEADS
    first = (pl.program_id(0) == 0) & (pl.program_id(1) == 0)

    @pl.when(first)
    def _():
        carry_ref[...] = jnp.zeros(carry_ref.shape, F32)

    x = x_ref[0]
    h = _rms(x, gx_ref[...]).astype(BF16)
    q = (_dot(h, wq_ref[...]) * (hd ** -0.5)).astype(BF16)
    k = k_ref[0, 0]
    v = v_ref[0, 0]
    outs = []
    for a in range(XATTN_HEADS):
        sl = slice(a * hd, (a + 1) * hd)
        s = _dot_nt(q[:, sl], k[:, sl])
        p = jnp.exp(s - jnp.max(s, axis=-1, keepdims=True))
        o = _dot(p.astype(BF16), v[:, sl]) / jnp.sum(p, axis=-1, keepdims=True)
        outs.append(o)
    att = jnp.concatenate(outs, axis=1).astype(BF16)
    x2 = x + _dot(att, wo_ref[...])
    x_out_ref[0] = x2

    hf = _rms(x2, gf_ref[...])
    h_out_ref[...] = _pack_bf16_pairs(hf)

    h_hi, h_lo = _split_bf16(hf)
    both = _dot_nt(wr_ref[...], h_hi)
    logits = (both[0:ROUTER_ROWS, :] + both[ROUTER_ROWS:2 * ROUTER_ROWS, :]
              + _dot_nt(wr_ref[0:ROUTER_ROWS, :], h_lo) + br_ref[...])
    gl = logits[N_EXPERTS:N_EXPERTS + N_GROUPS, :]
    gi = lax.broadcasted_iota(I32, gl.shape, 0).astype(F32)
    gmax = jnp.max(gl, axis=0, keepdims=True)
    g_sel = jnp.min(jnp.where(gl == gmax, gi, float(N_GROUPS)), axis=0, keepdims=True)
    pg_sel = 1.0 / jnp.sum(jnp.exp(gl - gmax), axis=0, keepdims=True)

    el = jnp.zeros((EXPERTS_PER_GROUP, ts), F32)
    for gidx in range(N_GROUPS):
        lo = gidx * EXPERTS_PER_GROUP
        el = jnp.where(g_sel == float(gidx), logits[lo:lo + EXPERTS_PER_GROUP, :], el)
    ei = lax.broadcasted_iota(I32, el.shape, 0).astype(F32)
    m1 = jnp.max(el, axis=0, keepdims=True)
    i1 = jnp.min(jnp.where(el == m1, ei, float(EXPERTS_PER_GROUP)), axis=0, keepdims=True)
    rest = jnp.where(ei == i1, -jnp.inf, el)
    m2 = jnp.max(rest, axis=0, keepdims=True)
    i2 = jnp.min(jnp.where(rest == m2, ei, float(EXPERTS_PER_GROUP)), axis=0, keepdims=True)
    ratio = jnp.exp(m2 - m1)
    gate1 = pg_sel / (1.0 + ratio)
    gate2 = pg_sel * ratio / (1.0 + ratio)
    e1 = g_sel * float(EXPERTS_PER_GROUP) + i1
    e2 = g_sel * float(EXPERTS_PER_GROUP) + i2

    xi = lax.broadcasted_iota(I32, (N_EXPERTS, ts), 0).astype(F32)
    oh1 = jnp.where(xi == e1, 1.0, 0.0)
    oh2 = jnp.where(xi == e2, 1.0, 0.0)
    oh = oh1 + oh2
    n_blk = ts // LANES
    stacked = jnp.concatenate([oh[:, c * LANES:(c + 1) * LANES] for c in range(n_blk)], axis=0)
    within = _dot(stacked.astype(BF16), upper_ref[...])
    totals = jnp.sum(stacked, axis=1, keepdims=True)
    run = carry_ref[...]
    before = []
    for c in range(n_blk):
        before.append(within[c * N_EXPERTS:(c + 1) * N_EXPERTS, :] + run)
        run = run + totals[c * N_EXPERTS:(c + 1) * N_EXPERTS, :]
    before = jnp.concatenate(before, axis=1)
    rank1 = jnp.sum(oh1 * before, axis=0, keepdims=True)
    rank2 = jnp.sum(oh2 * before, axis=0, keepdims=True)
    carry_ref[...] = run
    cnt_ref[...] = jnp.broadcast_to(run, cnt_ref.shape)

    route_ref[0:1, :] = e1.astype(I32)
    route_ref[1:2, :] = e2.astype(I32)
    route_ref[2:3, :] = rank1.astype(I32)
    route_ref[3:4, :] = rank2.astype(I32)
    route_ref[4:5, :] = lax.bitcast_convert_type(gate1, I32)
    route_ref[5:6, :] = lax.bitcast_convert_type(gate2, I32)
    route_ref[6:8, :] = jnp.zeros((2, ts), I32)


def _xattn_router(x, g_x, w_q, k_mem, v_mem, layer, w_o, g_f, w_grp, b_grp, w_exp, b_exp):
    b, s, d = x.shape
    nm = k_mem.shape[2]
    t = b * s
    ts = min(TILE_XATTN, s)
    nj = s // ts
    row = lambda v: v.reshape(1, -1)
    const = lambda shape: pl.BlockSpec(shape, lambda i, j: (0,) * len(shape))
    pad = ROUTER_ROWS - N_GROUPS - N_EXPERTS
    w_r = jnp.pad(jnp.concatenate([w_exp, w_grp], axis=1).T, ((0, pad), (0, 0)))
    w_r_hi = w_r.astype(BF16)
    w_r_split = jnp.concatenate([w_r_hi, (w_r - w_r_hi.astype(F32)).astype(BF16)], axis=0)
    b_r = jnp.pad(jnp.concatenate([b_exp, b_grp]), (0, pad)).reshape(ROUTER_ROWS, 1)
    ti = jnp.arange(LANES)
    upper = (ti[:, None] < ti[None, :]).astype(BF16)
    return pl.pallas_call(
        _xattn_router_kernel,
        out_shape=(
            jax.ShapeDtypeStruct(x.shape, F32),
            jax.ShapeDtypeStruct((t, d // 2), U32),
            jax.ShapeDtypeStruct((8, t), I32),
            jax.ShapeDtypeStruct((N_EXPERTS, 128), F32),
        ),
        grid=(b, nj),
        in_specs=[
            pl.BlockSpec((1, ts, d), lambda i, j: (i, j, 0)),
            const((1, d)), const((d, d)),
            pl.BlockSpec((1, 1, nm, d), lambda i, j: (layer, i, 0, 0)),
            pl.BlockSpec((1, 1, nm, d), lambda i, j: (layer, i, 0, 0)),
            const((d, d)), const((1, d)), const((2 * ROUTER_ROWS, d)), const((ROUTER_ROWS, 1)), const((LANES, LANES)),
        ],
        out_specs=(
            pl.BlockSpec((1, ts, d), lambda i, j: (i, j, 0)),
            pl.BlockSpec((ts, d // 2), lambda i, j: (i * nj + j, 0)),
            pl.BlockSpec((8, ts), lambda i, j: (0, i * nj + j)),
            pl.BlockSpec((N_EXPERTS, 128), lambda i, j: (0, 0)),
        ),
        scratch_shapes=[pltpu.VMEM((N_EXPERTS, 1), F32)],
        compiler_params=_cparams(("arbitrary", "arbitrary")),
        name="xattn_router",
    )(x, row(g_x), w_q.astype(BF16), k_mem, v_mem, w_o.astype(BF16), row(g_f), w_r_split, b_r, upper)


def _sc_mesh():
    return plsc.VectorSubcoreMesh(core_axis_name="c", subcore_axis_name="s",
                                  num_cores=SC_CORES, num_subcores=SC_SUBCORES)


def _sc_worker():
    return lax.axis_index("s") * SC_CORES + lax.axis_index("c")


def _dispatch(h_packed, dest, n_rows):
    t, w = h_packed.shape
    chunk = SC_DISPATCH_CHUNK
    per_worker = t // SC_WORKERS
    n_chunks = per_worker // chunk
    assert n_chunks % 2 == 0 and n_chunks * chunk * SC_WORKERS == t
    dest = dest.reshape(TOP_K, SC_WORKERS * n_chunks, chunk)
    rows_buf = pltpu.VMEM((chunk, w), U32)

    @functools.partial(
        pl.kernel, mesh=_sc_mesh(),
        out_type=jax.ShapeDtypeStruct((n_rows, w), U32),
        scratch_types=[pltpu.VMEM((n_chunks, chunk), I32), pltpu.VMEM((n_chunks, chunk), I32), rows_buf, rows_buf,
                       pltpu.SemaphoreType.DMA((2,)), pltpu.SemaphoreType.DMA((2, TOP_K))],
        name="moe_dispatch_sc",
    )
    def run(h_hbm, d0_hbm, d1_hbm, xbuf_hbm, idx0_v, idx1_v, buf_a, buf_b, read_sem, write_sem):
        wid = _sc_worker()
        pltpu.sync_copy(d0_hbm.at[pl.ds(wid * n_chunks, n_chunks)], idx0_v)
        pltpu.sync_copy(d1_hbm.at[pl.ds(wid * n_chunks, n_chunks)], idx1_v)

        @pl.loop(0, n_chunks, step=2)
        def _(i):
            reads = [pltpu.async_copy(h_hbm.at[pl.ds(wid * per_worker + (i + j) * chunk, chunk)], buf, read_sem.at[j])
                     for j, buf in enumerate((buf_a, buf_b))]
            writes = []
            for j, buf in enumerate((buf_a, buf_b)):
                reads[j].wait()
                writes.append(pltpu.async_copy(buf, xbuf_hbm.at[idx0_v.at[i + j]], write_sem.at[j, 0]))
                writes.append(pltpu.async_copy(buf, xbuf_hbm.at[idx1_v.at[i + j]], write_sem.at[j, 1]))
            for copy in writes:
                copy.wait()

    return run(h_packed, dest[0], dest[1])


def _gather_pairs(y_buf, dest):
    t = dest.shape[1]
    w = y_buf.shape[1]
    chunk = SC_GATHER_CHUNK
    per_worker = t // SC_WORKERS
    n_chunks = per_worker // chunk
    assert n_chunks % 2 == 0 and n_chunks * chunk * SC_WORKERS == t
    dest = dest.reshape(TOP_K, SC_WORKERS * n_chunks, chunk)
    out = jax.ShapeDtypeStruct((t, w), U32)
    rows_buf = pltpu.VMEM((chunk, w), U32)

    @functools.partial(
        pl.kernel, mesh=_sc_mesh(),
        out_type=(out, out),
        scratch_types=[pltpu.VMEM((n_chunks, chunk), I32), pltpu.VMEM((n_chunks, chunk), I32),
                       rows_buf, rows_buf, rows_buf, rows_buf, pltpu.SemaphoreType.DMA((2, TOP_K))],
        name="moe_gather_sc",
    )
    def run(y_hbm, d0_hbm, d1_hbm, y0_hbm, y1_hbm, idx0_v, idx1_v, buf_a0, buf_a1, buf_b0, buf_b1, sem):
        wid = _sc_worker()
        pltpu.sync_copy(d0_hbm.at[pl.ds(wid * n_chunks, n_chunks)], idx0_v)
        pltpu.sync_copy(d1_hbm.at[pl.ds(wid * n_chunks, n_chunks)], idx1_v)

        @pl.loop(0, n_chunks, step=2)
        def _(i):
            bufs = ((buf_a0, buf_a1), (buf_b0, buf_b1))
            gathers = [[pltpu.async_copy(y_hbm.at[idx_v.at[i + j]], bufs[j][k], sem.at[j, k])
                        for k, idx_v in enumerate((idx0_v, idx1_v))] for j in range(2)]
            stores = []
            for j in range(2):
                rows = pl.ds(wid * per_worker + (i + j) * chunk, chunk)
                for k, out_hbm in enumerate((y0_hbm, y1_hbm)):
                    gathers[j][k].wait()
                    stores.append(pltpu.async_copy(bufs[j][k], out_hbm.at[rows], sem.at[j, k]))
            for copy in stores:
                copy.wait()

    return run(y_buf, dest[0], dest[1])


def _expert_kernel(be_ref, bf_ref, bv_ref, slot_ref, next_ref, x_ref, wg_hbm, wu_hbm, wd_hbm, y_ref,
                   wg_f32, wu_f32, wd_f32, wg_bf, wu_bf, wd_bf, sems, *, layer):
    blk = pl.program_id(0)
    valid = bv_ref[blk]

    def weight_copies(expert, slot):
        return [pltpu.make_async_copy(src.at[layer, expert], dst.at[slot], sems.at[slot, i])
                for i, (src, dst) in enumerate(((wg_hbm, wg_f32), (wu_hbm, wu_f32), (wd_hbm, wd_f32)))]

    @pl.when(blk == 0)
    def _():
        for copy in weight_copies(be_ref[0], 0):
            copy.start()

    @pl.when(bf_ref[blk] == 1)
    def _():
        slot = slot_ref[blk]
        for copy in weight_copies(be_ref[blk], slot):
            copy.wait()

        @pl.when(next_ref[blk] >= 0)
        def _():
            for copy in weight_copies(next_ref[blk], 1 - slot):
                copy.start()

        wg_bf[...] = wg_f32[slot].astype(BF16)
        wu_bf[...] = wu_f32[slot].astype(BF16)
        wd_bf[...] = wd_f32[slot].astype(BF16)

    def gated_mlp(rows):
        live = lax.broadcasted_iota(I32, (rows, x_ref.shape[1]), 0) < valid
        xb = _unpack_bf16_pairs(jnp.where(live, x_ref[0:rows, :], jnp.uint32(0))).astype(BF16)
        gt = _dot(xb, wg_bf[...])
        up = _dot(xb, wu_bf[...])
        act = (gt * _sigmoid(gt) * up).astype(BF16)
        y_ref[0:rows, :] = _pack_bf16_pairs(_dot(act, wd_bf[...]))
        if rows < y_ref.shape[0]:
            y_ref[rows:, :] = jnp.zeros((y_ref.shape[0] - rows, y_ref.shape[1]), U32)

    half = x_ref.shape[0] // 2

    @pl.when(valid > half)
    def _():
        gated_mlp(x_ref.shape[0])

    @pl.when((valid > 0) & (valid <= half))
    def _():
        gated_mlp(half)

    @pl.when(valid <= 0)
    def _():
        y_ref[...] = jnp.zeros(y_ref.shape, U32)


def _experts(x_buf, block_expert, block_first, block_valid, w_gate, w_up, w_down, layer):
    n_rows, w = x_buf.shape
    d, de = w_gate.shape[2], w_gate.shape[3]
    bm = MOE_BLOCK_ROWS
    n = n_rows // bm
    block_slot = (jnp.cumsum(block_first) - 1) % 2
    idx = jnp.arange(n, dtype=I32)
    later_first = jnp.concatenate([jnp.where(block_first[1:] == 1, idx[1:], n), jnp.full((1,), n, I32)])
    next_first = lax.cummin(later_first, reverse=True)
    block_next = jnp.where(next_first < n, block_expert[jnp.minimum(next_first, n - 1)], -1).astype(I32)
    any_space = pl.BlockSpec(memory_space=pl.ANY)
    grid_spec = pltpu.PrefetchScalarGridSpec(
        num_scalar_prefetch=5,
        grid=(n,),
        in_specs=[pl.BlockSpec((bm, w), lambda i, *_: (i, 0)), any_space, any_space, any_space],
        out_specs=pl.BlockSpec((bm, w), lambda i, *_: (i, 0)),
        scratch_shapes=[pltpu.VMEM((2, d, de), F32), pltpu.VMEM((2, d, de), F32), pltpu.VMEM((2, de, d), F32),
                        pltpu.VMEM((d, de), BF16), pltpu.VMEM((d, de), BF16), pltpu.VMEM((de, d), BF16),
                        pltpu.SemaphoreType.DMA((2, 3))],
    )
    return pl.pallas_call(
        functools.partial(_expert_kernel, layer=layer),
        out_shape=jax.ShapeDtypeStruct((n_rows, w), U32),
        grid_spec=grid_spec,
        compiler_params=_cparams(("arbitrary",)),
        name="moe_experts",
    )(block_expert, block_first, block_valid, block_slot.astype(I32), block_next, x_buf, w_gate, w_up, w_down)


def _combine_kernel(x_ref, gate_ref, y0_ref, y1_ref, gfin_ref, *rest, final_norm):
    o_ref = rest[-1]
    out = _moe_residual(x_ref[...], gate_ref[...], y0_ref[...], y1_ref[...])
    if final_norm:
        out = _rms(out, gfin_ref[...])
    o_ref[...] = out


def _combine(x2, y0, y1, gates, g_final, final_norm, first_row=0, out_so_far=None):
    t, d = x2.shape
    rows, w = y0.shape
    ts = min(TILE_COMBINE, rows)
    off = first_row // ts
    in_specs = [
        pl.BlockSpec((ts, d), lambda i: (i + off, 0)),
        pl.BlockSpec((8, ts), lambda i: (0, i + off)),
        pl.BlockSpec((ts, w), lambda i: (i, 0)),
        pl.BlockSpec((ts, w), lambda i: (i, 0)),
        pl.BlockSpec((1, d), lambda i: (0, 0)),
    ]
    args = [x2, gates, y0, y1, g_final.reshape(1, d)]
    aliases = {}
    if out_so_far is not None:
        in_specs.append(pl.BlockSpec(memory_space=pl.ANY))
        args.append(out_so_far)
        aliases = {len(args) - 1: 0}
    return pl.pallas_call(
        functools.partial(_combine_kernel, final_norm=final_norm),
        out_shape=jax.ShapeDtypeStruct((t, d), F32),
        grid=(rows // ts,),
        in_specs=in_specs,
        out_specs=pl.BlockSpec((ts, d), lambda i: (i + off, 0)),
        input_output_aliases=aliases,
        compiler_params=_cparams(("arbitrary",)),
        name="moe_combine",
    )(*args)


def _moe_layout(route, counts):
    bm = MOE_BLOCK_ROWS
    t = route.shape[1]
    n_blocks = (t * TOP_K) // bm + N_EXPERTS
    cnt = counts[:, 0].astype(I32)
    padded = (cnt + bm - 1) // bm * bm
    pad_ends = jnp.cumsum(padded)
    pad_off = pad_ends - padded
    experts = jnp.arange(N_EXPERTS, dtype=I32)
    hit = route[0:TOP_K, :, None] == experts
    dest = jnp.sum(jnp.where(hit, pad_off, 0), axis=-1) + route[TOP_K:2 * TOP_K]
    gates = route
    starts = jnp.arange(n_blocks, dtype=I32) * bm
    block_expert = jnp.minimum(jnp.sum((pad_ends[None, :] <= starts[:, None]).astype(I32), axis=1),
                               N_EXPERTS - 1)
    block_first = jnp.concatenate([jnp.ones((1,), I32), (block_expert[1:] != block_expert[:-1]).astype(I32)])
    own = block_expert[:, None] == experts
    block_valid = jnp.clip(jnp.sum(jnp.where(own, cnt + pad_off, 0), axis=1) - starts, 0, bm)
    block_valid = jnp.where(starts < pad_ends[-1], block_valid, 0).astype(I32)
    return dest, gates, block_expert, block_first, block_valid, n_blocks * bm


def kernel(x, mem, norm_mix, norm_xattn, norm_ffn, norm_mem, norm_final, conv_w_in, conv_b_in, conv_w_dw,
           conv_b_dw, conv_ln_g, conv_ln_b, conv_w_out, conv_b_out, gla_w_in, gla_w_a2, gla_b_a, gla_norm_g,
           gla_w_o, xa_w_q, xa_w_kv, xa_w_o, moe_w_grp, moe_b_grp, moe_w_exp, moe_b_exp, moe_w_gate, moe_w_up,
           moe_w_down):
    b, s, d = x.shape
    depth = norm_mix.shape[0]
    k_mem, v_mem = _mem_kv(mem, norm_mem, xa_w_kv)
    moe = None
    for i in range(depth):
        j = i // 2
        if i % 2 == 0:
            if moe is not None:
                x = _combine(x.reshape(b * s, d), moe[1], moe[2], moe[0], norm_final, False).reshape(b, s, d)
            x = _conv_mixer(x, norm_mix[i], conv_w_in[j], conv_b_in[j], conv_w_dw[j], conv_b_dw[j],
                            conv_ln_g[j], conv_ln_b[j], conv_w_out[j], conv_b_out[j])
        else:
            x = _gla_mixer(x, *moe, norm_mix[i], gla_w_in[j], gla_w_a2[j], gla_b_a[j], gla_norm_g[j], gla_w_o[j])
        x2, h_packed, route, counts = _xattn_router(
            x, norm_xattn[i], xa_w_q[i], k_mem, v_mem, i, xa_w_o[i], norm_ffn[i],
            moe_w_grp[i], moe_b_grp[i], moe_w_exp[i], moe_b_exp[i])
        dest, gates, block_expert, block_first, block_valid, n_rows = _moe_layout(route, counts)
        x_buf = _dispatch(h_packed, dest, n_rows)
        y_buf = _experts(x_buf, block_expert, block_first, block_valid, moe_w_gate, moe_w_up, moe_w_down, i)
        if i < depth - 1:
            y0, y1 = _gather_pairs(y_buf, dest)
            x, moe = x2, (gates, y0, y1)
    part = (b * s) // FINAL_PARTS
    out = None
    for p in range(FINAL_PARTS):
        y0, y1 = _gather_pairs(y_buf, dest[:, p * part:(p + 1) * part])
        out = _combine(x2.reshape(b * s, d), y0, y1, gates, norm_final, True, p * part, out)
    return out.reshape(b, s, d)
```

```python
import functools

import jax
import jax.numpy as jnp
from jax import lax
from jax.experimental import pallas as pl
from jax.experimental.pallas import tpu as pltpu
from jax.experimental.pallas import tpu_sc as plsc

F32 = jnp.float32
BF16 = jnp.bfloat16
I32 = jnp.int32
U32 = jnp.uint32

EPS = 1e-6
CONV_KERNEL = 31
CONV_CARRY = 32
CONV_ROWS = 64
CONV_COLS = 256
SUBLANES = 8
LANES = 128
GLA_HEADS = 4
GLA_RANK = 16
GLA_RANK_PAD = 128
GLA_TAU = 16.0
GLA_LEAF = 32
XATTN_HEADS = 4
N_GROUPS = 4
EXPERTS_PER_GROUP = 8
N_EXPERTS = N_GROUPS * EXPERTS_PER_GROUP
ROUTER_ROWS = 40
TOP_K = 2

TILE_CONV = 512
TILE_GLA = 256
TILE_XATTN = 1024
TILE_COMBINE = 512
FINAL_PARTS = 2
SC_CORES = 2
SC_SUBCORES = 16
SC_WORKERS = SC_CORES * SC_SUBCORES
SC_DISPATCH_CHUNK = 64
SC_GATHER_CHUNK = 32
MOE_BLOCK_ROWS = 512
VMEM_LIMIT = 56 * 1024 * 1024


def _cparams(sem):
    return pltpu.CompilerParams(dimension_semantics=sem, vmem_limit_bytes=VMEM_LIMIT)


def _rms(x, g):
    return x * lax.rsqrt(jnp.mean(x * x, axis=-1, keepdims=True) + EPS) * g


def _sigmoid(x):
    return 0.5 * jnp.tanh(0.5 * x) + 0.5


def _split_bf16(x):
    hi = x.astype(BF16)
    lo = (x - hi.astype(F32)).astype(BF16)
    return hi, lo


def _dot(a, b):
    return jnp.dot(a, b, preferred_element_type=F32)


def _dot_nt(a, b):
    return lax.dot_general(a, b, (((1,), (1,)), ((), ())), preferred_element_type=F32)


def _dot_tn(a, b):
    return lax.dot_general(a, b, (((0,), (0,)), ((), ())), preferred_element_type=F32)


def _pack_bf16_pairs(x):
    w = x.shape[1] // 2
    hi = lax.bitcast_convert_type(x[:, :w].astype(BF16).astype(F32), U32)
    lo = lax.bitcast_convert_type(x[:, w:].astype(BF16).astype(F32), U32)
    return hi | (lo >> 16)


def _unpack_bf16_pairs(p):
    hi = lax.bitcast_convert_type(p & jnp.uint32(0xFFFF0000), F32)
    lo = lax.bitcast_convert_type(p << 16, F32)
    return jnp.concatenate([hi, lo], axis=1)


def _memkv_kernel(mem_ref, g_ref, w_ref, k_ref, v_ref):
    d = mem_ref.shape[-1]
    mn = _rms(mem_ref[0], g_ref[...]).astype(BF16)
    kv = _dot(mn, w_ref[0])
    k_ref[0, 0] = kv[:, :d].astype(BF16)
    v_ref[0, 0] = kv[:, d:].astype(BF16)


def _mem_kv(mem, norm_mem, w_kv):
    b, nm, d = mem.shape
    depth = w_kv.shape[0]
    out = jax.ShapeDtypeStruct((depth, b, nm, d), BF16)
    return pl.pallas_call(
        _memkv_kernel,
        out_shape=(out, out),
        grid=(depth, b),
        in_specs=[
            pl.BlockSpec((1, nm, d), lambda l, i: (i, 0, 0)),
            pl.BlockSpec((1, d), lambda l, i: (0, 0)),
            pl.BlockSpec((1, d, 2 * d), lambda l, i: (l, 0, 0)),
        ],
        out_specs=(
            pl.BlockSpec((1, 1, nm, d), lambda l, i: (l, i, 0, 0)),
            pl.BlockSpec((1, 1, nm, d), lambda l, i: (l, i, 0, 0)),
        ),
        compiler_params=_cparams(("arbitrary", "arbitrary")),
        name="mem_kv",
    )(mem, norm_mem.reshape(1, d), w_kv.astype(BF16))


def _conv_kernel(x_ref, g_ref, win_ref, bin_ref, wdw_ref, bdw_ref, lng_ref, lnb_ref, wout_ref, bout_ref,
                 o_ref, ext_ref, conv_ref):
    ts, d = x_ref.shape[1], x_ref.shape[2]

    @pl.when(pl.program_id(1) == 0)
    def _():
        ext_ref[...] = jnp.zeros(ext_ref.shape, F32)

    x = x_ref[0]
    h = _rms(x, g_ref[...]).astype(BF16)
    u = _dot(h, win_ref[...]) + bin_ref[...]
    glu = u[:, :d] * _sigmoid(u[:, d:])
    for b in range(SUBLANES):
        ext_ref[b, CONV_CARRY - b:CONV_CARRY - b + ts, :] = glu

    first = CONV_CARRY - (CONV_KERNEL - 1)

    def chunk(i, carry):
        r0 = pl.multiple_of(i * CONV_ROWS, CONV_ROWS)
        for c0 in range(0, d, CONV_COLS):
            cols = slice(c0, c0 + CONV_COLS)
            acc = [jnp.zeros((SUBLANES, CONV_COLS), F32) for _ in range(CONV_ROWS // SUBLANES)]
            for k in range(CONV_KERNEL):
                b = (first + k) % SUBLANES
                wk = wdw_ref[k * SUBLANES:(k + 1) * SUBLANES, cols]
                for j in range(CONV_ROWS // SUBLANES):
                    rows = pl.ds(r0 + (first + k - b) + j * SUBLANES, SUBLANES)
                    acc[j] = acc[j] + ext_ref[b, rows, cols] * wk
            for j in range(CONV_ROWS // SUBLANES):
                conv_ref[pl.ds(r0 + j * SUBLANES, SUBLANES), cols] = acc[j]
        return carry

    lax.fori_loop(0, ts // CONV_ROWS, chunk, 0)
    for b in range(SUBLANES):
        ext_ref[b, 0:CONV_CARRY, :] = ext_ref[b, ts:ts + CONV_CARRY, :]

    c = conv_ref[...] + bdw_ref[...]
    mu = jnp.mean(c, axis=-1, keepdims=True)
    cc = c - mu
    var = jnp.mean(cc * cc, axis=-1, keepdims=True)
    un = cc * lax.rsqrt(var + EPS) * lng_ref[...] + lnb_ref[...]
    act = (un * _sigmoid(un)).astype(BF16)
    o_ref[0] = x + _dot(act, wout_ref[...]) + bout_ref[...]


def _conv_mixer(x, g, w_in, b_in, w_dw, b_dw, ln_g, ln_b, w_out, b_out):
    b, s, d = x.shape
    ts = min(TILE_CONV, s)
    row = lambda v: v.reshape(1, -1)
    const = lambda shape: pl.BlockSpec(shape, lambda i, j: (0,) * len(shape))
    return pl.pallas_call(
        _conv_kernel,
        out_shape=jax.ShapeDtypeStruct(x.shape, F32),
        grid=(b, s // ts),
        in_specs=[
            pl.BlockSpec((1, ts, d), lambda i, j: (i, j, 0)),
            const((1, d)), const((d, 2 * d)), const((1, 2 * d)), const((CONV_KERNEL * SUBLANES, d)), const((1, d)),
            const((1, d)), const((1, d)), const((d, d)), const((1, d)),
        ],
        out_specs=pl.BlockSpec((1, ts, d), lambda i, j: (i, j, 0)),
        scratch_shapes=[pltpu.VMEM((SUBLANES, CONV_CARRY + ts, d), F32), pltpu.VMEM((ts, d), F32)],
        compiler_params=_cparams(("arbitrary", "arbitrary")),
        name="conv_mixer",
    )(x, row(g), w_in.astype(BF16), row(b_in), jnp.repeat(w_dw, SUBLANES, axis=0), row(b_dw), row(ln_g), row(ln_b),
      w_out.astype(BF16), row(b_out))


def _gla_levels(ts):
    sizes = [GLA_LEAF]
    while sizes[-1] < ts:
        sizes.append(sizes[-1] * 2)
    return sizes


def _moe_residual(x, route, y0, y1):
    g = lax.bitcast_convert_type(route, F32)
    g = jnp.concatenate([g, jnp.zeros((LANES - g.shape[0], g.shape[1]), F32)], axis=0).T
    gate0 = g[:, 2 * TOP_K:2 * TOP_K + 1]
    gate1 = g[:, 2 * TOP_K + 1:2 * TOP_K + 2]
    return x + _unpack_bf16_pairs(y0) * gate0 + _unpack_bf16_pairs(y1) * gate1


def _gla_kernel(x_ref, gate_ref, y0_ref, y1_ref, g_ref, wq_ref, wk_ref, wv_ref, wa_ref, wr_ref, wa2_ref, ba_ref,
                ng_ref, wo_ref, o_ref, state_ref, x_s, q_s, k_s, v_s, og_s, la_s, *, tiles_per_seq):
    j = pl.program_id(0)
    dkh = wq_ref.shape[1] // GLA_HEADS

    @pl.when(j == 0)
    def _():
        for ref in (x_s, q_s, k_s, og_s, la_s):
            ref[...] = jnp.zeros(ref.shape, F32)
        v_s[...] = jnp.zeros(v_s.shape, BF16)

    @pl.when((j == 0) | ((j - 1) % tiles_per_seq == 0))
    def _():
        state_ref[...] = jnp.zeros(state_ref.shape, F32)

    def stage_a(slot):
        x = _moe_residual(x_ref[0], gate_ref[...], y0_ref[...], y1_ref[...])
        h = _rms(x, g_ref[...]).astype(BF16)
        x_s[slot] = x
        q_s[slot] = _dot(h, wq_ref[...]) * (dkh ** -0.5)
        k_s[slot] = _dot(h, wk_ref[...])
        v_s[slot] = _dot(h, wv_ref[...]).astype(BF16)
        r = _dot(h, wr_ref[...])
        og_s[slot] = r * _sigmoid(r)
        a = _dot(h, wa_ref[...]).astype(BF16)
        z = _dot(jnp.concatenate([a, a], axis=1), wa2_ref[...]) + ba_ref[...]
        la_s[slot] = -(jnp.maximum(-z, 0.0) + jnp.log(1.0 + jnp.exp(-jnp.abs(z)))) * (1.0 / GLA_TAU)

    def stage_b(slot):
        _gla_recurrence(x_s[slot], q_s[slot], k_s[slot], v_s[slot], og_s[slot], la_s[slot],
                        ng_ref, wo_ref, o_ref, state_ref)

    @pl.when(j % 2 == 0)
    def _():
        stage_b(1)
        stage_a(0)

    @pl.when(j % 2 == 1)
    def _():
        stage_b(0)
        stage_a(1)


def _gla_recurrence(x, q, k, v, out_gate, log_a, ng_ref, wo_ref, o_ref, state_ref):
    ts, d = x.shape
    dk = q.shape[1]
    dkh = dk // GLA_HEADS
    dvh = d // GLA_HEADS

    row = lax.broadcasted_iota(I32, (ts, ts), 0)
    col = lax.broadcasted_iota(I32, (ts, ts), 1)
    tri = jnp.where(col <= row, 1.0, 0.0).astype(BF16)
    la_hi, la_lo = _split_bf16(log_a)
    bcum = _dot(tri, la_hi) + _dot(tri, la_lo)
    b_last = bcum[ts - 1:ts, :]

    q_in = (q * jnp.exp(bcum)).astype(BF16)
    k_out = (k * jnp.exp(b_last - bcum)).astype(BF16)

    sizes = _gla_levels(ts)
    scores = [None] * GLA_HEADS
    for lvl, size in enumerate(sizes):
        half = size // 2
        same_block = (row & -size) == (col & -size)
        if lvl == 0:
            pair = same_block & (col <= row)
            q_ok = k_ok = None
        else:
            pair = same_block & ((row & (size - 1)) >= half) & ((col & (size - 1)) < half)
            pos = lax.broadcasted_iota(I32, (ts, dk), 0) & (size - 1)
            q_ok = pos >= half
            k_ok = pos < half
        ref = jnp.concatenate(
            [jnp.broadcast_to(bcum[r0 + half:r0 + half + 1, :], (size, dk)) for r0 in range(0, ts, size)], axis=0)
        ql = q * jnp.exp(bcum - ref)
        kl = k * jnp.exp(ref - bcum)
        if q_ok is not None:
            ql = jnp.where(q_ok, ql, 0.0)
            kl = jnp.where(k_ok, kl, 0.0)
        ql = ql.astype(BF16)
        kl = kl.astype(BF16)
        for hd in range(GLA_HEADS):
            c0 = hd * dkh
            a = _dot_nt(ql[:, c0:c0 + dkh], kl[:, c0:c0 + dkh])
            a = jnp.where(pair, a, 0.0)
            scores[hd] = a if scores[hd] is None else scores[hd] + a

    outs = []
    for hd in range(GLA_HEADS):
        c0 = hd * dkh
        v_h = v[:, hd * dvh:(hd + 1) * dvh]
        st = state_ref[hd]
        o_h = _dot(scores[hd].astype(BF16), v_h) + _dot_nt(q_in[:, c0:c0 + dkh], st.astype(BF16))
        decay = jnp.exp(b_last[:, c0:c0 + dkh])
        state_ref[hd] = st * decay + _dot_tn(v_h, k_out[:, c0:c0 + dkh])
        o_h = o_h * lax.rsqrt(jnp.mean(o_h * o_h, axis=-1, keepdims=True) + EPS) * ng_ref[...]
        outs.append(o_h)
    o = jnp.concatenate(outs, axis=1) * out_gate
    o_ref[0] = x + _dot(o.astype(BF16), wo_ref[...])


def _gla_mixer(x, gates, y0, y1, g, w_in, w_a2, b_a, norm_g, w_o):
    b, s, d = x.shape
    dk = w_a2.shape[1]
    ts = min(TILE_GLA, s)
    nj = s // ts
    row = lambda v: v.reshape(1, -1)
    const = lambda shape: pl.BlockSpec(shape, lambda j: (0,) * len(shape))
    wq = w_in[:, :dk].astype(BF16)
    wk = w_in[:, dk:2 * dk].astype(BF16)
    wv = w_in[:, 2 * dk:2 * dk + d].astype(BF16)
    wa = jnp.pad(w_in[:, 2 * dk + d:2 * dk + d + GLA_RANK], ((0, 0), (0, GLA_RANK_PAD - GLA_RANK))).astype(BF16)
    wr = w_in[:, 2 * dk + d + GLA_RANK:].astype(BF16)
    wa2 = jnp.pad(w_a2, ((0, GLA_RANK_PAD - GLA_RANK), (0, 0)))
    wa2_hi = wa2.astype(BF16)
    wa2_lo = (wa2 - wa2_hi.astype(F32)).astype(BF16)
    wa2_split = jnp.concatenate([wa2_hi, wa2_lo], axis=0)
    dvh = d // GLA_HEADS
    n = b * nj

    def tile(j, lag):
        return jnp.clip(j - lag, 0, n - 1)

    tokens = lambda lag: (lambda j: (tile(j, lag), 0))
    return pl.pallas_call(
        functools.partial(_gla_kernel, tiles_per_seq=nj),
        out_shape=jax.ShapeDtypeStruct(x.shape, F32),
        grid=(n + 1,),
        in_specs=[
            pl.BlockSpec((1, ts, d), lambda j: (tile(j, 0) // nj, tile(j, 0) % nj, 0)),
            pl.BlockSpec((8, ts), lambda j: (0, tile(j, 0))),
            pl.BlockSpec((ts, d // 2), tokens(0)),
            pl.BlockSpec((ts, d // 2), tokens(0)),
            const((1, d)), const((d, dk)), const((d, dk)), const((d, d)), const((d, GLA_RANK_PAD)),
            const((d, d)), const((2 * GLA_RANK_PAD, dk)), const((1, dk)), const((1, dvh)), const((d, d)),
        ],
        out_specs=pl.BlockSpec((1, ts, d), lambda j: (tile(j, 1) // nj, tile(j, 1) % nj, 0)),
        scratch_shapes=[pltpu.VMEM((GLA_HEADS, dvh, dk // GLA_HEADS), F32),
                        pltpu.VMEM((2, ts, d), F32), pltpu.VMEM((2, ts, dk), F32), pltpu.VMEM((2, ts, dk), F32),
                        pltpu.VMEM((2, ts, d), BF16), pltpu.VMEM((2, ts, d), F32), pltpu.VMEM((2, ts, dk), F32)],
        compiler_params=_cparams(("arbitrary",)),
        name="gla_mixer",
    )(x, gates, y0, y1, row(g), wq, wk, wv, wa, wr, wa2_split, row(b_a), row(norm_g), w_o.astype(BF16))


def _xattn_router_kernel(x_ref, gx_ref, wq_ref, k_ref, v_ref, wo_ref, gf_ref, wr_ref, br_ref, upper_ref,
                         x_out_ref, h_out_ref, route_ref, cnt_ref, carry_ref):
    ts, d = x_ref.shape[1], x_ref.shape[2]
    hd = d // XATTN_HEADS
    first = (pl.program_id(0) == 0) & (pl.program_id(1) == 0)

    @pl.when(first)
    def _():
        carry_ref[...] = jnp.zeros(carry_ref.shape, F32)

    x = x_ref[0]
    h = _rms(x, gx_ref[...]).astype(BF16)
    q = (_dot(h, wq_ref[...]) * (hd ** -0.5)).astype(BF16)
    k = k_ref[0, 0]
    v = v_ref[0, 0]
    outs = []
    for a in range(XATTN_HEADS):
        sl = slice(a * hd, (a + 1) * hd)
        s = _dot_nt(q[:, sl], k[:, sl])
        p = jnp.exp(s - jnp.max(s, axis=-1, keepdims=True))
        o = _dot(p.astype(BF16), v[:, sl]) / jnp.sum(p, axis=-1, keepdims=True)
        outs.append(o)
    att = jnp.concatenate(outs, axis=1).astype(BF16)
    x2 = x + _dot(att, wo_ref[...])
    x_out_ref[0] = x2

    hf = _rms(x2, gf_ref[...])
    h_out_ref[...] = _pack_bf16_pairs(hf)

    both = _dot_nt(wr_ref[...], hf.astype(BF16))
    logits = both[0:ROUTER_ROWS, :] + both[ROUTER_ROWS:2 * ROUTER_ROWS, :] + br_ref[...]
    gl = logits[N_EXPERTS:N_EXPERTS + N_GROUPS, :]
    gi = lax.broadcasted_iota(I32, gl.shape, 0).astype(F32)
    gmax = jnp.max(gl, axis=0, keepdims=True)
    g_sel = jnp.min(jnp.where(gl == gmax, gi, float(N_GROUPS)), axis=0, keepdims=True)
    pg_sel = 1.0 / jnp.sum(jnp.exp(gl - gmax), axis=0, keepdims=True)

    el = jnp.zeros((EXPERTS_PER_GROUP, ts), F32)
    for gidx in range(N_GROUPS):
        lo = gidx * EXPERTS_PER_GROUP
        el = jnp.where(g_sel == float(gidx), logits[lo:lo + EXPERTS_PER_GROUP, :], el)
    ei = lax.broadcasted_iota(I32, el.shape, 0).astype(F32)
    m1 = jnp.max(el, axis=0, keepdims=True)
    i1 = jnp.min(jnp.where(el == m1, ei, float(EXPERTS_PER_GROUP)), axis=0, keepdims=True)
    rest = jnp.where(ei == i1, -jnp.inf, el)
    m2 = jnp.max(rest, axis=0, keepdims=True)
    i2 = jnp.min(jnp.where(rest == m2, ei, float(EXPERTS_PER_GROUP)), axis=0, keepdims=True)
    ratio = jnp.exp(m2 - m1)
    gate1 = pg_sel / (1.0 + ratio)
    gate2 = pg_sel * ratio / (1.0 + ratio)
    e1 = g_sel * float(EXPERTS_PER_GROUP) + i1
    e2 = g_sel * float(EXPERTS_PER_GROUP) + i2

    xi = lax.broadcasted_iota(I32, (N_EXPERTS, ts), 0).astype(F32)
    oh1 = jnp.where(xi == e1, 1.0, 0.0)
    oh2 = jnp.where(xi == e2, 1.0, 0.0)
    oh = oh1 + oh2
    n_blk = ts // LANES
    stacked = jnp.concatenate([oh[:, c * LANES:(c + 1) * LANES] for c in range(n_blk)], axis=0)
    within = _dot(stacked.astype(BF16), upper_ref[...])
    totals = jnp.sum(stacked, axis=1, keepdims=True)
    run = carry_ref[...]
    before = []
    for c in range(n_blk):
        before.append(within[c * N_EXPERTS:(c + 1) * N_EXPERTS, :] + run)
        run = run + totals[c * N_EXPERTS:(c + 1) * N_EXPERTS, :]
    before = jnp.concatenate(before, axis=1)
    rank1 = jnp.sum(oh1 * before, axis=0, keepdims=True)
    rank2 = jnp.sum(oh2 * before, axis=0, keepdims=True)
    carry_ref[...] = run
    cnt_ref[...] = jnp.broadcast_to(run, cnt_ref.shape)

    route_ref[0:1, :] = e1.astype(I32)
    route_ref[1:2, :] = e2.astype(I32)
    route_ref[2:3, :] = rank1.astype(I32)
    route_ref[3:4, :] = rank2.astype(I32)
    route_ref[4:5, :] = lax.bitcast_convert_type(gate1, I32)
    route_ref[5:6, :] = lax.bitcast_convert_type(gate2, I32)
    route_ref[6:8, :] = jnp.zeros((2, ts), I32)


def _xattn_router(x, g_x, w_q, k_mem, v_mem, layer, w_o, g_f, w_grp, b_grp, w_exp, b_exp):
    b, s, d = x.shape
    nm = k_mem.shape[2]
    t = b * s
    ts = min(TILE_XATTN, s)
    nj = s // ts
    row = lambda v: v.reshape(1, -1)
    const = lambda shape: pl.BlockSpec(shape, lambda i, j: (0,) * len(shape))
    pad = ROUTER_ROWS - N_GROUPS - N_EXPERTS
    w_r = jnp.pad(jnp.concatenate([w_exp, w_grp], axis=1).T, ((0, pad), (0, 0)))
    w_r_hi = w_r.astype(BF16)
    w_r_split = jnp.concatenate([w_r_hi, (w_r - w_r_hi.astype(F32)).astype(BF16)], axis=0)
    b_r = jnp.pad(jnp.concatenate([b_exp, b_grp]), (0, pad)).reshape(ROUTER_ROWS, 1)
    ti = jnp.arange(LANES)
    upper = (ti[:, None] < ti[None, :]).astype(BF16)
    return pl.pallas_call(
        _xattn_router_kernel,
        out_shape=(
            jax.ShapeDtypeStruct(x.shape, F32),
            jax.ShapeDtypeStruct((t, d // 2), U32),
            jax.ShapeDtypeStruct((8, t), I32),
            jax.ShapeDtypeStruct((N_EXPERTS, 128), F32),
        ),
        grid=(b, nj),
        in_specs=[
            pl.BlockSpec((1, ts, d), lambda i, j: (i, j, 0)),
            const((1, d)), const((d, d)),
            pl.BlockSpec((1, 1, nm, d), lambda i, j: (layer, i, 0, 0)),
            pl.BlockSpec((1, 1, nm, d), lambda i, j: (layer, i, 0, 0)),
            const((d, d)), const((1, d)), const((2 * ROUTER_ROWS, d)), const((ROUTER_ROWS, 1)), const((LANES, LANES)),
        ],
        out_specs=(
            pl.BlockSpec((1, ts, d), lambda i, j: (i, j, 0)),
            pl.BlockSpec((ts, d // 2), lambda i, j: (i * nj + j, 0)),
            pl.BlockSpec((8, ts), lambda i, j: (0, i * nj + j)),
            pl.BlockSpec((N_EXPERTS, 128), lambda i, j: (0, 0)),
        ),
        scratch_shapes=[pltpu.VMEM((N_EXPERTS, 1), F32)],
        compiler_params=_cparams(("arbitrary", "arbitrary")),
        name="xattn_router",
    )(x, row(g_x), w_q.astype(BF16), k_mem, v_mem, w_o.astype(BF16), row(g_f), w_r_split, b_r, upper)


def _sc_mesh():
    return plsc.VectorSubcoreMesh(core_axis_name="c", subcore_axis_name="s",
                                  num_cores=SC_CORES, num_subcores=SC_SUBCORES)


def _sc_worker():
    return lax.axis_index("s") * SC_CORES + lax.axis_index("c")


def _dispatch(h_packed, dest, n_rows):
    t, w = h_packed.shape
    chunk = SC_DISPATCH_CHUNK
    per_worker = t // SC_WORKERS
    n_chunks = per_worker // chunk
    assert n_chunks % 2 == 0 and n_chunks * chunk * SC_WORKERS == t
    dest = dest.reshape(TOP_K, SC_WORKERS * n_chunks, chunk)
    rows_buf = pltpu.VMEM((chunk, w), U32)

    @functools.partial(
        pl.kernel, mesh=_sc_mesh(),
        out_type=jax.ShapeDtypeStruct((n_rows, w), U32),
        scratch_types=[pltpu.VMEM((n_chunks, chunk), I32), pltpu.VMEM((n_chunks, chunk), I32), rows_buf, rows_buf,
                       pltpu.SemaphoreType.DMA((2,)), pltpu.SemaphoreType.DMA((2, TOP_K))],
        name="moe_dispatch_sc",
    )
    def run(h_hbm, d0_hbm, d1_hbm, xbuf_hbm, idx0_v, idx1_v, buf_a, buf_b, read_sem, write_sem):
        wid = _sc_worker()
        pltpu.sync_copy(d0_hbm.at[pl.ds(wid * n_chunks, n_chunks)], idx0_v)
        pltpu.sync_copy(d1_hbm.at[pl.ds(wid * n_chunks, n_chunks)], idx1_v)

        @pl.loop(0, n_chunks, step=2)
        def _(i):
            reads = [pltpu.async_copy(h_hbm.at[pl.ds(wid * per_worker + (i + j) * chunk, chunk)], buf, read_sem.at[j])
                     for j, buf in enumerate((buf_a, buf_b))]
            writes = []
            for j, buf in enumerate((buf_a, buf_b)):
                reads[j].wait()
                writes.append(pltpu.async_copy(buf, xbuf_hbm.at[idx0_v.at[i + j]], write_sem.at[j, 0]))
                writes.append(pltpu.async_copy(buf, xbuf_hbm.at[idx1_v.at[i + j]], write_sem.at[j, 1]))
            for copy in writes:
                copy.wait()

    return run(h_packed, dest[0], dest[1])


def _gather_pairs(y_buf, dest):
    t = dest.shape[1]
    w = y_buf.shape[1]
    chunk = SC_GATHER_CHUNK
    per_worker = t // SC_WORKERS
    n_chunks = per_worker // chunk
    assert n_chunks % 2 == 0 and n_chunks * chunk * SC_WORKERS == t
    dest = dest.reshape(TOP_K, SC_WORKERS * n_chunks, chunk)
    out = jax.ShapeDtypeStruct((t, w), U32)
    rows_buf = pltpu.VMEM((chunk, w), U32)

    @functools.partial(
        pl.kernel, mesh=_sc_mesh(),
        out_type=(out, out),
        scratch_types=[pltpu.VMEM((n_chunks, chunk), I32), pltpu.VMEM((n_chunks, chunk), I32),
                       rows_buf, rows_buf, rows_buf, rows_buf, pltpu.SemaphoreType.DMA((2, TOP_K))],
        name="moe_gather_sc",
    )
    def run(y_hbm, d0_hbm, d1_hbm, y0_hbm, y1_hbm, idx0_v, idx1_v, buf_a0, buf_a1, buf_b0, buf_b1, sem):
        wid = _sc_worker()
        pltpu.sync_copy(d0_hbm.at[pl.ds(wid * n_chunks, n_chunks)], idx0_v)
        pltpu.sync_copy(d1_hbm.at[pl.ds(wid * n_chunks, n_chunks)], idx1_v)

        @pl.loop(0, n_chunks, step=2)
        def _(i):
            bufs = ((buf_a0, buf_a1), (buf_b0, buf_b1))
            gathers = [[pltpu.async_copy(y_hbm.at[idx_v.at[i + j]], bufs[j][k], sem.at[j, k])
                        for k, idx_v in enumerate((idx0_v, idx1_v))] for j in range(2)]
            stores = []
            for j in range(2):
                rows = pl.ds(wid * per_worker + (i + j) * chunk, chunk)
                for k, out_hbm in enumerate((y0_hbm, y1_hbm)):
                    gathers[j][k].wait()
                    stores.append(pltpu.async_copy(bufs[j][k], out_hbm.at[rows], sem.at[j, k]))
            for copy in stores:
                copy.wait()

    return run(y_buf, dest[0], dest[1])


def _expert_kernel(be_ref, bf_ref, bv_ref, slot_ref, next_ref, x_ref, wg_hbm, wu_hbm, wd_hbm, y_ref,
                   wg_f32, wu_f32, wd_f32, wg_bf, wu_bf, wd_bf, sems, *, layer):
    blk = pl.program_id(0)
    valid = bv_ref[blk]

    def weight_copies(expert, slot):
        return [pltpu.make_async_copy(src.at[layer, expert], dst.at[slot], sems.at[slot, i])
                for i, (src, dst) in enumerate(((wg_hbm, wg_f32), (wu_hbm, wu_f32), (wd_hbm, wd_f32)))]

    @pl.when(blk == 0)
    def _():
        for copy in weight_copies(be_ref[0], 0):
            copy.start()

    @pl.when(bf_ref[blk] == 1)
    def _():
        slot = slot_ref[blk]
        for copy in weight_copies(be_ref[blk], slot):
            copy.wait()

        @pl.when(next_ref[blk] >= 0)
        def _():
            for copy in weight_copies(next_ref[blk], 1 - slot):
                copy.start()

        wg_bf[...] = wg_f32[slot].astype(BF16)
        wu_bf[...] = wu_f32[slot].astype(BF16)
        wd_bf[...] = wd_f32[slot].astype(BF16)

    @pl.when(valid > 0)
    def _():
        live = lax.broadcasted_iota(I32, x_ref.shape, 0) < valid
        xb = _unpack_bf16_pairs(jnp.where(live, x_ref[...], jnp.uint32(0))).astype(BF16)
        gt = _dot(xb, wg_bf[...])
        up = _dot(xb, wu_bf[...])
        act = (gt * _sigmoid(gt) * up).astype(BF16)
        y_ref[...] = _pack_bf16_pairs(_dot(act, wd_bf[...]))

    @pl.when(valid <= 0)
    def _():
        y_ref[...] = jnp.zeros(y_ref.shape, U32)


def _experts(x_buf, block_expert, block_first, block_valid, w_gate, w_up, w_down, layer):
    n_rows, w = x_buf.shape
    d, de = w_gate.shape[2], w_gate.shape[3]
    bm = MOE_BLOCK_ROWS
    n = n_rows // bm
    block_slot = (jnp.cumsum(block_first) - 1) % 2
    idx = jnp.arange(n, dtype=I32)
    later_first = jnp.concatenate([jnp.where(block_first[1:] == 1, idx[1:], n), jnp.full((1,), n, I32)])
    next_first = lax.cummin(later_first, reverse=True)
    block_next = jnp.where(next_first < n, block_expert[jnp.minimum(next_first, n - 1)], -1).astype(I32)
    any_space = pl.BlockSpec(memory_space=pl.ANY)
    grid_spec = pltpu.PrefetchScalarGridSpec(
        num_scalar_prefetch=5,
        grid=(n,),
        in_specs=[pl.BlockSpec((bm, w), lambda i, *_: (i, 0)), any_space, any_space, any_space],
        out_specs=pl.BlockSpec((bm, w), lambda i, *_: (i, 0)),
        scratch_shapes=[pltpu.VMEM((2, d, de), F32), pltpu.VMEM((2, d, de), F32), pltpu.VMEM((2, de, d), F32),
                        pltpu.VMEM((d, de), BF16), pltpu.VMEM((d, de), BF16), pltpu.VMEM((de, d), BF16),
                        pltpu.SemaphoreType.DMA((2, 3))],
    )
    return pl.pallas_call(
        functools.partial(_expert_kernel, layer=layer),
        out_shape=jax.ShapeDtypeStruct((n_rows, w), U32),
        grid_spec=grid_spec,
        compiler_params=_cparams(("arbitrary",)),
        name="moe_experts",
    )(block_expert, block_first, block_valid, block_slot.astype(I32), block_next, x_buf, w_gate, w_up, w_down)


def _combine_kernel(x_ref, gate_ref, y0_ref, y1_ref, gfin_ref, *rest, final_norm):
    o_ref = rest[-1]
    out = _moe_residual(x_ref[...], gate_ref[...], y0_ref[...], y1_ref[...])
    if final_norm:
        out = _rms(out, gfin_ref[...])
    o_ref[...] = out


def _combine(x2, y0, y1, gates, g_final, final_norm, first_row=0, out_so_far=None):
    t, d = x2.shape
    rows, w = y0.shape
    ts = min(TILE_COMBINE, rows)
    off = first_row // ts
    in_specs = [
        pl.BlockSpec((ts, d), lambda i: (i + off, 0)),
        pl.BlockSpec((8, ts), lambda i: (0, i + off)),
        pl.BlockSpec((ts, w), lambda i: (i, 0)),
        pl.BlockSpec((ts, w), lambda i: (i, 0)),
        pl.BlockSpec((1, d), lambda i: (0, 0)),
    ]
    args = [x2, gates, y0, y1, g_final.reshape(1, d)]
    aliases = {}
    if out_so_far is not None:
        in_specs.append(pl.BlockSpec(memory_space=pl.ANY))
        args.append(out_so_far)
        aliases = {len(args) - 1: 0}
    return pl.pallas_call(
        functools.partial(_combine_kernel, final_norm=final_norm),
        out_shape=jax.ShapeDtypeStruct((t, d), F32),
        grid=(rows // ts,),
        in_specs=in_specs,
        out_specs=pl.BlockSpec((ts, d), lambda i: (i + off, 0)),
        input_output_aliases=aliases,
        compiler_params=_cparams(("arbitrary",)),
        name="moe_combine",
    )(*args)


def _moe_layout(route, counts):
    bm = MOE_BLOCK_ROWS
    t = route.shape[1]
    n_blocks = (t * TOP_K) // bm + N_EXPERTS
    cnt = counts[:, 0].astype(I32)
    padded = (cnt + bm - 1) // bm * bm
    pad_ends = jnp.cumsum(padded)
    pad_off = pad_ends - padded
    experts = jnp.arange(N_EXPERTS, dtype=I32)
    hit = route[0:TOP_K, :, None] == experts
    dest = jnp.sum(jnp.where(hit, pad_off, 0), axis=-1) + route[TOP_K:2 * TOP_K]
    gates = route
    starts = jnp.arange(n_blocks, dtype=I32) * bm
    block_expert = jnp.minimum(jnp.sum((pad_ends[None, :] <= starts[:, None]).astype(I32), axis=1),
                               N_EXPERTS - 1)
    block_first = jnp.concatenate([jnp.ones((1,), I32), (block_expert[1:] != block_expert[:-1]).astype(I32)])
    own = block_expert[:, None] == experts
    block_valid = jnp.clip(jnp.sum(jnp.where(own, cnt + pad_off, 0), axis=1) - starts, 0, bm)
    block_valid = jnp.where(starts < pad_ends[-1], block_valid, 0).astype(I32)
    return dest, gates, block_expert, block_first, block_valid, n_blocks * bm


def kernel(x, mem, norm_mix, norm_xattn, norm_ffn, norm_mem, norm_final, conv_w_in, conv_b_in, conv_w_dw,
           conv_b_dw, conv_ln_g, conv_ln_b, conv_w_out, conv_b_out, gla_w_in, gla_w_a2, gla_b_a, gla_norm_g,
           gla_w_o, xa_w_q, xa_w_kv, xa_w_o, moe_w_grp, moe_b_grp, moe_w_exp, moe_b_exp, moe_w_gate, moe_w_up,
           moe_w_down):
    b, s, d = x.shape
    depth = norm_mix.shape[0]
    k_mem, v_mem = _mem_kv(mem, norm_mem, xa_w_kv)
    moe = None
    for i in range(depth):
        j = i // 2
        if i % 2 == 0:
            if moe is not None:
                x = _combine(x.reshape(b * s, d), moe[1], moe[2], moe[0], norm_final, False).reshape(b, s, d)
            x = _conv_mixer(x, norm_mix[i], conv_w_in[j], conv_b_in[j], conv_w_dw[j], conv_b_dw[j],
                            conv_ln_g[j], conv_ln_b[j], conv_w_out[j], conv_b_out[j])
        else:
            x = _gla_mixer(x, *moe, norm_mix[i], gla_w_in[j], gla_w_a2[j], gla_b_a[j], gla_norm_g[j], gla_w_o[j])
        x2, h_packed, route, counts = _xattn_router(
            x, norm_xattn[i], xa_w_q[i], k_mem, v_mem, i, xa_w_o[i], norm_ffn[i],
            moe_w_grp[i], moe_b_grp[i], moe_w_exp[i], moe_b_exp[i])
        dest, gates, block_expert, block_first, block_valid, n_rows = _moe_layout(route, counts)
        x_buf = _dispatch(h_packed, dest, n_rows)
        y_buf = _experts(x_buf, block_expert, block_first, block_valid, moe_w_gate, moe_w_up, moe_w_down, i)
        if i < depth - 1:
            y0, y1 = _gather_pairs(y_buf, dest)
            x, moe = x2, (gates, y0, y1)
    part = (b * s) // FINAL_PARTS
    out = None
    for p in range(FINAL_PARTS):
        y0, y1 = _gather_pairs(y_buf, dest[:, p * part:(p + 1) * part])
        out = _combine(x2.reshape(b * s, d), y0, y1, gates, norm_final, True, p * part, out)
    return out.reshape(b, s, d)
```

```python
import functools

import jax
import jax.numpy as jnp
from jax import lax
from jax.experimental import pallas as pl
from jax.experimental.pallas import tpu as pltpu
from jax.experimental.pallas import tpu_sc as plsc

F32 = jnp.float32
BF16 = jnp.bfloat16
I32 = jnp.int32
U32 = jnp.uint32

EPS = 1e-6
CONV_KERNEL = 31
CONV_CARRY = 32
CONV_ROWS = 64
CONV_COLS = 256
SUBLANES = 8
LANES = 128
GLA_HEADS = 4
GLA_RANK = 16
GLA_RANK_PAD = 128
GLA_TAU = 16.0
GLA_LEAF = 32
XATTN_HEADS = 4
N_GROUPS = 4
EXPERTS_PER_GROUP = 8
N_EXPERTS = N_GROUPS * EXPERTS_PER_GROUP
ROUTER_ROWS = 40
TOP_K = 2
ROUTE_ROWS = 8

TILE_CONV = 512
TILE_GLA = 256
TILE_XATTN = 1024
TILE_COMBINE = 512
FINAL_PARTS = 2
SC_CORES = 2
SC_SUBCORES = 16
SC_WORKERS = SC_CORES * SC_SUBCORES
SC_DISPATCH_CHUNK = 64
SC_GATHER_CHUNK = 32
MOE_BLOCK_ROWS = 512
VMEM_LIMIT = 56 * 1024 * 1024


def _cparams(sem):
    return pltpu.CompilerParams(dimension_semantics=sem, vmem_limit_bytes=VMEM_LIMIT)


def _rms(x, g):
    return x * lax.rsqrt(jnp.mean(x * x, axis=-1, keepdims=True) + EPS) * g


def _sigmoid(x):
    return 0.5 * jnp.tanh(0.5 * x) + 0.5


def _dot(a, b):
    return jnp.dot(a, b, preferred_element_type=F32)


def _dot_nt(a, b):
    return lax.dot_general(a, b, (((1,), (1,)), ((), ())), preferred_element_type=F32)


def _dot_tn(a, b):
    return lax.dot_general(a, b, (((0,), (0,)), ((), ())), preferred_element_type=F32)


def _pack_bf16_pairs(x):
    w = x.shape[1] // 2
    hi = lax.bitcast_convert_type(x[:, :w].astype(BF16).astype(F32), U32)
    lo = lax.bitcast_convert_type(x[:, w:].astype(BF16).astype(F32), U32)
    return hi | (lo >> 16)


def _unpack_bf16_pairs(p):
    hi = lax.bitcast_convert_type(p & jnp.uint32(0xFFFF0000), F32)
    lo = lax.bitcast_convert_type(p << 16, F32)
    return jnp.concatenate([hi, lo], axis=1)


def _memkv_kernel(mem_ref, g_ref, w_ref, k_ref, v_ref):
    d = mem_ref.shape[-1]
    mn = _rms(mem_ref[0], g_ref[...]).astype(BF16)
    kv = _dot(mn, w_ref[0])
    k_ref[0, 0] = kv[:, :d].astype(BF16)
    v_ref[0, 0] = kv[:, d:].astype(BF16)


def _mem_kv(mem, norm_mem, w_kv):
    b, nm, d = mem.shape
    depth = w_kv.shape[0]
    out = jax.ShapeDtypeStruct((depth, b, nm, d), BF16)
    return pl.pallas_call(
        _memkv_kernel,
        out_shape=(out, out),
        grid=(depth, b),
        in_specs=[
            pl.BlockSpec((1, nm, d), lambda l, i: (i, 0, 0)),
            pl.BlockSpec((1, d), lambda l, i: (0, 0)),
            pl.BlockSpec((1, d, 2 * d), lambda l, i: (l, 0, 0)),
        ],
        out_specs=(
            pl.BlockSpec((1, 1, nm, d), lambda l, i: (l, i, 0, 0)),
            pl.BlockSpec((1, 1, nm, d), lambda l, i: (l, i, 0, 0)),
        ),
        compiler_params=_cparams(("arbitrary", "arbitrary")),
        name="mem_kv",
    )(mem, norm_mem.reshape(1, d), w_kv.astype(BF16))


def _conv_kernel(x_ref, g_ref, win_ref, bin_ref, wdw_ref, bdw_ref, lng_ref, lnb_ref, wout_ref, bout_ref,
                 o_ref, ext_ref, conv_ref):
    ts, d = x_ref.shape[1], x_ref.shape[2]

    @pl.when(pl.program_id(1) == 0)
    def _():
        ext_ref[...] = jnp.zeros(ext_ref.shape, F32)

    x = x_ref[0]
    h = _rms(x, g_ref[...]).astype(BF16)
    u = _dot(h, win_ref[...]) + bin_ref[...]
    glu = u[:, :d] * _sigmoid(u[:, d:])
    for b in range(SUBLANES):
        ext_ref[b, CONV_CARRY - b:CONV_CARRY - b + ts, :] = glu

    first = CONV_CARRY - (CONV_KERNEL - 1)

    def chunk(i, carry):
        r0 = pl.multiple_of(i * CONV_ROWS, CONV_ROWS)
        for c0 in range(0, d, CONV_COLS):
            cols = slice(c0, c0 + CONV_COLS)
            acc = [jnp.zeros((SUBLANES, CONV_COLS), F32) for _ in range(CONV_ROWS // SUBLANES)]
            for k in range(CONV_KERNEL):
                b = (first + k) % SUBLANES
                wk = wdw_ref[k * SUBLANES:(k + 1) * SUBLANES, cols]
                for j in range(CONV_ROWS // SUBLANES):
                    rows = pl.ds(r0 + (first + k - b) + j * SUBLANES, SUBLANES)
                    acc[j] = acc[j] + ext_ref[b, rows, cols] * wk
            for j in range(CONV_ROWS // SUBLANES):
                conv_ref[pl.ds(r0 + j * SUBLANES, SUBLANES), cols] = acc[j]
        return carry

    lax.fori_loop(0, ts // CONV_ROWS, chunk, 0)
    for b in range(SUBLANES):
        ext_ref[b, 0:CONV_CARRY, :] = ext_ref[b, ts:ts + CONV_CARRY, :]

    c = conv_ref[...] + bdw_ref[...]
    mu = jnp.mean(c, axis=-1, keepdims=True)
    cc = c - mu
    var = jnp.mean(cc * cc, axis=-1, keepdims=True)
    un = cc * lax.rsqrt(var + EPS) * lng_ref[...] + lnb_ref[...]
    act = (un * _sigmoid(un)).astype(BF16)
    o_ref[0] = x + _dot(act, wout_ref[...]) + bout_ref[...]


def _conv_mixer(x, g, w_in, b_in, w_dw, b_dw, ln_g, ln_b, w_out, b_out):
    b, s, d = x.shape
    ts = min(TILE_CONV, s)
    assert s % ts == 0 and ts % CONV_ROWS == 0 and d % CONV_COLS == 0
    row = lambda v: v.reshape(1, -1)
    const = lambda shape: pl.BlockSpec(shape, lambda i, j: (0,) * len(shape))
    return pl.pallas_call(
        _conv_kernel,
        out_shape=jax.ShapeDtypeStruct(x.shape, F32),
        grid=(b, s // ts),
        in_specs=[
            pl.BlockSpec((1, ts, d), lambda i, j: (i, j, 0)),
            const((1, d)), const((d, 2 * d)), const((1, 2 * d)), const((CONV_KERNEL * SUBLANES, d)), const((1, d)),
            const((1, d)), const((1, d)), const((d, d)), const((1, d)),
        ],
        out_specs=pl.BlockSpec((1, ts, d), lambda i, j: (i, j, 0)),
        scratch_shapes=[pltpu.VMEM((SUBLANES, CONV_CARRY + ts, d), F32), pltpu.VMEM((ts, d), F32)],
        compiler_params=_cparams(("arbitrary", "arbitrary")),
        name="conv_mixer",
    )(x, row(g), w_in.astype(BF16), row(b_in), jnp.repeat(w_dw, SUBLANES, axis=0), row(b_dw), row(ln_g), row(ln_b),
      w_out.astype(BF16), row(b_out))


def _gla_levels(ts):
    sizes = [GLA_LEAF]
    while sizes[-1] < ts:
        sizes.append(sizes[-1] * 2)
    return sizes


def _moe_residual(x, route, y0, y1):
    g = lax.bitcast_convert_type(route, F32)
    g = jnp.concatenate([g, jnp.zeros((LANES - g.shape[0], g.shape[1]), F32)], axis=0).T
    gate0 = g[:, 2 * TOP_K:2 * TOP_K + 1]
    gate1 = g[:, 2 * TOP_K + 1:2 * TOP_K + 2]
    return x + _unpack_bf16_pairs(y0) * gate0 + _unpack_bf16_pairs(y1) * gate1


def _gla_kernel(x_ref, gate_ref, y0_ref, y1_ref, g_ref, wq_ref, wk_ref, wv_ref, wa_ref, wr_ref, wa2_ref, ba_ref,
                ng_ref, wo_ref, o_ref, state_ref, x_s, q_s, k_s, v_s, og_s, la_s, *, tiles_per_seq):
    j = pl.program_id(0)
    dkh = wq_ref.shape[1] // GLA_HEADS

    @pl.when(j == 0)
    def _():
        for ref in (x_s, q_s, k_s, og_s, la_s):
            ref[...] = jnp.zeros(ref.shape, F32)
        v_s[...] = jnp.zeros(v_s.shape, BF16)

    @pl.when((j == 0) | ((j - 1) % tiles_per_seq == 0))
    def _():
        state_ref[...] = jnp.zeros(state_ref.shape, F32)

    def stage_a(slot):
        x = _moe_residual(x_ref[0], gate_ref[...], y0_ref[...], y1_ref[...])
        h = _rms(x, g_ref[...]).astype(BF16)
        x_s[slot] = x
        q_s[slot] = _dot(h, wq_ref[...]) * (dkh ** -0.5)
        k_s[slot] = _dot(h, wk_ref[...])
        v_s[slot] = _dot(h, wv_ref[...]).astype(BF16)
        r = _dot(h, wr_ref[...])
        og_s[slot] = r * _sigmoid(r)
        a = _dot(h, wa_ref[...]).astype(BF16)
        z = _dot(jnp.concatenate([a, a], axis=1), wa2_ref[...]) + ba_ref[...]
        la_s[slot] = -(jnp.maximum(-z, 0.0) + jnp.log(1.0 + jnp.exp(-jnp.abs(z)))) * (1.0 / GLA_TAU)

    def stage_b(slot):
        _gla_recurrence(x_s[slot], q_s[slot], k_s[slot], v_s[slot], og_s[slot], la_s[slot],
                        ng_ref, wo_ref, o_ref, state_ref)

    @pl.when(j % 2 == 0)
    def _():
        stage_b(1)
        stage_a(0)

    @pl.when(j % 2 == 1)
    def _():
        stage_b(0)
        stage_a(1)


def _gla_recurrence(x, q, k, v, out_gate, log_a, ng_ref, wo_ref, o_ref, state_ref):
    ts, d = x.shape
    dk = q.shape[1]
    dkh = dk // GLA_HEADS
    dvh = d // GLA_HEADS

    row = lax.broadcasted_iota(I32, (ts, ts), 0)
    col = lax.broadcasted_iota(I32, (ts, ts), 1)
    tri = jnp.where(col <= row, 1.0, 0.0).astype(BF16)
    bcum = _dot(tri, log_a.astype(BF16))
    b_last = bcum[ts - 1:ts, :]

    q_in = (q * jnp.exp(bcum)).astype(BF16)
    k_out = (k * jnp.exp(b_last - bcum)).astype(BF16)

    sizes = _gla_levels(ts)
    scores = [None] * GLA_HEADS
    for lvl, size in enumerate(sizes):
        half = size // 2
        same_block = (row & -size) == (col & -size)
        if lvl == 0:
            pair = same_block & (col <= row)
            q_ok = k_ok = None
        else:
            pair = same_block & ((row & (size - 1)) >= half) & ((col & (size - 1)) < half)
            pos = lax.broadcasted_iota(I32, (ts, dk), 0) & (size - 1)
            q_ok = pos >= half
            k_ok = pos < half
        ref = jnp.concatenate(
            [jnp.broadcast_to(bcum[r0 + half:r0 + half + 1, :], (size, dk)) for r0 in range(0, ts, size)], axis=0)
        ql = q * jnp.exp(bcum - ref)
        kl = k * jnp.exp(ref - bcum)
        if q_ok is not None:
            ql = jnp.where(q_ok, ql, 0.0)
            kl = jnp.where(k_ok, kl, 0.0)
        ql = ql.astype(BF16)
        kl = kl.astype(BF16)
        for hd in range(GLA_HEADS):
            c0 = hd * dkh
            a = _dot_nt(ql[:, c0:c0 + dkh], kl[:, c0:c0 + dkh])
            a = jnp.where(pair, a, 0.0)
            scores[hd] = a if scores[hd] is None else scores[hd] + a

    outs = []
    for hd in range(GLA_HEADS):
        c0 = hd * dkh
        v_h = v[:, hd * dvh:(hd + 1) * dvh]
        st = state_ref[hd]
        o_h = _dot(scores[hd].astype(BF16), v_h) + _dot_nt(q_in[:, c0:c0 + dkh], st.astype(BF16))
        decay = jnp.exp(b_last[:, c0:c0 + dkh])
        state_ref[hd] = st * decay + _dot_tn(v_h, k_out[:, c0:c0 + dkh])
        o_h = o_h * lax.rsqrt(jnp.mean(o_h * o_h, axis=-1, keepdims=True) + EPS) * ng_ref[...]
        outs.append(o_h)
    o = jnp.concatenate(outs, axis=1) * out_gate
    o_ref[0] = x + _dot(o.astype(BF16), wo_ref[...])


def _gla_mixer(x, gates, y0, y1, g, w_in, w_a2, b_a, norm_g, w_o):
    b, s, d = x.shape
    dk = w_a2.shape[1]
    ts = min(TILE_GLA, s)
    assert s % ts == 0 and ts % GLA_LEAF == 0
    nj = s // ts
    row = lambda v: v.reshape(1, -1)
    const = lambda shape: pl.BlockSpec(shape, lambda j: (0,) * len(shape))
    wq = w_in[:, :dk].astype(BF16)
    wk = w_in[:, dk:2 * dk].astype(BF16)
    wv = w_in[:, 2 * dk:2 * dk + d].astype(BF16)
    wa = jnp.pad(w_in[:, 2 * dk + d:2 * dk + d + GLA_RANK], ((0, 0), (0, GLA_RANK_PAD - GLA_RANK))).astype(BF16)
    wr = w_in[:, 2 * dk + d + GLA_RANK:].astype(BF16)
    wa2 = jnp.pad(w_a2, ((0, GLA_RANK_PAD - GLA_RANK), (0, 0)))
    wa2_hi = wa2.astype(BF16)
    wa2_lo = (wa2 - wa2_hi.astype(F32)).astype(BF16)
    wa2_split = jnp.concatenate([wa2_hi, wa2_lo], axis=0)
    dvh = d // GLA_HEADS
    n = b * nj

    def tile(j, lag):
        return jnp.clip(j - lag, 0, n - 1)

    tokens = lambda lag: (lambda j: (tile(j, lag), 0))
    return pl.pallas_call(
        functools.partial(_gla_kernel, tiles_per_seq=nj),
        out_shape=jax.ShapeDtypeStruct(x.shape, F32),
        grid=(n + 1,),
        in_specs=[
            pl.BlockSpec((1, ts, d), lambda j: (tile(j, 0) // nj, tile(j, 0) % nj, 0)),
            pl.BlockSpec((ROUTE_ROWS, ts), lambda j: (0, tile(j, 0))),
            pl.BlockSpec((ts, d // 2), tokens(0)),
            pl.BlockSpec((ts, d // 2), tokens(0)),
            const((1, d)), const((d, dk)), const((d, dk)), const((d, d)), const((d, GLA_RANK_PAD)),
            const((d, d)), const((2 * GLA_RANK_PAD, dk)), const((1, dk)), const((1, dvh)), const((d, d)),
        ],
        out_specs=pl.BlockSpec((1, ts, d), lambda j: (tile(j, 1) // nj, tile(j, 1) % nj, 0)),
        scratch_shapes=[pltpu.VMEM((GLA_HEADS, dvh, dk // GLA_HEADS), F32),
                        pltpu.VMEM((2, ts, d), F32), pltpu.VMEM((2, ts, dk), F32), pltpu.VMEM((2, ts, dk), F32),
                        pltpu.VMEM((2, ts, d), BF16), pltpu.VMEM((2, ts, d), F32), pltpu.VMEM((2, ts, dk), F32)],
        compiler_params=_cparams(("arbitrary",)),
        name="gla_mixer",
    )(x, gates, y0, y1, row(g), wq, wk, wv, wa, wr, wa2_split, row(b_a), row(norm_g), w_o.astype(BF16))


def _xattn_router_kernel(x_ref, gx_ref, wq_ref, k_ref, v_ref, wo_ref, gf_ref, wr_ref, br_ref, upper_ref,
                         x_out_ref, h_out_ref, route_ref, cnt_ref, carry_ref):
    ts, d = x_ref.shape[1], x_ref.shape[2]
    hd = d // XATTN_HEADS
    first = (pl.program_id(0) == 0) & (pl.program_id(1) == 0)

    @pl.when(first)
    def _():
        carry_ref[...] = jnp.zeros(carry_ref.shape, F32)

    x = x_ref[0]
    h = _rms(x, gx_ref[...]).astype(BF16)
    q = (_dot(h, wq_ref[...]) * (hd ** -0.5)).astype(BF16)
    k = k_ref[0, 0]
    v = v_ref[0, 0]
    outs = []
    for a in range(XATTN_HEADS):
        sl = slice(a * hd, (a + 1) * hd)
        s = _dot_nt(q[:, sl], k[:, sl])
        p = jnp.exp(s - jnp.max(s, axis=-1, keepdims=True))
        o = _dot(p.astype(BF16), v[:, sl]) / jnp.sum(p, axis=-1, keepdims=True)
        outs.append(o)
    att = jnp.concatenate(outs, axis=1).astype(BF16)
    x2 = x + _dot(att, wo_ref[...])
    x_out_ref[0] = x2

    hf = _rms(x2, gf_ref[...])
    h_out_ref[...] = _pack_bf16_pairs(hf)

    both = _dot_nt(wr_ref[...], hf.astype(BF16))
    logits = both[0:ROUTER_ROWS, :] + both[ROUTER_ROWS:2 * ROUTER_ROWS, :] + br_ref[...]
    gl = logits[N_EXPERTS:N_EXPERTS + N_GROUPS, :]
    gi = lax.broadcasted_iota(I32, gl.shape, 0).astype(F32)
    gmax = jnp.max(gl, axis=0, keepdims=True)
    g_sel = jnp.min(jnp.where(gl == gmax, gi, float(N_GROUPS)), axis=0, keepdims=True)
    pg_sel = 1.0 / jnp.sum(jnp.exp(gl - gmax), axis=0, keepdims=True)

    el = jnp.zeros((EXPERTS_PER_GROUP, ts), F32)
    for gidx in range(N_GROUPS):
        lo = gidx * EXPERTS_PER_GROUP
        el = jnp.where(g_sel == float(gidx), logits[lo:lo + EXPERTS_PER_GROUP, :], el)
    ei = lax.broadcasted_iota(I32, el.shape, 0).astype(F32)
    m1 = jnp.max(el, axis=0, keepdims=True)
    i1 = jnp.min(jnp.where(el == m1, ei, float(EXPERTS_PER_GROUP)), axis=0, keepdims=True)
    rest = jnp.where(ei == i1, -jnp.inf, el)
    m2 = jnp.max(rest, axis=0, keepdims=True)
    i2 = jnp.min(jnp.where(rest == m2, ei, float(EXPERTS_PER_GROUP)), axis=0, keepdims=True)
    ratio = jnp.exp(m2 - m1)
    gate1 = pg_sel / (1.0 + ratio)
    gate2 = pg_sel * ratio / (1.0 + ratio)
    e1 = g_sel * float(EXPERTS_PER_GROUP) + i1
    e2 = g_sel * float(EXPERTS_PER_GROUP) + i2

    xi = lax.broadcasted_iota(I32, (N_EXPERTS, ts), 0).astype(F32)
    oh1 = jnp.where(xi == e1, 1.0, 0.0)
    oh2 = jnp.where(xi == e2, 1.0, 0.0)
    oh = oh1 + oh2
    n_blk = ts // LANES
    stacked = jnp.concatenate([oh[:, c * LANES:(c + 1) * LANES] for c in range(n_blk)], axis=0)
    within = _dot(stacked.astype(BF16), upper_ref[...])
    totals = jnp.sum(stacked, axis=1, keepdims=True)
    run = carry_ref[...]
    before = []
    for c in range(n_blk):
        before.append(within[c * N_EXPERTS:(c + 1) * N_EXPERTS, :] + run)
        run = run + totals[c * N_EXPERTS:(c + 1) * N_EXPERTS, :]
    before = jnp.concatenate(before, axis=1)
    rank1 = jnp.sum(oh1 * before, axis=0, keepdims=True)
    rank2 = jnp.sum(oh2 * before, axis=0, keepdims=True)
    carry_ref[...] = run
    cnt_ref[...] = jnp.broadcast_to(run, cnt_ref.shape)

    route_ref[0:1, :] = e1.astype(I32)
    route_ref[1:2, :] = e2.astype(I32)
    route_ref[2:3, :] = rank1.astype(I32)
    route_ref[3:4, :] = rank2.astype(I32)
    route_ref[4:5, :] = lax.bitcast_convert_type(gate1, I32)
    route_ref[5:6, :] = lax.bitcast_convert_type(gate2, I32)
    route_ref[3 * TOP_K:ROUTE_ROWS, :] = jnp.zeros((ROUTE_ROWS - 3 * TOP_K, ts), I32)


def _xattn_router(x, g_x, w_q, k_mem, v_mem, layer, w_o, g_f, w_grp, b_grp, w_exp, b_exp):
    b, s, d = x.shape
    nm = k_mem.shape[2]
    t = b * s
    ts = min(TILE_XATTN, s)
    assert s % ts == 0 and ts % LANES == 0
    nj = s // ts
    row = lambda v: v.reshape(1, -1)
    const = lambda shape: pl.BlockSpec(shape, lambda i, j: (0,) * len(shape))
    pad = ROUTER_ROWS - N_GROUPS - N_EXPERTS
    w_r = jnp.pad(jnp.concatenate([w_exp, w_grp], axis=1).T, ((0, pad), (0, 0)))
    w_r_hi = w_r.astype(BF16)
    w_r_split = jnp.concatenate([w_r_hi, (w_r - w_r_hi.astype(F32)).astype(BF16)], axis=0)
    b_r = jnp.pad(jnp.concatenate([b_exp, b_grp]), (0, pad)).reshape(ROUTER_ROWS, 1)
    ti = jnp.arange(LANES)
    upper = (ti[:, None] < ti[None, :]).astype(BF16)
    return pl.pallas_call(
        _xattn_router_kernel,
        out_shape=(
            jax.ShapeDtypeStruct(x.shape, F32),
            jax.ShapeDtypeStruct((t, d // 2), U32),
            jax.ShapeDtypeStruct((ROUTE_ROWS, t), I32),
            jax.ShapeDtypeStruct((N_EXPERTS, LANES), F32),
        ),
        grid=(b, nj),
        in_specs=[
            pl.BlockSpec((1, ts, d), lambda i, j: (i, j, 0)),
            const((1, d)), const((d, d)),
            pl.BlockSpec((1, 1, nm, d), lambda i, j: (layer, i, 0, 0)),
            pl.BlockSpec((1, 1, nm, d), lambda i, j: (layer, i, 0, 0)),
            const((d, d)), const((1, d)), const((2 * ROUTER_ROWS, d)), const((ROUTER_ROWS, 1)), const((LANES, LANES)),
        ],
        out_specs=(
            pl.BlockSpec((1, ts, d), lambda i, j: (i, j, 0)),
            pl.BlockSpec((ts, d // 2), lambda i, j: (i * nj + j, 0)),
            pl.BlockSpec((ROUTE_ROWS, ts), lambda i, j: (0, i * nj + j)),
            pl.BlockSpec((N_EXPERTS, LANES), lambda i, j: (0, 0)),
        ),
        scratch_shapes=[pltpu.VMEM((N_EXPERTS, 1), F32)],
        compiler_params=_cparams(("arbitrary", "arbitrary")),
        name="xattn_router",
    )(x, row(g_x), w_q.astype(BF16), k_mem, v_mem, w_o.astype(BF16), row(g_f), w_r_split, b_r, upper)


def _sc_mesh():
    return plsc.VectorSubcoreMesh(core_axis_name="c", subcore_axis_name="s",
                                  num_cores=SC_CORES, num_subcores=SC_SUBCORES)


def _sc_worker():
    return lax.axis_index("s") * SC_CORES + lax.axis_index("c")


def _dispatch(h_packed, dest, n_rows):
    t, w = h_packed.shape
    chunk = SC_DISPATCH_CHUNK
    per_worker = t // SC_WORKERS
    n_chunks = per_worker // chunk
    assert n_chunks % 2 == 0 and n_chunks * chunk * SC_WORKERS == t
    dest = dest.reshape(TOP_K, SC_WORKERS * n_chunks, chunk)
    rows_buf = pltpu.VMEM((chunk, w), U32)

    @functools.partial(
        pl.kernel, mesh=_sc_mesh(),
        out_type=jax.ShapeDtypeStruct((n_rows, w), U32),
        scratch_types=[pltpu.VMEM((n_chunks, chunk), I32), pltpu.VMEM((n_chunks, chunk), I32), rows_buf, rows_buf,
                       pltpu.SemaphoreType.DMA((2,)), pltpu.SemaphoreType.DMA((2, TOP_K))],
        name="moe_dispatch_sc",
    )
    def run(h_hbm, d0_hbm, d1_hbm, xbuf_hbm, idx0_v, idx1_v, buf_a, buf_b, read_sem, write_sem):
        wid = _sc_worker()
        pltpu.sync_copy(d0_hbm.at[pl.ds(wid * n_chunks, n_chunks)], idx0_v)
        pltpu.sync_copy(d1_hbm.at[pl.ds(wid * n_chunks, n_chunks)], idx1_v)

        @pl.loop(0, n_chunks, step=2)
        def _(i):
            reads = [pltpu.async_copy(h_hbm.at[pl.ds(wid * per_worker + (i + j) * chunk, chunk)], buf, read_sem.at[j])
                     for j, buf in enumerate((buf_a, buf_b))]
            writes = []
            for j, buf in enumerate((buf_a, buf_b)):
                reads[j].wait()
                writes.append(pltpu.async_copy(buf, xbuf_hbm.at[idx0_v.at[i + j]], write_sem.at[j, 0]))
                writes.append(pltpu.async_copy(buf, xbuf_hbm.at[idx1_v.at[i + j]], write_sem.at[j, 1]))
            for copy in writes:
                copy.wait()

    return run(h_packed, dest[0], dest[1])


def _gather_pairs(y_buf, dest):
    t = dest.shape[1]
    w = y_buf.shape[1]
    chunk = SC_GATHER_CHUNK
    per_worker = t // SC_WORKERS
    n_chunks = per_worker // chunk
    assert n_chunks % 2 == 0 and n_chunks * chunk * SC_WORKERS == t
    dest = dest.reshape(TOP_K, SC_WORKERS * n_chunks, chunk)
    out = jax.ShapeDtypeStruct((t, w), U32)
    rows_buf = pltpu.VMEM((chunk, w), U32)

    @functools.partial(
        pl.kernel, mesh=_sc_mesh(),
        out_type=(out, out),
        scratch_types=[pltpu.VMEM((n_chunks, chunk), I32), pltpu.VMEM((n_chunks, chunk), I32),
                       rows_buf, rows_buf, rows_buf, rows_buf, pltpu.SemaphoreType.DMA((2, TOP_K))],
        name="moe_gather_sc",
    )
    def run(y_hbm, d0_hbm, d1_hbm, y0_hbm, y1_hbm, idx0_v, idx1_v, buf_a0, buf_a1, buf_b0, buf_b1, sem):
        wid = _sc_worker()
        pltpu.sync_copy(d0_hbm.at[pl.ds(wid * n_chunks, n_chunks)], idx0_v)
        pltpu.sync_copy(d1_hbm.at[pl.ds(wid * n_chunks, n_chunks)], idx1_v)

        @pl.loop(0, n_chunks, step=2)
        def _(i):
            bufs = ((buf_a0, buf_a1), (buf_b0, buf_b1))
            gathers = [[pltpu.async_copy(y_hbm.at[idx_v.at[i + j]], bufs[j][k], sem.at[j, k])
                        for k, idx_v in enumerate((idx0_v, idx1_v))] for j in range(2)]
            stores = []
            for j in range(2):
                rows = pl.ds(wid * per_worker + (i + j) * chunk, chunk)
                for k, out_hbm in enumerate((y0_hbm, y1_hbm)):
                    gathers[j][k].wait()
                    stores.append(pltpu.async_copy(bufs[j][k], out_hbm.at[rows], sem.at[j, k]))
            for copy in stores:
                copy.wait()

    return run(y_buf, dest[0], dest[1])


def _expert_kernel(be_ref, bf_ref, bv_ref, slot_ref, next_ref, x_ref, wg_hbm, wu_hbm, wd_hbm, y_ref,
                   wg_f32, wu_f32, wd_f32, wg_bf, wu_bf, wd_bf, sems, *, layer):
    blk = pl.program_id(0)
    valid = bv_ref[blk]

    def weight_copies(expert, slot):
        return [pltpu.make_async_copy(src.at[layer, expert], dst.at[slot], sems.at[slot, i])
                for i, (src, dst) in enumerate(((wg_hbm, wg_f32), (wu_hbm, wu_f32), (wd_hbm, wd_f32)))]

    @pl.when(blk == 0)
    def _():
        for copy in weight_copies(be_ref[0], 0):
            copy.start()

    @pl.when(bf_ref[blk] == 1)
    def _():
        slot = slot_ref[blk]
        for copy in weight_copies(be_ref[blk], slot):
            copy.wait()

        @pl.when(next_ref[blk] >= 0)
        def _():
            for copy in weight_copies(next_ref[blk], 1 - slot):
                copy.start()

        wg_bf[...] = wg_f32[slot].astype(BF16)
        wu_bf[...] = wu_f32[slot].astype(BF16)
        wd_bf[...] = wd_f32[slot].astype(BF16)

    @pl.when(valid > 0)
    def _():
        live = lax.broadcasted_iota(I32, x_ref.shape, 0) < valid
        xb = _unpack_bf16_pairs(jnp.where(live, x_ref[...], jnp.uint32(0))).astype(BF16)
        gt = _dot(xb, wg_bf[...])
        up = _dot(xb, wu_bf[...])
        act = (gt * _sigmoid(gt) * up).astype(BF16)
        y_ref[...] = _pack_bf16_pairs(_dot(act, wd_bf[...]))

    @pl.when(valid <= 0)
    def _():
        y_ref[...] = jnp.zeros(y_ref.shape, U32)


def _experts(x_buf, block_expert, block_first, block_valid, w_gate, w_up, w_down, layer):
    n_rows, w = x_buf.shape
    d, de = w_gate.shape[2], w_gate.shape[3]
    bm = MOE_BLOCK_ROWS
    n = n_rows // bm
    block_slot = (jnp.cumsum(block_first) - 1) % 2
    idx = jnp.arange(n, dtype=I32)
    later_first = jnp.concatenate([jnp.where(block_first[1:] == 1, idx[1:], n), jnp.full((1,), n, I32)])
    next_first = lax.cummin(later_first, reverse=True)
    block_next = jnp.where(next_first < n, block_expert[jnp.minimum(next_first, n - 1)], -1).astype(I32)
    any_space = pl.BlockSpec(memory_space=pl.ANY)
    grid_spec = pltpu.PrefetchScalarGridSpec(
        num_scalar_prefetch=5,
        grid=(n,),
        in_specs=[pl.BlockSpec((bm, w), lambda i, *_: (i, 0)), any_space, any_space, any_space],
        out_specs=pl.BlockSpec((bm, w), lambda i, *_: (i, 0)),
        scratch_shapes=[pltpu.VMEM((2, d, de), F32), pltpu.VMEM((2, d, de), F32), pltpu.VMEM((2, de, d), F32),
                        pltpu.VMEM((d, de), BF16), pltpu.VMEM((d, de), BF16), pltpu.VMEM((de, d), BF16),
                        pltpu.SemaphoreType.DMA((2, 3))],
    )
    return pl.pallas_call(
        functools.partial(_expert_kernel, layer=layer),
        out_shape=jax.ShapeDtypeStruct((n_rows, w), U32),
        grid_spec=grid_spec,
        compiler_params=_cparams(("arbitrary",)),
        name="moe_experts",
    )(block_expert, block_first, block_valid, block_slot.astype(I32), block_next, x_buf, w_gate, w_up, w_down)


def _combine_kernel(x_ref, gate_ref, y0_ref, y1_ref, gfin_ref, *rest, final_norm):
    o_ref = rest[-1]
    out = _moe_residual(x_ref[...], gate_ref[...], y0_ref[...], y1_ref[...])
    if final_norm:
        out = _rms(out, gfin_ref[...])
    o_ref[...] = out


def _combine(x2, y0, y1, gates, g_final, final_norm, first_row=0, out_so_far=None):
    t, d = x2.shape
    rows, w = y0.shape
    ts = min(TILE_COMBINE, rows)
    assert rows % ts == 0 and first_row % ts == 0
    off = first_row // ts
    in_specs = [
        pl.BlockSpec((ts, d), lambda i: (i + off, 0)),
        pl.BlockSpec((ROUTE_ROWS, ts), lambda i: (0, i + off)),
        pl.BlockSpec((ts, w), lambda i: (i, 0)),
        pl.BlockSpec((ts, w), lambda i: (i, 0)),
        pl.BlockSpec((1, d), lambda i: (0, 0)),
    ]
    args = [x2, gates, y0, y1, g_final.reshape(1, d)]
    aliases = {}
    if out_so_far is not None:
        in_specs.append(pl.BlockSpec(memory_space=pl.ANY))
        args.append(out_so_far)
        aliases = {len(args) - 1: 0}
    return pl.pallas_call(
        functools.partial(_combine_kernel, final_norm=final_norm),
        out_shape=jax.ShapeDtypeStruct((t, d), F32),
        grid=(rows // ts,),
        in_specs=in_specs,
        out_specs=pl.BlockSpec((ts, d), lambda i: (i + off, 0)),
        input_output_aliases=aliases,
        compiler_params=_cparams(("arbitrary",)),
        name="moe_combine",
    )(*args)


def _moe_layout(route, counts):
    bm = MOE_BLOCK_ROWS
    t = route.shape[1]
    assert (t * TOP_K) % bm == 0
    n_blocks = (t * TOP_K) // bm + N_EXPERTS
    cnt = counts[:, 0].astype(I32)
    padded = (cnt + bm - 1) // bm * bm
    pad_ends = jnp.cumsum(padded)
    pad_off = pad_ends - padded
    experts = jnp.arange(N_EXPERTS, dtype=I32)
    hit = route[0:TOP_K, :, None] == experts
    dest = jnp.sum(jnp.where(hit, pad_off, 0), axis=-1) + route[TOP_K:2 * TOP_K]
    gates = route
    starts = jnp.arange(n_blocks, dtype=I32) * bm
    block_expert = jnp.minimum(jnp.sum((pad_ends[None, :] <= starts[:, None]).astype(I32), axis=1),
                               N_EXPERTS - 1)
    block_first = jnp.concatenate([jnp.ones((1,), I32), (block_expert[1:] != block_expert[:-1]).astype(I32)])
    own = block_expert[:, None] == experts
    block_valid = jnp.clip(jnp.sum(jnp.where(own, cnt + pad_off, 0), axis=1) - starts, 0, bm)
    block_valid = jnp.where(starts < pad_ends[-1], block_valid, 0).astype(I32)
    return dest, gates, block_expert, block_first, block_valid, n_blocks * bm


def kernel(x, mem, norm_mix, norm_xattn, norm_ffn, norm_mem, norm_final, conv_w_in, conv_b_in, conv_w_dw,
           conv_b_dw, conv_ln_g, conv_ln_b, conv_w_out, conv_b_out, gla_w_in, gla_w_a2, gla_b_a, gla_norm_g,
           gla_w_o, xa_w_q, xa_w_kv, xa_w_o, moe_w_grp, moe_b_grp, moe_w_exp, moe_b_exp, moe_w_gate, moe_w_up,
           moe_w_down):
    b, s, d = x.shape
    depth = norm_mix.shape[0]
    k_mem, v_mem = _mem_kv(mem, norm_mem, xa_w_kv)
    moe = None
    for i in range(depth):
        j = i // 2
        if i % 2 == 0:
            if moe is not None:
                x = _combine(x.reshape(b * s, d), moe[1], moe[2], moe[0], norm_final, False).reshape(b, s, d)
            x = _conv_mixer(x, norm_mix[i], conv_w_in[j], conv_b_in[j], conv_w_dw[j], conv_b_dw[j],
                            conv_ln_g[j], conv_ln_b[j], conv_w_out[j], conv_b_out[j])
        else:
            x = _gla_mixer(x, *moe, norm_mix[i], gla_w_in[j], gla_w_a2[j], gla_b_a[j], gla_norm_g[j], gla_w_o[j])
        x2, h_packed, route, counts = _xattn_router(
            x, norm_xattn[i], xa_w_q[i], k_mem, v_mem, i, xa_w_o[i], norm_ffn[i],
            moe_w_grp[i], moe_b_grp[i], moe_w_exp[i], moe_b_exp[i])
        dest, gates, block_expert, block_first, block_valid, n_rows = _moe_layout(route, counts)
        x_buf = _dispatch(h_packed, dest, n_rows)
        y_buf = _experts(x_buf, block_expert, block_first, block_valid, moe_w_gate, moe_w_up, moe_w_down, i)
        if i < depth - 1:
            y0, y1 = _gather_pairs(y_buf, dest)
            x, moe = x2, (gates, y0, y1)
    part = (b * s) // FINAL_PARTS
    out = None
    for p in range(FINAL_PARTS):
        y0, y1 = _gather_pairs(y_buf, dest[:, p * part:(p + 1) * part])
        out = _combine(x2.reshape(b * s, d), y0, y1, gates, norm_final, True, p * part, out)
    return out.reshape(b, s, d)
```

```python
import functools

import jax
import jax.numpy as jnp
from jax import lax
from jax.experimental import pallas as pl
from jax.experimental.pallas import tpu as pltpu
from jax.experimental.pallas import tpu_sc as plsc

F32 = jnp.float32
BF16 = jnp.bfloat16
I32 = jnp.int32
U32 = jnp.uint32

EPS = 1e-6
CONV_KERNEL = 31
CONV_CARRY = 32
CONV_ROWS = 64
CONV_COLS = 256
SUBLANES = 8
LANES = 128
GLA_HEADS = 4
GLA_RANK = 16
GLA_RANK_PAD = 128
GLA_TAU = 16.0
GLA_LEAF = 32
XATTN_HEADS = 4
N_GROUPS = 4
EXPERTS_PER_GROUP = 8
N_EXPERTS = N_GROUPS * EXPERTS_PER_GROUP
ROUTER_ROWS = 40
TOP_K = 2
ROUTE_ROWS = 8

TILE_CONV = 512
TILE_GLA = 256
TILE_XATTN = 1024
TILE_COMBINE = 512
FINAL_PARTS = 4
SC_CORES = 2
SC_SUBCORES = 16
SC_WORKERS = SC_CORES * SC_SUBCORES
SC_DISPATCH_CHUNK = 64
SC_GATHER_CHUNK = 32
MOE_BLOCK_ROWS = 512
VMEM_LIMIT = 56 * 1024 * 1024


def _cparams(sem):
    return pltpu.CompilerParams(dimension_semantics=sem, vmem_limit_bytes=VMEM_LIMIT)


def _rms(x, g):
    return x * lax.rsqrt(jnp.mean(x * x, axis=-1, keepdims=True) + EPS) * g


def _sigmoid(x):
    return 0.5 * jnp.tanh(0.5 * x) + 0.5


def _dot(a, b):
    return jnp.dot(a, b, preferred_element_type=F32)


def _dot_nt(a, b):
    return lax.dot_general(a, b, (((1,), (1,)), ((), ())), preferred_element_type=F32)


def _dot_tn(a, b):
    return lax.dot_general(a, b, (((0,), (0,)), ((), ())), preferred_element_type=F32)


def _pack_bf16_pairs(x):
    w = x.shape[1] // 2
    hi = lax.bitcast_convert_type(x[:, :w].astype(BF16).astype(F32), U32)
    lo = lax.bitcast_convert_type(x[:, w:].astype(BF16).astype(F32), U32)
    return hi | (lo >> 16)


def _unpack_bf16_pairs(p):
    hi = lax.bitcast_convert_type(p & jnp.uint32(0xFFFF0000), F32)
    lo = lax.bitcast_convert_type(p << 16, F32)
    return jnp.concatenate([hi, lo], axis=1)


def _memkv_kernel(mem_ref, g_ref, w_ref, k_ref, v_ref):
    d = mem_ref.shape[-1]
    mn = _rms(mem_ref[0], g_ref[...]).astype(BF16)
    kv = _dot(mn, w_ref[0])
    k_ref[0, 0] = kv[:, :d].astype(BF16)
    v_ref[0, 0] = kv[:, d:].astype(BF16)


def _mem_kv(mem, norm_mem, w_kv):
    b, nm, d = mem.shape
    depth = w_kv.shape[0]
    out = jax.ShapeDtypeStruct((depth, b, nm, d), BF16)
    return pl.pallas_call(
        _memkv_kernel,
        out_shape=(out, out),
        grid=(depth, b),
        in_specs=[
            pl.BlockSpec((1, nm, d), lambda l, i: (i, 0, 0)),
            pl.BlockSpec((1, d), lambda l, i: (0, 0)),
            pl.BlockSpec((1, d, 2 * d), lambda l, i: (l, 0, 0)),
        ],
        out_specs=(
            pl.BlockSpec((1, 1, nm, d), lambda l, i: (l, i, 0, 0)),
            pl.BlockSpec((1, 1, nm, d), lambda l, i: (l, i, 0, 0)),
        ),
        compiler_params=_cparams(("arbitrary", "arbitrary")),
        name="mem_kv",
    )(mem, norm_mem.reshape(1, d), w_kv.astype(BF16))


def _conv_kernel(x_ref, g_ref, win_ref, bin_ref, wdw_ref, bdw_ref, lng_ref, lnb_ref, wout_ref, bout_ref,
                 o_ref, ext_ref, conv_ref):
    ts, d = x_ref.shape[1], x_ref.shape[2]

    @pl.when(pl.program_id(1) == 0)
    def _():
        ext_ref[...] = jnp.zeros(ext_ref.shape, F32)

    x = x_ref[0]
    h = _rms(x, g_ref[...]).astype(BF16)
    u = _dot(h, win_ref[...]) + bin_ref[...]
    glu = u[:, :d] * _sigmoid(u[:, d:])
    for b in range(SUBLANES):
        ext_ref[b, CONV_CARRY - b:CONV_CARRY - b + ts, :] = glu

    first = CONV_CARRY - (CONV_KERNEL - 1)

    def chunk(i, carry):
        r0 = pl.multiple_of(i * CONV_ROWS, CONV_ROWS)
        for c0 in range(0, d, CONV_COLS):
            cols = slice(c0, c0 + CONV_COLS)
            acc = [jnp.zeros((SUBLANES, CONV_COLS), F32) for _ in range(CONV_ROWS // SUBLANES)]
            for k in range(CONV_KERNEL):
                b = (first + k) % SUBLANES
                wk = wdw_ref[k * SUBLANES:(k + 1) * SUBLANES, cols]
                for j in range(CONV_ROWS // SUBLANES):
                    rows = pl.ds(r0 + (first + k - b) + j * SUBLANES, SUBLANES)
                    acc[j] = acc[j] + ext_ref[b, rows, cols] * wk
            for j in range(CONV_ROWS // SUBLANES):
                conv_ref[pl.ds(r0 + j * SUBLANES, SUBLANES), cols] = acc[j]
        return carry

    lax.fori_loop(0, ts // CONV_ROWS, chunk, 0)
    for b in range(SUBLANES):
        ext_ref[b, 0:CONV_CARRY, :] = ext_ref[b, ts:ts + CONV_CARRY, :]

    c = conv_ref[...] + bdw_ref[...]
    mu = jnp.mean(c, axis=-1, keepdims=True)
    cc = c - mu
    var = jnp.mean(cc * cc, axis=-1, keepdims=True)
    un = cc * lax.rsqrt(var + EPS) * lng_ref[...] + lnb_ref[...]
    act = (un * _sigmoid(un)).astype(BF16)
    o_ref[0] = x + _dot(act, wout_ref[...]) + bout_ref[...]


def _conv_mixer(x, g, w_in, b_in, w_dw, b_dw, ln_g, ln_b, w_out, b_out):
    b, s, d = x.shape
    ts = min(TILE_CONV, s)
    assert s % ts == 0 and ts % CONV_ROWS == 0 and d % CONV_COLS == 0
    row = lambda v: v.reshape(1, -1)
    const = lambda shape: pl.BlockSpec(shape, lambda i, j: (0,) * len(shape))
    return pl.pallas_call(
        _conv_kernel,
        out_shape=jax.ShapeDtypeStruct(x.shape, F32),
        grid=(b, s // ts),
        in_specs=[
            pl.BlockSpec((1, ts, d), lambda i, j: (i, j, 0)),
            const((1, d)), const((d, 2 * d)), const((1, 2 * d)), const((CONV_KERNEL * SUBLANES, d)), const((1, d)),
            const((1, d)), const((1, d)), const((d, d)), const((1, d)),
        ],
        out_specs=pl.BlockSpec((1, ts, d), lambda i, j: (i, j, 0)),
        scratch_shapes=[pltpu.VMEM((SUBLANES, CONV_CARRY + ts, d), F32), pltpu.VMEM((ts, d), F32)],
        compiler_params=_cparams(("arbitrary", "arbitrary")),
        name="conv_mixer",
    )(x, row(g), w_in.astype(BF16), row(b_in), jnp.repeat(w_dw, SUBLANES, axis=0), row(b_dw), row(ln_g), row(ln_b),
      w_out.astype(BF16), row(b_out))


def _gla_levels(ts):
    sizes = [GLA_LEAF]
    while sizes[-1] < ts:
        sizes.append(sizes[-1] * 2)
    return sizes


def _moe_residual(x, route, y0, y1):
    g = lax.bitcast_convert_type(route, F32)
    g = jnp.concatenate([g, jnp.zeros((LANES - g.shape[0], g.shape[1]), F32)], axis=0).T
    gate0 = g[:, 2 * TOP_K:2 * TOP_K + 1]
    gate1 = g[:, 2 * TOP_K + 1:2 * TOP_K + 2]
    return x + _unpack_bf16_pairs(y0) * gate0 + _unpack_bf16_pairs(y1) * gate1


def _gla_kernel(x_ref, gate_ref, y0_ref, y1_ref, g_ref, wq_ref, wk_ref, wv_ref, wa_ref, wr_ref, wa2_ref, ba_ref,
                ng_ref, wo_ref, o_ref, state_ref, x_s, q_s, k_s, v_s, og_s, la_s, *, tiles_per_seq):
    j = pl.program_id(0)
    dkh = wq_ref.shape[1] // GLA_HEADS

    @pl.when(j == 0)
    def _():
        for ref in (x_s, q_s, k_s, og_s, la_s):
            ref[...] = jnp.zeros(ref.shape, F32)
        v_s[...] = jnp.zeros(v_s.shape, BF16)

    @pl.when((j == 0) | ((j - 1) % tiles_per_seq == 0))
    def _():
        state_ref[...] = jnp.zeros(state_ref.shape, F32)

    def stage_a(slot):
        x = _moe_residual(x_ref[0], gate_ref[...], y0_ref[...], y1_ref[...])
        h = _rms(x, g_ref[...]).astype(BF16)
        x_s[slot] = x
        q_s[slot] = _dot(h, wq_ref[...]) * (dkh ** -0.5)
        k_s[slot] = _dot(h, wk_ref[...])
        v_s[slot] = _dot(h, wv_ref[...]).astype(BF16)
        r = _dot(h, wr_ref[...])
        og_s[slot] = r * _sigmoid(r)
        a = _dot(h, wa_ref[...]).astype(BF16)
        z = _dot(jnp.concatenate([a, a], axis=1), wa2_ref[...]) + ba_ref[...]
        la_s[slot] = -(jnp.maximum(-z, 0.0) + jnp.log(1.0 + jnp.exp(-jnp.abs(z)))) * (1.0 / GLA_TAU)

    def stage_b(slot):
        _gla_recurrence(x_s[slot], q_s[slot], k_s[slot], v_s[slot], og_s[slot], la_s[slot],
                        ng_ref, wo_ref, o_ref, state_ref)

    @pl.when(j % 2 == 0)
    def _():
        stage_b(1)
        stage_a(0)

    @pl.when(j % 2 == 1)
    def _():
        stage_b(0)
        stage_a(1)


def _gla_recurrence(x, q, k, v, out_gate, log_a, ng_ref, wo_ref, o_ref, state_ref):
    ts, d = x.shape
    dk = q.shape[1]
    dkh = dk // GLA_HEADS
    dvh = d // GLA_HEADS

    row = lax.broadcasted_iota(I32, (ts, ts), 0)
    col = lax.broadcasted_iota(I32, (ts, ts), 1)
    tri = jnp.where(col <= row, 1.0, 0.0).astype(BF16)
    bcum = _dot(tri, log_a.astype(BF16))
    b_last = bcum[ts - 1:ts, :]

    q_in = (q * jnp.exp(bcum)).astype(BF16)
    k_out = (k * jnp.exp(b_last - bcum)).astype(BF16)

    sizes = _gla_levels(ts)
    scores = [None] * GLA_HEADS
    for lvl, size in enumerate(sizes):
        half = size // 2
        same_block = (row & -size) == (col & -size)
        if lvl == 0:
            pair = same_block & (col <= row)
            q_ok = k_ok = None
        else:
            pair = same_block & ((row & (size - 1)) >= half) & ((col & (size - 1)) < half)
            pos = lax.broadcasted_iota(I32, (ts, dk), 0) & (size - 1)
            q_ok = pos >= half
            k_ok = pos < half
        ref = jnp.concatenate(
            [jnp.broadcast_to(bcum[r0 + half:r0 + half + 1, :], (size, dk)) for r0 in range(0, ts, size)], axis=0)
        ql = q * jnp.exp(bcum - ref)
        kl = k * jnp.exp(ref - bcum)
        if q_ok is not None:
            ql = jnp.where(q_ok, ql, 0.0)
            kl = jnp.where(k_ok, kl, 0.0)
        ql = ql.astype(BF16)
        kl = kl.astype(BF16)
        for hd in range(GLA_HEADS):
            c0 = hd * dkh
            a = _dot_nt(ql[:, c0:c0 + dkh], kl[:, c0:c0 + dkh])
            a = jnp.where(pair, a, 0.0)
            scores[hd] = a if scores[hd] is None else scores[hd] + a

    outs = []
    for hd in range(GLA_HEADS):
        c0 = hd * dkh
        v_h = v[:, hd * dvh:(hd + 1) * dvh]
        st = state_ref[hd]
        o_h = _dot(scores[hd].astype(BF16), v_h) + _dot_nt(q_in[:, c0:c0 + dkh], st.astype(BF16))
        decay = jnp.exp(b_last[:, c0:c0 + dkh])
        state_ref[hd] = st * decay + _dot_tn(v_h, k_out[:, c0:c0 + dkh])
        o_h = o_h * lax.rsqrt(jnp.mean(o_h * o_h, axis=-1, keepdims=True) + EPS) * ng_ref[...]
        outs.append(o_h)
    o = jnp.concatenate(outs, axis=1) * out_gate
    o_ref[0] = x + _dot(o.astype(BF16), wo_ref[...])


def _gla_mixer(x, gates, y0, y1, g, w_in, w_a2, b_a, norm_g, w_o):
    b, s, d = x.shape
    dk = w_a2.shape[1]
    ts = min(TILE_GLA, s)
    assert s % ts == 0 and ts % GLA_LEAF == 0
    nj = s // ts
    row = lambda v: v.reshape(1, -1)
    const = lambda shape: pl.BlockSpec(shape, lambda j: (0,) * len(shape))
    wq = w_in[:, :dk].astype(BF16)
    wk = w_in[:, dk:2 * dk].astype(BF16)
    wv = w_in[:, 2 * dk:2 * dk + d].astype(BF16)
    wa = jnp.pad(w_in[:, 2 * dk + d:2 * dk + d + GLA_RANK], ((0, 0), (0, GLA_RANK_PAD - GLA_RANK))).astype(BF16)
    wr = w_in[:, 2 * dk + d + GLA_RANK:].astype(BF16)
    wa2 = jnp.pad(w_a2, ((0, GLA_RANK_PAD - GLA_RANK), (0, 0)))
    wa2_hi = wa2.astype(BF16)
    wa2_lo = (wa2 - wa2_hi.astype(F32)).astype(BF16)
    wa2_split = jnp.concatenate([wa2_hi, wa2_lo], axis=0)
    dvh = d // GLA_HEADS
    n = b * nj

    def tile(j, lag):
        return jnp.clip(j - lag, 0, n - 1)

    tokens = lambda lag: (lambda j: (tile(j, lag), 0))
    return pl.pallas_call(
        functools.partial(_gla_kernel, tiles_per_seq=nj),
        out_shape=jax.ShapeDtypeStruct(x.shape, F32),
        grid=(n + 1,),
        in_specs=[
            pl.BlockSpec((1, ts, d), lambda j: (tile(j, 0) // nj, tile(j, 0) % nj, 0)),
            pl.BlockSpec((ROUTE_ROWS, ts), lambda j: (0, tile(j, 0))),
            pl.BlockSpec((ts, d // 2), tokens(0)),
            pl.BlockSpec((ts, d // 2), tokens(0)),
            const((1, d)), const((d, dk)), const((d, dk)), const((d, d)), const((d, GLA_RANK_PAD)),
            const((d, d)), const((2 * GLA_RANK_PAD, dk)), const((1, dk)), const((1, dvh)), const((d, d)),
        ],
        out_specs=pl.BlockSpec((1, ts, d), lambda j: (tile(j, 1) // nj, tile(j, 1) % nj, 0)),
        scratch_shapes=[pltpu.VMEM((GLA_HEADS, dvh, dk // GLA_HEADS), F32),
                        pltpu.VMEM((2, ts, d), F32), pltpu.VMEM((2, ts, dk), F32), pltpu.VMEM((2, ts, dk), F32),
                        pltpu.VMEM((2, ts, d), BF16), pltpu.VMEM((2, ts, d), F32), pltpu.VMEM((2, ts, dk), F32)],
        compiler_params=_cparams(("arbitrary",)),
        name="gla_mixer",
    )(x, gates, y0, y1, row(g), wq, wk, wv, wa, wr, wa2_split, row(b_a), row(norm_g), w_o.astype(BF16))


def _xattn_router_kernel(x_ref, gx_ref, wq_ref, k_ref, v_ref, wo_ref, gf_ref, wr_ref, br_ref, upper_ref,
                         x_out_ref, h_out_ref, route_ref, cnt_ref, carry_ref):
    ts, d = x_ref.shape[1], x_ref.shape[2]
    hd = d // XATTN_HEADS
    first = (pl.program_id(0) == 0) & (pl.program_id(1) == 0)

    @pl.when(first)
    def _():
        carry_ref[...] = jnp.zeros(carry_ref.shape, F32)

    x = x_ref[0]
    h = _rms(x, gx_ref[...]).astype(BF16)
    q = (_dot(h, wq_ref[...]) * (hd ** -0.5)).astype(BF16)
    k = k_ref[0, 0]
    v = v_ref[0, 0]
    outs = []
    for a in range(XATTN_HEADS):
        sl = slice(a * hd, (a + 1) * hd)
        s = _dot_nt(q[:, sl], k[:, sl])
        p = jnp.exp(s - jnp.max(s, axis=-1, keepdims=True))
        o = _dot(p.astype(BF16), v[:, sl]) / jnp.sum(p, axis=-1, keepdims=True)
        outs.append(o)
    att = jnp.concatenate(outs, axis=1).astype(BF16)
    x2 = x + _dot(att, wo_ref[...])
    x_out_ref[0] = x2

    hf = _rms(x2, gf_ref[...])
    h_out_ref[...] = _pack_bf16_pairs(hf)

    both = _dot_nt(wr_ref[...], hf.astype(BF16))
    logits = both[0:ROUTER_ROWS, :] + both[ROUTER_ROWS:2 * ROUTER_ROWS, :] + br_ref[...]
    gl = logits[N_EXPERTS:N_EXPERTS + N_GROUPS, :]
    gi = lax.broadcasted_iota(I32, gl.shape, 0).astype(F32)
    gmax = jnp.max(gl, axis=0, keepdims=True)
    g_sel = jnp.min(jnp.where(gl == gmax, gi, float(N_GROUPS)), axis=0, keepdims=True)
    pg_sel = 1.0 / jnp.sum(jnp.exp(gl - gmax), axis=0, keepdims=True)

    el = jnp.zeros((EXPERTS_PER_GROUP, ts), F32)
    for gidx in range(N_GROUPS):
        lo = gidx * EXPERTS_PER_GROUP
        el = jnp.where(g_sel == float(gidx), logits[lo:lo + EXPERTS_PER_GROUP, :], el)
    ei = lax.broadcasted_iota(I32, el.shape, 0).astype(F32)
    m1 = jnp.max(el, axis=0, keepdims=True)
    i1 = jnp.min(jnp.where(el == m1, ei, float(EXPERTS_PER_GROUP)), axis=0, keepdims=True)
    rest = jnp.where(ei == i1, -jnp.inf, el)
    m2 = jnp.max(rest, axis=0, keepdims=True)
    i2 = jnp.min(jnp.where(rest == m2, ei, float(EXPERTS_PER_GROUP)), axis=0, keepdims=True)
    ratio = jnp.exp(m2 - m1)
    gate1 = pg_sel / (1.0 + ratio)
    gate2 = pg_sel * ratio / (1.0 + ratio)
    e1 = g_sel * float(EXPERTS_PER_GROUP) + i1
    e2 = g_sel * float(EXPERTS_PER_GROUP) + i2

    xi = lax.broadcasted_iota(I32, (N_EXPERTS, ts), 0).astype(F32)
    oh1 = jnp.where(xi == e1, 1.0, 0.0)
    oh2 = jnp.where(xi == e2, 1.0, 0.0)
    oh = oh1 + oh2
    n_blk = ts // LANES
    stacked = jnp.concatenate([oh[:, c * LANES:(c + 1) * LANES] for c in range(n_blk)], axis=0)
    within = _dot(stacked.astype(BF16), upper_ref[...])
    totals = jnp.sum(stacked, axis=1, keepdims=True)
    run = carry_ref[...]
    before = []
    for c in range(n_blk):
        before.append(within[c * N_EXPERTS:(c + 1) * N_EXPERTS, :] + run)
        run = run + totals[c * N_EXPERTS:(c + 1) * N_EXPERTS, :]
    before = jnp.concatenate(before, axis=1)
    rank1 = jnp.sum(oh1 * before, axis=0, keepdims=True)
    rank2 = jnp.sum(oh2 * before, axis=0, keepdims=True)
    carry_ref[...] = run
    cnt_ref[...] = jnp.broadcast_to(run, cnt_ref.shape)

    route_ref[0:1, :] = e1.astype(I32)
    route_ref[1:2, :] = e2.astype(I32)
    route_ref[2:3, :] = rank1.astype(I32)
    route_ref[3:4, :] = rank2.astype(I32)
    route_ref[4:5, :] = lax.bitcast_convert_type(gate1, I32)
    route_ref[5:6, :] = lax.bitcast_convert_type(gate2, I32)
    route_ref[3 * TOP_K:ROUTE_ROWS, :] = jnp.zeros((ROUTE_ROWS - 3 * TOP_K, ts), I32)


def _xattn_router(x, g_x, w_q, k_mem, v_mem, layer, w_o, g_f, w_grp, b_grp, w_exp, b_exp):
    b, s, d = x.shape
    nm = k_mem.shape[2]
    t = b * s
    ts = min(TILE_XATTN, s)
    assert s % ts == 0 and ts % LANES == 0
    nj = s // ts
    row = lambda v: v.reshape(1, -1)
    const = lambda shape: pl.BlockSpec(shape, lambda i, j: (0,) * len(shape))
    pad = ROUTER_ROWS - N_GROUPS - N_EXPERTS
    w_r = jnp.pad(jnp.concatenate([w_exp, w_grp], axis=1).T, ((0, pad), (0, 0)))
    w_r_hi = w_r.astype(BF16)
    w_r_split = jnp.concatenate([w_r_hi, (w_r - w_r_hi.astype(F32)).astype(BF16)], axis=0)
    b_r = jnp.pad(jnp.concatenate([b_exp, b_grp]), (0, pad)).reshape(ROUTER_ROWS, 1)
    ti = jnp.arange(LANES)
    upper = (ti[:, None] < ti[None, :]).astype(BF16)
    return pl.pallas_call(
        _xattn_router_kernel,
        out_shape=(
            jax.ShapeDtypeStruct(x.shape, F32),
            jax.ShapeDtypeStruct((t, d // 2), U32),
            jax.ShapeDtypeStruct((ROUTE_ROWS, t), I32),
            jax.ShapeDtypeStruct((N_EXPERTS, LANES), F32),
        ),
        grid=(b, nj),
        in_specs=[
            pl.BlockSpec((1, ts, d), lambda i, j: (i, j, 0)),
            const((1, d)), const((d, d)),
            pl.BlockSpec((1, 1, nm, d), lambda i, j: (layer, i, 0, 0)),
            pl.BlockSpec((1, 1, nm, d), lambda i, j: (layer, i, 0, 0)),
            const((d, d)), const((1, d)), const((2 * ROUTER_ROWS, d)), const((ROUTER_ROWS, 1)), const((LANES, LANES)),
        ],
        out_specs=(
            pl.BlockSpec((1, ts, d), lambda i, j: (i, j, 0)),
            pl.BlockSpec((ts, d // 2), lambda i, j: (i * nj + j, 0)),
            pl.BlockSpec((ROUTE_ROWS, ts), lambda i, j: (0, i * nj + j)),
            pl.BlockSpec((N_EXPERTS, LANES), lambda i, j: (0, 0)),
        ),
        scratch_shapes=[pltpu.VMEM((N_EXPERTS, 1), F32)],
        compiler_params=_cparams(("arbitrary", "arbitrary")),
        name="xattn_router",
    )(x, row(g_x), w_q.astype(BF16), k_mem, v_mem, w_o.astype(BF16), row(g_f), w_r_split, b_r, upper)


def _sc_mesh():
    return plsc.VectorSubcoreMesh(core_axis_name="c", subcore_axis_name="s",
                                  num_cores=SC_CORES, num_subcores=SC_SUBCORES)


def _sc_worker():
    return lax.axis_index("s") * SC_CORES + lax.axis_index("c")


def _dispatch(h_packed, dest, n_rows):
    t, w = h_packed.shape
    chunk = SC_DISPATCH_CHUNK
    per_worker = t // SC_WORKERS
    n_chunks = per_worker // chunk
    assert n_chunks % 2 == 0 and n_chunks * chunk * SC_WORKERS == t
    dest = dest.reshape(TOP_K, SC_WORKERS * n_chunks, chunk)
    rows_buf = pltpu.VMEM((chunk, w), U32)

    @functools.partial(
        pl.kernel, mesh=_sc_mesh(),
        out_type=jax.ShapeDtypeStruct((n_rows, w), U32),
        scratch_types=[pltpu.VMEM((n_chunks, chunk), I32), pltpu.VMEM((n_chunks, chunk), I32), rows_buf, rows_buf,
                       pltpu.SemaphoreType.DMA((2,)), pltpu.SemaphoreType.DMA((2, TOP_K))],
        name="moe_dispatch_sc",
    )
    def run(h_hbm, d0_hbm, d1_hbm, xbuf_hbm, idx0_v, idx1_v, buf_a, buf_b, read_sem, write_sem):
        wid = _sc_worker()
        pltpu.sync_copy(d0_hbm.at[pl.ds(wid * n_chunks, n_chunks)], idx0_v)
        pltpu.sync_copy(d1_hbm.at[pl.ds(wid * n_chunks, n_chunks)], idx1_v)

        @pl.loop(0, n_chunks, step=2)
        def _(i):
            reads = [pltpu.async_copy(h_hbm.at[pl.ds(wid * per_worker + (i + j) * chunk, chunk)], buf, read_sem.at[j])
                     for j, buf in enumerate((buf_a, buf_b))]
            writes = []
            for j, buf in enumerate((buf_a, buf_b)):
                reads[j].wait()
                writes.append(pltpu.async_copy(buf, xbuf_hbm.at[idx0_v.at[i + j]], write_sem.at[j, 0]))
                writes.append(pltpu.async_copy(buf, xbuf_hbm.at[idx1_v.at[i + j]], write_sem.at[j, 1]))
            for copy in writes:
                copy.wait()

    return run(h_packed, dest[0], dest[1])


def _gather_pairs(y_buf, dest):
    t = dest.shape[1]
    w = y_buf.shape[1]
    chunk = SC_GATHER_CHUNK
    per_worker = t // SC_WORKERS
    n_chunks = per_worker // chunk
    assert n_chunks % 2 == 0 and n_chunks * chunk * SC_WORKERS == t
    dest = dest.reshape(TOP_K, SC_WORKERS * n_chunks, chunk)
    out = jax.ShapeDtypeStruct((t, w), U32)
    rows_buf = pltpu.VMEM((chunk, w), U32)

    @functools.partial(
        pl.kernel, mesh=_sc_mesh(),
        out_type=(out, out),
        scratch_types=[pltpu.VMEM((n_chunks, chunk), I32), pltpu.VMEM((n_chunks, chunk), I32),
                       rows_buf, rows_buf, rows_buf, rows_buf, pltpu.SemaphoreType.DMA((2, TOP_K))],
        name="moe_gather_sc",
    )
    def run(y_hbm, d0_hbm, d1_hbm, y0_hbm, y1_hbm, idx0_v, idx1_v, buf_a0, buf_a1, buf_b0, buf_b1, sem):
        wid = _sc_worker()
        pltpu.sync_copy(d0_hbm.at[pl.ds(wid * n_chunks, n_chunks)], idx0_v)
        pltpu.sync_copy(d1_hbm.at[pl.ds(wid * n_chunks, n_chunks)], idx1_v)

        @pl.loop(0, n_chunks, step=2)
        def _(i):
            bufs = ((buf_a0, buf_a1), (buf_b0, buf_b1))
            gathers = [[pltpu.async_copy(y_hbm.at[idx_v.at[i + j]], bufs[j][k], sem.at[j, k])
                        for k, idx_v in enumerate((idx0_v, idx1_v))] for j in range(2)]
            stores = []
            for j in range(2):
                rows = pl.ds(wid * per_worker + (i + j) * chunk, chunk)
                for k, out_hbm in enumerate((y0_hbm, y1_hbm)):
                    gathers[j][k].wait()
                    stores.append(pltpu.async_copy(bufs[j][k], out_hbm.at[rows], sem.at[j, k]))
            for copy in stores:
                copy.wait()

    return run(y_buf, dest[0], dest[1])


def _expert_kernel(be_ref, bf_ref, bv_ref, slot_ref, next_ref, x_ref, wg_hbm, wu_hbm, wd_hbm, y_ref,
                   wg_f32, wu_f32, wd_f32, wg_bf, wu_bf, wd_bf, sems, *, layer):
    blk = pl.program_id(0)
    valid = bv_ref[blk]

    def weight_copies(expert, slot):
        return [pltpu.make_async_copy(src.at[layer, expert], dst.at[slot], sems.at[slot, i])
                for i, (src, dst) in enumerate(((wg_hbm, wg_f32), (wu_hbm, wu_f32), (wd_hbm, wd_f32)))]

    @pl.when(blk == 0)
    def _():
        for copy in weight_copies(be_ref[0], 0):
            copy.start()

    @pl.when(bf_ref[blk] == 1)
    def _():
        slot = slot_ref[blk]
        for copy in weight_copies(be_ref[blk], slot):
            copy.wait()

        @pl.when(next_ref[blk] >= 0)
        def _():
            for copy in weight_copies(next_ref[blk], 1 - slot):
                copy.start()

        wg_bf[...] = wg_f32[slot].astype(BF16)
        wu_bf[...] = wu_f32[slot].astype(BF16)
        wd_bf[...] = wd_f32[slot].astype(BF16)

    @pl.when(valid > 0)
    def _():
        live = lax.broadcasted_iota(I32, x_ref.shape, 0) < valid
        xb = _unpack_bf16_pairs(jnp.where(live, x_ref[...], jnp.uint32(0))).astype(BF16)
        gt = _dot(xb, wg_bf[...])
        up = _dot(xb, wu_bf[...])
        act = (gt * _sigmoid(gt) * up).astype(BF16)
        y_ref[...] = _pack_bf16_pairs(_dot(act, wd_bf[...]))

    @pl.when(valid <= 0)
    def _():
        y_ref[...] = jnp.zeros(y_ref.shape, U32)


def _experts(x_buf, block_expert, block_first, block_valid, w_gate, w_up, w_down, layer):
    n_rows, w = x_buf.shape
    d, de = w_gate.shape[2], w_gate.shape[3]
    bm = MOE_BLOCK_ROWS
    n = n_rows // bm
    block_slot = (jnp.cumsum(block_first) - 1) % 2
    idx = jnp.arange(n, dtype=I32)
    later_first = jnp.concatenate([jnp.where(block_first[1:] == 1, idx[1:], n), jnp.full((1,), n, I32)])
    next_first = lax.cummin(later_first, reverse=True)
    block_next = jnp.where(next_first < n, block_expert[jnp.minimum(next_first, n - 1)], -1).astype(I32)
    any_space = pl.BlockSpec(memory_space=pl.ANY)
    grid_spec = pltpu.PrefetchScalarGridSpec(
        num_scalar_prefetch=5,
        grid=(n,),
        in_specs=[pl.BlockSpec((bm, w), lambda i, *_: (i, 0)), any_space, any_space, any_space],
        out_specs=pl.BlockSpec((bm, w), lambda i, *_: (i, 0)),
        scratch_shapes=[pltpu.VMEM((2, d, de), F32), pltpu.VMEM((2, d, de), F32), pltpu.VMEM((2, de, d), F32),
                        pltpu.VMEM((d, de), BF16), pltpu.VMEM((d, de), BF16), pltpu.VMEM((de, d), BF16),
                        pltpu.SemaphoreType.DMA((2, 3))],
    )
    return pl.pallas_call(
        functools.partial(_expert_kernel, layer=layer),
        out_shape=jax.ShapeDtypeStruct((n_rows, w), U32),
        grid_spec=grid_spec,
        compiler_params=_cparams(("arbitrary",)),
        name="moe_experts",
    )(block_expert, block_first, block_valid, block_slot.astype(I32), block_next, x_buf, w_gate, w_up, w_down)


def _combine_kernel(x_ref, gate_ref, y0_ref, y1_ref, gfin_ref, *rest, final_norm):
    o_ref = rest[-1]
    out = _moe_residual(x_ref[...], gate_ref[...], y0_ref[...], y1_ref[...])
    if final_norm:
        out = _rms(out, gfin_ref[...])
    o_ref[...] = out


def _combine(x2, y0, y1, gates, g_final, final_norm, first_row=0, out_so_far=None):
    t, d = x2.shape
    rows, w = y0.shape
    ts = min(TILE_COMBINE, rows)
    assert rows % ts == 0 and first_row % ts == 0
    off = first_row // ts
    in_specs = [
        pl.BlockSpec((ts, d), lambda i: (i + off, 0)),
        pl.BlockSpec((ROUTE_ROWS, ts), lambda i: (0, i + off)),
        pl.BlockSpec((ts, w), lambda i: (i, 0)),
        pl.BlockSpec((ts, w), lambda i: (i, 0)),
        pl.BlockSpec((1, d), lambda i: (0, 0)),
    ]
    args = [x2, gates, y0, y1, g_final.reshape(1, d)]
    aliases = {}
    if out_so_far is not None:
        in_specs.append(pl.BlockSpec(memory_space=pl.ANY))
        args.append(out_so_far)
        aliases = {len(args) - 1: 0}
    return pl.pallas_call(
        functools.partial(_combine_kernel, final_norm=final_norm),
        out_shape=jax.ShapeDtypeStruct((t, d), F32),
        grid=(rows // ts,),
        in_specs=in_specs,
        out_specs=pl.BlockSpec((ts, d), lambda i: (i + off, 0)),
        input_output_aliases=aliases,
        compiler_params=_cparams(("arbitrary",)),
        name="moe_combine",
    )(*args)


def _moe_layout(route, counts):
    bm = MOE_BLOCK_ROWS
    t = route.shape[1]
    assert (t * TOP_K) % bm == 0
    n_blocks = (t * TOP_K) // bm + N_EXPERTS
    cnt = counts[:, 0].astype(I32)
    padded = (cnt + bm - 1) // bm * bm
    pad_ends = jnp.cumsum(padded)
    pad_off = pad_ends - padded
    experts = jnp.arange(N_EXPERTS, dtype=I32)
    hit = route[0:TOP_K, :, None] == experts
    dest = jnp.sum(jnp.where(hit, pad_off, 0), axis=-1) + route[TOP_K:2 * TOP_K]
    gates = route
    starts = jnp.arange(n_blocks, dtype=I32) * bm
    block_expert = jnp.minimum(jnp.sum((pad_ends[None, :] <= starts[:, None]).astype(I32), axis=1),
                               N_EXPERTS - 1)
    block_first = jnp.concatenate([jnp.ones((1,), I32), (block_expert[1:] != block_expert[:-1]).astype(I32)])
    own = block_expert[:, None] == experts
    block_valid = jnp.clip(jnp.sum(jnp.where(own, cnt + pad_off, 0), axis=1) - starts, 0, bm)
    block_valid = jnp.where(starts < pad_ends[-1], block_valid, 0).astype(I32)
    return dest, gates, block_expert, block_first, block_valid, n_blocks * bm


def kernel(x, mem, norm_mix, norm_xattn, norm_ffn, norm_mem, norm_final, conv_w_in, conv_b_in, conv_w_dw,
           conv_b_dw, conv_ln_g, conv_ln_b, conv_w_out, conv_b_out, gla_w_in, gla_w_a2, gla_b_a, gla_norm_g,
           gla_w_o, xa_w_q, xa_w_kv, xa_w_o, moe_w_grp, moe_b_grp, moe_w_exp, moe_b_exp, moe_w_gate, moe_w_up,
           moe_w_down):
    b, s, d = x.shape
    depth = norm_mix.shape[0]
    k_mem, v_mem = _mem_kv(mem, norm_mem, xa_w_kv)
    moe = None
    for i in range(depth):
        j = i // 2
        if i % 2 == 0:
            if moe is not None:
                x = _combine(x.reshape(b * s, d), moe[1], moe[2], moe[0], norm_final, False).reshape(b, s, d)
            x = _conv_mixer(x, norm_mix[i], conv_w_in[j], conv_b_in[j], conv_w_dw[j], conv_b_dw[j],
                            conv_ln_g[j], conv_ln_b[j], conv_w_out[j], conv_b_out[j])
        else:
            x = _gla_mixer(x, *moe, norm_mix[i], gla_w_in[j], gla_w_a2[j], gla_b_a[j], gla_norm_g[j], gla_w_o[j])
        x2, h_packed, route, counts = _xattn_router(
            x, norm_xattn[i], xa_w_q[i], k_mem, v_mem, i, xa_w_o[i], norm_ffn[i],
            moe_w_grp[i], moe_b_grp[i], moe_w_exp[i], moe_b_exp[i])
        dest, gates, block_expert, block_first, block_valid, n_rows = _moe_layout(route, counts)
        x_buf = _dispatch(h_packed, dest, n_rows)
        y_buf = _experts(x_buf, block_expert, block_first, block_valid, moe_w_gate, moe_w_up, moe_w_down, i)
        if i < depth - 1:
            y0, y1 = _gather_pairs(y_buf, dest)
            x, moe = x2, (gates, y0, y1)
    part = (b * s) // FINAL_PARTS
    out = None
    for p in range(FINAL_PARTS):
        y0, y1 = _gather_pairs(y_buf, dest[:, p * part:(p + 1) * part])
        out = _combine(x2.reshape(b * s, d), y0, y1, gates, norm_final, True, p * part, out)
    return out.reshape(b, s, d)
```

```python
import functools

import jax
import jax.numpy as jnp
from jax import lax
from jax.experimental import pallas as pl
from jax.experimental.pallas import tpu as pltpu
from jax.experimental.pallas import tpu_sc as plsc

F32 = jnp.float32
BF16 = jnp.bfloat16
I32 = jnp.int32
U32 = jnp.uint32

EPS = 1e-6
CONV_KERNEL = 31
CONV_CARRY = 32
CONV_ROWS = 64
CONV_COLS = 256
SUBLANES = 8
LANES = 128
GLA_HEADS = 4
GLA_RANK = 16
GLA_RANK_PAD = 128
GLA_TAU = 16.0
GLA_LEAF = 32
XATTN_HEADS = 4
N_GROUPS = 4
EXPERTS_PER_GROUP = 8
N_EXPERTS = N_GROUPS * EXPERTS_PER_GROUP
ROUTER_ROWS = 40
TOP_K = 2
ROUTE_ROWS = 8

TILE_CONV = 512
TILE_GLA = 256
TILE_XATTN = 1024
TILE_COMBINE = 512
FINAL_PARTS = 2
SC_CORES = 2
SC_SUBCORES = 16
SC_WORKERS = SC_CORES * SC_SUBCORES
SC_DISPATCH_CHUNK = 64
SC_GATHER_CHUNK = 32
MOE_BLOCK_ROWS = 512
VMEM_LIMIT = 56 * 1024 * 1024


def _cparams(sem):
    return pltpu.CompilerParams(dimension_semantics=sem, vmem_limit_bytes=VMEM_LIMIT)


def _rms(x, g):
    return x * lax.rsqrt(jnp.mean(x * x, axis=-1, keepdims=True) + EPS) * g


def _sigmoid(x):
    return 0.5 * jnp.tanh(0.5 * x) + 0.5


def _dot(a, b):
    return jnp.dot(a, b, preferred_element_type=F32)


def _dot_nt(a, b):
    return lax.dot_general(a, b, (((1,), (1,)), ((), ())), preferred_element_type=F32)


def _dot_tn(a, b):
    return lax.dot_general(a, b, (((0,), (0,)), ((), ())), preferred_element_type=F32)


def _pack_bf16_pairs(x):
    w = x.shape[1] // 2
    hi = lax.bitcast_convert_type(x[:, :w].astype(BF16).astype(F32), U32)
    lo = lax.bitcast_convert_type(x[:, w:].astype(BF16).astype(F32), U32)
    return hi | (lo >> 16)


def _unpack_bf16_pairs(p):
    hi = lax.bitcast_convert_type(p & jnp.uint32(0xFFFF0000), F32)
    lo = lax.bitcast_convert_type(p << 16, F32)
    return jnp.concatenate([hi, lo], axis=1)


def _memkv_kernel(mem_ref, g_ref, w_ref, k_ref, v_ref):
    d = mem_ref.shape[-1]
    mn = _rms(mem_ref[0], g_ref[...]).astype(BF16)
    kv = _dot(mn, w_ref[0])
    k_ref[0, 0] = kv[:, :d].astype(BF16)
    v_ref[0, 0] = kv[:, d:].astype(BF16)


def _mem_kv(mem, norm_mem, w_kv):
    b, nm, d = mem.shape
    depth = w_kv.shape[0]
    out = jax.ShapeDtypeStruct((depth, b, nm, d), BF16)
    return pl.pallas_call(
        _memkv_kernel,
        out_shape=(out, out),
        grid=(depth, b),
        in_specs=[
            pl.BlockSpec((1, nm, d), lambda l, i: (i, 0, 0)),
            pl.BlockSpec((1, d), lambda l, i: (0, 0)),
            pl.BlockSpec((1, d, 2 * d), lambda l, i: (l, 0, 0)),
        ],
        out_specs=(
            pl.BlockSpec((1, 1, nm, d), lambda l, i: (l, i, 0, 0)),
            pl.BlockSpec((1, 1, nm, d), lambda l, i: (l, i, 0, 0)),
        ),
        compiler_params=_cparams(("arbitrary", "arbitrary")),
        name="mem_kv",
    )(mem, norm_mem.reshape(1, d), w_kv.astype(BF16))


def _conv_kernel(x_ref, g_ref, win_ref, bin_ref, wdw_ref, bdw_ref, lng_ref, lnb_ref, wout_ref, bout_ref,
                 o_ref, ext_ref, conv_ref):
    ts, d = x_ref.shape[1], x_ref.shape[2]

    @pl.when(pl.program_id(1) == 0)
    def _():
        ext_ref[...] = jnp.zeros(ext_ref.shape, F32)

    x = x_ref[0]
    h = _rms(x, g_ref[...]).astype(BF16)
    for c0 in range(0, d, CONV_COLS):
        cols = slice(c0, c0 + CONV_COLS)
        gate_cols = slice(d + c0, d + c0 + CONV_COLS)
        val = _dot(h, win_ref[:, cols]) + bin_ref[:, cols]
        gate = _dot(h, win_ref[:, gate_cols]) + bin_ref[:, gate_cols]
        glu = val * _sigmoid(gate)
        for b in range(SUBLANES):
            ext_ref[b, CONV_CARRY - b:CONV_CARRY - b + ts, cols] = glu

    first = CONV_CARRY - (CONV_KERNEL - 1)

    def chunk(i, carry):
        r0 = pl.multiple_of(i * CONV_ROWS, CONV_ROWS)
        for c0 in range(0, d, CONV_COLS):
            cols = slice(c0, c0 + CONV_COLS)
            acc = [jnp.zeros((SUBLANES, CONV_COLS), F32) for _ in range(CONV_ROWS // SUBLANES)]
            for k in range(CONV_KERNEL):
                b = (first + k) % SUBLANES
                wk = wdw_ref[k * SUBLANES:(k + 1) * SUBLANES, cols]
                for j in range(CONV_ROWS // SUBLANES):
                    rows = pl.ds(r0 + (first + k - b) + j * SUBLANES, SUBLANES)
                    acc[j] = acc[j] + ext_ref[b, rows, cols] * wk
            for j in range(CONV_ROWS // SUBLANES):
                conv_ref[pl.ds(r0 + j * SUBLANES, SUBLANES), cols] = acc[j]
        return carry

    lax.fori_loop(0, ts // CONV_ROWS, chunk, 0)
    for b in range(SUBLANES):
        ext_ref[b, 0:CONV_CARRY, :] = ext_ref[b, ts:ts + CONV_CARRY, :]

    c = conv_ref[...] + bdw_ref[...]
    mu = jnp.mean(c, axis=-1, keepdims=True)
    cc = c - mu
    var = jnp.mean(cc * cc, axis=-1, keepdims=True)
    un = cc * lax.rsqrt(var + EPS) * lng_ref[...] + lnb_ref[...]
    act = (un * _sigmoid(un)).astype(BF16)
    o_ref[0] = x + _dot(act, wout_ref[...]) + bout_ref[...]


def _conv_mixer(x, g, w_in, b_in, w_dw, b_dw, ln_g, ln_b, w_out, b_out):
    b, s, d = x.shape
    ts = min(TILE_CONV, s)
    assert s % ts == 0 and ts % CONV_ROWS == 0 and d % CONV_COLS == 0
    row = lambda v: v.reshape(1, -1)
    const = lambda shape: pl.BlockSpec(shape, lambda i, j: (0,) * len(shape))
    return pl.pallas_call(
        _conv_kernel,
        out_shape=jax.ShapeDtypeStruct(x.shape, F32),
        grid=(b, s // ts),
        in_specs=[
            pl.BlockSpec((1, ts, d), lambda i, j: (i, j, 0)),
            const((1, d)), const((d, 2 * d)), const((1, 2 * d)), const((CONV_KERNEL * SUBLANES, d)), const((1, d)),
            const((1, d)), const((1, d)), const((d, d)), const((1, d)),
        ],
        out_specs=pl.BlockSpec((1, ts, d), lambda i, j: (i, j, 0)),
        scratch_shapes=[pltpu.VMEM((SUBLANES, CONV_CARRY + ts, d), F32), pltpu.VMEM((ts, d), F32)],
        compiler_params=_cparams(("arbitrary", "arbitrary")),
        name="conv_mixer",
    )(x, row(g), w_in.astype(BF16), row(b_in), jnp.repeat(w_dw, SUBLANES, axis=0), row(b_dw), row(ln_g), row(ln_b),
      w_out.astype(BF16), row(b_out))


def _gla_levels(ts):
    sizes = [GLA_LEAF]
    while sizes[-1] < ts:
        sizes.append(sizes[-1] * 2)
    return sizes


def _moe_residual(x, route, y0, y1):
    g = lax.bitcast_convert_type(route, F32)
    g = jnp.concatenate([g, jnp.zeros((LANES - g.shape[0], g.shape[1]), F32)], axis=0).T
    gate0 = g[:, 2 * TOP_K:2 * TOP_K + 1]
    gate1 = g[:, 2 * TOP_K + 1:2 * TOP_K + 2]
    return x + _unpack_bf16_pairs(y0) * gate0 + _unpack_bf16_pairs(y1) * gate1


def _gla_kernel(x_ref, gate_ref, y0_ref, y1_ref, g_ref, wq_ref, wk_ref, wv_ref, wa_ref, wr_ref, wa2_ref, ba_ref,
                ng_ref, wo_ref, o_ref, state_ref, x_s, q_s, k_s, v_s, og_s, la_s, *, tiles_per_seq):
    j = pl.program_id(0)
    dkh = wq_ref.shape[1] // GLA_HEADS

    @pl.when(j == 0)
    def _():
        for ref in (x_s, q_s, k_s, og_s, la_s):
            ref[...] = jnp.zeros(ref.shape, F32)
        v_s[...] = jnp.zeros(v_s.shape, BF16)

    @pl.when((j == 0) | ((j - 1) % tiles_per_seq == 0))
    def _():
        state_ref[...] = jnp.zeros(state_ref.shape, F32)

    def stage_a(slot):
        x = _moe_residual(x_ref[0], gate_ref[...], y0_ref[...], y1_ref[...])
        h = _rms(x, g_ref[...]).astype(BF16)
        x_s[slot] = x
        q_s[slot] = _dot(h, wq_ref[...]) * (dkh ** -0.5)
        k_s[slot] = _dot(h, wk_ref[...])
        v_s[slot] = _dot(h, wv_ref[...]).astype(BF16)
        r = _dot(h, wr_ref[...])
        og_s[slot] = r * _sigmoid(r)
        a = _dot(h, wa_ref[...]).astype(BF16)
        z = _dot(jnp.concatenate([a, a], axis=1), wa2_ref[...]) + ba_ref[...]
        la_s[slot] = -(jnp.maximum(-z, 0.0) + jnp.log(1.0 + jnp.exp(-jnp.abs(z)))) * (1.0 / GLA_TAU)

    def stage_b(slot):
        _gla_recurrence(x_s[slot], q_s[slot], k_s[slot], v_s[slot], og_s[slot], la_s[slot],
                        ng_ref, wo_ref, o_ref, state_ref)

    @pl.when(j % 2 == 0)
    def _():
        stage_b(1)
        stage_a(0)

    @pl.when(j % 2 == 1)
    def _():
        stage_b(0)
        stage_a(1)


def _gla_recurrence(x, q, k, v, out_gate, log_a, ng_ref, wo_ref, o_ref, state_ref):
    ts, d = x.shape
    dk = q.shape[1]
    dkh = dk // GLA_HEADS
    dvh = d // GLA_HEADS

    row = lax.broadcasted_iota(I32, (ts, ts), 0)
    col = lax.broadcasted_iota(I32, (ts, ts), 1)
    tri = jnp.where(col <= row, 1.0, 0.0).astype(BF16)
    bcum = _dot(tri, log_a.astype(BF16))
    b_last = bcum[ts - 1:ts, :]

    q_in = (q * jnp.exp(bcum)).astype(BF16)
    k_out = (k * jnp.exp(b_last - bcum)).astype(BF16)

    sizes = _gla_levels(ts)
    scores = [None] * GLA_HEADS
    for lvl, size in enumerate(sizes):
        half = size // 2
        same_block = (row & -size) == (col & -size)
        if lvl == 0:
            pair = same_block & (col <= row)
            q_ok = k_ok = None
        else:
            pair = same_block & ((row & (size - 1)) >= half) & ((col & (size - 1)) < half)
            pos = lax.broadcasted_iota(I32, (ts, dk), 0) & (size - 1)
            q_ok = pos >= half
            k_ok = pos < half
        ref = jnp.concatenate(
            [jnp.broadcast_to(bcum[r0 + half:r0 + half + 1, :], (size, dk)) for r0 in range(0, ts, size)], axis=0)
        ql = q * jnp.exp(bcum - ref)
        kl = k * jnp.exp(ref - bcum)
        if q_ok is not None:
            ql = jnp.where(q_ok, ql, 0.0)
            kl = jnp.where(k_ok, kl, 0.0)
        ql = ql.astype(BF16)
        kl = kl.astype(BF16)
        for hd in range(GLA_HEADS):
            c0 = hd * dkh
            a = _dot_nt(ql[:, c0:c0 + dkh], kl[:, c0:c0 + dkh])
            a = jnp.where(pair, a, 0.0)
            scores[hd] = a if scores[hd] is None else scores[hd] + a

    outs = []
    for hd in range(GLA_HEADS):
        c0 = hd * dkh
        v_h = v[:, hd * dvh:(hd + 1) * dvh]
        st = state_ref[hd]
        o_h = _dot(scores[hd].astype(BF16), v_h) + _dot_nt(q_in[:, c0:c0 + dkh], st.astype(BF16))
        decay = jnp.exp(b_last[:, c0:c0 + dkh])
        state_ref[hd] = st * decay + _dot_tn(v_h, k_out[:, c0:c0 + dkh])
        o_h = o_h * lax.rsqrt(jnp.mean(o_h * o_h, axis=-1, keepdims=True) + EPS) * ng_ref[...]
        outs.append(o_h)
    o = jnp.concatenate(outs, axis=1) * out_gate
    o_ref[0] = x + _dot(o.astype(BF16), wo_ref[...])


def _gla_mixer(x, gates, y0, y1, g, w_in, w_a2, b_a, norm_g, w_o):
    b, s, d = x.shape
    dk = w_a2.shape[1]
    ts = min(TILE_GLA, s)
    assert s % ts == 0 and ts % GLA_LEAF == 0
    nj = s // ts
    row = lambda v: v.reshape(1, -1)
    const = lambda shape: pl.BlockSpec(shape, lambda j: (0,) * len(shape))
    wq = w_in[:, :dk].astype(BF16)
    wk = w_in[:, dk:2 * dk].astype(BF16)
    wv = w_in[:, 2 * dk:2 * dk + d].astype(BF16)
    wa = jnp.pad(w_in[:, 2 * dk + d:2 * dk + d + GLA_RANK], ((0, 0), (0, GLA_RANK_PAD - GLA_RANK))).astype(BF16)
    wr = w_in[:, 2 * dk + d + GLA_RANK:].astype(BF16)
    wa2 = jnp.pad(w_a2, ((0, GLA_RANK_PAD - GLA_RANK), (0, 0)))
    wa2_hi = wa2.astype(BF16)
    wa2_lo = (wa2 - wa2_hi.astype(F32)).astype(BF16)
    wa2_split = jnp.concatenate([wa2_hi, wa2_lo], axis=0)
    dvh = d // GLA_HEADS
    n = b * nj

    def tile(j, lag):
        return jnp.clip(j - lag, 0, n - 1)

    tokens = lambda lag: (lambda j: (tile(j, lag), 0))
    return pl.pallas_call(
        functools.partial(_gla_kernel, tiles_per_seq=nj),
        out_shape=jax.ShapeDtypeStruct(x.shape, F32),
        grid=(n + 1,),
        in_specs=[
            pl.BlockSpec((1, ts, d), lambda j: (tile(j, 0) // nj, tile(j, 0) % nj, 0)),
            pl.BlockSpec((ROUTE_ROWS, ts), lambda j: (0, tile(j, 0))),
            pl.BlockSpec((ts, d // 2), tokens(0)),
            pl.BlockSpec((ts, d // 2), tokens(0)),
            const((1, d)), const((d, dk)), const((d, dk)), const((d, d)), const((d, GLA_RANK_PAD)),
            const((d, d)), const((2 * GLA_RANK_PAD, dk)), const((1, dk)), const((1, dvh)), const((d, d)),
        ],
        out_specs=pl.BlockSpec((1, ts, d), lambda j: (tile(j, 1) // nj, tile(j, 1) % nj, 0)),
        scratch_shapes=[pltpu.VMEM((GLA_HEADS, dvh, dk // GLA_HEADS), F32),
                        pltpu.VMEM((2, ts, d), F32), pltpu.VMEM((2, ts, dk), F32), pltpu.VMEM((2, ts, dk), F32),
                        pltpu.VMEM((2, ts, d), BF16), pltpu.VMEM((2, ts, d), F32), pltpu.VMEM((2, ts, dk), F32)],
        compiler_params=_cparams(("arbitrary",)),
        name="gla_mixer",
    )(x, gates, y0, y1, row(g), wq, wk, wv, wa, wr, wa2_split, row(b_a), row(norm_g), w_o.astype(BF16))


def _xattn_router_kernel(x_ref, gx_ref, wq_ref, k_ref, v_ref, wo_ref, gf_ref, wr_ref, br_ref, upper_ref,
                         x_out_ref, h_out_ref, route_ref, cnt_ref, carry_ref):
    ts, d = x_ref.shape[1], x_ref.shape[2]
    hd = d // XATTN_HEADS
    first = (pl.program_id(0) == 0) & (pl.program_id(1) == 0)

    @pl.when(first)
    def _():
        carry_ref[...] = jnp.zeros(carry_ref.shape, F32)

    x = x_ref[0]
    h = _rms(x, gx_ref[...]).astype(BF16)
    q = (_dot(h, wq_ref[...]) * (hd ** -0.5)).astype(BF16)
    k = k_ref[0, 0]
    v = v_ref[0, 0]
    outs = []
    for a in range(XATTN_HEADS):
        sl = slice(a * hd, (a + 1) * hd)
        s = _dot_nt(q[:, sl], k[:, sl])
        p = jnp.exp(s - jnp.max(s, axis=-1, keepdims=True))
        o = _dot(p.astype(BF16), v[:, sl]) / jnp.sum(p, axis=-1, keepdims=True)
        outs.append(o)
    att = jnp.concatenate(outs, axis=1).astype(BF16)
    x2 = x + _dot(att, wo_ref[...])
    x_out_ref[0] = x2

    hf = _rms(x2, gf_ref[...])
    h_out_ref[...] = _pack_bf16_pairs(hf)

    both = _dot_nt(wr_ref[...], hf.astype(BF16))
    logits = both[0:ROUTER_ROWS, :] + both[ROUTER_ROWS:2 * ROUTER_ROWS, :] + br_ref[...]
    gl = logits[N_EXPERTS:N_EXPERTS + N_GROUPS, :]
    gi = lax.broadcasted_iota(I32, gl.shape, 0).astype(F32)
    gmax = jnp.max(gl, axis=0, keepdims=True)
    g_sel = jnp.min(jnp.where(gl == gmax, gi, float(N_GROUPS)), axis=0, keepdims=True)
    pg_sel = 1.0 / jnp.sum(jnp.exp(gl - gmax), axis=0, keepdims=True)

    el = jnp.zeros((EXPERTS_PER_GROUP, ts), F32)
    for gidx in range(N_GROUPS):
        lo = gidx * EXPERTS_PER_GROUP
        el = jnp.where(g_sel == float(gidx), logits[lo:lo + EXPERTS_PER_GROUP, :], el)
    ei = lax.broadcasted_iota(I32, el.shape, 0).astype(F32)
    m1 = jnp.max(el, axis=0, keepdims=True)
    i1 = jnp.min(jnp.where(el == m1, ei, float(EXPERTS_PER_GROUP)), axis=0, keepdims=True)
    rest = jnp.where(ei == i1, -jnp.inf, el)
    m2 = jnp.max(rest, axis=0, keepdims=True)
    i2 = jnp.min(jnp.where(rest == m2, ei, float(EXPERTS_PER_GROUP)), axis=0, keepdims=True)
    ratio = jnp.exp(m2 - m1)
    gate1 = pg_sel / (1.0 + ratio)
    gate2 = pg_sel * ratio / (1.0 + ratio)
    e1 = g_sel * float(EXPERTS_PER_GROUP) + i1
    e2 = g_sel * float(EXPERTS_PER_GROUP) + i2

    xi = lax.broadcasted_iota(I32, (N_EXPERTS, ts), 0).astype(F32)
    oh1 = jnp.where(xi == e1, 1.0, 0.0)
    oh2 = jnp.where(xi == e2, 1.0, 0.0)
    oh = oh1 + oh2
    n_blk = ts // LANES
    stacked = jnp.concatenate([oh[:, c * LANES:(c + 1) * LANES] for c in range(n_blk)], axis=0)
    within = _dot(stacked.astype(BF16), upper_ref[...])
    totals = jnp.sum(stacked, axis=1, keepdims=True)
    run = carry_ref[...]
    before = []
    for c in range(n_blk):
        before.append(within[c * N_EXPERTS:(c + 1) * N_EXPERTS, :] + run)
        run = run + totals[c * N_EXPERTS:(c + 1) * N_EXPERTS, :]
    before = jnp.concatenate(before, axis=1)
    rank1 = jnp.sum(oh1 * before, axis=0, keepdims=True)
    rank2 = jnp.sum(oh2 * before, axis=0, keepdims=True)
    carry_ref[...] = run
    cnt_ref[...] = jnp.broadcast_to(run, cnt_ref.shape)

    route_ref[0:1, :] = e1.astype(I32)
    route_ref[1:2, :] = e2.astype(I32)
    route_ref[2:3, :] = rank1.astype(I32)
    route_ref[3:4, :] = rank2.astype(I32)
    route_ref[4:5, :] = lax.bitcast_convert_type(gate1, I32)
    route_ref[5:6, :] = lax.bitcast_convert_type(gate2, I32)
    route_ref[3 * TOP_K:ROUTE_ROWS, :] = jnp.zeros((ROUTE_ROWS - 3 * TOP_K, ts), I32)


def _xattn_router(x, g_x, w_q, k_mem, v_mem, layer, w_o, g_f, w_grp, b_grp, w_exp, b_exp):
    b, s, d = x.shape
    nm = k_mem.shape[2]
    t = b * s
    ts = min(TILE_XATTN, s)
    assert s % ts == 0 and ts % LANES == 0
    nj = s // ts
    row = lambda v: v.reshape(1, -1)
    const = lambda shape: pl.BlockSpec(shape, lambda i, j: (0,) * len(shape))
    pad = ROUTER_ROWS - N_GROUPS - N_EXPERTS
    w_r = jnp.pad(jnp.concatenate([w_exp, w_grp], axis=1).T, ((0, pad), (0, 0)))
    w_r_hi = w_r.astype(BF16)
    w_r_split = jnp.concatenate([w_r_hi, (w_r - w_r_hi.astype(F32)).astype(BF16)], axis=0)
    b_r = jnp.pad(jnp.concatenate([b_exp, b_grp]), (0, pad)).reshape(ROUTER_ROWS, 1)
    ti = jnp.arange(LANES)
    upper = (ti[:, None] < ti[None, :]).astype(BF16)
    return pl.pallas_call(
        _xattn_router_kernel,
        out_shape=(
            jax.ShapeDtypeStruct(x.shape, F32),
            jax.ShapeDtypeStruct((t, d // 2), U32),
            jax.ShapeDtypeStruct((ROUTE_ROWS, t), I32),
            jax.ShapeDtypeStruct((N_EXPERTS, LANES), F32),
        ),
        grid=(b, nj),
        in_specs=[
            pl.BlockSpec((1, ts, d), lambda i, j: (i, j, 0)),
            const((1, d)), const((d, d)),
            pl.BlockSpec((1, 1, nm, d), lambda i, j: (layer, i, 0, 0)),
            pl.BlockSpec((1, 1, nm, d), lambda i, j: (layer, i, 0, 0)),
            const((d, d)), const((1, d)), const((2 * ROUTER_ROWS, d)), const((ROUTER_ROWS, 1)), const((LANES, LANES)),
        ],
        out_specs=(
            pl.BlockSpec((1, ts, d), lambda i, j: (i, j, 0)),
            pl.BlockSpec((ts, d // 2), lambda i, j: (i * nj + j, 0)),
            pl.BlockSpec((ROUTE_ROWS, ts), lambda i, j: (0, i * nj + j)),
            pl.BlockSpec((N_EXPERTS, LANES), lambda i, j: (0, 0)),
        ),
        scratch_shapes=[pltpu.VMEM((N_EXPERTS, 1), F32)],
        compiler_params=_cparams(("arbitrary", "arbitrary")),
        name="xattn_router",
    )(x, row(g_x), w_q.astype(BF16), k_mem, v_mem, w_o.astype(BF16), row(g_f), w_r_split, b_r, upper)


def _sc_mesh():
    return plsc.VectorSubcoreMesh(core_axis_name="c", subcore_axis_name="s",
                                  num_cores=SC_CORES, num_subcores=SC_SUBCORES)


def _sc_worker():
    return lax.axis_index("s") * SC_CORES + lax.axis_index("c")


def _dispatch(h_packed, dest, n_rows):
    t, w = h_packed.shape
    chunk = SC_DISPATCH_CHUNK
    per_worker = t // SC_WORKERS
    n_chunks = per_worker // chunk
    assert n_chunks % 2 == 0 and n_chunks * chunk * SC_WORKERS == t
    dest = dest.reshape(TOP_K, SC_WORKERS * n_chunks, chunk)
    rows_buf = pltpu.VMEM((chunk, w), U32)

    @functools.partial(
        pl.kernel, mesh=_sc_mesh(),
        out_type=jax.ShapeDtypeStruct((n_rows, w), U32),
        scratch_types=[pltpu.VMEM((n_chunks, chunk), I32), pltpu.VMEM((n_chunks, chunk), I32), rows_buf, rows_buf,
                       pltpu.SemaphoreType.DMA((2,)), pltpu.SemaphoreType.DMA((2, TOP_K))],
        name="moe_dispatch_sc",
    )
    def run(h_hbm, d0_hbm, d1_hbm, xbuf_hbm, idx0_v, idx1_v, buf_a, buf_b, read_sem, write_sem):
        wid = _sc_worker()
        pltpu.sync_copy(d0_hbm.at[pl.ds(wid * n_chunks, n_chunks)], idx0_v)
        pltpu.sync_copy(d1_hbm.at[pl.ds(wid * n_chunks, n_chunks)], idx1_v)

        @pl.loop(0, n_chunks, step=2)
        def _(i):
            reads = [pltpu.async_copy(h_hbm.at[pl.ds(wid * per_worker + (i + j) * chunk, chunk)], buf, read_sem.at[j])
                     for j, buf in enumerate((buf_a, buf_b))]
            writes = []
            for j, buf in enumerate((buf_a, buf_b)):
                reads[j].wait()
                writes.append(pltpu.async_copy(buf, xbuf_hbm.at[idx0_v.at[i + j]], write_sem.at[j, 0]))
                writes.append(pltpu.async_copy(buf, xbuf_hbm.at[idx1_v.at[i + j]], write_sem.at[j, 1]))
            for copy in writes:
                copy.wait()

    return run(h_packed, dest[0], dest[1])


def _gather_pairs(y_buf, dest):
    t = dest.shape[1]
    w = y_buf.shape[1]
    chunk = SC_GATHER_CHUNK
    per_worker = t // SC_WORKERS
    n_chunks = per_worker // chunk
    assert n_chunks % 2 == 0 and n_chunks * chunk * SC_WORKERS == t
    dest = dest.reshape(TOP_K, SC_WORKERS * n_chunks, chunk)
    out = jax.ShapeDtypeStruct((t, w), U32)
    rows_buf = pltpu.VMEM((chunk, w), U32)

    @functools.partial(
        pl.kernel, mesh=_sc_mesh(),
        out_type=(out, out),
        scratch_types=[pltpu.VMEM((n_chunks, chunk), I32), pltpu.VMEM((n_chunks, chunk), I32),
                       rows_buf, rows_buf, rows_buf, rows_buf, pltpu.SemaphoreType.DMA((2, TOP_K))],
        name="moe_gather_sc",
    )
    def run(y_hbm, d0_hbm, d1_hbm, y0_hbm, y1_hbm, idx0_v, idx1_v, buf_a0, buf_a1, buf_b0, buf_b1, sem):
        wid = _sc_worker()
        pltpu.sync_copy(d0_hbm.at[pl.ds(wid * n_chunks, n_chunks)], idx0_v)
        pltpu.sync_copy(d1_hbm.at[pl.ds(wid * n_chunks, n_chunks)], idx1_v)

        @pl.loop(0, n_chunks, step=2)
        def _(i):
            bufs = ((buf_a0, buf_a1), (buf_b0, buf_b1))
            gathers = [[pltpu.async_copy(y_hbm.at[idx_v.at[i + j]], bufs[j][k], sem.at[j, k])
                        for k, idx_v in enumerate((idx0_v, idx1_v))] for j in range(2)]
            stores = []
            for j in range(2):
                rows = pl.ds(wid * per_worker + (i + j) * chunk, chunk)
                for k, out_hbm in enumerate((y0_hbm, y1_hbm)):
                    gathers[j][k].wait()
                    stores.append(pltpu.async_copy(bufs[j][k], out_hbm.at[rows], sem.at[j, k]))
            for copy in stores:
                copy.wait()

    return run(y_buf, dest[0], dest[1])


def _expert_kernel(be_ref, bf_ref, bv_ref, slot_ref, next_ref, x_ref, wg_hbm, wu_hbm, wd_hbm, y_ref,
                   wg_f32, wu_f32, wd_f32, wg_bf, wu_bf, wd_bf, sems, *, layer):
    blk = pl.program_id(0)
    valid = bv_ref[blk]

    def weight_copies(expert, slot):
        return [pltpu.make_async_copy(src.at[layer, expert], dst.at[slot], sems.at[slot, i])
                for i, (src, dst) in enumerate(((wg_hbm, wg_f32), (wu_hbm, wu_f32), (wd_hbm, wd_f32)))]

    @pl.when(blk == 0)
    def _():
        for copy in weight_copies(be_ref[0], 0):
            copy.start()

    @pl.when(bf_ref[blk] == 1)
    def _():
        slot = slot_ref[blk]
        for copy in weight_copies(be_ref[blk], slot):
            copy.wait()

        @pl.when(next_ref[blk] >= 0)
        def _():
            for copy in weight_copies(next_ref[blk], 1 - slot):
                copy.start()

        wg_bf[...] = wg_f32[slot].astype(BF16)
        wu_bf[...] = wu_f32[slot].astype(BF16)
        wd_bf[...] = wd_f32[slot].astype(BF16)

    @pl.when(valid > 0)
    def _():
        live = lax.broadcasted_iota(I32, x_ref.shape, 0) < valid
        xb = _unpack_bf16_pairs(jnp.where(live, x_ref[...], jnp.uint32(0))).astype(BF16)
        gt = _dot(xb, wg_bf[...])
        up = _dot(xb, wu_bf[...])
        act = (gt * _sigmoid(gt) * up).astype(BF16)
        y_ref[...] = _pack_bf16_pairs(_dot(act, wd_bf[...]))

    @pl.when(valid <= 0)
    def _():
        y_ref[...] = jnp.zeros(y_ref.shape, U32)


def _experts(x_buf, block_expert, block_first, block_valid, w_gate, w_up, w_down, layer):
    n_rows, w = x_buf.shape
    d, de = w_gate.shape[2], w_gate.shape[3]
    bm = MOE_BLOCK_ROWS
    n = n_rows // bm
    block_slot = (jnp.cumsum(block_first) - 1) % 2
    idx = jnp.arange(n, dtype=I32)
    later_first = jnp.concatenate([jnp.where(block_first[1:] == 1, idx[1:], n), jnp.full((1,), n, I32)])
    next_first = lax.cummin(later_first, reverse=True)
    block_next = jnp.where(next_first < n, block_expert[jnp.minimum(next_first, n - 1)], -1).astype(I32)
    any_space = pl.BlockSpec(memory_space=pl.ANY)
    grid_spec = pltpu.PrefetchScalarGridSpec(
        num_scalar_prefetch=5,
        grid=(n,),
        in_specs=[pl.BlockSpec((bm, w), lambda i, *_: (i, 0)), any_space, any_space, any_space],
        out_specs=pl.BlockSpec((bm, w), lambda i, *_: (i, 0)),
        scratch_shapes=[pltpu.VMEM((2, d, de), F32), pltpu.VMEM((2, d, de), F32), pltpu.VMEM((2, de, d), F32),
                        pltpu.VMEM((d, de), BF16), pltpu.VMEM((d, de), BF16), pltpu.VMEM((de, d), BF16),
                        pltpu.SemaphoreType.DMA((2, 3))],
    )
    return pl.pallas_call(
        functools.partial(_expert_kernel, layer=layer),
        out_shape=jax.ShapeDtypeStruct((n_rows, w), U32),
        grid_spec=grid_spec,
        compiler_params=_cparams(("arbitrary",)),
        name="moe_experts",
    )(block_expert, block_first, block_valid, block_slot.astype(I32), block_next, x_buf, w_gate, w_up, w_down)


def _combine_kernel(x_ref, gate_ref, y0_ref, y1_ref, gfin_ref, *rest, final_norm):
    o_ref = rest[-1]
    out = _moe_residual(x_ref[...], gate_ref[...], y0_ref[...], y1_ref[...])
    if final_norm:
        out = _rms(out, gfin_ref[...])
    o_ref[...] = out


def _combine(x2, y0, y1, gates, g_final, final_norm, first_row=0, out_so_far=None):
    t, d = x2.shape
    rows, w = y0.shape
    ts = min(TILE_COMBINE, rows)
    assert rows % ts == 0 and first_row % ts == 0
    off = first_row // ts
    in_specs = [
        pl.BlockSpec((ts, d), lambda i: (i + off, 0)),
        pl.BlockSpec((ROUTE_ROWS, ts), lambda i: (0, i + off)),
        pl.BlockSpec((ts, w), lambda i: (i, 0)),
        pl.BlockSpec((ts, w), lambda i: (i, 0)),
        pl.BlockSpec((1, d), lambda i: (0, 0)),
    ]
    args = [x2, gates, y0, y1, g_final.reshape(1, d)]
    aliases = {}
    if out_so_far is not None:
        in_specs.append(pl.BlockSpec(memory_space=pl.ANY))
        args.append(out_so_far)
        aliases = {len(args) - 1: 0}
    return pl.pallas_call(
        functools.partial(_combine_kernel, final_norm=final_norm),
        out_shape=jax.ShapeDtypeStruct((t, d), F32),
        grid=(rows // ts,),
        in_specs=in_specs,
        out_specs=pl.BlockSpec((ts, d), lambda i: (i + off, 0)),
        input_output_aliases=aliases,
        compiler_params=_cparams(("arbitrary",)),
        name="moe_combine",
    )(*args)


def _moe_layout(route, counts):
    bm = MOE_BLOCK_ROWS
    t = route.shape[1]
    assert (t * TOP_K) % bm == 0
    n_blocks = (t * TOP_K) // bm + N_EXPERTS
    cnt = counts[:, 0].astype(I32)
    padded = (cnt + bm - 1) // bm * bm
    pad_ends = jnp.cumsum(padded)
    pad_off = pad_ends - padded
    experts = jnp.arange(N_EXPERTS, dtype=I32)
    hit = route[0:TOP_K, :, None] == experts
    dest = jnp.sum(jnp.where(hit, pad_off, 0), axis=-1) + route[TOP_K:2 * TOP_K]
    gates = route
    starts = jnp.arange(n_blocks, dtype=I32) * bm
    block_expert = jnp.minimum(jnp.sum((pad_ends[None, :] <= starts[:, None]).astype(I32), axis=1),
                               N_EXPERTS - 1)
    block_first = jnp.concatenate([jnp.ones((1,), I32), (block_expert[1:] != block_expert[:-1]).astype(I32)])
    own = block_expert[:, None] == experts
    block_valid = jnp.clip(jnp.sum(jnp.where(own, cnt + pad_off, 0), axis=1) - starts, 0, bm)
    block_valid = jnp.where(starts < pad_ends[-1], block_valid, 0).astype(I32)
    return dest, gates, block_expert, block_first, block_valid, n_blocks * bm


def kernel(x, mem, norm_mix, norm_xattn, norm_ffn, norm_mem, norm_final, conv_w_in, conv_b_in, conv_w_dw,
           conv_b_dw, conv_ln_g, conv_ln_b, conv_w_out, conv_b_out, gla_w_in, gla_w_a2, gla_b_a, gla_norm_g,
           gla_w_o, xa_w_q, xa_w_kv, xa_w_o, moe_w_grp, moe_b_grp, moe_w_exp, moe_b_exp, moe_w_gate, moe_w_up,
           moe_w_down):
    b, s, d = x.shape
    depth = norm_mix.shape[0]
    k_mem, v_mem = _mem_kv(mem, norm_mem, xa_w_kv)
    moe = None
    for i in range(depth):
        j = i // 2
        if i % 2 == 0:
            if moe is not None:
                x = _combine(x.reshape(b * s, d), moe[1], moe[2], moe[0], norm_final, False).reshape(b, s, d)
            x = _conv_mixer(x, norm_mix[i], conv_w_in[j], conv_b_in[j], conv_w_dw[j], conv_b_dw[j],
                            conv_ln_g[j], conv_ln_b[j], conv_w_out[j], conv_b_out[j])
        else:
            x = _gla_mixer(x, *moe, norm_mix[i], gla_w_in[j], gla_w_a2[j], gla_b_a[j], gla_norm_g[j], gla_w_o[j])
        x2, h_packed, route, counts = _xattn_router(
            x, norm_xattn[i], xa_w_q[i], k_mem, v_mem, i, xa_w_o[i], norm_ffn[i],
            moe_w_grp[i], moe_b_grp[i], moe_w_exp[i], moe_b_exp[i])
        dest, gates, block_expert, block_first, block_valid, n_rows = _moe_layout(route, counts)
        x_buf = _dispatch(h_packed, dest, n_rows)
        y_buf = _experts(x_buf, block_expert, block_first, block_valid, moe_w_gate, moe_w_up, moe_w_down, i)
        if i < depth - 1:
            y0, y1 = _gather_pairs(y_buf, dest)
            x, moe = x2, (gates, y0, y1)
    part = (b * s) // FINAL_PARTS
    out = None
    for p in range(FINAL_PARTS):
        y0, y1 = _gather_pairs(y_buf, dest[:, p * part:(p + 1) * part])
        out = _combine(x2.reshape(b * s, d), y0, y1, gates, norm_final, True, p * part, out)
    return out.reshape(b, s, d)
```

```python
import functools

import jax
import jax.numpy as jnp
from jax import lax
from jax.experimental import pallas as pl
from jax.experimental.pallas import tpu as pltpu
from jax.experimental.pallas import tpu_sc as plsc

F32 = jnp.float32
BF16 = jnp.bfloat16
I32 = jnp.int32
U32 = jnp.uint32

EPS = 1e-6
CONV_KERNEL = 31
CONV_CARRY = 32
CONV_ROWS = 64
CONV_COLS = 256
SUBLANES = 8
LANES = 128
GLA_HEADS = 4
GLA_RANK = 16
GLA_RANK_PAD = 128
GLA_TAU = 16.0
GLA_LEAF = 32
XATTN_HEADS = 4
N_GROUPS = 4
EXPERTS_PER_GROUP = 8
N_EXPERTS = N_GROUPS * EXPERTS_PER_GROUP
ROUTER_ROWS = 40
TOP_K = 2
ROUTE_ROWS = 8

TILE_CONV = 512
TILE_GLA = 256
TILE_XATTN = 1024
TILE_COMBINE = 512
FINAL_PARTS = 2
SC_CORES = 2
SC_SUBCORES = 16
SC_WORKERS = SC_CORES * SC_SUBCORES
SC_DISPATCH_CHUNK = 64
SC_GATHER_CHUNK = 32
MOE_BLOCK_ROWS = 512
VMEM_LIMIT = 56 * 1024 * 1024


def _cparams(sem):
    return pltpu.CompilerParams(dimension_semantics=sem, vmem_limit_bytes=VMEM_LIMIT)


def _rms(x, g):
    return x * lax.rsqrt(jnp.mean(x * x, axis=-1, keepdims=True) + EPS) * g


def _sigmoid(x):
    return 0.5 * jnp.tanh(0.5 * x) + 0.5


def _dot(a, b):
    return jnp.dot(a, b, preferred_element_type=F32)


def _dot_nt(a, b):
    return lax.dot_general(a, b, (((1,), (1,)), ((), ())), preferred_element_type=F32)


def _dot_tn(a, b):
    return lax.dot_general(a, b, (((0,), (0,)), ((), ())), preferred_element_type=F32)


def _pack_bf16_pairs(x):
    w = x.shape[1] // 2
    hi = lax.bitcast_convert_type(x[:, :w].astype(BF16).astype(F32), U32)
    lo = lax.bitcast_convert_type(x[:, w:].astype(BF16).astype(F32), U32)
    return hi | (lo >> 16)


def _unpack_bf16_pairs(p):
    hi = lax.bitcast_convert_type(p & jnp.uint32(0xFFFF0000), F32)
    lo = lax.bitcast_convert_type(p << 16, F32)
    return jnp.concatenate([hi, lo], axis=1)


def _memkv_kernel(mem_ref, g_ref, wkv_ref, wq_ref, wo_ref, qk_ref, vo_ref):
    nm, d = mem_ref.shape[1], mem_ref.shape[2]
    hd = d // XATTN_HEADS
    mn = _rms(mem_ref[0], g_ref[...]).astype(BF16)
    kv = _dot(mn, wkv_ref[0])
    k = kv[:, :d].astype(BF16)
    v = kv[:, d:].astype(BF16)
    for a in range(XATTN_HEADS):
        sl = slice(a * hd, (a + 1) * hd)
        qk_ref[0, 0, :, a * nm:(a + 1) * nm] = (_dot_nt(wq_ref[0, :, sl], k[:, sl]) * (hd ** -0.5)).astype(BF16)
        vo_ref[0, 0, a * nm:(a + 1) * nm, :] = _dot(v[:, sl], wo_ref[0, sl, :]).astype(BF16)


def _mem_kv(mem, norm_mem, w_kv, w_q, w_o):
    b, nm, d = mem.shape
    depth = w_kv.shape[0]
    per_layer = lambda shape: pl.BlockSpec((1,) + shape, lambda l, i: (l, 0, 0))
    return pl.pallas_call(
        _memkv_kernel,
        out_shape=(jax.ShapeDtypeStruct((depth, b, d, XATTN_HEADS * nm), BF16),
                   jax.ShapeDtypeStruct((depth, b, XATTN_HEADS * nm, d), BF16)),
        grid=(depth, b),
        in_specs=[
            pl.BlockSpec((1, nm, d), lambda l, i: (i, 0, 0)),
            pl.BlockSpec((1, d), lambda l, i: (0, 0)),
            per_layer((d, 2 * d)), per_layer((d, d)), per_layer((d, d)),
        ],
        out_specs=(
            pl.BlockSpec((1, 1, d, XATTN_HEADS * nm), lambda l, i: (l, i, 0, 0)),
            pl.BlockSpec((1, 1, XATTN_HEADS * nm, d), lambda l, i: (l, i, 0, 0)),
        ),
        compiler_params=_cparams(("arbitrary", "arbitrary")),
        name="mem_kv",
    )(mem, norm_mem.reshape(1, d), w_kv.astype(BF16), w_q.astype(BF16), w_o.astype(BF16))


def _conv_kernel(x_ref, g_ref, win_ref, bin_ref, wdw_ref, bdw_ref, lng_ref, lnb_ref, wout_ref, bout_ref,
                 o_ref, ext_ref, conv_ref):
    ts, d = x_ref.shape[1], x_ref.shape[2]

    @pl.when(pl.program_id(1) == 0)
    def _():
        ext_ref[...] = jnp.zeros(ext_ref.shape, F32)

    x = x_ref[0]
    h = _rms(x, g_ref[...]).astype(BF16)
    u = _dot(h, win_ref[...]) + bin_ref[...]
    glu = u[:, :d] * _sigmoid(u[:, d:])
    for b in range(SUBLANES):
        ext_ref[b, CONV_CARRY - b:CONV_CARRY - b + ts, :] = glu

    first = CONV_CARRY - (CONV_KERNEL - 1)

    def chunk(i, carry):
        r0 = pl.multiple_of(i * CONV_ROWS, CONV_ROWS)
        for c0 in range(0, d, CONV_COLS):
            cols = slice(c0, c0 + CONV_COLS)
            acc = [jnp.zeros((SUBLANES, CONV_COLS), F32) for _ in range(CONV_ROWS // SUBLANES)]
            for k in range(CONV_KERNEL):
                b = (first + k) % SUBLANES
                wk = wdw_ref[k * SUBLANES:(k + 1) * SUBLANES, cols]
                for j in range(CONV_ROWS // SUBLANES):
                    rows = pl.ds(r0 + (first + k - b) + j * SUBLANES, SUBLANES)
                    acc[j] = acc[j] + ext_ref[b, rows, cols] * wk
            for j in range(CONV_ROWS // SUBLANES):
                conv_ref[pl.ds(r0 + j * SUBLANES, SUBLANES), cols] = acc[j]
        return carry

    lax.fori_loop(0, ts // CONV_ROWS, chunk, 0)
    for b in range(SUBLANES):
        ext_ref[b, 0:CONV_CARRY, :] = ext_ref[b, ts:ts + CONV_CARRY, :]

    c = conv_ref[...] + bdw_ref[...]
    mu = jnp.mean(c, axis=-1, keepdims=True)
    cc = c - mu
    var = jnp.mean(cc * cc, axis=-1, keepdims=True)
    un = cc * lax.rsqrt(var + EPS) * lng_ref[...] + lnb_ref[...]
    act = (un * _sigmoid(un)).astype(BF16)
    o_ref[0] = x + _dot(act, wout_ref[...]) + bout_ref[...]


def _conv_mixer(x, g, w_in, b_in, w_dw, b_dw, ln_g, ln_b, w_out, b_out):
    b, s, d = x.shape
    ts = min(TILE_CONV, s)
    assert s % ts == 0 and ts % CONV_ROWS == 0 and d % CONV_COLS == 0
    row = lambda v: v.reshape(1, -1)
    const = lambda shape: pl.BlockSpec(shape, lambda i, j: (0,) * len(shape))
    return pl.pallas_call(
        _conv_kernel,
        out_shape=jax.ShapeDtypeStruct(x.shape, F32),
        grid=(b, s // ts),
        in_specs=[
            pl.BlockSpec((1, ts, d), lambda i, j: (i, j, 0)),
            const((1, d)), const((d, 2 * d)), const((1, 2 * d)), const((CONV_KERNEL * SUBLANES, d)), const((1, d)),
            const((1, d)), const((1, d)), const((d, d)), const((1, d)),
        ],
        out_specs=pl.BlockSpec((1, ts, d), lambda i, j: (i, j, 0)),
        scratch_shapes=[pltpu.VMEM((SUBLANES, CONV_CARRY + ts, d), F32), pltpu.VMEM((ts, d), F32)],
        compiler_params=_cparams(("arbitrary", "arbitrary")),
        name="conv_mixer",
    )(x, row(g), w_in.astype(BF16), row(b_in), jnp.repeat(w_dw, SUBLANES, axis=0), row(b_dw), row(ln_g), row(ln_b),
      w_out.astype(BF16), row(b_out))


def _gla_levels(ts):
    sizes = [GLA_LEAF]
    while sizes[-1] < ts:
        sizes.append(sizes[-1] * 2)
    return sizes


def _moe_residual(x, route, y0, y1):
    g = lax.bitcast_convert_type(route, F32)
    g = jnp.concatenate([g, jnp.zeros((LANES - g.shape[0], g.shape[1]), F32)], axis=0).T
    gate0 = g[:, 2 * TOP_K:2 * TOP_K + 1]
    gate1 = g[:, 2 * TOP_K + 1:2 * TOP_K + 2]
    return x + _unpack_bf16_pairs(y0) * gate0 + _unpack_bf16_pairs(y1) * gate1


def _gla_kernel(x_ref, gate_ref, y0_ref, y1_ref, g_ref, wq_ref, wk_ref, wv_ref, wa_ref, wr_ref, wa2_ref, ba_ref,
                ng_ref, wo_ref, o_ref, state_ref, x_s, q_s, k_s, v_s, og_s, la_s, *, tiles_per_seq):
    j = pl.program_id(0)
    dkh = wq_ref.shape[1] // GLA_HEADS

    @pl.when(j == 0)
    def _():
        for ref in (x_s, q_s, k_s, og_s, la_s):
            ref[...] = jnp.zeros(ref.shape, F32)
        v_s[...] = jnp.zeros(v_s.shape, BF16)

    @pl.when((j == 0) | ((j - 1) % tiles_per_seq == 0))
    def _():
        state_ref[...] = jnp.zeros(state_ref.shape, F32)

    def stage_a(slot):
        x = _moe_residual(x_ref[0], gate_ref[...], y0_ref[...], y1_ref[...])
        h = _rms(x, g_ref[...]).astype(BF16)
        x_s[slot] = x
        q_s[slot] = _dot(h, wq_ref[...]) * (dkh ** -0.5)
        k_s[slot] = _dot(h, wk_ref[...])
        v_s[slot] = _dot(h, wv_ref[...]).astype(BF16)
        r = _dot(h, wr_ref[...])
        og_s[slot] = r * _sigmoid(r)
        a = _dot(h, wa_ref[...]).astype(BF16)
        z = _dot(jnp.concatenate([a, a], axis=1), wa2_ref[...]) + ba_ref[...]
        la_s[slot] = -(jnp.maximum(-z, 0.0) + jnp.log(1.0 + jnp.exp(-jnp.abs(z)))) * (1.0 / GLA_TAU)

    def stage_b(slot):
        _gla_recurrence(x_s[slot], q_s[slot], k_s[slot], v_s[slot], og_s[slot], la_s[slot],
                        ng_ref, wo_ref, o_ref, state_ref)

    @pl.when(j % 2 == 0)
    def _():
        stage_b(1)
        stage_a(0)

    @pl.when(j % 2 == 1)
    def _():
        stage_b(0)
        stage_a(1)


def _gla_recurrence(x, q, k, v, out_gate, log_a, ng_ref, wo_ref, o_ref, state_ref):
    ts, d = x.shape
    dk = q.shape[1]
    dkh = dk // GLA_HEADS
    dvh = d // GLA_HEADS

    row = lax.broadcasted_iota(I32, (ts, ts), 0)
    col = lax.broadcasted_iota(I32, (ts, ts), 1)
    tri = jnp.where(col <= row, 1.0, 0.0).astype(BF16)
    bcum = _dot(tri, log_a.astype(BF16))
    b_last = bcum[ts - 1:ts, :]

    q_in = (q * jnp.exp(bcum)).astype(BF16)
    k_out = (k * jnp.exp(b_last - bcum)).astype(BF16)

    sizes = _gla_levels(ts)
    scores = [None] * GLA_HEADS
    for lvl, size in enumerate(sizes):
        half = size // 2
        same_block = (row & -size) == (col & -size)
        if lvl == 0:
            pair = same_block & (col <= row)
            q_ok = k_ok = None
        else:
            pair = same_block & ((row & (size - 1)) >= half) & ((col & (size - 1)) < half)
            pos = lax.broadcasted_iota(I32, (ts, dk), 0) & (size - 1)
            q_ok = pos >= half
            k_ok = pos < half
        ref = jnp.concatenate(
            [jnp.broadcast_to(bcum[r0 + half:r0 + half + 1, :], (size, dk)) for r0 in range(0, ts, size)], axis=0)
        ql = q * jnp.exp(bcum - ref)
        kl = k * jnp.exp(ref - bcum)
        if q_ok is not None:
            ql = jnp.where(q_ok, ql, 0.0)
            kl = jnp.where(k_ok, kl, 0.0)
        ql = ql.astype(BF16)
        kl = kl.astype(BF16)
        for hd in range(GLA_HEADS):
            c0 = hd * dkh
            a = _dot_nt(ql[:, c0:c0 + dkh], kl[:, c0:c0 + dkh])
            a = jnp.where(pair, a, 0.0)
            scores[hd] = a if scores[hd] is None else scores[hd] + a

    outs = []
    for hd in range(GLA_HEADS):
        c0 = hd * dkh
        v_h = v[:, hd * dvh:(hd + 1) * dvh]
        st = state_ref[hd]
        o_h = _dot(scores[hd].astype(BF16), v_h) + _dot_nt(q_in[:, c0:c0 + dkh], st.astype(BF16))
        decay = jnp.exp(b_last[:, c0:c0 + dkh])
        state_ref[hd] = st * decay + _dot_tn(v_h, k_out[:, c0:c0 + dkh])
        o_h = o_h * lax.rsqrt(jnp.mean(o_h * o_h, axis=-1, keepdims=True) + EPS) * ng_ref[...]
        outs.append(o_h)
    o = jnp.concatenate(outs, axis=1) * out_gate
    o_ref[0] = x + _dot(o.astype(BF16), wo_ref[...])


def _gla_mixer(x, gates, y0, y1, g, w_in, w_a2, b_a, norm_g, w_o):
    b, s, d = x.shape
    dk = w_a2.shape[1]
    ts = min(TILE_GLA, s)
    assert s % ts == 0 and ts % GLA_LEAF == 0
    nj = s // ts
    row = lambda v: v.reshape(1, -1)
    const = lambda shape: pl.BlockSpec(shape, lambda j: (0,) * len(shape))
    wq = w_in[:, :dk].astype(BF16)
    wk = w_in[:, dk:2 * dk].astype(BF16)
    wv = w_in[:, 2 * dk:2 * dk + d].astype(BF16)
    wa = jnp.pad(w_in[:, 2 * dk + d:2 * dk + d + GLA_RANK], ((0, 0), (0, GLA_RANK_PAD - GLA_RANK))).astype(BF16)
    wr = w_in[:, 2 * dk + d + GLA_RANK:].astype(BF16)
    wa2 = jnp.pad(w_a2, ((0, GLA_RANK_PAD - GLA_RANK), (0, 0)))
    wa2_hi = wa2.astype(BF16)
    wa2_lo = (wa2 - wa2_hi.astype(F32)).astype(BF16)
    wa2_split = jnp.concatenate([wa2_hi, wa2_lo], axis=0)
    dvh = d // GLA_HEADS
    n = b * nj

    def tile(j, lag):
        return jnp.clip(j - lag, 0, n - 1)

    tokens = lambda lag: (lambda j: (tile(j, lag), 0))
    return pl.pallas_call(
        functools.partial(_gla_kernel, tiles_per_seq=nj),
        out_shape=jax.ShapeDtypeStruct(x.shape, F32),
        grid=(n + 1,),
        in_specs=[
            pl.BlockSpec((1, ts, d), lambda j: (tile(j, 0) // nj, tile(j, 0) % nj, 0)),
            pl.BlockSpec((ROUTE_ROWS, ts), lambda j: (0, tile(j, 0))),
            pl.BlockSpec((ts, d // 2), tokens(0)),
            pl.BlockSpec((ts, d // 2), tokens(0)),
            const((1, d)), const((d, dk)), const((d, dk)), const((d, d)), const((d, GLA_RANK_PAD)),
            const((d, d)), const((2 * GLA_RANK_PAD, dk)), const((1, dk)), const((1, dvh)), const((d, d)),
        ],
        out_specs=pl.BlockSpec((1, ts, d), lambda j: (tile(j, 1) // nj, tile(j, 1) % nj, 0)),
        scratch_shapes=[pltpu.VMEM((GLA_HEADS, dvh, dk // GLA_HEADS), F32),
                        pltpu.VMEM((2, ts, d), F32), pltpu.VMEM((2, ts, dk), F32), pltpu.VMEM((2, ts, dk), F32),
                        pltpu.VMEM((2, ts, d), BF16), pltpu.VMEM((2, ts, d), F32), pltpu.VMEM((2, ts, dk), F32)],
        compiler_params=_cparams(("arbitrary",)),
        name="gla_mixer",
    )(x, gates, y0, y1, row(g), wq, wk, wv, wa, wr, wa2_split, row(b_a), row(norm_g), w_o.astype(BF16))


def _xattn_router_kernel(x_ref, gx_ref, qk_ref, vo_ref, gf_ref, wr_ref, br_ref, upper_ref,
                         x_out_ref, h_out_ref, route_ref, cnt_ref, carry_ref):
    ts, d = x_ref.shape[1], x_ref.shape[2]
    nm = qk_ref.shape[3] // XATTN_HEADS
    first = (pl.program_id(0) == 0) & (pl.program_id(1) == 0)

    @pl.when(first)
    def _():
        carry_ref[...] = jnp.zeros(carry_ref.shape, F32)

    x = x_ref[0]
    h = _rms(x, gx_ref[...]).astype(BF16)
    scores = _dot(h, qk_ref[0, 0])
    probs = []
    for a in range(XATTN_HEADS):
        s = scores[:, a * nm:(a + 1) * nm]
        p = jnp.exp(s - jnp.max(s, axis=-1, keepdims=True))
        probs.append((p / jnp.sum(p, axis=-1, keepdims=True)).astype(BF16))
    x2 = x + _dot(jnp.concatenate(probs, axis=1), vo_ref[0, 0])
    x_out_ref[0] = x2

    hf = _rms(x2, gf_ref[...])
    h_out_ref[...] = _pack_bf16_pairs(hf)

    both = _dot_nt(wr_ref[...], hf.astype(BF16))
    logits = both[0:ROUTER_ROWS, :] + both[ROUTER_ROWS:2 * ROUTER_ROWS, :] + br_ref[...]
    gl = logits[N_EXPERTS:N_EXPERTS + N_GROUPS, :]
    gi = lax.broadcasted_iota(I32, gl.shape, 0).astype(F32)
    gmax = jnp.max(gl, axis=0, keepdims=True)
    g_sel = jnp.min(jnp.where(gl == gmax, gi, float(N_GROUPS)), axis=0, keepdims=True)
    pg_sel = 1.0 / jnp.sum(jnp.exp(gl - gmax), axis=0, keepdims=True)

    el = jnp.zeros((EXPERTS_PER_GROUP, ts), F32)
    for gidx in range(N_GROUPS):
        lo = gidx * EXPERTS_PER_GROUP
        el = jnp.where(g_sel == float(gidx), logits[lo:lo + EXPERTS_PER_GROUP, :], el)
    ei = lax.broadcasted_iota(I32, el.shape, 0).astype(F32)
    m1 = jnp.max(el, axis=0, keepdims=True)
    i1 = jnp.min(jnp.where(el == m1, ei, float(EXPERTS_PER_GROUP)), axis=0, keepdims=True)
    rest = jnp.where(ei == i1, -jnp.inf, el)
    m2 = jnp.max(rest, axis=0, keepdims=True)
    i2 = jnp.min(jnp.where(rest == m2, ei, float(EXPERTS_PER_GROUP)), axis=0, keepdims=True)
    ratio = jnp.exp(m2 - m1)
    gate1 = pg_sel / (1.0 + ratio)
    gate2 = pg_sel * ratio / (1.0 + ratio)
    e1 = g_sel * float(EXPERTS_PER_GROUP) + i1
    e2 = g_sel * float(EXPERTS_PER_GROUP) + i2

    xi = lax.broadcasted_iota(I32, (N_EXPERTS, ts), 0).astype(F32)
    oh1 = jnp.where(xi == e1, 1.0, 0.0)
    oh2 = jnp.where(xi == e2, 1.0, 0.0)
    oh = oh1 + oh2
    n_blk = ts // LANES
    stacked = jnp.concatenate([oh[:, c * LANES:(c + 1) * LANES] for c in range(n_blk)], axis=0)
    within = _dot(stacked.astype(BF16), upper_ref[...])
    totals = jnp.sum(stacked, axis=1, keepdims=True)
    run = carry_ref[...]
    before = []
    for c in range(n_blk):
        before.append(within[c * N_EXPERTS:(c + 1) * N_EXPERTS, :] + run)
        run = run + totals[c * N_EXPERTS:(c + 1) * N_EXPERTS, :]
    before = jnp.concatenate(before, axis=1)
    rank1 = jnp.sum(oh1 * before, axis=0, keepdims=True)
    rank2 = jnp.sum(oh2 * before, axis=0, keepdims=True)
    carry_ref[...] = run
    cnt_ref[...] = jnp.broadcast_to(run, cnt_ref.shape)

    route_ref[0:1, :] = e1.astype(I32)
    route_ref[1:2, :] = e2.astype(I32)
    route_ref[2:3, :] = rank1.astype(I32)
    route_ref[3:4, :] = rank2.astype(I32)
    route_ref[4:5, :] = lax.bitcast_convert_type(gate1, I32)
    route_ref[5:6, :] = lax.bitcast_convert_type(gate2, I32)
    route_ref[3 * TOP_K:ROUTE_ROWS, :] = jnp.zeros((ROUTE_ROWS - 3 * TOP_K, ts), I32)


def _xattn_router(x, g_x, qk_mem, vo_mem, layer, g_f, w_grp, b_grp, w_exp, b_exp):
    b, s, d = x.shape
    hm = qk_mem.shape[3]
    t = b * s
    ts = min(TILE_XATTN, s)
    assert s % ts == 0 and ts % LANES == 0
    nj = s // ts
    row = lambda v: v.reshape(1, -1)
    const = lambda shape: pl.BlockSpec(shape, lambda i, j: (0,) * len(shape))
    pad = ROUTER_ROWS - N_GROUPS - N_EXPERTS
    w_r = jnp.pad(jnp.concatenate([w_exp, w_grp], axis=1).T, ((0, pad), (0, 0)))
    w_r_hi = w_r.astype(BF16)
    w_r_split = jnp.concatenate([w_r_hi, (w_r - w_r_hi.astype(F32)).astype(BF16)], axis=0)
    b_r = jnp.pad(jnp.concatenate([b_exp, b_grp]), (0, pad)).reshape(ROUTER_ROWS, 1)
    ti = jnp.arange(LANES)
    upper = (ti[:, None] < ti[None, :]).astype(BF16)
    return pl.pallas_call(
        _xattn_router_kernel,
        out_shape=(
            jax.ShapeDtypeStruct(x.shape, F32),
            jax.ShapeDtypeStruct((t, d // 2), U32),
            jax.ShapeDtypeStruct((ROUTE_ROWS, t), I32),
            jax.ShapeDtypeStruct((N_EXPERTS, LANES), F32),
        ),
        grid=(b, nj),
        in_specs=[
            pl.BlockSpec((1, ts, d), lambda i, j: (i, j, 0)),
            const((1, d)),
            pl.BlockSpec((1, 1, d, hm), lambda i, j: (layer, i, 0, 0)),
            pl.BlockSpec((1, 1, hm, d), lambda i, j: (layer, i, 0, 0)),
            const((1, d)), const((2 * ROUTER_ROWS, d)), const((ROUTER_ROWS, 1)), const((LANES, LANES)),
        ],
        out_specs=(
            pl.BlockSpec((1, ts, d), lambda i, j: (i, j, 0)),
            pl.BlockSpec((ts, d // 2), lambda i, j: (i * nj + j, 0)),
            pl.BlockSpec((ROUTE_ROWS, ts), lambda i, j: (0, i * nj + j)),
            pl.BlockSpec((N_EXPERTS, LANES), lambda i, j: (0, 0)),
        ),
        scratch_shapes=[pltpu.VMEM((N_EXPERTS, 1), F32)],
        compiler_params=_cparams(("arbitrary", "arbitrary")),
        name="xattn_router",
    )(x, row(g_x), qk_mem, vo_mem, row(g_f), w_r_split, b_r, upper)


def _sc_mesh():
    return plsc.VectorSubcoreMesh(core_axis_name="c", subcore_axis_name="s",
                                  num_cores=SC_CORES, num_subcores=SC_SUBCORES)


def _sc_worker():
    return lax.axis_index("s") * SC_CORES + lax.axis_index("c")


def _dispatch(h_packed, dest, n_rows):
    t, w = h_packed.shape
    chunk = SC_DISPATCH_CHUNK
    per_worker = t // SC_WORKERS
    n_chunks = per_worker // chunk
    assert n_chunks % 2 == 0 and n_chunks * chunk * SC_WORKERS == t
    dest = dest.reshape(TOP_K, SC_WORKERS * n_chunks, chunk)
    rows_buf = pltpu.VMEM((chunk, w), U32)

    @functools.partial(
        pl.kernel, mesh=_sc_mesh(),
        out_type=jax.ShapeDtypeStruct((n_rows, w), U32),
        scratch_types=[pltpu.VMEM((n_chunks, chunk), I32), pltpu.VMEM((n_chunks, chunk), I32), rows_buf, rows_buf,
                       pltpu.SemaphoreType.DMA((2,)), pltpu.SemaphoreType.DMA((2, TOP_K))],
        name="moe_dispatch_sc",
    )
    def run(h_hbm, d0_hbm, d1_hbm, xbuf_hbm, idx0_v, idx1_v, buf_a, buf_b, read_sem, write_sem):
        wid = _sc_worker()
        pltpu.sync_copy(d0_hbm.at[pl.ds(wid * n_chunks, n_chunks)], idx0_v)
        pltpu.sync_copy(d1_hbm.at[pl.ds(wid * n_chunks, n_chunks)], idx1_v)

        @pl.loop(0, n_chunks, step=2)
        def _(i):
            reads = [pltpu.async_copy(h_hbm.at[pl.ds(wid * per_worker + (i + j) * chunk, chunk)], buf, read_sem.at[j])
                     for j, buf in enumerate((buf_a, buf_b))]
            writes = []
            for j, buf in enumerate((buf_a, buf_b)):
                reads[j].wait()
                writes.append(pltpu.async_copy(buf, xbuf_hbm.at[idx0_v.at[i + j]], write_sem.at[j, 0]))
                writes.append(pltpu.async_copy(buf, xbuf_hbm.at[idx1_v.at[i + j]], write_sem.at[j, 1]))
            for copy in writes:
                copy.wait()

    return run(h_packed, dest[0], dest[1])


def _gather_pairs(y_buf, dest):
    t = dest.shape[1]
    w = y_buf.shape[1]
    chunk = SC_GATHER_CHUNK
    per_worker = t // SC_WORKERS
    n_chunks = per_worker // chunk
    assert n_chunks % 2 == 0 and n_chunks * chunk * SC_WORKERS == t
    dest = dest.reshape(TOP_K, SC_WORKERS * n_chunks, chunk)
    out = jax.ShapeDtypeStruct((t, w), U32)
    rows_buf = pltpu.VMEM((chunk, w), U32)

    @functools.partial(
        pl.kernel, mesh=_sc_mesh(),
        out_type=(out, out),
        scratch_types=[pltpu.VMEM((n_chunks, chunk), I32), pltpu.VMEM((n_chunks, chunk), I32),
                       rows_buf, rows_buf, rows_buf, rows_buf, pltpu.SemaphoreType.DMA((2, TOP_K))],
        name="moe_gather_sc",
    )
    def run(y_hbm, d0_hbm, d1_hbm, y0_hbm, y1_hbm, idx0_v, idx1_v, buf_a0, buf_a1, buf_b0, buf_b1, sem):
        wid = _sc_worker()
        pltpu.sync_copy(d0_hbm.at[pl.ds(wid * n_chunks, n_chunks)], idx0_v)
        pltpu.sync_copy(d1_hbm.at[pl.ds(wid * n_chunks, n_chunks)], idx1_v)

        @pl.loop(0, n_chunks, step=2)
        def _(i):
            bufs = ((buf_a0, buf_a1), (buf_b0, buf_b1))
            gathers = [[pltpu.async_copy(y_hbm.at[idx_v.at[i + j]], bufs[j][k], sem.at[j, k])
                        for k, idx_v in enumerate((idx0_v, idx1_v))] for j in range(2)]
            stores = []
            for j in range(2):
                rows = pl.ds(wid * per_worker + (i + j) * chunk, chunk)
                for k, out_hbm in enumerate((y0_hbm, y1_hbm)):
                    gathers[j][k].wait()
                    stores.append(pltpu.async_copy(bufs[j][k], out_hbm.at[rows], sem.at[j, k]))
            for copy in stores:
                copy.wait()

    return run(y_buf, dest[0], dest[1])


def _expert_kernel(be_ref, bf_ref, bv_ref, slot_ref, next_ref, x_ref, wg_hbm, wu_hbm, wd_hbm, y_ref,
                   wg_f32, wu_f32, wd_f32, wg_bf, wu_bf, wd_bf, sems, *, layer):
    blk = pl.program_id(0)
    valid = bv_ref[blk]

    def weight_copies(expert, slot):
        return [pltpu.make_async_copy(src.at[layer, expert], dst.at[slot], sems.at[slot, i])
                for i, (src, dst) in enumerate(((wg_hbm, wg_f32), (wu_hbm, wu_f32), (wd_hbm, wd_f32)))]

    @pl.when(blk == 0)
    def _():
        for copy in weight_copies(be_ref[0], 0):
            copy.start()

    @pl.when(bf_ref[blk] == 1)
    def _():
        slot = slot_ref[blk]
        for copy in weight_copies(be_ref[blk], slot):
            copy.wait()

        @pl.when(next_ref[blk] >= 0)
        def _():
            for copy in weight_copies(next_ref[blk], 1 - slot):
                copy.start()

        wg_bf[...] = wg_f32[slot].astype(BF16)
        wu_bf[...] = wu_f32[slot].astype(BF16)
        wd_bf[...] = wd_f32[slot].astype(BF16)

    @pl.when(valid > 0)
    def _():
        live = lax.broadcasted_iota(I32, x_ref.shape, 0) < valid
        xb = _unpack_bf16_pairs(jnp.where(live, x_ref[...], jnp.uint32(0))).astype(BF16)
        gt = _dot(xb, wg_bf[...])
        up = _dot(xb, wu_bf[...])
        act = (gt * _sigmoid(gt) * up).astype(BF16)
        y_ref[...] = _pack_bf16_pairs(_dot(act, wd_bf[...]))

    @pl.when(valid <= 0)
    def _():
        y_ref[...] = jnp.zeros(y_ref.shape, U32)


def _experts(x_buf, block_expert, block_first, block_valid, w_gate, w_up, w_down, layer):
    n_rows, w = x_buf.shape
    d, de = w_gate.shape[2], w_gate.shape[3]
    bm = MOE_BLOCK_ROWS
    n = n_rows // bm
    block_slot = (jnp.cumsum(block_first) - 1) % 2
    idx = jnp.arange(n, dtype=I32)
    later_first = jnp.concatenate([jnp.where(block_first[1:] == 1, idx[1:], n), jnp.full((1,), n, I32)])
    next_first = lax.cummin(later_first, reverse=True)
    block_next = jnp.where(next_first < n, block_expert[jnp.minimum(next_first, n - 1)], -1).astype(I32)
    any_space = pl.BlockSpec(memory_space=pl.ANY)
    grid_spec = pltpu.PrefetchScalarGridSpec(
        num_scalar_prefetch=5,
        grid=(n,),
        in_specs=[pl.BlockSpec((bm, w), lambda i, *_: (i, 0)), any_space, any_space, any_space],
        out_specs=pl.BlockSpec((bm, w), lambda i, *_: (i, 0)),
        scratch_shapes=[pltpu.VMEM((2, d, de), F32), pltpu.VMEM((2, d, de), F32), pltpu.VMEM((2, de, d), F32),
                        pltpu.VMEM((d, de), BF16), pltpu.VMEM((d, de), BF16), pltpu.VMEM((de, d), BF16),
                        pltpu.SemaphoreType.DMA((2, 3))],
    )
    return pl.pallas_call(
        functools.partial(_expert_kernel, layer=layer),
        out_shape=jax.ShapeDtypeStruct((n_rows, w), U32),
        grid_spec=grid_spec,
        compiler_params=_cparams(("arbitrary",)),
        name="moe_experts",
    )(block_expert, block_first, block_valid, block_slot.astype(I32), block_next, x_buf, w_gate, w_up, w_down)


def _combine_kernel(x_ref, gate_ref, y0_ref, y1_ref, gfin_ref, *rest, final_norm):
    o_ref = rest[-1]
    out = _moe_residual(x_ref[...], gate_ref[...], y0_ref[...], y1_ref[...])
    if final_norm:
        out = _rms(out, gfin_ref[...])
    o_ref[...] = out


def _combine(x2, y0, y1, gates, g_final, final_norm, first_row=0, out_so_far=None):
    t, d = x2.shape
    rows, w = y0.shape
    ts = min(TILE_COMBINE, rows)
    assert rows % ts == 0 and first_row % ts == 0
    off = first_row // ts
    in_specs = [
        pl.BlockSpec((ts, d), lambda i: (i + off, 0)),
        pl.BlockSpec((ROUTE_ROWS, ts), lambda i: (0, i + off)),
        pl.BlockSpec((ts, w), lambda i: (i, 0)),
        pl.BlockSpec((ts, w), lambda i: (i, 0)),
        pl.BlockSpec((1, d), lambda i: (0, 0)),
    ]
    args = [x2, gates, y0, y1, g_final.reshape(1, d)]
    aliases = {}
    if out_so_far is not None:
        in_specs.append(pl.BlockSpec(memory_space=pl.ANY))
        args.append(out_so_far)
        aliases = {len(args) - 1: 0}
    return pl.pallas_call(
        functools.partial(_combine_kernel, final_norm=final_norm),
        out_shape=jax.ShapeDtypeStruct((t, d), F32),
        grid=(rows // ts,),
        in_specs=in_specs,
        out_specs=pl.BlockSpec((ts, d), lambda i: (i + off, 0)),
        input_output_aliases=aliases,
        compiler_params=_cparams(("arbitrary",)),
        name="moe_combine",
    )(*args)


def _moe_layout(route, counts):
    bm = MOE_BLOCK_ROWS
    t = route.shape[1]
    assert (t * TOP_K) % bm == 0
    n_blocks = (t * TOP_K) // bm + N_EXPERTS
    cnt = counts[:, 0].astype(I32)
    padded = (cnt + bm - 1) // bm * bm
    pad_ends = jnp.cumsum(padded)
    pad_off = pad_ends - padded
    experts = jnp.arange(N_EXPERTS, dtype=I32)
    hit = route[0:TOP_K, :, None] == experts
    dest = jnp.sum(jnp.where(hit, pad_off, 0), axis=-1) + route[TOP_K:2 * TOP_K]
    gates = route
    starts = jnp.arange(n_blocks, dtype=I32) * bm
    block_expert = jnp.minimum(jnp.sum((pad_ends[None, :] <= starts[:, None]).astype(I32), axis=1),
                               N_EXPERTS - 1)
    block_first = jnp.concatenate([jnp.ones((1,), I32), (block_expert[1:] != block_expert[:-1]).astype(I32)])
    own = block_expert[:, None] == experts
    block_valid = jnp.clip(jnp.sum(jnp.where(own, cnt + pad_off, 0), axis=1) - starts, 0, bm)
    block_valid = jnp.where(starts < pad_ends[-1], block_valid, 0).astype(I32)
    return dest, gates, block_expert, block_first, block_valid, n_blocks * bm


def kernel(x, mem, norm_mix, norm_xattn, norm_ffn, norm_mem, norm_final, conv_w_in, conv_b_in, conv_w_dw,
           conv_b_dw, conv_ln_g, conv_ln_b, conv_w_out, conv_b_out, gla_w_in, gla_w_a2, gla_b_a, gla_norm_g,
           gla_w_o, xa_w_q, xa_w_kv, xa_w_o, moe_w_grp, moe_b_grp, moe_w_exp, moe_b_exp, moe_w_gate, moe_w_up,
           moe_w_down):
    b, s, d = x.shape
    depth = norm_mix.shape[0]
    qk_mem, vo_mem = _mem_kv(mem, norm_mem, xa_w_kv, xa_w_q, xa_w_o)
    moe = None
    for i in range(depth):
        j = i // 2
        if i % 2 == 0:
            if moe is not None:
                x = _combine(x.reshape(b * s, d), moe[1], moe[2], moe[0], norm_final, False).reshape(b, s, d)
            x = _conv_mixer(x, norm_mix[i], conv_w_in[j], conv_b_in[j], conv_w_dw[j], conv_b_dw[j],
                            conv_ln_g[j], conv_ln_b[j], conv_w_out[j], conv_b_out[j])
        else:
            x = _gla_mixer(x, *moe, norm_mix[i], gla_w_in[j], gla_w_a2[j], gla_b_a[j], gla_norm_g[j], gla_w_o[j])
        x2, h_packed, route, counts = _xattn_router(
            x, norm_xattn[i], qk_mem, vo_mem, i, norm_ffn[i],
            moe_w_grp[i], moe_b_grp[i], moe_w_exp[i], moe_b_exp[i])
        dest, gates, block_expert, block_first, block_valid, n_rows = _moe_layout(route, counts)
        x_buf = _dispatch(h_packed, dest, n_rows)
        y_buf = _experts(x_buf, block_expert, block_first, block_valid, moe_w_gate, moe_w_up, moe_w_down, i)
        if i < depth - 1:
            y0, y1 = _gather_pairs(y_buf, dest)
            x, moe = x2, (gates, y0, y1)
    part = (b * s) // FINAL_PARTS
    out = None
    for p in range(FINAL_PARTS):
        y0, y1 = _gather_pairs(y_buf, dest[:, p * part:(p + 1) * part])
        out = _combine(x2.reshape(b * s, d), y0, y1, gates, norm_final, True, p * part, out)
    return out.reshape(b, s, d)
```

```python
import functools

import jax
import jax.numpy as jnp
from jax import lax
from jax.experimental import pallas as pl
from jax.experimental.pallas import tpu as pltpu
from jax.experimental.pallas import tpu_sc as plsc

F32 = jnp.float32
BF16 = jnp.bfloat16
I32 = jnp.int32
U32 = jnp.uint32

EPS = 1e-6
CONV_KERNEL = 31
CONV_CARRY = 32
CONV_ROWS = 64
CONV_COLS = 256
SUBLANES = 8
LANES = 128
GLA_HEADS = 4
GLA_RANK = 16
GLA_RANK_PAD = 128
GLA_TAU = 16.0
GLA_LEAF = 32
XATTN_HEADS = 4
N_GROUPS = 4
EXPERTS_PER_GROUP = 8
N_EXPERTS = N_GROUPS * EXPERTS_PER_GROUP
ROUTER_ROWS = 40
TOP_K = 2
ROUTE_ROWS = 8

TILE_CONV = 512
TILE_GLA = 256
TILE_XATTN = 1024
TILE_COMBINE = 512
TILE_SLOTS = 4096
FINAL_PARTS = 2
SC_CORES = 2
SC_SUBCORES = 16
SC_WORKERS = SC_CORES * SC_SUBCORES
SC_DISPATCH_CHUNK = 64
SC_GATHER_CHUNK = 32
MOE_BLOCK_ROWS = 512
VMEM_LIMIT = 56 * 1024 * 1024


def _cparams(sem):
    return pltpu.CompilerParams(dimension_semantics=sem, vmem_limit_bytes=VMEM_LIMIT)


def _rms(x, g):
    return x * lax.rsqrt(jnp.mean(x * x, axis=-1, keepdims=True) + EPS) * g


def _sigmoid(x):
    return 0.5 * jnp.tanh(0.5 * x) + 0.5


def _dot(a, b):
    return jnp.dot(a, b, preferred_element_type=F32)


def _dot_nt(a, b):
    return lax.dot_general(a, b, (((1,), (1,)), ((), ())), preferred_element_type=F32)


def _dot_tn(a, b):
    return lax.dot_general(a, b, (((0,), (0,)), ((), ())), preferred_element_type=F32)


def _pack_bf16_pairs(x):
    w = x.shape[1] // 2
    hi = lax.bitcast_convert_type(x[:, :w].astype(BF16).astype(F32), U32)
    lo = lax.bitcast_convert_type(x[:, w:].astype(BF16).astype(F32), U32)
    return hi | (lo >> 16)


def _unpack_bf16_pairs(p):
    hi = lax.bitcast_convert_type(p & jnp.uint32(0xFFFF0000), F32)
    lo = lax.bitcast_convert_type(p << 16, F32)
    return jnp.concatenate([hi, lo], axis=1)


def _memkv_kernel(mem_ref, g_ref, wkv_ref, wq_ref, wo_ref, qk_ref, vo_ref):
    nm, d = mem_ref.shape[1], mem_ref.shape[2]
    hd = d // XATTN_HEADS
    mn = _rms(mem_ref[0], g_ref[...]).astype(BF16)
    kv = _dot(mn, wkv_ref[0])
    k = kv[:, :d].astype(BF16)
    v = kv[:, d:].astype(BF16)
    for a in range(XATTN_HEADS):
        sl = slice(a * hd, (a + 1) * hd)
        qk_ref[0, 0, :, a * nm:(a + 1) * nm] = (_dot_nt(wq_ref[0, :, sl], k[:, sl]) * (hd ** -0.5)).astype(BF16)
        vo_ref[0, 0, a * nm:(a + 1) * nm, :] = _dot(v[:, sl], wo_ref[0, sl, :]).astype(BF16)


def _mem_kv(mem, norm_mem, w_kv, w_q, w_o):
    b, nm, d = mem.shape
    depth = w_kv.shape[0]
    per_layer = lambda shape: pl.BlockSpec((1,) + shape, lambda l, i: (l, 0, 0))
    return pl.pallas_call(
        _memkv_kernel,
        out_shape=(jax.ShapeDtypeStruct((depth, b, d, XATTN_HEADS * nm), BF16),
                   jax.ShapeDtypeStruct((depth, b, XATTN_HEADS * nm, d), BF16)),
        grid=(depth, b),
        in_specs=[
            pl.BlockSpec((1, nm, d), lambda l, i: (i, 0, 0)),
            pl.BlockSpec((1, d), lambda l, i: (0, 0)),
            per_layer((d, 2 * d)), per_layer((d, d)), per_layer((d, d)),
        ],
        out_specs=(
            pl.BlockSpec((1, 1, d, XATTN_HEADS * nm), lambda l, i: (l, i, 0, 0)),
            pl.BlockSpec((1, 1, XATTN_HEADS * nm, d), lambda l, i: (l, i, 0, 0)),
        ),
        compiler_params=_cparams(("arbitrary", "arbitrary")),
        name="mem_kv",
    )(mem, norm_mem.reshape(1, d), w_kv.astype(BF16), w_q.astype(BF16), w_o.astype(BF16))


def _conv_kernel(x_ref, g_ref, win_ref, bin_ref, wdw_ref, bdw_ref, lng_ref, lnb_ref, wout_ref, bout_ref,
                 o_ref, ext_ref, conv_ref):
    ts, d = x_ref.shape[1], x_ref.shape[2]

    @pl.when(pl.program_id(1) == 0)
    def _():
        ext_ref[...] = jnp.zeros(ext_ref.shape, F32)

    x = x_ref[0]
    h = _rms(x, g_ref[...]).astype(BF16)
    u = _dot(h, win_ref[...]) + bin_ref[...]
    glu = u[:, :d] * _sigmoid(u[:, d:])
    for b in range(SUBLANES):
        ext_ref[b, CONV_CARRY - b:CONV_CARRY - b + ts, :] = glu

    first = CONV_CARRY - (CONV_KERNEL - 1)

    def chunk(i, carry):
        r0 = pl.multiple_of(i * CONV_ROWS, CONV_ROWS)
        for c0 in range(0, d, CONV_COLS):
            cols = slice(c0, c0 + CONV_COLS)
            acc = [jnp.zeros((SUBLANES, CONV_COLS), F32) for _ in range(CONV_ROWS // SUBLANES)]
            for k in range(CONV_KERNEL):
                b = (first + k) % SUBLANES
                wk = wdw_ref[k * SUBLANES:(k + 1) * SUBLANES, cols]
                for j in range(CONV_ROWS // SUBLANES):
                    rows = pl.ds(r0 + (first + k - b) + j * SUBLANES, SUBLANES)
                    acc[j] = acc[j] + ext_ref[b, rows, cols] * wk
            for j in range(CONV_ROWS // SUBLANES):
                conv_ref[pl.ds(r0 + j * SUBLANES, SUBLANES), cols] = acc[j]
        return carry

    lax.fori_loop(0, ts // CONV_ROWS, chunk, 0)
    for b in range(SUBLANES):
        ext_ref[b, 0:CONV_CARRY, :] = ext_ref[b, ts:ts + CONV_CARRY, :]

    c = conv_ref[...] + bdw_ref[...]
    mu = jnp.mean(c, axis=-1, keepdims=True)
    cc = c - mu
    var = jnp.mean(cc * cc, axis=-1, keepdims=True)
    un = cc * lax.rsqrt(var + EPS) * lng_ref[...] + lnb_ref[...]
    act = (un * _sigmoid(un)).astype(BF16)
    o_ref[0] = x + _dot(act, wout_ref[...]) + bout_ref[...]


def _conv_mixer(x, g, w_in, b_in, w_dw, b_dw, ln_g, ln_b, w_out, b_out):
    b, s, d = x.shape
    ts = min(TILE_CONV, s)
    assert s % ts == 0 and ts % CONV_ROWS == 0 and d % CONV_COLS == 0
    row = lambda v: v.reshape(1, -1)
    const = lambda shape: pl.BlockSpec(shape, lambda i, j: (0,) * len(shape))
    return pl.pallas_call(
        _conv_kernel,
        out_shape=jax.ShapeDtypeStruct(x.shape, F32),
        grid=(b, s // ts),
        in_specs=[
            pl.BlockSpec((1, ts, d), lambda i, j: (i, j, 0)),
            const((1, d)), const((d, 2 * d)), const((1, 2 * d)), const((CONV_KERNEL * SUBLANES, d)), const((1, d)),
            const((1, d)), const((1, d)), const((d, d)), const((1, d)),
        ],
        out_specs=pl.BlockSpec((1, ts, d), lambda i, j: (i, j, 0)),
        scratch_shapes=[pltpu.VMEM((SUBLANES, CONV_CARRY + ts, d), F32), pltpu.VMEM((ts, d), F32)],
        compiler_params=_cparams(("arbitrary", "arbitrary")),
        name="conv_mixer",
    )(x, row(g), w_in.astype(BF16), row(b_in), jnp.repeat(w_dw, SUBLANES, axis=0), row(b_dw), row(ln_g), row(ln_b),
      w_out.astype(BF16), row(b_out))


def _gla_levels(ts):
    sizes = [GLA_LEAF]
    while sizes[-1] < ts:
        sizes.append(sizes[-1] * 2)
    return sizes


def _moe_residual(x, route, y0, y1):
    g = lax.bitcast_convert_type(route, F32)
    g = jnp.concatenate([g, jnp.zeros((LANES - g.shape[0], g.shape[1]), F32)], axis=0).T
    gate0 = g[:, 2 * TOP_K:2 * TOP_K + 1]
    gate1 = g[:, 2 * TOP_K + 1:2 * TOP_K + 2]
    return x + _unpack_bf16_pairs(y0) * gate0 + _unpack_bf16_pairs(y1) * gate1


def _gla_kernel(x_ref, gate_ref, y0_ref, y1_ref, g_ref, wq_ref, wk_ref, wv_ref, wa_ref, wr_ref, wa2_ref, ba_ref,
                ng_ref, wo_ref, o_ref, state_ref, x_s, q_s, k_s, v_s, og_s, la_s, *, tiles_per_seq):
    j = pl.program_id(0)
    dkh = wq_ref.shape[1] // GLA_HEADS

    @pl.when(j == 0)
    def _():
        for ref in (x_s, q_s, k_s, og_s, la_s):
            ref[...] = jnp.zeros(ref.shape, F32)
        v_s[...] = jnp.zeros(v_s.shape, BF16)

    @pl.when((j == 0) | ((j - 1) % tiles_per_seq == 0))
    def _():
        state_ref[...] = jnp.zeros(state_ref.shape, F32)

    def stage_a(slot):
        x = _moe_residual(x_ref[0], gate_ref[...], y0_ref[...], y1_ref[...])
        h = _rms(x, g_ref[...]).astype(BF16)
        x_s[slot] = x
        q_s[slot] = _dot(h, wq_ref[...]) * (dkh ** -0.5)
        k_s[slot] = _dot(h, wk_ref[...])
        v_s[slot] = _dot(h, wv_ref[...]).astype(BF16)
        r = _dot(h, wr_ref[...])
        og_s[slot] = r * _sigmoid(r)
        a = _dot(h, wa_ref[...]).astype(BF16)
        z = _dot(jnp.concatenate([a, a], axis=1), wa2_ref[...]) + ba_ref[...]
        la_s[slot] = -(jnp.maximum(-z, 0.0) + jnp.log(1.0 + jnp.exp(-jnp.abs(z)))) * (1.0 / GLA_TAU)

    def stage_b(slot):
        _gla_recurrence(x_s[slot], q_s[slot], k_s[slot], v_s[slot], og_s[slot], la_s[slot],
                        ng_ref, wo_ref, o_ref, state_ref)

    @pl.when(j % 2 == 0)
    def _():
        stage_b(1)
        stage_a(0)

    @pl.when(j % 2 == 1)
    def _():
        stage_b(0)
        stage_a(1)


def _gla_recurrence(x, q, k, v, out_gate, log_a, ng_ref, wo_ref, o_ref, state_ref):
    ts, d = x.shape
    dk = q.shape[1]
    dkh = dk // GLA_HEADS
    dvh = d // GLA_HEADS

    row = lax.broadcasted_iota(I32, (ts, ts), 0)
    col = lax.broadcasted_iota(I32, (ts, ts), 1)
    tri = jnp.where(col <= row, 1.0, 0.0).astype(BF16)
    bcum = _dot(tri, log_a.astype(BF16))
    b_last = bcum[ts - 1:ts, :]

    q_in = (q * jnp.exp(bcum)).astype(BF16)
    k_out = (k * jnp.exp(b_last - bcum)).astype(BF16)

    sizes = _gla_levels(ts)
    scores = [None] * GLA_HEADS
    for lvl, size in enumerate(sizes):
        half = size // 2
        same_block = (row & -size) == (col & -size)
        if lvl == 0:
            pair = same_block & (col <= row)
            q_ok = k_ok = None
        else:
            pair = same_block & ((row & (size - 1)) >= half) & ((col & (size - 1)) < half)
            pos = lax.broadcasted_iota(I32, (ts, dk), 0) & (size - 1)
            q_ok = pos >= half
            k_ok = pos < half
        ref = jnp.concatenate(
            [jnp.broadcast_to(bcum[r0 + half:r0 + half + 1, :], (size, dk)) for r0 in range(0, ts, size)], axis=0)
        ql = q * jnp.exp(bcum - ref)
        kl = k * jnp.exp(ref - bcum)
        if q_ok is not None:
            ql = jnp.where(q_ok, ql, 0.0)
            kl = jnp.where(k_ok, kl, 0.0)
        ql = ql.astype(BF16)
        kl = kl.astype(BF16)
        for hd in range(GLA_HEADS):
            c0 = hd * dkh
            a = _dot_nt(ql[:, c0:c0 + dkh], kl[:, c0:c0 + dkh])
            a = jnp.where(pair, a, 0.0)
            scores[hd] = a if scores[hd] is None else scores[hd] + a

    outs = []
    for hd in range(GLA_HEADS):
        c0 = hd * dkh
        v_h = v[:, hd * dvh:(hd + 1) * dvh]
        st = state_ref[hd]
        o_h = _dot(scores[hd].astype(BF16), v_h) + _dot_nt(q_in[:, c0:c0 + dkh], st.astype(BF16))
        decay = jnp.exp(b_last[:, c0:c0 + dkh])
        state_ref[hd] = st * decay + _dot_tn(v_h, k_out[:, c0:c0 + dkh])
        o_h = o_h * lax.rsqrt(jnp.mean(o_h * o_h, axis=-1, keepdims=True) + EPS) * ng_ref[...]
        outs.append(o_h)
    o = jnp.concatenate(outs, axis=1) * out_gate
    o_ref[0] = x + _dot(o.astype(BF16), wo_ref[...])


def _gla_mixer(x, gates, y0, y1, g, w_in, w_a2, b_a, norm_g, w_o):
    b, s, d = x.shape
    dk = w_a2.shape[1]
    ts = min(TILE_GLA, s)
    assert s % ts == 0 and ts % GLA_LEAF == 0
    nj = s // ts
    row = lambda v: v.reshape(1, -1)
    const = lambda shape: pl.BlockSpec(shape, lambda j: (0,) * len(shape))
    wq = w_in[:, :dk].astype(BF16)
    wk = w_in[:, dk:2 * dk].astype(BF16)
    wv = w_in[:, 2 * dk:2 * dk + d].astype(BF16)
    wa = jnp.pad(w_in[:, 2 * dk + d:2 * dk + d + GLA_RANK], ((0, 0), (0, GLA_RANK_PAD - GLA_RANK))).astype(BF16)
    wr = w_in[:, 2 * dk + d + GLA_RANK:].astype(BF16)
    wa2 = jnp.pad(w_a2, ((0, GLA_RANK_PAD - GLA_RANK), (0, 0)))
    wa2_hi = wa2.astype(BF16)
    wa2_lo = (wa2 - wa2_hi.astype(F32)).astype(BF16)
    wa2_split = jnp.concatenate([wa2_hi, wa2_lo], axis=0)
    dvh = d // GLA_HEADS
    n = b * nj

    def tile(j, lag):
        return jnp.clip(j - lag, 0, n - 1)

    tokens = lambda lag: (lambda j: (tile(j, lag), 0))
    return pl.pallas_call(
        functools.partial(_gla_kernel, tiles_per_seq=nj),
        out_shape=jax.ShapeDtypeStruct(x.shape, F32),
        grid=(n + 1,),
        in_specs=[
            pl.BlockSpec((1, ts, d), lambda j: (tile(j, 0) // nj, tile(j, 0) % nj, 0)),
            pl.BlockSpec((ROUTE_ROWS, ts), lambda j: (0, tile(j, 0))),
            pl.BlockSpec((ts, d // 2), tokens(0)),
            pl.BlockSpec((ts, d // 2), tokens(0)),
            const((1, d)), const((d, dk)), const((d, dk)), const((d, d)), const((d, GLA_RANK_PAD)),
            const((d, d)), const((2 * GLA_RANK_PAD, dk)), const((1, dk)), const((1, dvh)), const((d, d)),
        ],
        out_specs=pl.BlockSpec((1, ts, d), lambda j: (tile(j, 1) // nj, tile(j, 1) % nj, 0)),
        scratch_shapes=[pltpu.VMEM((GLA_HEADS, dvh, dk // GLA_HEADS), F32),
                        pltpu.VMEM((2, ts, d), F32), pltpu.VMEM((2, ts, dk), F32), pltpu.VMEM((2, ts, dk), F32),
                        pltpu.VMEM((2, ts, d), BF16), pltpu.VMEM((2, ts, d), F32), pltpu.VMEM((2, ts, dk), F32)],
        compiler_params=_cparams(("arbitrary",)),
        name="gla_mixer",
    )(x, gates, y0, y1, row(g), wq, wk, wv, wa, wr, wa2_split, row(b_a), row(norm_g), w_o.astype(BF16))


def _xattn_router_kernel(x_ref, gx_ref, qk_ref, vo_ref, gf_ref, wr_ref, br_ref, upper_ref,
                         x_out_ref, h_out_ref, route_ref, cnt_ref, carry_ref):
    ts, d = x_ref.shape[1], x_ref.shape[2]
    nm = qk_ref.shape[3] // XATTN_HEADS
    first = (pl.program_id(0) == 0) & (pl.program_id(1) == 0)

    @pl.when(first)
    def _():
        carry_ref[...] = jnp.zeros(carry_ref.shape, F32)

    x = x_ref[0]
    h = _rms(x, gx_ref[...]).astype(BF16)
    scores = _dot(h, qk_ref[0, 0])
    probs = []
    for a in range(XATTN_HEADS):
        s = scores[:, a * nm:(a + 1) * nm]
        p = jnp.exp(s - jnp.max(s, axis=-1, keepdims=True))
        probs.append((p / jnp.sum(p, axis=-1, keepdims=True)).astype(BF16))
    x2 = x + _dot(jnp.concatenate(probs, axis=1), vo_ref[0, 0])
    x_out_ref[0] = x2

    hf = _rms(x2, gf_ref[...])
    h_out_ref[...] = _pack_bf16_pairs(hf)

    both = _dot_nt(wr_ref[...], hf.astype(BF16))
    logits = both[0:ROUTER_ROWS, :] + both[ROUTER_ROWS:2 * ROUTER_ROWS, :] + br_ref[...]
    gl = logits[N_EXPERTS:N_EXPERTS + N_GROUPS, :]
    gi = lax.broadcasted_iota(I32, gl.shape, 0).astype(F32)
    gmax = jnp.max(gl, axis=0, keepdims=True)
    g_sel = jnp.min(jnp.where(gl == gmax, gi, float(N_GROUPS)), axis=0, keepdims=True)
    pg_sel = 1.0 / jnp.sum(jnp.exp(gl - gmax), axis=0, keepdims=True)

    el = jnp.zeros((EXPERTS_PER_GROUP, ts), F32)
    for gidx in range(N_GROUPS):
        lo = gidx * EXPERTS_PER_GROUP
        el = jnp.where(g_sel == float(gidx), logits[lo:lo + EXPERTS_PER_GROUP, :], el)
    ei = lax.broadcasted_iota(I32, el.shape, 0).astype(F32)
    m1 = jnp.max(el, axis=0, keepdims=True)
    i1 = jnp.min(jnp.where(el == m1, ei, float(EXPERTS_PER_GROUP)), axis=0, keepdims=True)
    rest = jnp.where(ei == i1, -jnp.inf, el)
    m2 = jnp.max(rest, axis=0, keepdims=True)
    i2 = jnp.min(jnp.where(rest == m2, ei, float(EXPERTS_PER_GROUP)), axis=0, keepdims=True)
    ratio = jnp.exp(m2 - m1)
    gate1 = pg_sel / (1.0 + ratio)
    gate2 = pg_sel * ratio / (1.0 + ratio)
    e1 = g_sel * float(EXPERTS_PER_GROUP) + i1
    e2 = g_sel * float(EXPERTS_PER_GROUP) + i2

    xi = lax.broadcasted_iota(I32, (N_EXPERTS, ts), 0).astype(F32)
    oh1 = jnp.where(xi == e1, 1.0, 0.0)
    oh2 = jnp.where(xi == e2, 1.0, 0.0)
    oh = oh1 + oh2
    n_blk = ts // LANES
    stacked = jnp.concatenate([oh[:, c * LANES:(c + 1) * LANES] for c in range(n_blk)], axis=0)
    within = _dot(stacked.astype(BF16), upper_ref[...])
    totals = jnp.sum(stacked, axis=1, keepdims=True)
    run = carry_ref[...]
    before = []
    for c in range(n_blk):
        before.append(within[c * N_EXPERTS:(c + 1) * N_EXPERTS, :] + run)
        run = run + totals[c * N_EXPERTS:(c + 1) * N_EXPERTS, :]
    before = jnp.concatenate(before, axis=1)
    rank1 = jnp.sum(oh1 * before, axis=0, keepdims=True)
    rank2 = jnp.sum(oh2 * before, axis=0, keepdims=True)
    carry_ref[...] = run
    cnt_ref[...] = jnp.broadcast_to(run, cnt_ref.shape)

    route_ref[0:1, :] = e1.astype(I32)
    route_ref[1:2, :] = e2.astype(I32)
    route_ref[2:3, :] = rank1.astype(I32)
    route_ref[3:4, :] = rank2.astype(I32)
    route_ref[4:5, :] = lax.bitcast_convert_type(gate1, I32)
    route_ref[5:6, :] = lax.bitcast_convert_type(gate2, I32)
    route_ref[3 * TOP_K:ROUTE_ROWS, :] = jnp.zeros((ROUTE_ROWS - 3 * TOP_K, ts), I32)


def _xattn_router(x, g_x, qk_mem, vo_mem, layer, g_f, w_grp, b_grp, w_exp, b_exp):
    b, s, d = x.shape
    hm = qk_mem.shape[3]
    t = b * s
    ts = min(TILE_XATTN, s)
    assert s % ts == 0 and ts % LANES == 0
    nj = s // ts
    row = lambda v: v.reshape(1, -1)
    const = lambda shape: pl.BlockSpec(shape, lambda i, j: (0,) * len(shape))
    pad = ROUTER_ROWS - N_GROUPS - N_EXPERTS
    w_r = jnp.pad(jnp.concatenate([w_exp, w_grp], axis=1).T, ((0, pad), (0, 0)))
    w_r_hi = w_r.astype(BF16)
    w_r_split = jnp.concatenate([w_r_hi, (w_r - w_r_hi.astype(F32)).astype(BF16)], axis=0)
    b_r = jnp.pad(jnp.concatenate([b_exp, b_grp]), (0, pad)).reshape(ROUTER_ROWS, 1)
    ti = jnp.arange(LANES)
    upper = (ti[:, None] < ti[None, :]).astype(BF16)
    return pl.pallas_call(
        _xattn_router_kernel,
        out_shape=(
            jax.ShapeDtypeStruct(x.shape, F32),
            jax.ShapeDtypeStruct((t, d // 2), U32),
            jax.ShapeDtypeStruct((ROUTE_ROWS, t), I32),
            jax.ShapeDtypeStruct((N_EXPERTS, LANES), F32),
        ),
        grid=(b, nj),
        in_specs=[
            pl.BlockSpec((1, ts, d), lambda i, j: (i, j, 0)),
            const((1, d)),
            pl.BlockSpec((1, 1, d, hm), lambda i, j: (layer, i, 0, 0)),
            pl.BlockSpec((1, 1, hm, d), lambda i, j: (layer, i, 0, 0)),
            const((1, d)), const((2 * ROUTER_ROWS, d)), const((ROUTER_ROWS, 1)), const((LANES, LANES)),
        ],
        out_specs=(
            pl.BlockSpec((1, ts, d), lambda i, j: (i, j, 0)),
            pl.BlockSpec((ts, d // 2), lambda i, j: (i * nj + j, 0)),
            pl.BlockSpec((ROUTE_ROWS, ts), lambda i, j: (0, i * nj + j)),
            pl.BlockSpec((N_EXPERTS, LANES), lambda i, j: (0, 0)),
        ),
        scratch_shapes=[pltpu.VMEM((N_EXPERTS, 1), F32)],
        compiler_params=_cparams(("arbitrary", "arbitrary")),
        name="xattn_router",
    )(x, row(g_x), qk_mem, vo_mem, row(g_f), w_r_split, b_r, upper)


def _sc_mesh():
    return plsc.VectorSubcoreMesh(core_axis_name="c", subcore_axis_name="s",
                                  num_cores=SC_CORES, num_subcores=SC_SUBCORES)


def _sc_worker():
    return lax.axis_index("s") * SC_CORES + lax.axis_index("c")


def _dispatch(h_packed, dest, n_rows):
    t, w = h_packed.shape
    chunk = SC_DISPATCH_CHUNK
    per_worker = t // SC_WORKERS
    n_chunks = per_worker // chunk
    assert n_chunks % 2 == 0 and n_chunks * chunk * SC_WORKERS == t
    dest = dest.reshape(TOP_K, SC_WORKERS * n_chunks, chunk)
    rows_buf = pltpu.VMEM((chunk, w), U32)

    @functools.partial(
        pl.kernel, mesh=_sc_mesh(),
        out_type=jax.ShapeDtypeStruct((n_rows, w), U32),
        scratch_types=[pltpu.VMEM((n_chunks, chunk), I32), pltpu.VMEM((n_chunks, chunk), I32), rows_buf, rows_buf,
                       pltpu.SemaphoreType.DMA((2,)), pltpu.SemaphoreType.DMA((2, TOP_K))],
        name="moe_dispatch_sc",
    )
    def run(h_hbm, d0_hbm, d1_hbm, xbuf_hbm, idx0_v, idx1_v, buf_a, buf_b, read_sem, write_sem):
        wid = _sc_worker()
        pltpu.sync_copy(d0_hbm.at[pl.ds(wid * n_chunks, n_chunks)], idx0_v)
        pltpu.sync_copy(d1_hbm.at[pl.ds(wid * n_chunks, n_chunks)], idx1_v)

        @pl.loop(0, n_chunks, step=2)
        def _(i):
            reads = [pltpu.async_copy(h_hbm.at[pl.ds(wid * per_worker + (i + j) * chunk, chunk)], buf, read_sem.at[j])
                     for j, buf in enumerate((buf_a, buf_b))]
            writes = []
            for j, buf in enumerate((buf_a, buf_b)):
                reads[j].wait()
                writes.append(pltpu.async_copy(buf, xbuf_hbm.at[idx0_v.at[i + j]], write_sem.at[j, 0]))
                writes.append(pltpu.async_copy(buf, xbuf_hbm.at[idx1_v.at[i + j]], write_sem.at[j, 1]))
            for copy in writes:
                copy.wait()

    return run(h_packed, dest[0], dest[1])


def _gather_pairs(y_buf, dest):
    t = dest.shape[1]
    w = y_buf.shape[1]
    chunk = SC_GATHER_CHUNK
    per_worker = t // SC_WORKERS
    n_chunks = per_worker // chunk
    assert n_chunks % 2 == 0 and n_chunks * chunk * SC_WORKERS == t
    dest = dest.reshape(TOP_K, SC_WORKERS * n_chunks, chunk)
    out = jax.ShapeDtypeStruct((t, w), U32)
    rows_buf = pltpu.VMEM((chunk, w), U32)

    @functools.partial(
        pl.kernel, mesh=_sc_mesh(),
        out_type=(out, out),
        scratch_types=[pltpu.VMEM((n_chunks, chunk), I32), pltpu.VMEM((n_chunks, chunk), I32),
                       rows_buf, rows_buf, rows_buf, rows_buf, pltpu.SemaphoreType.DMA((2, TOP_K))],
        name="moe_gather_sc",
    )
    def run(y_hbm, d0_hbm, d1_hbm, y0_hbm, y1_hbm, idx0_v, idx1_v, buf_a0, buf_a1, buf_b0, buf_b1, sem):
        wid = _sc_worker()
        pltpu.sync_copy(d0_hbm.at[pl.ds(wid * n_chunks, n_chunks)], idx0_v)
        pltpu.sync_copy(d1_hbm.at[pl.ds(wid * n_chunks, n_chunks)], idx1_v)

        @pl.loop(0, n_chunks, step=2)
        def _(i):
            bufs = ((buf_a0, buf_a1), (buf_b0, buf_b1))
            gathers = [[pltpu.async_copy(y_hbm.at[idx_v.at[i + j]], bufs[j][k], sem.at[j, k])
                        for k, idx_v in enumerate((idx0_v, idx1_v))] for j in range(2)]
            stores = []
            for j in range(2):
                rows = pl.ds(wid * per_worker + (i + j) * chunk, chunk)
                for k, out_hbm in enumerate((y0_hbm, y1_hbm)):
                    gathers[j][k].wait()
                    stores.append(pltpu.async_copy(bufs[j][k], out_hbm.at[rows], sem.at[j, k]))
            for copy in stores:
                copy.wait()

    return run(y_buf, dest[0], dest[1])


def _expert_kernel(be_ref, bf_ref, bv_ref, slot_ref, next_ref, x_ref, wg_hbm, wu_hbm, wd_hbm, y_ref,
                   wg_f32, wu_f32, wd_f32, wg_bf, wu_bf, wd_bf, sems, *, layer):
    blk = pl.program_id(0)
    valid = bv_ref[blk]

    def weight_copies(expert, slot):
        return [pltpu.make_async_copy(src.at[layer, expert], dst.at[slot], sems.at[slot, i])
                for i, (src, dst) in enumerate(((wg_hbm, wg_f32), (wu_hbm, wu_f32), (wd_hbm, wd_f32)))]

    @pl.when(blk == 0)
    def _():
        for copy in weight_copies(be_ref[0], 0):
            copy.start()

    @pl.when(bf_ref[blk] == 1)
    def _():
        slot = slot_ref[blk]
        for copy in weight_copies(be_ref[blk], slot):
            copy.wait()

        @pl.when(next_ref[blk] >= 0)
        def _():
            for copy in weight_copies(next_ref[blk], 1 - slot):
                copy.start()

        wg_bf[...] = wg_f32[slot].astype(BF16)
        wu_bf[...] = wu_f32[slot].astype(BF16)
        wd_bf[...] = wd_f32[slot].astype(BF16)

    @pl.when(valid > 0)
    def _():
        live = lax.broadcasted_iota(I32, x_ref.shape, 0) < valid
        xb = _unpack_bf16_pairs(jnp.where(live, x_ref[...], jnp.uint32(0))).astype(BF16)
        gt = _dot(xb, wg_bf[...])
        up = _dot(xb, wu_bf[...])
        act = (gt * _sigmoid(gt) * up).astype(BF16)
        y_ref[...] = _pack_bf16_pairs(_dot(act, wd_bf[...]))

    @pl.when(valid <= 0)
    def _():
        y_ref[...] = jnp.zeros(y_ref.shape, U32)


def _experts(x_buf, block_expert, block_first, block_valid, w_gate, w_up, w_down, layer):
    n_rows, w = x_buf.shape
    d, de = w_gate.shape[2], w_gate.shape[3]
    bm = MOE_BLOCK_ROWS
    n = n_rows // bm
    block_slot = (jnp.cumsum(block_first) - 1) % 2
    idx = jnp.arange(n, dtype=I32)
    later_first = jnp.concatenate([jnp.where(block_first[1:] == 1, idx[1:], n), jnp.full((1,), n, I32)])
    next_first = lax.cummin(later_first, reverse=True)
    block_next = jnp.where(next_first < n, block_expert[jnp.minimum(next_first, n - 1)], -1).astype(I32)
    any_space = pl.BlockSpec(memory_space=pl.ANY)
    grid_spec = pltpu.PrefetchScalarGridSpec(
        num_scalar_prefetch=5,
        grid=(n,),
        in_specs=[pl.BlockSpec((bm, w), lambda i, *_: (i, 0)), any_space, any_space, any_space],
        out_specs=pl.BlockSpec((bm, w), lambda i, *_: (i, 0)),
        scratch_shapes=[pltpu.VMEM((2, d, de), F32), pltpu.VMEM((2, d, de), F32), pltpu.VMEM((2, de, d), F32),
                        pltpu.VMEM((d, de), BF16), pltpu.VMEM((d, de), BF16), pltpu.VMEM((de, d), BF16),
                        pltpu.SemaphoreType.DMA((2, 3))],
    )
    return pl.pallas_call(
        functools.partial(_expert_kernel, layer=layer),
        out_shape=jax.ShapeDtypeStruct((n_rows, w), U32),
        grid_spec=grid_spec,
        compiler_params=_cparams(("arbitrary",)),
        name="moe_experts",
    )(block_expert, block_first, block_valid, block_slot.astype(I32), block_next, x_buf, w_gate, w_up, w_down)


def _combine_kernel(x_ref, gate_ref, y0_ref, y1_ref, gfin_ref, *rest, final_norm):
    o_ref = rest[-1]
    out = _moe_residual(x_ref[...], gate_ref[...], y0_ref[...], y1_ref[...])
    if final_norm:
        out = _rms(out, gfin_ref[...])
    o_ref[...] = out


def _combine(x2, y0, y1, gates, g_final, final_norm, first_row=0, out_so_far=None):
    t, d = x2.shape
    rows, w = y0.shape
    ts = min(TILE_COMBINE, rows)
    assert rows % ts == 0 and first_row % ts == 0
    off = first_row // ts
    in_specs = [
        pl.BlockSpec((ts, d), lambda i: (i + off, 0)),
        pl.BlockSpec((ROUTE_ROWS, ts), lambda i: (0, i + off)),
        pl.BlockSpec((ts, w), lambda i: (i, 0)),
        pl.BlockSpec((ts, w), lambda i: (i, 0)),
        pl.BlockSpec((1, d), lambda i: (0, 0)),
    ]
    args = [x2, gates, y0, y1, g_final.reshape(1, d)]
    aliases = {}
    if out_so_far is not None:
        in_specs.append(pl.BlockSpec(memory_space=pl.ANY))
        args.append(out_so_far)
        aliases = {len(args) - 1: 0}
    return pl.pallas_call(
        functools.partial(_combine_kernel, final_norm=final_norm),
        out_shape=jax.ShapeDtypeStruct((t, d), F32),
        grid=(rows // ts,),
        in_specs=in_specs,
        out_specs=pl.BlockSpec((ts, d), lambda i: (i + off, 0)),
        input_output_aliases=aliases,
        compiler_params=_cparams(("arbitrary",)),
        name="moe_combine",
    )(*args)


def _slots_kernel(off_ref, route_ref, dest_ref):
    expert = route_ref[0:TOP_K, :]
    start = jnp.zeros(expert.shape, I32)
    for e in range(N_EXPERTS):
        start = jnp.where(expert == e, off_ref[e], start)
    dest_ref[...] = start + route_ref[TOP_K:2 * TOP_K, :]


def _slots(route, pad_off):
    t = route.shape[1]
    ts = min(TILE_SLOTS, t)
    assert t % ts == 0
    grid_spec = pltpu.PrefetchScalarGridSpec(
        num_scalar_prefetch=1,
        grid=(t // ts,),
        in_specs=[pl.BlockSpec((ROUTE_ROWS, ts), lambda i, off: (0, i))],
        out_specs=pl.BlockSpec((TOP_K, ts), lambda i, off: (0, i)),
    )
    return pl.pallas_call(
        _slots_kernel,
        out_shape=jax.ShapeDtypeStruct((TOP_K, t), I32),
        grid_spec=grid_spec,
        compiler_params=_cparams(("arbitrary",)),
        name="moe_slots",
    )(pad_off, route)


def _moe_layout(route, counts):
    bm = MOE_BLOCK_ROWS
    t = route.shape[1]
    assert (t * TOP_K) % bm == 0
    n_blocks = (t * TOP_K) // bm + N_EXPERTS
    cnt = counts[:, 0].astype(I32)
    padded = (cnt + bm - 1) // bm * bm
    pad_ends = jnp.cumsum(padded)
    pad_off = pad_ends - padded
    experts = jnp.arange(N_EXPERTS, dtype=I32)
    dest = _slots(route, pad_off.astype(I32))
    gates = route
    starts = jnp.arange(n_blocks, dtype=I32) * bm
    block_expert = jnp.minimum(jnp.sum((pad_ends[None, :] <= starts[:, None]).astype(I32), axis=1),
                               N_EXPERTS - 1)
    block_first = jnp.concatenate([jnp.ones((1,), I32), (block_expert[1:] != block_expert[:-1]).astype(I32)])
    own = block_expert[:, None] == experts
    block_valid = jnp.clip(jnp.sum(jnp.where(own, cnt + pad_off, 0), axis=1) - starts, 0, bm)
    block_valid = jnp.where(starts < pad_ends[-1], block_valid, 0).astype(I32)
    return dest, gates, block_expert, block_first, block_valid, n_blocks * bm


def kernel(x, mem, norm_mix, norm_xattn, norm_ffn, norm_mem, norm_final, conv_w_in, conv_b_in, conv_w_dw,
           conv_b_dw, conv_ln_g, conv_ln_b, conv_w_out, conv_b_out, gla_w_in, gla_w_a2, gla_b_a, gla_norm_g,
           gla_w_o, xa_w_q, xa_w_kv, xa_w_o, moe_w_grp, moe_b_grp, moe_w_exp, moe_b_exp, moe_w_gate, moe_w_up,
           moe_w_down):
    b, s, d = x.shape
    depth = norm_mix.shape[0]
    qk_mem, vo_mem = _mem_kv(mem, norm_mem, xa_w_kv, xa_w_q, xa_w_o)
    moe = None
    for i in range(depth):
        j = i // 2
        if i % 2 == 0:
            if moe is not None:
                x = _combine(x.reshape(b * s, d), moe[1], moe[2], moe[0], norm_final, False).reshape(b, s, d)
            x = _conv_mixer(x, norm_mix[i], conv_w_in[j], conv_b_in[j], conv_w_dw[j], conv_b_dw[j],
                            conv_ln_g[j], conv_ln_b[j], conv_w_out[j], conv_b_out[j])
        else:
            x = _gla_mixer(x, *moe, norm_mix[i], gla_w_in[j], gla_w_a2[j], gla_b_a[j], gla_norm_g[j], gla_w_o[j])
        x2, h_packed, route, counts = _xattn_router(
            x, norm_xattn[i], qk_mem, vo_mem, i, norm_ffn[i],
            moe_w_grp[i], moe_b_grp[i], moe_w_exp[i], moe_b_exp[i])
        dest, gates, block_expert, block_first, block_valid, n_rows = _moe_layout(route, counts)
        x_buf = _dispatch(h_packed, dest, n_rows)
        y_buf = _experts(x_buf, block_expert, block_first, block_valid, moe_w_gate, moe_w_up, moe_w_down, i)
        if i < depth - 1:
            y0, y1 = _gather_pairs(y_buf, dest)
            x, moe = x2, (gates, y0, y1)
    part = (b * s) // FINAL_PARTS
    out = None
    for p in range(FINAL_PARTS):
        y0, y1 = _gather_pairs(y_buf, dest[:, p * part:(p + 1) * part])
        out = _combine(x2.reshape(b * s, d), y0, y1, gates, norm_final, True, p * part, out)
    return out.reshape(b, s, d)
```

```python
import functools

import jax
import jax.numpy as jnp
from jax import lax
from jax.experimental import pallas as pl
from jax.experimental.pallas import tpu as pltpu
from jax.experimental.pallas import tpu_sc as plsc

F32 = jnp.float32
BF16 = jnp.bfloat16
I32 = jnp.int32
U32 = jnp.uint32

EPS = 1e-6
CONV_KERNEL = 31
CONV_CARRY = 32
CONV_ROWS = 64
CONV_COLS = 256
SUBLANES = 8
LANES = 128
GLA_HEADS = 4
GLA_RANK = 16
GLA_RANK_PAD = 128
GLA_TAU = 16.0
GLA_LEAF = 32
XATTN_HEADS = 4
N_GROUPS = 4
EXPERTS_PER_GROUP = 8
N_EXPERTS = N_GROUPS * EXPERTS_PER_GROUP
ROUTER_ROWS = 40
TOP_K = 2
ROUTE_ROWS = 8

TILE_CONV = 512
TILE_GLA = 256
TILE_XATTN = 1024
TILE_COMBINE = 512
FINAL_PARTS = 2
SC_CORES = 2
SC_SUBCORES = 16
SC_WORKERS = SC_CORES * SC_SUBCORES
SC_DISPATCH_CHUNK = 64
SC_GATHER_CHUNK = 32
MOE_BLOCK_ROWS = 512
VMEM_LIMIT = 56 * 1024 * 1024


def _cparams(sem):
    return pltpu.CompilerParams(dimension_semantics=sem, vmem_limit_bytes=VMEM_LIMIT)


def _rms(x, g):
    return x * lax.rsqrt(jnp.mean(x * x, axis=-1, keepdims=True) + EPS) * g


def _sigmoid(x):
    return 0.5 * jnp.tanh(0.5 * x) + 0.5


def _dot(a, b):
    return jnp.dot(a, b, preferred_element_type=F32)


def _dot_nt(a, b):
    return lax.dot_general(a, b, (((1,), (1,)), ((), ())), preferred_element_type=F32)


def _dot_tn(a, b):
    return lax.dot_general(a, b, (((0,), (0,)), ((), ())), preferred_element_type=F32)


def _pack_bf16_pairs(x):
    w = x.shape[1] // 2
    hi = lax.bitcast_convert_type(x[:, :w].astype(BF16).astype(F32), U32)
    lo = lax.bitcast_convert_type(x[:, w:].astype(BF16).astype(F32), U32)
    return hi | (lo >> 16)


def _unpack_bf16_pairs(p):
    hi = lax.bitcast_convert_type(p & jnp.uint32(0xFFFF0000), F32)
    lo = lax.bitcast_convert_type(p << 16, F32)
    return jnp.concatenate([hi, lo], axis=1)


def _memkv_kernel(mem_ref, g_ref, wkv_ref, wq_ref, wo_ref, qk_ref, vo_ref):
    nm, d = mem_ref.shape[1], mem_ref.shape[2]
    hd = d // XATTN_HEADS
    mn = _rms(mem_ref[0], g_ref[...]).astype(BF16)
    kv = _dot(mn, wkv_ref[0])
    k = kv[:, :d].astype(BF16)
    v = kv[:, d:].astype(BF16)
    for a in range(XATTN_HEADS):
        sl = slice(a * hd, (a + 1) * hd)
        qk_ref[0, 0, :, a * nm:(a + 1) * nm] = (_dot_nt(wq_ref[0, :, sl], k[:, sl]) * (hd ** -0.5)).astype(BF16)
        vo_ref[0, 0, a * nm:(a + 1) * nm, :] = _dot(v[:, sl], wo_ref[0, sl, :]).astype(BF16)


def _mem_kv(mem, norm_mem, w_kv, w_q, w_o):
    b, nm, d = mem.shape
    depth = w_kv.shape[0]
    per_layer = lambda shape: pl.BlockSpec((1,) + shape, lambda l, i: (l, 0, 0))
    return pl.pallas_call(
        _memkv_kernel,
        out_shape=(jax.ShapeDtypeStruct((depth, b, d, XATTN_HEADS * nm), BF16),
                   jax.ShapeDtypeStruct((depth, b, XATTN_HEADS * nm, d), BF16)),
        grid=(depth, b),
        in_specs=[
            pl.BlockSpec((1, nm, d), lambda l, i: (i, 0, 0)),
            pl.BlockSpec((1, d), lambda l, i: (0, 0)),
            per_layer((d, 2 * d)), per_layer((d, d)), per_layer((d, d)),
        ],
        out_specs=(
            pl.BlockSpec((1, 1, d, XATTN_HEADS * nm), lambda l, i: (l, i, 0, 0)),
            pl.BlockSpec((1, 1, XATTN_HEADS * nm, d), lambda l, i: (l, i, 0, 0)),
        ),
        compiler_params=_cparams(("arbitrary", "arbitrary")),
        name="mem_kv",
    )(mem, norm_mem.reshape(1, d), w_kv.astype(BF16), w_q.astype(BF16), w_o.astype(BF16))


def _conv_kernel(x_ref, g_ref, win_ref, bin_ref, wdw_ref, bdw_ref, lng_ref, lnb_ref, wout_ref, bout_ref,
                 o_ref, ext_ref, conv_ref):
    ts, d = x_ref.shape[1], x_ref.shape[2]

    @pl.when(pl.program_id(1) == 0)
    def _():
        ext_ref[...] = jnp.zeros(ext_ref.shape, F32)

    x = x_ref[0]
    h = _rms(x, g_ref[...]).astype(BF16)
    u = _dot(h, win_ref[...]) + bin_ref[...]
    glu = u[:, :d] * _sigmoid(u[:, d:])
    for b in range(SUBLANES):
        ext_ref[b, CONV_CARRY - b:CONV_CARRY - b + ts, :] = glu

    first = CONV_CARRY - (CONV_KERNEL - 1)

    def chunk(i, carry):
        r0 = pl.multiple_of(i * CONV_ROWS, CONV_ROWS)
        for c0 in range(0, d, CONV_COLS):
            cols = slice(c0, c0 + CONV_COLS)
            acc = [jnp.zeros((SUBLANES, CONV_COLS), F32) for _ in range(CONV_ROWS // SUBLANES)]
            for k in range(CONV_KERNEL):
                b = (first + k) % SUBLANES
                wk = wdw_ref[k * SUBLANES:(k + 1) * SUBLANES, cols]
                for j in range(CONV_ROWS // SUBLANES):
                    rows = pl.ds(r0 + (first + k - b) + j * SUBLANES, SUBLANES)
                    acc[j] = acc[j] + ext_ref[b, rows, cols] * wk
            for j in range(CONV_ROWS // SUBLANES):
                conv_ref[pl.ds(r0 + j * SUBLANES, SUBLANES), cols] = acc[j]
        return carry

    lax.fori_loop(0, ts // CONV_ROWS, chunk, 0)
    for b in range(SUBLANES):
        ext_ref[b, 0:CONV_CARRY, :] = ext_ref[b, ts:ts + CONV_CARRY, :]

    c = conv_ref[...] + bdw_ref[...]
    mu = jnp.mean(c, axis=-1, keepdims=True)
    cc = c - mu
    var = jnp.mean(cc * cc, axis=-1, keepdims=True)
    un = cc * lax.rsqrt(var + EPS) * lng_ref[...] + lnb_ref[...]
    act = (un * _sigmoid(un)).astype(BF16)
    o_ref[0] = x + _dot(act, wout_ref[...]) + bout_ref[...]


def _conv_mixer(x, g, w_in, b_in, w_dw, b_dw, ln_g, ln_b, w_out, b_out):
    b, s, d = x.shape
    ts = min(TILE_CONV, s)
    assert s % ts == 0 and ts % CONV_ROWS == 0 and d % CONV_COLS == 0
    row = lambda v: v.reshape(1, -1)
    const = lambda shape: pl.BlockSpec(shape, lambda i, j: (0,) * len(shape))
    return pl.pallas_call(
        _conv_kernel,
        out_shape=jax.ShapeDtypeStruct(x.shape, F32),
        grid=(b, s // ts),
        in_specs=[
            pl.BlockSpec((1, ts, d), lambda i, j: (i, j, 0)),
            const((1, d)), const((d, 2 * d)), const((1, 2 * d)), const((CONV_KERNEL * SUBLANES, d)), const((1, d)),
            const((1, d)), const((1, d)), const((d, d)), const((1, d)),
        ],
        out_specs=pl.BlockSpec((1, ts, d), lambda i, j: (i, j, 0)),
        scratch_shapes=[pltpu.VMEM((SUBLANES, CONV_CARRY + ts, d), F32), pltpu.VMEM((ts, d), F32)],
        compiler_params=_cparams(("arbitrary", "arbitrary")),
        name="conv_mixer",
    )(x, row(g), w_in.astype(BF16), row(b_in), jnp.repeat(w_dw, SUBLANES, axis=0), row(b_dw), row(ln_g), row(ln_b),
      w_out.astype(BF16), row(b_out))


def _gla_levels(ts):
    sizes = [GLA_LEAF]
    while sizes[-1] < ts:
        sizes.append(sizes[-1] * 2)
    return sizes


def _moe_residual(x, route, y0, y1):
    g = lax.bitcast_convert_type(route, F32)
    g = jnp.concatenate([g, jnp.zeros((LANES - g.shape[0], g.shape[1]), F32)], axis=0).T
    gate0 = g[:, 2 * TOP_K:2 * TOP_K + 1]
    gate1 = g[:, 2 * TOP_K + 1:2 * TOP_K + 2]
    return x + _unpack_bf16_pairs(y0) * gate0 + _unpack_bf16_pairs(y1) * gate1


def _gla_kernel(x_ref, gate_ref, y0_ref, y1_ref, g_ref, wq_ref, wk_ref, wv_ref, wa_ref, wr_ref, wa2_ref, ba_ref,
                ng_ref, wo_ref, o_ref, state_ref, x_s, q_s, k_s, v_s, og_s, la_s, *, tiles_per_seq):
    j = pl.program_id(0)
    dkh = wq_ref.shape[1] // GLA_HEADS

    @pl.when(j == 0)
    def _():
        for ref in (x_s, q_s, k_s, og_s, la_s):
            ref[...] = jnp.zeros(ref.shape, F32)
        v_s[...] = jnp.zeros(v_s.shape, BF16)

    @pl.when((j == 0) | ((j - 1) % tiles_per_seq == 0))
    def _():
        state_ref[...] = jnp.zeros(state_ref.shape, F32)

    def stage_a(slot):
        x = _moe_residual(x_ref[0], gate_ref[...], y0_ref[...], y1_ref[...])
        h = _rms(x, g_ref[...]).astype(BF16)
        x_s[slot] = x
        q_s[slot] = _dot(h, wq_ref[...]) * (dkh ** -0.5)
        k_s[slot] = _dot(h, wk_ref[...])
        v_s[slot] = _dot(h, wv_ref[...]).astype(BF16)
        r = _dot(h, wr_ref[...])
        og_s[slot] = r * _sigmoid(r)
        a = _dot(h, wa_ref[...]).astype(BF16)
        z = _dot(jnp.concatenate([a, a], axis=1), wa2_ref[...]) + ba_ref[...]
        la_s[slot] = -(jnp.maximum(-z, 0.0) + jnp.log(1.0 + jnp.exp(-jnp.abs(z)))) * (1.0 / GLA_TAU)

    def stage_b(slot):
        _gla_recurrence(x_s[slot], q_s[slot], k_s[slot], v_s[slot], og_s[slot], la_s[slot],
                        ng_ref, wo_ref, o_ref, state_ref)

    @pl.when(j % 2 == 0)
    def _():
        stage_b(1)
        stage_a(0)

    @pl.when(j % 2 == 1)
    def _():
        stage_b(0)
        stage_a(1)


def _gla_recurrence(x, q, k, v, out_gate, log_a, ng_ref, wo_ref, o_ref, state_ref):
    ts, d = x.shape
    dk = q.shape[1]
    dkh = dk // GLA_HEADS
    dvh = d // GLA_HEADS

    row = lax.broadcasted_iota(I32, (ts, ts), 0)
    col = lax.broadcasted_iota(I32, (ts, ts), 1)
    tri = jnp.where(col <= row, 1.0, 0.0).astype(BF16)
    bcum = _dot(tri, log_a.astype(BF16))
    b_last = bcum[ts - 1:ts, :]

    q_in = (q * jnp.exp(bcum)).astype(BF16)
    k_out = (k * jnp.exp(b_last - bcum)).astype(BF16)

    sizes = _gla_levels(ts)
    scores = [None] * GLA_HEADS
    for lvl, size in enumerate(sizes):
        half = size // 2
        same_block = (row & -size) == (col & -size)
        if lvl == 0:
            pair = same_block & (col <= row)
            q_ok = k_ok = None
        else:
            pair = same_block & ((row & (size - 1)) >= half) & ((col & (size - 1)) < half)
            pos = lax.broadcasted_iota(I32, (ts, dk), 0) & (size - 1)
            q_ok = pos >= half
            k_ok = pos < half
        ref = jnp.concatenate(
            [jnp.broadcast_to(bcum[r0 + half:r0 + half + 1, :], (size, dk)) for r0 in range(0, ts, size)], axis=0)
        ql = q * jnp.exp(bcum - ref)
        kl = k * jnp.exp(ref - bcum)
        if q_ok is not None:
            ql = jnp.where(q_ok, ql, 0.0)
            kl = jnp.where(k_ok, kl, 0.0)
        ql = ql.astype(BF16)
        kl = kl.astype(BF16)
        for hd in range(GLA_HEADS):
            c0 = hd * dkh
            a = _dot_nt(ql[:, c0:c0 + dkh], kl[:, c0:c0 + dkh])
            a = jnp.where(pair, a, 0.0)
            scores[hd] = a if scores[hd] is None else scores[hd] + a

    outs = []
    for hd in range(GLA_HEADS):
        c0 = hd * dkh
        v_h = v[:, hd * dvh:(hd + 1) * dvh]
        st = state_ref[hd]
        o_h = _dot(scores[hd].astype(BF16), v_h) + _dot_nt(q_in[:, c0:c0 + dkh], st.astype(BF16))
        decay = jnp.exp(b_last[:, c0:c0 + dkh])
        state_ref[hd] = st * decay + _dot_tn(v_h, k_out[:, c0:c0 + dkh])
        o_h = o_h * lax.rsqrt(jnp.mean(o_h * o_h, axis=-1, keepdims=True) + EPS) * ng_ref[...]
        outs.append(o_h)
    o = jnp.concatenate(outs, axis=1) * out_gate
    o_ref[0] = x + _dot(o.astype(BF16), wo_ref[...])


def _gla_split_kernel(w_ref, wq_ref, wk_ref, wv_ref, wa_ref, wr_ref):
    dk, d = wq_ref.shape[1], wv_ref.shape[1]
    w = w_ref[...]
    wq_ref[...] = w[:, 0:dk].astype(BF16)
    wk_ref[...] = w[:, dk:2 * dk].astype(BF16)
    wv_ref[...] = w[:, 2 * dk:2 * dk + d].astype(BF16)
    a_tile = w[:, 2 * dk + d:2 * dk + d + GLA_RANK_PAD]
    lane = lax.broadcasted_iota(I32, a_tile.shape, 1)
    wa_ref[...] = jnp.where(lane < GLA_RANK, a_tile, 0.0).astype(BF16)
    wr_ref[...] = w[:, 2 * dk + d + GLA_RANK:2 * dk + 2 * d + GLA_RANK].astype(BF16)


def _gla_split_weights(w_in, dk):
    d, cols = w_in.shape
    rows = min(TILE_GLA, d)
    assert d % rows == 0 and cols == 2 * dk + 2 * d + GLA_RANK
    part = lambda n: jax.ShapeDtypeStruct((d, n), BF16)
    block = lambda n: pl.BlockSpec((rows, n), lambda i: (i, 0))
    return pl.pallas_call(
        _gla_split_kernel,
        out_shape=(part(dk), part(dk), part(d), part(GLA_RANK_PAD), part(d)),
        grid=(d // rows,),
        in_specs=[block(cols)],
        out_specs=(block(dk), block(dk), block(d), block(GLA_RANK_PAD), block(d)),
        compiler_params=_cparams(("arbitrary",)),
        name="gla_split_weights",
    )(w_in)


def _gla_mixer(x, gates, y0, y1, g, w_in, w_a2, b_a, norm_g, w_o):
    b, s, d = x.shape
    dk = w_a2.shape[1]
    ts = min(TILE_GLA, s)
    assert s % ts == 0 and ts % GLA_LEAF == 0
    nj = s // ts
    row = lambda v: v.reshape(1, -1)
    const = lambda shape: pl.BlockSpec(shape, lambda j: (0,) * len(shape))
    wq, wk, wv, wa, wr = _gla_split_weights(w_in, dk)
    wa2 = jnp.pad(w_a2, ((0, GLA_RANK_PAD - GLA_RANK), (0, 0)))
    wa2_hi = wa2.astype(BF16)
    wa2_lo = (wa2 - wa2_hi.astype(F32)).astype(BF16)
    wa2_split = jnp.concatenate([wa2_hi, wa2_lo], axis=0)
    dvh = d // GLA_HEADS
    n = b * nj

    def tile(j, lag):
        return jnp.clip(j - lag, 0, n - 1)

    tokens = lambda lag: (lambda j: (tile(j, lag), 0))
    return pl.pallas_call(
        functools.partial(_gla_kernel, tiles_per_seq=nj),
        out_shape=jax.ShapeDtypeStruct(x.shape, F32),
        grid=(n + 1,),
        in_specs=[
            pl.BlockSpec((1, ts, d), lambda j: (tile(j, 0) // nj, tile(j, 0) % nj, 0)),
            pl.BlockSpec((ROUTE_ROWS, ts), lambda j: (0, tile(j, 0))),
            pl.BlockSpec((ts, d // 2), tokens(0)),
            pl.BlockSpec((ts, d // 2), tokens(0)),
            const((1, d)), const((d, dk)), const((d, dk)), const((d, d)), const((d, GLA_RANK_PAD)),
            const((d, d)), const((2 * GLA_RANK_PAD, dk)), const((1, dk)), const((1, dvh)), const((d, d)),
        ],
        out_specs=pl.BlockSpec((1, ts, d), lambda j: (tile(j, 1) // nj, tile(j, 1) % nj, 0)),
        scratch_shapes=[pltpu.VMEM((GLA_HEADS, dvh, dk // GLA_HEADS), F32),
                        pltpu.VMEM((2, ts, d), F32), pltpu.VMEM((2, ts, dk), F32), pltpu.VMEM((2, ts, dk), F32),
                        pltpu.VMEM((2, ts, d), BF16), pltpu.VMEM((2, ts, d), F32), pltpu.VMEM((2, ts, dk), F32)],
        compiler_params=_cparams(("arbitrary",)),
        name="gla_mixer",
    )(x, gates, y0, y1, row(g), wq, wk, wv, wa, wr, wa2_split, row(b_a), row(norm_g), w_o.astype(BF16))


def _xattn_router_kernel(x_ref, gx_ref, qk_ref, vo_ref, gf_ref, wr_ref, br_ref, upper_ref,
                         x_out_ref, h_out_ref, route_ref, cnt_ref, carry_ref):
    ts, d = x_ref.shape[1], x_ref.shape[2]
    nm = qk_ref.shape[3] // XATTN_HEADS
    first = (pl.program_id(0) == 0) & (pl.program_id(1) == 0)

    @pl.when(first)
    def _():
        carry_ref[...] = jnp.zeros(carry_ref.shape, F32)

    x = x_ref[0]
    h = _rms(x, gx_ref[...]).astype(BF16)
    scores = _dot(h, qk_ref[0, 0])
    probs = []
    for a in range(XATTN_HEADS):
        s = scores[:, a * nm:(a + 1) * nm]
        p = jnp.exp(s - jnp.max(s, axis=-1, keepdims=True))
        probs.append((p / jnp.sum(p, axis=-1, keepdims=True)).astype(BF16))
    x2 = x + _dot(jnp.concatenate(probs, axis=1), vo_ref[0, 0])
    x_out_ref[0] = x2

    hf = _rms(x2, gf_ref[...])
    h_out_ref[...] = _pack_bf16_pairs(hf)

    both = _dot_nt(wr_ref[...], hf.astype(BF16))
    logits = both[0:ROUTER_ROWS, :] + both[ROUTER_ROWS:2 * ROUTER_ROWS, :] + br_ref[...]
    gl = logits[N_EXPERTS:N_EXPERTS + N_GROUPS, :]
    gi = lax.broadcasted_iota(I32, gl.shape, 0).astype(F32)
    gmax = jnp.max(gl, axis=0, keepdims=True)
    g_sel = jnp.min(jnp.where(gl == gmax, gi, float(N_GROUPS)), axis=0, keepdims=True)
    pg_sel = 1.0 / jnp.sum(jnp.exp(gl - gmax), axis=0, keepdims=True)

    el = jnp.zeros((EXPERTS_PER_GROUP, ts), F32)
    for gidx in range(N_GROUPS):
        lo = gidx * EXPERTS_PER_GROUP
        el = jnp.where(g_sel == float(gidx), logits[lo:lo + EXPERTS_PER_GROUP, :], el)
    ei = lax.broadcasted_iota(I32, el.shape, 0).astype(F32)
    m1 = jnp.max(el, axis=0, keepdims=True)
    i1 = jnp.min(jnp.where(el == m1, ei, float(EXPERTS_PER_GROUP)), axis=0, keepdims=True)
    rest = jnp.where(ei == i1, -jnp.inf, el)
    m2 = jnp.max(rest, axis=0, keepdims=True)
    i2 = jnp.min(jnp.where(rest == m2, ei, float(EXPERTS_PER_GROUP)), axis=0, keepdims=True)
    ratio = jnp.exp(m2 - m1)
    gate1 = pg_sel / (1.0 + ratio)
    gate2 = pg_sel * ratio / (1.0 + ratio)
    e1 = g_sel * float(EXPERTS_PER_GROUP) + i1
    e2 = g_sel * float(EXPERTS_PER_GROUP) + i2

    xi = lax.broadcasted_iota(I32, (N_EXPERTS, ts), 0).astype(F32)
    oh1 = jnp.where(xi == e1, 1.0, 0.0)
    oh2 = jnp.where(xi == e2, 1.0, 0.0)
    oh = oh1 + oh2
    n_blk = ts // LANES
    stacked = jnp.concatenate([oh[:, c * LANES:(c + 1) * LANES] for c in range(n_blk)], axis=0)
    within = _dot(stacked.astype(BF16), upper_ref[...])
    totals = jnp.sum(stacked, axis=1, keepdims=True)
    run = carry_ref[...]
    before = []
    for c in range(n_blk):
        before.append(within[c * N_EXPERTS:(c + 1) * N_EXPERTS, :] + run)
        run = run + totals[c * N_EXPERTS:(c + 1) * N_EXPERTS, :]
    before = jnp.concatenate(before, axis=1)
    rank1 = jnp.sum(oh1 * before, axis=0, keepdims=True)
    rank2 = jnp.sum(oh2 * before, axis=0, keepdims=True)
    carry_ref[...] = run
    cnt_ref[...] = jnp.broadcast_to(run, cnt_ref.shape)

    route_ref[0:1, :] = e1.astype(I32)
    route_ref[1:2, :] = e2.astype(I32)
    route_ref[2:3, :] = rank1.astype(I32)
    route_ref[3:4, :] = rank2.astype(I32)
    route_ref[4:5, :] = lax.bitcast_convert_type(gate1, I32)
    route_ref[5:6, :] = lax.bitcast_convert_type(gate2, I32)
    route_ref[3 * TOP_K:ROUTE_ROWS, :] = jnp.zeros((ROUTE_ROWS - 3 * TOP_K, ts), I32)


def _xattn_router(x, g_x, qk_mem, vo_mem, layer, g_f, w_grp, b_grp, w_exp, b_exp):
    b, s, d = x.shape
    hm = qk_mem.shape[3]
    t = b * s
    ts = min(TILE_XATTN, s)
    assert s % ts == 0 and ts % LANES == 0
    nj = s // ts
    row = lambda v: v.reshape(1, -1)
    const = lambda shape: pl.BlockSpec(shape, lambda i, j: (0,) * len(shape))
    pad = ROUTER_ROWS - N_GROUPS - N_EXPERTS
    w_r = jnp.pad(jnp.concatenate([w_exp, w_grp], axis=1).T, ((0, pad), (0, 0)))
    w_r_hi = w_r.astype(BF16)
    w_r_split = jnp.concatenate([w_r_hi, (w_r - w_r_hi.astype(F32)).astype(BF16)], axis=0)
    b_r = jnp.pad(jnp.concatenate([b_exp, b_grp]), (0, pad)).reshape(ROUTER_ROWS, 1)
    ti = jnp.arange(LANES)
    upper = (ti[:, None] < ti[None, :]).astype(BF16)
    return pl.pallas_call(
        _xattn_router_kernel,
        out_shape=(
            jax.ShapeDtypeStruct(x.shape, F32),
            jax.ShapeDtypeStruct((t, d // 2), U32),
            jax.ShapeDtypeStruct((ROUTE_ROWS, t), I32),
            jax.ShapeDtypeStruct((N_EXPERTS, LANES), F32),
        ),
        grid=(b, nj),
        in_specs=[
            pl.BlockSpec((1, ts, d), lambda i, j: (i, j, 0)),
            const((1, d)),
            pl.BlockSpec((1, 1, d, hm), lambda i, j: (layer, i, 0, 0)),
            pl.BlockSpec((1, 1, hm, d), lambda i, j: (layer, i, 0, 0)),
            const((1, d)), const((2 * ROUTER_ROWS, d)), const((ROUTER_ROWS, 1)), const((LANES, LANES)),
        ],
        out_specs=(
            pl.BlockSpec((1, ts, d), lambda i, j: (i, j, 0)),
            pl.BlockSpec((ts, d // 2), lambda i, j: (i * nj + j, 0)),
            pl.BlockSpec((ROUTE_ROWS, ts), lambda i, j: (0, i * nj + j)),
            pl.BlockSpec((N_EXPERTS, LANES), lambda i, j: (0, 0)),
        ),
        scratch_shapes=[pltpu.VMEM((N_EXPERTS, 1), F32)],
        compiler_params=_cparams(("arbitrary", "arbitrary")),
        name="xattn_router",
    )(x, row(g_x), qk_mem, vo_mem, row(g_f), w_r_split, b_r, upper)


def _sc_mesh():
    return plsc.VectorSubcoreMesh(core_axis_name="c", subcore_axis_name="s",
                                  num_cores=SC_CORES, num_subcores=SC_SUBCORES)


def _sc_worker():
    return lax.axis_index("s") * SC_CORES + lax.axis_index("c")


def _dispatch(h_packed, dest, n_rows):
    t, w = h_packed.shape
    chunk = SC_DISPATCH_CHUNK
    per_worker = t // SC_WORKERS
    n_chunks = per_worker // chunk
    assert n_chunks % 2 == 0 and n_chunks * chunk * SC_WORKERS == t
    dest = dest.reshape(TOP_K, SC_WORKERS * n_chunks, chunk)
    rows_buf = pltpu.VMEM((chunk, w), U32)

    @functools.partial(
        pl.kernel, mesh=_sc_mesh(),
        out_type=jax.ShapeDtypeStruct((n_rows, w), U32),
        scratch_types=[pltpu.VMEM((n_chunks, chunk), I32), pltpu.VMEM((n_chunks, chunk), I32), rows_buf, rows_buf,
                       pltpu.SemaphoreType.DMA((2,)), pltpu.SemaphoreType.DMA((2, TOP_K))],
        name="moe_dispatch_sc",
    )
    def run(h_hbm, d0_hbm, d1_hbm, xbuf_hbm, idx0_v, idx1_v, buf_a, buf_b, read_sem, write_sem):
        wid = _sc_worker()
        pltpu.sync_copy(d0_hbm.at[pl.ds(wid * n_chunks, n_chunks)], idx0_v)
        pltpu.sync_copy(d1_hbm.at[pl.ds(wid * n_chunks, n_chunks)], idx1_v)

        @pl.loop(0, n_chunks, step=2)
        def _(i):
            reads = [pltpu.async_copy(h_hbm.at[pl.ds(wid * per_worker + (i + j) * chunk, chunk)], buf, read_sem.at[j])
                     for j, buf in enumerate((buf_a, buf_b))]
            writes = []
            for j, buf in enumerate((buf_a, buf_b)):
                reads[j].wait()
                writes.append(pltpu.async_copy(buf, xbuf_hbm.at[idx0_v.at[i + j]], write_sem.at[j, 0]))
                writes.append(pltpu.async_copy(buf, xbuf_hbm.at[idx1_v.at[i + j]], write_sem.at[j, 1]))
            for copy in writes:
                copy.wait()

    return run(h_packed, dest[0], dest[1])


def _gather_pairs(y_buf, dest):
    t = dest.shape[1]
    w = y_buf.shape[1]
    chunk = SC_GATHER_CHUNK
    per_worker = t // SC_WORKERS
    n_chunks = per_worker // chunk
    assert n_chunks % 2 == 0 and n_chunks * chunk * SC_WORKERS == t
    dest = dest.reshape(TOP_K, SC_WORKERS * n_chunks, chunk)
    out = jax.ShapeDtypeStruct((t, w), U32)
    rows_buf = pltpu.VMEM((chunk, w), U32)

    @functools.partial(
        pl.kernel, mesh=_sc_mesh(),
        out_type=(out, out),
        scratch_types=[pltpu.VMEM((n_chunks, chunk), I32), pltpu.VMEM((n_chunks, chunk), I32),
                       rows_buf, rows_buf, rows_buf, rows_buf, pltpu.SemaphoreType.DMA((2, TOP_K))],
        name="moe_gather_sc",
    )
    def run(y_hbm, d0_hbm, d1_hbm, y0_hbm, y1_hbm, idx0_v, idx1_v, buf_a0, buf_a1, buf_b0, buf_b1, sem):
        wid = _sc_worker()
        pltpu.sync_copy(d0_hbm.at[pl.ds(wid * n_chunks, n_chunks)], idx0_v)
        pltpu.sync_copy(d1_hbm.at[pl.ds(wid * n_chunks, n_chunks)], idx1_v)

        @pl.loop(0, n_chunks, step=2)
        def _(i):
            bufs = ((buf_a0, buf_a1), (buf_b0, buf_b1))
            gathers = [[pltpu.async_copy(y_hbm.at[idx_v.at[i + j]], bufs[j][k], sem.at[j, k])
                        for k, idx_v in enumerate((idx0_v, idx1_v))] for j in range(2)]
            stores = []
            for j in range(2):
                rows = pl.ds(wid * per_worker + (i + j) * chunk, chunk)
                for k, out_hbm in enumerate((y0_hbm, y1_hbm)):
                    gathers[j][k].wait()
                    stores.append(pltpu.async_copy(bufs[j][k], out_hbm.at[rows], sem.at[j, k]))
            for copy in stores:
                copy.wait()

    return run(y_buf, dest[0], dest[1])


def _expert_kernel(be_ref, bf_ref, bv_ref, slot_ref, next_ref, x_ref, wg_hbm, wu_hbm, wd_hbm, y_ref,
                   wg_f32, wu_f32, wd_f32, wg_bf, wu_bf, wd_bf, sems, *, layer):
    blk = pl.program_id(0)
    valid = bv_ref[blk]

    def weight_copies(expert, slot):
        return [pltpu.make_async_copy(src.at[layer, expert], dst.at[slot], sems.at[slot, i])
                for i, (src, dst) in enumerate(((wg_hbm, wg_f32), (wu_hbm, wu_f32), (wd_hbm, wd_f32)))]

    @pl.when(blk == 0)
    def _():
        for copy in weight_copies(be_ref[0], 0):
            copy.start()

    @pl.when(bf_ref[blk] == 1)
    def _():
        slot = slot_ref[blk]
        for copy in weight_copies(be_ref[blk], slot):
            copy.wait()

        @pl.when(next_ref[blk] >= 0)
        def _():
            for copy in weight_copies(next_ref[blk], 1 - slot):
                copy.start()

        wg_bf[...] = wg_f32[slot].astype(BF16)
        wu_bf[...] = wu_f32[slot].astype(BF16)
        wd_bf[...] = wd_f32[slot].astype(BF16)

    @pl.when(valid > 0)
    def _():
        live = lax.broadcasted_iota(I32, x_ref.shape, 0) < valid
        xb = _unpack_bf16_pairs(jnp.where(live, x_ref[...], jnp.uint32(0))).astype(BF16)
        gt = _dot(xb, wg_bf[...])
        up = _dot(xb, wu_bf[...])
        act = (gt * _sigmoid(gt) * up).astype(BF16)
        y_ref[...] = _pack_bf16_pairs(_dot(act, wd_bf[...]))

    @pl.when(valid <= 0)
    def _():
        y_ref[...] = jnp.zeros(y_ref.shape, U32)


def _experts(x_buf, block_expert, block_first, block_valid, w_gate, w_up, w_down, layer):
    n_rows, w = x_buf.shape
    d, de = w_gate.shape[2], w_gate.shape[3]
    bm = MOE_BLOCK_ROWS
    n = n_rows // bm
    block_slot = (jnp.cumsum(block_first) - 1) % 2
    idx = jnp.arange(n, dtype=I32)
    later_first = jnp.concatenate([jnp.where(block_first[1:] == 1, idx[1:], n), jnp.full((1,), n, I32)])
    next_first = lax.cummin(later_first, reverse=True)
    block_next = jnp.where(next_first < n, block_expert[jnp.minimum(next_first, n - 1)], -1).astype(I32)
    any_space = pl.BlockSpec(memory_space=pl.ANY)
    grid_spec = pltpu.PrefetchScalarGridSpec(
        num_scalar_prefetch=5,
        grid=(n,),
        in_specs=[pl.BlockSpec((bm, w), lambda i, *_: (i, 0)), any_space, any_space, any_space],
        out_specs=pl.BlockSpec((bm, w), lambda i, *_: (i, 0)),
        scratch_shapes=[pltpu.VMEM((2, d, de), F32), pltpu.VMEM((2, d, de), F32), pltpu.VMEM((2, de, d), F32),
                        pltpu.VMEM((d, de), BF16), pltpu.VMEM((d, de), BF16), pltpu.VMEM((de, d), BF16),
                        pltpu.SemaphoreType.DMA((2, 3))],
    )
    return pl.pallas_call(
        functools.partial(_expert_kernel, layer=layer),
        out_shape=jax.ShapeDtypeStruct((n_rows, w), U32),
        grid_spec=grid_spec,
        compiler_params=_cparams(("arbitrary",)),
        name="moe_experts",
    )(block_expert, block_first, block_valid, block_slot.astype(I32), block_next, x_buf, w_gate, w_up, w_down)


def _combine_kernel(x_ref, gate_ref, y0_ref, y1_ref, gfin_ref, *rest, final_norm):
    o_ref = rest[-1]
    out = _moe_residual(x_ref[...], gate_ref[...], y0_ref[...], y1_ref[...])
    if final_norm:
        out = _rms(out, gfin_ref[...])
    o_ref[...] = out


def _combine(x2, y0, y1, gates, g_final, final_norm, first_row=0, out_so_far=None):
    t, d = x2.shape
    rows, w = y0.shape
    ts = min(TILE_COMBINE, rows)
    assert rows % ts == 0 and first_row % ts == 0
    off = first_row // ts
    in_specs = [
        pl.BlockSpec((ts, d), lambda i: (i + off, 0)),
        pl.BlockSpec((ROUTE_ROWS, ts), lambda i: (0, i + off)),
        pl.BlockSpec((ts, w), lambda i: (i, 0)),
        pl.BlockSpec((ts, w), lambda i: (i, 0)),
        pl.BlockSpec((1, d), lambda i: (0, 0)),
    ]
    args = [x2, gates, y0, y1, g_final.reshape(1, d)]
    aliases = {}
    if out_so_far is not None:
        in_specs.append(pl.BlockSpec(memory_space=pl.ANY))
        args.append(out_so_far)
        aliases = {len(args) - 1: 0}
    return pl.pallas_call(
        functools.partial(_combine_kernel, final_norm=final_norm),
        out_shape=jax.ShapeDtypeStruct((t, d), F32),
        grid=(rows // ts,),
        in_specs=in_specs,
        out_specs=pl.BlockSpec((ts, d), lambda i: (i + off, 0)),
        input_output_aliases=aliases,
        compiler_params=_cparams(("arbitrary",)),
        name="moe_combine",
    )(*args)


def _moe_layout(route, counts):
    bm = MOE_BLOCK_ROWS
    t = route.shape[1]
    assert (t * TOP_K) % bm == 0
    n_blocks = (t * TOP_K) // bm + N_EXPERTS
    cnt = counts[:, 0].astype(I32)
    padded = (cnt + bm - 1) // bm * bm
    pad_ends = jnp.cumsum(padded)
    pad_off = pad_ends - padded
    experts = jnp.arange(N_EXPERTS, dtype=I32)
    hit = route[0:TOP_K, :, None] == experts
    dest = jnp.sum(jnp.where(hit, pad_off, 0), axis=-1) + route[TOP_K:2 * TOP_K]
    gates = route
    starts = jnp.arange(n_blocks, dtype=I32) * bm
    block_expert = jnp.minimum(jnp.sum((pad_ends[None, :] <= starts[:, None]).astype(I32), axis=1),
                               N_EXPERTS - 1)
    block_first = jnp.concatenate([jnp.ones((1,), I32), (block_expert[1:] != block_expert[:-1]).astype(I32)])
    own = block_expert[:, None] == experts
    block_valid = jnp.clip(jnp.sum(jnp.where(own, cnt + pad_off, 0), axis=1) - starts, 0, bm)
    block_valid = jnp.where(starts < pad_ends[-1], block_valid, 0).astype(I32)
    return dest, gates, block_expert, block_first, block_valid, n_blocks * bm


def kernel(x, mem, norm_mix, norm_xattn, norm_ffn, norm_mem, norm_final, conv_w_in, conv_b_in, conv_w_dw,
           conv_b_dw, conv_ln_g, conv_ln_b, conv_w_out, conv_b_out, gla_w_in, gla_w_a2, gla_b_a, gla_norm_g,
           gla_w_o, xa_w_q, xa_w_kv, xa_w_o, moe_w_grp, moe_b_grp, moe_w_exp, moe_b_exp, moe_w_gate, moe_w_up,
           moe_w_down):
    b, s, d = x.shape
    depth = norm_mix.shape[0]
    qk_mem, vo_mem = _mem_kv(mem, norm_mem, xa_w_kv, xa_w_q, xa_w_o)
    moe = None
    for i in range(depth):
        j = i // 2
        if i % 2 == 0:
            if moe is not None:
                x = _combine(x.reshape(b * s, d), moe[1], moe[2], moe[0], norm_final, False).reshape(b, s, d)
            x = _conv_mixer(x, norm_mix[i], conv_w_in[j], conv_b_in[j], conv_w_dw[j], conv_b_dw[j],
                            conv_ln_g[j], conv_ln_b[j], conv_w_out[j], conv_b_out[j])
        else:
            x = _gla_mixer(x, *moe, norm_mix[i], gla_w_in[j], gla_w_a2[j], gla_b_a[j], gla_norm_g[j], gla_w_o[j])
        x2, h_packed, route, counts = _xattn_router(
            x, norm_xattn[i], qk_mem, vo_mem, i, norm_ffn[i],
            moe_w_grp[i], moe_b_grp[i], moe_w_exp[i], moe_b_exp[i])
        dest, gates, block_expert, block_first, block_valid, n_rows = _moe_layout(route, counts)
        x_buf = _dispatch(h_packed, dest, n_rows)
        y_buf = _experts(x_buf, block_expert, block_first, block_valid, moe_w_gate, moe_w_up, moe_w_down, i)
        if i < depth - 1:
            y0, y1 = _gather_pairs(y_buf, dest)
            x, moe = x2, (gates, y0, y1)
    part = (b * s) // FINAL_PARTS
    out = None
    for p in range(FINAL_PARTS):
        y0, y1 = _gather_pairs(y_buf, dest[:, p * part:(p + 1) * part])
        out = _combine(x2.reshape(b * s, d), y0, y1, gates, norm_final, True, p * part, out)
    return out.reshape(b, s, d)
```

```python
import functools

import jax
import jax.numpy as jnp
from jax import lax
from jax.experimental import pallas as pl
from jax.experimental.pallas import tpu as pltpu
from jax.experimental.pallas import tpu_sc as plsc

F32 = jnp.float32
BF16 = jnp.bfloat16
I32 = jnp.int32
U32 = jnp.uint32

EPS = 1e-6
CONV_KERNEL = 31
CONV_CARRY = 32
CONV_ROWS = 64
CONV_COLS = 256
SUBLANES = 8
LANES = 128
GLA_HEADS = 4
GLA_RANK = 16
GLA_RANK_PAD = 128
GLA_TAU = 16.0
GLA_LEAF = 32
XATTN_HEADS = 4
N_GROUPS = 4
EXPERTS_PER_GROUP = 8
N_EXPERTS = N_GROUPS * EXPERTS_PER_GROUP
ROUTER_ROWS = 40
TOP_K = 2
ROUTE_ROWS = 8

TILE_CONV = 512
TILE_GLA = 256
TILE_XATTN = 1024
TILE_COMBINE = 512
SC_CORES = 2
SC_SUBCORES = 16
SC_WORKERS = SC_CORES * SC_SUBCORES
SC_DISPATCH_CHUNK = 64
SC_GATHER_CHUNK = 32
MOE_BLOCK_ROWS = 512
VMEM_LIMIT = 56 * 1024 * 1024


def _cparams(sem):
    return pltpu.CompilerParams(dimension_semantics=sem, vmem_limit_bytes=VMEM_LIMIT)


def _rms(x, g):
    return x * lax.rsqrt(jnp.mean(x * x, axis=-1, keepdims=True) + EPS) * g


def _sigmoid(x):
    return 0.5 * jnp.tanh(0.5 * x) + 0.5


def _dot(a, b):
    return jnp.dot(a, b, preferred_element_type=F32)


def _dot_nt(a, b):
    return lax.dot_general(a, b, (((1,), (1,)), ((), ())), preferred_element_type=F32)


def _dot_tn(a, b):
    return lax.dot_general(a, b, (((0,), (0,)), ((), ())), preferred_element_type=F32)


def _pack_bf16_pairs(x):
    w = x.shape[1] // 2
    hi = lax.bitcast_convert_type(x[:, :w].astype(BF16).astype(F32), U32)
    lo = lax.bitcast_convert_type(x[:, w:].astype(BF16).astype(F32), U32)
    return hi | (lo >> 16)


def _unpack_bf16_pairs(p):
    hi = lax.bitcast_convert_type(p & jnp.uint32(0xFFFF0000), F32)
    lo = lax.bitcast_convert_type(p << 16, F32)
    return jnp.concatenate([hi, lo], axis=1)


def _memkv_kernel(mem_ref, g_ref, wkv_ref, wq_ref, wo_ref, qk_ref, vo_ref):
    nm, d = mem_ref.shape[1], mem_ref.shape[2]
    hd = d // XATTN_HEADS
    mn = _rms(mem_ref[0], g_ref[...]).astype(BF16)
    kv = _dot(mn, wkv_ref[0])
    k = kv[:, :d].astype(BF16)
    v = kv[:, d:].astype(BF16)
    for a in range(XATTN_HEADS):
        sl = slice(a * hd, (a + 1) * hd)
        qk_ref[0, 0, :, a * nm:(a + 1) * nm] = (_dot_nt(wq_ref[0, :, sl], k[:, sl]) * (hd ** -0.5)).astype(BF16)
        vo_ref[0, 0, a * nm:(a + 1) * nm, :] = _dot(v[:, sl], wo_ref[0, sl, :]).astype(BF16)


def _mem_kv(mem, norm_mem, w_kv, w_q, w_o):
    b, nm, d = mem.shape
    depth = w_kv.shape[0]
    per_layer = lambda shape: pl.BlockSpec((1,) + shape, lambda l, i: (l, 0, 0))
    return pl.pallas_call(
        _memkv_kernel,
        out_shape=(jax.ShapeDtypeStruct((depth, b, d, XATTN_HEADS * nm), BF16),
                   jax.ShapeDtypeStruct((depth, b, XATTN_HEADS * nm, d), BF16)),
        grid=(depth, b),
        in_specs=[
            pl.BlockSpec((1, nm, d), lambda l, i: (i, 0, 0)),
            pl.BlockSpec((1, d), lambda l, i: (0, 0)),
            per_layer((d, 2 * d)), per_layer((d, d)), per_layer((d, d)),
        ],
        out_specs=(
            pl.BlockSpec((1, 1, d, XATTN_HEADS * nm), lambda l, i: (l, i, 0, 0)),
            pl.BlockSpec((1, 1, XATTN_HEADS * nm, d), lambda l, i: (l, i, 0, 0)),
        ),
        compiler_params=_cparams(("arbitrary", "arbitrary")),
        name="mem_kv",
    )(mem, norm_mem.reshape(1, d), w_kv.astype(BF16), w_q.astype(BF16), w_o.astype(BF16))


def _conv_kernel(x_ref, g_ref, win_ref, bin_ref, wdw_ref, bdw_ref, lng_ref, lnb_ref, wout_ref, bout_ref,
                 o_ref, ext_ref, conv_ref):
    ts, d = x_ref.shape[1], x_ref.shape[2]

    @pl.when(pl.program_id(1) == 0)
    def _():
        ext_ref[...] = jnp.zeros(ext_ref.shape, F32)

    x = x_ref[0]
    h = _rms(x, g_ref[...]).astype(BF16)
    u = _dot(h, win_ref[...]) + bin_ref[...]
    glu = u[:, :d] * _sigmoid(u[:, d:])
    for b in range(SUBLANES):
        ext_ref[b, CONV_CARRY - b:CONV_CARRY - b + ts, :] = glu

    first = CONV_CARRY - (CONV_KERNEL - 1)

    def chunk(i, carry):
        r0 = pl.multiple_of(i * CONV_ROWS, CONV_ROWS)
        for c0 in range(0, d, CONV_COLS):
            cols = slice(c0, c0 + CONV_COLS)
            acc = [jnp.zeros((SUBLANES, CONV_COLS), F32) for _ in range(CONV_ROWS // SUBLANES)]
            for k in range(CONV_KERNEL):
                b = (first + k) % SUBLANES
                wk = wdw_ref[k * SUBLANES:(k + 1) * SUBLANES, cols]
                for j in range(CONV_ROWS // SUBLANES):
                    rows = pl.ds(r0 + (first + k - b) + j * SUBLANES, SUBLANES)
                    acc[j] = acc[j] + ext_ref[b, rows, cols] * wk
            for j in range(CONV_ROWS // SUBLANES):
                conv_ref[pl.ds(r0 + j * SUBLANES, SUBLANES), cols] = acc[j]
        return carry

    lax.fori_loop(0, ts // CONV_ROWS, chunk, 0)
    for b in range(SUBLANES):
        ext_ref[b, 0:CONV_CARRY, :] = ext_ref[b, ts:ts + CONV_CARRY, :]

    c = conv_ref[...] + bdw_ref[...]
    mu = jnp.mean(c, axis=-1, keepdims=True)
    cc = c - mu
    var = jnp.mean(cc * cc, axis=-1, keepdims=True)
    un = cc * lax.rsqrt(var + EPS) * lng_ref[...] + lnb_ref[...]
    act = (un * _sigmoid(un)).astype(BF16)
    o_ref[0] = x + _dot(act, wout_ref[...]) + bout_ref[...]


def _conv_mixer(x, g, w_in, b_in, w_dw, b_dw, ln_g, ln_b, w_out, b_out):
    b, s, d = x.shape
    ts = min(TILE_CONV, s)
    assert s % ts == 0 and ts % CONV_ROWS == 0 and d % CONV_COLS == 0
    row = lambda v: v.reshape(1, -1)
    const = lambda shape: pl.BlockSpec(shape, lambda i, j: (0,) * len(shape))
    return pl.pallas_call(
        _conv_kernel,
        out_shape=jax.ShapeDtypeStruct(x.shape, F32),
        grid=(b, s // ts),
        in_specs=[
            pl.BlockSpec((1, ts, d), lambda i, j: (i, j, 0)),
            const((1, d)), const((d, 2 * d)), const((1, 2 * d)), const((CONV_KERNEL * SUBLANES, d)), const((1, d)),
            const((1, d)), const((1, d)), const((d, d)), const((1, d)),
        ],
        out_specs=pl.BlockSpec((1, ts, d), lambda i, j: (i, j, 0)),
        scratch_shapes=[pltpu.VMEM((SUBLANES, CONV_CARRY + ts, d), F32), pltpu.VMEM((ts, d), F32)],
        compiler_params=_cparams(("arbitrary", "arbitrary")),
        name="conv_mixer",
    )(x, row(g), w_in.astype(BF16), row(b_in), jnp.repeat(w_dw, SUBLANES, axis=0), row(b_dw), row(ln_g), row(ln_b),
      w_out.astype(BF16), row(b_out))


def _gla_levels(ts):
    sizes = [GLA_LEAF]
    while sizes[-1] < ts:
        sizes.append(sizes[-1] * 2)
    return sizes


def _moe_residual(x, route, y0, y1):
    g = lax.bitcast_convert_type(route, F32)
    g = jnp.concatenate([g, jnp.zeros((LANES - g.shape[0], g.shape[1]), F32)], axis=0).T
    gate0 = g[:, 2 * TOP_K:2 * TOP_K + 1]
    gate1 = g[:, 2 * TOP_K + 1:2 * TOP_K + 2]
    return x + _unpack_bf16_pairs(y0) * gate0 + _unpack_bf16_pairs(y1) * gate1


def _gla_kernel(x_ref, gate_ref, y0_ref, y1_ref, g_ref, wq_ref, wk_ref, wv_ref, wa_ref, wr_ref, wa2_ref, ba_ref,
                ng_ref, wo_ref, o_ref, state_ref, x_s, q_s, k_s, v_s, og_s, la_s, *, tiles_per_seq):
    j = pl.program_id(0)
    dkh = wq_ref.shape[1] // GLA_HEADS

    @pl.when(j == 0)
    def _():
        for ref in (x_s, q_s, k_s, og_s, la_s):
            ref[...] = jnp.zeros(ref.shape, F32)
        v_s[...] = jnp.zeros(v_s.shape, BF16)

    @pl.when((j == 0) | ((j - 1) % tiles_per_seq == 0))
    def _():
        state_ref[...] = jnp.zeros(state_ref.shape, F32)

    def stage_a(slot):
        x = _moe_residual(x_ref[0], gate_ref[...], y0_ref[...], y1_ref[...])
        h = _rms(x, g_ref[...]).astype(BF16)
        x_s[slot] = x
        q_s[slot] = _dot(h, wq_ref[...]) * (dkh ** -0.5)
        k_s[slot] = _dot(h, wk_ref[...])
        v_s[slot] = _dot(h, wv_ref[...]).astype(BF16)
        r = _dot(h, wr_ref[...])
        og_s[slot] = r * _sigmoid(r)
        a = _dot(h, wa_ref[...]).astype(BF16)
        z = _dot(jnp.concatenate([a, a], axis=1), wa2_ref[...]) + ba_ref[...]
        la_s[slot] = -(jnp.maximum(-z, 0.0) + jnp.log(1.0 + jnp.exp(-jnp.abs(z)))) * (1.0 / GLA_TAU)

    def stage_b(slot):
        _gla_recurrence(x_s[slot], q_s[slot], k_s[slot], v_s[slot], og_s[slot], la_s[slot],
                        ng_ref, wo_ref, o_ref, state_ref)

    @pl.when(j % 2 == 0)
    def _():
        stage_b(1)
        stage_a(0)

    @pl.when(j % 2 == 1)
    def _():
        stage_b(0)
        stage_a(1)


def _gla_recurrence(x, q, k, v, out_gate, log_a, ng_ref, wo_ref, o_ref, state_ref):
    ts, d = x.shape
    dk = q.shape[1]
    dkh = dk // GLA_HEADS
    dvh = d // GLA_HEADS

    row = lax.broadcasted_iota(I32, (ts, ts), 0)
    col = lax.broadcasted_iota(I32, (ts, ts), 1)
    tri = jnp.where(col <= row, 1.0, 0.0).astype(BF16)
    bcum = _dot(tri, log_a.astype(BF16))
    b_last = bcum[ts - 1:ts, :]

    q_in = (q * jnp.exp(bcum)).astype(BF16)
    k_out = (k * jnp.exp(b_last - bcum)).astype(BF16)

    sizes = _gla_levels(ts)
    scores = [None] * GLA_HEADS
    for lvl, size in enumerate(sizes):
        half = size // 2
        same_block = (row & -size) == (col & -size)
        if lvl == 0:
            pair = same_block & (col <= row)
            q_ok = k_ok = None
        else:
            pair = same_block & ((row & (size - 1)) >= half) & ((col & (size - 1)) < half)
            pos = lax.broadcasted_iota(I32, (ts, dk), 0) & (size - 1)
            q_ok = pos >= half
            k_ok = pos < half
        ref = jnp.concatenate(
            [jnp.broadcast_to(bcum[r0 + half:r0 + half + 1, :], (size, dk)) for r0 in range(0, ts, size)], axis=0)
        ql = q * jnp.exp(bcum - ref)
        kl = k * jnp.exp(ref - bcum)
        if q_ok is not None:
            ql = jnp.where(q_ok, ql, 0.0)
            kl = jnp.where(k_ok, kl, 0.0)
        ql = ql.astype(BF16)
        kl = kl.astype(BF16)
        for hd in range(GLA_HEADS):
            c0 = hd * dkh
            a = _dot_nt(ql[:, c0:c0 + dkh], kl[:, c0:c0 + dkh])
            a = jnp.where(pair, a, 0.0)
            scores[hd] = a if scores[hd] is None else scores[hd] + a

    outs = []
    for hd in range(GLA_HEADS):
        c0 = hd * dkh
        v_h = v[:, hd * dvh:(hd + 1) * dvh]
        st = state_ref[hd]
        o_h = _dot(scores[hd].astype(BF16), v_h) + _dot_nt(q_in[:, c0:c0 + dkh], st.astype(BF16))
        decay = jnp.exp(b_last[:, c0:c0 + dkh])
        state_ref[hd] = st * decay + _dot_tn(v_h, k_out[:, c0:c0 + dkh])
        o_h = o_h * lax.rsqrt(jnp.mean(o_h * o_h, axis=-1, keepdims=True) + EPS) * ng_ref[...]
        outs.append(o_h)
    o = jnp.concatenate(outs, axis=1) * out_gate
    o_ref[0] = x + _dot(o.astype(BF16), wo_ref[...])


def _gla_mixer(x, gates, y0, y1, g, w_in, w_a2, b_a, norm_g, w_o):
    b, s, d = x.shape
    dk = w_a2.shape[1]
    ts = min(TILE_GLA, s)
    assert s % ts == 0 and ts % GLA_LEAF == 0
    nj = s // ts
    row = lambda v: v.reshape(1, -1)
    const = lambda shape: pl.BlockSpec(shape, lambda j: (0,) * len(shape))
    wq = w_in[:, :dk].astype(BF16)
    wk = w_in[:, dk:2 * dk].astype(BF16)
    wv = w_in[:, 2 * dk:2 * dk + d].astype(BF16)
    wa = jnp.pad(w_in[:, 2 * dk + d:2 * dk + d + GLA_RANK], ((0, 0), (0, GLA_RANK_PAD - GLA_RANK))).astype(BF16)
    wr = w_in[:, 2 * dk + d + GLA_RANK:].astype(BF16)
    wa2 = jnp.pad(w_a2, ((0, GLA_RANK_PAD - GLA_RANK), (0, 0)))
    wa2_hi = wa2.astype(BF16)
    wa2_lo = (wa2 - wa2_hi.astype(F32)).astype(BF16)
    wa2_split = jnp.concatenate([wa2_hi, wa2_lo], axis=0)
    dvh = d // GLA_HEADS
    n = b * nj

    def tile(j, lag):
        return jnp.clip(j - lag, 0, n - 1)

    tokens = lambda lag: (lambda j: (tile(j, lag), 0))
    return pl.pallas_call(
        functools.partial(_gla_kernel, tiles_per_seq=nj),
        out_shape=jax.ShapeDtypeStruct(x.shape, F32),
        grid=(n + 1,),
        in_specs=[
            pl.BlockSpec((1, ts, d), lambda j: (tile(j, 0) // nj, tile(j, 0) % nj, 0)),
            pl.BlockSpec((ROUTE_ROWS, ts), lambda j: (0, tile(j, 0))),
            pl.BlockSpec((ts, d // 2), tokens(0)),
            pl.BlockSpec((ts, d // 2), tokens(0)),
            const((1, d)), const((d, dk)), const((d, dk)), const((d, d)), const((d, GLA_RANK_PAD)),
            const((d, d)), const((2 * GLA_RANK_PAD, dk)), const((1, dk)), const((1, dvh)), const((d, d)),
        ],
        out_specs=pl.BlockSpec((1, ts, d), lambda j: (tile(j, 1) // nj, tile(j, 1) % nj, 0)),
        scratch_shapes=[pltpu.VMEM((GLA_HEADS, dvh, dk // GLA_HEADS), F32),
                        pltpu.VMEM((2, ts, d), F32), pltpu.VMEM((2, ts, dk), F32), pltpu.VMEM((2, ts, dk), F32),
                        pltpu.VMEM((2, ts, d), BF16), pltpu.VMEM((2, ts, d), F32), pltpu.VMEM((2, ts, dk), F32)],
        compiler_params=_cparams(("arbitrary",)),
        name="gla_mixer",
    )(x, gates, y0, y1, row(g), wq, wk, wv, wa, wr, wa2_split, row(b_a), row(norm_g), w_o.astype(BF16))


def _xattn_router_kernel(x_ref, gx_ref, qk_ref, vo_ref, gf_ref, wr_ref, br_ref, upper_ref,
                         x_out_ref, h_out_ref, route_ref, cnt_ref, carry_ref):
    ts, d = x_ref.shape[1], x_ref.shape[2]
    nm = qk_ref.shape[3] // XATTN_HEADS
    first = (pl.program_id(0) == 0) & (pl.program_id(1) == 0)

    @pl.when(first)
    def _():
        carry_ref[...] = jnp.zeros(carry_ref.shape, F32)

    x = x_ref[0]
    h = _rms(x, gx_ref[...]).astype(BF16)
    scores = _dot(h, qk_ref[0, 0])
    probs = []
    for a in range(XATTN_HEADS):
        s = scores[:, a * nm:(a + 1) * nm]
        p = jnp.exp(s - jnp.max(s, axis=-1, keepdims=True))
        probs.append((p / jnp.sum(p, axis=-1, keepdims=True)).astype(BF16))
    x2 = x + _dot(jnp.concatenate(probs, axis=1), vo_ref[0, 0])
    x_out_ref[0] = x2

    hf = _rms(x2, gf_ref[...])
    h_out_ref[...] = _pack_bf16_pairs(hf)

    both = _dot_nt(wr_ref[...], hf.astype(BF16))
    logits = both[0:ROUTER_ROWS, :] + both[ROUTER_ROWS:2 * ROUTER_ROWS, :] + br_ref[...]
    gl = logits[N_EXPERTS:N_EXPERTS + N_GROUPS, :]
    gi = lax.broadcasted_iota(I32, gl.shape, 0).astype(F32)
    gmax = jnp.max(gl, axis=0, keepdims=True)
    g_sel = jnp.min(jnp.where(gl == gmax, gi, float(N_GROUPS)), axis=0, keepdims=True)
    pg_sel = 1.0 / jnp.sum(jnp.exp(gl - gmax), axis=0, keepdims=True)

    el = jnp.zeros((EXPERTS_PER_GROUP, ts), F32)
    for gidx in range(N_GROUPS):
        lo = gidx * EXPERTS_PER_GROUP
        el = jnp.where(g_sel == float(gidx), logits[lo:lo + EXPERTS_PER_GROUP, :], el)
    ei = lax.broadcasted_iota(I32, el.shape, 0).astype(F32)
    m1 = jnp.max(el, axis=0, keepdims=True)
    i1 = jnp.min(jnp.where(el == m1, ei, float(EXPERTS_PER_GROUP)), axis=0, keepdims=True)
    rest = jnp.where(ei == i1, -jnp.inf, el)
    m2 = jnp.max(rest, axis=0, keepdims=True)
    i2 = jnp.min(jnp.where(rest == m2, ei, float(EXPERTS_PER_GROUP)), axis=0, keepdims=True)
    ratio = jnp.exp(m2 - m1)
    gate1 = pg_sel / (1.0 + ratio)
    gate2 = pg_sel * ratio / (1.0 + ratio)
    e1 = g_sel * float(EXPERTS_PER_GROUP) + i1
    e2 = g_sel * float(EXPERTS_PER_GROUP) + i2

    xi = lax.broadcasted_iota(I32, (N_EXPERTS, ts), 0).astype(F32)
    oh1 = jnp.where(xi == e1, 1.0, 0.0)
    oh2 = jnp.where(xi == e2, 1.0, 0.0)
    oh = oh1 + oh2
    n_blk = ts // LANES
    stacked = jnp.concatenate([oh[:, c * LANES:(c + 1) * LANES] for c in range(n_blk)], axis=0)
    within = _dot(stacked.astype(BF16), upper_ref[...])
    totals = jnp.sum(stacked, axis=1, keepdims=True)
    run = carry_ref[...]
    before = []
    for c in range(n_blk):
        before.append(within[c * N_EXPERTS:(c + 1) * N_EXPERTS, :] + run)
        run = run + totals[c * N_EXPERTS:(c + 1) * N_EXPERTS, :]
    before = jnp.concatenate(before, axis=1)
    rank1 = jnp.sum(oh1 * before, axis=0, keepdims=True)
    rank2 = jnp.sum(oh2 * before, axis=0, keepdims=True)
    carry_ref[...] = run
    cnt_ref[...] = jnp.broadcast_to(run, cnt_ref.shape)

    route_ref[0:1, :] = e1.astype(I32)
    route_ref[1:2, :] = e2.astype(I32)
    route_ref[2:3, :] = rank1.astype(I32)
    route_ref[3:4, :] = rank2.astype(I32)
    route_ref[4:5, :] = lax.bitcast_convert_type(gate1, I32)
    route_ref[5:6, :] = lax.bitcast_convert_type(gate2, I32)
    route_ref[3 * TOP_K:ROUTE_ROWS, :] = jnp.zeros((ROUTE_ROWS - 3 * TOP_K, ts), I32)


def _xattn_router(x, g_x, qk_mem, vo_mem, layer, g_f, w_grp, b_grp, w_exp, b_exp):
    b, s, d = x.shape
    hm = qk_mem.shape[3]
    t = b * s
    ts = min(TILE_XATTN, s)
    assert s % ts == 0 and ts % LANES == 0
    nj = s // ts
    row = lambda v: v.reshape(1, -1)
    const = lambda shape: pl.BlockSpec(shape, lambda i, j: (0,) * len(shape))
    pad = ROUTER_ROWS - N_GROUPS - N_EXPERTS
    w_r = jnp.pad(jnp.concatenate([w_exp, w_grp], axis=1).T, ((0, pad), (0, 0)))
    w_r_hi = w_r.astype(BF16)
    w_r_split = jnp.concatenate([w_r_hi, (w_r - w_r_hi.astype(F32)).astype(BF16)], axis=0)
    b_r = jnp.pad(jnp.concatenate([b_exp, b_grp]), (0, pad)).reshape(ROUTER_ROWS, 1)
    ti = jnp.arange(LANES)
    upper = (ti[:, None] < ti[None, :]).astype(BF16)
    return pl.pallas_call(
        _xattn_router_kernel,
        out_shape=(
            jax.ShapeDtypeStruct(x.shape, F32),
            jax.ShapeDtypeStruct((t, d // 2), U32),
            jax.ShapeDtypeStruct((ROUTE_ROWS, t), I32),
            jax.ShapeDtypeStruct((N_EXPERTS, LANES), F32),
        ),
        grid=(b, nj),
        in_specs=[
            pl.BlockSpec((1, ts, d), lambda i, j: (i, j, 0)),
            const((1, d)),
            pl.BlockSpec((1, 1, d, hm), lambda i, j: (layer, i, 0, 0)),
            pl.BlockSpec((1, 1, hm, d), lambda i, j: (layer, i, 0, 0)),
            const((1, d)), const((2 * ROUTER_ROWS, d)), const((ROUTER_ROWS, 1)), const((LANES, LANES)),
        ],
        out_specs=(
            pl.BlockSpec((1, ts, d), lambda i, j: (i, j, 0)),
            pl.BlockSpec((ts, d // 2), lambda i, j: (i * nj + j, 0)),
            pl.BlockSpec((ROUTE_ROWS, ts), lambda i, j: (0, i * nj + j)),
            pl.BlockSpec((N_EXPERTS, LANES), lambda i, j: (0, 0)),
        ),
        scratch_shapes=[pltpu.VMEM((N_EXPERTS, 1), F32)],
        compiler_params=_cparams(("arbitrary", "arbitrary")),
        name="xattn_router",
    )(x, row(g_x), qk_mem, vo_mem, row(g_f), w_r_split, b_r, upper)


def _sc_mesh():
    return plsc.VectorSubcoreMesh(core_axis_name="c", subcore_axis_name="s",
                                  num_cores=SC_CORES, num_subcores=SC_SUBCORES)


def _sc_worker():
    return lax.axis_index("s") * SC_CORES + lax.axis_index("c")


def _dispatch(h_packed, dest, n_rows):
    t, w = h_packed.shape
    chunk = SC_DISPATCH_CHUNK
    per_worker = t // SC_WORKERS
    n_chunks = per_worker // chunk
    assert n_chunks % 2 == 0 and n_chunks * chunk * SC_WORKERS == t
    dest = dest.reshape(TOP_K, SC_WORKERS * n_chunks, chunk)
    rows_buf = pltpu.VMEM((chunk, w), U32)

    @functools.partial(
        pl.kernel, mesh=_sc_mesh(),
        out_type=jax.ShapeDtypeStruct((n_rows, w), U32),
        scratch_types=[pltpu.VMEM((n_chunks, chunk), I32), pltpu.VMEM((n_chunks, chunk), I32), rows_buf, rows_buf,
                       pltpu.SemaphoreType.DMA((2,)), pltpu.SemaphoreType.DMA((2, TOP_K))],
        name="moe_dispatch_sc",
    )
    def run(h_hbm, d0_hbm, d1_hbm, xbuf_hbm, idx0_v, idx1_v, buf_a, buf_b, read_sem, write_sem):
        wid = _sc_worker()
        pltpu.sync_copy(d0_hbm.at[pl.ds(wid * n_chunks, n_chunks)], idx0_v)
        pltpu.sync_copy(d1_hbm.at[pl.ds(wid * n_chunks, n_chunks)], idx1_v)

        @pl.loop(0, n_chunks, step=2)
        def _(i):
            reads = [pltpu.async_copy(h_hbm.at[pl.ds(wid * per_worker + (i + j) * chunk, chunk)], buf, read_sem.at[j])
                     for j, buf in enumerate((buf_a, buf_b))]
            writes = []
            for j, buf in enumerate((buf_a, buf_b)):
                reads[j].wait()
                writes.append(pltpu.async_copy(buf, xbuf_hbm.at[idx0_v.at[i + j]], write_sem.at[j, 0]))
                writes.append(pltpu.async_copy(buf, xbuf_hbm.at[idx1_v.at[i + j]], write_sem.at[j, 1]))
            for copy in writes:
                copy.wait()

    return run(h_packed, dest[0], dest[1])


def _gather_pairs(y_buf, dest):
    t = dest.shape[1]
    w = y_buf.shape[1]
    chunk = SC_GATHER_CHUNK
    per_worker = t // SC_WORKERS
    n_chunks = per_worker // chunk
    assert n_chunks % 2 == 0 and n_chunks * chunk * SC_WORKERS == t
    dest = dest.reshape(TOP_K, SC_WORKERS * n_chunks, chunk)
    out = jax.ShapeDtypeStruct((t, w), U32)
    rows_buf = pltpu.VMEM((chunk, w), U32)

    @functools.partial(
        pl.kernel, mesh=_sc_mesh(),
        out_type=(out, out),
        scratch_types=[pltpu.VMEM((n_chunks, chunk), I32), pltpu.VMEM((n_chunks, chunk), I32),
                       rows_buf, rows_buf, rows_buf, rows_buf, pltpu.SemaphoreType.DMA((2, TOP_K))],
        name="moe_gather_sc",
    )
    def run(y_hbm, d0_hbm, d1_hbm, y0_hbm, y1_hbm, idx0_v, idx1_v, buf_a0, buf_a1, buf_b0, buf_b1, sem):
        wid = _sc_worker()
        pltpu.sync_copy(d0_hbm.at[pl.ds(wid * n_chunks, n_chunks)], idx0_v)
        pltpu.sync_copy(d1_hbm.at[pl.ds(wid * n_chunks, n_chunks)], idx1_v)

        @pl.loop(0, n_chunks, step=2)
        def _(i):
            bufs = ((buf_a0, buf_a1), (buf_b0, buf_b1))
            gathers = [[pltpu.async_copy(y_hbm.at[idx_v.at[i + j]], bufs[j][k], sem.at[j, k])
                        for k, idx_v in enumerate((idx0_v, idx1_v))] for j in range(2)]
            stores = []
            for j in range(2):
                rows = pl.ds(wid * per_worker + (i + j) * chunk, chunk)
                for k, out_hbm in enumerate((y0_hbm, y1_hbm)):
                    gathers[j][k].wait()
                    stores.append(pltpu.async_copy(bufs[j][k], out_hbm.at[rows], sem.at[j, k]))
            for copy in stores:
                copy.wait()

    return run(y_buf, dest[0], dest[1])


def _expert_kernel(be_ref, bf_ref, bv_ref, slot_ref, next_ref, x_ref, wg_hbm, wu_hbm, wd_hbm, y_ref,
                   wg_f32, wu_f32, wd_f32, wg_bf, wu_bf, wd_bf, sems, *, layer):
    blk = pl.program_id(0)
    valid = bv_ref[blk]

    def weight_copies(expert, slot):
        return [pltpu.make_async_copy(src.at[layer, expert], dst.at[slot], sems.at[slot, i])
                for i, (src, dst) in enumerate(((wg_hbm, wg_f32), (wu_hbm, wu_f32), (wd_hbm, wd_f32)))]

    @pl.when(blk == 0)
    def _():
        for copy in weight_copies(be_ref[0], 0):
            copy.start()

    @pl.when(bf_ref[blk] == 1)
    def _():
        slot = slot_ref[blk]
        for copy in weight_copies(be_ref[blk], slot):
            copy.wait()

        @pl.when(next_ref[blk] >= 0)
        def _():
            for copy in weight_copies(next_ref[blk], 1 - slot):
                copy.start()

        wg_bf[...] = wg_f32[slot].astype(BF16)
        wu_bf[...] = wu_f32[slot].astype(BF16)
        wd_bf[...] = wd_f32[slot].astype(BF16)

    @pl.when(valid > 0)
    def _():
        live = lax.broadcasted_iota(I32, x_ref.shape, 0) < valid
        xb = _unpack_bf16_pairs(jnp.where(live, x_ref[...], jnp.uint32(0))).astype(BF16)
        gt = _dot(xb, wg_bf[...])
        up = _dot(xb, wu_bf[...])
        act = (gt * _sigmoid(gt) * up).astype(BF16)
        y_ref[...] = _pack_bf16_pairs(_dot(act, wd_bf[...]))

    @pl.when(valid <= 0)
    def _():
        y_ref[...] = jnp.zeros(y_ref.shape, U32)


def _experts(x_buf, block_expert, block_first, block_valid, w_gate, w_up, w_down, layer):
    n_rows, w = x_buf.shape
    d, de = w_gate.shape[2], w_gate.shape[3]
    bm = MOE_BLOCK_ROWS
    n = n_rows // bm
    block_slot = (jnp.cumsum(block_first) - 1) % 2
    idx = jnp.arange(n, dtype=I32)
    later_first = jnp.concatenate([jnp.where(block_first[1:] == 1, idx[1:], n), jnp.full((1,), n, I32)])
    next_first = lax.cummin(later_first, reverse=True)
    block_next = jnp.where(next_first < n, block_expert[jnp.minimum(next_first, n - 1)], -1).astype(I32)
    any_space = pl.BlockSpec(memory_space=pl.ANY)
    grid_spec = pltpu.PrefetchScalarGridSpec(
        num_scalar_prefetch=5,
        grid=(n,),
        in_specs=[pl.BlockSpec((bm, w), lambda i, *_: (i, 0)), any_space, any_space, any_space],
        out_specs=pl.BlockSpec((bm, w), lambda i, *_: (i, 0)),
        scratch_shapes=[pltpu.VMEM((2, d, de), F32), pltpu.VMEM((2, d, de), F32), pltpu.VMEM((2, de, d), F32),
                        pltpu.VMEM((d, de), BF16), pltpu.VMEM((d, de), BF16), pltpu.VMEM((de, d), BF16),
                        pltpu.SemaphoreType.DMA((2, 3))],
    )
    return pl.pallas_call(
        functools.partial(_expert_kernel, layer=layer),
        out_shape=jax.ShapeDtypeStruct((n_rows, w), U32),
        grid_spec=grid_spec,
        compiler_params=_cparams(("arbitrary",)),
        name="moe_experts",
    )(block_expert, block_first, block_valid, block_slot.astype(I32), block_next, x_buf, w_gate, w_up, w_down)


def _combine_kernel(x_ref, gate_ref, y0_ref, y1_ref, gfin_ref, o_ref, *, final_norm):
    out = _moe_residual(x_ref[...], gate_ref[...], y0_ref[...], y1_ref[...])
    if final_norm:
        out = _rms(out, gfin_ref[...])
    o_ref[...] = out


def _combine(x2, y0, y1, gates, g_final, final_norm):
    t, d = x2.shape
    w = y0.shape[1]
    ts = min(TILE_COMBINE, t)
    assert t % ts == 0
    return pl.pallas_call(
        functools.partial(_combine_kernel, final_norm=final_norm),
        out_shape=jax.ShapeDtypeStruct((t, d), F32),
        grid=(t // ts,),
        in_specs=[
            pl.BlockSpec((ts, d), lambda i: (i, 0)),
            pl.BlockSpec((ROUTE_ROWS, ts), lambda i: (0, i)),
            pl.BlockSpec((ts, w), lambda i: (i, 0)),
            pl.BlockSpec((ts, w), lambda i: (i, 0)),
            pl.BlockSpec((1, d), lambda i: (0, 0)),
        ],
        out_specs=pl.BlockSpec((ts, d), lambda i: (i, 0)),
        compiler_params=_cparams(("arbitrary",)),
        name="moe_combine",
    )(x2, gates, y0, y1, g_final.reshape(1, d))


def _moe_layout(route, counts):
    bm = MOE_BLOCK_ROWS
    t = route.shape[1]
    assert (t * TOP_K) % bm == 0
    n_blocks = (t * TOP_K) // bm + N_EXPERTS
    cnt = counts[:, 0].astype(I32)
    padded = (cnt + bm - 1) // bm * bm
    pad_ends = jnp.cumsum(padded)
    pad_off = pad_ends - padded
    experts = jnp.arange(N_EXPERTS, dtype=I32)
    hit = route[0:TOP_K, :, None] == experts
    dest = jnp.sum(jnp.where(hit, pad_off, 0), axis=-1) + route[TOP_K:2 * TOP_K]
    gates = route
    starts = jnp.arange(n_blocks, dtype=I32) * bm
    block_expert = jnp.minimum(jnp.sum((pad_ends[None, :] <= starts[:, None]).astype(I32), axis=1),
                               N_EXPERTS - 1)
    block_first = jnp.concatenate([jnp.ones((1,), I32), (block_expert[1:] != block_expert[:-1]).astype(I32)])
    own = block_expert[:, None] == experts
    block_valid = jnp.clip(jnp.sum(jnp.where(own, cnt + pad_off, 0), axis=1) - starts, 0, bm)
    block_valid = jnp.where(starts < pad_ends[-1], block_valid, 0).astype(I32)
    return dest, gates, block_expert, block_first, block_valid, n_blocks * bm


def kernel(x, mem, norm_mix, norm_xattn, norm_ffn, norm_mem, norm_final, conv_w_in, conv_b_in, conv_w_dw,
           conv_b_dw, conv_ln_g, conv_ln_b, conv_w_out, conv_b_out, gla_w_in, gla_w_a2, gla_b_a, gla_norm_g,
           gla_w_o, xa_w_q, xa_w_kv, xa_w_o, moe_w_grp, moe_b_grp, moe_w_exp, moe_b_exp, moe_w_gate, moe_w_up,
           moe_w_down):
    b, s, d = x.shape
    depth = norm_mix.shape[0]
    qk_mem, vo_mem = _mem_kv(mem, norm_mem, xa_w_kv, xa_w_q, xa_w_o)
    moe = None
    for i in range(depth):
        j = i // 2
        if i % 2 == 0:
            if moe is not None:
                x = _combine(x.reshape(b * s, d), moe[1], moe[2], moe[0], norm_final, False).reshape(b, s, d)
            x = _conv_mixer(x, norm_mix[i], conv_w_in[j], conv_b_in[j], conv_w_dw[j], conv_b_dw[j],
                            conv_ln_g[j], conv_ln_b[j], conv_w_out[j], conv_b_out[j])
        else:
            x = _gla_mixer(x, *moe, norm_mix[i], gla_w_in[j], gla_w_a2[j], gla_b_a[j], gla_norm_g[j], gla_w_o[j])
        x2, h_packed, route, counts = _xattn_router(
            x, norm_xattn[i], qk_mem, vo_mem, i, norm_ffn[i],
            moe_w_grp[i], moe_b_grp[i], moe_w_exp[i], moe_b_exp[i])
        dest, gates, block_expert, block_first, block_valid, n_rows = _moe_layout(route, counts)
        x_buf = _dispatch(h_packed, dest, n_rows)
        y_buf = _experts(x_buf, block_expert, block_first, block_valid, moe_w_gate, moe_w_up, moe_w_down, i)
        y0, y1 = _gather_pairs(y_buf, dest)
        x, moe = x2, (gates, y0, y1)
    return _combine(x.reshape(b * s, d), moe[1], moe[2], moe[0], norm_final, True).reshape(b, s, d)
```

```python
import functools

import jax
import jax.numpy as jnp
from jax import lax
from jax.experimental import pallas as pl
from jax.experimental.pallas import tpu as pltpu
from jax.experimental.pallas import tpu_sc as plsc

F32 = jnp.float32
BF16 = jnp.bfloat16
I32 = jnp.int32
U32 = jnp.uint32

EPS = 1e-6
CONV_KERNEL = 31
CONV_CARRY = 32
CONV_ROWS = 64
CONV_COLS = 256
SUBLANES = 8
LANES = 128
GLA_HEADS = 4
GLA_RANK = 16
GLA_RANK_PAD = 128
GLA_TAU = 16.0
GLA_LEAF = 32
XATTN_HEADS = 4
N_GROUPS = 4
EXPERTS_PER_GROUP = 8
N_EXPERTS = N_GROUPS * EXPERTS_PER_GROUP
ROUTER_ROWS = 40
TOP_K = 2
ROUTE_ROWS = 8

TILE_CONV = 512
TILE_GLA = 256
TILE_XATTN = 1024
TILE_COMBINE = 512
SC_CORES = 2
SC_SUBCORES = 16
SC_WORKERS = SC_CORES * SC_SUBCORES
SC_DISPATCH_CHUNK = 64
SC_GATHER_CHUNK = 32
MOE_BLOCK_ROWS = 512
VMEM_LIMIT = 56 * 1024 * 1024


def _cparams(sem):
    return pltpu.CompilerParams(dimension_semantics=sem, vmem_limit_bytes=VMEM_LIMIT)


def _rms(x, g):
    return x * lax.rsqrt(jnp.mean(x * x, axis=-1, keepdims=True) + EPS) * g


def _sigmoid(x):
    return 0.5 * jnp.tanh(0.5 * x) + 0.5


def _dot(a, b):
    return jnp.dot(a, b, preferred_element_type=F32)


def _dot_nt(a, b):
    return lax.dot_general(a, b, (((1,), (1,)), ((), ())), preferred_element_type=F32)


def _dot_tn(a, b):
    return lax.dot_general(a, b, (((0,), (0,)), ((), ())), preferred_element_type=F32)


def _pack_bf16_pairs(x):
    w = x.shape[1] // 2
    hi = lax.bitcast_convert_type(x[:, :w].astype(BF16).astype(F32), U32)
    lo = lax.bitcast_convert_type(x[:, w:].astype(BF16).astype(F32), U32)
    return hi | (lo >> 16)


def _unpack_bf16_pairs(p):
    hi = lax.bitcast_convert_type(p & jnp.uint32(0xFFFF0000), F32)
    lo = lax.bitcast_convert_type(p << 16, F32)
    return jnp.concatenate([hi, lo], axis=1)


def _memkv_kernel(mem_ref, g_ref, wkv_ref, wq_ref, wo_ref, qk_ref, vo_ref):
    nm, d = mem_ref.shape[1], mem_ref.shape[2]
    hd = d // XATTN_HEADS
    mn = _rms(mem_ref[0], g_ref[...]).astype(BF16)
    kv = _dot(mn, wkv_ref[0])
    k = kv[:, :d].astype(BF16)
    v = kv[:, d:].astype(BF16)
    for a in range(XATTN_HEADS):
        sl = slice(a * hd, (a + 1) * hd)
        qk_ref[0, 0, :, a * nm:(a + 1) * nm] = (_dot_nt(wq_ref[0, :, sl], k[:, sl]) * (hd ** -0.5)).astype(BF16)
        vo_ref[0, 0, a * nm:(a + 1) * nm, :] = _dot(v[:, sl], wo_ref[0, sl, :]).astype(BF16)


def _mem_kv(mem, norm_mem, w_kv, w_q, w_o):
    b, nm, d = mem.shape
    depth = w_kv.shape[0]
    per_layer = lambda shape: pl.BlockSpec((1,) + shape, lambda l, i: (l, 0, 0))
    return pl.pallas_call(
        _memkv_kernel,
        out_shape=(jax.ShapeDtypeStruct((depth, b, d, XATTN_HEADS * nm), BF16),
                   jax.ShapeDtypeStruct((depth, b, XATTN_HEADS * nm, d), BF16)),
        grid=(depth, b),
        in_specs=[
            pl.BlockSpec((1, nm, d), lambda l, i: (i, 0, 0)),
            pl.BlockSpec((1, d), lambda l, i: (0, 0)),
            per_layer((d, 2 * d)), per_layer((d, d)), per_layer((d, d)),
        ],
        out_specs=(
            pl.BlockSpec((1, 1, d, XATTN_HEADS * nm), lambda l, i: (l, i, 0, 0)),
            pl.BlockSpec((1, 1, XATTN_HEADS * nm, d), lambda l, i: (l, i, 0, 0)),
        ),
        compiler_params=_cparams(("arbitrary", "arbitrary")),
        name="mem_kv",
    )(mem, norm_mem.reshape(1, d), w_kv.astype(BF16), w_q.astype(BF16), w_o.astype(BF16))


def _conv_kernel(x_ref, g_ref, win_ref, bin_ref, wdw_ref, bdw_ref, lng_ref, lnb_ref, wout_ref, bout_ref,
                 o_ref, ext_ref, even_ref, odd_ref):
    ts, d = x_ref.shape[1], x_ref.shape[2]
    n_shifts = ext_ref.shape[0]

    @pl.when(pl.program_id(1) == 0)
    def _():
        ext_ref[...] = jnp.zeros(ext_ref.shape, F32)

    x = x_ref[0]
    h = _rms(x, g_ref[...]).astype(BF16)
    u = _dot(h, win_ref[...]) + bin_ref[...]
    glu = u[:, :d] * _sigmoid(u[:, d:])
    for c in range(n_shifts):
        ext_ref[c, CONV_CARRY - 2 * c:CONV_CARRY - 2 * c + ts, :] = glu

    first = CONV_CARRY - (CONV_KERNEL - 1)

    def taps(r0, n_rows, parity, dst_ref):
        for c0 in range(0, d, CONV_COLS):
            cols = slice(c0, c0 + CONV_COLS)
            acc = [jnp.zeros((SUBLANES, CONV_COLS), F32) for _ in range(n_rows // SUBLANES)]
            for k in range(CONV_KERNEL):
                if (first + k) % 2 != parity:
                    continue
                shift = (first + k - parity) % SUBLANES
                base = first + k - parity - shift
                wk = wdw_ref[k * SUBLANES:(k + 1) * SUBLANES, cols]
                for j in range(n_rows // SUBLANES):
                    rows = pl.ds(r0 + base + j * SUBLANES, SUBLANES)
                    acc[j] = acc[j] + ext_ref[shift // 2, rows, cols] * wk
            for j in range(n_rows // SUBLANES):
                dst_ref[pl.ds(r0 + j * SUBLANES, SUBLANES), cols] = acc[j]

    def chunk(i, carry):
        r0 = pl.multiple_of(i * CONV_ROWS, CONV_ROWS)
        taps(r0, CONV_ROWS, 0, even_ref)
        taps(r0, CONV_ROWS, 1, odd_ref)
        return carry

    lax.fori_loop(0, ts // CONV_ROWS, chunk, 0)
    taps(ts, SUBLANES, 1, odd_ref)
    for c in range(n_shifts):
        ext_ref[c, 0:CONV_CARRY, :] = ext_ref[c, ts:ts + CONV_CARRY, :]

    c = even_ref[...] + odd_ref[1:ts + 1, :] + bdw_ref[...]
    mu = jnp.mean(c, axis=-1, keepdims=True)
    cc = c - mu
    var = jnp.mean(cc * cc, axis=-1, keepdims=True)
    un = cc * lax.rsqrt(var + EPS) * lng_ref[...] + lnb_ref[...]
    act = (un * _sigmoid(un)).astype(BF16)
    o_ref[0] = x + _dot(act, wout_ref[...]) + bout_ref[...]


def _conv_mixer(x, g, w_in, b_in, w_dw, b_dw, ln_g, ln_b, w_out, b_out):
    b, s, d = x.shape
    ts = min(TILE_CONV, s)
    assert s % ts == 0 and ts % CONV_ROWS == 0 and d % CONV_COLS == 0
    row = lambda v: v.reshape(1, -1)
    const = lambda shape: pl.BlockSpec(shape, lambda i, j: (0,) * len(shape))
    return pl.pallas_call(
        _conv_kernel,
        out_shape=jax.ShapeDtypeStruct(x.shape, F32),
        grid=(b, s // ts),
        in_specs=[
            pl.BlockSpec((1, ts, d), lambda i, j: (i, j, 0)),
            const((1, d)), const((d, 2 * d)), const((1, 2 * d)), const((CONV_KERNEL * SUBLANES, d)), const((1, d)),
            const((1, d)), const((1, d)), const((d, d)), const((1, d)),
        ],
        out_specs=pl.BlockSpec((1, ts, d), lambda i, j: (i, j, 0)),
        scratch_shapes=[pltpu.VMEM((SUBLANES // 2, CONV_CARRY + ts, d), F32), pltpu.VMEM((ts, d), F32),
                        pltpu.VMEM((ts + SUBLANES, d), F32)],
        compiler_params=_cparams(("arbitrary", "arbitrary")),
        name="conv_mixer",
    )(x, row(g), w_in.astype(BF16), row(b_in), jnp.repeat(w_dw, SUBLANES, axis=0), row(b_dw), row(ln_g), row(ln_b),
      w_out.astype(BF16), row(b_out))


def _gla_levels(ts):
    sizes = [GLA_LEAF]
    while sizes[-1] < ts:
        sizes.append(sizes[-1] * 2)
    return sizes


def _moe_residual(x, route, y0, y1):
    g = lax.bitcast_convert_type(route, F32)
    g = jnp.concatenate([g, jnp.zeros((LANES - g.shape[0], g.shape[1]), F32)], axis=0).T
    gate0 = g[:, 2 * TOP_K:2 * TOP_K + 1]
    gate1 = g[:, 2 * TOP_K + 1:2 * TOP_K + 2]
    return x + _unpack_bf16_pairs(y0) * gate0 + _unpack_bf16_pairs(y1) * gate1


def _gla_kernel(x_ref, gate_ref, y0_ref, y1_ref, g_ref, wq_ref, wk_ref, wv_ref, wa_ref, wr_ref, wa2_ref, ba_ref,
                ng_ref, wo_ref, o_ref, state_ref, x_s, q_s, k_s, v_s, og_s, la_s, *, tiles_per_seq):
    j = pl.program_id(0)
    dkh = wq_ref.shape[1] // GLA_HEADS

    @pl.when(j == 0)
    def _():
        for ref in (x_s, q_s, k_s, og_s, la_s):
            ref[...] = jnp.zeros(ref.shape, F32)
        v_s[...] = jnp.zeros(v_s.shape, BF16)

    @pl.when((j == 0) | ((j - 1) % tiles_per_seq == 0))
    def _():
        state_ref[...] = jnp.zeros(state_ref.shape, F32)

    def stage_a(slot):
        x = _moe_residual(x_ref[0], gate_ref[...], y0_ref[...], y1_ref[...])
        h = _rms(x, g_ref[...]).astype(BF16)
        x_s[slot] = x
        q_s[slot] = _dot(h, wq_ref[...]) * (dkh ** -0.5)
        k_s[slot] = _dot(h, wk_ref[...])
        v_s[slot] = _dot(h, wv_ref[...]).astype(BF16)
        r = _dot(h, wr_ref[...])
        og_s[slot] = r * _sigmoid(r)
        a = _dot(h, wa_ref[...]).astype(BF16)
        z = _dot(jnp.concatenate([a, a], axis=1), wa2_ref[...]) + ba_ref[...]
        la_s[slot] = -(jnp.maximum(-z, 0.0) + jnp.log(1.0 + jnp.exp(-jnp.abs(z)))) * (1.0 / GLA_TAU)

    def stage_b(slot):
        _gla_recurrence(x_s[slot], q_s[slot], k_s[slot], v_s[slot], og_s[slot], la_s[slot],
                        ng_ref, wo_ref, o_ref, state_ref)

    @pl.when(j % 2 == 0)
    def _():
        stage_b(1)
        stage_a(0)

    @pl.when(j % 2 == 1)
    def _():
        stage_b(0)
        stage_a(1)


def _gla_recurrence(x, q, k, v, out_gate, log_a, ng_ref, wo_ref, o_ref, state_ref):
    ts, d = x.shape
    dk = q.shape[1]
    dkh = dk // GLA_HEADS
    dvh = d // GLA_HEADS

    row = lax.broadcasted_iota(I32, (ts, ts), 0)
    col = lax.broadcasted_iota(I32, (ts, ts), 1)
    tri = jnp.where(col <= row, 1.0, 0.0).astype(BF16)
    bcum = _dot(tri, log_a.astype(BF16))
    b_last = bcum[ts - 1:ts, :]

    q_in = (q * jnp.exp(bcum)).astype(BF16)
    k_out = (k * jnp.exp(b_last - bcum)).astype(BF16)

    sizes = _gla_levels(ts)
    scores = [None] * GLA_HEADS
    for lvl, size in enumerate(sizes):
        half = size // 2
        same_block = (row & -size) == (col & -size)
        if lvl == 0:
            pair = same_block & (col <= row)
            q_ok = k_ok = None
        else:
            pair = same_block & ((row & (size - 1)) >= half) & ((col & (size - 1)) < half)
            pos = lax.broadcasted_iota(I32, (ts, dk), 0) & (size - 1)
            q_ok = pos >= half
            k_ok = pos < half
        ref = jnp.concatenate(
            [jnp.broadcast_to(bcum[r0 + half:r0 + half + 1, :], (size, dk)) for r0 in range(0, ts, size)], axis=0)
        ql = q * jnp.exp(bcum - ref)
        kl = k * jnp.exp(ref - bcum)
        if q_ok is not None:
            ql = jnp.where(q_ok, ql, 0.0)
            kl = jnp.where(k_ok, kl, 0.0)
        ql = ql.astype(BF16)
        kl = kl.astype(BF16)
        for hd in range(GLA_HEADS):
            c0 = hd * dkh
            a = _dot_nt(ql[:, c0:c0 + dkh], kl[:, c0:c0 + dkh])
            a = jnp.where(pair, a, 0.0)
            scores[hd] = a if scores[hd] is None else scores[hd] + a

    outs = []
    for hd in range(GLA_HEADS):
        c0 = hd * dkh
        v_h = v[:, hd * dvh:(hd + 1) * dvh]
        st = state_ref[hd]
        o_h = _dot(scores[hd].astype(BF16), v_h) + _dot_nt(q_in[:, c0:c0 + dkh], st.astype(BF16))
        decay = jnp.exp(b_last[:, c0:c0 + dkh])
        state_ref[hd] = st * decay + _dot_tn(v_h, k_out[:, c0:c0 + dkh])
        o_h = o_h * lax.rsqrt(jnp.mean(o_h * o_h, axis=-1, keepdims=True) + EPS) * ng_ref[...]
        outs.append(o_h)
    o = jnp.concatenate(outs, axis=1) * out_gate
    o_ref[0] = x + _dot(o.astype(BF16), wo_ref[...])


def _gla_mixer(x, gates, y0, y1, g, w_in, w_a2, b_a, norm_g, w_o):
    b, s, d = x.shape
    dk = w_a2.shape[1]
    ts = min(TILE_GLA, s)
    assert s % ts == 0 and ts % GLA_LEAF == 0
    nj = s // ts
    row = lambda v: v.reshape(1, -1)
    const = lambda shape: pl.BlockSpec(shape, lambda j: (0,) * len(shape))
    wq = w_in[:, :dk].astype(BF16)
    wk = w_in[:, dk:2 * dk].astype(BF16)
    wv = w_in[:, 2 * dk:2 * dk + d].astype(BF16)
    wa = jnp.pad(w_in[:, 2 * dk + d:2 * dk + d + GLA_RANK], ((0, 0), (0, GLA_RANK_PAD - GLA_RANK))).astype(BF16)
    wr = w_in[:, 2 * dk + d + GLA_RANK:].astype(BF16)
    wa2 = jnp.pad(w_a2, ((0, GLA_RANK_PAD - GLA_RANK), (0, 0)))
    wa2_hi = wa2.astype(BF16)
    wa2_lo = (wa2 - wa2_hi.astype(F32)).astype(BF16)
    wa2_split = jnp.concatenate([wa2_hi, wa2_lo], axis=0)
    dvh = d // GLA_HEADS
    n = b * nj

    def tile(j, lag):
        return jnp.clip(j - lag, 0, n - 1)

    tokens = lambda lag: (lambda j: (tile(j, lag), 0))
    return pl.pallas_call(
        functools.partial(_gla_kernel, tiles_per_seq=nj),
        out_shape=jax.ShapeDtypeStruct(x.shape, F32),
        grid=(n + 1,),
        in_specs=[
            pl.BlockSpec((1, ts, d), lambda j: (tile(j, 0) // nj, tile(j, 0) % nj, 0)),
            pl.BlockSpec((ROUTE_ROWS, ts), lambda j: (0, tile(j, 0))),
            pl.BlockSpec((ts, d // 2), tokens(0)),
            pl.BlockSpec((ts, d // 2), tokens(0)),
            const((1, d)), const((d, dk)), const((d, dk)), const((d, d)), const((d, GLA_RANK_PAD)),
            const((d, d)), const((2 * GLA_RANK_PAD, dk)), const((1, dk)), const((1, dvh)), const((d, d)),
        ],
        out_specs=pl.BlockSpec((1, ts, d), lambda j: (tile(j, 1) // nj, tile(j, 1) % nj, 0)),
        scratch_shapes=[pltpu.VMEM((GLA_HEADS, dvh, dk // GLA_HEADS), F32),
                        pltpu.VMEM((2, ts, d), F32), pltpu.VMEM((2, ts, dk), F32), pltpu.VMEM((2, ts, dk), F32),
                        pltpu.VMEM((2, ts, d), BF16), pltpu.VMEM((2, ts, d), F32), pltpu.VMEM((2, ts, dk), F32)],
        compiler_params=_cparams(("arbitrary",)),
        name="gla_mixer",
    )(x, gates, y0, y1, row(g), wq, wk, wv, wa, wr, wa2_split, row(b_a), row(norm_g), w_o.astype(BF16))


def _xattn_router_kernel(x_ref, gx_ref, qk_ref, vo_ref, gf_ref, wr_ref, br_ref, upper_ref,
                         x_out_ref, h_out_ref, route_ref, cnt_ref, carry_ref):
    ts, d = x_ref.shape[1], x_ref.shape[2]
    nm = qk_ref.shape[3] // XATTN_HEADS
    first = (pl.program_id(0) == 0) & (pl.program_id(1) == 0)

    @pl.when(first)
    def _():
        carry_ref[...] = jnp.zeros(carry_ref.shape, F32)

    x = x_ref[0]
    h = _rms(x, gx_ref[...]).astype(BF16)
    scores = _dot(h, qk_ref[0, 0])
    probs = []
    for a in range(XATTN_HEADS):
        s = scores[:, a * nm:(a + 1) * nm]
        p = jnp.exp(s - jnp.max(s, axis=-1, keepdims=True))
        probs.append((p / jnp.sum(p, axis=-1, keepdims=True)).astype(BF16))
    x2 = x + _dot(jnp.concatenate(probs, axis=1), vo_ref[0, 0])
    x_out_ref[0] = x2

    hf = _rms(x2, gf_ref[...])
    h_out_ref[...] = _pack_bf16_pairs(hf)

    both = _dot_nt(wr_ref[...], hf.astype(BF16))
    logits = both[0:ROUTER_ROWS, :] + both[ROUTER_ROWS:2 * ROUTER_ROWS, :] + br_ref[...]
    gl = logits[N_EXPERTS:N_EXPERTS + N_GROUPS, :]
    gi = lax.broadcasted_iota(I32, gl.shape, 0).astype(F32)
    gmax = jnp.max(gl, axis=0, keepdims=True)
    g_sel = jnp.min(jnp.where(gl == gmax, gi, float(N_GROUPS)), axis=0, keepdims=True)
    pg_sel = 1.0 / jnp.sum(jnp.exp(gl - gmax), axis=0, keepdims=True)

    el = jnp.zeros((EXPERTS_PER_GROUP, ts), F32)
    for gidx in range(N_GROUPS):
        lo = gidx * EXPERTS_PER_GROUP
        el = jnp.where(g_sel == float(gidx), logits[lo:lo + EXPERTS_PER_GROUP, :], el)
    ei = lax.broadcasted_iota(I32, el.shape, 0).astype(F32)
    m1 = jnp.max(el, axis=0, keepdims=True)
    i1 = jnp.min(jnp.where(el == m1, ei, float(EXPERTS_PER_GROUP)), axis=0, keepdims=True)
    rest = jnp.where(ei == i1, -jnp.inf, el)
    m2 = jnp.max(rest, axis=0, keepdims=True)
    i2 = jnp.min(jnp.where(rest == m2, ei, float(EXPERTS_PER_GROUP)), axis=0, keepdims=True)
    ratio = jnp.exp(m2 - m1)
    gate1 = pg_sel / (1.0 + ratio)
    gate2 = pg_sel * ratio / (1.0 + ratio)
    e1 = g_sel * float(EXPERTS_PER_GROUP) + i1
    e2 = g_sel * float(EXPERTS_PER_GROUP) + i2

    xi = lax.broadcasted_iota(I32, (N_EXPERTS, ts), 0).astype(F32)
    oh1 = jnp.where(xi == e1, 1.0, 0.0)
    oh2 = jnp.where(xi == e2, 1.0, 0.0)
    oh = oh1 + oh2
    n_blk = ts // LANES
    stacked = jnp.concatenate([oh[:, c * LANES:(c + 1) * LANES] for c in range(n_blk)], axis=0)
    within = _dot(stacked.astype(BF16), upper_ref[...])
    totals = jnp.sum(stacked, axis=1, keepdims=True)
    run = carry_ref[...]
    before = []
    for c in range(n_blk):
        before.append(within[c * N_EXPERTS:(c + 1) * N_EXPERTS, :] + run)
        run = run + totals[c * N_EXPERTS:(c + 1) * N_EXPERTS, :]
    before = jnp.concatenate(before, axis=1)
    rank1 = jnp.sum(oh1 * before, axis=0, keepdims=True)
    rank2 = jnp.sum(oh2 * before, axis=0, keepdims=True)
    carry_ref[...] = run
    cnt_ref[...] = jnp.broadcast_to(run, cnt_ref.shape)

    route_ref[0:1, :] = e1.astype(I32)
    route_ref[1:2, :] = e2.astype(I32)
    route_ref[2:3, :] = rank1.astype(I32)
    route_ref[3:4, :] = rank2.astype(I32)
    route_ref[4:5, :] = lax.bitcast_convert_type(gate1, I32)
    route_ref[5:6, :] = lax.bitcast_convert_type(gate2, I32)
    route_ref[3 * TOP_K:ROUTE_ROWS, :] = jnp.zeros((ROUTE_ROWS - 3 * TOP_K, ts), I32)


def _xattn_router(x, g_x, qk_mem, vo_mem, layer, g_f, w_grp, b_grp, w_exp, b_exp):
    b, s, d = x.shape
    hm = qk_mem.shape[3]
    t = b * s
    ts = min(TILE_XATTN, s)
    assert s % ts == 0 and ts % LANES == 0
    nj = s // ts
    row = lambda v: v.reshape(1, -1)
    const = lambda shape: pl.BlockSpec(shape, lambda i, j: (0,) * len(shape))
    pad = ROUTER_ROWS - N_GROUPS - N_EXPERTS
    w_r = jnp.pad(jnp.concatenate([w_exp, w_grp], axis=1).T, ((0, pad), (0, 0)))
    w_r_hi = w_r.astype(BF16)
    w_r_split = jnp.concatenate([w_r_hi, (w_r - w_r_hi.astype(F32)).astype(BF16)], axis=0)
    b_r = jnp.pad(jnp.concatenate([b_exp, b_grp]), (0, pad)).reshape(ROUTER_ROWS, 1)
    ti = jnp.arange(LANES)
    upper = (ti[:, None] < ti[None, :]).astype(BF16)
    return pl.pallas_call(
        _xattn_router_kernel,
        out_shape=(
            jax.ShapeDtypeStruct(x.shape, F32),
            jax.ShapeDtypeStruct((t, d // 2), U32),
            jax.ShapeDtypeStruct((ROUTE_ROWS, t), I32),
            jax.ShapeDtypeStruct((N_EXPERTS, LANES), F32),
        ),
        grid=(b, nj),
        in_specs=[
            pl.BlockSpec((1, ts, d), lambda i, j: (i, j, 0)),
            const((1, d)),
            pl.BlockSpec((1, 1, d, hm), lambda i, j: (layer, i, 0, 0)),
            pl.BlockSpec((1, 1, hm, d), lambda i, j: (layer, i, 0, 0)),
            const((1, d)), const((2 * ROUTER_ROWS, d)), const((ROUTER_ROWS, 1)), const((LANES, LANES)),
        ],
        out_specs=(
            pl.BlockSpec((1, ts, d), lambda i, j: (i, j, 0)),
            pl.BlockSpec((ts, d // 2), lambda i, j: (i * nj + j, 0)),
            pl.BlockSpec((ROUTE_ROWS, ts), lambda i, j: (0, i * nj + j)),
            pl.BlockSpec((N_EXPERTS, LANES), lambda i, j: (0, 0)),
        ),
        scratch_shapes=[pltpu.VMEM((N_EXPERTS, 1), F32)],
        compiler_params=_cparams(("arbitrary", "arbitrary")),
        name="xattn_router",
    )(x, row(g_x), qk_mem, vo_mem, row(g_f), w_r_split, b_r, upper)


def _sc_mesh():
    return plsc.VectorSubcoreMesh(core_axis_name="c", subcore_axis_name="s",
                                  num_cores=SC_CORES, num_subcores=SC_SUBCORES)


def _sc_worker():
    return lax.axis_index("s") * SC_CORES + lax.axis_index("c")


def _dispatch(h_packed, dest, n_rows):
    t, w = h_packed.shape
    chunk = SC_DISPATCH_CHUNK
    per_worker = t // SC_WORKERS
    n_chunks = per_worker // chunk
    assert n_chunks % 2 == 0 and n_chunks * chunk * SC_WORKERS == t
    dest = dest.reshape(TOP_K, SC_WORKERS * n_chunks, chunk)
    rows_buf = pltpu.VMEM((chunk, w), U32)

    @functools.partial(
        pl.kernel, mesh=_sc_mesh(),
        out_type=jax.ShapeDtypeStruct((n_rows, w), U32),
        scratch_types=[pltpu.VMEM((n_chunks, chunk), I32), pltpu.VMEM((n_chunks, chunk), I32), rows_buf, rows_buf,
                       pltpu.SemaphoreType.DMA((2,)), pltpu.SemaphoreType.DMA((2, TOP_K))],
        name="moe_dispatch_sc",
    )
    def run(h_hbm, d0_hbm, d1_hbm, xbuf_hbm, idx0_v, idx1_v, buf_a, buf_b, read_sem, write_sem):
        wid = _sc_worker()
        pltpu.sync_copy(d0_hbm.at[pl.ds(wid * n_chunks, n_chunks)], idx0_v)
        pltpu.sync_copy(d1_hbm.at[pl.ds(wid * n_chunks, n_chunks)], idx1_v)

        @pl.loop(0, n_chunks, step=2)
        def _(i):
            reads = [pltpu.async_copy(h_hbm.at[pl.ds(wid * per_worker + (i + j) * chunk, chunk)], buf, read_sem.at[j])
                     for j, buf in enumerate((buf_a, buf_b))]
            writes = []
            for j, buf in enumerate((buf_a, buf_b)):
                reads[j].wait()
                writes.append(pltpu.async_copy(buf, xbuf_hbm.at[idx0_v.at[i + j]], write_sem.at[j, 0]))
                writes.append(pltpu.async_copy(buf, xbuf_hbm.at[idx1_v.at[i + j]], write_sem.at[j, 1]))
            for copy in writes:
                copy.wait()

    return run(h_packed, dest[0], dest[1])


def _gather_pairs(y_buf, dest):
    t = dest.shape[1]
    w = y_buf.shape[1]
    chunk = SC_GATHER_CHUNK
    per_worker = t // SC_WORKERS
    n_chunks = per_worker // chunk
    assert n_chunks % 2 == 0 and n_chunks * chunk * SC_WORKERS == t
    dest = dest.reshape(TOP_K, SC_WORKERS * n_chunks, chunk)
    out = jax.ShapeDtypeStruct((t, w), U32)
    rows_buf = pltpu.VMEM((chunk, w), U32)

    @functools.partial(
        pl.kernel, mesh=_sc_mesh(),
        out_type=(out, out),
        scratch_types=[pltpu.VMEM((n_chunks, chunk), I32), pltpu.VMEM((n_chunks, chunk), I32),
                       rows_buf, rows_buf, rows_buf, rows_buf, pltpu.SemaphoreType.DMA((2, TOP_K))],
        name="moe_gather_sc",
    )
    def run(y_hbm, d0_hbm, d1_hbm, y0_hbm, y1_hbm, idx0_v, idx1_v, buf_a0, buf_a1, buf_b0, buf_b1, sem):
        wid = _sc_worker()
        pltpu.sync_copy(d0_hbm.at[pl.ds(wid * n_chunks, n_chunks)], idx0_v)
        pltpu.sync_copy(d1_hbm.at[pl.ds(wid * n_chunks, n_chunks)], idx1_v)

        @pl.loop(0, n_chunks, step=2)
        def _(i):
            bufs = ((buf_a0, buf_a1), (buf_b0, buf_b1))
            gathers = [[pltpu.async_copy(y_hbm.at[idx_v.at[i + j]], bufs[j][k], sem.at[j, k])
                        for k, idx_v in enumerate((idx0_v, idx1_v))] for j in range(2)]
            stores = []
            for j in range(2):
                rows = pl.ds(wid * per_worker + (i + j) * chunk, chunk)
                for k, out_hbm in enumerate((y0_hbm, y1_hbm)):
                    gathers[j][k].wait()
                    stores.append(pltpu.async_copy(bufs[j][k], out_hbm.at[rows], sem.at[j, k]))
            for copy in stores:
                copy.wait()

    return run(y_buf, dest[0], dest[1])


def _expert_kernel(be_ref, bf_ref, bv_ref, slot_ref, next_ref, x_ref, wg_hbm, wu_hbm, wd_hbm, y_ref,
                   wg_f32, wu_f32, wd_f32, wg_bf, wu_bf, wd_bf, sems, *, layer):
    blk = pl.program_id(0)
    valid = bv_ref[blk]

    def weight_copies(expert, slot):
        return [pltpu.make_async_copy(src.at[layer, expert], dst.at[slot], sems.at[slot, i])
                for i, (src, dst) in enumerate(((wg_hbm, wg_f32), (wu_hbm, wu_f32), (wd_hbm, wd_f32)))]

    @pl.when(blk == 0)
    def _():
        for copy in weight_copies(be_ref[0], 0):
            copy.start()

    @pl.when(bf_ref[blk] == 1)
    def _():
        slot = slot_ref[blk]
        for copy in weight_copies(be_ref[blk], slot):
            copy.wait()

        @pl.when(next_ref[blk] >= 0)
        def _():
            for copy in weight_copies(next_ref[blk], 1 - slot):
                copy.start()

        wg_bf[...] = wg_f32[slot].astype(BF16)
        wu_bf[...] = wu_f32[slot].astype(BF16)
        wd_bf[...] = wd_f32[slot].astype(BF16)

    @pl.when(valid > 0)
    def _():
        live = lax.broadcasted_iota(I32, x_ref.shape, 0) < valid
        xb = _unpack_bf16_pairs(jnp.where(live, x_ref[...], jnp.uint32(0))).astype(BF16)
        gt = _dot(xb, wg_bf[...])
        up = _dot(xb, wu_bf[...])
        act = (gt * _sigmoid(gt) * up).astype(BF16)
        y_ref[...] = _pack_bf16_pairs(_dot(act, wd_bf[...]))

    @pl.when(valid <= 0)
    def _():
        y_ref[...] = jnp.zeros(y_ref.shape, U32)


def _experts(x_buf, block_expert, block_first, block_valid, w_gate, w_up, w_down, layer):
    n_rows, w = x_buf.shape
    d, de = w_gate.shape[2], w_gate.shape[3]
    bm = MOE_BLOCK_ROWS
    n = n_rows // bm
    block_slot = (jnp.cumsum(block_first) - 1) % 2
    idx = jnp.arange(n, dtype=I32)
    later_first = jnp.concatenate([jnp.where(block_first[1:] == 1, idx[1:], n), jnp.full((1,), n, I32)])
    next_first = lax.cummin(later_first, reverse=True)
    block_next = jnp.where(next_first < n, block_expert[jnp.minimum(next_first, n - 1)], -1).astype(I32)
    any_space = pl.BlockSpec(memory_space=pl.ANY)
    grid_spec = pltpu.PrefetchScalarGridSpec(
        num_scalar_prefetch=5,
        grid=(n,),
        in_specs=[pl.BlockSpec((bm, w), lambda i, *_: (i, 0)), any_space, any_space, any_space],
        out_specs=pl.BlockSpec((bm, w), lambda i, *_: (i, 0)),
        scratch_shapes=[pltpu.VMEM((2, d, de), F32), pltpu.VMEM((2, d, de), F32), pltpu.VMEM((2, de, d), F32),
                        pltpu.VMEM((d, de), BF16), pltpu.VMEM((d, de), BF16), pltpu.VMEM((de, d), BF16),
                        pltpu.SemaphoreType.DMA((2, 3))],
    )
    return pl.pallas_call(
        functools.partial(_expert_kernel, layer=layer),
        out_shape=jax.ShapeDtypeStruct((n_rows, w), U32),
        grid_spec=grid_spec,
        compiler_params=_cparams(("arbitrary",)),
        name="moe_experts",
    )(block_expert, block_first, block_valid, block_slot.astype(I32), block_next, x_buf, w_gate, w_up, w_down)


def _combine_kernel(x_ref, gate_ref, y0_ref, y1_ref, gfin_ref, o_ref, *, final_norm):
    out = _moe_residual(x_ref[...], gate_ref[...], y0_ref[...], y1_ref[...])
    if final_norm:
        out = _rms(out, gfin_ref[...])
    o_ref[...] = out


def _combine(x2, y0, y1, gates, g_final, final_norm):
    t, d = x2.shape
    w = y0.shape[1]
    ts = min(TILE_COMBINE, t)
    assert t % ts == 0
    return pl.pallas_call(
        functools.partial(_combine_kernel, final_norm=final_norm),
        out_shape=jax.ShapeDtypeStruct((t, d), F32),
        grid=(t // ts,),
        in_specs=[
            pl.BlockSpec((ts, d), lambda i: (i, 0)),
            pl.BlockSpec((ROUTE_ROWS, ts), lambda i: (0, i)),
            pl.BlockSpec((ts, w), lambda i: (i, 0)),
            pl.BlockSpec((ts, w), lambda i: (i, 0)),
            pl.BlockSpec((1, d), lambda i: (0, 0)),
        ],
        out_specs=pl.BlockSpec((ts, d), lambda i: (i, 0)),
        compiler_params=_cparams(("arbitrary",)),
        name="moe_combine",
    )(x2, gates, y0, y1, g_final.reshape(1, d))


def _moe_layout(route, counts):
    bm = MOE_BLOCK_ROWS
    t = route.shape[1]
    assert (t * TOP_K) % bm == 0
    n_blocks = (t * TOP_K) // bm + N_EXPERTS
    cnt = counts[:, 0].astype(I32)
    padded = (cnt + bm - 1) // bm * bm
    pad_ends = jnp.cumsum(padded)
    pad_off = pad_ends - padded
    experts = jnp.arange(N_EXPERTS, dtype=I32)
    hit = route[0:TOP_K, :, None] == experts
    dest = jnp.sum(jnp.where(hit, pad_off, 0), axis=-1) + route[TOP_K:2 * TOP_K]
    gates = route
    starts = jnp.arange(n_blocks, dtype=I32) * bm
    block_expert = jnp.minimum(jnp.sum((pad_ends[None, :] <= starts[:, None]).astype(I32), axis=1),
                               N_EXPERTS - 1)
    block_first = jnp.concatenate([jnp.ones((1,), I32), (block_expert[1:] != block_expert[:-1]).astype(I32)])
    own = block_expert[:, None] == experts
    block_valid = jnp.clip(jnp.sum(jnp.where(own, cnt + pad_off, 0), axis=1) - starts, 0, bm)
    block_valid = jnp.where(starts < pad_ends[-1], block_valid, 0).astype(I32)
    return dest, gates, block_expert, block_first, block_valid, n_blocks * bm


def kernel(x, mem, norm_mix, norm_xattn, norm_ffn, norm_mem, norm_final, conv_w_in, conv_b_in, conv_w_dw,
           conv_b_dw, conv_ln_g, conv_ln_b, conv_w_out, conv_b_out, gla_w_in, gla_w_a2, gla_b_a, gla_norm_g,
           gla_w_o, xa_w_q, xa_w_kv, xa_w_o, moe_w_grp, moe_b_grp, moe_w_exp, moe_b_exp, moe_w_gate, moe_w_up,
           moe_w_down):
    b, s, d = x.shape
    depth = norm_mix.shape[0]
    qk_mem, vo_mem = _mem_kv(mem, norm_mem, xa_w_kv, xa_w_q, xa_w_o)
    moe = None
    for i in range(depth):
        j = i // 2
        if i % 2 == 0:
            if moe is not None:
                x = _combine(x.reshape(b * s, d), moe[1], moe[2], moe[0], norm_final, False).reshape(b, s, d)
            x = _conv_mixer(x, norm_mix[i], conv_w_in[j], conv_b_in[j], conv_w_dw[j], conv_b_dw[j],
                            conv_ln_g[j], conv_ln_b[j], conv_w_out[j], conv_b_out[j])
        else:
            x = _gla_mixer(x, *moe, norm_mix[i], gla_w_in[j], gla_w_a2[j], gla_b_a[j], gla_norm_g[j], gla_w_o[j])
        x2, h_packed, route, counts = _xattn_router(
            x, norm_xattn[i], qk_mem, vo_mem, i, norm_ffn[i],
            moe_w_grp[i], moe_b_grp[i], moe_w_exp[i], moe_b_exp[i])
        dest, gates, block_expert, block_first, block_valid, n_rows = _moe_layout(route, counts)
        x_buf = _dispatch(h_packed, dest, n_rows)
        y_buf = _experts(x_buf, block_expert, block_first, block_valid, moe_w_gate, moe_w_up, moe_w_down, i)
        y0, y1 = _gather_pairs(y_buf, dest)
        x, moe = x2, (gates, y0, y1)
    return _combine(x.reshape(b * s, d), moe[1], moe[2], moe[0], norm_final, True).reshape(b, s, d)
```

```python
import functools

import jax
import jax.numpy as jnp
from jax import lax
from jax.experimental import pallas as pl
from jax.experimental.pallas import tpu as pltpu
from jax.experimental.pallas import tpu_sc as plsc

F32 = jnp.float32
BF16 = jnp.bfloat16
I32 = jnp.int32
U32 = jnp.uint32

EPS = 1e-6
CONV_KERNEL = 31
CONV_CARRY = 32
CONV_ROWS = 64
CONV_COLS = 256
SUBLANES = 8
LANES = 128
GLA_HEADS = 4
GLA_RANK = 16
GLA_RANK_PAD = 128
GLA_TAU = 16.0
GLA_LEAF = 32
XATTN_HEADS = 4
N_GROUPS = 4
EXPERTS_PER_GROUP = 8
N_EXPERTS = N_GROUPS * EXPERTS_PER_GROUP
ROUTER_ROWS = 40
TOP_K = 2
ROUTE_ROWS = 8

TILE_CONV = 512
TILE_GLA = 256
TILE_XATTN = 1024
TILE_COMBINE = 512
SC_CORES = 2
SC_SUBCORES = 16
SC_WORKERS = SC_CORES * SC_SUBCORES
SC_DISPATCH_CHUNK = 64
SC_GATHER_CHUNK = 32
MOE_BLOCK_ROWS = 512
VMEM_LIMIT = 56 * 1024 * 1024


def _cparams(sem):
    return pltpu.CompilerParams(dimension_semantics=sem, vmem_limit_bytes=VMEM_LIMIT)


def _rms(x, g):
    return x * lax.rsqrt(jnp.mean(x * x, axis=-1, keepdims=True) + EPS) * g


def _sigmoid(x):
    return 0.5 * jnp.tanh(0.5 * x) + 0.5


def _dot(a, b):
    return jnp.dot(a, b, preferred_element_type=F32)


def _dot_nt(a, b):
    return lax.dot_general(a, b, (((1,), (1,)), ((), ())), preferred_element_type=F32)


def _dot_tn(a, b):
    return lax.dot_general(a, b, (((0,), (0,)), ((), ())), preferred_element_type=F32)


def _pack_bf16_pairs(x):
    w = x.shape[1] // 2
    hi = lax.bitcast_convert_type(x[:, :w].astype(BF16).astype(F32), U32)
    lo = lax.bitcast_convert_type(x[:, w:].astype(BF16).astype(F32), U32)
    return hi | (lo >> 16)


def _unpack_bf16_pairs(p):
    hi = lax.bitcast_convert_type(p & jnp.uint32(0xFFFF0000), F32)
    lo = lax.bitcast_convert_type(p << 16, F32)
    return jnp.concatenate([hi, lo], axis=1)


def _memkv_kernel(mem_ref, g_ref, wkv_ref, wq_ref, wo_ref, qk_ref, vo_ref):
    nm, d = mem_ref.shape[1], mem_ref.shape[2]
    hd = d // XATTN_HEADS
    mn = _rms(mem_ref[0], g_ref[...]).astype(BF16)
    kv = _dot(mn, wkv_ref[0])
    k = kv[:, :d].astype(BF16)
    v = kv[:, d:].astype(BF16)
    for a in range(XATTN_HEADS):
        sl = slice(a * hd, (a + 1) * hd)
        qk_ref[0, 0, :, a * nm:(a + 1) * nm] = (_dot_nt(wq_ref[0, :, sl], k[:, sl]) * (hd ** -0.5)).astype(BF16)
        vo_ref[0, 0, a * nm:(a + 1) * nm, :] = _dot(v[:, sl], wo_ref[0, sl, :]).astype(BF16)


def _mem_kv(mem, norm_mem, w_kv, w_q, w_o):
    b, nm, d = mem.shape
    depth = w_kv.shape[0]
    per_layer = lambda shape: pl.BlockSpec((1,) + shape, lambda l, i: (l, 0, 0))
    return pl.pallas_call(
        _memkv_kernel,
        out_shape=(jax.ShapeDtypeStruct((depth, b, d, XATTN_HEADS * nm), BF16),
                   jax.ShapeDtypeStruct((depth, b, XATTN_HEADS * nm, d), BF16)),
        grid=(depth, b),
        in_specs=[
            pl.BlockSpec((1, nm, d), lambda l, i: (i, 0, 0)),
            pl.BlockSpec((1, d), lambda l, i: (0, 0)),
            per_layer((d, 2 * d)), per_layer((d, d)), per_layer((d, d)),
        ],
        out_specs=(
            pl.BlockSpec((1, 1, d, XATTN_HEADS * nm), lambda l, i: (l, i, 0, 0)),
            pl.BlockSpec((1, 1, XATTN_HEADS * nm, d), lambda l, i: (l, i, 0, 0)),
        ),
        compiler_params=_cparams(("arbitrary", "arbitrary")),
        name="mem_kv",
    )(mem, norm_mem.reshape(1, d), w_kv.astype(BF16), w_q.astype(BF16), w_o.astype(BF16))


def _conv_kernel(x_ref, g_ref, win_ref, bin_ref, wdw_ref, bdw_ref, lng_ref, lnb_ref, wout_ref, bout_ref,
                 o_ref, ext_ref, even_ref, odd_ref):
    ts, d = x_ref.shape[1], x_ref.shape[2]
    n_shifts = ext_ref.shape[0]

    @pl.when(pl.program_id(1) == 0)
    def _():
        ext_ref[...] = jnp.zeros(ext_ref.shape, F32)

    x = x_ref[0]
    h = _rms(x, g_ref[...]).astype(BF16)
    u = _dot(h, win_ref[...]) + bin_ref[...]
    glu = u[:, :d] * _sigmoid(u[:, d:])
    for c in range(n_shifts):
        ext_ref[c, CONV_CARRY - 2 * c:CONV_CARRY - 2 * c + ts, :] = glu

    first = CONV_CARRY - (CONV_KERNEL - 1)

    def taps(r0, n_rows, parity, dst_ref):
        for c0 in range(0, d, CONV_COLS):
            cols = slice(c0, c0 + CONV_COLS)
            acc = [jnp.zeros((SUBLANES, CONV_COLS), F32) for _ in range(n_rows // SUBLANES)]
            for k in range(CONV_KERNEL):
                if (first + k) % 2 != parity:
                    continue
                shift = (first + k - parity) % SUBLANES
                base = first + k - parity - shift
                wk = wdw_ref[k * SUBLANES:(k + 1) * SUBLANES, cols]
                for j in range(n_rows // SUBLANES):
                    rows = pl.ds(r0 + base + j * SUBLANES, SUBLANES)
                    acc[j] = acc[j] + ext_ref[shift // 2, rows, cols] * wk
            for j in range(n_rows // SUBLANES):
                dst_ref[pl.ds(r0 + j * SUBLANES, SUBLANES), cols] = acc[j]

    def chunk(i, carry):
        r0 = pl.multiple_of(i * CONV_ROWS, CONV_ROWS)
        taps(r0, CONV_ROWS, 0, even_ref)
        taps(r0, CONV_ROWS, 1, odd_ref)
        return carry

    lax.fori_loop(0, ts // CONV_ROWS, chunk, 0)
    taps(ts, SUBLANES, 1, odd_ref)
    for c in range(n_shifts):
        ext_ref[c, 0:CONV_CARRY, :] = ext_ref[c, ts:ts + CONV_CARRY, :]

    c = even_ref[...] + odd_ref[1:ts + 1, :] + bdw_ref[...]
    mu = jnp.mean(c, axis=-1, keepdims=True)
    cc = c - mu
    var = jnp.mean(cc * cc, axis=-1, keepdims=True)
    un = cc * lax.rsqrt(var + EPS) * lng_ref[...] + lnb_ref[...]
    act = (un * _sigmoid(un)).astype(BF16)
    o_ref[0] = x + _dot(act, wout_ref[...]) + bout_ref[...]


def _conv_mixer(x, g, w_in, b_in, w_dw, b_dw, ln_g, ln_b, w_out, b_out):
    b, s, d = x.shape
    ts = min(TILE_CONV, s)
    assert s % ts == 0 and ts % CONV_ROWS == 0 and d % CONV_COLS == 0
    row = lambda v: v.reshape(1, -1)
    const = lambda shape: pl.BlockSpec(shape, lambda i, j: (0,) * len(shape))
    return pl.pallas_call(
        _conv_kernel,
        out_shape=jax.ShapeDtypeStruct(x.shape, F32),
        grid=(b, s // ts),
        in_specs=[
            pl.BlockSpec((1, ts, d), lambda i, j: (i, j, 0)),
            const((1, d)), const((d, 2 * d)), const((1, 2 * d)), const((CONV_KERNEL * SUBLANES, d)), const((1, d)),
            const((1, d)), const((1, d)), const((d, d)), const((1, d)),
        ],
        out_specs=pl.BlockSpec((1, ts, d), lambda i, j: (i, j, 0)),
        scratch_shapes=[pltpu.VMEM((SUBLANES // 2, CONV_CARRY + ts, d), F32), pltpu.VMEM((ts, d), F32),
                        pltpu.VMEM((ts + SUBLANES, d), F32)],
        compiler_params=_cparams(("arbitrary", "arbitrary")),
        name="conv_mixer",
    )(x, row(g), w_in.astype(BF16), row(b_in), jnp.repeat(w_dw, SUBLANES, axis=0), row(b_dw), row(ln_g), row(ln_b),
      w_out.astype(BF16), row(b_out))


def _gla_levels(ts):
    sizes = [GLA_LEAF]
    while sizes[-1] < ts:
        sizes.append(sizes[-1] * 2)
    return sizes


def _moe_residual(x, route, y0, y1):
    g = lax.bitcast_convert_type(route, F32)
    g = jnp.concatenate([g, jnp.zeros((LANES - g.shape[0], g.shape[1]), F32)], axis=0).T
    gate0 = g[:, 2 * TOP_K:2 * TOP_K + 1]
    gate1 = g[:, 2 * TOP_K + 1:2 * TOP_K + 2]
    return x + _unpack_bf16_pairs(y0) * gate0 + _unpack_bf16_pairs(y1) * gate1


def _gla_kernel(x_ref, gate_ref, y0_ref, y1_ref, g_ref, wq_ref, wk_ref, wv_ref, wa_ref, wr_ref, wa2_ref, ba_ref,
                ng_ref, wo_ref, o_ref, state_ref, x_s, q_s, k_s, v_s, og_s, la_s, *, tiles_per_seq):
    j = pl.program_id(0)
    dkh = wq_ref.shape[1] // GLA_HEADS

    @pl.when(j == 0)
    def _():
        for ref in (x_s, q_s, k_s, og_s, la_s):
            ref[...] = jnp.zeros(ref.shape, F32)
        v_s[...] = jnp.zeros(v_s.shape, BF16)

    @pl.when((j == 0) | ((j - 1) % tiles_per_seq == 0))
    def _():
        state_ref[...] = jnp.zeros(state_ref.shape, F32)

    def stage_a(slot):
        x = _moe_residual(x_ref[0], gate_ref[...], y0_ref[...], y1_ref[...])
        h = _rms(x, g_ref[...]).astype(BF16)
        x_s[slot] = x
        q_s[slot] = _dot(h, wq_ref[...]) * (dkh ** -0.5)
        k_s[slot] = _dot(h, wk_ref[...])
        v_s[slot] = _dot(h, wv_ref[...]).astype(BF16)
        r = _dot(h, wr_ref[...])
        og_s[slot] = r * _sigmoid(r)
        a = _dot(h, wa_ref[...]).astype(BF16)
        z = _dot(jnp.concatenate([a, a], axis=1), wa2_ref[...]) + ba_ref[...]
        la_s[slot] = -(jnp.maximum(-z, 0.0) + jnp.log(1.0 + jnp.exp(-jnp.abs(z)))) * (1.0 / GLA_TAU)

    def stage_b(slot):
        _gla_recurrence(x_s[slot], q_s[slot], k_s[slot], v_s[slot], og_s[slot], la_s[slot],
                        ng_ref, wo_ref, o_ref, state_ref)

    @pl.when(j % 2 == 0)
    def _():
        stage_b(1)
        stage_a(0)

    @pl.when(j % 2 == 1)
    def _():
        stage_b(0)
        stage_a(1)


def _gla_recurrence(x, q, k, v, out_gate, log_a, ng_ref, wo_ref, o_ref, state_ref):
    ts, d = x.shape
    dk = q.shape[1]
    dkh = dk // GLA_HEADS
    dvh = d // GLA_HEADS

    row = lax.broadcasted_iota(I32, (ts, ts), 0)
    col = lax.broadcasted_iota(I32, (ts, ts), 1)
    tri = jnp.where(col <= row, 1.0, 0.0).astype(BF16)
    bcum = _dot(tri, log_a.astype(BF16))
    b_last = bcum[ts - 1:ts, :]

    q_in = (q * jnp.exp(bcum)).astype(BF16)
    k_out = (k * jnp.exp(b_last - bcum)).astype(BF16)

    sizes = _gla_levels(ts)
    scores = [None] * GLA_HEADS
    for lvl, size in enumerate(sizes):
        half = size // 2
        same_block = (row & -size) == (col & -size)
        if lvl == 0:
            pair = same_block & (col <= row)
            q_ok = k_ok = None
        else:
            pair = same_block & ((row & (size - 1)) >= half) & ((col & (size - 1)) < half)
            pos = lax.broadcasted_iota(I32, (ts, dk), 0) & (size - 1)
            q_ok = pos >= half
            k_ok = pos < half
        ref = jnp.concatenate(
            [jnp.broadcast_to(bcum[r0 + half:r0 + half + 1, :], (size, dk)) for r0 in range(0, ts, size)], axis=0)
        ql = q * jnp.exp(bcum - ref)
        kl = k * jnp.exp(ref - bcum)
        if q_ok is not None:
            ql = jnp.where(q_ok, ql, 0.0)
            kl = jnp.where(k_ok, kl, 0.0)
        ql = ql.astype(BF16)
        kl = kl.astype(BF16)
        for hd in range(GLA_HEADS):
            c0 = hd * dkh
            a = _dot_nt(ql[:, c0:c0 + dkh], kl[:, c0:c0 + dkh])
            a = jnp.where(pair, a, 0.0)
            scores[hd] = a if scores[hd] is None else scores[hd] + a

    outs = []
    for hd in range(GLA_HEADS):
        c0 = hd * dkh
        v_h = v[:, hd * dvh:(hd + 1) * dvh]
        st = state_ref[hd]
        o_h = _dot(scores[hd].astype(BF16), v_h) + _dot_nt(q_in[:, c0:c0 + dkh], st.astype(BF16))
        decay = jnp.exp(b_last[:, c0:c0 + dkh])
        state_ref[hd] = st * decay + _dot_tn(v_h, k_out[:, c0:c0 + dkh])
        o_h = o_h * lax.rsqrt(jnp.mean(o_h * o_h, axis=-1, keepdims=True) + EPS) * ng_ref[...]
        outs.append(o_h)
    o = jnp.concatenate(outs, axis=1) * out_gate
    o_ref[0] = x + _dot(o.astype(BF16), wo_ref[...])


def _gla_mixer(x, gates, y0, y1, g, w_in, w_a2, b_a, norm_g, w_o):
    b, s, d = x.shape
    dk = w_a2.shape[1]
    ts = min(TILE_GLA, s)
    assert s % ts == 0 and ts % GLA_LEAF == 0
    nj = s // ts
    row = lambda v: v.reshape(1, -1)
    const = lambda shape: pl.BlockSpec(shape, lambda j: (0,) * len(shape))
    wq = w_in[:, :dk].astype(BF16)
    wk = w_in[:, dk:2 * dk].astype(BF16)
    wv = w_in[:, 2 * dk:2 * dk + d].astype(BF16)
    wa = jnp.pad(w_in[:, 2 * dk + d:2 * dk + d + GLA_RANK], ((0, 0), (0, GLA_RANK_PAD - GLA_RANK))).astype(BF16)
    wr = w_in[:, 2 * dk + d + GLA_RANK:].astype(BF16)
    wa2 = jnp.pad(w_a2, ((0, GLA_RANK_PAD - GLA_RANK), (0, 0)))
    wa2_hi = wa2.astype(BF16)
    wa2_lo = (wa2 - wa2_hi.astype(F32)).astype(BF16)
    wa2_split = jnp.concatenate([wa2_hi, wa2_lo], axis=0)
    dvh = d // GLA_HEADS
    n = b * nj

    def tile(j, lag):
        return jnp.clip(j - lag, 0, n - 1)

    tokens = lambda lag: (lambda j: (tile(j, lag), 0))
    return pl.pallas_call(
        functools.partial(_gla_kernel, tiles_per_seq=nj),
        out_shape=jax.ShapeDtypeStruct(x.shape, F32),
        grid=(n + 1,),
        in_specs=[
            pl.BlockSpec((1, ts, d), lambda j: (tile(j, 0) // nj, tile(j, 0) % nj, 0)),
            pl.BlockSpec((ROUTE_ROWS, ts), lambda j: (0, tile(j, 0))),
            pl.BlockSpec((ts, d // 2), tokens(0)),
            pl.BlockSpec((ts, d // 2), tokens(0)),
            const((1, d)), const((d, dk)), const((d, dk)), const((d, d)), const((d, GLA_RANK_PAD)),
            const((d, d)), const((2 * GLA_RANK_PAD, dk)), const((1, dk)), const((1, dvh)), const((d, d)),
        ],
        out_specs=pl.BlockSpec((1, ts, d), lambda j: (tile(j, 1) // nj, tile(j, 1) % nj, 0)),
        scratch_shapes=[pltpu.VMEM((GLA_HEADS, dvh, dk // GLA_HEADS), F32),
                        pltpu.VMEM((2, ts, d), F32), pltpu.VMEM((2, ts, dk), F32), pltpu.VMEM((2, ts, dk), F32),
                        pltpu.VMEM((2, ts, d), BF16), pltpu.VMEM((2, ts, d), F32), pltpu.VMEM((2, ts, dk), F32)],
        compiler_params=_cparams(("arbitrary",)),
        name="gla_mixer",
    )(x, gates, y0, y1, row(g), wq, wk, wv, wa, wr, wa2_split, row(b_a), row(norm_g), w_o.astype(BF16))


def _xattn_router_kernel(x_ref, gx_ref, qk_ref, vo_ref, gf_ref, wr_ref, br_ref, upper_ref,
                         x_out_ref, h_out_ref, route_ref, cnt_ref, carry_ref):
    ts, d = x_ref.shape[1], x_ref.shape[2]
    nm = qk_ref.shape[3] // XATTN_HEADS
    first = (pl.program_id(0) == 0) & (pl.program_id(1) == 0)

    @pl.when(first)
    def _():
        carry_ref[...] = jnp.zeros(carry_ref.shape, F32)

    x = x_ref[0]
    h = _rms(x, gx_ref[...]).astype(BF16)
    scores = _dot(h, qk_ref[0, 0])
    probs = []
    for a in range(XATTN_HEADS):
        s = scores[:, a * nm:(a + 1) * nm]
        p = jnp.exp(s - jnp.max(s, axis=-1, keepdims=True))
        probs.append((p / jnp.sum(p, axis=-1, keepdims=True)).astype(BF16))
    x2 = x + _dot(jnp.concatenate(probs, axis=1), vo_ref[0, 0])
    x_out_ref[0] = x2

    hf = _rms(x2, gf_ref[...])
    h_out_ref[...] = _pack_bf16_pairs(hf)

    both = _dot_nt(wr_ref[...], hf.astype(BF16))
    logits = both[0:ROUTER_ROWS, :] + both[ROUTER_ROWS:2 * ROUTER_ROWS, :] + br_ref[...]
    gl = logits[N_EXPERTS:N_EXPERTS + N_GROUPS, :]
    gi = lax.broadcasted_iota(I32, gl.shape, 0).astype(F32)
    gmax = jnp.max(gl, axis=0, keepdims=True)
    g_sel = jnp.min(jnp.where(gl == gmax, gi, float(N_GROUPS)), axis=0, keepdims=True)
    pg_sel = 1.0 / jnp.sum(jnp.exp(gl - gmax), axis=0, keepdims=True)

    el = jnp.zeros((EXPERTS_PER_GROUP, ts), F32)
    for gidx in range(N_GROUPS):
        lo = gidx * EXPERTS_PER_GROUP
        el = jnp.where(g_sel == float(gidx), logits[lo:lo + EXPERTS_PER_GROUP, :], el)
    ei = lax.broadcasted_iota(I32, el.shape, 0).astype(F32)
    m1 = jnp.max(el, axis=0, keepdims=True)
    i1 = jnp.min(jnp.where(el == m1, ei, float(EXPERTS_PER_GROUP)), axis=0, keepdims=True)
    rest = jnp.where(ei == i1, -jnp.inf, el)
    m2 = jnp.max(rest, axis=0, keepdims=True)
    i2 = jnp.min(jnp.where(rest == m2, ei, float(EXPERTS_PER_GROUP)), axis=0, keepdims=True)
    ratio = jnp.exp(m2 - m1)
    gate1 = pg_sel / (1.0 + ratio)
    gate2 = pg_sel * ratio / (1.0 + ratio)
    e1 = g_sel * float(EXPERTS_PER_GROUP) + i1
    e2 = g_sel * float(EXPERTS_PER_GROUP) + i2

    xi = lax.broadcasted_iota(I32, (N_EXPERTS, ts), 0).astype(F32)
    oh1 = jnp.where(xi == e1, 1.0, 0.0)
    oh2 = jnp.where(xi == e2, 1.0, 0.0)
    oh = oh1 + oh2
    n_blk = ts // LANES
    stacked = jnp.concatenate([oh[:, c * LANES:(c + 1) * LANES] for c in range(n_blk)], axis=0)
    within = _dot(stacked.astype(BF16), upper_ref[...])
    totals = jnp.sum(stacked, axis=1, keepdims=True)
    run = carry_ref[...]
    before = []
    for c in range(n_blk):
        before.append(within[c * N_EXPERTS:(c + 1) * N_EXPERTS, :] + run)
        run = run + totals[c * N_EXPERTS:(c + 1) * N_EXPERTS, :]
    before = jnp.concatenate(before, axis=1)
    rank1 = jnp.sum(oh1 * before, axis=0, keepdims=True)
    rank2 = jnp.sum(oh2 * before, axis=0, keepdims=True)
    carry_ref[...] = run
    cnt_ref[...] = jnp.broadcast_to(run, cnt_ref.shape)

    route_ref[0:1, :] = e1.astype(I32)
    route_ref[1:2, :] = e2.astype(I32)
    route_ref[2:3, :] = rank1.astype(I32)
    route_ref[3:4, :] = rank2.astype(I32)
    route_ref[4:5, :] = lax.bitcast_convert_type(gate1, I32)
    route_ref[5:6, :] = lax.bitcast_convert_type(gate2, I32)
    route_ref[3 * TOP_K:ROUTE_ROWS, :] = jnp.zeros((ROUTE_ROWS - 3 * TOP_K, ts), I32)


def _xattn_router(x, g_x, qk_mem, vo_mem, layer, g_f, w_grp, b_grp, w_exp, b_exp):
    b, s, d = x.shape
    hm = qk_mem.shape[3]
    t = b * s
    ts = min(TILE_XATTN, s)
    assert s % ts == 0 and ts % LANES == 0
    nj = s // ts
    row = lambda v: v.reshape(1, -1)
    const = lambda shape: pl.BlockSpec(shape, lambda i, j: (0,) * len(shape))
    pad = ROUTER_ROWS - N_GROUPS - N_EXPERTS
    w_r = jnp.pad(jnp.concatenate([w_exp, w_grp], axis=1).T, ((0, pad), (0, 0)))
    w_r_hi = w_r.astype(BF16)
    w_r_split = jnp.concatenate([w_r_hi, (w_r - w_r_hi.astype(F32)).astype(BF16)], axis=0)
    b_r = jnp.pad(jnp.concatenate([b_exp, b_grp]), (0, pad)).reshape(ROUTER_ROWS, 1)
    ti = jnp.arange(LANES)
    upper = (ti[:, None] < ti[None, :]).astype(BF16)
    return pl.pallas_call(
        _xattn_router_kernel,
        out_shape=(
            jax.ShapeDtypeStruct(x.shape, F32),
            jax.ShapeDtypeStruct((t, d // 2), U32),
            jax.ShapeDtypeStruct((ROUTE_ROWS, t), I32),
            jax.ShapeDtypeStruct((N_EXPERTS, LANES), F32),
        ),
        grid=(b, nj),
        in_specs=[
            pl.BlockSpec((1, ts, d), lambda i, j: (i, j, 0)),
            const((1, d)),
            pl.BlockSpec((1, 1, d, hm), lambda i, j: (layer, i, 0, 0)),
            pl.BlockSpec((1, 1, hm, d), lambda i, j: (layer, i, 0, 0)),
            const((1, d)), const((2 * ROUTER_ROWS, d)), const((ROUTER_ROWS, 1)), const((LANES, LANES)),
        ],
        out_specs=(
            pl.BlockSpec((1, ts, d), lambda i, j: (i, j, 0)),
            pl.BlockSpec((ts, d // 2), lambda i, j: (i * nj + j, 0)),
            pl.BlockSpec((ROUTE_ROWS, ts), lambda i, j: (0, i * nj + j)),
            pl.BlockSpec((N_EXPERTS, LANES), lambda i, j: (0, 0)),
        ),
        scratch_shapes=[pltpu.VMEM((N_EXPERTS, 1), F32)],
        compiler_params=_cparams(("arbitrary", "arbitrary")),
        name="xattn_router",
    )(x, row(g_x), qk_mem, vo_mem, row(g_f), w_r_split, b_r, upper)


def _sc_mesh():
    return plsc.VectorSubcoreMesh(core_axis_name="c", subcore_axis_name="s",
                                  num_cores=SC_CORES, num_subcores=SC_SUBCORES)


def _sc_worker():
    return lax.axis_index("s") * SC_CORES + lax.axis_index("c")


def _dispatch(h_packed, dest, n_rows):
    t, w = h_packed.shape
    chunk = SC_DISPATCH_CHUNK
    per_worker = t // SC_WORKERS
    n_chunks = per_worker // chunk
    assert n_chunks % 2 == 0 and n_chunks * chunk * SC_WORKERS == t
    dest = dest.reshape(TOP_K, SC_WORKERS * n_chunks, chunk)
    rows_buf = pltpu.VMEM((chunk, w), U32)

    @functools.partial(
        pl.kernel, mesh=_sc_mesh(),
        out_type=jax.ShapeDtypeStruct((n_rows, w), U32),
        scratch_types=[pltpu.VMEM((n_chunks, chunk), I32), pltpu.VMEM((n_chunks, chunk), I32), rows_buf, rows_buf,
                       pltpu.SemaphoreType.DMA((2,)), pltpu.SemaphoreType.DMA((2, TOP_K))],
        name="moe_dispatch_sc",
    )
    def run(h_hbm, d0_hbm, d1_hbm, xbuf_hbm, idx0_v, idx1_v, buf_a, buf_b, read_sem, write_sem):
        wid = _sc_worker()
        pltpu.sync_copy(d0_hbm.at[pl.ds(wid * n_chunks, n_chunks)], idx0_v)
        pltpu.sync_copy(d1_hbm.at[pl.ds(wid * n_chunks, n_chunks)], idx1_v)

        @pl.loop(0, n_chunks, step=2)
        def _(i):
            reads = [pltpu.async_copy(h_hbm.at[pl.ds(wid * per_worker + (i + j) * chunk, chunk)], buf, read_sem.at[j])
                     for j, buf in enumerate((buf_a, buf_b))]
            writes = []
            for j, buf in enumerate((buf_a, buf_b)):
                reads[j].wait()
                writes.append(pltpu.async_copy(buf, xbuf_hbm.at[idx0_v.at[i + j]], write_sem.at[j, 0]))
                writes.append(pltpu.async_copy(buf, xbuf_hbm.at[idx1_v.at[i + j]], write_sem.at[j, 1]))
            for copy in writes:
                copy.wait()

    return run(h_packed, dest[0], dest[1])


def _gather_pairs(y_buf, dest):
    t = dest.shape[1]
    w = y_buf.shape[1]
    chunk = SC_GATHER_CHUNK
    per_worker = t // SC_WORKERS
    n_chunks = per_worker // chunk
    assert n_chunks % 2 == 0 and n_chunks * chunk * SC_WORKERS == t
    dest = dest.reshape(TOP_K, SC_WORKERS * n_chunks, chunk)
    out = jax.ShapeDtypeStruct((t, w), U32)
    rows_buf = pltpu.VMEM((chunk, w), U32)

    @functools.partial(
        pl.kernel, mesh=_sc_mesh(),
        out_type=(out, out),
        scratch_types=[pltpu.VMEM((n_chunks, chunk), I32), pltpu.VMEM((n_chunks, chunk), I32),
                       rows_buf, rows_buf, rows_buf, rows_buf, pltpu.SemaphoreType.DMA((2, TOP_K))],
        name="moe_gather_sc",
    )
    def run(y_hbm, d0_hbm, d1_hbm, y0_hbm, y1_hbm, idx0_v, idx1_v, buf_a0, buf_a1, buf_b0, buf_b1, sem):
        wid = _sc_worker()
        pltpu.sync_copy(d0_hbm.at[pl.ds(wid * n_chunks, n_chunks)], idx0_v)
        pltpu.sync_copy(d1_hbm.at[pl.ds(wid * n_chunks, n_chunks)], idx1_v)

        @pl.loop(0, n_chunks, step=2)
        def _(i):
            bufs = ((buf_a0, buf_a1), (buf_b0, buf_b1))
            gathers = [[pltpu.async_copy(y_hbm.at[idx_v.at[i + j]], bufs[j][k], sem.at[j, k])
                        for k, idx_v in enumerate((idx0_v, idx1_v))] for j in range(2)]
            stores = []
            for j in range(2):
                rows = pl.ds(wid * per_worker + (i + j) * chunk, chunk)
                for k, out_hbm in enumerate((y0_hbm, y1_hbm)):
                    gathers[j][k].wait()
                    stores.append(pltpu.async_copy(bufs[j][k], out_hbm.at[rows], sem.at[j, k]))
            for copy in stores:
                copy.wait()

    return run(y_buf, dest[0], dest[1])


def _expert_kernel(be_ref, bf_ref, bv_ref, slot_ref, next_ref, x_ref, wg_hbm, wu_hbm, wd_hbm, y_ref,
                   wg_f32, wu_f32, wd_f32, wg_bf, wu_bf, wd_bf, sems, *, layer):
    blk = pl.program_id(0)
    valid = bv_ref[blk]

    def weight_copies(expert, slot):
        return [pltpu.make_async_copy(src.at[layer, expert], dst.at[slot], sems.at[slot, i])
                for i, (src, dst) in enumerate(((wg_hbm, wg_f32), (wu_hbm, wu_f32), (wd_hbm, wd_f32)))]

    @pl.when(blk == 0)
    def _():
        for copy in weight_copies(be_ref[0], 0):
            copy.start()

    @pl.when(bf_ref[blk] == 1)
    def _():
        slot = slot_ref[blk]
        for copy in weight_copies(be_ref[blk], slot):
            copy.wait()

        @pl.when(next_ref[blk] >= 0)
        def _():
            for copy in weight_copies(next_ref[blk], 1 - slot):
                copy.start()

        wg_bf[...] = wg_f32[slot].astype(BF16)
        wu_bf[...] = wu_f32[slot].astype(BF16)
        wd_bf[...] = wd_f32[slot].astype(BF16)

    @pl.when(valid > 0)
    def _():
        live = lax.broadcasted_iota(I32, x_ref.shape, 0) < valid
        xb = _unpack_bf16_pairs(jnp.where(live, x_ref[...], jnp.uint32(0))).astype(BF16)
        gt = _dot(xb, wg_bf[...])
        up = _dot(xb, wu_bf[...])
        act = (gt * _sigmoid(gt) * up).astype(BF16)
        y_ref[...] = _pack_bf16_pairs(_dot(act, wd_bf[...]))

    @pl.when(valid <= 0)
    def _():
        y_ref[...] = jnp.zeros(y_ref.shape, U32)


def _experts(x_buf, block_expert, block_first, block_valid, w_gate, w_up, w_down, layer):
    n_rows, w = x_buf.shape
    d, de = w_gate.shape[2], w_gate.shape[3]
    bm = MOE_BLOCK_ROWS
    n = n_rows // bm
    block_slot = (jnp.cumsum(block_first) - 1) % 2
    idx = jnp.arange(n, dtype=I32)
    later_first = jnp.concatenate([jnp.where(block_first[1:] == 1, idx[1:], n), jnp.full((1,), n, I32)])
    next_first = lax.cummin(later_first, reverse=True)
    block_next = jnp.where(next_first < n, block_expert[jnp.minimum(next_first, n - 1)], -1).astype(I32)
    any_space = pl.BlockSpec(memory_space=pl.ANY)
    grid_spec = pltpu.PrefetchScalarGridSpec(
        num_scalar_prefetch=5,
        grid=(n,),
        in_specs=[pl.BlockSpec((bm, w), lambda i, *_: (i, 0)), any_space, any_space, any_space],
        out_specs=pl.BlockSpec((bm, w), lambda i, *_: (i, 0)),
        scratch_shapes=[pltpu.VMEM((2, d, de), F32), pltpu.VMEM((2, d, de), F32), pltpu.VMEM((2, de, d), F32),
                        pltpu.VMEM((d, de), BF16), pltpu.VMEM((d, de), BF16), pltpu.VMEM((de, d), BF16),
                        pltpu.SemaphoreType.DMA((2, 3))],
    )
    return pl.pallas_call(
        functools.partial(_expert_kernel, layer=layer),
        out_shape=jax.ShapeDtypeStruct((n_rows, w), U32),
        grid_spec=grid_spec,
        compiler_params=_cparams(("arbitrary",)),
        name="moe_experts",
    )(block_expert, block_first, block_valid, block_slot.astype(I32), block_next, x_buf, w_gate, w_up, w_down)


def _combine_kernel(x_ref, gate_ref, y0_ref, y1_ref, gfin_ref, o_ref, *, final_norm):
    out = _moe_residual(x_ref[...], gate_ref[...], y0_ref[...], y1_ref[...])
    if final_norm:
        out = _rms(out, gfin_ref[...])
    o_ref[...] = out


def _combine(x2, y0, y1, gates, g_final, final_norm):
    t, d = x2.shape
    w = y0.shape[1]
    ts = min(TILE_COMBINE, t)
    assert t % ts == 0
    return pl.pallas_call(
        functools.partial(_combine_kernel, final_norm=final_norm),
        out_shape=jax.ShapeDtypeStruct((t, d), F32),
        grid=(t // ts,),
        in_specs=[
            pl.BlockSpec((ts, d), lambda i: (i, 0)),
            pl.BlockSpec((ROUTE_ROWS, ts), lambda i: (0, i)),
            pl.BlockSpec((ts, w), lambda i: (i, 0)),
            pl.BlockSpec((ts, w), lambda i: (i, 0)),
            pl.BlockSpec((1, d), lambda i: (0, 0)),
        ],
        out_specs=pl.BlockSpec((ts, d), lambda i: (i, 0)),
        compiler_params=_cparams(("arbitrary",)),
        name="moe_combine",
    )(x2, gates, y0, y1, g_final.reshape(1, d))


def _moe_layout(route, counts):
    bm = MOE_BLOCK_ROWS
    t = route.shape[1]
    assert (t * TOP_K) % bm == 0
    n_blocks = (t * TOP_K) // bm + N_EXPERTS
    cnt = counts[:, 0].astype(I32)
    padded = (cnt + bm - 1) // bm * bm
    pad_ends = jnp.cumsum(padded)
    pad_off = pad_ends - padded
    experts = jnp.arange(N_EXPERTS, dtype=I32)
    hit = route[0:TOP_K, :, None] == experts
    dest = jnp.sum(jnp.where(hit, pad_off, 0), axis=-1) + route[TOP_K:2 * TOP_K]
    gates = route
    starts = jnp.arange(n_blocks, dtype=I32) * bm
    block_expert = jnp.minimum(jnp.sum((pad_ends[None, :] <= starts[:, None]).astype(I32), axis=1),
                               N_EXPERTS - 1)
    block_first = jnp.concatenate([jnp.ones((1,), I32), (block_expert[1:] != block_expert[:-1]).astype(I32)])
    own = block_expert[:, None] == experts
    block_valid = jnp.clip(jnp.sum(jnp.where(own, cnt + pad_off, 0), axis=1) - starts, 0, bm)
    block_valid = jnp.where(starts < pad_ends[-1], block_valid, 0).astype(I32)
    return dest, gates, block_expert, block_first, block_valid, n_blocks * bm


def kernel(x, mem, norm_mix, norm_xattn, norm_ffn, norm_mem, norm_final, conv_w_in, conv_b_in, conv_w_dw,
           conv_b_dw, conv_ln_g, conv_ln_b, conv_w_out, conv_b_out, gla_w_in, gla_w_a2, gla_b_a, gla_norm_g,
           gla_w_o, xa_w_q, xa_w_kv, xa_w_o, moe_w_grp, moe_b_grp, moe_w_exp, moe_b_exp, moe_w_gate, moe_w_up,
           moe_w_down):
    b, s, d = x.shape
    depth = norm_mix.shape[0]
    qk_mem, vo_mem = _mem_kv(mem, norm_mem, xa_w_kv, xa_w_q, xa_w_o)
    moe = None
    for i in range(depth):
        j = i // 2
        if i % 2 == 0:
            if moe is not None:
                x = _combine(x.reshape(b * s, d), moe[1], moe[2], moe[0], norm_final, False).reshape(b, s, d)
            x = _conv_mixer(x, norm_mix[i], conv_w_in[j], conv_b_in[j], conv_w_dw[j], conv_b_dw[j],
                            conv_ln_g[j], conv_ln_b[j], conv_w_out[j], conv_b_out[j])
        else:
            x = _gla_mixer(x, *moe, norm_mix[i], gla_w_in[j] * after_router, gla_w_a2[j], gla_b_a[j], gla_norm_g[j],
                           gla_w_o[j] * after_router)
        x2, h_packed, route, counts = _xattn_router(
            x, norm_xattn[i], qk_mem, vo_mem, i, norm_ffn[i],
            moe_w_grp[i], moe_b_grp[i], moe_w_exp[i], moe_b_exp[i])
        after_router = jnp.where(counts[0, 0] < 0.0, 0.0, 1.0)
        dest, gates, block_expert, block_first, block_valid, n_rows = _moe_layout(route, counts)
        x_buf = _dispatch(h_packed, dest, n_rows)
        y_buf = _experts(x_buf, block_expert, block_first, block_valid, moe_w_gate, moe_w_up, moe_w_down, i)
        y0, y1 = _gather_pairs(y_buf, dest)
        x, moe = x2, (gates, y0, y1)
    return _combine(x.reshape(b * s, d), moe[1], moe[2], moe[0], norm_final, True).reshape(b, s, d)
```

```python
import functools

import jax
import jax.numpy as jnp
from jax import lax
from jax.experimental import pallas as pl
from jax.experimental.pallas import tpu as pltpu
from jax.experimental.pallas import tpu_sc as plsc

F32 = jnp.float32
BF16 = jnp.bfloat16
I32 = jnp.int32
U32 = jnp.uint32

EPS = 1e-6
CONV_KERNEL = 31
CONV_CARRY = 32
CONV_ROWS = 64
CONV_COLS = 256
SUBLANES = 8
LANES = 128
GLA_HEADS = 4
GLA_RANK = 16
GLA_RANK_PAD = 128
GLA_TAU = 16.0
GLA_LEAF = 32
XATTN_HEADS = 4
N_GROUPS = 4
EXPERTS_PER_GROUP = 8
N_EXPERTS = N_GROUPS * EXPERTS_PER_GROUP
ROUTER_ROWS = 40
TOP_K = 2
ROUTE_ROWS = 8

TILE_CONV = 512
TILE_GLA = 256
TILE_XATTN = 1024
TILE_COMBINE = 512
SC_CORES = 2
SC_SUBCORES = 16
SC_WORKERS = SC_CORES * SC_SUBCORES
SC_DISPATCH_CHUNK = 64
SC_GATHER_CHUNK = 32
MOE_BLOCK_ROWS = 512
VMEM_LIMIT = 56 * 1024 * 1024


def _cparams(sem):
    return pltpu.CompilerParams(dimension_semantics=sem, vmem_limit_bytes=VMEM_LIMIT)


def _rms(x, g):
    return x * lax.rsqrt(jnp.mean(x * x, axis=-1, keepdims=True) + EPS) * g


def _sigmoid(x):
    return 0.5 * jnp.tanh(0.5 * x) + 0.5


def _silu(x):
    h = 0.5 * x
    return h + h * jnp.tanh(h)


def _dot(a, b):
    return jnp.dot(a, b, preferred_element_type=F32)


def _dot_nt(a, b):
    return lax.dot_general(a, b, (((1,), (1,)), ((), ())), preferred_element_type=F32)


def _dot_tn(a, b):
    return lax.dot_general(a, b, (((0,), (0,)), ((), ())), preferred_element_type=F32)


def _pack_bf16_pairs(x):
    w = x.shape[1] // 2
    hi = lax.bitcast_convert_type(x[:, :w].astype(BF16).astype(F32), U32)
    lo = lax.bitcast_convert_type(x[:, w:].astype(BF16).astype(F32), U32)
    return hi | (lo >> 16)


def _unpack_bf16_pairs(p):
    hi = lax.bitcast_convert_type(p & jnp.uint32(0xFFFF0000), F32)
    lo = lax.bitcast_convert_type(p << 16, F32)
    return jnp.concatenate([hi, lo], axis=1)


def _memkv_kernel(mem_ref, g_ref, wkv_ref, wq_ref, wo_ref, qk_ref, vo_ref, wkv_bf, wq_bf, wo_bf):
    nm, d = mem_ref.shape[1], mem_ref.shape[2]
    hd = d // XATTN_HEADS

    @pl.when(pl.program_id(1) == 0)
    def _():
        wkv_bf[...] = wkv_ref[0].astype(BF16)
        wq_bf[...] = wq_ref[0].astype(BF16)
        wo_bf[...] = wo_ref[0].astype(BF16)

    mn = _rms(mem_ref[0], g_ref[...]).astype(BF16)
    kv = _dot(mn, wkv_bf[...])
    k = kv[:, :d].astype(BF16)
    v = kv[:, d:].astype(BF16)
    for a in range(XATTN_HEADS):
        sl = slice(a * hd, (a + 1) * hd)
        qk_ref[0, 0, :, a * nm:(a + 1) * nm] = (_dot_nt(wq_bf[:, sl], k[:, sl]) * (hd ** -0.5)).astype(BF16)
        vo_ref[0, 0, a * nm:(a + 1) * nm, :] = _dot(v[:, sl], wo_bf[sl, :]).astype(BF16)


def _mem_kv(mem, norm_mem, w_kv, w_q, w_o):
    b, nm, d = mem.shape
    depth = w_kv.shape[0]
    per_layer = lambda shape: pl.BlockSpec((1,) + shape, lambda l, i: (l, 0, 0))
    return pl.pallas_call(
        _memkv_kernel,
        out_shape=(jax.ShapeDtypeStruct((depth, b, d, XATTN_HEADS * nm), BF16),
                   jax.ShapeDtypeStruct((depth, b, XATTN_HEADS * nm, d), BF16)),
        grid=(depth, b),
        in_specs=[
            pl.BlockSpec((1, nm, d), lambda l, i: (i, 0, 0)),
            pl.BlockSpec((1, d), lambda l, i: (0, 0)),
            per_layer((d, 2 * d)), per_layer((d, d)), per_layer((d, d)),
        ],
        out_specs=(
            pl.BlockSpec((1, 1, d, XATTN_HEADS * nm), lambda l, i: (l, i, 0, 0)),
            pl.BlockSpec((1, 1, XATTN_HEADS * nm, d), lambda l, i: (l, i, 0, 0)),
        ),
        scratch_shapes=[pltpu.VMEM((d, 2 * d), BF16), pltpu.VMEM((d, d), BF16), pltpu.VMEM((d, d), BF16)],
        compiler_params=_cparams(("arbitrary", "arbitrary")),
        name="mem_kv",
    )(mem, norm_mem.reshape(1, d), w_kv, w_q, w_o)


def _conv_kernel(x_ref, g_ref, win_ref, bin_ref, wdw_ref, bdw_ref, lng_ref, lnb_ref, wout_ref, bout_ref,
                 o_ref, ext_ref, even_ref, odd_ref):
    ts, d = x_ref.shape[1], x_ref.shape[2]
    n_shifts = ext_ref.shape[0]

    @pl.when(pl.program_id(1) == 0)
    def _():
        ext_ref[...] = jnp.zeros(ext_ref.shape, F32)

    x = x_ref[0]
    h = _rms(x, g_ref[...]).astype(BF16)
    u = _dot(h, win_ref[...]) + bin_ref[...]
    glu = u[:, :d] * _sigmoid(u[:, d:])
    for c in range(n_shifts):
        ext_ref[c, CONV_CARRY - 2 * c:CONV_CARRY - 2 * c + ts, :] = glu

    first = CONV_CARRY - (CONV_KERNEL - 1)

    def taps(r0, n_rows, parity, dst_ref):
        for c0 in range(0, d, CONV_COLS):
            cols = slice(c0, c0 + CONV_COLS)
            acc = [jnp.zeros((SUBLANES, CONV_COLS), F32) for _ in range(n_rows // SUBLANES)]
            for k in range(CONV_KERNEL):
                if (first + k) % 2 != parity:
                    continue
                shift = (first + k - parity) % SUBLANES
                base = first + k - parity - shift
                wk = wdw_ref[k * SUBLANES:(k + 1) * SUBLANES, cols]
                for j in range(n_rows // SUBLANES):
                    rows = pl.ds(r0 + base + j * SUBLANES, SUBLANES)
                    acc[j] = acc[j] + ext_ref[shift // 2, rows, cols] * wk
            for j in range(n_rows // SUBLANES):
                dst_ref[pl.ds(r0 + j * SUBLANES, SUBLANES), cols] = acc[j]

    def chunk(i, carry):
        r0 = pl.multiple_of(i * CONV_ROWS, CONV_ROWS)
        taps(r0, CONV_ROWS, 0, even_ref)
        taps(r0, CONV_ROWS, 1, odd_ref)
        return carry

    lax.fori_loop(0, ts // CONV_ROWS, chunk, 0)
    taps(ts, SUBLANES, 1, odd_ref)
    for c in range(n_shifts):
        ext_ref[c, 0:CONV_CARRY, :] = ext_ref[c, ts:ts + CONV_CARRY, :]

    c = even_ref[...] + odd_ref[1:ts + 1, :] + bdw_ref[...]
    mu = jnp.mean(c, axis=-1, keepdims=True)
    cc = c - mu
    var = jnp.mean(cc * cc, axis=-1, keepdims=True)
    un = cc * lax.rsqrt(var + EPS) * lng_ref[...] + lnb_ref[...]
    act = _silu(un).astype(BF16)
    o_ref[0] = x + _dot(act, wout_ref[...]) + bout_ref[...]


def _conv_mixer(x, g, w_in, b_in, w_dw, b_dw, ln_g, ln_b, w_out, b_out):
    b, s, d = x.shape
    ts = min(TILE_CONV, s)
    assert s % ts == 0 and ts % CONV_ROWS == 0 and d % CONV_COLS == 0
    row = lambda v: v.reshape(1, -1)
    const = lambda shape: pl.BlockSpec(shape, lambda i, j: (0,) * len(shape))
    return pl.pallas_call(
        _conv_kernel,
        out_shape=jax.ShapeDtypeStruct(x.shape, F32),
        grid=(b, s // ts),
        in_specs=[
            pl.BlockSpec((1, ts, d), lambda i, j: (i, j, 0)),
            const((1, d)), const((d, 2 * d)), const((1, 2 * d)), const((CONV_KERNEL * SUBLANES, d)), const((1, d)),
            const((1, d)), const((1, d)), const((d, d)), const((1, d)),
        ],
        out_specs=pl.BlockSpec((1, ts, d), lambda i, j: (i, j, 0)),
        scratch_shapes=[pltpu.VMEM((SUBLANES // 2, CONV_CARRY + ts, d), F32), pltpu.VMEM((ts, d), F32),
                        pltpu.VMEM((ts + SUBLANES, d), F32)],
        compiler_params=_cparams(("arbitrary", "arbitrary")),
        name="conv_mixer",
    )(x, row(g), w_in.astype(BF16), row(b_in), jnp.repeat(w_dw, SUBLANES, axis=0), row(b_dw), row(ln_g), row(ln_b),
      w_out.astype(BF16), row(b_out))


def _gla_levels(ts):
    sizes = [GLA_LEAF]
    while sizes[-1] < ts:
        sizes.append(sizes[-1] * 2)
    return sizes


def _moe_residual(x, route, y0, y1):
    g = lax.bitcast_convert_type(route, F32)
    g = jnp.concatenate([g, jnp.zeros((LANES - g.shape[0], g.shape[1]), F32)], axis=0).T
    gate0 = g[:, 2 * TOP_K:2 * TOP_K + 1]
    gate1 = g[:, 2 * TOP_K + 1:2 * TOP_K + 2]
    return x + _unpack_bf16_pairs(y0) * gate0 + _unpack_bf16_pairs(y1) * gate1


def _gla_kernel(x_ref, gate_ref, y0_ref, y1_ref, g_ref, wq_ref, wk_ref, wv_ref, wa_ref, wr_ref, wa2_ref, ba_ref,
                ng_ref, wo_ref, o_ref, state_ref, x_s, q_s, k_s, v_s, og_s, la_s, *, tiles_per_seq):
    j = pl.program_id(0)
    dkh = wq_ref.shape[1] // GLA_HEADS

    @pl.when(j == 0)
    def _():
        for ref in (x_s, q_s, k_s, og_s, la_s):
            ref[...] = jnp.zeros(ref.shape, F32)
        v_s[...] = jnp.zeros(v_s.shape, BF16)

    @pl.when((j == 0) | ((j - 1) % tiles_per_seq == 0))
    def _():
        state_ref[...] = jnp.zeros(state_ref.shape, F32)

    def stage_a(slot):
        x = _moe_residual(x_ref[0], gate_ref[...], y0_ref[...], y1_ref[...])
        h = _rms(x, g_ref[...]).astype(BF16)
        x_s[slot] = x
        q_s[slot] = _dot(h, wq_ref[...]) * (dkh ** -0.5)
        k_s[slot] = _dot(h, wk_ref[...])
        v_s[slot] = _dot(h, wv_ref[...]).astype(BF16)
        r = _dot(h, wr_ref[...])
        og_s[slot] = _silu(r)
        a = _dot(h, wa_ref[...]).astype(BF16)
        z = _dot(jnp.concatenate([a, a], axis=1), wa2_ref[...]) + ba_ref[...]
        la_s[slot] = -(jnp.maximum(-z, 0.0) + jnp.log(1.0 + jnp.exp(-jnp.abs(z)))) * (1.0 / GLA_TAU)

    def stage_b(slot):
        _gla_recurrence(x_s[slot], q_s[slot], k_s[slot], v_s[slot], og_s[slot], la_s[slot],
                        ng_ref, wo_ref, o_ref, state_ref)

    @pl.when(j % 2 == 0)
    def _():
        stage_b(1)
        stage_a(0)

    @pl.when(j % 2 == 1)
    def _():
        stage_b(0)
        stage_a(1)


def _gla_recurrence(x, q, k, v, out_gate, log_a, ng_ref, wo_ref, o_ref, state_ref):
    ts, d = x.shape
    dk = q.shape[1]
    dkh = dk // GLA_HEADS
    dvh = d // GLA_HEADS

    row = lax.broadcasted_iota(I32, (ts, ts), 0)
    col = lax.broadcasted_iota(I32, (ts, ts), 1)
    tri = jnp.where(col <= row, 1.0, 0.0).astype(BF16)
    bcum = _dot(tri, log_a.astype(BF16))
    b_last = bcum[ts - 1:ts, :]

    q_in = (q * jnp.exp(bcum)).astype(BF16)
    k_out = (k * jnp.exp(b_last - bcum)).astype(BF16)

    sizes = _gla_levels(ts)
    scores = [None] * GLA_HEADS
    for lvl, size in enumerate(sizes):
        half = size // 2
        same_block = (row & -size) == (col & -size)
        if lvl == 0:
            pair = same_block & (col <= row)
            q_ok = k_ok = None
        else:
            pair = same_block & ((row & (size - 1)) >= half) & ((col & (size - 1)) < half)
            pos = lax.broadcasted_iota(I32, (ts, dk), 0) & (size - 1)
            q_ok = pos >= half
            k_ok = pos < half
        ref = jnp.concatenate(
            [jnp.broadcast_to(bcum[r0 + half:r0 + half + 1, :], (size, dk)) for r0 in range(0, ts, size)], axis=0)
        ql = q * jnp.exp(bcum - ref)
        kl = k * jnp.exp(ref - bcum)
        if q_ok is not None:
            ql = jnp.where(q_ok, ql, 0.0)
            kl = jnp.where(k_ok, kl, 0.0)
        ql = ql.astype(BF16)
        kl = kl.astype(BF16)
        for hd in range(GLA_HEADS):
            c0 = hd * dkh
            a = _dot_nt(ql[:, c0:c0 + dkh], kl[:, c0:c0 + dkh])
            a = jnp.where(pair, a, 0.0)
            scores[hd] = a if scores[hd] is None else scores[hd] + a

    outs = []
    for hd in range(GLA_HEADS):
        c0 = hd * dkh
        v_h = v[:, hd * dvh:(hd + 1) * dvh]
        st = state_ref[hd]
        o_h = _dot(scores[hd].astype(BF16), v_h) + _dot_nt(q_in[:, c0:c0 + dkh], st.astype(BF16))
        decay = jnp.exp(b_last[:, c0:c0 + dkh])
        state_ref[hd] = st * decay + _dot_tn(v_h, k_out[:, c0:c0 + dkh])
        o_h = o_h * lax.rsqrt(jnp.mean(o_h * o_h, axis=-1, keepdims=True) + EPS) * ng_ref[...]
        outs.append(o_h)
    o = jnp.concatenate(outs, axis=1) * out_gate
    o_ref[0] = x + _dot(o.astype(BF16), wo_ref[...])


def _gla_mixer(x, gates, y0, y1, g, w_in, w_a2, b_a, norm_g, w_o):
    b, s, d = x.shape
    dk = w_a2.shape[1]
    ts = min(TILE_GLA, s)
    assert s % ts == 0 and ts % GLA_LEAF == 0
    nj = s // ts
    row = lambda v: v.reshape(1, -1)
    const = lambda shape: pl.BlockSpec(shape, lambda j: (0,) * len(shape))
    wq = w_in[:, :dk].astype(BF16)
    wk = w_in[:, dk:2 * dk].astype(BF16)
    wv = w_in[:, 2 * dk:2 * dk + d].astype(BF16)
    wa = jnp.pad(w_in[:, 2 * dk + d:2 * dk + d + GLA_RANK], ((0, 0), (0, GLA_RANK_PAD - GLA_RANK))).astype(BF16)
    wr = w_in[:, 2 * dk + d + GLA_RANK:].astype(BF16)
    wa2 = jnp.pad(w_a2, ((0, GLA_RANK_PAD - GLA_RANK), (0, 0)))
    wa2_hi = wa2.astype(BF16)
    wa2_lo = (wa2 - wa2_hi.astype(F32)).astype(BF16)
    wa2_split = jnp.concatenate([wa2_hi, wa2_lo], axis=0)
    dvh = d // GLA_HEADS
    n = b * nj

    def tile(j, lag):
        return jnp.clip(j - lag, 0, n - 1)

    tokens = lambda lag: (lambda j: (tile(j, lag), 0))
    return pl.pallas_call(
        functools.partial(_gla_kernel, tiles_per_seq=nj),
        out_shape=jax.ShapeDtypeStruct(x.shape, F32),
        grid=(n + 1,),
        in_specs=[
            pl.BlockSpec((1, ts, d), lambda j: (tile(j, 0) // nj, tile(j, 0) % nj, 0)),
            pl.BlockSpec((ROUTE_ROWS, ts), lambda j: (0, tile(j, 0))),
            pl.BlockSpec((ts, d // 2), tokens(0)),
            pl.BlockSpec((ts, d // 2), tokens(0)),
            const((1, d)), const((d, dk)), const((d, dk)), const((d, d)), const((d, GLA_RANK_PAD)),
            const((d, d)), const((2 * GLA_RANK_PAD, dk)), const((1, dk)), const((1, dvh)), const((d, d)),
        ],
        out_specs=pl.BlockSpec((1, ts, d), lambda j: (tile(j, 1) // nj, tile(j, 1) % nj, 0)),
        scratch_shapes=[pltpu.VMEM((GLA_HEADS, dvh, dk // GLA_HEADS), F32),
                        pltpu.VMEM((2, ts, d), F32), pltpu.VMEM((2, ts, dk), F32), pltpu.VMEM((2, ts, dk), F32),
                        pltpu.VMEM((2, ts, d), BF16), pltpu.VMEM((2, ts, d), F32), pltpu.VMEM((2, ts, dk), F32)],
        compiler_params=_cparams(("arbitrary",)),
        name="gla_mixer",
    )(x, gates, y0, y1, row(g), wq, wk, wv, wa, wr, wa2_split, row(b_a), row(norm_g), w_o.astype(BF16))


def _xattn_router_kernel(x_ref, gx_ref, qk_ref, vo_ref, gf_ref, wr_ref, br_ref, upper_ref,
                         x_out_ref, h_out_ref, route_ref, cnt_ref, carry_ref):
    ts, d = x_ref.shape[1], x_ref.shape[2]
    nm = qk_ref.shape[3] // XATTN_HEADS
    first = (pl.program_id(0) == 0) & (pl.program_id(1) == 0)

    @pl.when(first)
    def _():
        carry_ref[...] = jnp.zeros(carry_ref.shape, F32)

    x = x_ref[0]
    h = _rms(x, gx_ref[...]).astype(BF16)
    scores = _dot(h, qk_ref[0, 0])
    probs = []
    for a in range(XATTN_HEADS):
        s = scores[:, a * nm:(a + 1) * nm]
        p = jnp.exp(s - jnp.max(s, axis=-1, keepdims=True))
        probs.append((p / jnp.sum(p, axis=-1, keepdims=True)).astype(BF16))
    x2 = x + _dot(jnp.concatenate(probs, axis=1), vo_ref[0, 0])
    x_out_ref[0] = x2

    hf = _rms(x2, gf_ref[...])
    h_out_ref[...] = _pack_bf16_pairs(hf)

    both = _dot_nt(wr_ref[...], hf.astype(BF16))
    logits = both[0:ROUTER_ROWS, :] + both[ROUTER_ROWS:2 * ROUTER_ROWS, :] + br_ref[...]
    gl = logits[N_EXPERTS:N_EXPERTS + N_GROUPS, :]
    gi = lax.broadcasted_iota(I32, gl.shape, 0).astype(F32)
    gmax = jnp.max(gl, axis=0, keepdims=True)
    g_sel = jnp.min(jnp.where(gl == gmax, gi, float(N_GROUPS)), axis=0, keepdims=True)
    pg_sel = 1.0 / jnp.sum(jnp.exp(gl - gmax), axis=0, keepdims=True)

    el = jnp.zeros((EXPERTS_PER_GROUP, ts), F32)
    for gidx in range(N_GROUPS):
        lo = gidx * EXPERTS_PER_GROUP
        el = jnp.where(g_sel == float(gidx), logits[lo:lo + EXPERTS_PER_GROUP, :], el)
    ei = lax.broadcasted_iota(I32, el.shape, 0).astype(F32)
    m1 = jnp.max(el, axis=0, keepdims=True)
    i1 = jnp.min(jnp.where(el == m1, ei, float(EXPERTS_PER_GROUP)), axis=0, keepdims=True)
    rest = jnp.where(ei == i1, -jnp.inf, el)
    m2 = jnp.max(rest, axis=0, keepdims=True)
    i2 = jnp.min(jnp.where(rest == m2, ei, float(EXPERTS_PER_GROUP)), axis=0, keepdims=True)
    ratio = jnp.exp(m2 - m1)
    gate1 = pg_sel / (1.0 + ratio)
    gate2 = pg_sel * ratio / (1.0 + ratio)
    e1 = g_sel * float(EXPERTS_PER_GROUP) + i1
    e2 = g_sel * float(EXPERTS_PER_GROUP) + i2

    xi = lax.broadcasted_iota(I32, (N_EXPERTS, ts), 0).astype(F32)
    oh1 = jnp.where(xi == e1, 1.0, 0.0)
    oh2 = jnp.where(xi == e2, 1.0, 0.0)
    oh = oh1 + oh2
    n_blk = ts // LANES
    stacked = jnp.concatenate([oh[:, c * LANES:(c + 1) * LANES] for c in range(n_blk)], axis=0)
    within = _dot(stacked.astype(BF16), upper_ref[...])
    totals = jnp.sum(stacked, axis=1, keepdims=True)
    run = carry_ref[...]
    before = []
    for c in range(n_blk):
        before.append(within[c * N_EXPERTS:(c + 1) * N_EXPERTS, :] + run)
        run = run + totals[c * N_EXPERTS:(c + 1) * N_EXPERTS, :]
    before = jnp.concatenate(before, axis=1)
    rank1 = jnp.sum(oh1 * before, axis=0, keepdims=True)
    rank2 = jnp.sum(oh2 * before, axis=0, keepdims=True)
    carry_ref[...] = run
    cnt_ref[...] = jnp.broadcast_to(run, cnt_ref.shape)

    route_ref[0:1, :] = e1.astype(I32)
    route_ref[1:2, :] = e2.astype(I32)
    route_ref[2:3, :] = rank1.astype(I32)
    route_ref[3:4, :] = rank2.astype(I32)
    route_ref[4:5, :] = lax.bitcast_convert_type(gate1, I32)
    route_ref[5:6, :] = lax.bitcast_convert_type(gate2, I32)
    route_ref[3 * TOP_K:ROUTE_ROWS, :] = jnp.zeros((ROUTE_ROWS - 3 * TOP_K, ts), I32)


def _xattn_router(x, g_x, qk_mem, vo_mem, layer, g_f, w_grp, b_grp, w_exp, b_exp):
    b, s, d = x.shape
    hm = qk_mem.shape[3]
    t = b * s
    ts = min(TILE_XATTN, s)
    assert s % ts == 0 and ts % LANES == 0
    nj = s // ts
    row = lambda v: v.reshape(1, -1)
    const = lambda shape: pl.BlockSpec(shape, lambda i, j: (0,) * len(shape))
    pad = ROUTER_ROWS - N_GROUPS - N_EXPERTS
    w_r = jnp.pad(jnp.concatenate([w_exp, w_grp], axis=1).T, ((0, pad), (0, 0)))
    w_r_hi = w_r.astype(BF16)
    w_r_split = jnp.concatenate([w_r_hi, (w_r - w_r_hi.astype(F32)).astype(BF16)], axis=0)
    b_r = jnp.pad(jnp.concatenate([b_exp, b_grp]), (0, pad)).reshape(ROUTER_ROWS, 1)
    ti = jnp.arange(LANES)
    upper = (ti[:, None] < ti[None, :]).astype(BF16)
    return pl.pallas_call(
        _xattn_router_kernel,
        out_shape=(
            jax.ShapeDtypeStruct(x.shape, F32),
            jax.ShapeDtypeStruct((t, d // 2), U32),
            jax.ShapeDtypeStruct((ROUTE_ROWS, t), I32),
            jax.ShapeDtypeStruct((N_EXPERTS, LANES), F32),
        ),
        grid=(b, nj),
        in_specs=[
            pl.BlockSpec((1, ts, d), lambda i, j: (i, j, 0)),
            const((1, d)),
            pl.BlockSpec((1, 1, d, hm), lambda i, j: (layer, i, 0, 0)),
            pl.BlockSpec((1, 1, hm, d), lambda i, j: (layer, i, 0, 0)),
            const((1, d)), const((2 * ROUTER_ROWS, d)), const((ROUTER_ROWS, 1)), const((LANES, LANES)),
        ],
        out_specs=(
            pl.BlockSpec((1, ts, d), lambda i, j: (i, j, 0)),
            pl.BlockSpec((ts, d // 2), lambda i, j: (i * nj + j, 0)),
            pl.BlockSpec((ROUTE_ROWS, ts), lambda i, j: (0, i * nj + j)),
            pl.BlockSpec((N_EXPERTS, LANES), lambda i, j: (0, 0)),
        ),
        scratch_shapes=[pltpu.VMEM((N_EXPERTS, 1), F32)],
        compiler_params=_cparams(("arbitrary", "arbitrary")),
        name="xattn_router",
    )(x, row(g_x), qk_mem, vo_mem, row(g_f), w_r_split, b_r, upper)


def _sc_mesh():
    return plsc.VectorSubcoreMesh(core_axis_name="c", subcore_axis_name="s",
                                  num_cores=SC_CORES, num_subcores=SC_SUBCORES)


def _sc_worker():
    return lax.axis_index("s") * SC_CORES + lax.axis_index("c")


def _dispatch(h_packed, dest, n_rows):
    t, w = h_packed.shape
    chunk = SC_DISPATCH_CHUNK
    per_worker = t // SC_WORKERS
    n_chunks = per_worker // chunk
    assert n_chunks % 2 == 0 and n_chunks * chunk * SC_WORKERS == t
    dest = dest.reshape(TOP_K, SC_WORKERS * n_chunks, chunk)
    rows_buf = pltpu.VMEM((chunk, w), U32)

    @functools.partial(
        pl.kernel, mesh=_sc_mesh(),
        out_type=jax.ShapeDtypeStruct((n_rows, w), U32),
        scratch_types=[pltpu.VMEM((n_chunks, chunk), I32), pltpu.VMEM((n_chunks, chunk), I32), rows_buf, rows_buf,
                       pltpu.SemaphoreType.DMA((2,)), pltpu.SemaphoreType.DMA((2, TOP_K))],
        name="moe_dispatch_sc",
    )
    def run(h_hbm, d0_hbm, d1_hbm, xbuf_hbm, idx0_v, idx1_v, buf_a, buf_b, read_sem, write_sem):
        wid = _sc_worker()
        pltpu.sync_copy(d0_hbm.at[pl.ds(wid * n_chunks, n_chunks)], idx0_v)
        pltpu.sync_copy(d1_hbm.at[pl.ds(wid * n_chunks, n_chunks)], idx1_v)

        @pl.loop(0, n_chunks, step=2)
        def _(i):
            reads = [pltpu.async_copy(h_hbm.at[pl.ds(wid * per_worker + (i + j) * chunk, chunk)], buf, read_sem.at[j])
                     for j, buf in enumerate((buf_a, buf_b))]
            writes = []
            for j, buf in enumerate((buf_a, buf_b)):
                reads[j].wait()
                writes.append(pltpu.async_copy(buf, xbuf_hbm.at[idx0_v.at[i + j]], write_sem.at[j, 0]))
                writes.append(pltpu.async_copy(buf, xbuf_hbm.at[idx1_v.at[i + j]], write_sem.at[j, 1]))
            for copy in writes:
                copy.wait()

    return run(h_packed, dest[0], dest[1])


def _gather_pairs(y_buf, dest):
    t = dest.shape[1]
    w = y_buf.shape[1]
    chunk = SC_GATHER_CHUNK
    per_worker = t // SC_WORKERS
    n_chunks = per_worker // chunk
    assert n_chunks % 2 == 0 and n_chunks * chunk * SC_WORKERS == t
    dest = dest.reshape(TOP_K, SC_WORKERS * n_chunks, chunk)
    out = jax.ShapeDtypeStruct((t, w), U32)
    rows_buf = pltpu.VMEM((chunk, w), U32)

    @functools.partial(
        pl.kernel, mesh=_sc_mesh(),
        out_type=(out, out),
        scratch_types=[pltpu.VMEM((n_chunks, chunk), I32), pltpu.VMEM((n_chunks, chunk), I32),
                       rows_buf, rows_buf, rows_buf, rows_buf, pltpu.SemaphoreType.DMA((2, TOP_K))],
        name="moe_gather_sc",
    )
    def run(y_hbm, d0_hbm, d1_hbm, y0_hbm, y1_hbm, idx0_v, idx1_v, buf_a0, buf_a1, buf_b0, buf_b1, sem):
        wid = _sc_worker()
        pltpu.sync_copy(d0_hbm.at[pl.ds(wid * n_chunks, n_chunks)], idx0_v)
        pltpu.sync_copy(d1_hbm.at[pl.ds(wid * n_chunks, n_chunks)], idx1_v)

        @pl.loop(0, n_chunks, step=2)
        def _(i):
            bufs = ((buf_a0, buf_a1), (buf_b0, buf_b1))
            gathers = [[pltpu.async_copy(y_hbm.at[idx_v.at[i + j]], bufs[j][k], sem.at[j, k])
                        for k, idx_v in enumerate((idx0_v, idx1_v))] for j in range(2)]
            stores = []
            for j in range(2):
                rows = pl.ds(wid * per_worker + (i + j) * chunk, chunk)
                for k, out_hbm in enumerate((y0_hbm, y1_hbm)):
                    gathers[j][k].wait()
                    stores.append(pltpu.async_copy(bufs[j][k], out_hbm.at[rows], sem.at[j, k]))
            for copy in stores:
                copy.wait()

    return run(y_buf, dest[0], dest[1])


def _expert_kernel(be_ref, bf_ref, bv_ref, slot_ref, next_ref, x_ref, wg_hbm, wu_hbm, wd_hbm, y_ref,
                   wg_f32, wu_f32, wd_f32, wg_bf, wu_bf, wd_bf, sems, *, layer):
    blk = pl.program_id(0)
    valid = bv_ref[blk]

    def weight_copies(expert, slot):
        return [pltpu.make_async_copy(src.at[layer, expert], dst.at[slot], sems.at[slot, i])
                for i, (src, dst) in enumerate(((wg_hbm, wg_f32), (wu_hbm, wu_f32), (wd_hbm, wd_f32)))]

    @pl.when(blk == 0)
    def _():
        for copy in weight_copies(be_ref[0], 0):
            copy.start()

    @pl.when(bf_ref[blk] == 1)
    def _():
        slot = slot_ref[blk]
        for copy in weight_copies(be_ref[blk], slot):
            copy.wait()

        @pl.when(next_ref[blk] >= 0)
        def _():
            for copy in weight_copies(next_ref[blk], 1 - slot):
                copy.start()

        wg_bf[...] = wg_f32[slot].astype(BF16)
        wu_bf[...] = wu_f32[slot].astype(BF16)
        wd_bf[...] = wd_f32[slot].astype(BF16)

    @pl.when(valid > 0)
    def _():
        live = lax.broadcasted_iota(I32, x_ref.shape, 0) < valid
        xb = _unpack_bf16_pairs(jnp.where(live, x_ref[...], jnp.uint32(0))).astype(BF16)
        gt = _dot(xb, wg_bf[...])
        up = _dot(xb, wu_bf[...])
        act = (_silu(gt) * up).astype(BF16)
        y_ref[...] = _pack_bf16_pairs(_dot(act, wd_bf[...]))

    @pl.when(valid <= 0)
    def _():
        y_ref[...] = jnp.zeros(y_ref.shape, U32)


def _experts(x_buf, block_expert, block_first, block_valid, w_gate, w_up, w_down, layer):
    n_rows, w = x_buf.shape
    d, de = w_gate.shape[2], w_gate.shape[3]
    bm = MOE_BLOCK_ROWS
    n = n_rows // bm
    block_slot = (jnp.cumsum(block_first) - 1) % 2
    idx = jnp.arange(n, dtype=I32)
    later_first = jnp.concatenate([jnp.where(block_first[1:] == 1, idx[1:], n), jnp.full((1,), n, I32)])
    next_first = lax.cummin(later_first, reverse=True)
    block_next = jnp.where(next_first < n, block_expert[jnp.minimum(next_first, n - 1)], -1).astype(I32)
    any_space = pl.BlockSpec(memory_space=pl.ANY)
    grid_spec = pltpu.PrefetchScalarGridSpec(
        num_scalar_prefetch=5,
        grid=(n,),
        in_specs=[pl.BlockSpec((bm, w), lambda i, *_: (i, 0)), any_space, any_space, any_space],
        out_specs=pl.BlockSpec((bm, w), lambda i, *_: (i, 0)),
        scratch_shapes=[pltpu.VMEM((2, d, de), F32), pltpu.VMEM((2, d, de), F32), pltpu.VMEM((2, de, d), F32),
                        pltpu.VMEM((d, de), BF16), pltpu.VMEM((d, de), BF16), pltpu.VMEM((de, d), BF16),
                        pltpu.SemaphoreType.DMA((2, 3))],
    )
    return pl.pallas_call(
        functools.partial(_expert_kernel, layer=layer),
        out_shape=jax.ShapeDtypeStruct((n_rows, w), U32),
        grid_spec=grid_spec,
        compiler_params=_cparams(("arbitrary",)),
        name="moe_experts",
    )(block_expert, block_first, block_valid, block_slot.astype(I32), block_next, x_buf, w_gate, w_up, w_down)


def _combine_kernel(x_ref, gate_ref, y0_ref, y1_ref, gfin_ref, o_ref, *, final_norm):
    out = _moe_residual(x_ref[...], gate_ref[...], y0_ref[...], y1_ref[...])
    if final_norm:
        out = _rms(out, gfin_ref[...])
    o_ref[...] = out


def _combine(x2, y0, y1, gates, g_final, final_norm):
    t, d = x2.shape
    w = y0.shape[1]
    ts = min(TILE_COMBINE, t)
    assert t % ts == 0
    return pl.pallas_call(
        functools.partial(_combine_kernel, final_norm=final_norm),
        out_shape=jax.ShapeDtypeStruct((t, d), F32),
        grid=(t // ts,),
        in_specs=[
            pl.BlockSpec((ts, d), lambda i: (i, 0)),
            pl.BlockSpec((ROUTE_ROWS, ts), lambda i: (0, i)),
            pl.BlockSpec((ts, w), lambda i: (i, 0)),
            pl.BlockSpec((ts, w), lambda i: (i, 0)),
            pl.BlockSpec((1, d), lambda i: (0, 0)),
        ],
        out_specs=pl.BlockSpec((ts, d), lambda i: (i, 0)),
        compiler_params=_cparams(("arbitrary",)),
        name="moe_combine",
    )(x2, gates, y0, y1, g_final.reshape(1, d))


def _moe_layout(route, counts):
    bm = MOE_BLOCK_ROWS
    t = route.shape[1]
    assert (t * TOP_K) % bm == 0
    n_blocks = (t * TOP_K) // bm + N_EXPERTS
    cnt = counts[:, 0].astype(I32)
    padded = (cnt + bm - 1) // bm * bm
    pad_ends = jnp.cumsum(padded)
    pad_off = pad_ends - padded
    experts = jnp.arange(N_EXPERTS, dtype=I32)
    hit = route[0:TOP_K, :, None] == experts
    dest = jnp.sum(jnp.where(hit, pad_off, 0), axis=-1) + route[TOP_K:2 * TOP_K]
    gates = route
    starts = jnp.arange(n_blocks, dtype=I32) * bm
    block_expert = jnp.minimum(jnp.sum((pad_ends[None, :] <= starts[:, None]).astype(I32), axis=1),
                               N_EXPERTS - 1)
    block_first = jnp.concatenate([jnp.ones((1,), I32), (block_expert[1:] != block_expert[:-1]).astype(I32)])
    own = block_expert[:, None] == experts
    block_valid = jnp.clip(jnp.sum(jnp.where(own, cnt + pad_off, 0), axis=1) - starts, 0, bm)
    block_valid = jnp.where(starts < pad_ends[-1], block_valid, 0).astype(I32)
    return dest, gates, block_expert, block_first, block_valid, n_blocks * bm


def kernel(x, mem, norm_mix, norm_xattn, norm_ffn, norm_mem, norm_final, conv_w_in, conv_b_in, conv_w_dw,
           conv_b_dw, conv_ln_g, conv_ln_b, conv_w_out, conv_b_out, gla_w_in, gla_w_a2, gla_b_a, gla_norm_g,
           gla_w_o, xa_w_q, xa_w_kv, xa_w_o, moe_w_grp, moe_b_grp, moe_w_exp, moe_b_exp, moe_w_gate, moe_w_up,
           moe_w_down):
    b, s, d = x.shape
    depth = norm_mix.shape[0]
    qk_mem, vo_mem = _mem_kv(mem, norm_mem, xa_w_kv, xa_w_q, xa_w_o)
    moe = None
    for i in range(depth):
        j = i // 2
        if i % 2 == 0:
            if moe is not None:
                x = _combine(x.reshape(b * s, d), moe[1], moe[2], moe[0], norm_final, False).reshape(b, s, d)
            x = _conv_mixer(x, norm_mix[i], conv_w_in[j], conv_b_in[j], conv_w_dw[j], conv_b_dw[j],
                            conv_ln_g[j], conv_ln_b[j], conv_w_out[j], conv_b_out[j])
        else:
            x = _gla_mixer(x, *moe, norm_mix[i], gla_w_in[j], gla_w_a2[j], gla_b_a[j], gla_norm_g[j], gla_w_o[j])
        x2, h_packed, route, counts = _xattn_router(
            x, norm_xattn[i], qk_mem, vo_mem, i, norm_ffn[i],
            moe_w_grp[i], moe_b_grp[i], moe_w_exp[i], moe_b_exp[i])
        dest, gates, block_expert, block_first, block_valid, n_rows = _moe_layout(route, counts)
        x_buf = _dispatch(h_packed, dest, n_rows)
        y_buf = _experts(x_buf, block_expert, block_first, block_valid, moe_w_gate, moe_w_up, moe_w_down, i)
        y0, y1 = _gather_pairs(y_buf, dest)
        x, moe = x2, (gates, y0, y1)
    return _combine(x.reshape(b * s, d), moe[1], moe[2], moe[0], norm_final, True).reshape(b, s, d)
```

```python
import functools

import jax
import jax.numpy as jnp
from jax import lax
from jax.experimental import pallas as pl
from jax.experimental.pallas import tpu as pltpu
from jax.experimental.pallas import tpu_sc as plsc

F32 = jnp.float32
BF16 = jnp.bfloat16
I32 = jnp.int32
U32 = jnp.uint32

EPS = 1e-6
CONV_KERNEL = 31
CONV_CARRY = 32
CONV_ROWS = 64
CONV_COLS = 256
SUBLANES = 8
LANES = 128
GLA_HEADS = 4
GLA_RANK = 16
GLA_RANK_PAD = 128
GLA_TAU = 16.0
GLA_LEAF = 32
XATTN_HEADS = 4
N_GROUPS = 4
EXPERTS_PER_GROUP = 8
N_EXPERTS = N_GROUPS * EXPERTS_PER_GROUP
ROUTER_ROWS = 40
TOP_K = 2
ROUTE_ROWS = 8

TILE_CONV = 512
TILE_GLA = 256
TILE_XATTN = 1024
TILE_COMBINE = 512
SC_CORES = 2
SC_SUBCORES = 16
SC_WORKERS = SC_CORES * SC_SUBCORES
SC_DISPATCH_CHUNK = 64
SC_GATHER_CHUNK = 32
MOE_BLOCK_ROWS = 512
VMEM_LIMIT = 56 * 1024 * 1024


def _cparams(sem):
    return pltpu.CompilerParams(dimension_semantics=sem, vmem_limit_bytes=VMEM_LIMIT)


def _rms(x, g):
    return x * lax.rsqrt(jnp.mean(x * x, axis=-1, keepdims=True) + EPS) * g


def _sigmoid(x):
    return 0.5 * jnp.tanh(0.5 * x) + 0.5


def _silu(x):
    h = 0.5 * x
    return h + h * jnp.tanh(h)


def _dot(a, b):
    return jnp.dot(a, b, preferred_element_type=F32)


def _dot_nt(a, b):
    return lax.dot_general(a, b, (((1,), (1,)), ((), ())), preferred_element_type=F32)


def _dot_tn(a, b):
    return lax.dot_general(a, b, (((0,), (0,)), ((), ())), preferred_element_type=F32)


def _pack_bf16_pairs(x):
    w = x.shape[1] // 2
    hi = lax.bitcast_convert_type(x[:, :w].astype(BF16).astype(F32), U32)
    lo = lax.bitcast_convert_type(x[:, w:].astype(BF16).astype(F32), U32)
    return hi | (lo >> 16)


def _unpack_bf16_pairs(p):
    hi = lax.bitcast_convert_type(p & jnp.uint32(0xFFFF0000), F32)
    lo = lax.bitcast_convert_type(p << 16, F32)
    return jnp.concatenate([hi, lo], axis=1)


def _memkv_kernel(mem_ref, g_ref, wkv_ref, wq_ref, wo_ref, qk_ref, vo_ref, wkv_bf, wq_bf, wo_bf):
    nm, d = mem_ref.shape[1], mem_ref.shape[2]
    hd = d // XATTN_HEADS

    @pl.when(pl.program_id(1) == 0)
    def _():
        wkv_bf[...] = wkv_ref[0].astype(BF16)
        wq_bf[...] = wq_ref[0].astype(BF16)
        wo_bf[...] = wo_ref[0].astype(BF16)

    mn = _rms(mem_ref[0], g_ref[...]).astype(BF16)
    kv = _dot(mn, wkv_bf[...])
    k = kv[:, :d].astype(BF16)
    v = kv[:, d:].astype(BF16)
    for a in range(XATTN_HEADS):
        sl = slice(a * hd, (a + 1) * hd)
        qk_ref[0, 0, :, a * nm:(a + 1) * nm] = (_dot_nt(wq_bf[:, sl], k[:, sl]) * (hd ** -0.5)).astype(BF16)
        vo_ref[0, 0, a * nm:(a + 1) * nm, :] = _dot(v[:, sl], wo_bf[sl, :]).astype(BF16)


def _mem_kv(mem, norm_mem, w_kv, w_q, w_o):
    b, nm, d = mem.shape
    depth = w_kv.shape[0]
    per_layer = lambda shape: pl.BlockSpec((1,) + shape, lambda l, i: (l, 0, 0))
    return pl.pallas_call(
        _memkv_kernel,
        out_shape=(jax.ShapeDtypeStruct((depth, b, d, XATTN_HEADS * nm), BF16),
                   jax.ShapeDtypeStruct((depth, b, XATTN_HEADS * nm, d), BF16)),
        grid=(depth, b),
        in_specs=[
            pl.BlockSpec((1, nm, d), lambda l, i: (i, 0, 0)),
            pl.BlockSpec((1, d), lambda l, i: (0, 0)),
            per_layer((d, 2 * d)), per_layer((d, d)), per_layer((d, d)),
        ],
        out_specs=(
            pl.BlockSpec((1, 1, d, XATTN_HEADS * nm), lambda l, i: (l, i, 0, 0)),
            pl.BlockSpec((1, 1, XATTN_HEADS * nm, d), lambda l, i: (l, i, 0, 0)),
        ),
        scratch_shapes=[pltpu.VMEM((d, 2 * d), BF16), pltpu.VMEM((d, d), BF16), pltpu.VMEM((d, d), BF16)],
        compiler_params=_cparams(("arbitrary", "arbitrary")),
        name="mem_kv",
    )(mem, norm_mem.reshape(1, d), w_kv, w_q, w_o)


def _conv_kernel(x_ref, g_ref, win_ref, bin_ref, wdw_ref, bdw_ref, lng_ref, lnb_ref, wout_ref, bout_ref,
                 o_ref, ext_ref, even_ref, odd_ref):
    ts, d = x_ref.shape[1], x_ref.shape[2]
    n_shifts = ext_ref.shape[0]

    @pl.when(pl.program_id(1) == 0)
    def _():
        ext_ref[...] = jnp.zeros(ext_ref.shape, F32)

    x = x_ref[0]
    h = _rms(x, g_ref[...]).astype(BF16)
    u = _dot(h, win_ref[...]) + bin_ref[...]
    glu = u[:, :d] * _sigmoid(u[:, d:])
    for c in range(n_shifts):
        ext_ref[c, CONV_CARRY - 2 * c:CONV_CARRY - 2 * c + ts, :] = glu

    first = CONV_CARRY - (CONV_KERNEL - 1)

    def taps(r0, n_rows, parity, dst_ref):
        for c0 in range(0, d, CONV_COLS):
            cols = slice(c0, c0 + CONV_COLS)
            acc = [jnp.zeros((SUBLANES, CONV_COLS), F32) for _ in range(n_rows // SUBLANES)]
            for k in range(CONV_KERNEL):
                if (first + k) % 2 != parity:
                    continue
                shift = (first + k - parity) % SUBLANES
                base = first + k - parity - shift
                wk = wdw_ref[k * SUBLANES:(k + 1) * SUBLANES, cols]
                for j in range(n_rows // SUBLANES):
                    rows = pl.ds(r0 + base + j * SUBLANES, SUBLANES)
                    acc[j] = acc[j] + ext_ref[shift // 2, rows, cols] * wk
            for j in range(n_rows // SUBLANES):
                dst_ref[pl.ds(r0 + j * SUBLANES, SUBLANES), cols] = acc[j]

    def chunk(i, carry):
        r0 = pl.multiple_of(i * CONV_ROWS, CONV_ROWS)
        taps(r0, CONV_ROWS, 0, even_ref)
        taps(r0, CONV_ROWS, 1, odd_ref)
        return carry

    lax.fori_loop(0, ts // CONV_ROWS, chunk, 0)
    taps(ts, SUBLANES, 1, odd_ref)
    for c in range(n_shifts):
        ext_ref[c, 0:CONV_CARRY, :] = ext_ref[c, ts:ts + CONV_CARRY, :]

    c = even_ref[...] + odd_ref[1:ts + 1, :] + bdw_ref[...]
    mu = jnp.mean(c, axis=-1, keepdims=True)
    cc = c - mu
    var = jnp.mean(cc * cc, axis=-1, keepdims=True)
    un = cc * lax.rsqrt(var + EPS) * lng_ref[...] + lnb_ref[...]
    act = _silu(un).astype(BF16)
    o_ref[0] = x + _dot(act, wout_ref[...]) + bout_ref[...]


def _conv_mixer(x, g, w_in, b_in, w_dw, b_dw, ln_g, ln_b, w_out, b_out):
    b, s, d = x.shape
    ts = min(TILE_CONV, s)
    assert s % ts == 0 and ts % CONV_ROWS == 0 and d % CONV_COLS == 0
    row = lambda v: v.reshape(1, -1)
    const = lambda shape: pl.BlockSpec(shape, lambda i, j: (0,) * len(shape))
    return pl.pallas_call(
        _conv_kernel,
        out_shape=jax.ShapeDtypeStruct(x.shape, F32),
        grid=(b, s // ts),
        in_specs=[
            pl.BlockSpec((1, ts, d), lambda i, j: (i, j, 0)),
            const((1, d)), const((d, 2 * d)), const((1, 2 * d)), const((CONV_KERNEL * SUBLANES, d)), const((1, d)),
            const((1, d)), const((1, d)), const((d, d)), const((1, d)),
        ],
        out_specs=pl.BlockSpec((1, ts, d), lambda i, j: (i, j, 0)),
        scratch_shapes=[pltpu.VMEM((SUBLANES // 2, CONV_CARRY + ts, d), F32), pltpu.VMEM((ts, d), F32),
                        pltpu.VMEM((ts + SUBLANES, d), F32)],
        compiler_params=_cparams(("arbitrary", "arbitrary")),
        name="conv_mixer",
    )(x, row(g), w_in.astype(BF16), row(b_in), jnp.repeat(w_dw, SUBLANES, axis=0), row(b_dw), row(ln_g), row(ln_b),
      w_out.astype(BF16), row(b_out))


def _gla_levels(ts):
    sizes = [GLA_LEAF]
    while sizes[-1] < ts:
        sizes.append(sizes[-1] * 2)
    return sizes


def _moe_residual(x, route, y0, y1):
    g = lax.bitcast_convert_type(route, F32)
    g = jnp.concatenate([g, jnp.zeros((LANES - g.shape[0], g.shape[1]), F32)], axis=0).T
    gate0 = g[:, 2 * TOP_K:2 * TOP_K + 1]
    gate1 = g[:, 2 * TOP_K + 1:2 * TOP_K + 2]
    return x + _unpack_bf16_pairs(y0) * gate0 + _unpack_bf16_pairs(y1) * gate1


def _gla_kernel(x_ref, gate_ref, y0_ref, y1_ref, g_ref, win_ref, wa2_ref, ba_ref, ng_ref, wout_ref, o_ref,
                state_ref, x_s, q_s, k_s, v_s, og_s, la_s, wq_ref, wk_ref, wv_ref, wa_ref, wr_ref, wo_ref,
                *, tiles_per_seq):
    j = pl.program_id(0)
    dk, d = wq_ref.shape[1], wv_ref.shape[1]
    dkh = dk // GLA_HEADS

    @pl.when(j == 0)
    def _():
        for ref in (x_s, q_s, k_s, og_s, la_s):
            ref[...] = jnp.zeros(ref.shape, F32)
        v_s[...] = jnp.zeros(v_s.shape, BF16)
        wq_ref[...] = win_ref[:, 0:dk].astype(BF16)
        wk_ref[...] = win_ref[:, dk:2 * dk].astype(BF16)
        wv_ref[...] = win_ref[:, 2 * dk:2 * dk + d].astype(BF16)
        wa_ref[...] = win_ref[:, 2 * dk + d:2 * dk + d + GLA_RANK_PAD].astype(BF16)
        wr_ref[...] = win_ref[:, 2 * dk + d + GLA_RANK:2 * dk + 2 * d + GLA_RANK].astype(BF16)
        wo_ref[...] = wout_ref[...].astype(BF16)

    @pl.when((j == 0) | ((j - 1) % tiles_per_seq == 0))
    def _():
        state_ref[...] = jnp.zeros(state_ref.shape, F32)

    def stage_a(slot):
        x = _moe_residual(x_ref[0], gate_ref[...], y0_ref[...], y1_ref[...])
        h = _rms(x, g_ref[...]).astype(BF16)
        x_s[slot] = x
        q_s[slot] = _dot(h, wq_ref[...]) * (dkh ** -0.5)
        k_s[slot] = _dot(h, wk_ref[...])
        v_s[slot] = _dot(h, wv_ref[...]).astype(BF16)
        r = _dot(h, wr_ref[...])
        og_s[slot] = _silu(r)
        a = _dot(h, wa_ref[...]).astype(BF16)
        z = _dot(jnp.concatenate([a, a], axis=1), wa2_ref[...]) + ba_ref[...]
        la_s[slot] = -(jnp.maximum(-z, 0.0) + jnp.log(1.0 + jnp.exp(-jnp.abs(z)))) * (1.0 / GLA_TAU)

    def stage_b(slot):
        _gla_recurrence(x_s[slot], q_s[slot], k_s[slot], v_s[slot], og_s[slot], la_s[slot],
                        ng_ref, wo_ref, o_ref, state_ref)

    @pl.when(j % 2 == 0)
    def _():
        stage_b(1)
        stage_a(0)

    @pl.when(j % 2 == 1)
    def _():
        stage_b(0)
        stage_a(1)


def _gla_recurrence(x, q, k, v, out_gate, log_a, ng_ref, wo_ref, o_ref, state_ref):
    ts, d = x.shape
    dk = q.shape[1]
    dkh = dk // GLA_HEADS
    dvh = d // GLA_HEADS

    row = lax.broadcasted_iota(I32, (ts, ts), 0)
    col = lax.broadcasted_iota(I32, (ts, ts), 1)
    tri = jnp.where(col <= row, 1.0, 0.0).astype(BF16)
    bcum = _dot(tri, log_a.astype(BF16))
    b_last = bcum[ts - 1:ts, :]

    q_in = (q * jnp.exp(bcum)).astype(BF16)
    k_out = (k * jnp.exp(b_last - bcum)).astype(BF16)

    sizes = _gla_levels(ts)
    scores = [None] * GLA_HEADS
    for lvl, size in enumerate(sizes):
        half = size // 2
        same_block = (row & -size) == (col & -size)
        if lvl == 0:
            pair = same_block & (col <= row)
            q_ok = k_ok = None
        else:
            pair = same_block & ((row & (size - 1)) >= half) & ((col & (size - 1)) < half)
            pos = lax.broadcasted_iota(I32, (ts, dk), 0) & (size - 1)
            q_ok = pos >= half
            k_ok = pos < half
        ref = jnp.concatenate(
            [jnp.broadcast_to(bcum[r0 + half:r0 + half + 1, :], (size, dk)) for r0 in range(0, ts, size)], axis=0)
        ql = q * jnp.exp(bcum - ref)
        kl = k * jnp.exp(ref - bcum)
        if q_ok is not None:
            ql = jnp.where(q_ok, ql, 0.0)
            kl = jnp.where(k_ok, kl, 0.0)
        ql = ql.astype(BF16)
        kl = kl.astype(BF16)
        for hd in range(GLA_HEADS):
            c0 = hd * dkh
            a = _dot_nt(ql[:, c0:c0 + dkh], kl[:, c0:c0 + dkh])
            a = jnp.where(pair, a, 0.0)
            scores[hd] = a if scores[hd] is None else scores[hd] + a

    outs = []
    for hd in range(GLA_HEADS):
        c0 = hd * dkh
        v_h = v[:, hd * dvh:(hd + 1) * dvh]
        st = state_ref[hd]
        o_h = _dot(scores[hd].astype(BF16), v_h) + _dot_nt(q_in[:, c0:c0 + dkh], st.astype(BF16))
        decay = jnp.exp(b_last[:, c0:c0 + dkh])
        state_ref[hd] = st * decay + _dot_tn(v_h, k_out[:, c0:c0 + dkh])
        o_h = o_h * lax.rsqrt(jnp.mean(o_h * o_h, axis=-1, keepdims=True) + EPS) * ng_ref[...]
        outs.append(o_h)
    o = jnp.concatenate(outs, axis=1) * out_gate
    o_ref[0] = x + _dot(o.astype(BF16), wo_ref[...])


def _gla_mixer(x, gates, y0, y1, g, w_in, w_a2, b_a, norm_g, w_o):
    b, s, d = x.shape
    dk = w_a2.shape[1]
    ts = min(TILE_GLA, s)
    assert s % ts == 0 and ts % GLA_LEAF == 0
    nj = s // ts
    row = lambda v: v.reshape(1, -1)
    const = lambda shape: pl.BlockSpec(shape, lambda j: (0,) * len(shape))
    once = lambda shape: pl.BlockSpec(shape, lambda j: (0,) * len(shape), pipeline_mode=pl.Buffered(1))
    assert w_in.shape[1] == 2 * dk + 2 * d + GLA_RANK
    wa2 = jnp.pad(w_a2, ((0, GLA_RANK_PAD - GLA_RANK), (0, 0)))
    wa2_hi = wa2.astype(BF16)
    wa2_lo = (wa2 - wa2_hi.astype(F32)).astype(BF16)
    wa2_split = jnp.concatenate([wa2_hi, wa2_lo], axis=0)
    dvh = d // GLA_HEADS
    n = b * nj

    def tile(j, lag):
        return jnp.clip(j - lag, 0, n - 1)

    tokens = lambda lag: (lambda j: (tile(j, lag), 0))
    return pl.pallas_call(
        functools.partial(_gla_kernel, tiles_per_seq=nj),
        out_shape=jax.ShapeDtypeStruct(x.shape, F32),
        grid=(n + 1,),
        in_specs=[
            pl.BlockSpec((1, ts, d), lambda j: (tile(j, 0) // nj, tile(j, 0) % nj, 0)),
            pl.BlockSpec((ROUTE_ROWS, ts), lambda j: (0, tile(j, 0))),
            pl.BlockSpec((ts, d // 2), tokens(0)),
            pl.BlockSpec((ts, d // 2), tokens(0)),
            const((1, d)), once(w_in.shape), const((2 * GLA_RANK_PAD, dk)), const((1, dk)), const((1, dvh)),
            once((d, d)),
        ],
        out_specs=pl.BlockSpec((1, ts, d), lambda j: (tile(j, 1) // nj, tile(j, 1) % nj, 0)),
        scratch_shapes=[pltpu.VMEM((GLA_HEADS, dvh, dk // GLA_HEADS), F32),
                        pltpu.VMEM((2, ts, d), F32), pltpu.VMEM((2, ts, dk), F32), pltpu.VMEM((2, ts, dk), F32),
                        pltpu.VMEM((2, ts, d), BF16), pltpu.VMEM((2, ts, d), F32), pltpu.VMEM((2, ts, dk), F32),
                        pltpu.VMEM((d, dk), BF16), pltpu.VMEM((d, dk), BF16), pltpu.VMEM((d, d), BF16),
                        pltpu.VMEM((d, GLA_RANK_PAD), BF16), pltpu.VMEM((d, d), BF16), pltpu.VMEM((d, d), BF16)],
        compiler_params=_cparams(("arbitrary",)),
        name="gla_mixer",
    )(x, gates, y0, y1, row(g), w_in, wa2_split, row(b_a), row(norm_g), w_o)


def _xattn_router_kernel(x_ref, gx_ref, qk_ref, vo_ref, gf_ref, wr_ref, br_ref, upper_ref,
                         x_out_ref, h_out_ref, route_ref, cnt_ref, carry_ref):
    ts, d = x_ref.shape[1], x_ref.shape[2]
    nm = qk_ref.shape[3] // XATTN_HEADS
    first = (pl.program_id(0) == 0) & (pl.program_id(1) == 0)

    @pl.when(first)
    def _():
        carry_ref[...] = jnp.zeros(carry_ref.shape, F32)

    x = x_ref[0]
    h = _rms(x, gx_ref[...]).astype(BF16)
    scores = _dot(h, qk_ref[0, 0])
    probs = []
    for a in range(XATTN_HEADS):
        s = scores[:, a * nm:(a + 1) * nm]
        p = jnp.exp(s - jnp.max(s, axis=-1, keepdims=True))
        probs.append((p / jnp.sum(p, axis=-1, keepdims=True)).astype(BF16))
    x2 = x + _dot(jnp.concatenate(probs, axis=1), vo_ref[0, 0])
    x_out_ref[0] = x2

    hf = _rms(x2, gf_ref[...])
    h_out_ref[...] = _pack_bf16_pairs(hf)

    both = _dot_nt(wr_ref[...], hf.astype(BF16))
    logits = both[0:ROUTER_ROWS, :] + both[ROUTER_ROWS:2 * ROUTER_ROWS, :] + br_ref[...]
    gl = logits[N_EXPERTS:N_EXPERTS + N_GROUPS, :]
    gi = lax.broadcasted_iota(I32, gl.shape, 0).astype(F32)
    gmax = jnp.max(gl, axis=0, keepdims=True)
    g_sel = jnp.min(jnp.where(gl == gmax, gi, float(N_GROUPS)), axis=0, keepdims=True)
    pg_sel = 1.0 / jnp.sum(jnp.exp(gl - gmax), axis=0, keepdims=True)

    el = jnp.zeros((EXPERTS_PER_GROUP, ts), F32)
    for gidx in range(N_GROUPS):
        lo = gidx * EXPERTS_PER_GROUP
        el = jnp.where(g_sel == float(gidx), logits[lo:lo + EXPERTS_PER_GROUP, :], el)
    ei = lax.broadcasted_iota(I32, el.shape, 0).astype(F32)
    m1 = jnp.max(el, axis=0, keepdims=True)
    i1 = jnp.min(jnp.where(el == m1, ei, float(EXPERTS_PER_GROUP)), axis=0, keepdims=True)
    rest = jnp.where(ei == i1, -jnp.inf, el)
    m2 = jnp.max(rest, axis=0, keepdims=True)
    i2 = jnp.min(jnp.where(rest == m2, ei, float(EXPERTS_PER_GROUP)), axis=0, keepdims=True)
    ratio = jnp.exp(m2 - m1)
    gate1 = pg_sel / (1.0 + ratio)
    gate2 = pg_sel * ratio / (1.0 + ratio)
    e1 = g_sel * float(EXPERTS_PER_GROUP) + i1
    e2 = g_sel * float(EXPERTS_PER_GROUP) + i2

    xi = lax.broadcasted_iota(I32, (N_EXPERTS, ts), 0).astype(F32)
    oh1 = jnp.where(xi == e1, 1.0, 0.0)
    oh2 = jnp.where(xi == e2, 1.0, 0.0)
    oh = oh1 + oh2
    n_blk = ts // LANES
    stacked = jnp.concatenate([oh[:, c * LANES:(c + 1) * LANES] for c in range(n_blk)], axis=0)
    within = _dot(stacked.astype(BF16), upper_ref[...])
    totals = jnp.sum(stacked, axis=1, keepdims=True)
    run = carry_ref[...]
    before = []
    for c in range(n_blk):
        before.append(within[c * N_EXPERTS:(c + 1) * N_EXPERTS, :] + run)
        run = run + totals[c * N_EXPERTS:(c + 1) * N_EXPERTS, :]
    before = jnp.concatenate(before, axis=1)
    rank1 = jnp.sum(oh1 * before, axis=0, keepdims=True)
    rank2 = jnp.sum(oh2 * before, axis=0, keepdims=True)
    carry_ref[...] = run
    cnt_ref[...] = jnp.broadcast_to(run, cnt_ref.shape)

    route_ref[0:1, :] = e1.astype(I32)
    route_ref[1:2, :] = e2.astype(I32)
    route_ref[2:3, :] = rank1.astype(I32)
    route_ref[3:4, :] = rank2.astype(I32)
    route_ref[4:5, :] = lax.bitcast_convert_type(gate1, I32)
    route_ref[5:6, :] = lax.bitcast_convert_type(gate2, I32)
    route_ref[3 * TOP_K:ROUTE_ROWS, :] = jnp.zeros((ROUTE_ROWS - 3 * TOP_K, ts), I32)


def _xattn_router(x, g_x, qk_mem, vo_mem, layer, g_f, w_grp, b_grp, w_exp, b_exp):
    b, s, d = x.shape
    hm = qk_mem.shape[3]
    t = b * s
    ts = min(TILE_XATTN, s)
    assert s % ts == 0 and ts % LANES == 0
    nj = s // ts
    row = lambda v: v.reshape(1, -1)
    const = lambda shape: pl.BlockSpec(shape, lambda i, j: (0,) * len(shape))
    pad = ROUTER_ROWS - N_GROUPS - N_EXPERTS
    w_r = jnp.pad(jnp.concatenate([w_exp, w_grp], axis=1).T, ((0, pad), (0, 0)))
    w_r_hi = w_r.astype(BF16)
    w_r_split = jnp.concatenate([w_r_hi, (w_r - w_r_hi.astype(F32)).astype(BF16)], axis=0)
    b_r = jnp.pad(jnp.concatenate([b_exp, b_grp]), (0, pad)).reshape(ROUTER_ROWS, 1)
    ti = jnp.arange(LANES)
    upper = (ti[:, None] < ti[None, :]).astype(BF16)
    return pl.pallas_call(
        _xattn_router_kernel,
        out_shape=(
            jax.ShapeDtypeStruct(x.shape, F32),
            jax.ShapeDtypeStruct((t, d // 2), U32),
            jax.ShapeDtypeStruct((ROUTE_ROWS, t), I32),
            jax.ShapeDtypeStruct((N_EXPERTS, LANES), F32),
        ),
        grid=(b, nj),
        in_specs=[
            pl.BlockSpec((1, ts, d), lambda i, j: (i, j, 0)),
            const((1, d)),
            pl.BlockSpec((1, 1, d, hm), lambda i, j: (layer, i, 0, 0)),
            pl.BlockSpec((1, 1, hm, d), lambda i, j: (layer, i, 0, 0)),
            const((1, d)), const((2 * ROUTER_ROWS, d)), const((ROUTER_ROWS, 1)), const((LANES, LANES)),
        ],
        out_specs=(
            pl.BlockSpec((1, ts, d), lambda i, j: (i, j, 0)),
            pl.BlockSpec((ts, d // 2), lambda i, j: (i * nj + j, 0)),
            pl.BlockSpec((ROUTE_ROWS, ts), lambda i, j: (0, i * nj + j)),
            pl.BlockSpec((N_EXPERTS, LANES), lambda i, j: (0, 0)),
        ),
        scratch_shapes=[pltpu.VMEM((N_EXPERTS, 1), F32)],
        compiler_params=_cparams(("arbitrary", "arbitrary")),
        name="xattn_router",
    )(x, row(g_x), qk_mem, vo_mem, row(g_f), w_r_split, b_r, upper)


def _sc_mesh():
    return plsc.VectorSubcoreMesh(core_axis_name="c", subcore_axis_name="s",
                                  num_cores=SC_CORES, num_subcores=SC_SUBCORES)


def _sc_worker():
    return lax.axis_index("s") * SC_CORES + lax.axis_index("c")


def _dispatch(h_packed, dest, n_rows):
    t, w = h_packed.shape
    chunk = SC_DISPATCH_CHUNK
    per_worker = t // SC_WORKERS
    n_chunks = per_worker // chunk
    assert n_chunks % 2 == 0 and n_chunks * chunk * SC_WORKERS == t
    dest = dest.reshape(TOP_K, SC_WORKERS * n_chunks, chunk)
    rows_buf = pltpu.VMEM((chunk, w), U32)

    @functools.partial(
        pl.kernel, mesh=_sc_mesh(),
        out_type=jax.ShapeDtypeStruct((n_rows, w), U32),
        scratch_types=[pltpu.VMEM((n_chunks, chunk), I32), pltpu.VMEM((n_chunks, chunk), I32), rows_buf, rows_buf,
                       pltpu.SemaphoreType.DMA((2,)), pltpu.SemaphoreType.DMA((2, TOP_K))],
        name="moe_dispatch_sc",
    )
    def run(h_hbm, d0_hbm, d1_hbm, xbuf_hbm, idx0_v, idx1_v, buf_a, buf_b, read_sem, write_sem):
        wid = _sc_worker()
        pltpu.sync_copy(d0_hbm.at[pl.ds(wid * n_chunks, n_chunks)], idx0_v)
        pltpu.sync_copy(d1_hbm.at[pl.ds(wid * n_chunks, n_chunks)], idx1_v)

        @pl.loop(0, n_chunks, step=2)
        def _(i):
            reads = [pltpu.async_copy(h_hbm.at[pl.ds(wid * per_worker + (i + j) * chunk, chunk)], buf, read_sem.at[j])
                     for j, buf in enumerate((buf_a, buf_b))]
            writes = []
            for j, buf in enumerate((buf_a, buf_b)):
                reads[j].wait()
                writes.append(pltpu.async_copy(buf, xbuf_hbm.at[idx0_v.at[i + j]], write_sem.at[j, 0]))
                writes.append(pltpu.async_copy(buf, xbuf_hbm.at[idx1_v.at[i + j]], write_sem.at[j, 1]))
            for copy in writes:
                copy.wait()

    return run(h_packed, dest[0], dest[1])


def _gather_pairs(y_buf, dest):
    t = dest.shape[1]
    w = y_buf.shape[1]
    chunk = SC_GATHER_CHUNK
    per_worker = t // SC_WORKERS
    n_chunks = per_worker // chunk
    assert n_chunks % 2 == 0 and n_chunks * chunk * SC_WORKERS == t
    dest = dest.reshape(TOP_K, SC_WORKERS * n_chunks, chunk)
    out = jax.ShapeDtypeStruct((t, w), U32)
    rows_buf = pltpu.VMEM((chunk, w), U32)

    @functools.partial(
        pl.kernel, mesh=_sc_mesh(),
        out_type=(out, out),
        scratch_types=[pltpu.VMEM((n_chunks, chunk), I32), pltpu.VMEM((n_chunks, chunk), I32),
                       rows_buf, rows_buf, rows_buf, rows_buf, pltpu.SemaphoreType.DMA((2, TOP_K))],
        name="moe_gather_sc",
    )
    def run(y_hbm, d0_hbm, d1_hbm, y0_hbm, y1_hbm, idx0_v, idx1_v, buf_a0, buf_a1, buf_b0, buf_b1, sem):
        wid = _sc_worker()
        pltpu.sync_copy(d0_hbm.at[pl.ds(wid * n_chunks, n_chunks)], idx0_v)
        pltpu.sync_copy(d1_hbm.at[pl.ds(wid * n_chunks, n_chunks)], idx1_v)

        @pl.loop(0, n_chunks, step=2)
        def _(i):
            bufs = ((buf_a0, buf_a1), (buf_b0, buf_b1))
            gathers = [[pltpu.async_copy(y_hbm.at[idx_v.at[i + j]], bufs[j][k], sem.at[j, k])
                        for k, idx_v in enumerate((idx0_v, idx1_v))] for j in range(2)]
            stores = []
            for j in range(2):
                rows = pl.ds(wid * per_worker + (i + j) * chunk, chunk)
                for k, out_hbm in enumerate((y0_hbm, y1_hbm)):
                    gathers[j][k].wait()
                    stores.append(pltpu.async_copy(bufs[j][k], out_hbm.at[rows], sem.at[j, k]))
            for copy in stores:
                copy.wait()

    return run(y_buf, dest[0], dest[1])


def _expert_kernel(be_ref, bf_ref, bv_ref, slot_ref, next_ref, x_ref, wg_hbm, wu_hbm, wd_hbm, y_ref,
                   wg_f32, wu_f32, wd_f32, wg_bf, wu_bf, wd_bf, sems, *, layer):
    blk = pl.program_id(0)
    valid = bv_ref[blk]

    def weight_copies(expert, slot):
        return [pltpu.make_async_copy(src.at[layer, expert], dst.at[slot], sems.at[slot, i])
                for i, (src, dst) in enumerate(((wg_hbm, wg_f32), (wu_hbm, wu_f32), (wd_hbm, wd_f32)))]

    @pl.when(blk == 0)
    def _():
        for copy in weight_copies(be_ref[0], 0):
            copy.start()

    @pl.when(bf_ref[blk] == 1)
    def _():
        slot = slot_ref[blk]
        for copy in weight_copies(be_ref[blk], slot):
            copy.wait()

        @pl.when(next_ref[blk] >= 0)
        def _():
            for copy in weight_copies(next_ref[blk], 1 - slot):
                copy.start()

        wg_bf[...] = wg_f32[slot].astype(BF16)
        wu_bf[...] = wu_f32[slot].astype(BF16)
        wd_bf[...] = wd_f32[slot].astype(BF16)

    @pl.when(valid > 0)
    def _():
        live = lax.broadcasted_iota(I32, x_ref.shape, 0) < valid
        xb = _unpack_bf16_pairs(jnp.where(live, x_ref[...], jnp.uint32(0))).astype(BF16)
        gt = _dot(xb, wg_bf[...])
        up = _dot(xb, wu_bf[...])
        act = (_silu(gt) * up).astype(BF16)
        y_ref[...] = _pack_bf16_pairs(_dot(act, wd_bf[...]))

    @pl.when(valid <= 0)
    def _():
        y_ref[...] = jnp.zeros(y_ref.shape, U32)


def _experts(x_buf, block_expert, block_first, block_valid, w_gate, w_up, w_down, layer):
    n_rows, w = x_buf.shape
    d, de = w_gate.shape[2], w_gate.shape[3]
    bm = MOE_BLOCK_ROWS
    n = n_rows // bm
    block_slot = (jnp.cumsum(block_first) - 1) % 2
    idx = jnp.arange(n, dtype=I32)
    later_first = jnp.concatenate([jnp.where(block_first[1:] == 1, idx[1:], n), jnp.full((1,), n, I32)])
    next_first = lax.cummin(later_first, reverse=True)
    block_next = jnp.where(next_first < n, block_expert[jnp.minimum(next_first, n - 1)], -1).astype(I32)
    any_space = pl.BlockSpec(memory_space=pl.ANY)
    grid_spec = pltpu.PrefetchScalarGridSpec(
        num_scalar_prefetch=5,
        grid=(n,),
        in_specs=[pl.BlockSpec((bm, w), lambda i, *_: (i, 0)), any_space, any_space, any_space],
        out_specs=pl.BlockSpec((bm, w), lambda i, *_: (i, 0)),
        scratch_shapes=[pltpu.VMEM((2, d, de), F32), pltpu.VMEM((2, d, de), F32), pltpu.VMEM((2, de, d), F32),
                        pltpu.VMEM((d, de), BF16), pltpu.VMEM((d, de), BF16), pltpu.VMEM((de, d), BF16),
                        pltpu.SemaphoreType.DMA((2, 3))],
    )
    return pl.pallas_call(
        functools.partial(_expert_kernel, layer=layer),
        out_shape=jax.ShapeDtypeStruct((n_rows, w), U32),
        grid_spec=grid_spec,
        compiler_params=_cparams(("arbitrary",)),
        name="moe_experts",
    )(block_expert, block_first, block_valid, block_slot.astype(I32), block_next, x_buf, w_gate, w_up, w_down)


def _combine_kernel(x_ref, gate_ref, y0_ref, y1_ref, gfin_ref, o_ref, *, final_norm):
    out = _moe_residual(x_ref[...], gate_ref[...], y0_ref[...], y1_ref[...])
    if final_norm:
        out = _rms(out, gfin_ref[...])
    o_ref[...] = out


def _combine(x2, y0, y1, gates, g_final, final_norm):
    t, d = x2.shape
    w = y0.shape[1]
    ts = min(TILE_COMBINE, t)
    assert t % ts == 0
    return pl.pallas_call(
        functools.partial(_combine_kernel, final_norm=final_norm),
        out_shape=jax.ShapeDtypeStruct((t, d), F32),
        grid=(t // ts,),
        in_specs=[
            pl.BlockSpec((ts, d), lambda i: (i, 0)),
            pl.BlockSpec((ROUTE_ROWS, ts), lambda i: (0, i)),
            pl.BlockSpec((ts, w), lambda i: (i, 0)),
            pl.BlockSpec((ts, w), lambda i: (i, 0)),
            pl.BlockSpec((1, d), lambda i: (0, 0)),
        ],
        out_specs=pl.BlockSpec((ts, d), lambda i: (i, 0)),
        compiler_params=_cparams(("arbitrary",)),
        name="moe_combine",
    )(x2, gates, y0, y1, g_final.reshape(1, d))


def _moe_layout(route, counts):
    bm = MOE_BLOCK_ROWS
    t = route.shape[1]
    assert (t * TOP_K) % bm == 0
    n_blocks = (t * TOP_K) // bm + N_EXPERTS
    cnt = counts[:, 0].astype(I32)
    padded = (cnt + bm - 1) // bm * bm
    pad_ends = jnp.cumsum(padded)
    pad_off = pad_ends - padded
    experts = jnp.arange(N_EXPERTS, dtype=I32)
    hit = route[0:TOP_K, :, None] == experts
    dest = jnp.sum(jnp.where(hit, pad_off, 0), axis=-1) + route[TOP_K:2 * TOP_K]
    gates = route
    starts = jnp.arange(n_blocks, dtype=I32) * bm
    block_expert = jnp.minimum(jnp.sum((pad_ends[None, :] <= starts[:, None]).astype(I32), axis=1),
                               N_EXPERTS - 1)
    block_first = jnp.concatenate([jnp.ones((1,), I32), (block_expert[1:] != block_expert[:-1]).astype(I32)])
    own = block_expert[:, None] == experts
    block_valid = jnp.clip(jnp.sum(jnp.where(own, cnt + pad_off, 0), axis=1) - starts, 0, bm)
    block_valid = jnp.where(starts < pad_ends[-1], block_valid, 0).astype(I32)
    return dest, gates, block_expert, block_first, block_valid, n_blocks * bm


def kernel(x, mem, norm_mix, norm_xattn, norm_ffn, norm_mem, norm_final, conv_w_in, conv_b_in, conv_w_dw,
           conv_b_dw, conv_ln_g, conv_ln_b, conv_w_out, conv_b_out, gla_w_in, gla_w_a2, gla_b_a, gla_norm_g,
           gla_w_o, xa_w_q, xa_w_kv, xa_w_o, moe_w_grp, moe_b_grp, moe_w_exp, moe_b_exp, moe_w_gate, moe_w_up,
           moe_w_down):
    b, s, d = x.shape
    depth = norm_mix.shape[0]
    qk_mem, vo_mem = _mem_kv(mem, norm_mem, xa_w_kv, xa_w_q, xa_w_o)
    moe = None
    for i in range(depth):
        j = i // 2
        if i % 2 == 0:
            if moe is not None:
                x = _combine(x.reshape(b * s, d), moe[1], moe[2], moe[0], norm_final, False).reshape(b, s, d)
            x = _conv_mixer(x, norm_mix[i], conv_w_in[j], conv_b_in[j], conv_w_dw[j], conv_b_dw[j],
                            conv_ln_g[j], conv_ln_b[j], conv_w_out[j], conv_b_out[j])
        else:
            x = _gla_mixer(x, *moe, norm_mix[i], gla_w_in[j], gla_w_a2[j], gla_b_a[j], gla_norm_g[j], gla_w_o[j])
        x2, h_packed, route, counts = _xattn_router(
            x, norm_xattn[i], qk_mem, vo_mem, i, norm_ffn[i],
            moe_w_grp[i], moe_b_grp[i], moe_w_exp[i], moe_b_exp[i])
        dest, gates, block_expert, block_first, block_valid, n_rows = _moe_layout(route, counts)
        x_buf = _dispatch(h_packed, dest, n_rows)
        y_buf = _experts(x_buf, block_expert, block_first, block_valid, moe_w_gate, moe_w_up, moe_w_down, i)
        y0, y1 = _gather_pairs(y_buf, dest)
        x, moe = x2, (gates, y0, y1)
    return _combine(x.reshape(b * s, d), moe[1], moe[2], moe[0], norm_final, True).reshape(b, s, d)
```

```python
import functools

import jax
import jax.numpy as jnp
from jax import lax
from jax.experimental import pallas as pl
from jax.experimental.pallas import tpu as pltpu
from jax.experimental.pallas import tpu_sc as plsc

F32 = jnp.float32
BF16 = jnp.bfloat16
I32 = jnp.int32
U32 = jnp.uint32

EPS = 1e-6
CONV_KERNEL = 31
CONV_CARRY = 32
CONV_ROWS = 64
CONV_COLS = 256
SUBLANES = 8
LANES = 128
GLA_HEADS = 4
GLA_RANK = 16
GLA_RANK_PAD = 128
GLA_TAU = 16.0
GLA_LEAF = 32
XATTN_HEADS = 4
N_GROUPS = 4
EXPERTS_PER_GROUP = 8
N_EXPERTS = N_GROUPS * EXPERTS_PER_GROUP
ROUTER_ROWS = 40
TOP_K = 2
ROUTE_ROWS = 8

TILE_CONV = 512
TILE_GLA = 256
TILE_XATTN = 1024
TILE_COMBINE = 512
SC_CORES = 2
SC_SUBCORES = 16
SC_WORKERS = SC_CORES * SC_SUBCORES
SC_DISPATCH_CHUNK = 64
SC_GATHER_CHUNK = 32
MOE_BLOCK_ROWS = 512
VMEM_LIMIT = 56 * 1024 * 1024


def _cparams(sem):
    return pltpu.CompilerParams(dimension_semantics=sem, vmem_limit_bytes=VMEM_LIMIT)


def _rms(x, g):
    return x * lax.rsqrt(jnp.mean(x * x, axis=-1, keepdims=True) + EPS) * g


def _sigmoid(x):
    return 0.5 * jnp.tanh(0.5 * x) + 0.5


def _silu(x):
    h = 0.5 * x
    return h + h * jnp.tanh(h)


def _dot(a, b):
    return jnp.dot(a, b, preferred_element_type=F32)


def _dot_nt(a, b):
    return lax.dot_general(a, b, (((1,), (1,)), ((), ())), preferred_element_type=F32)


def _dot_tn(a, b):
    return lax.dot_general(a, b, (((0,), (0,)), ((), ())), preferred_element_type=F32)


def _pack_bf16_pairs(x):
    w = x.shape[1] // 2
    hi = lax.bitcast_convert_type(x[:, :w].astype(BF16).astype(F32), U32)
    lo = lax.bitcast_convert_type(x[:, w:].astype(BF16).astype(F32), U32)
    return hi | (lo >> 16)


def _unpack_bf16_pairs(p):
    hi = lax.bitcast_convert_type(p & jnp.uint32(0xFFFF0000), F32)
    lo = lax.bitcast_convert_type(p << 16, F32)
    return jnp.concatenate([hi, lo], axis=1)


def _memkv_kernel(mem_ref, g_ref, wkv_ref, wq_ref, wo_ref, qk_ref, vo_ref, wkv_bf, wq_bf, wo_bf):
    nm, d = mem_ref.shape[1], mem_ref.shape[2]
    hd = d // XATTN_HEADS

    @pl.when(pl.program_id(1) == 0)
    def _():
        wkv_bf[...] = wkv_ref[0].astype(BF16)
        wq_bf[...] = wq_ref[0].astype(BF16)
        wo_bf[...] = wo_ref[0].astype(BF16)

    mn = _rms(mem_ref[0], g_ref[...]).astype(BF16)
    kv = _dot(mn, wkv_bf[...])
    k = kv[:, :d].astype(BF16)
    v = kv[:, d:].astype(BF16)
    for a in range(XATTN_HEADS):
        sl = slice(a * hd, (a + 1) * hd)
        qk_ref[0, 0, :, a * nm:(a + 1) * nm] = (_dot_nt(wq_bf[:, sl], k[:, sl]) * (hd ** -0.5)).astype(BF16)
        vo_ref[0, 0, a * nm:(a + 1) * nm, :] = _dot(v[:, sl], wo_bf[sl, :]).astype(BF16)


def _mem_kv(mem, norm_mem, w_kv, w_q, w_o, layer):
    b, nm, d = mem.shape
    per_layer = lambda shape: pl.BlockSpec((1,) + shape, lambda l, i: (layer, 0, 0))
    return pl.pallas_call(
        _memkv_kernel,
        out_shape=(jax.ShapeDtypeStruct((1, b, d, XATTN_HEADS * nm), BF16),
                   jax.ShapeDtypeStruct((1, b, XATTN_HEADS * nm, d), BF16)),
        grid=(1, b),
        in_specs=[
            pl.BlockSpec((1, nm, d), lambda l, i: (i, 0, 0)),
            pl.BlockSpec((1, d), lambda l, i: (0, 0)),
            per_layer((d, 2 * d)), per_layer((d, d)), per_layer((d, d)),
        ],
        out_specs=(
            pl.BlockSpec((1, 1, d, XATTN_HEADS * nm), lambda l, i: (l, i, 0, 0)),
            pl.BlockSpec((1, 1, XATTN_HEADS * nm, d), lambda l, i: (l, i, 0, 0)),
        ),
        scratch_shapes=[pltpu.VMEM((d, 2 * d), BF16), pltpu.VMEM((d, d), BF16), pltpu.VMEM((d, d), BF16)],
        compiler_params=_cparams(("arbitrary", "arbitrary")),
        name="mem_kv",
    )(mem, norm_mem.reshape(1, d), w_kv, w_q, w_o)


def _conv_kernel(x_ref, g_ref, win_ref, bin_ref, wdw_ref, bdw_ref, lng_ref, lnb_ref, wout_ref, bout_ref,
                 o_ref, ext_ref, even_ref, odd_ref):
    ts, d = x_ref.shape[1], x_ref.shape[2]
    n_shifts = ext_ref.shape[0]

    @pl.when(pl.program_id(1) == 0)
    def _():
        ext_ref[...] = jnp.zeros(ext_ref.shape, F32)

    x = x_ref[0]
    h = _rms(x, g_ref[...]).astype(BF16)
    u = _dot(h, win_ref[...]) + bin_ref[...]
    glu = u[:, :d] * _sigmoid(u[:, d:])
    for c in range(n_shifts):
        ext_ref[c, CONV_CARRY - 2 * c:CONV_CARRY - 2 * c + ts, :] = glu

    first = CONV_CARRY - (CONV_KERNEL - 1)

    def taps(r0, n_rows, parity, dst_ref):
        for c0 in range(0, d, CONV_COLS):
            cols = slice(c0, c0 + CONV_COLS)
            acc = [jnp.zeros((SUBLANES, CONV_COLS), F32) for _ in range(n_rows // SUBLANES)]
            for k in range(CONV_KERNEL):
                if (first + k) % 2 != parity:
                    continue
                shift = (first + k - parity) % SUBLANES
                base = first + k - parity - shift
                wk = wdw_ref[k * SUBLANES:(k + 1) * SUBLANES, cols]
                for j in range(n_rows // SUBLANES):
                    rows = pl.ds(r0 + base + j * SUBLANES, SUBLANES)
                    acc[j] = acc[j] + ext_ref[shift // 2, rows, cols] * wk
            for j in range(n_rows // SUBLANES):
                dst_ref[pl.ds(r0 + j * SUBLANES, SUBLANES), cols] = acc[j]

    def chunk(i, carry):
        r0 = pl.multiple_of(i * CONV_ROWS, CONV_ROWS)
        taps(r0, CONV_ROWS, 0, even_ref)
        taps(r0, CONV_ROWS, 1, odd_ref)
        return carry

    lax.fori_loop(0, ts // CONV_ROWS, chunk, 0)
    taps(ts, SUBLANES, 1, odd_ref)
    for c in range(n_shifts):
        ext_ref[c, 0:CONV_CARRY, :] = ext_ref[c, ts:ts + CONV_CARRY, :]

    c = even_ref[...] + odd_ref[1:ts + 1, :] + bdw_ref[...]
    mu = jnp.mean(c, axis=-1, keepdims=True)
    cc = c - mu
    var = jnp.mean(cc * cc, axis=-1, keepdims=True)
    un = cc * lax.rsqrt(var + EPS) * lng_ref[...] + lnb_ref[...]
    act = _silu(un).astype(BF16)
    o_ref[0] = x + _dot(act, wout_ref[...]) + bout_ref[...]


def _conv_mixer(x, g, w_in, b_in, w_dw, b_dw, ln_g, ln_b, w_out, b_out):
    b, s, d = x.shape
    ts = min(TILE_CONV, s)
    assert s % ts == 0 and ts % CONV_ROWS == 0 and d % CONV_COLS == 0
    row = lambda v: v.reshape(1, -1)
    const = lambda shape: pl.BlockSpec(shape, lambda i, j: (0,) * len(shape))
    return pl.pallas_call(
        _conv_kernel,
        out_shape=jax.ShapeDtypeStruct(x.shape, F32),
        grid=(b, s // ts),
        in_specs=[
            pl.BlockSpec((1, ts, d), lambda i, j: (i, j, 0)),
            const((1, d)), const((d, 2 * d)), const((1, 2 * d)), const((CONV_KERNEL * SUBLANES, d)), const((1, d)),
            const((1, d)), const((1, d)), const((d, d)), const((1, d)),
        ],
        out_specs=pl.BlockSpec((1, ts, d), lambda i, j: (i, j, 0)),
        scratch_shapes=[pltpu.VMEM((SUBLANES // 2, CONV_CARRY + ts, d), F32), pltpu.VMEM((ts, d), F32),
                        pltpu.VMEM((ts + SUBLANES, d), F32)],
        compiler_params=_cparams(("arbitrary", "arbitrary")),
        name="conv_mixer",
    )(x, row(g), w_in.astype(BF16), row(b_in), jnp.repeat(w_dw, SUBLANES, axis=0), row(b_dw), row(ln_g), row(ln_b),
      w_out.astype(BF16), row(b_out))


def _gla_levels(ts):
    sizes = [GLA_LEAF]
    while sizes[-1] < ts:
        sizes.append(sizes[-1] * 2)
    return sizes


def _moe_residual(x, route, y0, y1):
    g = lax.bitcast_convert_type(route, F32)
    g = jnp.concatenate([g, jnp.zeros((LANES - g.shape[0], g.shape[1]), F32)], axis=0).T
    gate0 = g[:, 2 * TOP_K:2 * TOP_K + 1]
    gate1 = g[:, 2 * TOP_K + 1:2 * TOP_K + 2]
    return x + _unpack_bf16_pairs(y0) * gate0 + _unpack_bf16_pairs(y1) * gate1


def _gla_kernel(x_ref, gate_ref, y0_ref, y1_ref, g_ref, wq_ref, wk_ref, wv_ref, wa_ref, wr_ref, wa2_ref, ba_ref,
                ng_ref, wo_ref, o_ref, state_ref, x_s, q_s, k_s, v_s, og_s, la_s, *, tiles_per_seq):
    j = pl.program_id(0)
    dkh = wq_ref.shape[1] // GLA_HEADS

    @pl.when(j == 0)
    def _():
        for ref in (x_s, q_s, k_s, og_s, la_s):
            ref[...] = jnp.zeros(ref.shape, F32)
        v_s[...] = jnp.zeros(v_s.shape, BF16)

    @pl.when((j == 0) | ((j - 1) % tiles_per_seq == 0))
    def _():
        state_ref[...] = jnp.zeros(state_ref.shape, F32)

    def stage_a(slot):
        x = _moe_residual(x_ref[0], gate_ref[...], y0_ref[...], y1_ref[...])
        h = _rms(x, g_ref[...]).astype(BF16)
        x_s[slot] = x
        q_s[slot] = _dot(h, wq_ref[...]) * (dkh ** -0.5)
        k_s[slot] = _dot(h, wk_ref[...])
        v_s[slot] = _dot(h, wv_ref[...]).astype(BF16)
        r = _dot(h, wr_ref[...])
        og_s[slot] = _silu(r)
        a = _dot(h, wa_ref[...]).astype(BF16)
        z = _dot(jnp.concatenate([a, a], axis=1), wa2_ref[...]) + ba_ref[...]
        la_s[slot] = -(jnp.maximum(-z, 0.0) + jnp.log(1.0 + jnp.exp(-jnp.abs(z)))) * (1.0 / GLA_TAU)

    def stage_b(slot):
        _gla_recurrence(x_s[slot], q_s[slot], k_s[slot], v_s[slot], og_s[slot], la_s[slot],
                        ng_ref, wo_ref, o_ref, state_ref)

    @pl.when(j % 2 == 0)
    def _():
        stage_b(1)
        stage_a(0)

    @pl.when(j % 2 == 1)
    def _():
        stage_b(0)
        stage_a(1)


def _gla_recurrence(x, q, k, v, out_gate, log_a, ng_ref, wo_ref, o_ref, state_ref):
    ts, d = x.shape
    dk = q.shape[1]
    dkh = dk // GLA_HEADS
    dvh = d // GLA_HEADS

    row = lax.broadcasted_iota(I32, (ts, ts), 0)
    col = lax.broadcasted_iota(I32, (ts, ts), 1)
    tri = jnp.where(col <= row, 1.0, 0.0).astype(BF16)
    bcum = _dot(tri, log_a.astype(BF16))
    b_last = bcum[ts - 1:ts, :]

    q_in = (q * jnp.exp(bcum)).astype(BF16)
    k_out = (k * jnp.exp(b_last - bcum)).astype(BF16)

    sizes = _gla_levels(ts)
    scores = [None] * GLA_HEADS
    for lvl, size in enumerate(sizes):
        half = size // 2
        same_block = (row & -size) == (col & -size)
        if lvl == 0:
            pair = same_block & (col <= row)
            q_ok = k_ok = None
        else:
            pair = same_block & ((row & (size - 1)) >= half) & ((col & (size - 1)) < half)
            pos = lax.broadcasted_iota(I32, (ts, dk), 0) & (size - 1)
            q_ok = pos >= half
            k_ok = pos < half
        ref = jnp.concatenate(
            [jnp.broadcast_to(bcum[r0 + half:r0 + half + 1, :], (size, dk)) for r0 in range(0, ts, size)], axis=0)
        ql = q * jnp.exp(bcum - ref)
        kl = k * jnp.exp(ref - bcum)
        if q_ok is not None:
            ql = jnp.where(q_ok, ql, 0.0)
            kl = jnp.where(k_ok, kl, 0.0)
        ql = ql.astype(BF16)
        kl = kl.astype(BF16)
        for hd in range(GLA_HEADS):
            c0 = hd * dkh
            a = _dot_nt(ql[:, c0:c0 + dkh], kl[:, c0:c0 + dkh])
            a = jnp.where(pair, a, 0.0)
            scores[hd] = a if scores[hd] is None else scores[hd] + a

    outs = []
    for hd in range(GLA_HEADS):
        c0 = hd * dkh
        v_h = v[:, hd * dvh:(hd + 1) * dvh]
        st = state_ref[hd]
        o_h = _dot(scores[hd].astype(BF16), v_h) + _dot_nt(q_in[:, c0:c0 + dkh], st.astype(BF16))
        decay = jnp.exp(b_last[:, c0:c0 + dkh])
        state_ref[hd] = st * decay + _dot_tn(v_h, k_out[:, c0:c0 + dkh])
        o_h = o_h * lax.rsqrt(jnp.mean(o_h * o_h, axis=-1, keepdims=True) + EPS) * ng_ref[...]
        outs.append(o_h)
    o = jnp.concatenate(outs, axis=1) * out_gate
    o_ref[0] = x + _dot(o.astype(BF16), wo_ref[...])


def _gla_mixer(x, gates, y0, y1, g, w_in, w_a2, b_a, norm_g, w_o):
    b, s, d = x.shape
    dk = w_a2.shape[1]
    ts = min(TILE_GLA, s)
    assert s % ts == 0 and ts % GLA_LEAF == 0
    nj = s // ts
    row = lambda v: v.reshape(1, -1)
    const = lambda shape: pl.BlockSpec(shape, lambda j: (0,) * len(shape))
    wq = w_in[:, :dk].astype(BF16)
    wk = w_in[:, dk:2 * dk].astype(BF16)
    wv = w_in[:, 2 * dk:2 * dk + d].astype(BF16)
    wa = jnp.pad(w_in[:, 2 * dk + d:2 * dk + d + GLA_RANK], ((0, 0), (0, GLA_RANK_PAD - GLA_RANK))).astype(BF16)
    wr = w_in[:, 2 * dk + d + GLA_RANK:].astype(BF16)
    wa2 = jnp.pad(w_a2, ((0, GLA_RANK_PAD - GLA_RANK), (0, 0)))
    wa2_hi = wa2.astype(BF16)
    wa2_lo = (wa2 - wa2_hi.astype(F32)).astype(BF16)
    wa2_split = jnp.concatenate([wa2_hi, wa2_lo], axis=0)
    dvh = d // GLA_HEADS
    n = b * nj

    def tile(j, lag):
        return jnp.clip(j - lag, 0, n - 1)

    tokens = lambda lag: (lambda j: (tile(j, lag), 0))
    return pl.pallas_call(
        functools.partial(_gla_kernel, tiles_per_seq=nj),
        out_shape=jax.ShapeDtypeStruct(x.shape, F32),
        grid=(n + 1,),
        in_specs=[
            pl.BlockSpec((1, ts, d), lambda j: (tile(j, 0) // nj, tile(j, 0) % nj, 0)),
            pl.BlockSpec((ROUTE_ROWS, ts), lambda j: (0, tile(j, 0))),
            pl.BlockSpec((ts, d // 2), tokens(0)),
            pl.BlockSpec((ts, d // 2), tokens(0)),
            const((1, d)), const((d, dk)), const((d, dk)), const((d, d)), const((d, GLA_RANK_PAD)),
            const((d, d)), const((2 * GLA_RANK_PAD, dk)), const((1, dk)), const((1, dvh)), const((d, d)),
        ],
        out_specs=pl.BlockSpec((1, ts, d), lambda j: (tile(j, 1) // nj, tile(j, 1) % nj, 0)),
        scratch_shapes=[pltpu.VMEM((GLA_HEADS, dvh, dk // GLA_HEADS), F32),
                        pltpu.VMEM((2, ts, d), F32), pltpu.VMEM((2, ts, dk), F32), pltpu.VMEM((2, ts, dk), F32),
                        pltpu.VMEM((2, ts, d), BF16), pltpu.VMEM((2, ts, d), F32), pltpu.VMEM((2, ts, dk), F32)],
        compiler_params=_cparams(("arbitrary",)),
        name="gla_mixer",
    )(x, gates, y0, y1, row(g), wq, wk, wv, wa, wr, wa2_split, row(b_a), row(norm_g), w_o.astype(BF16))


def _xattn_router_kernel(x_ref, gx_ref, qk_ref, vo_ref, gf_ref, wr_ref, br_ref, upper_ref,
                         x_out_ref, h_out_ref, route_ref, cnt_ref, carry_ref):
    ts, d = x_ref.shape[1], x_ref.shape[2]
    nm = qk_ref.shape[3] // XATTN_HEADS
    first = (pl.program_id(0) == 0) & (pl.program_id(1) == 0)

    @pl.when(first)
    def _():
        carry_ref[...] = jnp.zeros(carry_ref.shape, F32)

    x = x_ref[0]
    h = _rms(x, gx_ref[...]).astype(BF16)
    scores = _dot(h, qk_ref[0, 0])
    probs = []
    for a in range(XATTN_HEADS):
        s = scores[:, a * nm:(a + 1) * nm]
        p = jnp.exp(s - jnp.max(s, axis=-1, keepdims=True))
        probs.append((p / jnp.sum(p, axis=-1, keepdims=True)).astype(BF16))
    x2 = x + _dot(jnp.concatenate(probs, axis=1), vo_ref[0, 0])
    x_out_ref[0] = x2

    hf = _rms(x2, gf_ref[...])
    h_out_ref[...] = _pack_bf16_pairs(hf)

    both = _dot_nt(wr_ref[...], hf.astype(BF16))
    logits = both[0:ROUTER_ROWS, :] + both[ROUTER_ROWS:2 * ROUTER_ROWS, :] + br_ref[...]
    gl = logits[N_EXPERTS:N_EXPERTS + N_GROUPS, :]
    gi = lax.broadcasted_iota(I32, gl.shape, 0).astype(F32)
    gmax = jnp.max(gl, axis=0, keepdims=True)
    g_sel = jnp.min(jnp.where(gl == gmax, gi, float(N_GROUPS)), axis=0, keepdims=True)
    pg_sel = 1.0 / jnp.sum(jnp.exp(gl - gmax), axis=0, keepdims=True)

    el = jnp.zeros((EXPERTS_PER_GROUP, ts), F32)
    for gidx in range(N_GROUPS):
        lo = gidx * EXPERTS_PER_GROUP
        el = jnp.where(g_sel == float(gidx), logits[lo:lo + EXPERTS_PER_GROUP, :], el)
    ei = lax.broadcasted_iota(I32, el.shape, 0).astype(F32)
    m1 = jnp.max(el, axis=0, keepdims=True)
    i1 = jnp.min(jnp.where(el == m1, ei, float(EXPERTS_PER_GROUP)), axis=0, keepdims=True)
    rest = jnp.where(ei == i1, -jnp.inf, el)
    m2 = jnp.max(rest, axis=0, keepdims=True)
    i2 = jnp.min(jnp.where(rest == m2, ei, float(EXPERTS_PER_GROUP)), axis=0, keepdims=True)
    ratio = jnp.exp(m2 - m1)
    gate1 = pg_sel / (1.0 + ratio)
    gate2 = pg_sel * ratio / (1.0 + ratio)
    e1 = g_sel * float(EXPERTS_PER_GROUP) + i1
    e2 = g_sel * float(EXPERTS_PER_GROUP) + i2

    xi = lax.broadcasted_iota(I32, (N_EXPERTS, ts), 0).astype(F32)
    oh1 = jnp.where(xi == e1, 1.0, 0.0)
    oh2 = jnp.where(xi == e2, 1.0, 0.0)
    oh = oh1 + oh2
    n_blk = ts // LANES
    stacked = jnp.concatenate([oh[:, c * LANES:(c + 1) * LANES] for c in range(n_blk)], axis=0)
    within = _dot(stacked.astype(BF16), upper_ref[...])
    totals = jnp.sum(stacked, axis=1, keepdims=True)
    run = carry_ref[...]
    before = []
    for c in range(n_blk):
        before.append(within[c * N_EXPERTS:(c + 1) * N_EXPERTS, :] + run)
        run = run + totals[c * N_EXPERTS:(c + 1) * N_EXPERTS, :]
    before = jnp.concatenate(before, axis=1)
    rank1 = jnp.sum(oh1 * before, axis=0, keepdims=True)
    rank2 = jnp.sum(oh2 * before, axis=0, keepdims=True)
    carry_ref[...] = run
    cnt_ref[...] = jnp.broadcast_to(run, cnt_ref.shape)

    route_ref[0:1, :] = e1.astype(I32)
    route_ref[1:2, :] = e2.astype(I32)
    route_ref[2:3, :] = rank1.astype(I32)
    route_ref[3:4, :] = rank2.astype(I32)
    route_ref[4:5, :] = lax.bitcast_convert_type(gate1, I32)
    route_ref[5:6, :] = lax.bitcast_convert_type(gate2, I32)
    route_ref[3 * TOP_K:ROUTE_ROWS, :] = jnp.zeros((ROUTE_ROWS - 3 * TOP_K, ts), I32)


def _xattn_router(x, g_x, qk_mem, vo_mem, g_f, w_grp, b_grp, w_exp, b_exp):
    b, s, d = x.shape
    hm = qk_mem.shape[3]
    t = b * s
    ts = min(TILE_XATTN, s)
    assert s % ts == 0 and ts % LANES == 0
    nj = s // ts
    row = lambda v: v.reshape(1, -1)
    const = lambda shape: pl.BlockSpec(shape, lambda i, j: (0,) * len(shape))
    pad = ROUTER_ROWS - N_GROUPS - N_EXPERTS
    w_r = jnp.pad(jnp.concatenate([w_exp, w_grp], axis=1).T, ((0, pad), (0, 0)))
    w_r_hi = w_r.astype(BF16)
    w_r_split = jnp.concatenate([w_r_hi, (w_r - w_r_hi.astype(F32)).astype(BF16)], axis=0)
    b_r = jnp.pad(jnp.concatenate([b_exp, b_grp]), (0, pad)).reshape(ROUTER_ROWS, 1)
    ti = jnp.arange(LANES)
    upper = (ti[:, None] < ti[None, :]).astype(BF16)
    return pl.pallas_call(
        _xattn_router_kernel,
        out_shape=(
            jax.ShapeDtypeStruct(x.shape, F32),
            jax.ShapeDtypeStruct((t, d // 2), U32),
            jax.ShapeDtypeStruct((ROUTE_ROWS, t), I32),
            jax.ShapeDtypeStruct((N_EXPERTS, LANES), F32),
        ),
        grid=(b, nj),
        in_specs=[
            pl.BlockSpec((1, ts, d), lambda i, j: (i, j, 0)),
            const((1, d)),
            pl.BlockSpec((1, 1, d, hm), lambda i, j: (0, i, 0, 0)),
            pl.BlockSpec((1, 1, hm, d), lambda i, j: (0, i, 0, 0)),
            const((1, d)), const((2 * ROUTER_ROWS, d)), const((ROUTER_ROWS, 1)), const((LANES, LANES)),
        ],
        out_specs=(
            pl.BlockSpec((1, ts, d), lambda i, j: (i, j, 0)),
            pl.BlockSpec((ts, d // 2), lambda i, j: (i * nj + j, 0)),
            pl.BlockSpec((ROUTE_ROWS, ts), lambda i, j: (0, i * nj + j)),
            pl.BlockSpec((N_EXPERTS, LANES), lambda i, j: (0, 0)),
        ),
        scratch_shapes=[pltpu.VMEM((N_EXPERTS, 1), F32)],
        compiler_params=_cparams(("arbitrary", "arbitrary")),
        name="xattn_router",
    )(x, row(g_x), qk_mem, vo_mem, row(g_f), w_r_split, b_r, upper)


def _sc_mesh():
    return plsc.VectorSubcoreMesh(core_axis_name="c", subcore_axis_name="s",
                                  num_cores=SC_CORES, num_subcores=SC_SUBCORES)


def _sc_worker():
    return lax.axis_index("s") * SC_CORES + lax.axis_index("c")


def _dispatch(h_packed, dest, n_rows):
    t, w = h_packed.shape
    chunk = SC_DISPATCH_CHUNK
    per_worker = t // SC_WORKERS
    n_chunks = per_worker // chunk
    assert n_chunks % 2 == 0 and n_chunks * chunk * SC_WORKERS == t
    dest = dest.reshape(TOP_K, SC_WORKERS * n_chunks, chunk)
    rows_buf = pltpu.VMEM((chunk, w), U32)

    @functools.partial(
        pl.kernel, mesh=_sc_mesh(),
        out_type=jax.ShapeDtypeStruct((n_rows, w), U32),
        scratch_types=[pltpu.VMEM((n_chunks, chunk), I32), pltpu.VMEM((n_chunks, chunk), I32), rows_buf, rows_buf,
                       pltpu.SemaphoreType.DMA((2,)), pltpu.SemaphoreType.DMA((2, TOP_K))],
        name="moe_dispatch_sc",
    )
    def run(h_hbm, d0_hbm, d1_hbm, xbuf_hbm, idx0_v, idx1_v, buf_a, buf_b, read_sem, write_sem):
        wid = _sc_worker()
        pltpu.sync_copy(d0_hbm.at[pl.ds(wid * n_chunks, n_chunks)], idx0_v)
        pltpu.sync_copy(d1_hbm.at[pl.ds(wid * n_chunks, n_chunks)], idx1_v)

        @pl.loop(0, n_chunks, step=2)
        def _(i):
            reads = [pltpu.async_copy(h_hbm.at[pl.ds(wid * per_worker + (i + j) * chunk, chunk)], buf, read_sem.at[j])
                     for j, buf in enumerate((buf_a, buf_b))]
            writes = []
            for j, buf in enumerate((buf_a, buf_b)):
                reads[j].wait()
                writes.append(pltpu.async_copy(buf, xbuf_hbm.at[idx0_v.at[i + j]], write_sem.at[j, 0]))
                writes.append(pltpu.async_copy(buf, xbuf_hbm.at[idx1_v.at[i + j]], write_sem.at[j, 1]))
            for copy in writes:
                copy.wait()

    return run(h_packed, dest[0], dest[1])


def _gather_pairs(y_buf, dest):
    t = dest.shape[1]
    w = y_buf.shape[1]
    chunk = SC_GATHER_CHUNK
    per_worker = t // SC_WORKERS
    n_chunks = per_worker // chunk
    assert n_chunks % 2 == 0 and n_chunks * chunk * SC_WORKERS == t
    dest = dest.reshape(TOP_K, SC_WORKERS * n_chunks, chunk)
    out = jax.ShapeDtypeStruct((t, w), U32)
    rows_buf = pltpu.VMEM((chunk, w), U32)

    @functools.partial(
        pl.kernel, mesh=_sc_mesh(),
        out_type=(out, out),
        scratch_types=[pltpu.VMEM((n_chunks, chunk), I32), pltpu.VMEM((n_chunks, chunk), I32),
                       rows_buf, rows_buf, rows_buf, rows_buf, pltpu.SemaphoreType.DMA((2, TOP_K))],
        name="moe_gather_sc",
    )
    def run(y_hbm, d0_hbm, d1_hbm, y0_hbm, y1_hbm, idx0_v, idx1_v, buf_a0, buf_a1, buf_b0, buf_b1, sem):
        wid = _sc_worker()
        pltpu.sync_copy(d0_hbm.at[pl.ds(wid * n_chunks, n_chunks)], idx0_v)
        pltpu.sync_copy(d1_hbm.at[pl.ds(wid * n_chunks, n_chunks)], idx1_v)

        @pl.loop(0, n_chunks, step=2)
        def _(i):
            bufs = ((buf_a0, buf_a1), (buf_b0, buf_b1))
            gathers = [[pltpu.async_copy(y_hbm.at[idx_v.at[i + j]], bufs[j][k], sem.at[j, k])
                        for k, idx_v in enumerate((idx0_v, idx1_v))] for j in range(2)]
            stores = []
            for j in range(2):
                rows = pl.ds(wid * per_worker + (i + j) * chunk, chunk)
                for k, out_hbm in enumerate((y0_hbm, y1_hbm)):
                    gathers[j][k].wait()
                    stores.append(pltpu.async_copy(bufs[j][k], out_hbm.at[rows], sem.at[j, k]))
            for copy in stores:
                copy.wait()

    return run(y_buf, dest[0], dest[1])


def _expert_kernel(be_ref, bf_ref, bv_ref, slot_ref, next_ref, x_ref, wg_hbm, wu_hbm, wd_hbm, y_ref,
                   wg_f32, wu_f32, wd_f32, wg_bf, wu_bf, wd_bf, sems, *, layer):
    blk = pl.program_id(0)
    valid = bv_ref[blk]

    def weight_copies(expert, slot):
        return [pltpu.make_async_copy(src.at[layer, expert], dst.at[slot], sems.at[slot, i])
                for i, (src, dst) in enumerate(((wg_hbm, wg_f32), (wu_hbm, wu_f32), (wd_hbm, wd_f32)))]

    @pl.when(blk == 0)
    def _():
        for copy in weight_copies(be_ref[0], 0):
            copy.start()

    @pl.when(bf_ref[blk] == 1)
    def _():
        slot = slot_ref[blk]
        for copy in weight_copies(be_ref[blk], slot):
            copy.wait()

        @pl.when(next_ref[blk] >= 0)
        def _():
            for copy in weight_copies(next_ref[blk], 1 - slot):
                copy.start()

        wg_bf[...] = wg_f32[slot].astype(BF16)
        wu_bf[...] = wu_f32[slot].astype(BF16)
        wd_bf[...] = wd_f32[slot].astype(BF16)

    @pl.when(valid > 0)
    def _():
        live = lax.broadcasted_iota(I32, x_ref.shape, 0) < valid
        xb = _unpack_bf16_pairs(jnp.where(live, x_ref[...], jnp.uint32(0))).astype(BF16)
        gt = _dot(xb, wg_bf[...])
        up = _dot(xb, wu_bf[...])
        act = (_silu(gt) * up).astype(BF16)
        y_ref[...] = _pack_bf16_pairs(_dot(act, wd_bf[...]))

    @pl.when(valid <= 0)
    def _():
        y_ref[...] = jnp.zeros(y_ref.shape, U32)


def _experts(x_buf, block_expert, block_first, block_valid, w_gate, w_up, w_down, layer):
    n_rows, w = x_buf.shape
    d, de = w_gate.shape[2], w_gate.shape[3]
    bm = MOE_BLOCK_ROWS
    n = n_rows // bm
    block_slot = (jnp.cumsum(block_first) - 1) % 2
    idx = jnp.arange(n, dtype=I32)
    later_first = jnp.concatenate([jnp.where(block_first[1:] == 1, idx[1:], n), jnp.full((1,), n, I32)])
    next_first = lax.cummin(later_first, reverse=True)
    block_next = jnp.where(next_first < n, block_expert[jnp.minimum(next_first, n - 1)], -1).astype(I32)
    any_space = pl.BlockSpec(memory_space=pl.ANY)
    grid_spec = pltpu.PrefetchScalarGridSpec(
        num_scalar_prefetch=5,
        grid=(n,),
        in_specs=[pl.BlockSpec((bm, w), lambda i, *_: (i, 0)), any_space, any_space, any_space],
        out_specs=pl.BlockSpec((bm, w), lambda i, *_: (i, 0)),
        scratch_shapes=[pltpu.VMEM((2, d, de), F32), pltpu.VMEM((2, d, de), F32), pltpu.VMEM((2, de, d), F32),
                        pltpu.VMEM((d, de), BF16), pltpu.VMEM((d, de), BF16), pltpu.VMEM((de, d), BF16),
                        pltpu.SemaphoreType.DMA((2, 3))],
    )
    return pl.pallas_call(
        functools.partial(_expert_kernel, layer=layer),
        out_shape=jax.ShapeDtypeStruct((n_rows, w), U32),
        grid_spec=grid_spec,
        compiler_params=_cparams(("arbitrary",)),
        name="moe_experts",
    )(block_expert, block_first, block_valid, block_slot.astype(I32), block_next, x_buf, w_gate, w_up, w_down)


def _combine_kernel(x_ref, gate_ref, y0_ref, y1_ref, gfin_ref, o_ref, *, final_norm):
    out = _moe_residual(x_ref[...], gate_ref[...], y0_ref[...], y1_ref[...])
    if final_norm:
        out = _rms(out, gfin_ref[...])
    o_ref[...] = out


def _combine(x2, y0, y1, gates, g_final, final_norm):
    t, d = x2.shape
    w = y0.shape[1]
    ts = min(TILE_COMBINE, t)
    assert t % ts == 0
    return pl.pallas_call(
        functools.partial(_combine_kernel, final_norm=final_norm),
        out_shape=jax.ShapeDtypeStruct((t, d), F32),
        grid=(t // ts,),
        in_specs=[
            pl.BlockSpec((ts, d), lambda i: (i, 0)),
            pl.BlockSpec((ROUTE_ROWS, ts), lambda i: (0, i)),
            pl.BlockSpec((ts, w), lambda i: (i, 0)),
            pl.BlockSpec((ts, w), lambda i: (i, 0)),
            pl.BlockSpec((1, d), lambda i: (0, 0)),
        ],
        out_specs=pl.BlockSpec((ts, d), lambda i: (i, 0)),
        compiler_params=_cparams(("arbitrary",)),
        name="moe_combine",
    )(x2, gates, y0, y1, g_final.reshape(1, d))


def _moe_layout(route, counts):
    bm = MOE_BLOCK_ROWS
    t = route.shape[1]
    assert (t * TOP_K) % bm == 0
    n_blocks = (t * TOP_K) // bm + N_EXPERTS
    cnt = counts[:, 0].astype(I32)
    padded = (cnt + bm - 1) // bm * bm
    pad_ends = jnp.cumsum(padded)
    pad_off = pad_ends - padded
    experts = jnp.arange(N_EXPERTS, dtype=I32)
    hit = route[0:TOP_K, :, None] == experts
    dest = jnp.sum(jnp.where(hit, pad_off, 0), axis=-1) + route[TOP_K:2 * TOP_K]
    gates = route
    starts = jnp.arange(n_blocks, dtype=I32) * bm
    block_expert = jnp.minimum(jnp.sum((pad_ends[None, :] <= starts[:, None]).astype(I32), axis=1),
                               N_EXPERTS - 1)
    block_first = jnp.concatenate([jnp.ones((1,), I32), (block_expert[1:] != block_expert[:-1]).astype(I32)])
    own = block_expert[:, None] == experts
    block_valid = jnp.clip(jnp.sum(jnp.where(own, cnt + pad_off, 0), axis=1) - starts, 0, bm)
    block_valid = jnp.where(starts < pad_ends[-1], block_valid, 0).astype(I32)
    return dest, gates, block_expert, block_first, block_valid, n_blocks * bm


def kernel(x, mem, norm_mix, norm_xattn, norm_ffn, norm_mem, norm_final, conv_w_in, conv_b_in, conv_w_dw,
           conv_b_dw, conv_ln_g, conv_ln_b, conv_w_out, conv_b_out, gla_w_in, gla_w_a2, gla_b_a, gla_norm_g,
           gla_w_o, xa_w_q, xa_w_kv, xa_w_o, moe_w_grp, moe_b_grp, moe_w_exp, moe_b_exp, moe_w_gate, moe_w_up,
           moe_w_down):
    b, s, d = x.shape
    depth = norm_mix.shape[0]
    moe = None
    for i in range(depth):
        j = i // 2
        if i % 2 == 0:
            if moe is not None:
                x = _combine(x.reshape(b * s, d), moe[1], moe[2], moe[0], norm_final, False).reshape(b, s, d)
            x = _conv_mixer(x, norm_mix[i], conv_w_in[j], conv_b_in[j], conv_w_dw[j], conv_b_dw[j],
                            conv_ln_g[j], conv_ln_b[j], conv_w_out[j], conv_b_out[j])
        else:
            x = _gla_mixer(x, *moe, norm_mix[i], gla_w_in[j], gla_w_a2[j], gla_b_a[j], gla_norm_g[j], gla_w_o[j])
        qk_mem, vo_mem = _mem_kv(mem, norm_mem, xa_w_kv, xa_w_q, xa_w_o, i)
        x2, h_packed, route, counts = _xattn_router(
            x, norm_xattn[i], qk_mem, vo_mem, norm_ffn[i],
            moe_w_grp[i], moe_b_grp[i], moe_w_exp[i], moe_b_exp[i])
        dest, gates, block_expert, block_first, block_valid, n_rows = _moe_layout(route, counts)
        x_buf = _dispatch(h_packed, dest, n_rows)
        y_buf = _experts(x_buf, block_expert, block_first, block_valid, moe_w_gate, moe_w_up, moe_w_down, i)
        y0, y1 = _gather_pairs(y_buf, dest)
        x, moe = x2, (gates, y0, y1)
    return _combine(x.reshape(b * s, d), moe[1], moe[2], moe[0], norm_final, True).reshape(b, s, d)
```

```python
import functools

import jax
import jax.numpy as jnp
from jax import lax
from jax.experimental import pallas as pl
from jax.experimental.pallas import tpu as pltpu
from jax.experimental.pallas import tpu_sc as plsc

F32 = jnp.float32
BF16 = jnp.bfloat16
I32 = jnp.int32
U32 = jnp.uint32

EPS = 1e-6
CONV_KERNEL = 31
CONV_CARRY = 32
CONV_ROWS = 64
CONV_COLS = 256
SUBLANES = 8
LANES = 128
GLA_HEADS = 4
GLA_RANK = 16
GLA_RANK_PAD = 128
GLA_TAU = 16.0
GLA_LEAF = 32
XATTN_HEADS = 4
N_GROUPS = 4
EXPERTS_PER_GROUP = 8
N_EXPERTS = N_GROUPS * EXPERTS_PER_GROUP
ROUTER_ROWS = 40
TOP_K = 2
ROUTE_ROWS = 8

TILE_CONV = 512
TILE_GLA = 256
TILE_XATTN = 1024
TILE_COMBINE = 512
SC_CORES = 2
SC_SUBCORES = 16
SC_WORKERS = SC_CORES * SC_SUBCORES
SC_DISPATCH_CHUNK = 64
SC_GATHER_CHUNK = 32
MOE_BLOCK_ROWS = 512
VMEM_LIMIT = 56 * 1024 * 1024


def _cparams(sem):
    return pltpu.CompilerParams(dimension_semantics=sem, vmem_limit_bytes=VMEM_LIMIT)


def _rms(x, g):
    return x * lax.rsqrt(jnp.mean(x * x, axis=-1, keepdims=True) + EPS) * g


def _sigmoid(x):
    return 0.5 * jnp.tanh(0.5 * x) + 0.5


def _silu(x):
    h = 0.5 * x
    return h + h * jnp.tanh(h)


def _dot(a, b):
    return jnp.dot(a, b, preferred_element_type=F32)


def _dot_nt(a, b):
    return lax.dot_general(a, b, (((1,), (1,)), ((), ())), preferred_element_type=F32)


def _dot_tn(a, b):
    return lax.dot_general(a, b, (((0,), (0,)), ((), ())), preferred_element_type=F32)


def _pack_bf16_pairs(x):
    w = x.shape[1] // 2
    hi = lax.bitcast_convert_type(x[:, :w].astype(BF16).astype(F32), U32)
    lo = lax.bitcast_convert_type(x[:, w:].astype(BF16).astype(F32), U32)
    return hi | (lo >> 16)


def _unpack_bf16_pairs(p):
    hi = lax.bitcast_convert_type(p & jnp.uint32(0xFFFF0000), F32)
    lo = lax.bitcast_convert_type(p << 16, F32)
    return jnp.concatenate([hi, lo], axis=1)


def _memkv_kernel(mem_ref, g_ref, wkv_ref, wq_ref, wo_ref, qk_ref, vo_ref, wkv_bf, wq_bf, wo_bf):
    nm, d = mem_ref.shape[1], mem_ref.shape[2]
    hd = d // XATTN_HEADS

    @pl.when(pl.program_id(1) == 0)
    def _():
        wkv_bf[...] = wkv_ref[0].astype(BF16)
        wq_bf[...] = wq_ref[0].astype(BF16)
        wo_bf[...] = wo_ref[0].astype(BF16)

    mn = _rms(mem_ref[0], g_ref[...]).astype(BF16)
    kv = _dot(mn, wkv_bf[...])
    k = kv[:, :d].astype(BF16)
    v = kv[:, d:].astype(BF16)
    for a in range(XATTN_HEADS):
        sl = slice(a * hd, (a + 1) * hd)
        qk_ref[0, 0, :, a * nm:(a + 1) * nm] = (_dot_nt(wq_bf[:, sl], k[:, sl]) * (hd ** -0.5)).astype(BF16)
        vo_ref[0, 0, a * nm:(a + 1) * nm, :] = _dot(v[:, sl], wo_bf[sl, :]).astype(BF16)


def _mem_kv(mem, norm_mem, w_kv, w_q, w_o):
    b, nm, d = mem.shape
    depth = w_kv.shape[0]
    per_layer = lambda shape: pl.BlockSpec((1,) + shape, lambda l, i: (l, 0, 0))
    return pl.pallas_call(
        _memkv_kernel,
        out_shape=(jax.ShapeDtypeStruct((depth, b, d, XATTN_HEADS * nm), BF16),
                   jax.ShapeDtypeStruct((depth, b, XATTN_HEADS * nm, d), BF16)),
        grid=(depth, b),
        in_specs=[
            pl.BlockSpec((1, nm, d), lambda l, i: (i, 0, 0)),
            pl.BlockSpec((1, d), lambda l, i: (0, 0)),
            per_layer((d, 2 * d)), per_layer((d, d)), per_layer((d, d)),
        ],
        out_specs=(
            pl.BlockSpec((1, 1, d, XATTN_HEADS * nm), lambda l, i: (l, i, 0, 0)),
            pl.BlockSpec((1, 1, XATTN_HEADS * nm, d), lambda l, i: (l, i, 0, 0)),
        ),
        scratch_shapes=[pltpu.VMEM((d, 2 * d), BF16), pltpu.VMEM((d, d), BF16), pltpu.VMEM((d, d), BF16)],
        compiler_params=_cparams(("arbitrary", "arbitrary")),
        name="mem_kv",
    )(mem, norm_mem.reshape(1, d), w_kv, w_q, w_o)


def _conv_kernel(x_ref, g_ref, win_ref, bin_ref, wdw_ref, bdw_ref, lng_ref, lnb_ref, wout_ref, bout_ref,
                 o_ref, ext_ref, even_ref, odd_ref):
    ts, d = x_ref.shape[1], x_ref.shape[2]
    n_shifts = ext_ref.shape[0]

    @pl.when(pl.program_id(1) == 0)
    def _():
        ext_ref[...] = jnp.zeros(ext_ref.shape, F32)

    x = x_ref[0]
    h = _rms(x, g_ref[...]).astype(BF16)
    u = _dot(h, win_ref[...]) + bin_ref[...]
    glu = u[:, :d] * _sigmoid(u[:, d:])
    for c in range(n_shifts):
        ext_ref[c, CONV_CARRY - 2 * c:CONV_CARRY - 2 * c + ts, :] = glu

    first = CONV_CARRY - (CONV_KERNEL - 1)

    def taps(r0, n_rows, parity, dst_ref):
        for c0 in range(0, d, CONV_COLS):
            cols = slice(c0, c0 + CONV_COLS)
            acc = [jnp.zeros((SUBLANES, CONV_COLS), F32) for _ in range(n_rows // SUBLANES)]
            for k in range(CONV_KERNEL):
                if (first + k) % 2 != parity:
                    continue
                shift = (first + k - parity) % SUBLANES
                base = first + k - parity - shift
                wk = wdw_ref[k * SUBLANES:(k + 1) * SUBLANES, cols]
                for j in range(n_rows // SUBLANES):
                    rows = pl.ds(r0 + base + j * SUBLANES, SUBLANES)
                    acc[j] = acc[j] + ext_ref[shift // 2, rows, cols] * wk
            for j in range(n_rows // SUBLANES):
                dst_ref[pl.ds(r0 + j * SUBLANES, SUBLANES), cols] = acc[j]

    def chunk(i, carry):
        r0 = pl.multiple_of(i * CONV_ROWS, CONV_ROWS)
        taps(r0, CONV_ROWS, 0, even_ref)
        taps(r0, CONV_ROWS, 1, odd_ref)
        return carry

    lax.fori_loop(0, ts // CONV_ROWS, chunk, 0)
    taps(ts, SUBLANES, 1, odd_ref)
    for c in range(n_shifts):
        ext_ref[c, 0:CONV_CARRY, :] = ext_ref[c, ts:ts + CONV_CARRY, :]

    c = even_ref[...] + odd_ref[1:ts + 1, :] + bdw_ref[...]
    mu = jnp.mean(c, axis=-1, keepdims=True)
    cc = c - mu
    var = jnp.mean(cc * cc, axis=-1, keepdims=True)
    un = cc * lax.rsqrt(var + EPS) * lng_ref[...] + lnb_ref[...]
    act = _silu(un).astype(BF16)
    o_ref[0] = x + _dot(act, wout_ref[...]) + bout_ref[...]


def _conv_mixer(x, g, w_in, b_in, w_dw, b_dw, ln_g, ln_b, w_out, b_out):
    b, s, d = x.shape
    ts = min(TILE_CONV, s)
    assert s % ts == 0 and ts % CONV_ROWS == 0 and d % CONV_COLS == 0
    row = lambda v: v.reshape(1, -1)
    const = lambda shape: pl.BlockSpec(shape, lambda i, j: (0,) * len(shape))
    return pl.pallas_call(
        _conv_kernel,
        out_shape=jax.ShapeDtypeStruct(x.shape, F32),
        grid=(b, s // ts),
        in_specs=[
            pl.BlockSpec((1, ts, d), lambda i, j: (i, j, 0)),
            const((1, d)), const((d, 2 * d)), const((1, 2 * d)), const((CONV_KERNEL * SUBLANES, d)), const((1, d)),
            const((1, d)), const((1, d)), const((d, d)), const((1, d)),
        ],
        out_specs=pl.BlockSpec((1, ts, d), lambda i, j: (i, j, 0)),
        scratch_shapes=[pltpu.VMEM((SUBLANES // 2, CONV_CARRY + ts, d), F32), pltpu.VMEM((ts, d), F32),
                        pltpu.VMEM((ts + SUBLANES, d), F32)],
        compiler_params=_cparams(("arbitrary", "arbitrary")),
        name="conv_mixer",
    )(x, row(g), w_in.astype(BF16), row(b_in), jnp.repeat(w_dw, SUBLANES, axis=0), row(b_dw), row(ln_g), row(ln_b),
      w_out.astype(BF16), row(b_out))


def _gla_levels(ts):
    sizes = [GLA_LEAF]
    while sizes[-1] < ts:
        sizes.append(sizes[-1] * 2)
    return sizes


def _moe_residual(x, route, y0, y1):
    g = lax.bitcast_convert_type(route, F32)
    g = jnp.concatenate([g, jnp.zeros((LANES - g.shape[0], g.shape[1]), F32)], axis=0).T
    gate0 = g[:, 2 * TOP_K:2 * TOP_K + 1]
    gate1 = g[:, 2 * TOP_K + 1:2 * TOP_K + 2]
    return x + _unpack_bf16_pairs(y0) * gate0 + _unpack_bf16_pairs(y1) * gate1


def _gla_kernel(x_ref, gate_ref, y0_ref, y1_ref, g_ref, wq_ref, wk_ref, wv_ref, wa_ref, wr_ref, wa2_ref, ba_ref,
                ng_ref, wo_ref, o_ref, state_ref, x_s, q_s, k_s, v_s, og_s, la_s, *, tiles_per_seq):
    j = pl.program_id(0)
    dkh = wq_ref.shape[1] // GLA_HEADS

    @pl.when(j == 0)
    def _():
        for ref in (x_s, q_s, k_s, og_s, la_s):
            ref[...] = jnp.zeros(ref.shape, F32)
        v_s[...] = jnp.zeros(v_s.shape, BF16)

    @pl.when((j == 0) | ((j - 1) % tiles_per_seq == 0))
    def _():
        state_ref[...] = jnp.zeros(state_ref.shape, F32)

    def stage_a(slot):
        x = _moe_residual(x_ref[0], gate_ref[...], y0_ref[...], y1_ref[...])
        h = _rms(x, g_ref[...]).astype(BF16)
        x_s[slot] = x
        q_s[slot] = _dot(h, wq_ref[...]) * (dkh ** -0.5)
        k_s[slot] = _dot(h, wk_ref[...])
        v_s[slot] = _dot(h, wv_ref[...]).astype(BF16)
        r = _dot(h, wr_ref[...])
        og_s[slot] = _silu(r)
        a = _dot(h, wa_ref[...]).astype(BF16)
        z = _dot(jnp.concatenate([a, a], axis=1), wa2_ref[...]) + ba_ref[...]
        la_s[slot] = -(jnp.maximum(-z, 0.0) + jnp.log(1.0 + jnp.exp(-jnp.abs(z)))) * (1.0 / GLA_TAU)

    def stage_b(slot):
        _gla_recurrence(x_s[slot], q_s[slot], k_s[slot], v_s[slot], og_s[slot], la_s[slot],
                        ng_ref, wo_ref, o_ref, state_ref)

    @pl.when(j % 2 == 0)
    def _():
        stage_b(1)
        stage_a(0)

    @pl.when(j % 2 == 1)
    def _():
        stage_b(0)
        stage_a(1)


def _gla_recurrence(x, q, k, v, out_gate, log_a, ng_ref, wo_ref, o_ref, state_ref):
    ts, d = x.shape
    dk = q.shape[1]
    dkh = dk // GLA_HEADS
    dvh = d // GLA_HEADS

    row = lax.broadcasted_iota(I32, (ts, ts), 0)
    col = lax.broadcasted_iota(I32, (ts, ts), 1)
    tri = jnp.where(col <= row, 1.0, 0.0).astype(BF16)
    bcum = _dot(tri, log_a.astype(BF16))
    b_last = bcum[ts - 1:ts, :]

    q_in = (q * jnp.exp(bcum)).astype(BF16)
    k_out = (k * jnp.exp(b_last - bcum)).astype(BF16)

    sizes = _gla_levels(ts)
    scores = [None] * GLA_HEADS
    for lvl, size in enumerate(sizes):
        half = size // 2
        same_block = (row & -size) == (col & -size)
        if lvl == 0:
            pair = same_block & (col <= row)
            q_ok = k_ok = None
        else:
            pair = same_block & ((row & (size - 1)) >= half) & ((col & (size - 1)) < half)
            pos = lax.broadcasted_iota(I32, (ts, dk), 0) & (size - 1)
            q_ok = pos >= half
            k_ok = pos < half
        ref = jnp.concatenate(
            [jnp.broadcast_to(bcum[r0 + half:r0 + half + 1, :], (size, dk)) for r0 in range(0, ts, size)], axis=0)
        ql = q * jnp.exp(bcum - ref)
        kl = k * jnp.exp(ref - bcum)
        if q_ok is not None:
            ql = jnp.where(q_ok, ql, 0.0)
            kl = jnp.where(k_ok, kl, 0.0)
        ql = ql.astype(BF16)
        kl = kl.astype(BF16)
        for hd in range(GLA_HEADS):
            c0 = hd * dkh
            a = _dot_nt(ql[:, c0:c0 + dkh], kl[:, c0:c0 + dkh])
            a = jnp.where(pair, a, 0.0)
            scores[hd] = a if scores[hd] is None else scores[hd] + a

    outs = []
    for hd in range(GLA_HEADS):
        c0 = hd * dkh
        v_h = v[:, hd * dvh:(hd + 1) * dvh]
        st = state_ref[hd]
        o_h = _dot(scores[hd].astype(BF16), v_h) + _dot_nt(q_in[:, c0:c0 + dkh], st.astype(BF16))
        decay = jnp.exp(b_last[:, c0:c0 + dkh])
        state_ref[hd] = st * decay + _dot_tn(v_h, k_out[:, c0:c0 + dkh])
        o_h = o_h * lax.rsqrt(jnp.mean(o_h * o_h, axis=-1, keepdims=True) + EPS) * ng_ref[...]
        outs.append(o_h)
    o = jnp.concatenate(outs, axis=1) * out_gate
    o_ref[0] = x + _dot(o.astype(BF16), wo_ref[...])


def _gla_mixer(x, gates, y_buf, dest, g, w_in, w_a2, b_a, norm_g, w_o):
    b, s, d = x.shape
    dk = w_a2.shape[1]
    ts = min(TILE_GLA, s)
    assert s % ts == 0 and ts % GLA_LEAF == 0
    nj = s // ts
    row = lambda v: v.reshape(1, -1)
    const = lambda shape: pl.BlockSpec(shape, lambda j: (0,) * len(shape))
    wq = w_in[:, :dk].astype(BF16)
    wk = w_in[:, dk:2 * dk].astype(BF16)
    wv = w_in[:, 2 * dk:2 * dk + d].astype(BF16)
    wa = jnp.pad(w_in[:, 2 * dk + d:2 * dk + d + GLA_RANK], ((0, 0), (0, GLA_RANK_PAD - GLA_RANK))).astype(BF16)
    wr = w_in[:, 2 * dk + d + GLA_RANK:].astype(BF16)
    wa2 = jnp.pad(w_a2, ((0, GLA_RANK_PAD - GLA_RANK), (0, 0)))
    wa2_hi = wa2.astype(BF16)
    wa2_lo = (wa2 - wa2_hi.astype(F32)).astype(BF16)
    wa2_split = jnp.concatenate([wa2_hi, wa2_lo], axis=0)
    dvh = d // GLA_HEADS
    weights = (row(g), wq, wk, wv, wa, wr, wa2_split, row(b_a), row(norm_g), w_o.astype(BF16))

    def tile(j, lag):
        return jnp.clip(j - lag, 0, nj - 1)

    def one_sequence(x, seq, y0, y1):
        return pl.pallas_call(
            functools.partial(_gla_kernel, tiles_per_seq=nj),
            out_shape=jax.ShapeDtypeStruct(x.shape, F32),
            grid=(nj + 1,),
            in_specs=[
                pl.BlockSpec((1, ts, d), lambda j: (seq, tile(j, 0), 0)),
                pl.BlockSpec((ROUTE_ROWS, ts), lambda j: (0, seq * nj + tile(j, 0))),
                pl.BlockSpec((ts, d // 2), lambda j: (tile(j, 0), 0)),
                pl.BlockSpec((ts, d // 2), lambda j: (tile(j, 0), 0)),
                const((1, d)), const((d, dk)), const((d, dk)), const((d, d)), const((d, GLA_RANK_PAD)),
                const((d, d)), const((2 * GLA_RANK_PAD, dk)), const((1, dk)), const((1, dvh)), const((d, d)),
            ],
            out_specs=pl.BlockSpec((1, ts, d), lambda j: (seq, tile(j, 1), 0)),
            scratch_shapes=[pltpu.VMEM((GLA_HEADS, dvh, dk // GLA_HEADS), F32),
                            pltpu.VMEM((2, ts, d), F32), pltpu.VMEM((2, ts, dk), F32), pltpu.VMEM((2, ts, dk), F32),
                            pltpu.VMEM((2, ts, d), BF16), pltpu.VMEM((2, ts, d), F32), pltpu.VMEM((2, ts, dk), F32)],
            input_output_aliases={0: 0},
            compiler_params=_cparams(("arbitrary",)),
            name="gla_mixer",
        )(x, gates, y0, y1, *weights)

    for seq in range(b):
        y0, y1 = _gather_pairs(y_buf, dest[:, seq * s:(seq + 1) * s])
        x = one_sequence(x, seq, y0, y1)
    return x


def _xattn_router_kernel(x_ref, gx_ref, qk_ref, vo_ref, gf_ref, wr_ref, br_ref, upper_ref,
                         x_out_ref, h_out_ref, route_ref, cnt_ref, carry_ref):
    ts, d = x_ref.shape[1], x_ref.shape[2]
    nm = qk_ref.shape[3] // XATTN_HEADS
    first = (pl.program_id(0) == 0) & (pl.program_id(1) == 0)

    @pl.when(first)
    def _():
        carry_ref[...] = jnp.zeros(carry_ref.shape, F32)

    x = x_ref[0]
    h = _rms(x, gx_ref[...]).astype(BF16)
    scores = _dot(h, qk_ref[0, 0])
    probs = []
    for a in range(XATTN_HEADS):
        s = scores[:, a * nm:(a + 1) * nm]
        p = jnp.exp(s - jnp.max(s, axis=-1, keepdims=True))
        probs.append((p / jnp.sum(p, axis=-1, keepdims=True)).astype(BF16))
    x2 = x + _dot(jnp.concatenate(probs, axis=1), vo_ref[0, 0])
    x_out_ref[0] = x2

    hf = _rms(x2, gf_ref[...])
    h_out_ref[...] = _pack_bf16_pairs(hf)

    both = _dot_nt(wr_ref[...], hf.astype(BF16))
    logits = both[0:ROUTER_ROWS, :] + both[ROUTER_ROWS:2 * ROUTER_ROWS, :] + br_ref[...]
    gl = logits[N_EXPERTS:N_EXPERTS + N_GROUPS, :]
    gi = lax.broadcasted_iota(I32, gl.shape, 0).astype(F32)
    gmax = jnp.max(gl, axis=0, keepdims=True)
    g_sel = jnp.min(jnp.where(gl == gmax, gi, float(N_GROUPS)), axis=0, keepdims=True)
    pg_sel = 1.0 / jnp.sum(jnp.exp(gl - gmax), axis=0, keepdims=True)

    el = jnp.zeros((EXPERTS_PER_GROUP, ts), F32)
    for gidx in range(N_GROUPS):
        lo = gidx * EXPERTS_PER_GROUP
        el = jnp.where(g_sel == float(gidx), logits[lo:lo + EXPERTS_PER_GROUP, :], el)
    ei = lax.broadcasted_iota(I32, el.shape, 0).astype(F32)
    m1 = jnp.max(el, axis=0, keepdims=True)
    i1 = jnp.min(jnp.where(el == m1, ei, float(EXPERTS_PER_GROUP)), axis=0, keepdims=True)
    rest = jnp.where(ei == i1, -jnp.inf, el)
    m2 = jnp.max(rest, axis=0, keepdims=True)
    i2 = jnp.min(jnp.where(rest == m2, ei, float(EXPERTS_PER_GROUP)), axis=0, keepdims=True)
    ratio = jnp.exp(m2 - m1)
    gate1 = pg_sel / (1.0 + ratio)
    gate2 = pg_sel * ratio / (1.0 + ratio)
    e1 = g_sel * float(EXPERTS_PER_GROUP) + i1
    e2 = g_sel * float(EXPERTS_PER_GROUP) + i2

    xi = lax.broadcasted_iota(I32, (N_EXPERTS, ts), 0).astype(F32)
    oh1 = jnp.where(xi == e1, 1.0, 0.0)
    oh2 = jnp.where(xi == e2, 1.0, 0.0)
    oh = oh1 + oh2
    n_blk = ts // LANES
    stacked = jnp.concatenate([oh[:, c * LANES:(c + 1) * LANES] for c in range(n_blk)], axis=0)
    within = _dot(stacked.astype(BF16), upper_ref[...])
    totals = jnp.sum(stacked, axis=1, keepdims=True)
    run = carry_ref[...]
    before = []
    for c in range(n_blk):
        before.append(within[c * N_EXPERTS:(c + 1) * N_EXPERTS, :] + run)
        run = run + totals[c * N_EXPERTS:(c + 1) * N_EXPERTS, :]
    before = jnp.concatenate(before, axis=1)
    rank1 = jnp.sum(oh1 * before, axis=0, keepdims=True)
    rank2 = jnp.sum(oh2 * before, axis=0, keepdims=True)
    carry_ref[...] = run
    cnt_ref[...] = jnp.broadcast_to(run, cnt_ref.shape)

    route_ref[0:1, :] = e1.astype(I32)
    route_ref[1:2, :] = e2.astype(I32)
    route_ref[2:3, :] = rank1.astype(I32)
    route_ref[3:4, :] = rank2.astype(I32)
    route_ref[4:5, :] = lax.bitcast_convert_type(gate1, I32)
    route_ref[5:6, :] = lax.bitcast_convert_type(gate2, I32)
    route_ref[3 * TOP_K:ROUTE_ROWS, :] = jnp.zeros((ROUTE_ROWS - 3 * TOP_K, ts), I32)


def _xattn_router(x, g_x, qk_mem, vo_mem, layer, g_f, w_grp, b_grp, w_exp, b_exp):
    b, s, d = x.shape
    hm = qk_mem.shape[3]
    t = b * s
    ts = min(TILE_XATTN, s)
    assert s % ts == 0 and ts % LANES == 0
    nj = s // ts
    row = lambda v: v.reshape(1, -1)
    const = lambda shape: pl.BlockSpec(shape, lambda i, j: (0,) * len(shape))
    pad = ROUTER_ROWS - N_GROUPS - N_EXPERTS
    w_r = jnp.pad(jnp.concatenate([w_exp, w_grp], axis=1).T, ((0, pad), (0, 0)))
    w_r_hi = w_r.astype(BF16)
    w_r_split = jnp.concatenate([w_r_hi, (w_r - w_r_hi.astype(F32)).astype(BF16)], axis=0)
    b_r = jnp.pad(jnp.concatenate([b_exp, b_grp]), (0, pad)).reshape(ROUTER_ROWS, 1)
    ti = jnp.arange(LANES)
    upper = (ti[:, None] < ti[None, :]).astype(BF16)
    return pl.pallas_call(
        _xattn_router_kernel,
        out_shape=(
            jax.ShapeDtypeStruct(x.shape, F32),
            jax.ShapeDtypeStruct((t, d // 2), U32),
            jax.ShapeDtypeStruct((ROUTE_ROWS, t), I32),
            jax.ShapeDtypeStruct((N_EXPERTS, LANES), F32),
        ),
        grid=(b, nj),
        in_specs=[
            pl.BlockSpec((1, ts, d), lambda i, j: (i, j, 0)),
            const((1, d)),
            pl.BlockSpec((1, 1, d, hm), lambda i, j: (layer, i, 0, 0)),
            pl.BlockSpec((1, 1, hm, d), lambda i, j: (layer, i, 0, 0)),
            const((1, d)), const((2 * ROUTER_ROWS, d)), const((ROUTER_ROWS, 1)), const((LANES, LANES)),
        ],
        out_specs=(
            pl.BlockSpec((1, ts, d), lambda i, j: (i, j, 0)),
            pl.BlockSpec((ts, d // 2), lambda i, j: (i * nj + j, 0)),
            pl.BlockSpec((ROUTE_ROWS, ts), lambda i, j: (0, i * nj + j)),
            pl.BlockSpec((N_EXPERTS, LANES), lambda i, j: (0, 0)),
        ),
        scratch_shapes=[pltpu.VMEM((N_EXPERTS, 1), F32)],
        compiler_params=_cparams(("arbitrary", "arbitrary")),
        name="xattn_router",
    )(x, row(g_x), qk_mem, vo_mem, row(g_f), w_r_split, b_r, upper)


def _sc_mesh():
    return plsc.VectorSubcoreMesh(core_axis_name="c", subcore_axis_name="s",
                                  num_cores=SC_CORES, num_subcores=SC_SUBCORES)


def _sc_worker():
    return lax.axis_index("s") * SC_CORES + lax.axis_index("c")


def _dispatch(h_packed, dest, n_rows):
    t, w = h_packed.shape
    chunk = SC_DISPATCH_CHUNK
    per_worker = t // SC_WORKERS
    n_chunks = per_worker // chunk
    assert n_chunks % 2 == 0 and n_chunks * chunk * SC_WORKERS == t
    dest = dest.reshape(TOP_K, SC_WORKERS * n_chunks, chunk)
    rows_buf = pltpu.VMEM((chunk, w), U32)

    @functools.partial(
        pl.kernel, mesh=_sc_mesh(),
        out_type=jax.ShapeDtypeStruct((n_rows, w), U32),
        scratch_types=[pltpu.VMEM((n_chunks, chunk), I32), pltpu.VMEM((n_chunks, chunk), I32), rows_buf, rows_buf,
                       pltpu.SemaphoreType.DMA((2,)), pltpu.SemaphoreType.DMA((2, TOP_K))],
        name="moe_dispatch_sc",
    )
    def run(h_hbm, d0_hbm, d1_hbm, xbuf_hbm, idx0_v, idx1_v, buf_a, buf_b, read_sem, write_sem):
        wid = _sc_worker()
        pltpu.sync_copy(d0_hbm.at[pl.ds(wid * n_chunks, n_chunks)], idx0_v)
        pltpu.sync_copy(d1_hbm.at[pl.ds(wid * n_chunks, n_chunks)], idx1_v)

        @pl.loop(0, n_chunks, step=2)
        def _(i):
            reads = [pltpu.async_copy(h_hbm.at[pl.ds(wid * per_worker + (i + j) * chunk, chunk)], buf, read_sem.at[j])
                     for j, buf in enumerate((buf_a, buf_b))]
            writes = []
            for j, buf in enumerate((buf_a, buf_b)):
                reads[j].wait()
                writes.append(pltpu.async_copy(buf, xbuf_hbm.at[idx0_v.at[i + j]], write_sem.at[j, 0]))
                writes.append(pltpu.async_copy(buf, xbuf_hbm.at[idx1_v.at[i + j]], write_sem.at[j, 1]))
            for copy in writes:
                copy.wait()

    return run(h_packed, dest[0], dest[1])


def _gather_pairs(y_buf, dest):
    t = dest.shape[1]
    w = y_buf.shape[1]
    chunk = SC_GATHER_CHUNK
    per_worker = t // SC_WORKERS
    n_chunks = per_worker // chunk
    assert n_chunks % 2 == 0 and n_chunks * chunk * SC_WORKERS == t
    dest = dest.reshape(TOP_K, SC_WORKERS * n_chunks, chunk)
    out = jax.ShapeDtypeStruct((t, w), U32)
    rows_buf = pltpu.VMEM((chunk, w), U32)

    @functools.partial(
        pl.kernel, mesh=_sc_mesh(),
        out_type=(out, out),
        scratch_types=[pltpu.VMEM((n_chunks, chunk), I32), pltpu.VMEM((n_chunks, chunk), I32),
                       rows_buf, rows_buf, rows_buf, rows_buf, pltpu.SemaphoreType.DMA((2, TOP_K))],
        name="moe_gather_sc",
    )
    def run(y_hbm, d0_hbm, d1_hbm, y0_hbm, y1_hbm, idx0_v, idx1_v, buf_a0, buf_a1, buf_b0, buf_b1, sem):
        wid = _sc_worker()
        pltpu.sync_copy(d0_hbm.at[pl.ds(wid * n_chunks, n_chunks)], idx0_v)
        pltpu.sync_copy(d1_hbm.at[pl.ds(wid * n_chunks, n_chunks)], idx1_v)

        @pl.loop(0, n_chunks, step=2)
        def _(i):
            bufs = ((buf_a0, buf_a1), (buf_b0, buf_b1))
            gathers = [[pltpu.async_copy(y_hbm.at[idx_v.at[i + j]], bufs[j][k], sem.at[j, k])
                        for k, idx_v in enumerate((idx0_v, idx1_v))] for j in range(2)]
            stores = []
            for j in range(2):
                rows = pl.ds(wid * per_worker + (i + j) * chunk, chunk)
                for k, out_hbm in enumerate((y0_hbm, y1_hbm)):
                    gathers[j][k].wait()
                    stores.append(pltpu.async_copy(bufs[j][k], out_hbm.at[rows], sem.at[j, k]))
            for copy in stores:
                copy.wait()

    return run(y_buf, dest[0], dest[1])


def _expert_kernel(be_ref, bf_ref, bv_ref, slot_ref, next_ref, x_ref, wg_hbm, wu_hbm, wd_hbm, y_ref,
                   wg_f32, wu_f32, wd_f32, wg_bf, wu_bf, wd_bf, sems, *, layer):
    blk = pl.program_id(0)
    valid = bv_ref[blk]

    def weight_copies(expert, slot):
        return [pltpu.make_async_copy(src.at[layer, expert], dst.at[slot], sems.at[slot, i])
                for i, (src, dst) in enumerate(((wg_hbm, wg_f32), (wu_hbm, wu_f32), (wd_hbm, wd_f32)))]

    @pl.when(blk == 0)
    def _():
        for copy in weight_copies(be_ref[0], 0):
            copy.start()

    @pl.when(bf_ref[blk] == 1)
    def _():
        slot = slot_ref[blk]
        for copy in weight_copies(be_ref[blk], slot):
            copy.wait()

        @pl.when(next_ref[blk] >= 0)
        def _():
            for copy in weight_copies(next_ref[blk], 1 - slot):
                copy.start()

        wg_bf[...] = wg_f32[slot].astype(BF16)
        wu_bf[...] = wu_f32[slot].astype(BF16)
        wd_bf[...] = wd_f32[slot].astype(BF16)

    @pl.when(valid > 0)
    def _():
        live = lax.broadcasted_iota(I32, x_ref.shape, 0) < valid
        xb = _unpack_bf16_pairs(jnp.where(live, x_ref[...], jnp.uint32(0))).astype(BF16)
        gt = _dot(xb, wg_bf[...])
        up = _dot(xb, wu_bf[...])
        act = (_silu(gt) * up).astype(BF16)
        y_ref[...] = _pack_bf16_pairs(_dot(act, wd_bf[...]))

    @pl.when(valid <= 0)
    def _():
        y_ref[...] = jnp.zeros(y_ref.shape, U32)


def _experts(x_buf, block_expert, block_first, block_valid, w_gate, w_up, w_down, layer):
    n_rows, w = x_buf.shape
    d, de = w_gate.shape[2], w_gate.shape[3]
    bm = MOE_BLOCK_ROWS
    n = n_rows // bm
    block_slot = (jnp.cumsum(block_first) - 1) % 2
    idx = jnp.arange(n, dtype=I32)
    later_first = jnp.concatenate([jnp.where(block_first[1:] == 1, idx[1:], n), jnp.full((1,), n, I32)])
    next_first = lax.cummin(later_first, reverse=True)
    block_next = jnp.where(next_first < n, block_expert[jnp.minimum(next_first, n - 1)], -1).astype(I32)
    any_space = pl.BlockSpec(memory_space=pl.ANY)
    grid_spec = pltpu.PrefetchScalarGridSpec(
        num_scalar_prefetch=5,
        grid=(n,),
        in_specs=[pl.BlockSpec((bm, w), lambda i, *_: (i, 0)), any_space, any_space, any_space],
        out_specs=pl.BlockSpec((bm, w), lambda i, *_: (i, 0)),
        scratch_shapes=[pltpu.VMEM((2, d, de), F32), pltpu.VMEM((2, d, de), F32), pltpu.VMEM((2, de, d), F32),
                        pltpu.VMEM((d, de), BF16), pltpu.VMEM((d, de), BF16), pltpu.VMEM((de, d), BF16),
                        pltpu.SemaphoreType.DMA((2, 3))],
    )
    return pl.pallas_call(
        functools.partial(_expert_kernel, layer=layer),
        out_shape=jax.ShapeDtypeStruct((n_rows, w), U32),
        grid_spec=grid_spec,
        compiler_params=_cparams(("arbitrary",)),
        name="moe_experts",
    )(block_expert, block_first, block_valid, block_slot.astype(I32), block_next, x_buf, w_gate, w_up, w_down)


def _combine_kernel(x_ref, gate_ref, y0_ref, y1_ref, gfin_ref, o_ref, *, final_norm):
    out = _moe_residual(x_ref[...], gate_ref[...], y0_ref[...], y1_ref[...])
    if final_norm:
        out = _rms(out, gfin_ref[...])
    o_ref[...] = out


def _combine(x2, y0, y1, gates, g_final, final_norm):
    t, d = x2.shape
    w = y0.shape[1]
    ts = min(TILE_COMBINE, t)
    assert t % ts == 0
    return pl.pallas_call(
        functools.partial(_combine_kernel, final_norm=final_norm),
        out_shape=jax.ShapeDtypeStruct((t, d), F32),
        grid=(t // ts,),
        in_specs=[
            pl.BlockSpec((ts, d), lambda i: (i, 0)),
            pl.BlockSpec((ROUTE_ROWS, ts), lambda i: (0, i)),
            pl.BlockSpec((ts, w), lambda i: (i, 0)),
            pl.BlockSpec((ts, w), lambda i: (i, 0)),
            pl.BlockSpec((1, d), lambda i: (0, 0)),
        ],
        out_specs=pl.BlockSpec((ts, d), lambda i: (i, 0)),
        compiler_params=_cparams(("arbitrary",)),
        name="moe_combine",
    )(x2, gates, y0, y1, g_final.reshape(1, d))


def _moe_layout(route, counts):
    bm = MOE_BLOCK_ROWS
    t = route.shape[1]
    assert (t * TOP_K) % bm == 0
    n_blocks = (t * TOP_K) // bm + N_EXPERTS
    cnt = counts[:, 0].astype(I32)
    padded = (cnt + bm - 1) // bm * bm
    pad_ends = jnp.cumsum(padded)
    pad_off = pad_ends - padded
    experts = jnp.arange(N_EXPERTS, dtype=I32)
    hit = route[0:TOP_K, :, None] == experts
    dest = jnp.sum(jnp.where(hit, pad_off, 0), axis=-1) + route[TOP_K:2 * TOP_K]
    gates = route
    starts = jnp.arange(n_blocks, dtype=I32) * bm
    block_expert = jnp.minimum(jnp.sum((pad_ends[None, :] <= starts[:, None]).astype(I32), axis=1),
                               N_EXPERTS - 1)
    block_first = jnp.concatenate([jnp.ones((1,), I32), (block_expert[1:] != block_expert[:-1]).astype(I32)])
    own = block_expert[:, None] == experts
    block_valid = jnp.clip(jnp.sum(jnp.where(own, cnt + pad_off, 0), axis=1) - starts, 0, bm)
    block_valid = jnp.where(starts < pad_ends[-1], block_valid, 0).astype(I32)
    return dest, gates, block_expert, block_first, block_valid, n_blocks * bm


def kernel(x, mem, norm_mix, norm_xattn, norm_ffn, norm_mem, norm_final, conv_w_in, conv_b_in, conv_w_dw,
           conv_b_dw, conv_ln_g, conv_ln_b, conv_w_out, conv_b_out, gla_w_in, gla_w_a2, gla_b_a, gla_norm_g,
           gla_w_o, xa_w_q, xa_w_kv, xa_w_o, moe_w_grp, moe_b_grp, moe_w_exp, moe_b_exp, moe_w_gate, moe_w_up,
           moe_w_down):
    b, s, d = x.shape
    depth = norm_mix.shape[0]
    qk_mem, vo_mem = _mem_kv(mem, norm_mem, xa_w_kv, xa_w_q, xa_w_o)
    moe = None
    for i in range(depth):
        j = i // 2
        if i % 2 == 0:
            if moe is not None:
                y0, y1 = _gather_pairs(moe[1], moe[2])
                x = _combine(x.reshape(b * s, d), y0, y1, moe[0], norm_final, False).reshape(b, s, d)
            x = _conv_mixer(x, norm_mix[i], conv_w_in[j], conv_b_in[j], conv_w_dw[j], conv_b_dw[j],
                            conv_ln_g[j], conv_ln_b[j], conv_w_out[j], conv_b_out[j])
        else:
            x = _gla_mixer(x, *moe, norm_mix[i], gla_w_in[j], gla_w_a2[j], gla_b_a[j], gla_norm_g[j], gla_w_o[j])
        x2, h_packed, route, counts = _xattn_router(
            x, norm_xattn[i], qk_mem, vo_mem, i, norm_ffn[i],
            moe_w_grp[i], moe_b_grp[i], moe_w_exp[i], moe_b_exp[i])
        dest, gates, block_expert, block_first, block_valid, n_rows = _moe_layout(route, counts)
        x_buf = _dispatch(h_packed, dest, n_rows)
        y_buf = _experts(x_buf, block_expert, block_first, block_valid, moe_w_gate, moe_w_up, moe_w_down, i)
        x, moe = x2, (gates, y_buf, dest)
    y0, y1 = _gather_pairs(moe[1], moe[2])
    return _combine(x.reshape(b * s, d), y0, y1, moe[0], norm_final, True).reshape(b, s, d)
```

```python
import functools

import jax
import jax.numpy as jnp
from jax import lax
from jax.experimental import pallas as pl
from jax.experimental.pallas import tpu as pltpu
from jax.experimental.pallas import tpu_sc as plsc

F32 = jnp.float32
BF16 = jnp.bfloat16
I32 = jnp.int32
U32 = jnp.uint32

EPS = 1e-6
CONV_KERNEL = 31
CONV_CARRY = 32
CONV_ROWS = 64
CONV_COLS = 256
SUBLANES = 8
LANES = 128
GLA_HEADS = 4
GLA_RANK = 16
GLA_RANK_PAD = 128
GLA_TAU = 16.0
GLA_LEAF = 32
XATTN_HEADS = 4
N_GROUPS = 4
EXPERTS_PER_GROUP = 8
N_EXPERTS = N_GROUPS * EXPERTS_PER_GROUP
ROUTER_ROWS = 40
TOP_K = 2
ROUTE_ROWS = 8

TILE_CONV = 512
TILE_GLA = 256
TILE_XATTN = 1024
TILE_COMBINE = 512
SC_CORES = 2
SC_SUBCORES = 16
SC_WORKERS = SC_CORES * SC_SUBCORES
SC_DISPATCH_CHUNK = 64
SC_GATHER_CHUNK = 32
MOE_BLOCK_ROWS = 512
VMEM_LIMIT = 56 * 1024 * 1024


def _cparams(sem):
    return pltpu.CompilerParams(dimension_semantics=sem, vmem_limit_bytes=VMEM_LIMIT)


def _rms(x, g):
    return x * lax.rsqrt(jnp.mean(x * x, axis=-1, keepdims=True) + EPS) * g


def _sigmoid(x):
    return 0.5 * jnp.tanh(0.5 * x) + 0.5


def _silu(x):
    h = 0.5 * x
    return h + h * jnp.tanh(h)


def _dot(a, b):
    return jnp.dot(a, b, preferred_element_type=F32)


def _dot_nt(a, b):
    return lax.dot_general(a, b, (((1,), (1,)), ((), ())), preferred_element_type=F32)


def _dot_tn(a, b):
    return lax.dot_general(a, b, (((0,), (0,)), ((), ())), preferred_element_type=F32)


def _pack_bf16_pairs(x):
    w = x.shape[1] // 2
    hi = lax.bitcast_convert_type(x[:, :w].astype(BF16).astype(F32), U32)
    lo = lax.bitcast_convert_type(x[:, w:].astype(BF16).astype(F32), U32)
    return hi | (lo >> 16)


def _unpack_bf16_pairs(p):
    hi = lax.bitcast_convert_type(p & jnp.uint32(0xFFFF0000), F32)
    lo = lax.bitcast_convert_type(p << 16, F32)
    return jnp.concatenate([hi, lo], axis=1)


def _memkv_kernel(mem_ref, g_ref, wkv_ref, wq_ref, wo_ref, qk_ref, vo_ref, wkv_bf, wq_bf, wo_bf):
    nm, d = mem_ref.shape[1], mem_ref.shape[2]
    hd = d // XATTN_HEADS

    @pl.when(pl.program_id(1) == 0)
    def _():
        wkv_bf[...] = wkv_ref[0].astype(BF16)
        wq_bf[...] = wq_ref[0].astype(BF16)
        wo_bf[...] = wo_ref[0].astype(BF16)

    mn = _rms(mem_ref[0], g_ref[...]).astype(BF16)
    kv = _dot(mn, wkv_bf[...])
    k = kv[:, :d].astype(BF16)
    v = kv[:, d:].astype(BF16)
    for a in range(XATTN_HEADS):
        sl = slice(a * hd, (a + 1) * hd)
        qk_ref[0, 0, :, a * nm:(a + 1) * nm] = (_dot_nt(wq_bf[:, sl], k[:, sl]) * (hd ** -0.5)).astype(BF16)
        vo_ref[0, 0, a * nm:(a + 1) * nm, :] = _dot(v[:, sl], wo_bf[sl, :]).astype(BF16)


def _mem_kv(mem, norm_mem, w_kv, w_q, w_o):
    b, nm, d = mem.shape
    depth = w_kv.shape[0]
    per_layer = lambda shape: pl.BlockSpec((1,) + shape, lambda l, i: (l, 0, 0))
    return pl.pallas_call(
        _memkv_kernel,
        out_shape=(jax.ShapeDtypeStruct((depth, b, d, XATTN_HEADS * nm), BF16),
                   jax.ShapeDtypeStruct((depth, b, XATTN_HEADS * nm, d), BF16)),
        grid=(depth, b),
        in_specs=[
            pl.BlockSpec((1, nm, d), lambda l, i: (i, 0, 0)),
            pl.BlockSpec((1, d), lambda l, i: (0, 0)),
            per_layer((d, 2 * d)), per_layer((d, d)), per_layer((d, d)),
        ],
        out_specs=(
            pl.BlockSpec((1, 1, d, XATTN_HEADS * nm), lambda l, i: (l, i, 0, 0)),
            pl.BlockSpec((1, 1, XATTN_HEADS * nm, d), lambda l, i: (l, i, 0, 0)),
        ),
        scratch_shapes=[pltpu.VMEM((d, 2 * d), BF16), pltpu.VMEM((d, d), BF16), pltpu.VMEM((d, d), BF16)],
        compiler_params=_cparams(("arbitrary", "arbitrary")),
        name="mem_kv",
    )(mem, norm_mem.reshape(1, d), w_kv, w_q, w_o)


def _conv_kernel(x_ref, g_ref, win_ref, bin_ref, wdw_ref, bdw_ref, lng_ref, lnb_ref, wout_ref, bout_ref,
                 o_ref, ext_ref, even_ref, odd_ref):
    ts, d = x_ref.shape[1], x_ref.shape[2]
    n_shifts = ext_ref.shape[0]

    @pl.when(pl.program_id(1) == 0)
    def _():
        ext_ref[...] = jnp.zeros(ext_ref.shape, F32)

    x = x_ref[0]
    h = _rms(x, g_ref[...]).astype(BF16)
    u = _dot(h, win_ref[...]) + bin_ref[...]
    glu = u[:, :d] * _sigmoid(u[:, d:])
    for c in range(n_shifts):
        ext_ref[c, CONV_CARRY - 2 * c:CONV_CARRY - 2 * c + ts, :] = glu

    first = CONV_CARRY - (CONV_KERNEL - 1)

    def taps(r0, n_rows, parity, dst_ref):
        for c0 in range(0, d, CONV_COLS):
            cols = slice(c0, c0 + CONV_COLS)
            acc = [jnp.zeros((SUBLANES, CONV_COLS), F32) for _ in range(n_rows // SUBLANES)]
            for k in range(CONV_KERNEL):
                if (first + k) % 2 != parity:
                    continue
                shift = (first + k - parity) % SUBLANES
                base = first + k - parity - shift
                wk = wdw_ref[k * SUBLANES:(k + 1) * SUBLANES, cols]
                for j in range(n_rows // SUBLANES):
                    rows = pl.ds(r0 + base + j * SUBLANES, SUBLANES)
                    acc[j] = acc[j] + ext_ref[shift // 2, rows, cols] * wk
            for j in range(n_rows // SUBLANES):
                dst_ref[pl.ds(r0 + j * SUBLANES, SUBLANES), cols] = acc[j]

    def chunk(i, carry):
        r0 = pl.multiple_of(i * CONV_ROWS, CONV_ROWS)
        taps(r0, CONV_ROWS, 0, even_ref)
        taps(r0, CONV_ROWS, 1, odd_ref)
        return carry

    lax.fori_loop(0, ts // CONV_ROWS, chunk, 0)
    taps(ts, SUBLANES, 1, odd_ref)
    for c in range(n_shifts):
        ext_ref[c, 0:CONV_CARRY, :] = ext_ref[c, ts:ts + CONV_CARRY, :]

    c = even_ref[...] + odd_ref[1:ts + 1, :] + bdw_ref[...]
    mu = jnp.mean(c, axis=-1, keepdims=True)
    cc = c - mu
    var = jnp.mean(cc * cc, axis=-1, keepdims=True)
    un = cc * lax.rsqrt(var + EPS) * lng_ref[...] + lnb_ref[...]
    act = _silu(un).astype(BF16)
    o_ref[0] = x + _dot(act, wout_ref[...]) + bout_ref[...]


def _conv_mixer(x, g, w_in, b_in, w_dw, b_dw, ln_g, ln_b, w_out, b_out):
    b, s, d = x.shape
    ts = min(TILE_CONV, s)
    assert s % ts == 0 and ts % CONV_ROWS == 0 and d % CONV_COLS == 0
    row = lambda v: v.reshape(1, -1)
    const = lambda shape: pl.BlockSpec(shape, lambda i, j: (0,) * len(shape))
    return pl.pallas_call(
        _conv_kernel,
        out_shape=jax.ShapeDtypeStruct(x.shape, F32),
        grid=(b, s // ts),
        in_specs=[
            pl.BlockSpec((1, ts, d), lambda i, j: (i, j, 0)),
            const((1, d)), const((d, 2 * d)), const((1, 2 * d)), const((CONV_KERNEL * SUBLANES, d)), const((1, d)),
            const((1, d)), const((1, d)), const((d, d)), const((1, d)),
        ],
        out_specs=pl.BlockSpec((1, ts, d), lambda i, j: (i, j, 0)),
        scratch_shapes=[pltpu.VMEM((SUBLANES // 2, CONV_CARRY + ts, d), F32), pltpu.VMEM((ts, d), F32),
                        pltpu.VMEM((ts + SUBLANES, d), F32)],
        compiler_params=_cparams(("arbitrary", "arbitrary")),
        name="conv_mixer",
    )(x, row(g), w_in.astype(BF16), row(b_in), jnp.repeat(w_dw, SUBLANES, axis=0), row(b_dw), row(ln_g), row(ln_b),
      w_out.astype(BF16), row(b_out))


def _gla_levels(ts):
    sizes = [GLA_LEAF]
    while sizes[-1] < ts:
        sizes.append(sizes[-1] * 2)
    return sizes


def _moe_residual(x, route, y0, y1):
    g = lax.bitcast_convert_type(route, F32)
    g = jnp.concatenate([g, jnp.zeros((LANES - g.shape[0], g.shape[1]), F32)], axis=0).T
    gate0 = g[:, 2 * TOP_K:2 * TOP_K + 1]
    gate1 = g[:, 2 * TOP_K + 1:2 * TOP_K + 2]
    return x + _unpack_bf16_pairs(y0) * gate0 + _unpack_bf16_pairs(y1) * gate1


def _gla_kernel(x_ref, gate_ref, y0_ref, y1_ref, g_ref, wq_ref, wk_ref, wv_ref, wa_ref, wr_ref, wa2_ref, ba_ref,
                ng_ref, wo_ref, o_ref, state_ref, x_s, q_s, k_s, v_s, og_s, la_s, *, tiles_per_seq):
    j = pl.program_id(0)
    dkh = wq_ref.shape[1] // GLA_HEADS

    def stage_a(slot):
        x = _moe_residual(x_ref[0], gate_ref[...], y0_ref[...], y1_ref[...])
        h = _rms(x, g_ref[...]).astype(BF16)
        x_s[slot] = x
        q_s[slot] = _dot(h, wq_ref[...]) * (dkh ** -0.5)
        k_s[slot] = _dot(h, wk_ref[...])
        v_s[slot] = _dot(h, wv_ref[...]).astype(BF16)
        r = _dot(h, wr_ref[...])
        og_s[slot] = _silu(r)
        a = _dot(h, wa_ref[...]).astype(BF16)
        z = _dot(jnp.concatenate([a, a], axis=1), wa2_ref[...]) + ba_ref[...]
        la_s[slot] = -(jnp.maximum(-z, 0.0) + jnp.log(1.0 + jnp.exp(-jnp.abs(z)))) * (1.0 / GLA_TAU)

    def stage_b(slot):
        _gla_recurrence(x_s[slot], q_s[slot], k_s[slot], v_s[slot], og_s[slot], la_s[slot],
                        ng_ref, wo_ref, o_ref, state_ref)

    @pl.when(j == 0)
    def _():
        state_ref[...] = jnp.zeros(state_ref.shape, F32)
        stage_a(0)

    @pl.when((j > 0) & (j < tiles_per_seq) & (j % 2 == 0))
    def _():
        stage_b(1)
        stage_a(0)

    @pl.when((j < tiles_per_seq) & (j % 2 == 1))
    def _():
        stage_b(0)
        stage_a(1)

    @pl.when(j == tiles_per_seq)
    def _():
        stage_b((tiles_per_seq - 1) % 2)


def _gla_recurrence(x, q, k, v, out_gate, log_a, ng_ref, wo_ref, o_ref, state_ref):
    ts, d = x.shape
    dk = q.shape[1]
    dkh = dk // GLA_HEADS
    dvh = d // GLA_HEADS

    row = lax.broadcasted_iota(I32, (ts, ts), 0)
    col = lax.broadcasted_iota(I32, (ts, ts), 1)
    tri = jnp.where(col <= row, 1.0, 0.0).astype(BF16)
    bcum = _dot(tri, log_a.astype(BF16))
    b_last = bcum[ts - 1:ts, :]

    q_in = (q * jnp.exp(bcum)).astype(BF16)
    k_out = (k * jnp.exp(b_last - bcum)).astype(BF16)

    sizes = _gla_levels(ts)
    scores = [None] * GLA_HEADS
    for lvl, size in enumerate(sizes):
        half = size // 2
        same_block = (row & -size) == (col & -size)
        if lvl == 0:
            pair = same_block & (col <= row)
            q_ok = k_ok = None
        else:
            pair = same_block & ((row & (size - 1)) >= half) & ((col & (size - 1)) < half)
            pos = lax.broadcasted_iota(I32, (ts, dk), 0) & (size - 1)
            q_ok = pos >= half
            k_ok = pos < half
        ref = jnp.concatenate(
            [jnp.broadcast_to(bcum[r0 + half:r0 + half + 1, :], (size, dk)) for r0 in range(0, ts, size)], axis=0)
        ql = q * jnp.exp(bcum - ref)
        kl = k * jnp.exp(ref - bcum)
        if q_ok is not None:
            ql = jnp.where(q_ok, ql, 0.0)
            kl = jnp.where(k_ok, kl, 0.0)
        ql = ql.astype(BF16)
        kl = kl.astype(BF16)
        for hd in range(GLA_HEADS):
            c0 = hd * dkh
            a = _dot_nt(ql[:, c0:c0 + dkh], kl[:, c0:c0 + dkh])
            a = jnp.where(pair, a, 0.0)
            scores[hd] = a if scores[hd] is None else scores[hd] + a

    outs = []
    for hd in range(GLA_HEADS):
        c0 = hd * dkh
        v_h = v[:, hd * dvh:(hd + 1) * dvh]
        st = state_ref[hd]
        o_h = _dot(scores[hd].astype(BF16), v_h) + _dot_nt(q_in[:, c0:c0 + dkh], st.astype(BF16))
        decay = jnp.exp(b_last[:, c0:c0 + dkh])
        state_ref[hd] = st * decay + _dot_tn(v_h, k_out[:, c0:c0 + dkh])
        o_h = o_h * lax.rsqrt(jnp.mean(o_h * o_h, axis=-1, keepdims=True) + EPS) * ng_ref[...]
        outs.append(o_h)
    o = jnp.concatenate(outs, axis=1) * out_gate
    o_ref[0] = x + _dot(o.astype(BF16), wo_ref[...])


def _gla_mixer(x, gates, y_buf, dest, g, w_in, w_a2, b_a, norm_g, w_o):
    b, s, d = x.shape
    dk = w_a2.shape[1]
    ts = min(TILE_GLA, s)
    assert s % ts == 0 and ts % GLA_LEAF == 0
    nj = s // ts
    row = lambda v: v.reshape(1, -1)
    const = lambda shape: pl.BlockSpec(shape, lambda j: (0,) * len(shape))
    wq = w_in[:, :dk].astype(BF16)
    wk = w_in[:, dk:2 * dk].astype(BF16)
    wv = w_in[:, 2 * dk:2 * dk + d].astype(BF16)
    wa = jnp.pad(w_in[:, 2 * dk + d:2 * dk + d + GLA_RANK], ((0, 0), (0, GLA_RANK_PAD - GLA_RANK))).astype(BF16)
    wr = w_in[:, 2 * dk + d + GLA_RANK:].astype(BF16)
    wa2 = jnp.pad(w_a2, ((0, GLA_RANK_PAD - GLA_RANK), (0, 0)))
    wa2_hi = wa2.astype(BF16)
    wa2_lo = (wa2 - wa2_hi.astype(F32)).astype(BF16)
    wa2_split = jnp.concatenate([wa2_hi, wa2_lo], axis=0)
    dvh = d // GLA_HEADS
    weights = (row(g), wq, wk, wv, wa, wr, wa2_split, row(b_a), row(norm_g), w_o.astype(BF16))

    def tile(j, lag):
        return jnp.clip(j - lag, 0, nj - 1)

    def one_sequence(x, seq, y0, y1):
        return pl.pallas_call(
            functools.partial(_gla_kernel, tiles_per_seq=nj),
            out_shape=jax.ShapeDtypeStruct(x.shape, F32),
            grid=(nj + 1,),
            in_specs=[
                pl.BlockSpec((1, ts, d), lambda j: (seq, tile(j, 0), 0)),
                pl.BlockSpec((ROUTE_ROWS, ts), lambda j: (0, seq * nj + tile(j, 0))),
                pl.BlockSpec((ts, d // 2), lambda j: (tile(j, 0), 0)),
                pl.BlockSpec((ts, d // 2), lambda j: (tile(j, 0), 0)),
                const((1, d)), const((d, dk)), const((d, dk)), const((d, d)), const((d, GLA_RANK_PAD)),
                const((d, d)), const((2 * GLA_RANK_PAD, dk)), const((1, dk)), const((1, dvh)), const((d, d)),
            ],
            out_specs=pl.BlockSpec((1, ts, d), lambda j: (seq, tile(j, 1), 0)),
            scratch_shapes=[pltpu.VMEM((GLA_HEADS, dvh, dk // GLA_HEADS), F32),
                            pltpu.VMEM((2, ts, d), F32), pltpu.VMEM((2, ts, dk), F32), pltpu.VMEM((2, ts, dk), F32),
                            pltpu.VMEM((2, ts, d), BF16), pltpu.VMEM((2, ts, d), F32), pltpu.VMEM((2, ts, dk), F32)],
            input_output_aliases={0: 0},
            compiler_params=_cparams(("arbitrary",)),
            name="gla_mixer",
        )(x, gates, y0, y1, *weights)

    for seq in range(b):
        y0, y1 = _gather_pairs(y_buf, dest[:, seq * s:(seq + 1) * s])
        x = one_sequence(x, seq, y0, y1)
    return x


def _xattn_router_kernel(x_ref, gx_ref, qk_ref, vo_ref, gf_ref, wr_ref, br_ref, upper_ref,
                         x_out_ref, h_out_ref, route_ref, cnt_ref, carry_ref):
    ts, d = x_ref.shape[1], x_ref.shape[2]
    nm = qk_ref.shape[3] // XATTN_HEADS
    first = (pl.program_id(0) == 0) & (pl.program_id(1) == 0)

    @pl.when(first)
    def _():
        carry_ref[...] = jnp.zeros(carry_ref.shape, F32)

    x = x_ref[0]
    h = _rms(x, gx_ref[...]).astype(BF16)
    scores = _dot(h, qk_ref[0, 0])
    probs = []
    for a in range(XATTN_HEADS):
        s = scores[:, a * nm:(a + 1) * nm]
        p = jnp.exp(s - jnp.max(s, axis=-1, keepdims=True))
        probs.append((p / jnp.sum(p, axis=-1, keepdims=True)).astype(BF16))
    x2 = x + _dot(jnp.concatenate(probs, axis=1), vo_ref[0, 0])
    x_out_ref[0] = x2

    hf = _rms(x2, gf_ref[...])
    h_out_ref[...] = _pack_bf16_pairs(hf)

    both = _dot_nt(wr_ref[...], hf.astype(BF16))
    logits = both[0:ROUTER_ROWS, :] + both[ROUTER_ROWS:2 * ROUTER_ROWS, :] + br_ref[...]
    gl = logits[N_EXPERTS:N_EXPERTS + N_GROUPS, :]
    gi = lax.broadcasted_iota(I32, gl.shape, 0).astype(F32)
    gmax = jnp.max(gl, axis=0, keepdims=True)
    g_sel = jnp.min(jnp.where(gl == gmax, gi, float(N_GROUPS)), axis=0, keepdims=True)
    pg_sel = 1.0 / jnp.sum(jnp.exp(gl - gmax), axis=0, keepdims=True)

    el = jnp.zeros((EXPERTS_PER_GROUP, ts), F32)
    for gidx in range(N_GROUPS):
        lo = gidx * EXPERTS_PER_GROUP
        el = jnp.where(g_sel == float(gidx), logits[lo:lo + EXPERTS_PER_GROUP, :], el)
    ei = lax.broadcasted_iota(I32, el.shape, 0).astype(F32)
    m1 = jnp.max(el, axis=0, keepdims=True)
    i1 = jnp.min(jnp.where(el == m1, ei, float(EXPERTS_PER_GROUP)), axis=0, keepdims=True)
    rest = jnp.where(ei == i1, -jnp.inf, el)
    m2 = jnp.max(rest, axis=0, keepdims=True)
    i2 = jnp.min(jnp.where(rest == m2, ei, float(EXPERTS_PER_GROUP)), axis=0, keepdims=True)
    ratio = jnp.exp(m2 - m1)
    gate1 = pg_sel / (1.0 + ratio)
    gate2 = pg_sel * ratio / (1.0 + ratio)
    e1 = g_sel * float(EXPERTS_PER_GROUP) + i1
    e2 = g_sel * float(EXPERTS_PER_GROUP) + i2

    xi = lax.broadcasted_iota(I32, (N_EXPERTS, ts), 0).astype(F32)
    oh1 = jnp.where(xi == e1, 1.0, 0.0)
    oh2 = jnp.where(xi == e2, 1.0, 0.0)
    oh = oh1 + oh2
    n_blk = ts // LANES
    stacked = jnp.concatenate([oh[:, c * LANES:(c + 1) * LANES] for c in range(n_blk)], axis=0)
    within = _dot(stacked.astype(BF16), upper_ref[...])
    totals = jnp.sum(stacked, axis=1, keepdims=True)
    run = carry_ref[...]
    before = []
    for c in range(n_blk):
        before.append(within[c * N_EXPERTS:(c + 1) * N_EXPERTS, :] + run)
        run = run + totals[c * N_EXPERTS:(c + 1) * N_EXPERTS, :]
    before = jnp.concatenate(before, axis=1)
    rank1 = jnp.sum(oh1 * before, axis=0, keepdims=True)
    rank2 = jnp.sum(oh2 * before, axis=0, keepdims=True)
    carry_ref[...] = run
    cnt_ref[...] = jnp.broadcast_to(run, cnt_ref.shape)

    route_ref[0:1, :] = e1.astype(I32)
    route_ref[1:2, :] = e2.astype(I32)
    route_ref[2:3, :] = rank1.astype(I32)
    route_ref[3:4, :] = rank2.astype(I32)
    route_ref[4:5, :] = lax.bitcast_convert_type(gate1, I32)
    route_ref[5:6, :] = lax.bitcast_convert_type(gate2, I32)
    route_ref[3 * TOP_K:ROUTE_ROWS, :] = jnp.zeros((ROUTE_ROWS - 3 * TOP_K, ts), I32)


def _xattn_router(x, g_x, qk_mem, vo_mem, layer, g_f, w_grp, b_grp, w_exp, b_exp):
    b, s, d = x.shape
    hm = qk_mem.shape[3]
    t = b * s
    ts = min(TILE_XATTN, s)
    assert s % ts == 0 and ts % LANES == 0
    nj = s // ts
    row = lambda v: v.reshape(1, -1)
    const = lambda shape: pl.BlockSpec(shape, lambda i, j: (0,) * len(shape))
    pad = ROUTER_ROWS - N_GROUPS - N_EXPERTS
    w_r = jnp.pad(jnp.concatenate([w_exp, w_grp], axis=1).T, ((0, pad), (0, 0)))
    w_r_hi = w_r.astype(BF16)
    w_r_split = jnp.concatenate([w_r_hi, (w_r - w_r_hi.astype(F32)).astype(BF16)], axis=0)
    b_r = jnp.pad(jnp.concatenate([b_exp, b_grp]), (0, pad)).reshape(ROUTER_ROWS, 1)
    ti = jnp.arange(LANES)
    upper = (ti[:, None] < ti[None, :]).astype(BF16)
    return pl.pallas_call(
        _xattn_router_kernel,
        out_shape=(
            jax.ShapeDtypeStruct(x.shape, F32),
            jax.ShapeDtypeStruct((t, d // 2), U32),
            jax.ShapeDtypeStruct((ROUTE_ROWS, t), I32),
            jax.ShapeDtypeStruct((N_EXPERTS, LANES), F32),
        ),
        grid=(b, nj),
        in_specs=[
            pl.BlockSpec((1, ts, d), lambda i, j: (i, j, 0)),
            const((1, d)),
            pl.BlockSpec((1, 1, d, hm), lambda i, j: (layer, i, 0, 0)),
            pl.BlockSpec((1, 1, hm, d), lambda i, j: (layer, i, 0, 0)),
            const((1, d)), const((2 * ROUTER_ROWS, d)), const((ROUTER_ROWS, 1)), const((LANES, LANES)),
        ],
        out_specs=(
            pl.BlockSpec((1, ts, d), lambda i, j: (i, j, 0)),
            pl.BlockSpec((ts, d // 2), lambda i, j: (i * nj + j, 0)),
            pl.BlockSpec((ROUTE_ROWS, ts), lambda i, j: (0, i * nj + j)),
            pl.BlockSpec((N_EXPERTS, LANES), lambda i, j: (0, 0)),
        ),
        scratch_shapes=[pltpu.VMEM((N_EXPERTS, 1), F32)],
        compiler_params=_cparams(("arbitrary", "arbitrary")),
        name="xattn_router",
    )(x, row(g_x), qk_mem, vo_mem, row(g_f), w_r_split, b_r, upper)


def _sc_mesh():
    return plsc.VectorSubcoreMesh(core_axis_name="c", subcore_axis_name="s",
                                  num_cores=SC_CORES, num_subcores=SC_SUBCORES)


def _sc_worker():
    return lax.axis_index("s") * SC_CORES + lax.axis_index("c")


def _dispatch(h_packed, dest, n_rows):
    t, w = h_packed.shape
    chunk = SC_DISPATCH_CHUNK
    per_worker = t // SC_WORKERS
    n_chunks = per_worker // chunk
    assert n_chunks % 2 == 0 and n_chunks * chunk * SC_WORKERS == t
    dest = dest.reshape(TOP_K, SC_WORKERS * n_chunks, chunk)
    rows_buf = pltpu.VMEM((chunk, w), U32)

    @functools.partial(
        pl.kernel, mesh=_sc_mesh(),
        out_type=jax.ShapeDtypeStruct((n_rows, w), U32),
        scratch_types=[pltpu.VMEM((n_chunks, chunk), I32), pltpu.VMEM((n_chunks, chunk), I32), rows_buf, rows_buf,
                       pltpu.SemaphoreType.DMA((2,)), pltpu.SemaphoreType.DMA((2, TOP_K))],
        name="moe_dispatch_sc",
    )
    def run(h_hbm, d0_hbm, d1_hbm, xbuf_hbm, idx0_v, idx1_v, buf_a, buf_b, read_sem, write_sem):
        wid = _sc_worker()
        pltpu.sync_copy(d0_hbm.at[pl.ds(wid * n_chunks, n_chunks)], idx0_v)
        pltpu.sync_copy(d1_hbm.at[pl.ds(wid * n_chunks, n_chunks)], idx1_v)

        @pl.loop(0, n_chunks, step=2)
        def _(i):
            reads = [pltpu.async_copy(h_hbm.at[pl.ds(wid * per_worker + (i + j) * chunk, chunk)], buf, read_sem.at[j])
                     for j, buf in enumerate((buf_a, buf_b))]
            writes = []
            for j, buf in enumerate((buf_a, buf_b)):
                reads[j].wait()
                writes.append(pltpu.async_copy(buf, xbuf_hbm.at[idx0_v.at[i + j]], write_sem.at[j, 0]))
                writes.append(pltpu.async_copy(buf, xbuf_hbm.at[idx1_v.at[i + j]], write_sem.at[j, 1]))
            for copy in writes:
                copy.wait()

    return run(h_packed, dest[0], dest[1])


def _gather_pairs(y_buf, dest):
    t = dest.shape[1]
    w = y_buf.shape[1]
    chunk = SC_GATHER_CHUNK
    per_worker = t // SC_WORKERS
    n_chunks = per_worker // chunk
    assert n_chunks % 2 == 0 and n_chunks * chunk * SC_WORKERS == t
    dest = dest.reshape(TOP_K, SC_WORKERS * n_chunks, chunk)
    out = jax.ShapeDtypeStruct((t, w), U32)
    rows_buf = pltpu.VMEM((chunk, w), U32)

    @functools.partial(
        pl.kernel, mesh=_sc_mesh(),
        out_type=(out, out),
        scratch_types=[pltpu.VMEM((n_chunks, chunk), I32), pltpu.VMEM((n_chunks, chunk), I32),
                       rows_buf, rows_buf, rows_buf, rows_buf, pltpu.SemaphoreType.DMA((2, TOP_K))],
        name="moe_gather_sc",
    )
    def run(y_hbm, d0_hbm, d1_hbm, y0_hbm, y1_hbm, idx0_v, idx1_v, buf_a0, buf_a1, buf_b0, buf_b1, sem):
        wid = _sc_worker()
        pltpu.sync_copy(d0_hbm.at[pl.ds(wid * n_chunks, n_chunks)], idx0_v)
        pltpu.sync_copy(d1_hbm.at[pl.ds(wid * n_chunks, n_chunks)], idx1_v)

        @pl.loop(0, n_chunks, step=2)
        def _(i):
            bufs = ((buf_a0, buf_a1), (buf_b0, buf_b1))
            gathers = [[pltpu.async_copy(y_hbm.at[idx_v.at[i + j]], bufs[j][k], sem.at[j, k])
                        for k, idx_v in enumerate((idx0_v, idx1_v))] for j in range(2)]
            stores = []
            for j in range(2):
                rows = pl.ds(wid * per_worker + (i + j) * chunk, chunk)
                for k, out_hbm in enumerate((y0_hbm, y1_hbm)):
                    gathers[j][k].wait()
                    stores.append(pltpu.async_copy(bufs[j][k], out_hbm.at[rows], sem.at[j, k]))
            for copy in stores:
                copy.wait()

    return run(y_buf, dest[0], dest[1])


def _expert_kernel(be_ref, bf_ref, bv_ref, slot_ref, next_ref, x_ref, wg_hbm, wu_hbm, wd_hbm, y_ref,
                   wg_f32, wu_f32, wd_f32, wg_bf, wu_bf, wd_bf, sems, *, layer):
    blk = pl.program_id(0)
    valid = bv_ref[blk]

    def weight_copies(expert, slot):
        return [pltpu.make_async_copy(src.at[layer, expert], dst.at[slot], sems.at[slot, i])
                for i, (src, dst) in enumerate(((wg_hbm, wg_f32), (wu_hbm, wu_f32), (wd_hbm, wd_f32)))]

    @pl.when(blk == 0)
    def _():
        for copy in weight_copies(be_ref[0], 0):
            copy.start()

    @pl.when(bf_ref[blk] == 1)
    def _():
        slot = slot_ref[blk]
        for copy in weight_copies(be_ref[blk], slot):
            copy.wait()

        @pl.when(next_ref[blk] >= 0)
        def _():
            for copy in weight_copies(next_ref[blk], 1 - slot):
                copy.start()

        wg_bf[...] = wg_f32[slot].astype(BF16)
        wu_bf[...] = wu_f32[slot].astype(BF16)
        wd_bf[...] = wd_f32[slot].astype(BF16)

    @pl.when(valid > 0)
    def _():
        live = lax.broadcasted_iota(I32, x_ref.shape, 0) < valid
        xb = _unpack_bf16_pairs(jnp.where(live, x_ref[...], jnp.uint32(0))).astype(BF16)
        gt = _dot(xb, wg_bf[...])
        up = _dot(xb, wu_bf[...])
        act = (_silu(gt) * up).astype(BF16)
        y_ref[...] = _pack_bf16_pairs(_dot(act, wd_bf[...]))

    @pl.when(valid <= 0)
    def _():
        y_ref[...] = jnp.zeros(y_ref.shape, U32)


def _experts(x_buf, block_expert, block_first, block_valid, w_gate, w_up, w_down, layer):
    n_rows, w = x_buf.shape
    d, de = w_gate.shape[2], w_gate.shape[3]
    bm = MOE_BLOCK_ROWS
    n = n_rows // bm
    block_slot = (jnp.cumsum(block_first) - 1) % 2
    idx = jnp.arange(n, dtype=I32)
    later_first = jnp.concatenate([jnp.where(block_first[1:] == 1, idx[1:], n), jnp.full((1,), n, I32)])
    next_first = lax.cummin(later_first, reverse=True)
    block_next = jnp.where(next_first < n, block_expert[jnp.minimum(next_first, n - 1)], -1).astype(I32)
    any_space = pl.BlockSpec(memory_space=pl.ANY)
    grid_spec = pltpu.PrefetchScalarGridSpec(
        num_scalar_prefetch=5,
        grid=(n,),
        in_specs=[pl.BlockSpec((bm, w), lambda i, *_: (i, 0)), any_space, any_space, any_space],
        out_specs=pl.BlockSpec((bm, w), lambda i, *_: (i, 0)),
        scratch_shapes=[pltpu.VMEM((2, d, de), F32), pltpu.VMEM((2, d, de), F32), pltpu.VMEM((2, de, d), F32),
                        pltpu.VMEM((d, de), BF16), pltpu.VMEM((d, de), BF16), pltpu.VMEM((de, d), BF16),
                        pltpu.SemaphoreType.DMA((2, 3))],
    )
    return pl.pallas_call(
        functools.partial(_expert_kernel, layer=layer),
        out_shape=jax.ShapeDtypeStruct((n_rows, w), U32),
        grid_spec=grid_spec,
        compiler_params=_cparams(("arbitrary",)),
        name="moe_experts",
    )(block_expert, block_first, block_valid, block_slot.astype(I32), block_next, x_buf, w_gate, w_up, w_down)


def _combine_kernel(x_ref, gate_ref, y0_ref, y1_ref, gfin_ref, o_ref, *, final_norm):
    out = _moe_residual(x_ref[...], gate_ref[...], y0_ref[...], y1_ref[...])
    if final_norm:
        out = _rms(out, gfin_ref[...])
    o_ref[...] = out


def _combine(x2, y0, y1, gates, g_final, final_norm):
    t, d = x2.shape
    w = y0.shape[1]
    ts = min(TILE_COMBINE, t)
    assert t % ts == 0
    return pl.pallas_call(
        functools.partial(_combine_kernel, final_norm=final_norm),
        out_shape=jax.ShapeDtypeStruct((t, d), F32),
        grid=(t // ts,),
        in_specs=[
            pl.BlockSpec((ts, d), lambda i: (i, 0)),
            pl.BlockSpec((ROUTE_ROWS, ts), lambda i: (0, i)),
            pl.BlockSpec((ts, w), lambda i: (i, 0)),
            pl.BlockSpec((ts, w), lambda i: (i, 0)),
            pl.BlockSpec((1, d), lambda i: (0, 0)),
        ],
        out_specs=pl.BlockSpec((ts, d), lambda i: (i, 0)),
        compiler_params=_cparams(("arbitrary",)),
        name="moe_combine",
    )(x2, gates, y0, y1, g_final.reshape(1, d))


def _moe_layout(route, counts):
    bm = MOE_BLOCK_ROWS
    t = route.shape[1]
    assert (t * TOP_K) % bm == 0
    n_blocks = (t * TOP_K) // bm + N_EXPERTS
    cnt = counts[:, 0].astype(I32)
    padded = (cnt + bm - 1) // bm * bm
    pad_ends = jnp.cumsum(padded)
    pad_off = pad_ends - padded
    experts = jnp.arange(N_EXPERTS, dtype=I32)
    hit = route[0:TOP_K, :, None] == experts
    dest = jnp.sum(jnp.where(hit, pad_off, 0), axis=-1) + route[TOP_K:2 * TOP_K]
    gates = route
    starts = jnp.arange(n_blocks, dtype=I32) * bm
    block_expert = jnp.minimum(jnp.sum((pad_ends[None, :] <= starts[:, None]).astype(I32), axis=1),
                               N_EXPERTS - 1)
    block_first = jnp.concatenate([jnp.ones((1,), I32), (block_expert[1:] != block_expert[:-1]).astype(I32)])
    own = block_expert[:, None] == experts
    block_valid = jnp.clip(jnp.sum(jnp.where(own, cnt + pad_off, 0), axis=1) - starts, 0, bm)
    block_valid = jnp.where(starts < pad_ends[-1], block_valid, 0).astype(I32)
    return dest, gates, block_expert, block_first, block_valid, n_blocks * bm


def kernel(x, mem, norm_mix, norm_xattn, norm_ffn, norm_mem, norm_final, conv_w_in, conv_b_in, conv_w_dw,
           conv_b_dw, conv_ln_g, conv_ln_b, conv_w_out, conv_b_out, gla_w_in, gla_w_a2, gla_b_a, gla_norm_g,
           gla_w_o, xa_w_q, xa_w_kv, xa_w_o, moe_w_grp, moe_b_grp, moe_w_exp, moe_b_exp, moe_w_gate, moe_w_up,
           moe_w_down):
    b, s, d = x.shape
    depth = norm_mix.shape[0]
    qk_mem, vo_mem = _mem_kv(mem, norm_mem, xa_w_kv, xa_w_q, xa_w_o)
    moe = None
    for i in range(depth):
        j = i // 2
        if i % 2 == 0:
            if moe is not None:
                y0, y1 = _gather_pairs(moe[1], moe[2])
                x = _combine(x.reshape(b * s, d), y0, y1, moe[0], norm_final, False).reshape(b, s, d)
            x = _conv_mixer(x, norm_mix[i], conv_w_in[j], conv_b_in[j], conv_w_dw[j], conv_b_dw[j],
                            conv_ln_g[j], conv_ln_b[j], conv_w_out[j], conv_b_out[j])
        else:
            x = _gla_mixer(x, *moe, norm_mix[i], gla_w_in[j], gla_w_a2[j], gla_b_a[j], gla_norm_g[j], gla_w_o[j])
        x2, h_packed, route, counts = _xattn_router(
            x, norm_xattn[i], qk_mem, vo_mem, i, norm_ffn[i],
            moe_w_grp[i], moe_b_grp[i], moe_w_exp[i], moe_b_exp[i])
        dest, gates, block_expert, block_first, block_valid, n_rows = _moe_layout(route, counts)
        x_buf = _dispatch(h_packed, dest, n_rows)
        y_buf = _experts(x_buf, block_expert, block_first, block_valid, moe_w_gate, moe_w_up, moe_w_down, i)
        x, moe = x2, (gates, y_buf, dest)
    y0, y1 = _gather_pairs(moe[1], moe[2])
    return _combine(x.reshape(b * s, d), y0, y1, moe[0], norm_final, True).reshape(b, s, d)
```

```python
import functools

import jax
import jax.numpy as jnp
from jax import lax
from jax.experimental import pallas as pl
from jax.experimental.pallas import tpu as pltpu
from jax.experimental.pallas import tpu_sc as plsc

F32 = jnp.float32
BF16 = jnp.bfloat16
I32 = jnp.int32
U32 = jnp.uint32

EPS = 1e-6
CONV_KERNEL = 31
CONV_CARRY = 32
CONV_ROWS = 64
CONV_COLS = 256
SUBLANES = 8
LANES = 128
GLA_HEADS = 4
GLA_RANK = 16
GLA_RANK_PAD = 128
GLA_TAU = 16.0
GLA_LEAF = 32
XATTN_HEADS = 4
N_GROUPS = 4
EXPERTS_PER_GROUP = 8
N_EXPERTS = N_GROUPS * EXPERTS_PER_GROUP
ROUTER_ROWS = 40
TOP_K = 2
ROUTE_ROWS = 8

TILE_CONV = 512
TILE_GLA = 256
TILE_XATTN = 1024
TILE_COMBINE = 512
SC_CORES = 2
SC_SUBCORES = 16
SC_WORKERS = SC_CORES * SC_SUBCORES
SC_DISPATCH_CHUNK = 64
SC_GATHER_CHUNK = 32
MOE_BLOCK_ROWS = 512
VMEM_LIMIT = 56 * 1024 * 1024


def _cparams(sem):
    return pltpu.CompilerParams(dimension_semantics=sem, vmem_limit_bytes=VMEM_LIMIT)


def _rms(x, g):
    return x * lax.rsqrt(jnp.mean(x * x, axis=-1, keepdims=True) + EPS) * g


def _sigmoid(x):
    return 0.5 * jnp.tanh(0.5 * x) + 0.5


def _silu(x):
    h = 0.5 * x
    return h + h * jnp.tanh(h)


def _dot(a, b):
    return jnp.dot(a, b, preferred_element_type=F32)


def _dot_nt(a, b):
    return lax.dot_general(a, b, (((1,), (1,)), ((), ())), preferred_element_type=F32)


def _dot_tn(a, b):
    return lax.dot_general(a, b, (((0,), (0,)), ((), ())), preferred_element_type=F32)


def _pack_bf16_pairs(x):
    w = x.shape[1] // 2
    hi = lax.bitcast_convert_type(x[:, :w].astype(BF16).astype(F32), U32)
    lo = lax.bitcast_convert_type(x[:, w:].astype(BF16).astype(F32), U32)
    return hi | (lo >> 16)


def _unpack_bf16_pairs(p):
    hi = lax.bitcast_convert_type(p & jnp.uint32(0xFFFF0000), F32)
    lo = lax.bitcast_convert_type(p << 16, F32)
    return jnp.concatenate([hi, lo], axis=1)


def _memkv_kernel(mem_ref, g_ref, wkv_ref, wq_ref, wo_ref, qk_ref, vo_ref, wkv_bf, wq_bf, wo_bf):
    nm, d = mem_ref.shape[1], mem_ref.shape[2]
    hd = d // XATTN_HEADS

    @pl.when(pl.program_id(1) == 0)
    def _():
        wkv_bf[...] = wkv_ref[0].astype(BF16)
        wq_bf[...] = wq_ref[0].astype(BF16)
        wo_bf[...] = wo_ref[0].astype(BF16)

    mn = _rms(mem_ref[0], g_ref[...]).astype(BF16)
    kv = _dot(mn, wkv_bf[...])
    k = kv[:, :d].astype(BF16)
    v = kv[:, d:].astype(BF16)
    for a in range(XATTN_HEADS):
        sl = slice(a * hd, (a + 1) * hd)
        qk_ref[0, 0, :, a * nm:(a + 1) * nm] = (_dot_nt(wq_bf[:, sl], k[:, sl]) * (hd ** -0.5)).astype(BF16)
        vo_ref[0, 0, a * nm:(a + 1) * nm, :] = _dot(v[:, sl], wo_bf[sl, :]).astype(BF16)


def _mem_kv(mem, norm_mem, w_kv, w_q, w_o):
    b, nm, d = mem.shape
    depth = w_kv.shape[0]
    per_layer = lambda shape: pl.BlockSpec((1,) + shape, lambda l, i: (l, 0, 0))
    return pl.pallas_call(
        _memkv_kernel,
        out_shape=(jax.ShapeDtypeStruct((depth, b, d, XATTN_HEADS * nm), BF16),
                   jax.ShapeDtypeStruct((depth, b, XATTN_HEADS * nm, d), BF16)),
        grid=(depth, b),
        in_specs=[
            pl.BlockSpec((1, nm, d), lambda l, i: (i, 0, 0)),
            pl.BlockSpec((1, d), lambda l, i: (0, 0)),
            per_layer((d, 2 * d)), per_layer((d, d)), per_layer((d, d)),
        ],
        out_specs=(
            pl.BlockSpec((1, 1, d, XATTN_HEADS * nm), lambda l, i: (l, i, 0, 0)),
            pl.BlockSpec((1, 1, XATTN_HEADS * nm, d), lambda l, i: (l, i, 0, 0)),
        ),
        scratch_shapes=[pltpu.VMEM((d, 2 * d), BF16), pltpu.VMEM((d, d), BF16), pltpu.VMEM((d, d), BF16)],
        compiler_params=_cparams(("arbitrary", "arbitrary")),
        name="mem_kv",
    )(mem, norm_mem.reshape(1, d), w_kv, w_q, w_o)


def _conv_kernel(x_ref, g_ref, win_ref, bin_ref, wdw_ref, bdw_ref, lng_ref, lnb_ref, wout_ref, bout_ref,
                 o_ref, ext_ref, even_ref, odd_ref):
    ts, d = x_ref.shape[1], x_ref.shape[2]
    n_shifts = ext_ref.shape[0]

    @pl.when(pl.program_id(1) == 0)
    def _():
        ext_ref[...] = jnp.zeros(ext_ref.shape, F32)

    x = x_ref[0]
    h = _rms(x, g_ref[...]).astype(BF16)
    u = _dot(h, win_ref[...]) + bin_ref[...]
    glu = u[:, :d] * _sigmoid(u[:, d:])
    for c in range(n_shifts):
        ext_ref[c, CONV_CARRY - 2 * c:CONV_CARRY - 2 * c + ts, :] = glu

    first = CONV_CARRY - (CONV_KERNEL - 1)

    def taps(r0, n_rows, parity, dst_ref):
        for c0 in range(0, d, CONV_COLS):
            cols = slice(c0, c0 + CONV_COLS)
            acc = [jnp.zeros((SUBLANES, CONV_COLS), F32) for _ in range(n_rows // SUBLANES)]
            for k in range(CONV_KERNEL):
                if (first + k) % 2 != parity:
                    continue
                shift = (first + k - parity) % SUBLANES
                base = first + k - parity - shift
                wk = wdw_ref[k * SUBLANES:(k + 1) * SUBLANES, cols]
                for j in range(n_rows // SUBLANES):
                    rows = pl.ds(r0 + base + j * SUBLANES, SUBLANES)
                    acc[j] = acc[j] + ext_ref[shift // 2, rows, cols] * wk
            for j in range(n_rows // SUBLANES):
                dst_ref[pl.ds(r0 + j * SUBLANES, SUBLANES), cols] = acc[j]

    def chunk(i, carry):
        r0 = pl.multiple_of(i * CONV_ROWS, CONV_ROWS)
        taps(r0, CONV_ROWS, 0, even_ref)
        taps(r0, CONV_ROWS, 1, odd_ref)
        return carry

    lax.fori_loop(0, ts // CONV_ROWS, chunk, 0)
    taps(ts, SUBLANES, 1, odd_ref)
    for c in range(n_shifts):
        ext_ref[c, 0:CONV_CARRY, :] = ext_ref[c, ts:ts + CONV_CARRY, :]

    c = even_ref[...] + odd_ref[1:ts + 1, :] + bdw_ref[...]
    mu = jnp.mean(c, axis=-1, keepdims=True)
    cc = c - mu
    var = jnp.mean(cc * cc, axis=-1, keepdims=True)
    un = cc * lax.rsqrt(var + EPS) * lng_ref[...] + lnb_ref[...]
    act = _silu(un).astype(BF16)
    o_ref[0] = x + _dot(act, wout_ref[...]) + bout_ref[...]


def _conv_mixer(x, g, w_in, b_in, w_dw, b_dw, ln_g, ln_b, w_out, b_out):
    b, s, d = x.shape
    ts = min(TILE_CONV, s)
    assert s % ts == 0 and ts % CONV_ROWS == 0 and d % CONV_COLS == 0
    row = lambda v: v.reshape(1, -1)
    const = lambda shape: pl.BlockSpec(shape, lambda i, j: (0,) * len(shape))
    return pl.pallas_call(
        _conv_kernel,
        out_shape=jax.ShapeDtypeStruct(x.shape, F32),
        grid=(b, s // ts),
        in_specs=[
            pl.BlockSpec((1, ts, d), lambda i, j: (i, j, 0)),
            const((1, d)), const((d, 2 * d)), const((1, 2 * d)), const((CONV_KERNEL * SUBLANES, d)), const((1, d)),
            const((1, d)), const((1, d)), const((d, d)), const((1, d)),
        ],
        out_specs=pl.BlockSpec((1, ts, d), lambda i, j: (i, j, 0)),
        scratch_shapes=[pltpu.VMEM((SUBLANES // 2, CONV_CARRY + ts, d), F32), pltpu.VMEM((ts, d), F32),
                        pltpu.VMEM((ts + SUBLANES, d), F32)],
        compiler_params=_cparams(("arbitrary", "arbitrary")),
        name="conv_mixer",
    )(x, row(g), w_in.astype(BF16), row(b_in), jnp.repeat(w_dw, SUBLANES, axis=0), row(b_dw), row(ln_g), row(ln_b),
      w_out.astype(BF16), row(b_out))


def _gla_levels(ts):
    sizes = [GLA_LEAF]
    while sizes[-1] < ts:
        sizes.append(sizes[-1] * 2)
    return sizes


def _moe_residual(x, route, y0, y1):
    g = lax.bitcast_convert_type(route, F32)
    g = jnp.concatenate([g, jnp.zeros((LANES - g.shape[0], g.shape[1]), F32)], axis=0).T
    gate0 = g[:, 2 * TOP_K:2 * TOP_K + 1]
    gate1 = g[:, 2 * TOP_K + 1:2 * TOP_K + 2]
    return x + _unpack_bf16_pairs(y0) * gate0 + _unpack_bf16_pairs(y1) * gate1


def _gla_kernel(x_ref, gate_ref, y0_ref, y1_ref, g_ref, wq_ref, wk_ref, wv_ref, wa_ref, wr_ref, wa2_ref, ba_ref,
                ng_ref, wo_ref, o_ref, state_ref, x_s, q_s, k_s, v_s, og_s, la_s, *, tiles_per_seq):
    j = pl.program_id(0)
    dkh = wq_ref.shape[1] // GLA_HEADS

    @pl.when(j == 0)
    def _():
        for ref in (x_s, q_s, k_s, og_s, la_s):
            ref[...] = jnp.zeros(ref.shape, F32)
        v_s[...] = jnp.zeros(v_s.shape, BF16)

    @pl.when((j == 0) | ((j - 1) % tiles_per_seq == 0))
    def _():
        state_ref[...] = jnp.zeros(state_ref.shape, F32)

    def stage_a(slot):
        x = _moe_residual(x_ref[0], gate_ref[...], y0_ref[...], y1_ref[...])
        h = _rms(x, g_ref[...]).astype(BF16)
        x_s[slot] = x
        q_s[slot] = _dot(h, wq_ref[...]) * (dkh ** -0.5)
        k_s[slot] = _dot(h, wk_ref[...])
        v_s[slot] = _dot(h, wv_ref[...]).astype(BF16)
        r = _dot(h, wr_ref[...])
        og_s[slot] = _silu(r)
        a = _dot(h, wa_ref[...]).astype(BF16)
        z = _dot(jnp.concatenate([a, a], axis=1), wa2_ref[...]) + ba_ref[...]
        la_s[slot] = -(jnp.maximum(-z, 0.0) + jnp.log(1.0 + jnp.exp(-jnp.abs(z)))) * (1.0 / GLA_TAU)

    def stage_b(slot):
        _gla_recurrence(x_s[slot], q_s[slot], k_s[slot], v_s[slot], og_s[slot], la_s[slot],
                        ng_ref, wo_ref, o_ref, state_ref)

    @pl.when(j % 2 == 0)
    def _():
        stage_b(1)
        stage_a(0)

    @pl.when(j % 2 == 1)
    def _():
        stage_b(0)
        stage_a(1)


def _gla_recurrence(x, q, k, v, out_gate, log_a, ng_ref, wo_ref, o_ref, state_ref):
    ts, d = x.shape
    dk = q.shape[1]
    dkh = dk // GLA_HEADS
    dvh = d // GLA_HEADS

    row = lax.broadcasted_iota(I32, (ts, ts), 0)
    col = lax.broadcasted_iota(I32, (ts, ts), 1)
    tri = jnp.where(col <= row, 1.0, 0.0).astype(BF16)
    bcum = _dot(tri, log_a.astype(BF16))
    b_last = bcum[ts - 1:ts, :]

    q_in = (q * jnp.exp(bcum)).astype(BF16)
    k_out = (k * jnp.exp(b_last - bcum)).astype(BF16)

    sizes = _gla_levels(ts)
    scores = [None] * GLA_HEADS
    for lvl, size in enumerate(sizes):
        half = size // 2
        same_block = (row & -size) == (col & -size)
        if lvl == 0:
            pair = same_block & (col <= row)
            q_ok = k_ok = None
        else:
            pair = same_block & ((row & (size - 1)) >= half) & ((col & (size - 1)) < half)
            pos = lax.broadcasted_iota(I32, (ts, dk), 0) & (size - 1)
            q_ok = pos >= half
            k_ok = pos < half
        ref = jnp.concatenate(
            [jnp.broadcast_to(bcum[r0 + half:r0 + half + 1, :], (size, dk)) for r0 in range(0, ts, size)], axis=0)
        ql = q * jnp.exp(bcum - ref)
        kl = k * jnp.exp(ref - bcum)
        if q_ok is not None:
            ql = jnp.where(q_ok, ql, 0.0)
            kl = jnp.where(k_ok, kl, 0.0)
        ql = ql.astype(BF16)
        kl = kl.astype(BF16)
        for hd in range(GLA_HEADS):
            c0 = hd * dkh
            a = _dot_nt(ql[:, c0:c0 + dkh], kl[:, c0:c0 + dkh])
            a = jnp.where(pair, a, 0.0)
            scores[hd] = a if scores[hd] is None else scores[hd] + a

    outs = []
    for hd in range(GLA_HEADS):
        c0 = hd * dkh
        v_h = v[:, hd * dvh:(hd + 1) * dvh]
        st = state_ref[hd]
        o_h = _dot(scores[hd].astype(BF16), v_h) + _dot_nt(q_in[:, c0:c0 + dkh], st.astype(BF16))
        decay = jnp.exp(b_last[:, c0:c0 + dkh])
        state_ref[hd] = st * decay + _dot_tn(v_h, k_out[:, c0:c0 + dkh])
        o_h = o_h * lax.rsqrt(jnp.mean(o_h * o_h, axis=-1, keepdims=True) + EPS) * ng_ref[...]
        outs.append(o_h)
    o = jnp.concatenate(outs, axis=1) * out_gate
    o_ref[0] = x + _dot(o.astype(BF16), wo_ref[...])


def _gla_mixer(x, gates, y_buf, dest, g, w_in, w_a2, b_a, norm_g, w_o):
    b, s, d = x.shape
    dk = w_a2.shape[1]
    ts = min(TILE_GLA, s)
    assert s % ts == 0 and ts % GLA_LEAF == 0
    nj = s // ts
    row = lambda v: v.reshape(1, -1)
    const = lambda shape: pl.BlockSpec(shape, lambda j: (0,) * len(shape))
    wq = w_in[:, :dk].astype(BF16)
    wk = w_in[:, dk:2 * dk].astype(BF16)
    wv = w_in[:, 2 * dk:2 * dk + d].astype(BF16)
    wa = jnp.pad(w_in[:, 2 * dk + d:2 * dk + d + GLA_RANK], ((0, 0), (0, GLA_RANK_PAD - GLA_RANK))).astype(BF16)
    wr = w_in[:, 2 * dk + d + GLA_RANK:].astype(BF16)
    wa2 = jnp.pad(w_a2, ((0, GLA_RANK_PAD - GLA_RANK), (0, 0)))
    wa2_hi = wa2.astype(BF16)
    wa2_lo = (wa2 - wa2_hi.astype(F32)).astype(BF16)
    wa2_split = jnp.concatenate([wa2_hi, wa2_lo], axis=0)
    dvh = d // GLA_HEADS
    weights = (row(g), wq, wk, wv, wa, wr, wa2_split, row(b_a), row(norm_g), w_o.astype(BF16))

    def sequences(x, first, count, y0, y1):
        n = count * nj
        tile = lambda j, lag: jnp.clip(j - lag, 0, n - 1)
        return pl.pallas_call(
            functools.partial(_gla_kernel, tiles_per_seq=nj),
            out_shape=jax.ShapeDtypeStruct(x.shape, F32),
            grid=(n + 1,),
            in_specs=[
                pl.BlockSpec((1, ts, d), lambda j: (first + tile(j, 0) // nj, tile(j, 0) % nj, 0)),
                pl.BlockSpec((ROUTE_ROWS, ts), lambda j: (0, first * nj + tile(j, 0))),
                pl.BlockSpec((ts, d // 2), lambda j: (tile(j, 0), 0)),
                pl.BlockSpec((ts, d // 2), lambda j: (tile(j, 0), 0)),
                const((1, d)), const((d, dk)), const((d, dk)), const((d, d)), const((d, GLA_RANK_PAD)),
                const((d, d)), const((2 * GLA_RANK_PAD, dk)), const((1, dk)), const((1, dvh)), const((d, d)),
            ],
            out_specs=pl.BlockSpec((1, ts, d), lambda j: (first + tile(j, 1) // nj, tile(j, 1) % nj, 0)),
            scratch_shapes=[pltpu.VMEM((GLA_HEADS, dvh, dk // GLA_HEADS), F32),
                            pltpu.VMEM((2, ts, d), F32), pltpu.VMEM((2, ts, dk), F32), pltpu.VMEM((2, ts, dk), F32),
                            pltpu.VMEM((2, ts, d), BF16), pltpu.VMEM((2, ts, d), F32), pltpu.VMEM((2, ts, dk), F32)],
            input_output_aliases={0: 0},
            compiler_params=_cparams(("arbitrary",)),
            name="gla_mixer",
        )(x, gates, y0, y1, *weights)

    for first, count in ((0, 1), (1, b - 1)) if b > 1 else ((0, 1),):
        y0, y1 = _gather_pairs(y_buf, dest[:, first * s:(first + count) * s])
        x = sequences(x, first, count, y0, y1)
    return x


def _xattn_router_kernel(x_ref, gx_ref, qk_ref, vo_ref, gf_ref, wr_ref, br_ref, upper_ref,
                         x_out_ref, h_out_ref, route_ref, cnt_ref, carry_ref):
    ts, d = x_ref.shape[1], x_ref.shape[2]
    nm = qk_ref.shape[3] // XATTN_HEADS
    first = (pl.program_id(0) == 0) & (pl.program_id(1) == 0)

    @pl.when(first)
    def _():
        carry_ref[...] = jnp.zeros(carry_ref.shape, F32)

    x = x_ref[0]
    h = _rms(x, gx_ref[...]).astype(BF16)
    scores = _dot(h, qk_ref[0, 0])
    probs = []
    for a in range(XATTN_HEADS):
        s = scores[:, a * nm:(a + 1) * nm]
        p = jnp.exp(s - jnp.max(s, axis=-1, keepdims=True))
        probs.append((p / jnp.sum(p, axis=-1, keepdims=True)).astype(BF16))
    x2 = x + _dot(jnp.concatenate(probs, axis=1), vo_ref[0, 0])
    x_out_ref[0] = x2

    hf = _rms(x2, gf_ref[...])
    h_out_ref[...] = _pack_bf16_pairs(hf)

    both = _dot_nt(wr_ref[...], hf.astype(BF16))
    logits = both[0:ROUTER_ROWS, :] + both[ROUTER_ROWS:2 * ROUTER_ROWS, :] + br_ref[...]
    gl = logits[N_EXPERTS:N_EXPERTS + N_GROUPS, :]
    gi = lax.broadcasted_iota(I32, gl.shape, 0).astype(F32)
    gmax = jnp.max(gl, axis=0, keepdims=True)
    g_sel = jnp.min(jnp.where(gl == gmax, gi, float(N_GROUPS)), axis=0, keepdims=True)
    pg_sel = 1.0 / jnp.sum(jnp.exp(gl - gmax), axis=0, keepdims=True)

    el = jnp.zeros((EXPERTS_PER_GROUP, ts), F32)
    for gidx in range(N_GROUPS):
        lo = gidx * EXPERTS_PER_GROUP
        el = jnp.where(g_sel == float(gidx), logits[lo:lo + EXPERTS_PER_GROUP, :], el)
    ei = lax.broadcasted_iota(I32, el.shape, 0).astype(F32)
    m1 = jnp.max(el, axis=0, keepdims=True)
    i1 = jnp.min(jnp.where(el == m1, ei, float(EXPERTS_PER_GROUP)), axis=0, keepdims=True)
    rest = jnp.where(ei == i1, -jnp.inf, el)
    m2 = jnp.max(rest, axis=0, keepdims=True)
    i2 = jnp.min(jnp.where(rest == m2, ei, float(EXPERTS_PER_GROUP)), axis=0, keepdims=True)
    ratio = jnp.exp(m2 - m1)
    gate1 = pg_sel / (1.0 + ratio)
    gate2 = pg_sel * ratio / (1.0 + ratio)
    e1 = g_sel * float(EXPERTS_PER_GROUP) + i1
    e2 = g_sel * float(EXPERTS_PER_GROUP) + i2

    xi = lax.broadcasted_iota(I32, (N_EXPERTS, ts), 0).astype(F32)
    oh1 = jnp.where(xi == e1, 1.0, 0.0)
    oh2 = jnp.where(xi == e2, 1.0, 0.0)
    oh = oh1 + oh2
    n_blk = ts // LANES
    stacked = jnp.concatenate([oh[:, c * LANES:(c + 1) * LANES] for c in range(n_blk)], axis=0)
    within = _dot(stacked.astype(BF16), upper_ref[...])
    totals = jnp.sum(stacked, axis=1, keepdims=True)
    run = carry_ref[...]
    before = []
    for c in range(n_blk):
        before.append(within[c * N_EXPERTS:(c + 1) * N_EXPERTS, :] + run)
        run = run + totals[c * N_EXPERTS:(c + 1) * N_EXPERTS, :]
    before = jnp.concatenate(before, axis=1)
    rank1 = jnp.sum(oh1 * before, axis=0, keepdims=True)
    rank2 = jnp.sum(oh2 * before, axis=0, keepdims=True)
    carry_ref[...] = run
    cnt_ref[...] = jnp.broadcast_to(run, cnt_ref.shape)

    route_ref[0:1, :] = e1.astype(I32)
    route_ref[1:2, :] = e2.astype(I32)
    route_ref[2:3, :] = rank1.astype(I32)
    route_ref[3:4, :] = rank2.astype(I32)
    route_ref[4:5, :] = lax.bitcast_convert_type(gate1, I32)
    route_ref[5:6, :] = lax.bitcast_convert_type(gate2, I32)
    route_ref[3 * TOP_K:ROUTE_ROWS, :] = jnp.zeros((ROUTE_ROWS - 3 * TOP_K, ts), I32)


def _xattn_router(x, g_x, qk_mem, vo_mem, layer, g_f, w_grp, b_grp, w_exp, b_exp):
    b, s, d = x.shape
    hm = qk_mem.shape[3]
    t = b * s
    ts = min(TILE_XATTN, s)
    assert s % ts == 0 and ts % LANES == 0
    nj = s // ts
    row = lambda v: v.reshape(1, -1)
    const = lambda shape: pl.BlockSpec(shape, lambda i, j: (0,) * len(shape))
    pad = ROUTER_ROWS - N_GROUPS - N_EXPERTS
    w_r = jnp.pad(jnp.concatenate([w_exp, w_grp], axis=1).T, ((0, pad), (0, 0)))
    w_r_hi = w_r.astype(BF16)
    w_r_split = jnp.concatenate([w_r_hi, (w_r - w_r_hi.astype(F32)).astype(BF16)], axis=0)
    b_r = jnp.pad(jnp.concatenate([b_exp, b_grp]), (0, pad)).reshape(ROUTER_ROWS, 1)
    ti = jnp.arange(LANES)
    upper = (ti[:, None] < ti[None, :]).astype(BF16)
    return pl.pallas_call(
        _xattn_router_kernel,
        out_shape=(
            jax.ShapeDtypeStruct(x.shape, F32),
            jax.ShapeDtypeStruct((t, d // 2), U32),
            jax.ShapeDtypeStruct((ROUTE_ROWS, t), I32),
            jax.ShapeDtypeStruct((N_EXPERTS, LANES), F32),
        ),
        grid=(b, nj),
        in_specs=[
            pl.BlockSpec((1, ts, d), lambda i, j: (i, j, 0)),
            const((1, d)),
            pl.BlockSpec((1, 1, d, hm), lambda i, j: (layer, i, 0, 0)),
            pl.BlockSpec((1, 1, hm, d), lambda i, j: (layer, i, 0, 0)),
            const((1, d)), const((2 * ROUTER_ROWS, d)), const((ROUTER_ROWS, 1)), const((LANES, LANES)),
        ],
        out_specs=(
            pl.BlockSpec((1, ts, d), lambda i, j: (i, j, 0)),
            pl.BlockSpec((ts, d // 2), lambda i, j: (i * nj + j, 0)),
            pl.BlockSpec((ROUTE_ROWS, ts), lambda i, j: (0, i * nj + j)),
            pl.BlockSpec((N_EXPERTS, LANES), lambda i, j: (0, 0)),
        ),
        scratch_shapes=[pltpu.VMEM((N_EXPERTS, 1), F32)],
        compiler_params=_cparams(("arbitrary", "arbitrary")),
        name="xattn_router",
    )(x, row(g_x), qk_mem, vo_mem, row(g_f), w_r_split, b_r, upper)


def _sc_mesh():
    return plsc.VectorSubcoreMesh(core_axis_name="c", subcore_axis_name="s",
                                  num_cores=SC_CORES, num_subcores=SC_SUBCORES)


def _sc_worker():
    return lax.axis_index("s") * SC_CORES + lax.axis_index("c")


def _dispatch(h_packed, dest, n_rows):
    t, w = h_packed.shape
    chunk = SC_DISPATCH_CHUNK
    per_worker = t // SC_WORKERS
    n_chunks = per_worker // chunk
    assert n_chunks % 2 == 0 and n_chunks * chunk * SC_WORKERS == t
    dest = dest.reshape(TOP_K, SC_WORKERS * n_chunks, chunk)
    rows_buf = pltpu.VMEM((chunk, w), U32)

    @functools.partial(
        pl.kernel, mesh=_sc_mesh(),
        out_type=jax.ShapeDtypeStruct((n_rows, w), U32),
        scratch_types=[pltpu.VMEM((n_chunks, chunk), I32), pltpu.VMEM((n_chunks, chunk), I32), rows_buf, rows_buf,
                       pltpu.SemaphoreType.DMA((2,)), pltpu.SemaphoreType.DMA((2, TOP_K))],
        name="moe_dispatch_sc",
    )
    def run(h_hbm, d0_hbm, d1_hbm, xbuf_hbm, idx0_v, idx1_v, buf_a, buf_b, read_sem, write_sem):
        wid = _sc_worker()
        pltpu.sync_copy(d0_hbm.at[pl.ds(wid * n_chunks, n_chunks)], idx0_v)
        pltpu.sync_copy(d1_hbm.at[pl.ds(wid * n_chunks, n_chunks)], idx1_v)

        @pl.loop(0, n_chunks, step=2)
        def _(i):
            reads = [pltpu.async_copy(h_hbm.at[pl.ds(wid * per_worker + (i + j) * chunk, chunk)], buf, read_sem.at[j])
                     for j, buf in enumerate((buf_a, buf_b))]
            writes = []
            for j, buf in enumerate((buf_a, buf_b)):
                reads[j].wait()
                writes.append(pltpu.async_copy(buf, xbuf_hbm.at[idx0_v.at[i + j]], write_sem.at[j, 0]))
                writes.append(pltpu.async_copy(buf, xbuf_hbm.at[idx1_v.at[i + j]], write_sem.at[j, 1]))
            for copy in writes:
                copy.wait()

    return run(h_packed, dest[0], dest[1])


def _gather_pairs(y_buf, dest):
    t = dest.shape[1]
    w = y_buf.shape[1]
    chunk = SC_GATHER_CHUNK
    per_worker = t // SC_WORKERS
    n_chunks = per_worker // chunk
    assert n_chunks % 2 == 0 and n_chunks * chunk * SC_WORKERS == t
    dest = dest.reshape(TOP_K, SC_WORKERS * n_chunks, chunk)
    out = jax.ShapeDtypeStruct((t, w), U32)
    rows_buf = pltpu.VMEM((chunk, w), U32)

    @functools.partial(
        pl.kernel, mesh=_sc_mesh(),
        out_type=(out, out),
        scratch_types=[pltpu.VMEM((n_chunks, chunk), I32), pltpu.VMEM((n_chunks, chunk), I32),
                       rows_buf, rows_buf, rows_buf, rows_buf, pltpu.SemaphoreType.DMA((2, TOP_K))],
        name="moe_gather_sc",
    )
    def run(y_hbm, d0_hbm, d1_hbm, y0_hbm, y1_hbm, idx0_v, idx1_v, buf_a0, buf_a1, buf_b0, buf_b1, sem):
        wid = _sc_worker()
        pltpu.sync_copy(d0_hbm.at[pl.ds(wid * n_chunks, n_chunks)], idx0_v)
        pltpu.sync_copy(d1_hbm.at[pl.ds(wid * n_chunks, n_chunks)], idx1_v)

        @pl.loop(0, n_chunks, step=2)
        def _(i):
            bufs = ((buf_a0, buf_a1), (buf_b0, buf_b1))
            gathers = [[pltpu.async_copy(y_hbm.at[idx_v.at[i + j]], bufs[j][k], sem.at[j, k])
                        for k, idx_v in enumerate((idx0_v, idx1_v))] for j in range(2)]
            stores = []
            for j in range(2):
                rows = pl.ds(wid * per_worker + (i + j) * chunk, chunk)
                for k, out_hbm in enumerate((y0_hbm, y1_hbm)):
                    gathers[j][k].wait()
                    stores.append(pltpu.async_copy(bufs[j][k], out_hbm.at[rows], sem.at[j, k]))
            for copy in stores:
                copy.wait()

    return run(y_buf, dest[0], dest[1])


def _expert_kernel(be_ref, bf_ref, bv_ref, slot_ref, next_ref, x_ref, wg_hbm, wu_hbm, wd_hbm, y_ref,
                   wg_f32, wu_f32, wd_f32, wg_bf, wu_bf, wd_bf, sems, *, layer):
    blk = pl.program_id(0)
    valid = bv_ref[blk]

    def weight_copies(expert, slot):
        return [pltpu.make_async_copy(src.at[layer, expert], dst.at[slot], sems.at[slot, i])
                for i, (src, dst) in enumerate(((wg_hbm, wg_f32), (wu_hbm, wu_f32), (wd_hbm, wd_f32)))]

    @pl.when(blk == 0)
    def _():
        for copy in weight_copies(be_ref[0], 0):
            copy.start()

    @pl.when(bf_ref[blk] == 1)
    def _():
        slot = slot_ref[blk]
        for copy in weight_copies(be_ref[blk], slot):
            copy.wait()

        @pl.when(next_ref[blk] >= 0)
        def _():
            for copy in weight_copies(next_ref[blk], 1 - slot):
                copy.start()

        wg_bf[...] = wg_f32[slot].astype(BF16)
        wu_bf[...] = wu_f32[slot].astype(BF16)
        wd_bf[...] = wd_f32[slot].astype(BF16)

    @pl.when(valid > 0)
    def _():
        live = lax.broadcasted_iota(I32, x_ref.shape, 0) < valid
        xb = _unpack_bf16_pairs(jnp.where(live, x_ref[...], jnp.uint32(0))).astype(BF16)
        gt = _dot(xb, wg_bf[...])
        up = _dot(xb, wu_bf[...])
        act = (_silu(gt) * up).astype(BF16)
        y_ref[...] = _pack_bf16_pairs(_dot(act, wd_bf[...]))

    @pl.when(valid <= 0)
    def _():
        y_ref[...] = jnp.zeros(y_ref.shape, U32)


def _experts(x_buf, block_expert, block_first, block_valid, w_gate, w_up, w_down, layer):
    n_rows, w = x_buf.shape
    d, de = w_gate.shape[2], w_gate.shape[3]
    bm = MOE_BLOCK_ROWS
    n = n_rows // bm
    block_slot = (jnp.cumsum(block_first) - 1) % 2
    idx = jnp.arange(n, dtype=I32)
    later_first = jnp.concatenate([jnp.where(block_first[1:] == 1, idx[1:], n), jnp.full((1,), n, I32)])
    next_first = lax.cummin(later_first, reverse=True)
    block_next = jnp.where(next_first < n, block_expert[jnp.minimum(next_first, n - 1)], -1).astype(I32)
    any_space = pl.BlockSpec(memory_space=pl.ANY)
    grid_spec = pltpu.PrefetchScalarGridSpec(
        num_scalar_prefetch=5,
        grid=(n,),
        in_specs=[pl.BlockSpec((bm, w), lambda i, *_: (i, 0)), any_space, any_space, any_space],
        out_specs=pl.BlockSpec((bm, w), lambda i, *_: (i, 0)),
        scratch_shapes=[pltpu.VMEM((2, d, de), F32), pltpu.VMEM((2, d, de), F32), pltpu.VMEM((2, de, d), F32),
                        pltpu.VMEM((d, de), BF16), pltpu.VMEM((d, de), BF16), pltpu.VMEM((de, d), BF16),
                        pltpu.SemaphoreType.DMA((2, 3))],
    )
    return pl.pallas_call(
        functools.partial(_expert_kernel, layer=layer),
        out_shape=jax.ShapeDtypeStruct((n_rows, w), U32),
        grid_spec=grid_spec,
        compiler_params=_cparams(("arbitrary",)),
        name="moe_experts",
    )(block_expert, block_first, block_valid, block_slot.astype(I32), block_next, x_buf, w_gate, w_up, w_down)


def _combine_kernel(x_ref, gate_ref, y0_ref, y1_ref, gfin_ref, o_ref, *, final_norm):
    out = _moe_residual(x_ref[...], gate_ref[...], y0_ref[...], y1_ref[...])
    if final_norm:
        out = _rms(out, gfin_ref[...])
    o_ref[...] = out


def _combine(x2, y0, y1, gates, g_final, final_norm):
    t, d = x2.shape
    w = y0.shape[1]
    ts = min(TILE_COMBINE, t)
    assert t % ts == 0
    return pl.pallas_call(
        functools.partial(_combine_kernel, final_norm=final_norm),
        out_shape=jax.ShapeDtypeStruct((t, d), F32),
        grid=(t // ts,),
        in_specs=[
            pl.BlockSpec((ts, d), lambda i: (i, 0)),
            pl.BlockSpec((ROUTE_ROWS, ts), lambda i: (0, i)),
            pl.BlockSpec((ts, w), lambda i: (i, 0)),
            pl.BlockSpec((ts, w), lambda i: (i, 0)),
            pl.BlockSpec((1, d), lambda i: (0, 0)),
        ],
        out_specs=pl.BlockSpec((ts, d), lambda i: (i, 0)),
        compiler_params=_cparams(("arbitrary",)),
        name="moe_combine",
    )(x2, gates, y0, y1, g_final.reshape(1, d))


def _moe_layout(route, counts):
    bm = MOE_BLOCK_ROWS
    t = route.shape[1]
    assert (t * TOP_K) % bm == 0
    n_blocks = (t * TOP_K) // bm + N_EXPERTS
    cnt = counts[:, 0].astype(I32)
    padded = (cnt + bm - 1) // bm * bm
    pad_ends = jnp.cumsum(padded)
    pad_off = pad_ends - padded
    experts = jnp.arange(N_EXPERTS, dtype=I32)
    hit = route[0:TOP_K, :, None] == experts
    dest = jnp.sum(jnp.where(hit, pad_off, 0), axis=-1) + route[TOP_K:2 * TOP_K]
    gates = route
    starts = jnp.arange(n_blocks, dtype=I32) * bm
    block_expert = jnp.minimum(jnp.sum((pad_ends[None, :] <= starts[:, None]).astype(I32), axis=1),
                               N_EXPERTS - 1)
    block_first = jnp.concatenate([jnp.ones((1,), I32), (block_expert[1:] != block_expert[:-1]).astype(I32)])
    own = block_expert[:, None] == experts
    block_valid = jnp.clip(jnp.sum(jnp.where(own, cnt + pad_off, 0), axis=1) - starts, 0, bm)
    block_valid = jnp.where(starts < pad_ends[-1], block_valid, 0).astype(I32)
    return dest, gates, block_expert, block_first, block_valid, n_blocks * bm


def kernel(x, mem, norm_mix, norm_xattn, norm_ffn, norm_mem, norm_final, conv_w_in, conv_b_in, conv_w_dw,
           conv_b_dw, conv_ln_g, conv_ln_b, conv_w_out, conv_b_out, gla_w_in, gla_w_a2, gla_b_a, gla_norm_g,
           gla_w_o, xa_w_q, xa_w_kv, xa_w_o, moe_w_grp, moe_b_grp, moe_w_exp, moe_b_exp, moe_w_gate, moe_w_up,
           moe_w_down):
    b, s, d = x.shape
    depth = norm_mix.shape[0]
    qk_mem, vo_mem = _mem_kv(mem, norm_mem, xa_w_kv, xa_w_q, xa_w_o)
    moe = None
    for i in range(depth):
        j = i // 2
        if i % 2 == 0:
            if moe is not None:
                y0, y1 = _gather_pairs(moe[1], moe[2])
                x = _combine(x.reshape(b * s, d), y0, y1, moe[0], norm_final, False).reshape(b, s, d)
            x = _conv_mixer(x, norm_mix[i], conv_w_in[j], conv_b_in[j], conv_w_dw[j], conv_b_dw[j],
                            conv_ln_g[j], conv_ln_b[j], conv_w_out[j], conv_b_out[j])
        else:
            x = _gla_mixer(x, *moe, norm_mix[i], gla_w_in[j], gla_w_a2[j], gla_b_a[j], gla_norm_g[j], gla_w_o[j])
        x2, h_packed, route, counts = _xattn_router(
            x, norm_xattn[i], qk_mem, vo_mem, i, norm_ffn[i],
            moe_w_grp[i], moe_b_grp[i], moe_w_exp[i], moe_b_exp[i])
        dest, gates, block_expert, block_first, block_valid, n_rows = _moe_layout(route, counts)
        x_buf = _dispatch(h_packed, dest, n_rows)
        y_buf = _experts(x_buf, block_expert, block_first, block_valid, moe_w_gate, moe_w_up, moe_w_down, i)
        x, moe = x2, (gates, y_buf, dest)
    y0, y1 = _gather_pairs(moe[1], moe[2])
    return _combine(x.reshape(b * s, d), y0, y1, moe[0], norm_final, True).reshape(b, s, d)
```
